```python
import jax, jax.numpy as jnp
from jax import lax
import numpy as np

D_MODEL = 4096
BATCH = 8
SEQ = 4096
DEPTH = 4

GRID_W = 64
HEAD_DIM = 128
ATTN_WIDTH = D_MODEL // 2
ATTN_HEADS = ATTN_WIDTH // HEAD_DIM
ATTN_KV_HEADS = ATTN_HEADS // 4
ATTN_GROUP = ATTN_HEADS // ATTN_KV_HEADS
KV_WIDTH = ATTN_KV_HEADS * HEAD_DIM
Q_BLOCK = 128
RET_WIDTH = D_MODEL - ATTN_WIDTH
RET_V_DIM = 256
RET_HEADS = RET_WIDTH // RET_V_DIM
RET_QK_DIM = 128
RET_QK_WIDTH = RET_HEADS * RET_QK_DIM
RET_CHUNK = 128
RET_DECAY_BASE_EXP = 5
MIX_WIDTH = ATTN_WIDTH + RET_WIDTH
ROPE_THETA = 10000.0
EPS = 1e-6

IN_SPLITS = (ATTN_WIDTH, KV_WIDTH, KV_WIDTH, ATTN_WIDTH, RET_QK_WIDTH, RET_QK_WIDTH, RET_WIDTH, RET_WIDTH)
IN_WIDTH = sum(IN_SPLITS)
SPLIT_POINTS = tuple(int(p) for p in np.cumsum(IN_SPLITS)[:-1])

kernel_name = 'hymba_gqa_axialrope_biretention_encoder'


def rmsnorm(x, g):
    xf = x.astype(jnp.float32)
    y = xf * lax.rsqrt(jnp.mean(xf * xf, axis=-1, keepdims=True) + EPS)
    return (y * g.astype(jnp.float32)).astype(x.dtype)


def axial_rope_tables(seq_len):
    rows = seq_len // GRID_W
    row = jnp.repeat(jnp.arange(rows), GRID_W).astype(jnp.float32)
    col = jnp.tile(jnp.arange(GRID_W), rows).astype(jnp.float32)
    axis_dim = HEAD_DIM // 2
    inv = ROPE_THETA ** (-jnp.arange(0, axis_dim, 2, dtype=jnp.float32) / axis_dim)
    ang_r = row[:, None] * inv[None, :]
    ang_c = col[:, None] * inv[None, :]
    return (jnp.cos(ang_r), jnp.sin(ang_r), jnp.cos(ang_c), jnp.sin(ang_c))


def _rotate(xp, cos, sin):
    x1, x2 = jnp.split(xp, 2, axis=-1)
    c = cos[None, :, None, :]
    s = sin[None, :, None, :]
    return jnp.concatenate([x1 * c - x2 * s, x1 * s + x2 * c], axis=-1)


def apply_axial_rope(x, rope):
    cos_r, sin_r, cos_c, sin_c = rope
    xf = x.astype(jnp.float32)
    half = x.shape[-1] // 2
    out = jnp.concatenate([_rotate(xf[..., :half], cos_r, sin_r),
                           _rotate(xf[..., half:], cos_c, sin_c)], axis=-1)
    return out.astype(x.dtype)


def gqa_attention(q, k, v):
    B, S = q.shape[0], q.shape[1]
    nb = S // Q_BLOCK
    qb = q.reshape(B, nb, Q_BLOCK, ATTN_KV_HEADS, ATTN_GROUP, HEAD_DIM).transpose(1, 0, 3, 4, 2, 5)
    kt = k.transpose(0, 2, 1, 3)
    vt = v.transpose(0, 2, 1, 3)

    def block(qblk):
        s = jnp.einsum('bkgqd,bksd->bkgqs', qblk, kt).astype(jnp.float32)
        p = jax.nn.softmax(s, axis=-1).astype(vt.dtype)
        return jnp.einsum('bkgqs,bksd->bkgqd', p, vt)

    o = lax.map(block, qb)
    return o.transpose(1, 0, 4, 2, 3, 5).reshape(B, S, ATTN_WIDTH)


def retention_chunkwise(q, k, v, log_g, include_diag):
    q = q.astype(jnp.float32)
    k = k.astype(jnp.float32)
    v = v.astype(jnp.float32)
    log_g = log_g.astype(jnp.float32)
    B, H, S, dk = q.shape
    dv = v.shape[-1]
    C = RET_CHUNK
    n = S // C
    qc = q.reshape(B, H, n, C, dk).transpose(2, 0, 1, 3, 4)
    kc = k.reshape(B, H, n, C, dk).transpose(2, 0, 1, 3, 4)
    vc = v.reshape(B, H, n, C, dv).transpose(2, 0, 1, 3, 4)
    idx = jnp.arange(C, dtype=jnp.float32)
    diff = idx[:, None] - idx[None, :]
    mask = (diff >= 0) if include_diag else (diff > 0)
    safe = jnp.where(mask, diff, 0.0)
    decay_in = jnp.where(mask[None], jnp.exp(log_g[:, None, None] * safe[None]), 0.0)
    xi = jnp.exp(log_g[:, None] * (idx + 1.0)[None])[..., None]
    zeta = jnp.exp(log_g[:, None] * (C - 1.0 - idx)[None])[..., None]
    g_chunk = jnp.exp(log_g * C)[:, None, None]

    def step(state, inp):
        qj, kj, vj = inp
        scores = jnp.einsum('bhqd,bhkd->bhqk', qj, kj) * decay_in
        inner = jnp.einsum('bhqk,bhke->bhqe', scores, vj)
        cross = jnp.einsum('bhqd,bhde->bhqe', qj * xi, state)
        state = g_chunk * state + jnp.einsum('bhkd,bhke->bhde', kj * zeta, vj)
        return state, inner + cross

    state0 = jnp.zeros((B, H, dk, dv), jnp.float32)
    _, out = lax.scan(step, state0, (qc, kc, vc))
    return out.transpose(1, 2, 0, 3, 4).reshape(B, H, S, dv)


def bidirectional_retention(q, k, v, log_gf, log_gb):
    fwd = retention_chunkwise(q, k, v, log_gf, True)
    bwd = retention_chunkwise(jnp.flip(q, 2), jnp.flip(k, 2), jnp.flip(v, 2), log_gb, False)
    return fwd + jnp.flip(bwd, 2)


def _fwd_setup_inputs(seed: int = 0) -> dict:
    key = jax.random.key(seed)
    ks = jax.random.split(key, 12)
    x = jax.random.normal(ks[0], (BATCH, SEQ, D_MODEL), jnp.float32)
    norm_w = 1.0 + 0.02 * jax.random.normal(ks[1], (DEPTH, D_MODEL), jnp.float32)
    w_in = jax.random.normal(ks[2], (DEPTH, D_MODEL, IN_WIDTH), jnp.float32) * (D_MODEL ** -0.5)
    q_norm = 1.0 + 0.02 * jax.random.normal(ks[3], (DEPTH, HEAD_DIM), jnp.float32)
    k_norm = 1.0 + 0.02 * jax.random.normal(ks[4], (DEPTH, HEAD_DIM), jnp.float32)
    base = jnp.asarray(np.log(2.0 ** (RET_DECAY_BASE_EXP + np.arange(RET_HEADS)) - 1.0), jnp.float32)
    ret_decay_fwd = base[None] + 0.1 * jax.random.normal(ks[5], (DEPTH, RET_HEADS), jnp.float32)
    ret_decay_bwd = base[None] + 0.1 * jax.random.normal(ks[6], (DEPTH, RET_HEADS), jnp.float32)
    ret_norm = 1.0 + 0.02 * jax.random.normal(ks[7], (DEPTH, RET_HEADS, RET_V_DIM), jnp.float32)
    w_out = jax.random.normal(ks[8], (DEPTH, MIX_WIDTH, D_MODEL), jnp.float32) * (MIX_WIDTH ** -0.5)
    final_norm = 1.0 + 0.02 * jax.random.normal(ks[9], (D_MODEL,), jnp.float32)
    return {'x': x, 'norm_w': norm_w, 'w_in': w_in, 'q_norm': q_norm, 'k_norm': k_norm,
            'ret_decay_fwd': ret_decay_fwd, 'ret_decay_bwd': ret_decay_bwd, 'ret_norm': ret_norm,
            'w_out': w_out, 'final_norm': final_norm}


def _fwd_reference(x, norm_w, w_in, q_norm, k_norm, ret_decay_fwd, ret_decay_bwd, ret_norm, w_out, final_norm):
    B, S, _ = x.shape
    rope = axial_rope_tables(S)
    attn_scale = HEAD_DIM ** -0.5
    ret_scale = RET_QK_DIM ** -0.5
    for l in range(DEPTH):
        h = rmsnorm(x, norm_w[l])
        proj = h @ w_in[l]
        aq, ak, av, ag, rq, rk, rv, rg = jnp.split(proj, SPLIT_POINTS, axis=-1)

        aq = apply_axial_rope(rmsnorm(aq.reshape(B, S, ATTN_HEADS, HEAD_DIM), q_norm[l]), rope) * attn_scale
        ak = apply_axial_rope(rmsnorm(ak.reshape(B, S, ATTN_KV_HEADS, HEAD_DIM), k_norm[l]), rope)
        av = av.reshape(B, S, ATTN_KV_HEADS, HEAD_DIM)
        a_out = (jax.nn.silu(ag) * gqa_attention(aq, ak, av)).astype(x.dtype)

        rq = apply_axial_rope(rq.reshape(B, S, RET_HEADS, RET_QK_DIM), rope).transpose(0, 2, 1, 3)
        rk = (apply_axial_rope(rk.reshape(B, S, RET_HEADS, RET_QK_DIM), rope) * ret_scale).transpose(0, 2, 1, 3)
        rv = rv.reshape(B, S, RET_HEADS, RET_V_DIM).transpose(0, 2, 1, 3)
        log_gf = jax.nn.log_sigmoid(ret_decay_fwd[l].astype(jnp.float32))
        log_gb = jax.nn.log_sigmoid(ret_decay_bwd[l].astype(jnp.float32))
        r = bidirectional_retention(rq, rk, rv, log_gf, log_gb).transpose(0, 2, 1, 3)
        r = rmsnorm(r, ret_norm[l]).reshape(B, S, RET_WIDTH)
        r_out = (jax.nn.silu(rg.astype(jnp.float32)) * r).astype(x.dtype)

        x = x + jnp.concatenate([a_out, r_out], axis=-1) @ w_out[l]
    return rmsnorm(x, final_norm)


import jax as _jax
import jax.numpy as _jnp

TWIN_FORMAT = 'train_step'
FWD_PARAMS = ['x', 'norm_w', 'w_in', 'q_norm', 'k_norm', 'ret_decay_fwd', 'ret_decay_bwd', 'ret_norm', 'w_out', 'final_norm']
TWIN_WEIGHTS = ['norm_w', 'w_in', 'q_norm', 'k_norm', 'ret_decay_fwd', 'ret_decay_bwd', 'ret_norm', 'w_out', 'final_norm']
TWIN_DIFF_INPUT = 'x'
TWIN_INPUTS = ['x', 'norm_w', 'w_in', 'q_norm', 'k_norm', 'ret_decay_fwd', 'ret_decay_bwd', 'ret_norm', 'w_out', 'final_norm', 'loss_target', 'm_norm_w', 'm_w_in', 'm_q_norm', 'm_k_norm', 'm_ret_decay_fwd', 'm_ret_decay_bwd', 'm_ret_norm', 'm_w_out', 'm_final_norm', 'v_norm_w', 'v_w_in', 'v_q_norm', 'v_k_norm', 'v_ret_decay_fwd', 'v_ret_decay_bwd', 'v_ret_norm', 'v_w_out', 'v_final_norm']
TWIN_OUTPUTS = ['loss', 'grad_x', 'grad_norm_w', 'grad_w_in', 'grad_q_norm', 'grad_k_norm', 'grad_ret_decay_fwd', 'grad_ret_decay_bwd', 'grad_ret_norm', 'grad_w_out', 'grad_final_norm', 'delta_norm_w', 'delta_w_in', 'delta_q_norm', 'delta_k_norm', 'delta_ret_decay_fwd', 'delta_ret_decay_bwd', 'delta_ret_norm', 'delta_w_out', 'delta_final_norm', 'new_m_norm_w', 'new_m_w_in', 'new_m_q_norm', 'new_m_k_norm', 'new_m_ret_decay_fwd', 'new_m_ret_decay_bwd', 'new_m_ret_norm', 'new_m_w_out', 'new_m_final_norm', 'new_v_norm_w', 'new_v_w_in', 'new_v_q_norm', 'new_v_k_norm', 'new_v_ret_decay_fwd', 'new_v_ret_decay_bwd', 'new_v_ret_norm', 'new_v_w_out', 'new_v_final_norm']
TWIN_LEAF_KINDS = {'loss': 'loss', 'grad_x': 'grad_x', 'grad_norm_w': 'grad_w', 'grad_w_in': 'grad_w', 'grad_q_norm': 'grad_w', 'grad_k_norm': 'grad_w', 'grad_ret_decay_fwd': 'grad_w', 'grad_ret_decay_bwd': 'grad_w', 'grad_ret_norm': 'grad_w', 'grad_w_out': 'grad_w', 'grad_final_norm': 'grad_w', 'delta_norm_w': 'delta_w', 'delta_w_in': 'delta_w', 'delta_q_norm': 'delta_w', 'delta_k_norm': 'delta_w', 'delta_ret_decay_fwd': 'delta_w', 'delta_ret_decay_bwd': 'delta_w', 'delta_ret_norm': 'delta_w', 'delta_w_out': 'delta_w', 'delta_final_norm': 'delta_w', 'new_m_norm_w': 'new_m', 'new_m_w_in': 'new_m', 'new_m_q_norm': 'new_m', 'new_m_k_norm': 'new_m', 'new_m_ret_decay_fwd': 'new_m', 'new_m_ret_decay_bwd': 'new_m', 'new_m_ret_norm': 'new_m', 'new_m_w_out': 'new_m', 'new_m_final_norm': 'new_m', 'new_v_norm_w': 'new_v', 'new_v_w_in': 'new_v', 'new_v_q_norm': 'new_v', 'new_v_k_norm': 'new_v', 'new_v_ret_decay_fwd': 'new_v', 'new_v_ret_decay_bwd': 'new_v', 'new_v_ret_norm': 'new_v', 'new_v_w_out': 'new_v', 'new_v_final_norm': 'new_v'}


def _forward(args):
    return _fwd_reference(*[args[k] for k in FWD_PARAMS])


def _output_shape():
    out = _jax.eval_shape(lambda: _forward(_fwd_setup_inputs(0)))
    return out.shape, out.dtype

N_MICROBATCH = 1
ADAM_LR = 0.001
ADAM_B1 = 0.9
ADAM_B2 = 0.999
ADAM_EPS = 1e-08
ADAM_WD = 0.01
ADAM_STEP = 10
PER_EXAMPLE_BATCH_AXIS = {'x': 0, 'loss_target': 0}
SHARED_INPUTS = []
_WEIGHT_DTYPES = {'norm_w': _jnp.float32, 'w_in': _jnp.float32, 'q_norm': _jnp.float32, 'k_norm': _jnp.float32, 'ret_decay_fwd': _jnp.float32, 'ret_decay_bwd': _jnp.float32, 'ret_norm': _jnp.float32, 'w_out': _jnp.float32, 'final_norm': _jnp.float32}
MOMENT_SCALE = {'norm_w': 4.217971e-02, 'w_in': 2.523566e-02, 'q_norm': 1.164668e-02, 'k_norm': 1.131608e-02, 'ret_decay_fwd': 1.313578e-01, 'ret_decay_bwd': 3.415333e-01, 'ret_norm': 2.876857e-02, 'w_out': 2.069325e-02, 'final_norm': 7.988901e+00}


def _to_microbatches(a, axis):
    t = _jnp.moveaxis(a, axis, 0)
    t = t.reshape((N_MICROBATCH, t.shape[0] // N_MICROBATCH) + t.shape[1:])
    return _jnp.moveaxis(t, 1, axis + 1)


def setup_inputs(seed: int = 0) -> dict:
    inp = _fwd_setup_inputs(seed)
    key = _jax.random.fold_in(_jax.random.key(seed), 7919)
    shape, _ = _output_shape()
    out = dict(inp)
    out["loss_target"] = _jax.random.normal(_jax.random.fold_in(key, 0), shape, _jnp.float32)
    for i, name in enumerate(TWIN_WEIGHTS):
        w = inp[name].astype(_jnp.float32)
        if MOMENT_SCALE is None:
            s = _jnp.sqrt(_jnp.mean(_jnp.square(w)) + 1e-30)
        else:
            s = MOMENT_SCALE[name]
        km, kv = _jax.random.split(_jax.random.fold_in(key, i + 1))
        out[name] = w
        out["m_" + name] = s * _jax.random.normal(km, w.shape, _jnp.float32)
        out["v_" + name] = (s * s) * _jax.random.uniform(kv, w.shape, _jnp.float32, 0.5, 1.5)
    if N_MICROBATCH > 1:
        for name, axis in PER_EXAMPLE_BATCH_AXIS.items():
            out[name] = _to_microbatches(out[name], axis)
    return {'x': out['x'], 'norm_w': out['norm_w'], 'w_in': out['w_in'], 'q_norm': out['q_norm'], 'k_norm': out['k_norm'], 'ret_decay_fwd': out['ret_decay_fwd'], 'ret_decay_bwd': out['ret_decay_bwd'], 'ret_norm': out['ret_norm'], 'w_out': out['w_out'], 'final_norm': out['final_norm'], 'loss_target': out['loss_target'], 'm_norm_w': out['m_norm_w'], 'm_w_in': out['m_w_in'], 'm_q_norm': out['m_q_norm'], 'm_k_norm': out['m_k_norm'], 'm_ret_decay_fwd': out['m_ret_decay_fwd'], 'm_ret_decay_bwd': out['m_ret_decay_bwd'], 'm_ret_norm': out['m_ret_norm'], 'm_w_out': out['m_w_out'], 'm_final_norm': out['m_final_norm'], 'v_norm_w': out['v_norm_w'], 'v_w_in': out['v_w_in'], 'v_q_norm': out['v_q_norm'], 'v_k_norm': out['v_k_norm'], 'v_ret_decay_fwd': out['v_ret_decay_fwd'], 'v_ret_decay_bwd': out['v_ret_decay_bwd'], 'v_ret_norm': out['v_ret_norm'], 'v_w_out': out['v_w_out'], 'v_final_norm': out['v_final_norm']}


def _loss(weights, diff, rest, loss_target):
    with _jax.named_scope("forward"):
        args = {**rest, TWIN_DIFF_INPUT: diff, **{k: w.astype(_WEIGHT_DTYPES[k]) for k, w in weights.items()}}
        y = _forward(args)
    with _jax.named_scope("loss_head"):
        err = _jnp.square(y.astype(_jnp.float32) - loss_target)
        return 0.5 * _jnp.sum(_jnp.mean(err, axis=-1)) if err.ndim else 0.5 * err


def _adamw(w, g, m, v):
    m = ADAM_B1 * m + (1.0 - ADAM_B1) * g
    v = ADAM_B2 * v + (1.0 - ADAM_B2) * _jnp.square(g)
    m_hat = m / (1.0 - ADAM_B1 ** ADAM_STEP)
    v_hat = v / (1.0 - ADAM_B2 ** ADAM_STEP)
    delta = -ADAM_LR * (m_hat / (_jnp.sqrt(v_hat) + ADAM_EPS) + ADAM_WD * w)
    return delta, m, v


def reference(x, norm_w, w_in, q_norm, k_norm, ret_decay_fwd, ret_decay_bwd, ret_norm, w_out, final_norm, loss_target, m_norm_w, m_w_in, m_q_norm, m_k_norm, m_ret_decay_fwd, m_ret_decay_bwd, m_ret_norm, m_w_out, m_final_norm, v_norm_w, v_w_in, v_q_norm, v_k_norm, v_ret_decay_fwd, v_ret_decay_bwd, v_ret_norm, v_w_out, v_final_norm):
    given = dict(x=x, norm_w=norm_w, w_in=w_in, q_norm=q_norm, k_norm=k_norm, ret_decay_fwd=ret_decay_fwd, ret_decay_bwd=ret_decay_bwd, ret_norm=ret_norm, w_out=w_out, final_norm=final_norm, loss_target=loss_target, m_norm_w=m_norm_w, m_w_in=m_w_in, m_q_norm=m_q_norm, m_k_norm=m_k_norm, m_ret_decay_fwd=m_ret_decay_fwd, m_ret_decay_bwd=m_ret_decay_bwd, m_ret_norm=m_ret_norm, m_w_out=m_w_out, m_final_norm=m_final_norm, v_norm_w=v_norm_w, v_w_in=v_w_in, v_q_norm=v_q_norm, v_k_norm=v_k_norm, v_ret_decay_fwd=v_ret_decay_fwd, v_ret_decay_bwd=v_ret_decay_bwd, v_ret_norm=v_ret_norm, v_w_out=v_w_out, v_final_norm=v_final_norm)
    weights = {n: given[n] for n in TWIN_WEIGHTS}
    shared = {n: given[n] for n in SHARED_INPUTS}
    per_example = {n: given[n] for n in ['x']}
    grad_fn = _jax.value_and_grad(_loss, argnums=(0, 1))

    def one_microbatch(ex, loss_target):
        ex = dict(ex)
        diff = ex.pop(TWIN_DIFF_INPUT)
        return grad_fn(weights, diff, {**shared, **ex}, loss_target)

    if N_MICROBATCH == 1:
        loss, (grad_w, grad_x) = one_microbatch(per_example, given["loss_target"])
    else:
        def body(carry, xs):
            loss_sum, grad_sum = carry
            l_k, (gw_k, gx_k) = one_microbatch(xs[0], xs[1])
            with _jax.named_scope("update"):
                return (loss_sum + l_k, _jax.tree.map(_jnp.add, grad_sum, gw_k)), gx_k

        init = (_jnp.zeros((), _jnp.float32), _jax.tree.map(_jnp.zeros_like, weights))
        (loss, grad_w), grad_x = _jax.lax.scan(body, init, (per_example, given["loss_target"]))
    with _jax.named_scope("update"):
        delta_w, new_m, new_v = {}, {}, {}
        for n in TWIN_WEIGHTS:
            delta_w[n], new_m[n], new_v[n] = _adamw(weights[n], grad_w[n], given["m_" + n], given["v_" + n])
    return (loss, grad_x, *[grad_w[n] for n in TWIN_WEIGHTS], *[delta_w[n] for n in TWIN_WEIGHTS],
            *[new_m[n] for n in TWIN_WEIGHTS], *[new_v[n] for n in TWIN_WEIGHTS])
```

```python
import functools
import math

import jax
import jax.numpy as jnp
import numpy as np
from jax import lax
from jax.experimental import pallas as pl
from jax.experimental.pallas import tpu as pltpu

F32 = jnp.float32
BF16 = jnp.bfloat16

N_DEV = 8
HEAD_DIM = 128
ATTN_GROUP = 4
RET_QK_DIM = 128
RET_V_DIM = 256
GRID_W = 64
ROPE_THETA = 10000.0
EPS = 1e-6
ADAM_LR = 0.001
ADAM_B1 = 0.9
ADAM_B2 = 0.999
ADAM_EPS = 1e-08
ADAM_WD = 0.01
ADAM_STEP = 10
ADAM_C1 = 1.0 - ADAM_B1 ** ADAM_STEP
ADAM_C2 = 1.0 - ADAM_B2 ** ADAM_STEP
LANES = 128
SMALL_ROWS_ALIGN = 8
VMEM_LIMIT = 56 * 1024 * 1024

NT_DIMS = (((1,), (1,)), ((), ()))
TN_DIMS = (((0,), (0,)), ((), ()))
MESH = pl.DeviceIdType.MESH


def _params(sem):
    return pltpu.CompilerParams(dimension_semantics=sem, vmem_limit_bytes=VMEM_LIMIT)


def _tile(dim, pref, align=16):
    if dim <= pref:
        return dim
    for t in range(pref - pref % align, 0, -align):
        if dim % t == 0:
            return t
    raise ValueError((dim, pref, align))


def _silu_parts(z):
    sg = 1.0 / (1.0 + jnp.exp(-z))
    return z * sg, sg * (1.0 + z * (1.0 - sg))


def _log_sigmoid(x):
    return jnp.minimum(x, 0.0) - jnp.log(1.0 + jnp.exp(-jnp.abs(x)))


def _swap_pairs(z):
    lane = lax.broadcasted_iota(jnp.int32, z.shape, 1)
    return jnp.where((lane % 64) < 32, pltpu.roll(z, 96, 1), pltpu.roll(z, 32, 1))


def _rope(z, cos, sin):
    return z * cos + _swap_pairs(z) * sin


def _rope_transposed(d, cos, sin):
    return d * cos + _swap_pairs(d * sin)


def _rope_tables(seq):
    rows = seq // GRID_W
    row = jnp.repeat(jnp.arange(rows), GRID_W).astype(F32)
    col = jnp.tile(jnp.arange(GRID_W), rows).astype(F32)
    axis_dim = HEAD_DIM // 2
    inv = ROPE_THETA ** (-jnp.arange(0, axis_dim, 2, dtype=F32) / axis_dim)
    ar = row[:, None] * inv[None, :]
    ac = col[:, None] * inv[None, :]
    cos = jnp.concatenate([jnp.cos(ar), jnp.cos(ar), jnp.cos(ac), jnp.cos(ac)], axis=-1)
    sin = jnp.concatenate([-jnp.sin(ar), jnp.sin(ar), -jnp.sin(ac), jnp.sin(ac)], axis=-1)
    return cos, sin


def _matmul(a, b, *, name, trans_b=False, b_off=0, n=None, out_dtype=F32, residual=None,
            tm=1024, tn=512, tk=4096):
    m, k = a.shape
    if n is None:
        n = b.shape[0] if trans_b else b.shape[1]
    tm, tn, tk = _tile(m, tm), _tile(n, tn, LANES), _tile(k, tk, LANES)
    assert b_off % tn == 0
    joff = b_off // tn
    nk = k // tk
    has_res = residual is not None

    def body(*refs):
        if has_res:
            a_ref, b_ref, r_ref, o_ref = refs[:4]
        else:
            a_ref, b_ref, o_ref = refs[:3]
        if trans_b:
            part = lax.dot_general(a_ref[...], b_ref[...], NT_DIMS, preferred_element_type=F32)
        else:
            part = jnp.dot(a_ref[...], b_ref[...], preferred_element_type=F32)

        def finish(r):
            if has_res:
                r = r + r_ref[...]
            o_ref[...] = r.astype(o_ref.dtype)

        if nk == 1:
            finish(part)
        else:
            acc_ref = refs[-1]
            kk = pl.program_id(2)

            @pl.when(kk == 0)
            def _():
                acc_ref[...] = part

            @pl.when(kk > 0)
            def _():
                acc_ref[...] += part

            @pl.when(kk == nk - 1)
            def _():
                finish(acc_ref[...])

    if trans_b:
        b_spec = pl.BlockSpec((tn, tk), lambda i, j, kk: (j + joff, kk))
    else:
        b_spec = pl.BlockSpec((tk, tn), lambda i, j, kk: (kk, j + joff))
    in_specs = [pl.BlockSpec((tm, tk), lambda i, j, kk: (i, kk)), b_spec]
    args = [a, b]
    if has_res:
        in_specs.append(pl.BlockSpec((tm, tn), lambda i, j, kk: (i, j)))
        args.append(residual)
    return pl.pallas_call(
        body, name=name, grid=(m // tm, n // tn, nk),
        in_specs=in_specs,
        out_specs=pl.BlockSpec((tm, tn), lambda i, j, kk: (i, j)),
        out_shape=jax.ShapeDtypeStruct((m, n), out_dtype),
        scratch_shapes=[pltpu.VMEM((tm, tn), F32)] if nk > 1 else [],
        compiler_params=_params(("parallel", "parallel", "arbitrary")),
    )(*args)


def _rms_fwd(x, w, *, ts=256):
    s, d = x.shape
    ts = _tile(s, ts)

    def body(x_ref, w_ref, h_ref):
        xv = x_ref[...]
        r = lax.rsqrt(jnp.mean(xv * xv, axis=-1, keepdims=True) + EPS)
        h_ref[...] = (xv * r * w_ref[...]).astype(h_ref.dtype)

    row = pl.BlockSpec((ts, d), lambda i: (i, 0))
    return pl.pallas_call(
        body, name="rms_fwd", grid=(s // ts,),
        in_specs=[row, pl.BlockSpec((1, d), lambda i: (0, 0))],
        out_specs=row, out_shape=jax.ShapeDtypeStruct((s, d), BF16),
        compiler_params=_params(("parallel",)),
    )(x, w.reshape(1, d))


def _rms_bwd(dh, x, g, w, *, ts=256):
    s, d = x.shape
    ts = _tile(s, ts)

    def body(dh_ref, x_ref, g_ref, w_ref, dx_ref, dxb_ref, dw_ref):
        xv = x_ref[...]
        r = lax.rsqrt(jnp.mean(xv * xv, axis=-1, keepdims=True) + EPS)
        xh = xv * r
        dhv = dh_ref[...]
        dn = dhv * w_ref[...]
        dx = g_ref[...] + r * (dn - xh * jnp.mean(dn * xh, axis=-1, keepdims=True))
        dx_ref[...] = dx
        dxb_ref[...] = dx.astype(BF16)
        part = jnp.sum(dhv * xh, axis=0, keepdims=True)

        @pl.when(pl.program_id(0) == 0)
        def _():
            dw_ref[...] = part

        @pl.when(pl.program_id(0) > 0)
        def _():
            dw_ref[...] += part

    row = pl.BlockSpec((ts, d), lambda i: (i, 0))
    vec = pl.BlockSpec((1, d), lambda i: (0, 0))
    return pl.pallas_call(
        body, name="rms_bwd", grid=(s // ts,),
        in_specs=[row, row, row, vec],
        out_specs=[row, row, vec],
        out_shape=[jax.ShapeDtypeStruct((s, d), F32), jax.ShapeDtypeStruct((s, d), BF16),
                   jax.ShapeDtypeStruct((1, d), F32)],
        compiler_params=_params(("arbitrary",)),
    )(dh, x, g, w.reshape(1, d))


def _loss_head(x, target, w, *, ts=256):
    s, d = x.shape
    ts = _tile(s, ts)

    def body(x_ref, t_ref, w_ref, loss_ref, dx_ref, dxb_ref, dw_ref):
        xv = x_ref[...]
        r = lax.rsqrt(jnp.mean(xv * xv, axis=-1, keepdims=True) + EPS)
        xh = xv * r
        wv = w_ref[...]
        diff = xh * wv - t_ref[...]
        lpart = 0.5 * jnp.sum(jnp.mean(diff * diff, axis=-1, keepdims=True), axis=0, keepdims=True)
        dout = diff * (1.0 / d)
        dn = dout * wv
        dx = r * (dn - xh * jnp.mean(dn * xh, axis=-1, keepdims=True))
        dx_ref[...] = dx
        dxb_ref[...] = dx.astype(BF16)
        part = jnp.sum(dout * xh, axis=0, keepdims=True)
        lrow = jnp.broadcast_to(lpart, loss_ref.shape)

        @pl.when(pl.program_id(0) == 0)
        def _():
            dw_ref[...] = part
            loss_ref[...] = lrow

        @pl.when(pl.program_id(0) > 0)
        def _():
            dw_ref[...] += part
            loss_ref[...] += lrow

    row = pl.BlockSpec((ts, d), lambda i: (i, 0))
    vec = pl.BlockSpec((1, d), lambda i: (0, 0))
    return pl.pallas_call(
        body, name="loss_head", grid=(s // ts,),
        in_specs=[row, row, vec],
        out_specs=[pl.BlockSpec((1, LANES), lambda i: (0, 0)), row, row, vec],
        out_shape=[jax.ShapeDtypeStruct((1, LANES), F32), jax.ShapeDtypeStruct((s, d), F32),
                   jax.ShapeDtypeStruct((s, d), BF16), jax.ShapeDtypeStruct((1, d), F32)],
        compiler_params=_params(("arbitrary",)),
    )(x, target, w.reshape(1, d))


def _prep_fwd(aq, ak, av, rq, rk, rv, cos, sin, qw, kw, *, ts=256):
    s = aq.shape[0]
    ts = _tile(s, ts)
    attn_scale = HEAD_DIM ** -0.5
    ret_scale = RET_QK_DIM ** -0.5
    nq, nk, nr = aq.shape[1] // HEAD_DIM, ak.shape[1] // HEAD_DIM, rq.shape[1] // RET_QK_DIM

    def body(aq_ref, ak_ref, av_ref, rq_ref, rk_ref, rv_ref, cos_ref, sin_ref, qw_ref, kw_ref,
             q_out, k_out, v_out, rq_out, rk_out, rv_out):
        c, sn = cos_ref[...], sin_ref[...]

        def normed(u, w):
            return u * lax.rsqrt(jnp.mean(u * u, axis=-1, keepdims=True) + EPS) * w

        for j in range(nq):
            sl = slice(j * HEAD_DIM, (j + 1) * HEAD_DIM)
            q_out[:, sl] = (_rope(normed(aq_ref[:, sl], qw_ref[...]), c, sn) * attn_scale).astype(BF16)
        for j in range(nk):
            sl = slice(j * HEAD_DIM, (j + 1) * HEAD_DIM)
            k_out[:, sl] = _rope(normed(ak_ref[:, sl], kw_ref[...]), c, sn).astype(BF16)
        for j in range(nr):
            sl = slice(j * RET_QK_DIM, (j + 1) * RET_QK_DIM)
            rq_out[:, sl] = _rope(rq_ref[:, sl], c, sn).astype(BF16)
            rk_out[:, sl] = (_rope(rk_ref[:, sl], c, sn) * ret_scale).astype(BF16)
        v_out[...] = av_ref[...].astype(BF16)
        rv_out[...] = rv_ref[...].astype(BF16)

    def row(arr):
        return pl.BlockSpec((ts, arr.shape[1]), lambda i: (i, 0))

    vec = pl.BlockSpec((1, HEAD_DIM), lambda i: (0, 0))
    ins = [aq, ak, av, rq, rk, rv]
    return pl.pallas_call(
        body, name="prep_fwd", grid=(s // ts,),
        in_specs=[row(a) for a in ins] + [row(cos), row(sin), vec, vec],
        out_specs=[row(a) for a in ins],
        out_shape=[jax.ShapeDtypeStruct(a.shape, BF16) for a in ins],
        compiler_params=_params(("parallel",)),
    )(*ins, cos, sin, qw.reshape(1, HEAD_DIM), kw.reshape(1, HEAD_DIM))


def _prep_bwd(dq, dk, drq, drk, aq, ak, cos, sin, qw, kw, *, ts=256):
    s = aq.shape[0]
    ts = _tile(s, ts)
    attn_scale = HEAD_DIM ** -0.5
    ret_scale = RET_QK_DIM ** -0.5
    nq, nk, nr = aq.shape[1] // HEAD_DIM, ak.shape[1] // HEAD_DIM, drq.shape[1] // RET_QK_DIM

    def body(dq_ref, dk_ref, drq_ref, drk_ref, aq_ref, ak_ref, cos_ref, sin_ref, qw_ref, kw_ref,
             daq_out, dak_out, drq_out, drk_out, dqw_ref, dkw_ref):
        c, sn = cos_ref[...], sin_ref[...]

        def unrope(d):
            return _rope_transposed(d, c, sn)

        def norm_bwd(dun, u, w):
            r = lax.rsqrt(jnp.mean(u * u, axis=-1, keepdims=True) + EPS)
            uh = u * r
            dn = dun * w
            du = r * (dn - uh * jnp.mean(dn * uh, axis=-1, keepdims=True))
            return du, jnp.sum(dun * uh, axis=0, keepdims=True)

        dqw = jnp.zeros((1, HEAD_DIM), F32)
        for j in range(nq):
            sl = slice(j * HEAD_DIM, (j + 1) * HEAD_DIM)
            du, dw = norm_bwd(unrope(dq_ref[:, sl] * attn_scale), aq_ref[:, sl], qw_ref[...])
            daq_out[:, sl] = du.astype(BF16)
            dqw = dqw + dw
        dkw = jnp.zeros((1, HEAD_DIM), F32)
        for j in range(nk):
            sl = slice(j * HEAD_DIM, (j + 1) * HEAD_DIM)
            du, dw = norm_bwd(unrope(dk_ref[:, sl]), ak_ref[:, sl], kw_ref[...])
            dak_out[:, sl] = du.astype(BF16)
            dkw = dkw + dw
        for j in range(nr):
            sl = slice(j * RET_QK_DIM, (j + 1) * RET_QK_DIM)
            drq_out[:, sl] = unrope(drq_ref[:, sl]).astype(BF16)
            drk_out[:, sl] = unrope(drk_ref[:, sl] * ret_scale).astype(BF16)

        @pl.when(pl.program_id(0) == 0)
        def _():
            dqw_ref[...] = dqw
            dkw_ref[...] = dkw

        @pl.when(pl.program_id(0) > 0)
        def _():
            dqw_ref[...] += dqw
            dkw_ref[...] += dkw

    def row(arr):
        return pl.BlockSpec((ts, arr.shape[1]), lambda i: (i, 0))

    vec = pl.BlockSpec((1, HEAD_DIM), lambda i: (0, 0))
    ins = [dq, dk, drq, drk, aq, ak, cos, sin]
    outs = [dq, dk, drq, drk]
    return pl.pallas_call(
        body, name="prep_bwd", grid=(s // ts,),
        in_specs=[row(a) for a in ins] + [vec, vec],
        out_specs=[row(a) for a in outs] + [vec, vec],
        out_shape=[jax.ShapeDtypeStruct(a.shape, BF16) for a in outs]
        + [jax.ShapeDtypeStruct((1, HEAD_DIM), F32)] * 2,
        compiler_params=_params(("arbitrary",)),
    )(*ins, qw.reshape(1, HEAD_DIM), kw.reshape(1, HEAD_DIM))


def _attn_fwd(q, k, v, *, tq=256):
    s, aw = q.shape
    tq = _tile(s, tq)
    heads, kvh = aw // HEAD_DIM, k.shape[1] // HEAD_DIM
    grp = heads // kvh

    def body(q_ref, k_ref, v_ref, o_ref, lse_ref):
        sc = lax.dot_general(q_ref[...], k_ref[...], NT_DIMS, preferred_element_type=F32)
        m = jnp.max(sc, axis=-1, keepdims=True)
        p = jnp.exp(sc - m)
        l = jnp.sum(p, axis=-1, keepdims=True)
        o = jnp.dot(p.astype(BF16), v_ref[...], preferred_element_type=F32)
        o_ref[...] = (o / l).astype(o_ref.dtype)
        lse_ref[...] = jnp.broadcast_to(m + jnp.log(l), lse_ref.shape)

    qspec = pl.BlockSpec((tq, HEAD_DIM), lambda kv, g, i: (i, kv * grp + g))
    kspec = pl.BlockSpec((s, HEAD_DIM), lambda kv, g, i: (0, kv))
    return pl.pallas_call(
        body, name="attn_fwd", grid=(kvh, grp, s // tq),
        in_specs=[qspec, kspec, kspec],
        out_specs=[qspec, qspec],
        out_shape=[jax.ShapeDtypeStruct((s, aw), BF16), jax.ShapeDtypeStruct((s, aw), F32)],
        compiler_params=_params(("parallel", "parallel", "parallel")),
    )(q, k, v)


def _attn_bwd(q, k, v, o, do, lse, *, tq=256):
    s, aw = q.shape
    tq = _tile(s, tq)
    heads, kvh = aw // HEAD_DIM, k.shape[1] // HEAD_DIM
    grp = heads // kvh
    nq = s // tq

    def body(q_ref, k_ref, v_ref, o_ref, do_ref, lse_ref, dq_ref, dk_ref, dv_ref, dk_acc, dv_acc):
        g, i = pl.program_id(1), pl.program_id(2)
        qv, kv_, vv, dov = q_ref[...], k_ref[...], v_ref[...], do_ref[...]
        sc = lax.dot_general(qv, kv_, NT_DIMS, preferred_element_type=F32)
        p = jnp.exp(sc - lse_ref[:, :1])
        dp = lax.dot_general(dov, vv, NT_DIMS, preferred_element_type=F32)
        delta = jnp.sum(dov.astype(F32) * o_ref[...].astype(F32), axis=-1, keepdims=True)
        ds = (p * (dp - delta)).astype(BF16)
        dq_ref[...] = jnp.dot(ds, kv_, preferred_element_type=F32)
        dvp = lax.dot_general(p.astype(BF16), dov, TN_DIMS, preferred_element_type=F32)
        dkp = lax.dot_general(ds, qv, TN_DIMS, preferred_element_type=F32)
        first = jnp.logical_and(g == 0, i == 0)

        @pl.when(first)
        def _():
            dv_acc[...] = dvp
            dk_acc[...] = dkp

        @pl.when(jnp.logical_not(first))
        def _():
            dv_acc[...] += dvp
            dk_acc[...] += dkp

        @pl.when(jnp.logical_and(g == grp - 1, i == nq - 1))
        def _():
            dk_ref[...] = dk_acc[...]
            dv_ref[...] = dv_acc[...].astype(dv_ref.dtype)

    qspec = pl.BlockSpec((tq, HEAD_DIM), lambda kv, g, i: (i, kv * grp + g))
    kspec = pl.BlockSpec((s, HEAD_DIM), lambda kv, g, i: (0, kv))
    return pl.pallas_call(
        body, name="attn_bwd", grid=(kvh, grp, nq),
        in_specs=[qspec, kspec, kspec, qspec, qspec, qspec],
        out_specs=[qspec, kspec, kspec],
        out_shape=[jax.ShapeDtypeStruct((s, aw), F32), jax.ShapeDtypeStruct(k.shape, F32),
                   jax.ShapeDtypeStruct(v.shape, BF16)],
        scratch_shapes=[pltpu.VMEM((s, HEAD_DIM), F32), pltpu.VMEM((s, HEAD_DIM), F32)],
        compiler_params=_params(("parallel", "arbitrary", "arbitrary")),
    )(q, k, v, o, do, lse)


def _decay_matrix(i, tq, s, df_ref, db_ref):
    n = i * tq + lax.broadcasted_iota(jnp.int32, (tq, s), 0)
    m = lax.broadcasted_iota(jnp.int32, (tq, s), 1)
    diff = (n - m).astype(F32)
    lf = _log_sigmoid(df_ref[0][:, :1])
    lb = _log_sigmoid(db_ref[0][:, :1])
    dec = jnp.exp(diff * jnp.where(diff >= 0, lf, -lb))
    return dec, diff


def _ret_specs(s, tq):
    qspec = pl.BlockSpec((tq, RET_QK_DIM), lambda h, i: (i, h))
    kspec = pl.BlockSpec((s, RET_QK_DIM), lambda h, i: (0, h))
    vspec = pl.BlockSpec((s, RET_V_DIM), lambda h, i: (0, h))
    ospec = pl.BlockSpec((tq, RET_V_DIM), lambda h, i: (i, h))
    dspec = pl.BlockSpec((1, 1, LANES), lambda h, i: (h, 0, 0))
    return qspec, kspec, vspec, ospec, dspec


def _ret_fwd(q, k, v, dec_f, dec_b, *, tq=256):
    s, qw = q.shape
    tq = _tile(s, tq)
    heads = qw // RET_QK_DIM
    qspec, kspec, vspec, ospec, dspec = _ret_specs(s, tq)

    def body(q_ref, k_ref, v_ref, df_ref, db_ref, o_ref):
        dec, _ = _decay_matrix(pl.program_id(1), tq, s, df_ref, db_ref)
        sc = lax.dot_general(q_ref[...], k_ref[...], NT_DIMS, preferred_element_type=F32)
        o_ref[...] = jnp.dot((sc * dec).astype(BF16), v_ref[...], preferred_element_type=F32)

    return pl.pallas_call(
        body, name="ret_fwd", grid=(heads, s // tq),
        in_specs=[qspec, kspec, vspec, dspec, dspec],
        out_specs=ospec, out_shape=jax.ShapeDtypeStruct(v.shape, F32),
        compiler_params=_params(("parallel", "parallel")),
    )(q, k, v, dec_f, dec_b)


def _ret_bwd(q, k, v, do, dec_f, dec_b, *, tq=256):
    s, qw = q.shape
    tq = _tile(s, tq)
    heads = qw // RET_QK_DIM
    nq = s // tq
    qspec, kspec, vspec, ospec, dspec = _ret_specs(s, tq)
    gspec = pl.BlockSpec((1, 8, LANES), lambda h, i: (h, 0, 0))

    def body(q_ref, k_ref, v_ref, do_ref, df_ref, db_ref, dq_ref, dk_ref, dv_ref, gf_ref, gb_ref,
             dk_acc, dv_acc):
        i = pl.program_id(1)
        dec, diff = _decay_matrix(i, tq, s, df_ref, db_ref)
        qv, kv_, dov = q_ref[...], k_ref[...], do_ref[...]
        sc = lax.dot_general(qv, kv_, NT_DIMS, preferred_element_type=F32)
        p = sc * dec
        dp = lax.dot_general(dov, v_ref[...], NT_DIMS, preferred_element_type=F32)
        ds = (dp * dec).astype(BF16)
        dq_ref[...] = jnp.dot(ds, kv_, preferred_element_type=F32)
        dkp = lax.dot_general(ds, qv, TN_DIMS, preferred_element_type=F32)
        dvp = lax.dot_general(p.astype(BF16), dov, TN_DIMS, preferred_element_type=F32)
        t = dp * p * diff
        sf = 1.0 / (1.0 + jnp.exp(df_ref[0][:, :1]))
        sb = 1.0 / (1.0 + jnp.exp(db_ref[0][:, :1]))
        gf = jnp.sum(jnp.sum(jnp.where(diff > 0, t, 0.0), axis=0, keepdims=True), axis=1, keepdims=True) * sf
        gb = jnp.sum(jnp.sum(jnp.where(diff < 0, -t, 0.0), axis=0, keepdims=True), axis=1, keepdims=True) * sb
        gf = jnp.broadcast_to(gf.reshape(1, 1, 1), gf_ref.shape)
        gb = jnp.broadcast_to(gb.reshape(1, 1, 1), gb_ref.shape)

        @pl.when(i == 0)
        def _():
            dk_acc[...] = dkp
            dv_acc[...] = dvp
            gf_ref[...] = gf
            gb_ref[...] = gb

        @pl.when(i > 0)
        def _():
            dk_acc[...] += dkp
            dv_acc[...] += dvp
            gf_ref[...] += gf
            gb_ref[...] += gb

        @pl.when(i == nq - 1)
        def _():
            dk_ref[...] = dk_acc[...]
            dv_ref[...] = dv_acc[...].astype(dv_ref.dtype)

    return pl.pallas_call(
        body, name="ret_bwd", grid=(heads, nq),
        in_specs=[qspec, kspec, vspec, ospec, dspec, dspec],
        out_specs=[qspec, kspec, vspec, gspec, gspec],
        out_shape=[jax.ShapeDtypeStruct(q.shape, F32), jax.ShapeDtypeStruct(k.shape, F32),
                   jax.ShapeDtypeStruct(v.shape, BF16),
                   jax.ShapeDtypeStruct((heads, 8, LANES), F32), jax.ShapeDtypeStruct((heads, 8, LANES), F32)],
        scratch_shapes=[pltpu.VMEM((s, RET_QK_DIM), F32), pltpu.VMEM((s, RET_V_DIM), F32)],
        compiler_params=_params(("parallel", "arbitrary")),
    )(q, k, v, do, dec_f, dec_b)


def _gate_fwd(att, ag, ret, rg, rnw, *, ts=256):
    s, aw = att.shape
    rw = ret.shape[1]
    ts = _tile(s, ts)
    rheads = rw // RET_V_DIM

    def body(att_ref, ag_ref, ret_ref, rg_ref, w_ref, y_ref):
        sa, _ = _silu_parts(ag_ref[...])
        y_ref[:, :aw] = (sa * att_ref[...].astype(F32)).astype(BF16)
        for h in range(rheads):
            sl = slice(h * RET_V_DIM, (h + 1) * RET_V_DIM)
            rt = ret_ref[:, sl]
            rn = rt * lax.rsqrt(jnp.mean(rt * rt, axis=-1, keepdims=True) + EPS) * w_ref[:, sl]
            sr, _ = _silu_parts(rg_ref[:, sl])
            y_ref[:, aw + h * RET_V_DIM:aw + (h + 1) * RET_V_DIM] = (sr * rn).astype(BF16)

    def row(w):
        return pl.BlockSpec((ts, w), lambda i: (i, 0))

    return pl.pallas_call(
        body, name="gate_fwd", grid=(s // ts,),
        in_specs=[row(aw), row(aw), row(rw), row(rw), pl.BlockSpec((1, rw), lambda i: (0, 0))],
        out_specs=row(aw + rw), out_shape=jax.ShapeDtypeStruct((s, aw + rw), BF16),
        compiler_params=_params(("parallel",)),
    )(att, ag, ret, rg, rnw.reshape(1, rw))


def _gate_bwd(dy, att, ag, ret, rg, rnw, *, ts=256):
    s, aw = att.shape
    rw = ret.shape[1]
    ts = _tile(s, ts)
    rheads = rw // RET_V_DIM

    def body(dy_ref, att_ref, ag_ref, ret_ref, rg_ref, w_ref, datt_ref, dag_ref, dret_ref, drg_ref, dw_ref):
        sa, dsa = _silu_parts(ag_ref[...])
        dya = dy_ref[:, :aw]
        datt_ref[...] = (dya * sa).astype(BF16)
        dag_ref[...] = (dya * att_ref[...].astype(F32) * dsa).astype(BF16)
        parts = []
        for h in range(rheads):
            sl = slice(h * RET_V_DIM, (h + 1) * RET_V_DIM)
            rt = ret_ref[:, sl]
            rr = lax.rsqrt(jnp.mean(rt * rt, axis=-1, keepdims=True) + EPS)
            rh = rt * rr
            wv = w_ref[:, sl]
            sr, dsr = _silu_parts(rg_ref[:, sl])
            dyr = dy_ref[:, aw + h * RET_V_DIM:aw + (h + 1) * RET_V_DIM]
            drg_ref[:, sl] = (dyr * rh * wv * dsr).astype(BF16)
            drn = dyr * sr
            dn = drn * wv
            dret_ref[:, sl] = (rr * (dn - rh * jnp.mean(dn * rh, axis=-1, keepdims=True))).astype(BF16)
            parts.append(jnp.sum(drn * rh, axis=0, keepdims=True))
        part = jnp.concatenate(parts, axis=-1)

        @pl.when(pl.program_id(0) == 0)
        def _():
            dw_ref[...] = part

        @pl.when(pl.program_id(0) > 0)
        def _():
            dw_ref[...] += part

    def row(w):
        return pl.BlockSpec((ts, w), lambda i: (i, 0))

    vec = pl.BlockSpec((1, rw), lambda i: (0, 0))
    return pl.pallas_call(
        body, name="gate_bwd", grid=(s // ts,),
        in_specs=[row(aw + rw), row(aw), row(aw), row(rw), row(rw), vec],
        out_specs=[row(aw), row(aw), row(rw), row(rw), vec],
        out_shape=[jax.ShapeDtypeStruct((s, aw), BF16), jax.ShapeDtypeStruct((s, aw), BF16),
                   jax.ShapeDtypeStruct((s, rw), BF16), jax.ShapeDtypeStruct((s, rw), BF16),
                   jax.ShapeDtypeStruct((1, rw), F32)],
        compiler_params=_params(("arbitrary",)),
    )(dy, att, ag, ret, rg, rnw.reshape(1, rw))


def _mesh_position():
    x, y, c = lax.axis_index("x"), lax.axis_index("y"), lax.axis_index("c")
    return x, y, c, 4 * x + 2 * y + c


def _peer(x, y, c, k):
    px = 1 - x if k & 4 else x
    py = 1 - y if k & 2 else y
    pc = 1 - c if k & 1 else c
    return (px, py, pc), 4 * px + 2 * py + pc


def _all_gather_weights(win_sh, wout_sh):
    d, iws = win_sh.shape
    mws = wout_sh.shape[0]

    def body(win_ref, wout_ref, win_full, wout_full, send_sems, recv_sems, local_sems):
        x, y, c, me = _mesh_position()

        def cols(p):
            return win_full.at[:, pl.ds(pl.multiple_of(p * iws, LANES), iws)]

        def rows(p):
            return wout_full.at[pl.ds(pl.multiple_of(p * mws, 16), mws), :]

        own = [pltpu.make_async_copy(win_ref, cols(me), local_sems.at[0]),
               pltpu.make_async_copy(wout_ref, rows(me), local_sems.at[1])]
        for cp in own:
            cp.start()
        sends, recvs = [], []
        for k in range(1, N_DEV):
            peer, pid = _peer(x, y, c, k)
            for a, (src, mine, theirs) in enumerate(((win_ref, cols(me), cols(pid)), (wout_ref, rows(me), rows(pid)))):
                sem = 2 * (k - 1) + a
                sends.append(pltpu.make_async_remote_copy(
                    src_ref=src, dst_ref=mine, send_sem=send_sems.at[sem], recv_sem=recv_sems.at[sem],
                    device_id=peer, device_id_type=MESH))
                recvs.append(pltpu.make_async_remote_copy(
                    src_ref=src, dst_ref=theirs, send_sem=send_sems.at[sem], recv_sem=recv_sems.at[sem],
                    device_id=peer, device_id_type=MESH))
        for cp in sends:
            cp.start()
        for cp in recvs:
            cp.wait_recv()
        for cp in sends:
            cp.wait_send()
        for cp in own:
            cp.wait()

    any_spec = pl.BlockSpec(memory_space=pl.ANY)
    n_sem = 2 * (N_DEV - 1)
    return pl.pallas_call(
        body, name="all_gather_weights",
        in_specs=[any_spec, any_spec], out_specs=[any_spec, any_spec],
        out_shape=[jax.ShapeDtypeStruct((d, N_DEV * iws), win_sh.dtype),
                   jax.ShapeDtypeStruct((N_DEV * mws, d), wout_sh.dtype)],
        scratch_shapes=[pltpu.SemaphoreType.DMA((n_sem,)), pltpu.SemaphoreType.DMA((n_sem,)),
                        pltpu.SemaphoreType.DMA((2,))],
        compiler_params=pltpu.CompilerParams(has_side_effects=True),
    )(win_sh, wout_sh)


def _reduce_scatter_slabs(dwin, dwout):
    d, iw = dwin.shape
    mw = dwout.shape[0]
    iws, mws = iw // N_DEV, mw // N_DEV

    def body(dwin_ref, dwout_ref, land_in, land_out, send_sems, recv_sems, local_sems):
        x, y, c, me = _mesh_position()

        def cols(p):
            return dwin_ref.at[:, pl.ds(pl.multiple_of(p * iws, LANES), iws)]

        def rows(p):
            return dwout_ref.at[pl.ds(pl.multiple_of(p * mws, 16), mws), :]

        own = [pltpu.make_async_copy(cols(me), land_in.at[me], local_sems.at[0]),
               pltpu.make_async_copy(rows(me), land_out.at[me], local_sems.at[1])]
        for cp in own:
            cp.start()
        sends, recvs = [], []
        for k in range(1, N_DEV):
            peer, pid = _peer(x, y, c, k)
            for a, (src, land) in enumerate(((cols(pid), land_in), (rows(pid), land_out))):
                sem = 2 * (k - 1) + a
                sends.append(pltpu.make_async_remote_copy(
                    src_ref=src, dst_ref=land.at[me], send_sem=send_sems.at[sem], recv_sem=recv_sems.at[sem],
                    device_id=peer, device_id_type=MESH))
                recvs.append(pltpu.make_async_remote_copy(
                    src_ref=src, dst_ref=land.at[pid], send_sem=send_sems.at[sem], recv_sem=recv_sems.at[sem],
                    device_id=peer, device_id_type=MESH))
        for cp in sends:
            cp.start()
        for cp in recvs:
            cp.wait_recv()
        for cp in sends:
            cp.wait_send()
        for cp in own:
            cp.wait()

    any_spec = pl.BlockSpec(memory_space=pl.ANY)
    n_sem = 2 * (N_DEV - 1)
    return pl.pallas_call(
        body, name="reduce_scatter_slabs",
        in_specs=[any_spec, any_spec], out_specs=[any_spec, any_spec],
        out_shape=[jax.ShapeDtypeStruct((N_DEV, d, iws), dwin.dtype),
                   jax.ShapeDtypeStruct((N_DEV, mws, d), dwout.dtype)],
        scratch_shapes=[pltpu.SemaphoreType.DMA((n_sem,)), pltpu.SemaphoreType.DMA((n_sem,)),
                        pltpu.SemaphoreType.DMA((2,))],
        compiler_params=pltpu.CompilerParams(has_side_effects=True),
    )(dwin, dwout)


def _exchange_small(buf, *, name):
    r = buf.shape[0]

    def body(buf_ref, all_ref, sum_ref, send_sems, recv_sems):
        x, y, c, me = _mesh_position()
        all_ref[me] = buf_ref[...]
        sends, recvs = [], []
        for k in range(1, N_DEV):
            peer, pid = _peer(x, y, c, k)
            sends.append(pltpu.make_async_remote_copy(
                src_ref=buf_ref, dst_ref=all_ref.at[me], send_sem=send_sems.at[k - 1], recv_sem=recv_sems.at[k - 1],
                device_id=peer, device_id_type=MESH))
            recvs.append(pltpu.make_async_remote_copy(
                src_ref=buf_ref, dst_ref=all_ref.at[pid], send_sem=send_sems.at[k - 1], recv_sem=recv_sems.at[k - 1],
                device_id=peer, device_id_type=MESH))
        for cp in sends:
            cp.start()
        for cp in recvs:
            cp.wait_recv()
        for cp in sends:
            cp.wait_send()
        total = all_ref[0]
        for p in range(1, N_DEV):
            total = total + all_ref[p]
        sum_ref[...] = total

    vmem = pl.BlockSpec(memory_space=pltpu.VMEM)
    return pl.pallas_call(
        body, name=name,
        in_specs=[vmem], out_specs=[vmem, vmem],
        out_shape=[jax.ShapeDtypeStruct((N_DEV, r, LANES), F32), jax.ShapeDtypeStruct((r, LANES), F32)],
        scratch_shapes=[pltpu.SemaphoreType.DMA((N_DEV - 1,)), pltpu.SemaphoreType.DMA((N_DEV - 1,))],
        compiler_params=pltpu.CompilerParams(has_side_effects=True),
    )(buf)


def _adamw_math(w, g, m, v):
    m2 = ADAM_B1 * m + (1.0 - ADAM_B1) * g
    v2 = ADAM_B2 * v + (1.0 - ADAM_B2) * (g * g)
    delta = -ADAM_LR * ((m2 / ADAM_C1) / (jnp.sqrt(v2 / ADAM_C2) + ADAM_EPS) + ADAM_WD * w)
    return delta, m2, v2


def _adamw_slabs(w, m, v, land, *, tr, name):
    r, c = w.shape
    tr = _tile(r, tr)

    def body(w_ref, m_ref, v_ref, land_ref, g_ref, d_ref, m2_ref, v2_ref):
        g = land_ref[0].astype(F32)
        for p in range(1, N_DEV):
            g = g + land_ref[p].astype(F32)
        delta, m2, v2 = _adamw_math(w_ref[...], g, m_ref[...], v_ref[...])
        g_ref[...] = g
        d_ref[...] = delta
        m2_ref[...] = m2
        v2_ref[...] = v2

    row = pl.BlockSpec((tr, c), lambda i: (i, 0))
    return pl.pallas_call(
        body, name=name, grid=(r // tr,),
        in_specs=[row, row, row, pl.BlockSpec((N_DEV, tr, c), lambda i: (0, i, 0))],
        out_specs=[row] * 4, out_shape=[jax.ShapeDtypeStruct((r, c), F32)] * 4,
        compiler_params=_params(("parallel",)),
    )(w, m, v, land)


def _adamw_small(w, g, m, v):
    def body(w_ref, g_ref, m_ref, v_ref, d_ref, m2_ref, v2_ref):
        delta, m2, v2 = _adamw_math(w_ref[...], g_ref[...], m_ref[...], v_ref[...])
        d_ref[...] = delta
        m2_ref[...] = m2
        v2_ref[...] = v2

    vmem = pl.BlockSpec(memory_space=pltpu.VMEM)
    return pl.pallas_call(
        body, name="adamw_small", in_specs=[vmem] * 4, out_specs=[vmem] * 3,
        out_shape=[jax.ShapeDtypeStruct(w.shape, F32)] * 3,
    )(w, g, m, v)


def _pack(parts):
    flat = jnp.concatenate([p.reshape(-1).astype(F32) for p in parts])
    rows = -(-flat.shape[0] // LANES)
    rows = -(-rows // SMALL_ROWS_ALIGN) * SMALL_ROWS_ALIGN
    flat = jnp.pad(flat, (0, rows * LANES - flat.shape[0]))
    return flat.reshape(rows, LANES)


def _unpack(buf, shapes):
    flat = buf.reshape(-1)
    out, pos = [], 0
    for shp in shapes:
        size = math.prod(shp)
        out.append(flat[pos:pos + size].reshape(shp))
        pos += size
    return out


def _section_widths(d):
    aw = d // 2
    kw = aw // ATTN_GROUP
    rw = d - aw
    rqw = (rw // RET_V_DIM) * RET_QK_DIM
    return (aw, kw, kw, aw, rqw, rqw, rw, rw)


def _layer_fwd(xl, nw, win_full, wout_full, qn, kn, dec_f, dec_b, rn, cos, sin):
    widths = _section_widths(xl.shape[1])
    offs = tuple(int(o) for o in np.cumsum((0,) + widths)[:-1])
    sec_tn = _tile(widths[1], 512)
    h = _rms_fwd(xl, nw)
    secs = [_matmul(h, win_full, name="proj", b_off=offs[i], n=widths[i], tn=sec_tn) for i in range(8)]
    aq, ak, av, ag, rq, rk, rv, rg = secs
    q, k, v, rqr, rkr, rvb = _prep_fwd(aq, ak, av, rq, rk, rv, cos, sin, qn, kn)
    att, lse = _attn_fwd(q, k, v)
    ret = _ret_fwd(rqr, rkr, rvb, dec_f, dec_b)
    y = _gate_fwd(att, ag, ret, rg, rn)
    xn = _matmul(y, wout_full, name="out_proj", residual=xl)
    saved = dict(x=xl, h=h, aq=aq, ak=ak, ag=ag, rg=rg, q=q, k=k, v=v, rq=rqr, rk=rkr, rv=rvb,
                 att=att, lse=lse, ret=ret, y=y, win=win_full, wout=wout_full)
    return xn, saved


def _layer_bwd(g, gb, sv, nw, qn, kn, dec_f, dec_b, rn, cos, sin):
    dy = _matmul(gb, sv["wout"], name="d_y", trans_b=True)
    dwout = _matmul(sv["y"].T, gb, name="d_wout", out_dtype=BF16)
    datt, dag, dret, drg, drn = _gate_bwd(dy, sv["att"], sv["ag"], sv["ret"], sv["rg"], rn)
    dq, dk, dav = _attn_bwd(sv["q"], sv["k"], sv["v"], sv["att"], datt, sv["lse"])
    drq, drk, drv, gf, gbk = _ret_bwd(sv["rq"], sv["rk"], sv["rv"], dret, dec_f, dec_b)
    daq, dak, drq_p, drk_p, dqn, dkn = _prep_bwd(dq, dk, drq, drk, sv["aq"], sv["ak"], cos, sin, qn, kn)
    dproj = jnp.concatenate([daq, dak, dav, dag, drq_p, drk_p, drv, drg], axis=-1)
    dwin = _matmul(sv["h"].T, dproj, name="d_win", out_dtype=BF16)
    dh = _matmul(dproj, sv["win"], name="d_h", trans_b=True, tk=_tile(dproj.shape[1], 2816))
    g, gb, dnw = _rms_bwd(dh, sv["x"], g, nw)
    small = dict(norm=dnw[0], qn=dqn[0], kn=dkn[0], df=gf[:, 0, 0], db=gbk[:, 0, 0], rn=drn[0])
    return g, gb, dwin, dwout, small


def kernel(x, norm_w, w_in, q_norm, k_norm, ret_decay_fwd, ret_decay_bwd, ret_norm, w_out, final_norm, loss_target, m_norm_w, m_w_in, m_q_norm, m_k_norm, m_ret_decay_fwd, m_ret_decay_bwd, m_ret_norm, m_w_out, m_final_norm, v_norm_w, v_w_in, v_q_norm, v_k_norm, v_ret_decay_fwd, v_ret_decay_bwd, v_ret_norm, v_w_out, v_final_norm):
    depth, d, _ = w_in.shape
    seq = x.shape[1]
    rw = _section_widths(d)[6]
    rheads = rw // RET_V_DIM
    rns = ret_norm.shape[-1]
    _, _, _, me = _mesh_position()

    target = loss_target[0]
    cos, sin = _rope_tables(seq)

    rn_all, _ = _exchange_small(_pack([ret_norm]), name="gather_ret_norm")
    rn_full = rn_all.reshape(N_DEV, -1)[:, :depth * rheads * rns].reshape(N_DEV, depth, rheads, rns)
    rn_full = jnp.transpose(rn_full, (1, 2, 0, 3)).reshape(depth, rw)

    dec_f = jnp.broadcast_to(ret_decay_fwd[:, :, None, None], (depth, rheads, 1, LANES))
    dec_b = jnp.broadcast_to(ret_decay_bwd[:, :, None, None], (depth, rheads, 1, LANES))

    win_bf = w_in.astype(BF16)
    wout_bf = w_out.astype(BF16)

    saved = []
    xl = x[0]
    for l in range(depth):
        win_full, wout_full = _all_gather_weights(win_bf[l], wout_bf[l])
        xl, sv = _layer_fwd(xl, norm_w[l], win_full, wout_full, q_norm[l], k_norm[l], dec_f[l], dec_b[l],
                            rn_full[l], cos, sin)
        saved.append(sv)

    loss_row, g, gb, d_final = _loss_head(xl, target, final_norm)

    d_norm, d_qn, d_kn, d_df, d_db, d_rn = [], [], [], [], [], []
    big = []
    for l in reversed(range(depth)):
        g, gb, dwin, dwout, sm = _layer_bwd(g, gb, saved[l], norm_w[l], q_norm[l], k_norm[l], dec_f[l], dec_b[l],
                                            rn_full[l], cos, sin)
        land_in, land_out = _reduce_scatter_slabs(dwin, dwout)
        big.append((_adamw_slabs(w_in[l], m_w_in[l], v_w_in[l], land_in, tr=256, name="adamw_w_in"),
                    _adamw_slabs(w_out[l], m_w_out[l], v_w_out[l], land_out, tr=64, name="adamw_w_out")))
        d_norm.append(sm["norm"])
        d_qn.append(sm["qn"])
        d_kn.append(sm["kn"])
        d_df.append(sm["df"])
        d_db.append(sm["db"])
        d_rn.append(sm["rn"])
    for lst in (d_norm, d_qn, d_kn, d_df, d_db, d_rn, big):
        lst.reverse()

    small_shapes = [(depth, d), (depth, HEAD_DIM), (depth, HEAD_DIM), (depth, rheads), (depth, rheads),
                    (depth, rheads, N_DEV * rns), (d,), (1,)]
    grads_local = [jnp.stack(d_norm), jnp.stack(d_qn), jnp.stack(d_kn), jnp.stack(d_df), jnp.stack(d_db),
                   jnp.stack(d_rn).reshape(depth, rheads, N_DEV * rns), d_final[0], loss_row[0, :1]]
    _, gsum = _exchange_small(_pack(grads_local), name="all_reduce_small")
    g_norm, g_qn, g_kn, g_df, g_db, g_rn_full, g_final, loss = _unpack(gsum, small_shapes)
    g_rn = lax.dynamic_slice_in_dim(g_rn_full, me * rns, rns, axis=2)
    small_g = [g_norm, g_qn, g_kn, g_df, g_db, g_rn, g_final]
    small_w = [norm_w, q_norm, k_norm, ret_decay_fwd, ret_decay_bwd, ret_norm, final_norm]
    small_m = [m_norm_w, m_q_norm, m_k_norm, m_ret_decay_fwd, m_ret_decay_bwd, m_ret_norm, m_final_norm]
    small_v = [v_norm_w, v_q_norm, v_k_norm, v_ret_decay_fwd, v_ret_decay_bwd, v_ret_norm, v_final_norm]
    shapes = [a.shape for a in small_w]
    sd, sm, sv2 = _adamw_small(_pack(small_w), _pack(small_g), _pack(small_m), _pack(small_v))
    small_d, small_m2, small_v2 = _unpack(sd, shapes), _unpack(sm, shapes), _unpack(sv2, shapes)

    def stacked(which, idx):
        return jnp.stack([big[l][which][idx] for l in range(depth)])

    def ordered(small, win_v, wout_v):
        return [small[0], win_v, small[1], small[2], small[3], small[4], small[5], wout_v, small[6]]

    grads = ordered(small_g, stacked(0, 0), stacked(1, 0))
    deltas = ordered(small_d, stacked(0, 1), stacked(1, 1))
    new_m = ordered(small_m2, stacked(0, 2), stacked(1, 2))
    new_v = ordered(small_v2, stacked(0, 3), stacked(1, 3))
    return (loss.reshape(()), g[None], *grads, *deltas, *new_m, *new_v)
```

```python
import functools
import math

import jax
import jax.numpy as jnp
import numpy as np
from jax import lax
from jax.experimental import pallas as pl
from jax.experimental.pallas import tpu as pltpu

F32 = jnp.float32
BF16 = jnp.bfloat16

N_DEV = 8
HEAD_DIM = 128
ATTN_GROUP = 4
RET_QK_DIM = 128
RET_V_DIM = 256
GRID_W = 64
ROPE_THETA = 10000.0
EPS = 1e-6
ADAM_LR = 0.001
ADAM_B1 = 0.9
ADAM_B2 = 0.999
ADAM_EPS = 1e-08
ADAM_WD = 0.01
ADAM_STEP = 10
ADAM_C1 = 1.0 - ADAM_B1 ** ADAM_STEP
ADAM_C2 = 1.0 - ADAM_B2 ** ADAM_STEP
LANES = 128
SMALL_ROWS_ALIGN = 8
VMEM_LIMIT = 56 * 1024 * 1024

NT_DIMS = (((1,), (1,)), ((), ()))
TN_DIMS = (((0,), (0,)), ((), ()))
MESH = pl.DeviceIdType.MESH


def _params(sem):
    return pltpu.CompilerParams(dimension_semantics=sem, vmem_limit_bytes=VMEM_LIMIT)


def _tile(dim, pref, align=16):
    if dim <= pref:
        return dim
    for t in range(pref - pref % align, 0, -align):
        if dim % t == 0:
            return t
    raise ValueError((dim, pref, align))


def _silu_parts(z):
    sg = 1.0 / (1.0 + jnp.exp(-z))
    return z * sg, sg * (1.0 + z * (1.0 - sg))


def _log_sigmoid(x):
    return jnp.minimum(x, 0.0) - jnp.log(1.0 + jnp.exp(-jnp.abs(x)))


def _swap_pairs(z):
    lane = lax.broadcasted_iota(jnp.int32, z.shape, 1)
    return jnp.where((lane % 64) < 32, pltpu.roll(z, 96, 1), pltpu.roll(z, 32, 1))


def _rope(z, cos, sin):
    return z * cos + _swap_pairs(z) * sin


def _rope_transposed(d, cos, sin):
    return d * cos + _swap_pairs(d * sin)


def _rope_tables(seq):
    rows = seq // GRID_W
    row = jnp.repeat(jnp.arange(rows), GRID_W).astype(F32)
    col = jnp.tile(jnp.arange(GRID_W), rows).astype(F32)
    axis_dim = HEAD_DIM // 2
    inv = ROPE_THETA ** (-jnp.arange(0, axis_dim, 2, dtype=F32) / axis_dim)
    ar = row[:, None] * inv[None, :]
    ac = col[:, None] * inv[None, :]
    cos = jnp.concatenate([jnp.cos(ar), jnp.cos(ar), jnp.cos(ac), jnp.cos(ac)], axis=-1)
    sin = jnp.concatenate([-jnp.sin(ar), jnp.sin(ar), -jnp.sin(ac), jnp.sin(ac)], axis=-1)
    return cos, sin


def _matmul(a, b, *, name, trans_b=False, b_off=0, n=None, out_dtype=F32, residual=None,
            tm=1024, tn=512, tk=4096):
    m, k = a.shape
    if n is None:
        n = b.shape[0] if trans_b else b.shape[1]
    tm, tn, tk = _tile(m, tm), _tile(n, tn, LANES), _tile(k, tk, LANES)
    assert b_off % tn == 0
    joff = b_off // tn
    nk = k // tk
    has_res = residual is not None

    def body(*refs):
        if has_res:
            a_ref, b_ref, r_ref, o_ref = refs[:4]
        else:
            a_ref, b_ref, o_ref = refs[:3]
        if trans_b:
            part = lax.dot_general(a_ref[...], b_ref[...], NT_DIMS, preferred_element_type=F32)
        else:
            part = jnp.dot(a_ref[...], b_ref[...], preferred_element_type=F32)

        def finish(r):
            if has_res:
                r = r + r_ref[...]
            o_ref[...] = r.astype(o_ref.dtype)

        if nk == 1:
            finish(part)
        else:
            acc_ref = refs[-1]
            kk = pl.program_id(2)

            @pl.when(kk == 0)
            def _():
                acc_ref[...] = part

            @pl.when(kk > 0)
            def _():
                acc_ref[...] += part

            @pl.when(kk == nk - 1)
            def _():
                finish(acc_ref[...])

    if trans_b:
        b_spec = pl.BlockSpec((tn, tk), lambda i, j, kk: (j + joff, kk))
    else:
        b_spec = pl.BlockSpec((tk, tn), lambda i, j, kk: (kk, j + joff))
    in_specs = [pl.BlockSpec((tm, tk), lambda i, j, kk: (i, kk)), b_spec]
    args = [a, b]
    if has_res:
        in_specs.append(pl.BlockSpec((tm, tn), lambda i, j, kk: (i, j)))
        args.append(residual)
    return pl.pallas_call(
        body, name=name, grid=(m // tm, n // tn, nk),
        in_specs=in_specs,
        out_specs=pl.BlockSpec((tm, tn), lambda i, j, kk: (i, j)),
        out_shape=jax.ShapeDtypeStruct((m, n), out_dtype),
        scratch_shapes=[pltpu.VMEM((tm, tn), F32)] if nk > 1 else [],
        compiler_params=_params(("parallel", "parallel", "arbitrary")),
    )(*args)


def _rms_fwd(x, w, *, ts=256):
    s, d = x.shape
    ts = _tile(s, ts)

    def body(x_ref, w_ref, h_ref):
        xv = x_ref[...]
        r = lax.rsqrt(jnp.mean(xv * xv, axis=-1, keepdims=True) + EPS)
        h_ref[...] = (xv * r * w_ref[...]).astype(h_ref.dtype)

    row = pl.BlockSpec((ts, d), lambda i: (i, 0))
    return pl.pallas_call(
        body, name="rms_fwd", grid=(s // ts,),
        in_specs=[row, pl.BlockSpec((1, d), lambda i: (0, 0))],
        out_specs=row, out_shape=jax.ShapeDtypeStruct((s, d), BF16),
        compiler_params=_params(("parallel",)),
    )(x, w.reshape(1, d))


def _rms_bwd(dh, x, g, w, *, ts=256):
    s, d = x.shape
    ts = _tile(s, ts)

    def body(dh_ref, x_ref, g_ref, w_ref, dx_ref, dxb_ref, dw_ref):
        xv = x_ref[...]
        r = lax.rsqrt(jnp.mean(xv * xv, axis=-1, keepdims=True) + EPS)
        xh = xv * r
        dhv = dh_ref[...]
        dn = dhv * w_ref[...]
        dx = g_ref[...] + r * (dn - xh * jnp.mean(dn * xh, axis=-1, keepdims=True))
        dx_ref[...] = dx
        dxb_ref[...] = dx.astype(BF16)
        part = jnp.sum(dhv * xh, axis=0, keepdims=True)

        @pl.when(pl.program_id(0) == 0)
        def _():
            dw_ref[...] = part

        @pl.when(pl.program_id(0) > 0)
        def _():
            dw_ref[...] += part

    row = pl.BlockSpec((ts, d), lambda i: (i, 0))
    vec = pl.BlockSpec((1, d), lambda i: (0, 0))
    return pl.pallas_call(
        body, name="rms_bwd", grid=(s // ts,),
        in_specs=[row, row, row, vec],
        out_specs=[row, row, vec],
        out_shape=[jax.ShapeDtypeStruct((s, d), F32), jax.ShapeDtypeStruct((s, d), BF16),
                   jax.ShapeDtypeStruct((1, d), F32)],
        compiler_params=_params(("arbitrary",)),
    )(dh, x, g, w.reshape(1, d))


def _loss_head(x, target, w, *, ts=256):
    s, d = x.shape
    ts = _tile(s, ts)

    def body(x_ref, t_ref, w_ref, loss_ref, dx_ref, dxb_ref, dw_ref):
        xv = x_ref[...]
        r = lax.rsqrt(jnp.mean(xv * xv, axis=-1, keepdims=True) + EPS)
        xh = xv * r
        wv = w_ref[...]
        diff = xh * wv - t_ref[...]
        lpart = 0.5 * jnp.sum(jnp.mean(diff * diff, axis=-1, keepdims=True), axis=0, keepdims=True)
        dout = diff * (1.0 / d)
        dn = dout * wv
        dx = r * (dn - xh * jnp.mean(dn * xh, axis=-1, keepdims=True))
        dx_ref[...] = dx
        dxb_ref[...] = dx.astype(BF16)
        part = jnp.sum(dout * xh, axis=0, keepdims=True)
        lrow = jnp.broadcast_to(lpart, loss_ref.shape)

        @pl.when(pl.program_id(0) == 0)
        def _():
            dw_ref[...] = part
            loss_ref[...] = lrow

        @pl.when(pl.program_id(0) > 0)
        def _():
            dw_ref[...] += part
            loss_ref[...] += lrow

    row = pl.BlockSpec((ts, d), lambda i: (i, 0))
    vec = pl.BlockSpec((1, d), lambda i: (0, 0))
    return pl.pallas_call(
        body, name="loss_head", grid=(s // ts,),
        in_specs=[row, row, vec],
        out_specs=[pl.BlockSpec((1, LANES), lambda i: (0, 0)), row, row, vec],
        out_shape=[jax.ShapeDtypeStruct((1, LANES), F32), jax.ShapeDtypeStruct((s, d), F32),
                   jax.ShapeDtypeStruct((s, d), BF16), jax.ShapeDtypeStruct((1, d), F32)],
        compiler_params=_params(("arbitrary",)),
    )(x, target, w.reshape(1, d))


def _prep_fwd(aq, ak, av, rq, rk, rv, cos, sin, qw, kw, *, ts=256):
    s = aq.shape[0]
    ts = _tile(s, ts)
    attn_scale = HEAD_DIM ** -0.5
    ret_scale = RET_QK_DIM ** -0.5
    nq, nk, nr = aq.shape[1] // HEAD_DIM, ak.shape[1] // HEAD_DIM, rq.shape[1] // RET_QK_DIM

    def body(aq_ref, ak_ref, av_ref, rq_ref, rk_ref, rv_ref, cos_ref, sin_ref, qw_ref, kw_ref,
             q_out, k_out, v_out, rq_out, rk_out, rv_out):
        c, sn = cos_ref[...], sin_ref[...]

        def normed(u, w):
            return u * lax.rsqrt(jnp.mean(u * u, axis=-1, keepdims=True) + EPS) * w

        for j in range(nq):
            sl = slice(j * HEAD_DIM, (j + 1) * HEAD_DIM)
            q_out[:, sl] = (_rope(normed(aq_ref[:, sl], qw_ref[...]), c, sn) * attn_scale).astype(BF16)
        for j in range(nk):
            sl = slice(j * HEAD_DIM, (j + 1) * HEAD_DIM)
            k_out[:, sl] = _rope(normed(ak_ref[:, sl], kw_ref[...]), c, sn).astype(BF16)
        for j in range(nr):
            sl = slice(j * RET_QK_DIM, (j + 1) * RET_QK_DIM)
            rq_out[:, sl] = _rope(rq_ref[:, sl], c, sn).astype(BF16)
            rk_out[:, sl] = (_rope(rk_ref[:, sl], c, sn) * ret_scale).astype(BF16)
        v_out[...] = av_ref[...].astype(BF16)
        rv_out[...] = rv_ref[...].astype(BF16)

    def row(arr):
        return pl.BlockSpec((ts, arr.shape[1]), lambda i: (i, 0))

    vec = pl.BlockSpec((1, HEAD_DIM), lambda i: (0, 0))
    ins = [aq, ak, av, rq, rk, rv]
    return pl.pallas_call(
        body, name="prep_fwd", grid=(s // ts,),
        in_specs=[row(a) for a in ins] + [row(cos), row(sin), vec, vec],
        out_specs=[row(a) for a in ins],
        out_shape=[jax.ShapeDtypeStruct(a.shape, BF16) for a in ins],
        compiler_params=_params(("parallel",)),
    )(*ins, cos, sin, qw.reshape(1, HEAD_DIM), kw.reshape(1, HEAD_DIM))


def _prep_bwd(dq, dk, drq, drk, aq, ak, cos, sin, qw, kw, *, ts=256):
    s = aq.shape[0]
    ts = _tile(s, ts)
    attn_scale = HEAD_DIM ** -0.5
    ret_scale = RET_QK_DIM ** -0.5
    nq, nk, nr = aq.shape[1] // HEAD_DIM, ak.shape[1] // HEAD_DIM, drq.shape[1] // RET_QK_DIM

    def body(dq_ref, dk_ref, drq_ref, drk_ref, aq_ref, ak_ref, cos_ref, sin_ref, qw_ref, kw_ref,
             daq_out, dak_out, drq_out, drk_out, dqw_ref, dkw_ref):
        c, sn = cos_ref[...], sin_ref[...]

        def unrope(d):
            return _rope_transposed(d, c, sn)

        def norm_bwd(dun, u, w):
            r = lax.rsqrt(jnp.mean(u * u, axis=-1, keepdims=True) + EPS)
            uh = u * r
            dn = dun * w
            du = r * (dn - uh * jnp.mean(dn * uh, axis=-1, keepdims=True))
            return du, jnp.sum(dun * uh, axis=0, keepdims=True)

        dqw = jnp.zeros((1, HEAD_DIM), F32)
        for j in range(nq):
            sl = slice(j * HEAD_DIM, (j + 1) * HEAD_DIM)
            du, dw = norm_bwd(unrope(dq_ref[:, sl] * attn_scale), aq_ref[:, sl], qw_ref[...])
            daq_out[:, sl] = du.astype(BF16)
            dqw = dqw + dw
        dkw = jnp.zeros((1, HEAD_DIM), F32)
        for j in range(nk):
            sl = slice(j * HEAD_DIM, (j + 1) * HEAD_DIM)
            du, dw = norm_bwd(unrope(dk_ref[:, sl]), ak_ref[:, sl], kw_ref[...])
            dak_out[:, sl] = du.astype(BF16)
            dkw = dkw + dw
        for j in range(nr):
            sl = slice(j * RET_QK_DIM, (j + 1) * RET_QK_DIM)
            drq_out[:, sl] = unrope(drq_ref[:, sl]).astype(BF16)
            drk_out[:, sl] = unrope(drk_ref[:, sl] * ret_scale).astype(BF16)

        @pl.when(pl.program_id(0) == 0)
        def _():
            dqw_ref[...] = dqw
            dkw_ref[...] = dkw

        @pl.when(pl.program_id(0) > 0)
        def _():
            dqw_ref[...] += dqw
            dkw_ref[...] += dkw

    def row(arr):
        return pl.BlockSpec((ts, arr.shape[1]), lambda i: (i, 0))

    vec = pl.BlockSpec((1, HEAD_DIM), lambda i: (0, 0))
    ins = [dq, dk, drq, drk, aq, ak, cos, sin]
    outs = [dq, dk, drq, drk]
    return pl.pallas_call(
        body, name="prep_bwd", grid=(s // ts,),
        in_specs=[row(a) for a in ins] + [vec, vec],
        out_specs=[row(a) for a in outs] + [vec, vec],
        out_shape=[jax.ShapeDtypeStruct(a.shape, BF16) for a in outs]
        + [jax.ShapeDtypeStruct((1, HEAD_DIM), F32)] * 2,
        compiler_params=_params(("arbitrary",)),
    )(*ins, qw.reshape(1, HEAD_DIM), kw.reshape(1, HEAD_DIM))


def _attn_fwd(q, k, v, *, tq=256):
    s, aw = q.shape
    tq = _tile(s, tq)
    heads, kvh = aw // HEAD_DIM, k.shape[1] // HEAD_DIM
    grp = heads // kvh

    def body(q_ref, k_ref, v_ref, o_ref, lse_ref):
        sc = lax.dot_general(q_ref[...], k_ref[...], NT_DIMS, preferred_element_type=F32)
        m = jnp.max(sc, axis=-1, keepdims=True)
        p = jnp.exp(sc - m)
        l = jnp.sum(p, axis=-1, keepdims=True)
        o = jnp.dot(p.astype(BF16), v_ref[...], preferred_element_type=F32)
        o_ref[...] = (o / l).astype(o_ref.dtype)
        lse_ref[...] = jnp.broadcast_to(m + jnp.log(l), lse_ref.shape)

    qspec = pl.BlockSpec((tq, HEAD_DIM), lambda kv, g, i: (i, kv * grp + g))
    kspec = pl.BlockSpec((s, HEAD_DIM), lambda kv, g, i: (0, kv))
    return pl.pallas_call(
        body, name="attn_fwd", grid=(kvh, grp, s // tq),
        in_specs=[qspec, kspec, kspec],
        out_specs=[qspec, qspec],
        out_shape=[jax.ShapeDtypeStruct((s, aw), BF16), jax.ShapeDtypeStruct((s, aw), F32)],
        compiler_params=_params(("parallel", "parallel", "parallel")),
    )(q, k, v)


def _attn_bwd(q, k, v, o, do, lse, *, tq=256):
    s, aw = q.shape
    tq = _tile(s, tq)
    heads, kvh = aw // HEAD_DIM, k.shape[1] // HEAD_DIM
    grp = heads // kvh
    nq = s // tq

    def body(q_ref, k_ref, v_ref, o_ref, do_ref, lse_ref, dq_ref, dk_ref, dv_ref, dk_acc, dv_acc):
        g, i = pl.program_id(1), pl.program_id(2)
        qv, kv_, vv, dov = q_ref[...], k_ref[...], v_ref[...], do_ref[...]
        sc = lax.dot_general(qv, kv_, NT_DIMS, preferred_element_type=F32)
        p = jnp.exp(sc - lse_ref[:, :1])
        dp = lax.dot_general(dov, vv, NT_DIMS, preferred_element_type=F32)
        delta = jnp.sum(dov.astype(F32) * o_ref[...].astype(F32), axis=-1, keepdims=True)
        ds = (p * (dp - delta)).astype(BF16)
        dq_ref[...] = jnp.dot(ds, kv_, preferred_element_type=F32)
        dvp = lax.dot_general(p.astype(BF16), dov, TN_DIMS, preferred_element_type=F32)
        dkp = lax.dot_general(ds, qv, TN_DIMS, preferred_element_type=F32)
        first = jnp.logical_and(g == 0, i == 0)

        @pl.when(first)
        def _():
            dv_acc[...] = dvp
            dk_acc[...] = dkp

        @pl.when(jnp.logical_not(first))
        def _():
            dv_acc[...] += dvp
            dk_acc[...] += dkp

        @pl.when(jnp.logical_and(g == grp - 1, i == nq - 1))
        def _():
            dk_ref[...] = dk_acc[...]
            dv_ref[...] = dv_acc[...].astype(dv_ref.dtype)

    qspec = pl.BlockSpec((tq, HEAD_DIM), lambda kv, g, i: (i, kv * grp + g))
    kspec = pl.BlockSpec((s, HEAD_DIM), lambda kv, g, i: (0, kv))
    return pl.pallas_call(
        body, name="attn_bwd", grid=(kvh, grp, nq),
        in_specs=[qspec, kspec, kspec, qspec, qspec, qspec],
        out_specs=[qspec, kspec, kspec],
        out_shape=[jax.ShapeDtypeStruct((s, aw), F32), jax.ShapeDtypeStruct(k.shape, F32),
                   jax.ShapeDtypeStruct(v.shape, BF16)],
        scratch_shapes=[pltpu.VMEM((s, HEAD_DIM), F32), pltpu.VMEM((s, HEAD_DIM), F32)],
        compiler_params=_params(("parallel", "arbitrary", "arbitrary")),
    )(q, k, v, o, do, lse)


def _decay_matrix(i, tq, s, df_ref, db_ref):
    n = i * tq + lax.broadcasted_iota(jnp.int32, (tq, s), 0)
    m = lax.broadcasted_iota(jnp.int32, (tq, s), 1)
    diff = (n - m).astype(F32)
    lf = _log_sigmoid(df_ref[0][:, :1])
    lb = _log_sigmoid(db_ref[0][:, :1])
    dec = jnp.exp(diff * jnp.where(diff >= 0, lf, -lb))
    return dec, diff


def _ret_specs(s, tq):
    qspec = pl.BlockSpec((tq, RET_QK_DIM), lambda h, i: (i, h))
    kspec = pl.BlockSpec((s, RET_QK_DIM), lambda h, i: (0, h))
    vspec = pl.BlockSpec((s, RET_V_DIM), lambda h, i: (0, h))
    ospec = pl.BlockSpec((tq, RET_V_DIM), lambda h, i: (i, h))
    dspec = pl.BlockSpec((1, 1, LANES), lambda h, i: (h, 0, 0))
    return qspec, kspec, vspec, ospec, dspec


def _ret_fwd(q, k, v, dec_f, dec_b, *, tq=256):
    s, qw = q.shape
    tq = _tile(s, tq)
    heads = qw // RET_QK_DIM
    qspec, kspec, vspec, ospec, dspec = _ret_specs(s, tq)

    def body(q_ref, k_ref, v_ref, df_ref, db_ref, o_ref):
        dec, _ = _decay_matrix(pl.program_id(1), tq, s, df_ref, db_ref)
        sc = lax.dot_general(q_ref[...], k_ref[...], NT_DIMS, preferred_element_type=F32)
        o_ref[...] = jnp.dot((sc * dec).astype(BF16), v_ref[...], preferred_element_type=F32)

    return pl.pallas_call(
        body, name="ret_fwd", grid=(heads, s // tq),
        in_specs=[qspec, kspec, vspec, dspec, dspec],
        out_specs=ospec, out_shape=jax.ShapeDtypeStruct(v.shape, F32),
        compiler_params=_params(("parallel", "parallel")),
    )(q, k, v, dec_f, dec_b)


def _ret_bwd(q, k, v, do, dec_f, dec_b, *, tq=256):
    s, qw = q.shape
    tq = _tile(s, tq)
    heads = qw // RET_QK_DIM
    nq = s // tq
    qspec, kspec, vspec, ospec, dspec = _ret_specs(s, tq)
    gspec = pl.BlockSpec((1, 8, LANES), lambda h, i: (h, 0, 0))

    def body(q_ref, k_ref, v_ref, do_ref, df_ref, db_ref, dq_ref, dk_ref, dv_ref, gf_ref, gb_ref,
             dk_acc, dv_acc):
        i = pl.program_id(1)
        dec, diff = _decay_matrix(i, tq, s, df_ref, db_ref)
        qv, kv_, dov = q_ref[...], k_ref[...], do_ref[...]
        sc = lax.dot_general(qv, kv_, NT_DIMS, preferred_element_type=F32)
        p = sc * dec
        dp = lax.dot_general(dov, v_ref[...], NT_DIMS, preferred_element_type=F32)
        ds = (dp * dec).astype(BF16)
        dq_ref[...] = jnp.dot(ds, kv_, preferred_element_type=F32)
        dkp = lax.dot_general(ds, qv, TN_DIMS, preferred_element_type=F32)
        dvp = lax.dot_general(p.astype(BF16), dov, TN_DIMS, preferred_element_type=F32)
        t = dp * p * diff
        sf = 1.0 / (1.0 + jnp.exp(df_ref[0][:, :1]))
        sb = 1.0 / (1.0 + jnp.exp(db_ref[0][:, :1]))
        gf = jnp.sum(jnp.sum(jnp.where(diff > 0, t, 0.0), axis=0, keepdims=True), axis=1, keepdims=True) * sf
        gb = jnp.sum(jnp.sum(jnp.where(diff < 0, -t, 0.0), axis=0, keepdims=True), axis=1, keepdims=True) * sb
        gf = jnp.broadcast_to(gf.reshape(1, 1, 1), gf_ref.shape)
        gb = jnp.broadcast_to(gb.reshape(1, 1, 1), gb_ref.shape)

        @pl.when(i == 0)
        def _():
            dk_acc[...] = dkp
            dv_acc[...] = dvp
            gf_ref[...] = gf
            gb_ref[...] = gb

        @pl.when(i > 0)
        def _():
            dk_acc[...] += dkp
            dv_acc[...] += dvp
            gf_ref[...] += gf
            gb_ref[...] += gb

        @pl.when(i == nq - 1)
        def _():
            dk_ref[...] = dk_acc[...]
            dv_ref[...] = dv_acc[...].astype(dv_ref.dtype)

    return pl.pallas_call(
        body, name="ret_bwd", grid=(heads, nq),
        in_specs=[qspec, kspec, vspec, ospec, dspec, dspec],
        out_specs=[qspec, kspec, vspec, gspec, gspec],
        out_shape=[jax.ShapeDtypeStruct(q.shape, F32), jax.ShapeDtypeStruct(k.shape, F32),
                   jax.ShapeDtypeStruct(v.shape, BF16),
                   jax.ShapeDtypeStruct((heads, 8, LANES), F32), jax.ShapeDtypeStruct((heads, 8, LANES), F32)],
        scratch_shapes=[pltpu.VMEM((s, RET_QK_DIM), F32), pltpu.VMEM((s, RET_V_DIM), F32)],
        compiler_params=_params(("parallel", "arbitrary")),
    )(q, k, v, do, dec_f, dec_b)


def _gate_fwd(att, ag, ret, rg, rnw, *, ts=256):
    s, aw = att.shape
    rw = ret.shape[1]
    ts = _tile(s, ts)
    rheads = rw // RET_V_DIM

    def body(att_ref, ag_ref, ret_ref, rg_ref, w_ref, y_ref):
        sa, _ = _silu_parts(ag_ref[...])
        y_ref[:, :aw] = (sa * att_ref[...].astype(F32)).astype(BF16)
        for h in range(rheads):
            sl = slice(h * RET_V_DIM, (h + 1) * RET_V_DIM)
            rt = ret_ref[:, sl]
            rn = rt * lax.rsqrt(jnp.mean(rt * rt, axis=-1, keepdims=True) + EPS) * w_ref[:, sl]
            sr, _ = _silu_parts(rg_ref[:, sl])
            y_ref[:, aw + h * RET_V_DIM:aw + (h + 1) * RET_V_DIM] = (sr * rn).astype(BF16)

    def row(w):
        return pl.BlockSpec((ts, w), lambda i: (i, 0))

    return pl.pallas_call(
        body, name="gate_fwd", grid=(s // ts,),
        in_specs=[row(aw), row(aw), row(rw), row(rw), pl.BlockSpec((1, rw), lambda i: (0, 0))],
        out_specs=row(aw + rw), out_shape=jax.ShapeDtypeStruct((s, aw + rw), BF16),
        compiler_params=_params(("parallel",)),
    )(att, ag, ret, rg, rnw.reshape(1, rw))


def _gate_bwd(dy, att, ag, ret, rg, rnw, *, ts=256):
    s, aw = att.shape
    rw = ret.shape[1]
    ts = _tile(s, ts)
    rheads = rw // RET_V_DIM

    def body(dy_ref, att_ref, ag_ref, ret_ref, rg_ref, w_ref, datt_ref, dag_ref, dret_ref, drg_ref, dw_ref):
        sa, dsa = _silu_parts(ag_ref[...])
        dya = dy_ref[:, :aw]
        datt_ref[...] = (dya * sa).astype(BF16)
        dag_ref[...] = (dya * att_ref[...].astype(F32) * dsa).astype(BF16)
        parts = []
        for h in range(rheads):
            sl = slice(h * RET_V_DIM, (h + 1) * RET_V_DIM)
            rt = ret_ref[:, sl]
            rr = lax.rsqrt(jnp.mean(rt * rt, axis=-1, keepdims=True) + EPS)
            rh = rt * rr
            wv = w_ref[:, sl]
            sr, dsr = _silu_parts(rg_ref[:, sl])
            dyr = dy_ref[:, aw + h * RET_V_DIM:aw + (h + 1) * RET_V_DIM]
            drg_ref[:, sl] = (dyr * rh * wv * dsr).astype(BF16)
            drn = dyr * sr
            dn = drn * wv
            dret_ref[:, sl] = (rr * (dn - rh * jnp.mean(dn * rh, axis=-1, keepdims=True))).astype(BF16)
            parts.append(jnp.sum(drn * rh, axis=0, keepdims=True))
        part = jnp.concatenate(parts, axis=-1)

        @pl.when(pl.program_id(0) == 0)
        def _():
            dw_ref[...] = part

        @pl.when(pl.program_id(0) > 0)
        def _():
            dw_ref[...] += part

    def row(w):
        return pl.BlockSpec((ts, w), lambda i: (i, 0))

    vec = pl.BlockSpec((1, rw), lambda i: (0, 0))
    return pl.pallas_call(
        body, name="gate_bwd", grid=(s // ts,),
        in_specs=[row(aw + rw), row(aw), row(aw), row(rw), row(rw), vec],
        out_specs=[row(aw), row(aw), row(rw), row(rw), vec],
        out_shape=[jax.ShapeDtypeStruct((s, aw), BF16), jax.ShapeDtypeStruct((s, aw), BF16),
                   jax.ShapeDtypeStruct((s, rw), BF16), jax.ShapeDtypeStruct((s, rw), BF16),
                   jax.ShapeDtypeStruct((1, rw), F32)],
        compiler_params=_params(("arbitrary",)),
    )(dy, att, ag, ret, rg, rnw.reshape(1, rw))


def _mesh_position():
    x, y, c = lax.axis_index("x"), lax.axis_index("y"), lax.axis_index("c")
    return x, y, c, 4 * x + 2 * y + c


def _peer(x, y, c, k):
    px = 1 - x if k & 4 else x
    py = 1 - y if k & 2 else y
    pc = 1 - c if k & 1 else c
    return (px, py, pc), 4 * px + 2 * py + pc


HBM_SPEC = pl.BlockSpec(memory_space=pltpu.HBM)
SEM_SPEC = pl.BlockSpec(memory_space=pltpu.SEMAPHORE)
ANY_SPEC = pl.BlockSpec(memory_space=pl.ANY)
DATAFLOW = pltpu.SideEffectType.DATAFLOW_SIDE_EFFECTING
N_SPLIT_COPIES = 2 * (N_DEV - 1)


def _hbm(a):
    return pltpu.with_memory_space_constraint(a, pltpu.HBM)


def _split_start(name, copies, bufs, after):
    def body(*refs):
        ins, (send_sems, recv_sems), token = refs[:4], refs[4 + len(after):6 + len(after)], refs[-1]
        sends, _ = copies(ins, send_sems, recv_sems)
        for cp in sends:
            cp.start()
        token[...] = jnp.zeros_like(token)

    return pl.pallas_call(
        body, name=name,
        out_shape=(pltpu.SemaphoreType.DMA((N_SPLIT_COPIES,)), pltpu.SemaphoreType.DMA((N_SPLIT_COPIES,)),
                   *[pltpu.HBM(b.shape, b.dtype) for b in bufs], jax.ShapeDtypeStruct((8, LANES), F32)),
        in_specs=[HBM_SPEC] * 4 + [ANY_SPEC] * len(after),
        out_specs=(SEM_SPEC, SEM_SPEC, HBM_SPEC, HBM_SPEC, HBM_SPEC, HBM_SPEC, pl.BlockSpec(memory_space=pltpu.VMEM)),
        input_output_aliases={0: 2, 1: 3, 2: 4, 3: 5},
        compiler_params=pltpu.CompilerParams(has_side_effects=DATAFLOW),
    )(*[_hbm(b) for b in bufs], *after)


def _split_wait(name, copies, started, after):
    send_sems, recv_sems, *bufs = started[:6]

    def body(*refs):
        ins, send_ref, recv_ref = refs[:4], refs[4], refs[5]
        sends, recvs = copies(ins, send_ref, recv_ref)
        for cp in sends:
            cp.wait_send()
        for cp in recvs:
            cp.wait_recv()

    return pl.pallas_call(
        body, name=name,
        out_shape=tuple(pltpu.HBM(b.shape, b.dtype) for b in bufs),
        in_specs=[HBM_SPEC] * 4 + [SEM_SPEC, SEM_SPEC] + [ANY_SPEC] * len(after),
        out_specs=(HBM_SPEC,) * 4,
        input_output_aliases={0: 0, 1: 1, 2: 2, 3: 3},
        compiler_params=pltpu.CompilerParams(has_side_effects=DATAFLOW),
    )(*bufs, send_sems, recv_sems, *after)


def _gather_copies(iws, mws):
    def copies(refs, send_sems, recv_sems):
        win_ref, wout_ref, win_full, wout_full = refs
        x, y, c, me = _mesh_position()

        def cols(p):
            return win_full.at[:, pl.ds(pl.multiple_of(p * iws, LANES), iws)]

        def rows(p):
            return wout_full.at[pl.ds(pl.multiple_of(p * mws, 16), mws), :]

        sends, recvs = [], []
        for k in range(1, N_DEV):
            peer, pid = _peer(x, y, c, k)
            for a, (src, mine, theirs) in enumerate(((win_ref, cols(me), cols(pid)), (wout_ref, rows(me), rows(pid)))):
                sem = 2 * (k - 1) + a
                sends.append(pltpu.make_async_remote_copy(
                    src_ref=src, dst_ref=mine, send_sem=send_sems.at[sem], recv_sem=recv_sems.at[sem],
                    device_id=peer, device_id_type=MESH))
                recvs.append(pltpu.make_async_remote_copy(
                    src_ref=src, dst_ref=theirs, send_sem=send_sems.at[sem], recv_sem=recv_sems.at[sem],
                    device_id=peer, device_id_type=MESH))
        return sends, recvs

    return copies


def _gather_start(win_sh, wout_sh, me, after, tag):
    d, iws = win_sh.shape
    mws = wout_sh.shape[0]
    win_full = lax.dynamic_update_slice(lax.empty((d, N_DEV * iws), win_sh.dtype), win_sh, (0, me * iws))
    wout_full = lax.dynamic_update_slice(lax.empty((N_DEV * mws, d), wout_sh.dtype), wout_sh, (me * mws, 0))
    return _split_start("gather_start_" + tag, _gather_copies(iws, mws), [win_sh, wout_sh, win_full, wout_full], after)


def _gather_wait(started, after, tag):
    iws, mws = started[2].shape[1], started[3].shape[0]
    out = _split_wait("gather_wait_" + tag, _gather_copies(iws, mws), started, after)
    return out[2], out[3]


def _scatter_copies(iws, mws):
    def copies(refs, send_sems, recv_sems):
        dwin_ref, dwout_ref, land_in, land_out = refs
        x, y, c, me = _mesh_position()

        def cols(p):
            return dwin_ref.at[:, pl.ds(pl.multiple_of(p * iws, LANES), iws)]

        def rows(p):
            return dwout_ref.at[pl.ds(pl.multiple_of(p * mws, 16), mws), :]

        sends, recvs = [], []
        for k in range(1, N_DEV):
            peer, pid = _peer(x, y, c, k)
            for a, (src, land) in enumerate(((cols(pid), land_in), (rows(pid), land_out))):
                sem = 2 * (k - 1) + a
                sends.append(pltpu.make_async_remote_copy(
                    src_ref=src, dst_ref=land.at[me], send_sem=send_sems.at[sem], recv_sem=recv_sems.at[sem],
                    device_id=peer, device_id_type=MESH))
                recvs.append(pltpu.make_async_remote_copy(
                    src_ref=src, dst_ref=land.at[pid], send_sem=send_sems.at[sem], recv_sem=recv_sems.at[sem],
                    device_id=peer, device_id_type=MESH))
        return sends, recvs

    return copies


def _scatter_start(dwin, dwout, me, tag):
    d, iw = dwin.shape
    mw = dwout.shape[0]
    iws, mws = iw // N_DEV, mw // N_DEV
    own_in = lax.dynamic_slice(dwin, (0, me * iws), (d, iws))
    own_out = lax.dynamic_slice(dwout, (me * mws, 0), (mws, d))
    land_in = lax.dynamic_update_slice(lax.empty((N_DEV, d, iws), dwin.dtype), own_in[None], (me, 0, 0))
    land_out = lax.dynamic_update_slice(lax.empty((N_DEV, mws, d), dwout.dtype), own_out[None], (me, 0, 0))
    return _split_start("scatter_start_" + tag, _scatter_copies(iws, mws), [dwin, dwout, land_in, land_out], [])


def _scatter_wait(started, after, tag):
    iws, mws = started[4].shape[2], started[5].shape[1]
    out = _split_wait("scatter_wait_" + tag, _scatter_copies(iws, mws), started, after)
    return out[2], out[3]


def _exchange_small(buf, *, name):
    r = buf.shape[0]

    def body(buf_ref, all_ref, sum_ref, send_sems, recv_sems):
        x, y, c, me = _mesh_position()
        all_ref[me] = buf_ref[...]
        sends, recvs = [], []
        for k in range(1, N_DEV):
            peer, pid = _peer(x, y, c, k)
            sends.append(pltpu.make_async_remote_copy(
                src_ref=buf_ref, dst_ref=all_ref.at[me], send_sem=send_sems.at[k - 1], recv_sem=recv_sems.at[k - 1],
                device_id=peer, device_id_type=MESH))
            recvs.append(pltpu.make_async_remote_copy(
                src_ref=buf_ref, dst_ref=all_ref.at[pid], send_sem=send_sems.at[k - 1], recv_sem=recv_sems.at[k - 1],
                device_id=peer, device_id_type=MESH))
        for cp in sends:
            cp.start()
        for cp in recvs:
            cp.wait_recv()
        for cp in sends:
            cp.wait_send()
        total = all_ref[0]
        for p in range(1, N_DEV):
            total = total + all_ref[p]
        sum_ref[...] = total

    vmem = pl.BlockSpec(memory_space=pltpu.VMEM)
    return pl.pallas_call(
        body, name=name,
        in_specs=[vmem], out_specs=[vmem, vmem],
        out_shape=[jax.ShapeDtypeStruct((N_DEV, r, LANES), F32), jax.ShapeDtypeStruct((r, LANES), F32)],
        scratch_shapes=[pltpu.SemaphoreType.DMA((N_DEV - 1,)), pltpu.SemaphoreType.DMA((N_DEV - 1,))],
        compiler_params=pltpu.CompilerParams(has_side_effects=True),
    )(buf)


def _adamw_math(w, g, m, v):
    m2 = ADAM_B1 * m + (1.0 - ADAM_B1) * g
    v2 = ADAM_B2 * v + (1.0 - ADAM_B2) * (g * g)
    delta = -ADAM_LR * ((m2 / ADAM_C1) / (jnp.sqrt(v2 / ADAM_C2) + ADAM_EPS) + ADAM_WD * w)
    return delta, m2, v2


def _adamw_slabs(layer, w, m, v, land, outs, order, *, tr, name):
    depth, r, c = w.shape
    tr = _tile(r, tr)

    def body(w_ref, m_ref, v_ref, land_ref, order_ref, o0, o1, o2, o3, g_ref, d_ref, m2_ref, v2_ref):
        g = land_ref[0].astype(F32)
        for p in range(1, N_DEV):
            g = g + land_ref[p].astype(F32)
        delta, m2, v2 = _adamw_math(w_ref[...], g, m_ref[...], v_ref[...])
        g_ref[...] = g
        d_ref[...] = delta
        m2_ref[...] = m2
        v2_ref[...] = v2

    row = pl.BlockSpec((None, tr, c), lambda i: (layer, i, 0))
    return pl.pallas_call(
        body, name=name, grid=(r // tr,),
        in_specs=[row, row, row, pl.BlockSpec((N_DEV, tr, c), lambda i: (0, i, 0)),
                  pl.BlockSpec((8, LANES), lambda i: (0, 0))] + [ANY_SPEC] * 4,
        out_specs=[row] * 4, out_shape=[jax.ShapeDtypeStruct((depth, r, c), F32)] * 4,
        input_output_aliases={5: 0, 6: 1, 7: 2, 8: 3},
        compiler_params=_params(("parallel",)),
    )(w, m, v, land, order, *outs)


def _adamw_small(w, g, m, v):
    def body(w_ref, g_ref, m_ref, v_ref, d_ref, m2_ref, v2_ref):
        delta, m2, v2 = _adamw_math(w_ref[...], g_ref[...], m_ref[...], v_ref[...])
        d_ref[...] = delta
        m2_ref[...] = m2
        v2_ref[...] = v2

    vmem = pl.BlockSpec(memory_space=pltpu.VMEM)
    return pl.pallas_call(
        body, name="adamw_small", in_specs=[vmem] * 4, out_specs=[vmem] * 3,
        out_shape=[jax.ShapeDtypeStruct(w.shape, F32)] * 3,
    )(w, g, m, v)


def _pack(parts):
    flat = jnp.concatenate([p.reshape(-1).astype(F32) for p in parts])
    rows = -(-flat.shape[0] // LANES)
    rows = -(-rows // SMALL_ROWS_ALIGN) * SMALL_ROWS_ALIGN
    flat = jnp.pad(flat, (0, rows * LANES - flat.shape[0]))
    return flat.reshape(rows, LANES)


def _unpack(buf, shapes):
    flat = buf.reshape(-1)
    out, pos = [], 0
    for shp in shapes:
        size = math.prod(shp)
        out.append(flat[pos:pos + size].reshape(shp))
        pos += size
    return out


def _section_widths(d):
    aw = d // 2
    kw = aw // ATTN_GROUP
    rw = d - aw
    rqw = (rw // RET_V_DIM) * RET_QK_DIM
    return (aw, kw, kw, aw, rqw, rqw, rw, rw)


def _layer_fwd(xl, nw, win_full, wout_full, qn, kn, dec_f, dec_b, rn, cos, sin):
    widths = _section_widths(xl.shape[1])
    offs = tuple(int(o) for o in np.cumsum((0,) + widths)[:-1])
    sec_tn = _tile(widths[1], 512)
    h = _rms_fwd(xl, nw)
    secs = [_matmul(h, win_full, name="proj", b_off=offs[i], n=widths[i], tn=sec_tn) for i in range(8)]
    aq, ak, av, ag, rq, rk, rv, rg = secs
    q, k, v, rqr, rkr, rvb = _prep_fwd(aq, ak, av, rq, rk, rv, cos, sin, qn, kn)
    att, lse = _attn_fwd(q, k, v)
    ret = _ret_fwd(rqr, rkr, rvb, dec_f, dec_b)
    y = _gate_fwd(att, ag, ret, rg, rn)
    xn = _matmul(y, wout_full, name="out_proj", residual=xl)
    saved = dict(x=xl, h=h, aq=aq, ak=ak, ag=ag, rg=rg, q=q, k=k, v=v, rq=rqr, rk=rkr, rv=rvb,
                 att=att, lse=lse, ret=ret, y=y, win=win_full, wout=wout_full)
    return xn, saved


def _layer_bwd_weights(gb, sv, qn, kn, dec_f, dec_b, rn, cos, sin):
    dy = _matmul(gb, sv["wout"], name="d_y", trans_b=True)
    dwout = _matmul(sv["y"].T, gb, name="d_wout", out_dtype=BF16)
    datt, dag, dret, drg, drn = _gate_bwd(dy, sv["att"], sv["ag"], sv["ret"], sv["rg"], rn)
    dq, dk, dav = _attn_bwd(sv["q"], sv["k"], sv["v"], sv["att"], datt, sv["lse"])
    drq, drk, drv, gf, gbk = _ret_bwd(sv["rq"], sv["rk"], sv["rv"], dret, dec_f, dec_b)
    daq, dak, drq_p, drk_p, dqn, dkn = _prep_bwd(dq, dk, drq, drk, sv["aq"], sv["ak"], cos, sin, qn, kn)
    dproj = jnp.concatenate([daq, dak, dav, dag, drq_p, drk_p, drv, drg], axis=-1)
    dwin = _matmul(sv["h"].T, dproj, name="d_win", out_dtype=BF16)
    small = dict(qn=dqn[0], kn=dkn[0], df=gf[:, 0, 0], db=gbk[:, 0, 0], rn=drn[0])
    return dproj, dwin, dwout, small


def _layer_bwd_input(g, dproj, sv, nw):
    dh = _matmul(dproj, sv["win"], name="d_h", trans_b=True, tk=_tile(dproj.shape[1], 2816, LANES))
    g, gb, dnw = _rms_bwd(dh, sv["x"], g, nw)
    return g, gb, dnw[0]


def kernel(x, norm_w, w_in, q_norm, k_norm, ret_decay_fwd, ret_decay_bwd, ret_norm, w_out, final_norm, loss_target, m_norm_w, m_w_in, m_q_norm, m_k_norm, m_ret_decay_fwd, m_ret_decay_bwd, m_ret_norm, m_w_out, m_final_norm, v_norm_w, v_w_in, v_q_norm, v_k_norm, v_ret_decay_fwd, v_ret_decay_bwd, v_ret_norm, v_w_out, v_final_norm):
    depth, d, _ = w_in.shape
    seq = x.shape[1]
    rw = _section_widths(d)[6]
    rheads = rw // RET_V_DIM
    rns = ret_norm.shape[-1]
    _, _, _, me = _mesh_position()

    target = loss_target[0]
    cos, sin = _rope_tables(seq)

    rn_all, _ = _exchange_small(_pack([ret_norm]), name="gather_ret_norm")
    rn_full = rn_all.reshape(N_DEV, -1)[:, :depth * rheads * rns].reshape(N_DEV, depth, rheads, rns)
    rn_full = jnp.transpose(rn_full, (1, 2, 0, 3)).reshape(depth, rw)

    dec_f = jnp.broadcast_to(ret_decay_fwd[:, :, None, None], (depth, rheads, 1, LANES))
    dec_b = jnp.broadcast_to(ret_decay_bwd[:, :, None, None], (depth, rheads, 1, LANES))

    win_bf = w_in.astype(BF16)
    wout_bf = w_out.astype(BF16)

    saved = []
    xl = x[0]
    full = _gather_wait(_gather_start(win_bf[0], wout_bf[0], me, [], "0"), [], "0")
    for l in range(depth):
        nw = norm_w[l]
        if l + 1 < depth:
            nxt = _gather_start(win_bf[l + 1], wout_bf[l + 1], me, [full[0]], str(l + 1))
            nw = nw + nxt[-1][0, 0]
        xl, sv = _layer_fwd(xl, nw, full[0], full[1], q_norm[l], k_norm[l], dec_f[l], dec_b[l],
                            rn_full[l], cos, sin)
        saved.append(sv)
        if l + 1 < depth:
            full = _gather_wait(nxt, [xl], str(l + 1))

    loss_row, g, gb, d_final = _loss_head(xl, target, final_norm)

    d_norm, d_qn, d_kn, d_df, d_db, d_rn = [], [], [], [], [], []
    lands = [None] * depth
    pending = None
    for l in reversed(range(depth)):
        dproj, dwin, dwout, sm = _layer_bwd_weights(gb, saved[l], q_norm[l], k_norm[l], dec_f[l], dec_b[l],
                                                    rn_full[l], cos, sin)
        started = _scatter_start(dwin, dwout, me, str(l))
        g, gb, dnw = _layer_bwd_input(g, dproj, saved[l], norm_w[l] + started[-1][0, 0])
        if pending is not None:
            lands[l + 1] = _scatter_wait(pending, [g], str(l + 1))
        pending = started
        d_norm.append(dnw)
        d_qn.append(sm["qn"])
        d_kn.append(sm["kn"])
        d_df.append(sm["df"])
        d_db.append(sm["db"])
        d_rn.append(sm["rn"])
    for lst in (d_norm, d_qn, d_kn, d_df, d_db, d_rn):
        lst.reverse()
    order = pending[-1]
    in_outs = [lax.empty(w_in.shape, F32) for _ in range(4)]
    out_outs = [lax.empty(w_out.shape, F32) for _ in range(4)]
    for l in reversed(range(depth)):
        if l == 0:
            lands[0] = _scatter_wait(pending, [g, in_outs[0], out_outs[0]], "0")
        in_outs = _adamw_slabs(l, w_in, m_w_in, v_w_in, lands[l][0], in_outs, order, tr=256, name="adamw_w_in")
        out_outs = _adamw_slabs(l, w_out, m_w_out, v_w_out, lands[l][1], out_outs, order, tr=64, name="adamw_w_out")

    small_shapes = [(depth, d), (depth, HEAD_DIM), (depth, HEAD_DIM), (depth, rheads), (depth, rheads),
                    (depth, rheads, N_DEV * rns), (d,), (1,)]
    grads_local = [jnp.stack(d_norm), jnp.stack(d_qn), jnp.stack(d_kn), jnp.stack(d_df), jnp.stack(d_db),
                   jnp.stack(d_rn).reshape(depth, rheads, N_DEV * rns), d_final[0], loss_row[0, :1]]
    _, gsum = _exchange_small(_pack(grads_local), name="all_reduce_small")
    g_norm, g_qn, g_kn, g_df, g_db, g_rn_full, g_final, loss = _unpack(gsum, small_shapes)
    g_rn = lax.dynamic_slice_in_dim(g_rn_full, me * rns, rns, axis=2)
    small_g = [g_norm, g_qn, g_kn, g_df, g_db, g_rn, g_final]
    small_w = [norm_w, q_norm, k_norm, ret_decay_fwd, ret_decay_bwd, ret_norm, final_norm]
    small_m = [m_norm_w, m_q_norm, m_k_norm, m_ret_decay_fwd, m_ret_decay_bwd, m_ret_norm, m_final_norm]
    small_v = [v_norm_w, v_q_norm, v_k_norm, v_ret_decay_fwd, v_ret_decay_bwd, v_ret_norm, v_final_norm]
    shapes = [a.shape for a in small_w]
    sd, sm, sv2 = _adamw_small(_pack(small_w), _pack(small_g), _pack(small_m), _pack(small_v))
    small_d, small_m2, small_v2 = _unpack(sd, shapes), _unpack(sm, shapes), _unpack(sv2, shapes)

    def ordered(small, win_v, wout_v):
        return [small[0], win_v, small[1], small[2], small[3], small[4], small[5], wout_v, small[6]]

    grads = ordered(small_g, in_outs[0], out_outs[0])
    deltas = ordered(small_d, in_outs[1], out_outs[1])
    new_m = ordered(small_m2, in_outs[2], out_outs[2])
    new_v = ordered(small_v2, in_outs[3], out_outs[3])
    return (loss.reshape(()), g[None], *grads, *deltas, *new_m, *new_v)
```

```python
import functools
import math

import jax
import jax.numpy as jnp
import numpy as np
from jax import lax
from jax.experimental import pallas as pl
from jax.experimental.pallas import tpu as pltpu

F32 = jnp.float32
BF16 = jnp.bfloat16

N_DEV = 8
HEAD_DIM = 128
ATTN_GROUP = 4
RET_QK_DIM = 128
RET_V_DIM = 256
GRID_W = 64
ROPE_THETA = 10000.0
EPS = 1e-6
ADAM_LR = 0.001
ADAM_B1 = 0.9
ADAM_B2 = 0.999
ADAM_EPS = 1e-08
ADAM_WD = 0.01
ADAM_STEP = 10
ADAM_C1 = 1.0 - ADAM_B1 ** ADAM_STEP
ADAM_C2 = 1.0 - ADAM_B2 ** ADAM_STEP
LANES = 128
SMALL_ROWS_ALIGN = 8
VMEM_LIMIT = 56 * 1024 * 1024

NT_DIMS = (((1,), (1,)), ((), ()))
TN_DIMS = (((0,), (0,)), ((), ()))
MESH = pl.DeviceIdType.MESH


def _params(sem):
    return pltpu.CompilerParams(dimension_semantics=sem, vmem_limit_bytes=VMEM_LIMIT)


def _tile(dim, pref, align=16):
    if dim <= pref:
        return dim
    for t in range(pref - pref % align, 0, -align):
        if dim % t == 0:
            return t
    raise ValueError((dim, pref, align))


def _silu_parts(z):
    sg = 1.0 / (1.0 + jnp.exp(-z))
    return z * sg, sg * (1.0 + z * (1.0 - sg))


def _log_sigmoid(x):
    return jnp.minimum(x, 0.0) - jnp.log(1.0 + jnp.exp(-jnp.abs(x)))


def _swap_pairs(z):
    lane = lax.broadcasted_iota(jnp.int32, z.shape, 1)
    return jnp.where((lane % 64) < 32, pltpu.roll(z, 96, 1), pltpu.roll(z, 32, 1))


def _rope(z, cos, sin):
    return z * cos + _swap_pairs(z) * sin


def _rope_transposed(d, cos, sin):
    return d * cos + _swap_pairs(d * sin)


def _rope_tables(seq):
    rows = seq // GRID_W
    row = jnp.repeat(jnp.arange(rows), GRID_W).astype(F32)
    col = jnp.tile(jnp.arange(GRID_W), rows).astype(F32)
    axis_dim = HEAD_DIM // 2
    inv = ROPE_THETA ** (-jnp.arange(0, axis_dim, 2, dtype=F32) / axis_dim)
    ar = row[:, None] * inv[None, :]
    ac = col[:, None] * inv[None, :]
    cos = jnp.concatenate([jnp.cos(ar), jnp.cos(ar), jnp.cos(ac), jnp.cos(ac)], axis=-1)
    sin = jnp.concatenate([-jnp.sin(ar), jnp.sin(ar), -jnp.sin(ac), jnp.sin(ac)], axis=-1)
    return cos, sin


def _matmul(a, b, *, name, trans_b=False, b_off=0, n=None, out_dtype=F32, residual=None,
            tm=1024, tn=512, tk=4096):
    m, k = a.shape
    if n is None:
        n = b.shape[0] if trans_b else b.shape[1]
    tm, tn, tk = _tile(m, tm), _tile(n, tn, LANES), _tile(k, tk, LANES)
    assert b_off % tn == 0
    joff = b_off // tn
    nk = k // tk
    has_res = residual is not None

    def body(*refs):
        if has_res:
            a_ref, b_ref, r_ref, o_ref = refs[:4]
        else:
            a_ref, b_ref, o_ref = refs[:3]
        if trans_b:
            part = lax.dot_general(a_ref[...], b_ref[...], NT_DIMS, preferred_element_type=F32)
        else:
            part = jnp.dot(a_ref[...], b_ref[...], preferred_element_type=F32)

        def finish(r):
            if has_res:
                r = r + r_ref[...]
            o_ref[...] = r.astype(o_ref.dtype)

        if nk == 1:
            finish(part)
        else:
            acc_ref = refs[-1]
            kk = pl.program_id(2)

            @pl.when(kk == 0)
            def _():
                acc_ref[...] = part

            @pl.when(kk > 0)
            def _():
                acc_ref[...] += part

            @pl.when(kk == nk - 1)
            def _():
                finish(acc_ref[...])

    if trans_b:
        b_spec = pl.BlockSpec((tn, tk), lambda i, j, kk: (j + joff, kk))
    else:
        b_spec = pl.BlockSpec((tk, tn), lambda i, j, kk: (kk, j + joff))
    in_specs = [pl.BlockSpec((tm, tk), lambda i, j, kk: (i, kk)), b_spec]
    args = [a, b]
    if has_res:
        in_specs.append(pl.BlockSpec((tm, tn), lambda i, j, kk: (i, j)))
        args.append(residual)
    return pl.pallas_call(
        body, name=name, grid=(m // tm, n // tn, nk),
        in_specs=in_specs,
        out_specs=pl.BlockSpec((tm, tn), lambda i, j, kk: (i, j)),
        out_shape=jax.ShapeDtypeStruct((m, n), out_dtype),
        scratch_shapes=[pltpu.VMEM((tm, tn), F32)] if nk > 1 else [],
        compiler_params=_params(("parallel", "parallel", "arbitrary")),
    )(*args)


def _rms_fwd(x, w, *, ts=256):
    s, d = x.shape
    ts = _tile(s, ts)

    def body(x_ref, w_ref, h_ref):
        xv = x_ref[...]
        r = lax.rsqrt(jnp.mean(xv * xv, axis=-1, keepdims=True) + EPS)
        h_ref[...] = (xv * r * w_ref[...]).astype(h_ref.dtype)

    row = pl.BlockSpec((ts, d), lambda i: (i, 0))
    return pl.pallas_call(
        body, name="rms_fwd", grid=(s // ts,),
        in_specs=[row, pl.BlockSpec((1, d), lambda i: (0, 0))],
        out_specs=row, out_shape=jax.ShapeDtypeStruct((s, d), BF16),
        compiler_params=_params(("parallel",)),
    )(x, w.reshape(1, d))


def _rms_bwd(dh, x, g, w, *, ts=256):
    s, d = x.shape
    ts = _tile(s, ts)

    def body(dh_ref, x_ref, g_ref, w_ref, dx_ref, dxb_ref, dw_ref):
        xv = x_ref[...]
        r = lax.rsqrt(jnp.mean(xv * xv, axis=-1, keepdims=True) + EPS)
        xh = xv * r
        dhv = dh_ref[...]
        dn = dhv * w_ref[...]
        dx = g_ref[...] + r * (dn - xh * jnp.mean(dn * xh, axis=-1, keepdims=True))
        dx_ref[...] = dx
        dxb_ref[...] = dx.astype(BF16)
        part = jnp.sum(dhv * xh, axis=0, keepdims=True)

        @pl.when(pl.program_id(0) == 0)
        def _():
            dw_ref[...] = part

        @pl.when(pl.program_id(0) > 0)
        def _():
            dw_ref[...] += part

    row = pl.BlockSpec((ts, d), lambda i: (i, 0))
    vec = pl.BlockSpec((1, d), lambda i: (0, 0))
    return pl.pallas_call(
        body, name="rms_bwd", grid=(s // ts,),
        in_specs=[row, row, row, vec],
        out_specs=[row, row, vec],
        out_shape=[jax.ShapeDtypeStruct((s, d), F32), jax.ShapeDtypeStruct((s, d), BF16),
                   jax.ShapeDtypeStruct((1, d), F32)],
        compiler_params=_params(("arbitrary",)),
    )(dh, x, g, w.reshape(1, d))


def _loss_head(x, target, w, *, ts=256):
    s, d = x.shape
    ts = _tile(s, ts)

    def body(x_ref, t_ref, w_ref, loss_ref, dx_ref, dxb_ref, dw_ref):
        xv = x_ref[...]
        r = lax.rsqrt(jnp.mean(xv * xv, axis=-1, keepdims=True) + EPS)
        xh = xv * r
        wv = w_ref[...]
        diff = xh * wv - t_ref[...]
        lpart = 0.5 * jnp.sum(jnp.mean(diff * diff, axis=-1, keepdims=True), axis=0, keepdims=True)
        dout = diff * (1.0 / d)
        dn = dout * wv
        dx = r * (dn - xh * jnp.mean(dn * xh, axis=-1, keepdims=True))
        dx_ref[...] = dx
        dxb_ref[...] = dx.astype(BF16)
        part = jnp.sum(dout * xh, axis=0, keepdims=True)
        lrow = jnp.broadcast_to(lpart, loss_ref.shape)

        @pl.when(pl.program_id(0) == 0)
        def _():
            dw_ref[...] = part
            loss_ref[...] = lrow

        @pl.when(pl.program_id(0) > 0)
        def _():
            dw_ref[...] += part
            loss_ref[...] += lrow

    row = pl.BlockSpec((ts, d), lambda i: (i, 0))
    vec = pl.BlockSpec((1, d), lambda i: (0, 0))
    return pl.pallas_call(
        body, name="loss_head", grid=(s // ts,),
        in_specs=[row, row, vec],
        out_specs=[pl.BlockSpec((1, LANES), lambda i: (0, 0)), row, row, vec],
        out_shape=[jax.ShapeDtypeStruct((1, LANES), F32), jax.ShapeDtypeStruct((s, d), F32),
                   jax.ShapeDtypeStruct((s, d), BF16), jax.ShapeDtypeStruct((1, d), F32)],
        compiler_params=_params(("arbitrary",)),
    )(x, target, w.reshape(1, d))


def _prep_fwd(aq, ak, av, rq, rk, rv, cos, sin, qw, kw, *, ts=256):
    s = aq.shape[0]
    ts = _tile(s, ts)
    attn_scale = HEAD_DIM ** -0.5
    ret_scale = RET_QK_DIM ** -0.5
    nq, nk, nr = aq.shape[1] // HEAD_DIM, ak.shape[1] // HEAD_DIM, rq.shape[1] // RET_QK_DIM

    def body(aq_ref, ak_ref, av_ref, rq_ref, rk_ref, rv_ref, cos_ref, sin_ref, qw_ref, kw_ref,
             q_out, k_out, v_out, rq_out, rk_out, rv_out):
        c, sn = cos_ref[...], sin_ref[...]

        def normed(u, w):
            return u * lax.rsqrt(jnp.mean(u * u, axis=-1, keepdims=True) + EPS) * w

        for j in range(nq):
            sl = slice(j * HEAD_DIM, (j + 1) * HEAD_DIM)
            q_out[:, sl] = (_rope(normed(aq_ref[:, sl], qw_ref[...]), c, sn) * attn_scale).astype(BF16)
        for j in range(nk):
            sl = slice(j * HEAD_DIM, (j + 1) * HEAD_DIM)
            k_out[:, sl] = _rope(normed(ak_ref[:, sl], kw_ref[...]), c, sn).astype(BF16)
        for j in range(nr):
            sl = slice(j * RET_QK_DIM, (j + 1) * RET_QK_DIM)
            rq_out[:, sl] = _rope(rq_ref[:, sl], c, sn).astype(BF16)
            rk_out[:, sl] = (_rope(rk_ref[:, sl], c, sn) * ret_scale).astype(BF16)
        v_out[...] = av_ref[...].astype(BF16)
        rv_out[...] = rv_ref[...].astype(BF16)

    def row(arr):
        return pl.BlockSpec((ts, arr.shape[1]), lambda i: (i, 0))

    vec = pl.BlockSpec((1, HEAD_DIM), lambda i: (0, 0))
    ins = [aq, ak, av, rq, rk, rv]
    return pl.pallas_call(
        body, name="prep_fwd", grid=(s // ts,),
        in_specs=[row(a) for a in ins] + [row(cos), row(sin), vec, vec],
        out_specs=[row(a) for a in ins],
        out_shape=[jax.ShapeDtypeStruct(a.shape, BF16) for a in ins],
        compiler_params=_params(("parallel",)),
    )(*ins, cos, sin, qw.reshape(1, HEAD_DIM), kw.reshape(1, HEAD_DIM))


def _prep_bwd(dq, dk, drq, drk, aq, ak, cos, sin, qw, kw, *, ts=256):
    s = aq.shape[0]
    ts = _tile(s, ts)
    attn_scale = HEAD_DIM ** -0.5
    ret_scale = RET_QK_DIM ** -0.5
    nq, nk, nr = aq.shape[1] // HEAD_DIM, ak.shape[1] // HEAD_DIM, drq.shape[1] // RET_QK_DIM

    def body(dq_ref, dk_ref, drq_ref, drk_ref, aq_ref, ak_ref, cos_ref, sin_ref, qw_ref, kw_ref,
             daq_out, dak_out, drq_out, drk_out, dqw_ref, dkw_ref):
        c, sn = cos_ref[...], sin_ref[...]

        def unrope(d):
            return _rope_transposed(d, c, sn)

        def norm_bwd(dun, u, w):
            r = lax.rsqrt(jnp.mean(u * u, axis=-1, keepdims=True) + EPS)
            uh = u * r
            dn = dun * w
            du = r * (dn - uh * jnp.mean(dn * uh, axis=-1, keepdims=True))
            return du, jnp.sum(dun * uh, axis=0, keepdims=True)

        dqw = jnp.zeros((1, HEAD_DIM), F32)
        for j in range(nq):
            sl = slice(j * HEAD_DIM, (j + 1) * HEAD_DIM)
            du, dw = norm_bwd(unrope(dq_ref[:, sl] * attn_scale), aq_ref[:, sl], qw_ref[...])
            daq_out[:, sl] = du.astype(BF16)
            dqw = dqw + dw
        dkw = jnp.zeros((1, HEAD_DIM), F32)
        for j in range(nk):
            sl = slice(j * HEAD_DIM, (j + 1) * HEAD_DIM)
            du, dw = norm_bwd(unrope(dk_ref[:, sl]), ak_ref[:, sl], kw_ref[...])
            dak_out[:, sl] = du.astype(BF16)
            dkw = dkw + dw
        for j in range(nr):
            sl = slice(j * RET_QK_DIM, (j + 1) * RET_QK_DIM)
            drq_out[:, sl] = unrope(drq_ref[:, sl]).astype(BF16)
            drk_out[:, sl] = unrope(drk_ref[:, sl] * ret_scale).astype(BF16)

        @pl.when(pl.program_id(0) == 0)
        def _():
            dqw_ref[...] = dqw
            dkw_ref[...] = dkw

        @pl.when(pl.program_id(0) > 0)
        def _():
            dqw_ref[...] += dqw
            dkw_ref[...] += dkw

    def row(arr):
        return pl.BlockSpec((ts, arr.shape[1]), lambda i: (i, 0))

    vec = pl.BlockSpec((1, HEAD_DIM), lambda i: (0, 0))
    ins = [dq, dk, drq, drk, aq, ak, cos, sin]
    outs = [dq, dk, drq, drk]
    return pl.pallas_call(
        body, name="prep_bwd", grid=(s // ts,),
        in_specs=[row(a) for a in ins] + [vec, vec],
        out_specs=[row(a) for a in outs] + [vec, vec],
        out_shape=[jax.ShapeDtypeStruct(a.shape, BF16) for a in outs]
        + [jax.ShapeDtypeStruct((1, HEAD_DIM), F32)] * 2,
        compiler_params=_params(("arbitrary",)),
    )(*ins, qw.reshape(1, HEAD_DIM), kw.reshape(1, HEAD_DIM))


def _attn_fwd(q, k, v, *, tq=1024, sub=256):
    s, aw = q.shape
    tq = _tile(s, tq)
    sub = _tile(tq, sub)
    heads, kvh = aw // HEAD_DIM, k.shape[1] // HEAD_DIM
    grp = heads // kvh

    def body(q_ref, k_ref, v_ref, o_ref, lse_ref):
        kv_ = k_ref[...]
        v_ext = jnp.concatenate([v_ref[...], jnp.ones((s, HEAD_DIM), BF16)], axis=-1)
        for r in range(tq // sub):
            rows = slice(r * sub, (r + 1) * sub)
            sc = lax.dot_general(q_ref[rows, :], kv_, NT_DIMS, preferred_element_type=F32)
            m = jnp.max(sc, axis=-1, keepdims=True)
            p = jnp.exp((sc - m).astype(BF16))
            oe = jnp.dot(p, v_ext, preferred_element_type=F32)
            l = oe[:, HEAD_DIM:HEAD_DIM + 1]
            o_ref[rows, :] = (oe[:, :HEAD_DIM] / l).astype(o_ref.dtype)
            lse_ref[rows, :] = jnp.broadcast_to(m + jnp.log(l), (sub, HEAD_DIM))

    qspec = pl.BlockSpec((tq, HEAD_DIM), lambda kv, g, i: (i, kv * grp + g))
    kspec = pl.BlockSpec((s, HEAD_DIM), lambda kv, g, i: (0, kv))
    return pl.pallas_call(
        body, name="attn_fwd", grid=(kvh, grp, s // tq),
        in_specs=[qspec, kspec, kspec],
        out_specs=[qspec, qspec],
        out_shape=[jax.ShapeDtypeStruct((s, aw), BF16), jax.ShapeDtypeStruct((s, aw), F32)],
        compiler_params=_params(("parallel", "parallel", "parallel")),
    )(q, k, v)


def _attn_bwd(q, k, v, o, do, lse, *, tq=512, sub=256):
    s, aw = q.shape
    tq = _tile(s, tq)
    sub = _tile(tq, sub)
    heads, kvh = aw // HEAD_DIM, k.shape[1] // HEAD_DIM
    grp = heads // kvh
    nq = s // tq

    def body(q_ref, k_ref, v_ref, o_ref, do_ref, lse_ref, dq_ref, dk_ref, dv_ref, dk_acc, dv_acc, p_scr, ds_scr):
        g, i = pl.program_id(1), pl.program_id(2)
        kv_, vv = k_ref[...], v_ref[...]
        for r in range(tq // sub):
            rows = slice(r * sub, (r + 1) * sub)
            qv, dov = q_ref[rows, :], do_ref[rows, :]
            sc = lax.dot_general(qv, kv_, NT_DIMS, preferred_element_type=F32)
            p = jnp.exp((sc - lse_ref[rows, :1]).astype(BF16))
            dp = lax.dot_general(dov, vv, NT_DIMS, preferred_element_type=F32)
            delta = jnp.sum(dov.astype(F32) * o_ref[rows, :].astype(F32), axis=-1, keepdims=True)
            ds = p * (dp - delta).astype(BF16)
            dq_ref[rows, :] = jnp.dot(ds, kv_, preferred_element_type=F32)
            p_scr[rows, :] = p
            ds_scr[rows, :] = ds
        dvp = lax.dot_general(p_scr[...], do_ref[...], TN_DIMS, preferred_element_type=F32)
        dkp = lax.dot_general(ds_scr[...], q_ref[...], TN_DIMS, preferred_element_type=F32)
        first = jnp.logical_and(g == 0, i == 0)

        @pl.when(first)
        def _():
            dv_acc[...] = dvp
            dk_acc[...] = dkp

        @pl.when(jnp.logical_not(first))
        def _():
            dv_acc[...] += dvp
            dk_acc[...] += dkp

        @pl.when(jnp.logical_and(g == grp - 1, i == nq - 1))
        def _():
            dk_ref[...] = dk_acc[...]
            dv_ref[...] = dv_acc[...].astype(dv_ref.dtype)

    qspec = pl.BlockSpec((tq, HEAD_DIM), lambda kv, g, i: (i, kv * grp + g))
    kspec = pl.BlockSpec((s, HEAD_DIM), lambda kv, g, i: (0, kv))
    return pl.pallas_call(
        body, name="attn_bwd", grid=(kvh, grp, nq),
        in_specs=[qspec, kspec, kspec, qspec, qspec, qspec],
        out_specs=[qspec, kspec, kspec],
        out_shape=[jax.ShapeDtypeStruct((s, aw), F32), jax.ShapeDtypeStruct(k.shape, F32),
                   jax.ShapeDtypeStruct(v.shape, BF16)],
        scratch_shapes=[pltpu.VMEM((s, HEAD_DIM), F32), pltpu.VMEM((s, HEAD_DIM), F32),
                        pltpu.VMEM((tq, s), BF16), pltpu.VMEM((tq, s), BF16)],
        compiler_params=_params(("parallel", "arbitrary", "arbitrary")),
    )(q, k, v, o, do, lse)


def _sum_all(z):
    return jnp.sum(jnp.sum(z, axis=0, keepdims=True), axis=1, keepdims=True)


def _chunk_consts(df_ref, db_ref, t):
    lf = _log_sigmoid(df_ref[0][:, :1])
    lb = _log_sigmoid(db_ref[0][:, :1])
    r = lax.broadcasted_iota(jnp.int32, (t, 1), 0).astype(F32)
    c = lax.broadcasted_iota(jnp.int32, (1, t), 1).astype(F32)
    diff = r - c
    dm = jnp.exp(diff * jnp.where(diff >= 0, lf, -lb))
    return dict(diff=diff, dm=dm, r=r,
                af=jnp.exp(lf * (r + 1.0)), bf=jnp.exp(lf * (t - 1.0 - r)), gf=jnp.exp(lf * t),
                ab=jnp.exp(lb * (t - r)), bb=jnp.exp(lb * r), gb=jnp.exp(lb * t))


def _scaled(x, f):
    return (x.astype(F32) * f).astype(BF16)


def _retc_specs(s):
    qspec = pl.BlockSpec((s, RET_QK_DIM), lambda h: (0, h))
    vspec = pl.BlockSpec((s, RET_V_DIM), lambda h: (0, h))
    dspec = pl.BlockSpec((1, 1, LANES), lambda h: (h, 0, 0))
    return qspec, vspec, dspec


def _retc_fwd(q, k, v, dec_f, dec_b, *, t=256):
    s, qw = q.shape
    t = _tile(s, t)
    heads, nc = qw // RET_QK_DIM, s // t
    qspec, vspec, dspec = _retc_specs(s)

    def body(q_ref, k_ref, v_ref, df_ref, db_ref, o_ref):
        cs = _chunk_consts(df_ref, db_ref, t)

        def rows_of(i):
            return pl.ds(pl.multiple_of(i * t, t), t)

        def forward(i, sf):
            rows = rows_of(i)
            qi, ki, vi = q_ref[rows, :], k_ref[rows, :], v_ref[rows, :]
            sc = lax.dot_general(qi, ki, NT_DIMS, preferred_element_type=F32)
            intra = jnp.dot((sc * cs["dm"]).astype(BF16), vi, preferred_element_type=F32)
            cross = jnp.dot(_scaled(qi, cs["af"]), sf.astype(BF16), preferred_element_type=F32)
            o_ref[rows, :] = intra + cross
            return cs["gf"] * sf + lax.dot_general(_scaled(ki, cs["bf"]), vi, TN_DIMS, preferred_element_type=F32)

        def backward(j, sb):
            rows = rows_of(nc - 1 - j)
            qi, ki, vi = q_ref[rows, :], k_ref[rows, :], v_ref[rows, :]
            o_ref[rows, :] += jnp.dot(_scaled(qi, cs["ab"]), sb.astype(BF16), preferred_element_type=F32)
            return cs["gb"] * sb + lax.dot_general(_scaled(ki, cs["bb"]), vi, TN_DIMS, preferred_element_type=F32)

        zero = jnp.zeros((RET_QK_DIM, RET_V_DIM), F32)
        lax.fori_loop(0, nc, forward, zero)
        lax.fori_loop(0, nc, backward, zero)

    return pl.pallas_call(
        body, name="ret_fwd", grid=(heads,),
        in_specs=[qspec, qspec, vspec, dspec, dspec],
        out_specs=vspec, out_shape=jax.ShapeDtypeStruct(v.shape, F32),
        compiler_params=_params(("parallel",)),
    )(q, k, v, dec_f, dec_b)


def _retc_bwd(q, k, v, do, dec_f, dec_b, *, t=256):
    s, qw = q.shape
    t = _tile(s, t)
    heads, nc = qw // RET_QK_DIM, s // t
    qspec, vspec, dspec = _retc_specs(s)
    gspec = pl.BlockSpec((1, 8, LANES), lambda h: (h, 0, 0))

    def body(q_ref, k_ref, v_ref, do_ref, df_ref, db_ref, dq_ref, dk_ref, dv_ref, gf_ref, gb_ref,
             sf_scr, sb_scr, dv_acc):
        cs = _chunk_consts(df_ref, db_ref, t)
        r, diff, dm = cs["r"], cs["diff"], cs["dm"]

        def rows_of(i):
            return pl.ds(pl.multiple_of(i * t, t), t)

        def tn(a, b):
            return lax.dot_general(a, b, TN_DIMS, preferred_element_type=F32)

        def nt(a, b):
            return lax.dot_general(a, b, NT_DIMS, preferred_element_type=F32)

        def states_f(i, sf):
            sf_scr[i] = sf
            rows = rows_of(i)
            return cs["gf"] * sf + tn(_scaled(k_ref[rows, :], cs["bf"]), v_ref[rows, :])

        def states_b(j, sb):
            i = nc - 1 - j
            sb_scr[i] = sb
            rows = rows_of(i)
            return cs["gb"] * sb + tn(_scaled(k_ref[rows, :], cs["bb"]), v_ref[rows, :])

        zero = jnp.zeros((RET_QK_DIM, RET_V_DIM), F32)
        lax.fori_loop(0, nc, states_f, zero)
        lax.fori_loop(0, nc, states_b, zero)

        def scan_grads(i, state, u, qf, kf, vi, doi, fa, fb, step, wa, wb):
            qa, kb = qf * fa, kf * fb
            ub = u.astype(BF16)
            dqa = nt(doi, state.astype(BF16))
            dkb = nt(vi, ub)
            dv = jnp.dot(kb.astype(BF16), ub, preferred_element_type=F32)
            dlog = _sum_all(dqa * qa * wa) + _sum_all(dkb * kb * wb) + t * step * _sum_all(u * state)
            u_new = step * u + tn(qa.astype(BF16), doi)
            return dqa * fa, dkb * fb, dv, u_new, dlog

        def sweep_f(j, carry):
            u, accf, accb = carry
            i = nc - 1 - j
            rows = rows_of(i)
            qi, ki, vi, doi = q_ref[rows, :], k_ref[rows, :], v_ref[rows, :], do_ref[rows, :]
            sc = nt(qi, ki)
            p = sc * dm
            dp = nt(doi, vi)
            ds = (dp * dm).astype(BF16)
            tt = dp * p * diff
            accf = accf + _sum_all(jnp.where(diff > 0, tt, 0.0))
            accb = accb + _sum_all(jnp.where(diff < 0, -tt, 0.0))
            dq1, dk1, dv1, u, dlog = scan_grads(i, sf_scr[i], u, qi.astype(F32), ki.astype(F32), vi, doi,
                                                cs["af"], cs["bf"], cs["gf"], r + 1.0, t - 1.0 - r)
            dq_ref[rows, :] = jnp.dot(ds, ki, preferred_element_type=F32) + dq1
            dk_ref[rows, :] = tn(ds, qi) + dk1
            dv_acc[rows, :] = tn(p.astype(BF16), doi) + dv1
            return u, accf + dlog, accb

        def sweep_b(i, carry):
            w, accb = carry
            rows = rows_of(i)
            qi, ki, vi, doi = q_ref[rows, :], k_ref[rows, :], v_ref[rows, :], do_ref[rows, :]
            dq1, dk1, dv1, w, dlog = scan_grads(i, sb_scr[i], w, qi.astype(F32), ki.astype(F32), vi, doi,
                                                cs["ab"], cs["bb"], cs["gb"], t - r, r)
            dq_ref[rows, :] += dq1
            dk_ref[rows, :] += dk1
            dv_acc[rows, :] += dv1
            return w, accb + dlog

        z11 = jnp.zeros((1, 1), F32)
        _, accf, accb = lax.fori_loop(0, nc, sweep_f, (zero, z11, z11))
        _, accb = lax.fori_loop(0, nc, sweep_b, (zero, accb))
        dv_ref[...] = dv_acc[...].astype(dv_ref.dtype)
        gf_ref[...] = jnp.broadcast_to((accf / (1.0 + jnp.exp(df_ref[0][:, :1]))).reshape(1, 1, 1), gf_ref.shape)
        gb_ref[...] = jnp.broadcast_to((accb / (1.0 + jnp.exp(db_ref[0][:, :1]))).reshape(1, 1, 1), gb_ref.shape)

    return pl.pallas_call(
        body, name="ret_bwd", grid=(heads,),
        in_specs=[qspec, qspec, vspec, vspec, dspec, dspec],
        out_specs=[qspec, qspec, vspec, gspec, gspec],
        out_shape=[jax.ShapeDtypeStruct(q.shape, F32), jax.ShapeDtypeStruct(k.shape, F32),
                   jax.ShapeDtypeStruct(v.shape, BF16),
                   jax.ShapeDtypeStruct((heads, 8, LANES), F32), jax.ShapeDtypeStruct((heads, 8, LANES), F32)],
        scratch_shapes=[pltpu.VMEM((nc, RET_QK_DIM, RET_V_DIM), F32), pltpu.VMEM((nc, RET_QK_DIM, RET_V_DIM), F32),
                        pltpu.VMEM((s, RET_V_DIM), F32)],
        compiler_params=_params(("parallel",)),
    )(q, k, v, do, dec_f, dec_b)


def _gate_fwd(att, ag, ret, rg, rnw, *, ts=256):
    s, aw = att.shape
    rw = ret.shape[1]
    ts = _tile(s, ts)
    rheads = rw // RET_V_DIM

    def body(att_ref, ag_ref, ret_ref, rg_ref, w_ref, y_ref):
        sa, _ = _silu_parts(ag_ref[...])
        y_ref[:, :aw] = (sa * att_ref[...].astype(F32)).astype(BF16)
        for h in range(rheads):
            sl = slice(h * RET_V_DIM, (h + 1) * RET_V_DIM)
            rt = ret_ref[:, sl]
            rn = rt * lax.rsqrt(jnp.mean(rt * rt, axis=-1, keepdims=True) + EPS) * w_ref[:, sl]
            sr, _ = _silu_parts(rg_ref[:, sl])
            y_ref[:, aw + h * RET_V_DIM:aw + (h + 1) * RET_V_DIM] = (sr * rn).astype(BF16)

    def row(w):
        return pl.BlockSpec((ts, w), lambda i: (i, 0))

    return pl.pallas_call(
        body, name="gate_fwd", grid=(s // ts,),
        in_specs=[row(aw), row(aw), row(rw), row(rw), pl.BlockSpec((1, rw), lambda i: (0, 0))],
        out_specs=row(aw + rw), out_shape=jax.ShapeDtypeStruct((s, aw + rw), BF16),
        compiler_params=_params(("parallel",)),
    )(att, ag, ret, rg, rnw.reshape(1, rw))


def _gate_bwd(dy, att, ag, ret, rg, rnw, *, ts=256):
    s, aw = att.shape
    rw = ret.shape[1]
    ts = _tile(s, ts)
    rheads = rw // RET_V_DIM

    def body(dy_ref, att_ref, ag_ref, ret_ref, rg_ref, w_ref, datt_ref, dag_ref, dret_ref, drg_ref, dw_ref):
        sa, dsa = _silu_parts(ag_ref[...])
        dya = dy_ref[:, :aw]
        datt_ref[...] = (dya * sa).astype(BF16)
        dag_ref[...] = (dya * att_ref[...].astype(F32) * dsa).astype(BF16)
        parts = []
        for h in range(rheads):
            sl = slice(h * RET_V_DIM, (h + 1) * RET_V_DIM)
            rt = ret_ref[:, sl]
            rr = lax.rsqrt(jnp.mean(rt * rt, axis=-1, keepdims=True) + EPS)
            rh = rt * rr
            wv = w_ref[:, sl]
            sr, dsr = _silu_parts(rg_ref[:, sl])
            dyr = dy_ref[:, aw + h * RET_V_DIM:aw + (h + 1) * RET_V_DIM]
            drg_ref[:, sl] = (dyr * rh * wv * dsr).astype(BF16)
            drn = dyr * sr
            dn = drn * wv
            dret_ref[:, sl] = (rr * (dn - rh * jnp.mean(dn * rh, axis=-1, keepdims=True))).astype(BF16)
            parts.append(jnp.sum(drn * rh, axis=0, keepdims=True))
        part = jnp.concatenate(parts, axis=-1)

        @pl.when(pl.program_id(0) == 0)
        def _():
            dw_ref[...] = part

        @pl.when(pl.program_id(0) > 0)
        def _():
            dw_ref[...] += part

    def row(w):
        return pl.BlockSpec((ts, w), lambda i: (i, 0))

    vec = pl.BlockSpec((1, rw), lambda i: (0, 0))
    return pl.pallas_call(
        body, name="gate_bwd", grid=(s // ts,),
        in_specs=[row(aw + rw), row(aw), row(aw), row(rw), row(rw), vec],
        out_specs=[row(aw), row(aw), row(rw), row(rw), vec],
        out_shape=[jax.ShapeDtypeStruct((s, aw), BF16), jax.ShapeDtypeStruct((s, aw), BF16),
                   jax.ShapeDtypeStruct((s, rw), BF16), jax.ShapeDtypeStruct((s, rw), BF16),
                   jax.ShapeDtypeStruct((1, rw), F32)],
        compiler_params=_params(("arbitrary",)),
    )(dy, att, ag, ret, rg, rnw.reshape(1, rw))


def _mesh_position():
    x, y, c = lax.axis_index("x"), lax.axis_index("y"), lax.axis_index("c")
    return x, y, c, 4 * x + 2 * y + c


def _peer(x, y, c, k):
    px = 1 - x if k & 4 else x
    py = 1 - y if k & 2 else y
    pc = 1 - c if k & 1 else c
    return (px, py, pc), 4 * px + 2 * py + pc


HBM_SPEC = pl.BlockSpec(memory_space=pltpu.HBM)
SEM_SPEC = pl.BlockSpec(memory_space=pltpu.SEMAPHORE)
ANY_SPEC = pl.BlockSpec(memory_space=pl.ANY)
DATAFLOW = pltpu.SideEffectType.DATAFLOW_SIDE_EFFECTING
N_SPLIT_COPIES = 2 * (N_DEV - 1)


def _hbm(a):
    return pltpu.with_memory_space_constraint(a, pltpu.HBM)


def _split_start(name, copies, bufs, after):
    def body(*refs):
        ins, (send_sems, recv_sems), token = refs[:4], refs[4 + len(after):6 + len(after)], refs[-1]
        sends, _ = copies(ins, send_sems, recv_sems)
        for cp in sends:
            cp.start()
        token[...] = jnp.zeros_like(token)

    return pl.pallas_call(
        body, name=name,
        out_shape=(pltpu.SemaphoreType.DMA((N_SPLIT_COPIES,)), pltpu.SemaphoreType.DMA((N_SPLIT_COPIES,)),
                   *[pltpu.HBM(b.shape, b.dtype) for b in bufs], jax.ShapeDtypeStruct((8, LANES), F32)),
        in_specs=[HBM_SPEC] * 4 + [ANY_SPEC] * len(after),
        out_specs=(SEM_SPEC, SEM_SPEC, HBM_SPEC, HBM_SPEC, HBM_SPEC, HBM_SPEC, pl.BlockSpec(memory_space=pltpu.VMEM)),
        input_output_aliases={0: 2, 1: 3, 2: 4, 3: 5},
        compiler_params=pltpu.CompilerParams(has_side_effects=DATAFLOW),
    )(*[_hbm(b) for b in bufs], *after)


def _split_wait(name, copies, started, after):
    send_sems, recv_sems, *bufs = started[:6]

    def body(*refs):
        ins, send_ref, recv_ref = refs[:4], refs[4], refs[5]
        sends, recvs = copies(ins, send_ref, recv_ref)
        for cp in sends:
            cp.wait_send()
        for cp in recvs:
            cp.wait_recv()

    return pl.pallas_call(
        body, name=name,
        out_shape=tuple(pltpu.HBM(b.shape, b.dtype) for b in bufs),
        in_specs=[HBM_SPEC] * 4 + [SEM_SPEC, SEM_SPEC] + [ANY_SPEC] * len(after),
        out_specs=(HBM_SPEC,) * 4,
        input_output_aliases={0: 0, 1: 1, 2: 2, 3: 3},
        compiler_params=pltpu.CompilerParams(has_side_effects=DATAFLOW),
    )(*bufs, send_sems, recv_sems, *after)


def _gather_copies(iws, mws):
    def copies(refs, send_sems, recv_sems):
        win_ref, wout_ref, win_full, wout_full = refs
        x, y, c, me = _mesh_position()

        def cols(p):
            return win_full.at[:, pl.ds(pl.multiple_of(p * iws, LANES), iws)]

        def rows(p):
            return wout_full.at[pl.ds(pl.multiple_of(p * mws, 16), mws), :]

        sends, recvs = [], []
        for k in range(1, N_DEV):
            peer, pid = _peer(x, y, c, k)
            for a, (src, mine, theirs) in enumerate(((win_ref, cols(me), cols(pid)), (wout_ref, rows(me), rows(pid)))):
                sem = 2 * (k - 1) + a
                sends.append(pltpu.make_async_remote_copy(
                    src_ref=src, dst_ref=mine, send_sem=send_sems.at[sem], recv_sem=recv_sems.at[sem],
                    device_id=peer, device_id_type=MESH))
                recvs.append(pltpu.make_async_remote_copy(
                    src_ref=src, dst_ref=theirs, send_sem=send_sems.at[sem], recv_sem=recv_sems.at[sem],
                    device_id=peer, device_id_type=MESH))
        return sends, recvs

    return copies


def _place_own(name, srcs, dsts_of, out_shapes):
    def body(*refs):
        n = len(srcs)
        src_refs, out_refs, sems = refs[:n], refs[n:2 * n], refs[2 * n]
        _, _, _, me = _mesh_position()
        cps = [pltpu.make_async_copy(s, d, sems.at[j]) for j, (s, d) in enumerate(zip(src_refs, dsts_of(out_refs, me)))]
        for cp in cps:
            cp.start()
        for cp in cps:
            cp.wait()

    return pl.pallas_call(
        body, name=name, in_specs=[ANY_SPEC] * len(srcs), out_specs=[ANY_SPEC] * len(srcs),
        out_shape=out_shapes, scratch_shapes=[pltpu.SemaphoreType.DMA((len(srcs),))],
    )(*srcs)


def _gather_start(win_sh, wout_sh, after, tag):
    d, iws = win_sh.shape
    mws = wout_sh.shape[0]

    def dsts(outs, me):
        return [outs[0].at[:, pl.ds(pl.multiple_of(me * iws, LANES), iws)],
                outs[1].at[pl.ds(pl.multiple_of(me * mws, 16), mws), :]]

    win_full, wout_full = _place_own(
        "place_shards", [win_sh, wout_sh], dsts,
        [jax.ShapeDtypeStruct((d, N_DEV * iws), win_sh.dtype), jax.ShapeDtypeStruct((N_DEV * mws, d), wout_sh.dtype)])
    return _split_start("gather_start_" + tag, _gather_copies(iws, mws), [win_sh, wout_sh, win_full, wout_full], after)


def _gather_wait(started, after, tag):
    iws, mws = started[2].shape[1], started[3].shape[0]
    out = _split_wait("gather_wait_" + tag, _gather_copies(iws, mws), started, after)
    return out[2], out[3]


def _scatter_copies(iws, mws):
    def copies(refs, send_sems, recv_sems):
        dwin_ref, dwout_ref, land_in, land_out = refs
        x, y, c, me = _mesh_position()

        def cols(p):
            return dwin_ref.at[:, pl.ds(pl.multiple_of(p * iws, LANES), iws)]

        def rows(p):
            return dwout_ref.at[pl.ds(pl.multiple_of(p * mws, 16), mws), :]

        sends, recvs = [], []
        for k in range(1, N_DEV):
            peer, pid = _peer(x, y, c, k)
            for a, (src, land) in enumerate(((cols(pid), land_in), (rows(pid), land_out))):
                sem = 2 * (k - 1) + a
                sends.append(pltpu.make_async_remote_copy(
                    src_ref=src, dst_ref=land.at[me], send_sem=send_sems.at[sem], recv_sem=recv_sems.at[sem],
                    device_id=peer, device_id_type=MESH))
                recvs.append(pltpu.make_async_remote_copy(
                    src_ref=src, dst_ref=land.at[pid], send_sem=send_sems.at[sem], recv_sem=recv_sems.at[sem],
                    device_id=peer, device_id_type=MESH))
        return sends, recvs

    return copies


def _scatter_start(dwin, dwout, tag):
    d, iw = dwin.shape
    mw = dwout.shape[0]
    iws, mws = iw // N_DEV, mw // N_DEV

    def place(refs, me):
        return [(refs[0].at[:, pl.ds(pl.multiple_of(me * iws, LANES), iws)], refs[2].at[me]),
                (refs[1].at[pl.ds(pl.multiple_of(me * mws, 16), mws), :], refs[3].at[me])]

    def body(dwin_ref, dwout_ref, land_in, land_out, sems):
        _, _, _, me = _mesh_position()
        cps = [pltpu.make_async_copy(s, d_, sems.at[j])
               for j, (s, d_) in enumerate(place((dwin_ref, dwout_ref, land_in, land_out), me))]
        for cp in cps:
            cp.start()
        for cp in cps:
            cp.wait()

    land_in, land_out = pl.pallas_call(
        body, name="place_slabs", in_specs=[ANY_SPEC] * 2, out_specs=[ANY_SPEC] * 2,
        out_shape=[jax.ShapeDtypeStruct((N_DEV, d, iws), dwin.dtype), jax.ShapeDtypeStruct((N_DEV, mws, d), dwout.dtype)],
        scratch_shapes=[pltpu.SemaphoreType.DMA((2,))],
    )(dwin, dwout)
    return _split_start("scatter_start_" + tag, _scatter_copies(iws, mws), [dwin, dwout, land_in, land_out], [])


def _scatter_wait(started, after, tag):
    iws, mws = started[4].shape[2], started[5].shape[1]
    out = _split_wait("scatter_wait_" + tag, _scatter_copies(iws, mws), started, after)
    return out[2], out[3]


def _exchange_small(buf, *, name, after=()):
    r = buf.shape[0]

    def body(*refs):
        buf_ref = refs[0]
        all_ref, sum_ref, send_sems, recv_sems = refs[1 + len(after):]
        x, y, c, me = _mesh_position()
        all_ref[me] = buf_ref[...]
        sends, recvs = [], []
        for k in range(1, N_DEV):
            peer, pid = _peer(x, y, c, k)
            sends.append(pltpu.make_async_remote_copy(
                src_ref=buf_ref, dst_ref=all_ref.at[me], send_sem=send_sems.at[k - 1], recv_sem=recv_sems.at[k - 1],
                device_id=peer, device_id_type=MESH))
            recvs.append(pltpu.make_async_remote_copy(
                src_ref=buf_ref, dst_ref=all_ref.at[pid], send_sem=send_sems.at[k - 1], recv_sem=recv_sems.at[k - 1],
                device_id=peer, device_id_type=MESH))
        for cp in sends:
            cp.start()
        for cp in recvs:
            cp.wait_recv()
        for cp in sends:
            cp.wait_send()
        total = all_ref[0]
        for p in range(1, N_DEV):
            total = total + all_ref[p]
        sum_ref[...] = total

    vmem = pl.BlockSpec(memory_space=pltpu.VMEM)
    return pl.pallas_call(
        body, name=name,
        in_specs=[vmem] + [ANY_SPEC] * len(after), out_specs=[vmem, vmem],
        out_shape=[jax.ShapeDtypeStruct((N_DEV, r, LANES), F32), jax.ShapeDtypeStruct((r, LANES), F32)],
        scratch_shapes=[pltpu.SemaphoreType.DMA((N_DEV - 1,)), pltpu.SemaphoreType.DMA((N_DEV - 1,))],
        compiler_params=pltpu.CompilerParams(has_side_effects=True),
    )(buf, *after)


def _adamw_math(w, g, m, v):
    m2 = ADAM_B1 * m + (1.0 - ADAM_B1) * g
    v2 = ADAM_B2 * v + (1.0 - ADAM_B2) * (g * g)
    delta = -ADAM_LR * ((m2 / ADAM_C1) / (jnp.sqrt(v2 / ADAM_C2) + ADAM_EPS) + ADAM_WD * w)
    return delta, m2, v2


def _adamw_slabs(layer, w, m, v, land, outs, order, *, tr, name):
    depth, r, c = w.shape
    tr = _tile(r, tr)

    def body(w_ref, m_ref, v_ref, land_ref, order_ref, o0, o1, o2, o3, g_ref, d_ref, m2_ref, v2_ref):
        g = land_ref[0].astype(F32)
        for p in range(1, N_DEV):
            g = g + land_ref[p].astype(F32)
        delta, m2, v2 = _adamw_math(w_ref[...], g, m_ref[...], v_ref[...])
        g_ref[...] = g
        d_ref[...] = delta
        m2_ref[...] = m2
        v2_ref[...] = v2

    row = pl.BlockSpec((None, tr, c), lambda i: (layer, i, 0))
    return pl.pallas_call(
        body, name=name, grid=(r // tr,),
        in_specs=[row, row, row, pl.BlockSpec((N_DEV, tr, c), lambda i: (0, i, 0)),
                  pl.BlockSpec((8, LANES), lambda i: (0, 0))] + [ANY_SPEC] * 4,
        out_specs=[row] * 4, out_shape=[jax.ShapeDtypeStruct((depth, r, c), F32)] * 4,
        input_output_aliases={5: 0, 6: 1, 7: 2, 8: 3},
        compiler_params=_params(("parallel",)),
    )(w, m, v, land, order, *outs)


def _adamw_small(w, g, m, v):
    def body(w_ref, g_ref, m_ref, v_ref, d_ref, m2_ref, v2_ref):
        delta, m2, v2 = _adamw_math(w_ref[...], g_ref[...], m_ref[...], v_ref[...])
        d_ref[...] = delta
        m2_ref[...] = m2
        v2_ref[...] = v2

    vmem = pl.BlockSpec(memory_space=pltpu.VMEM)
    return pl.pallas_call(
        body, name="adamw_small", in_specs=[vmem] * 4, out_specs=[vmem] * 3,
        out_shape=[jax.ShapeDtypeStruct(w.shape, F32)] * 3,
    )(w, g, m, v)


def _pack(parts):
    flat = jnp.concatenate([p.reshape(-1).astype(F32) for p in parts])
    rows = -(-flat.shape[0] // LANES)
    rows = -(-rows // SMALL_ROWS_ALIGN) * SMALL_ROWS_ALIGN
    flat = jnp.pad(flat, (0, rows * LANES - flat.shape[0]))
    return flat.reshape(rows, LANES)


def _unpack(buf, shapes):
    flat = buf.reshape(-1)
    out, pos = [], 0
    for shp in shapes:
        size = math.prod(shp)
        out.append(flat[pos:pos + size].reshape(shp))
        pos += size
    return out


def _section_widths(d):
    aw = d // 2
    kw = aw // ATTN_GROUP
    rw = d - aw
    rqw = (rw // RET_V_DIM) * RET_QK_DIM
    return (aw, kw, kw, aw, rqw, rqw, rw, rw)


def _layer_fwd(xl, nw, win_full, wout_full, qn, kn, dec_f, dec_b, rn, cos, sin):
    widths = _section_widths(xl.shape[1])
    offs = tuple(int(o) for o in np.cumsum((0,) + widths)[:-1])
    sec_tn = _tile(widths[1], 512)
    h = _rms_fwd(xl, nw)
    secs = [_matmul(h, win_full, name="proj", b_off=offs[i], n=widths[i], tn=sec_tn) for i in range(8)]
    aq, ak, av, ag, rq, rk, rv, rg = secs
    q, k, v, rqr, rkr, rvb = _prep_fwd(aq, ak, av, rq, rk, rv, cos, sin, qn, kn)
    att, lse = _attn_fwd(q, k, v)
    ret = _retc_fwd(rqr, rkr, rvb, dec_f, dec_b)
    y = _gate_fwd(att, ag, ret, rg, rn)
    xn = _matmul(y, wout_full, name="out_proj", residual=xl)
    saved = dict(x=xl, h=h, aq=aq, ak=ak, ag=ag, rg=rg, q=q, k=k, v=v, rq=rqr, rk=rkr, rv=rvb,
                 att=att, lse=lse, ret=ret, y=y, win=win_full, wout=wout_full)
    return xn, saved


def _layer_bwd_weights(gb, sv, qn, kn, dec_f, dec_b, rn, cos, sin):
    dy = _matmul(gb, sv["wout"], name="d_y", trans_b=True)
    dwout = _matmul(sv["y"].T, gb, name="d_wout", out_dtype=BF16)
    datt, dag, dret, drg, drn = _gate_bwd(dy, sv["att"], sv["ag"], sv["ret"], sv["rg"], rn)
    dq, dk, dav = _attn_bwd(sv["q"], sv["k"], sv["v"], sv["att"], datt, sv["lse"])
    drq, drk, drv, gf, gbk = _retc_bwd(sv["rq"], sv["rk"], sv["rv"], dret, dec_f, dec_b)
    daq, dak, drq_p, drk_p, dqn, dkn = _prep_bwd(dq, dk, drq, drk, sv["aq"], sv["ak"], cos, sin, qn, kn)
    dproj = jnp.concatenate([daq, dak, dav, dag, drq_p, drk_p, drv, drg], axis=-1)
    dwin = _matmul(sv["h"].T, dproj, name="d_win", out_dtype=BF16)
    small = dict(qn=dqn[0], kn=dkn[0], df=gf[:, 0, 0], db=gbk[:, 0, 0], rn=drn[0])
    return dproj, dwin, dwout, small


def _layer_bwd_input(g, dproj, sv, nw):
    dh = _matmul(dproj, sv["win"], name="d_h", trans_b=True, tk=_tile(dproj.shape[1], 2816, LANES))
    g, gb, dnw = _rms_bwd(dh, sv["x"], g, nw)
    return g, gb, dnw[0]


def kernel(x, norm_w, w_in, q_norm, k_norm, ret_decay_fwd, ret_decay_bwd, ret_norm, w_out, final_norm, loss_target, m_norm_w, m_w_in, m_q_norm, m_k_norm, m_ret_decay_fwd, m_ret_decay_bwd, m_ret_norm, m_w_out, m_final_norm, v_norm_w, v_w_in, v_q_norm, v_k_norm, v_ret_decay_fwd, v_ret_decay_bwd, v_ret_norm, v_w_out, v_final_norm):
    depth, d, _ = w_in.shape
    seq = x.shape[1]
    rw = _section_widths(d)[6]
    rheads = rw // RET_V_DIM
    rns = ret_norm.shape[-1]
    _, _, _, me = _mesh_position()

    target = loss_target[0]
    cos, sin = _rope_tables(seq)

    rn_all, _ = _exchange_small(_pack([ret_norm]), name="gather_ret_norm")
    rn_full = rn_all.reshape(N_DEV, -1)[:, :depth * rheads * rns].reshape(N_DEV, depth, rheads, rns)
    rn_full = jnp.transpose(rn_full, (1, 2, 0, 3)).reshape(depth, rw)

    dec_f = jnp.broadcast_to(ret_decay_fwd[:, :, None, None], (depth, rheads, 1, LANES))
    dec_b = jnp.broadcast_to(ret_decay_bwd[:, :, None, None], (depth, rheads, 1, LANES))

    win_bf = w_in.astype(BF16)
    wout_bf = w_out.astype(BF16)

    saved = []
    xl = x[0]
    full = _gather_wait(_gather_start(win_bf[0], wout_bf[0], [], "0"), [], "0")
    for l in range(depth):
        nw = norm_w[l]
        if l + 1 < depth:
            nxt = _gather_start(win_bf[l + 1], wout_bf[l + 1], [full[0]], str(l + 1))
            nw = nw + nxt[-1][0, 0]
        xl, sv = _layer_fwd(xl, nw, full[0], full[1], q_norm[l], k_norm[l], dec_f[l], dec_b[l],
                            rn_full[l], cos, sin)
        saved.append(sv)
        if l + 1 < depth:
            full = _gather_wait(nxt, [xl], str(l + 1))

    loss_row, g, gb, d_final = _loss_head(xl, target, final_norm)

    d_norm, d_qn, d_kn, d_df, d_db, d_rn = [], [], [], [], [], []
    lands = [None] * depth
    pending = None
    for l in reversed(range(depth)):
        dproj, dwin, dwout, sm = _layer_bwd_weights(gb, saved[l], q_norm[l], k_norm[l], dec_f[l], dec_b[l],
                                                    rn_full[l], cos, sin)
        started = _scatter_start(dwin, dwout, str(l))
        g, gb, dnw = _layer_bwd_input(g, dproj, saved[l], norm_w[l] + started[-1][0, 0])
        if pending is not None:
            lands[l + 1] = _scatter_wait(pending, [g], str(l + 1))
        pending = started
        d_norm.append(dnw)
        d_qn.append(sm["qn"])
        d_kn.append(sm["kn"])
        d_df.append(sm["df"])
        d_db.append(sm["db"])
        d_rn.append(sm["rn"])
    for lst in (d_norm, d_qn, d_kn, d_df, d_db, d_rn):
        lst.reverse()
    order = pending[-1]
    in_outs = [lax.empty(w_in.shape, F32) for _ in range(4)]
    out_outs = [lax.empty(w_out.shape, F32) for _ in range(4)]
    for l in reversed(range(depth)):
        if l == 0:
            lands[0] = _scatter_wait(pending, [g, in_outs[0], out_outs[0]], "0")
        in_outs = _adamw_slabs(l, w_in, m_w_in, v_w_in, lands[l][0], in_outs, order, tr=256, name="adamw_w_in")
        out_outs = _adamw_slabs(l, w_out, m_w_out, v_w_out, lands[l][1], out_outs, order, tr=64, name="adamw_w_out")

    small_shapes = [(depth, d), (depth, HEAD_DIM), (depth, HEAD_DIM), (depth, rheads), (depth, rheads),
                    (depth, rheads, N_DEV * rns), (d,), (1,)]
    grads_local = [jnp.stack(d_norm), jnp.stack(d_qn), jnp.stack(d_kn), jnp.stack(d_df), jnp.stack(d_db),
                   jnp.stack(d_rn).reshape(depth, rheads, N_DEV * rns), d_final[0], loss_row[0, :1]]
    _, gsum = _exchange_small(_pack(grads_local), name="all_reduce_small", after=(in_outs[0], out_outs[0]))
    g_norm, g_qn, g_kn, g_df, g_db, g_rn_full, g_final, loss = _unpack(gsum, small_shapes)
    g_rn = lax.dynamic_slice_in_dim(g_rn_full, me * rns, rns, axis=2)
    small_g = [g_norm, g_qn, g_kn, g_df, g_db, g_rn, g_final]
    small_w = [norm_w, q_norm, k_norm, ret_decay_fwd, ret_decay_bwd, ret_norm, final_norm]
    small_m = [m_norm_w, m_q_norm, m_k_norm, m_ret_decay_fwd, m_ret_decay_bwd, m_ret_norm, m_final_norm]
    small_v = [v_norm_w, v_q_norm, v_k_norm, v_ret_decay_fwd, v_ret_decay_bwd, v_ret_norm, v_final_norm]
    shapes = [a.shape for a in small_w]
    sd, sm, sv2 = _adamw_small(_pack(small_w), _pack(small_g), _pack(small_m), _pack(small_v))
    small_d, small_m2, small_v2 = _unpack(sd, shapes), _unpack(sm, shapes), _unpack(sv2, shapes)

    def ordered(small, win_v, wout_v):
        return [small[0], win_v, small[1], small[2], small[3], small[4], small[5], wout_v, small[6]]

    grads = ordered(small_g, in_outs[0], out_outs[0])
    deltas = ordered(small_d, in_outs[1], out_outs[1])
    new_m = ordered(small_m2, in_outs[2], out_outs[2])
    new_v = ordered(small_v2, in_outs[3], out_outs[3])
    return (loss.reshape(()), g[None], *grads, *deltas, *new_m, *new_v)
```

```python
import functools
import math

import jax
import jax.numpy as jnp
import numpy as np
from jax import lax
from jax.experimental import pallas as pl
from jax.experimental.pallas import tpu as pltpu

F32 = jnp.float32
BF16 = jnp.bfloat16

N_DEV = 8
HEAD_DIM = 128
ATTN_GROUP = 4
RET_QK_DIM = 128
RET_V_DIM = 256
GRID_W = 64
ROPE_THETA = 10000.0
EPS = 1e-6
ADAM_LR = 0.001
ADAM_B1 = 0.9
ADAM_B2 = 0.999
ADAM_EPS = 1e-08
ADAM_WD = 0.01
ADAM_STEP = 10
ADAM_C1 = 1.0 - ADAM_B1 ** ADAM_STEP
ADAM_C2 = 1.0 - ADAM_B2 ** ADAM_STEP
LANES = 128
SMALL_ROWS_ALIGN = 8
VMEM_LIMIT = 56 * 1024 * 1024

NT_DIMS = (((1,), (1,)), ((), ()))
TN_DIMS = (((0,), (0,)), ((), ()))
MESH = pl.DeviceIdType.MESH


def _params(sem):
    return pltpu.CompilerParams(dimension_semantics=sem, vmem_limit_bytes=VMEM_LIMIT)


def _tile(dim, pref, align=16):
    if dim <= pref:
        return dim
    for t in range(pref - pref % align, 0, -align):
        if dim % t == 0:
            return t
    raise ValueError((dim, pref, align))


def _silu_parts(z):
    sg = 1.0 / (1.0 + jnp.exp(-z))
    return z * sg, sg * (1.0 + z * (1.0 - sg))


def _log_sigmoid(x):
    return jnp.minimum(x, 0.0) - jnp.log(1.0 + jnp.exp(-jnp.abs(x)))


def _swap_pairs(z):
    lane = lax.broadcasted_iota(jnp.int32, z.shape, 1)
    return jnp.where((lane % 64) < 32, pltpu.roll(z, 96, 1), pltpu.roll(z, 32, 1))


def _rope(z, cos, sin):
    return z * cos + _swap_pairs(z) * sin


def _rope_transposed(d, cos, sin):
    return d * cos + _swap_pairs(d * sin)


def _rope_tables(seq):
    rows = seq // GRID_W
    row = jnp.repeat(jnp.arange(rows), GRID_W).astype(F32)
    col = jnp.tile(jnp.arange(GRID_W), rows).astype(F32)
    axis_dim = HEAD_DIM // 2
    inv = ROPE_THETA ** (-jnp.arange(0, axis_dim, 2, dtype=F32) / axis_dim)
    ar = row[:, None] * inv[None, :]
    ac = col[:, None] * inv[None, :]
    cos = jnp.concatenate([jnp.cos(ar), jnp.cos(ar), jnp.cos(ac), jnp.cos(ac)], axis=-1)
    sin = jnp.concatenate([-jnp.sin(ar), jnp.sin(ar), -jnp.sin(ac), jnp.sin(ac)], axis=-1)
    return cos, sin


def _matmul(a, b, *, name, trans_b=False, b_off=0, n=None, out_dtype=F32, residual=None,
            tm=1024, tn=512, tk=4096):
    m, k = a.shape
    if n is None:
        n = b.shape[0] if trans_b else b.shape[1]
    tm, tn, tk = _tile(m, tm), _tile(n, tn, LANES), _tile(k, tk, LANES)
    assert b_off % tn == 0
    joff = b_off // tn
    nk = k // tk
    has_res = residual is not None

    def body(*refs):
        if has_res:
            a_ref, b_ref, r_ref, o_ref = refs[:4]
        else:
            a_ref, b_ref, o_ref = refs[:3]
        if trans_b:
            part = lax.dot_general(a_ref[...], b_ref[...], NT_DIMS, preferred_element_type=F32)
        else:
            part = jnp.dot(a_ref[...], b_ref[...], preferred_element_type=F32)

        def finish(r):
            if has_res:
                r = r + r_ref[...]
            o_ref[...] = r.astype(o_ref.dtype)

        if nk == 1:
            finish(part)
        else:
            acc_ref = refs[-1]
            kk = pl.program_id(2)

            @pl.when(kk == 0)
            def _():
                acc_ref[...] = part

            @pl.when(kk > 0)
            def _():
                acc_ref[...] += part

            @pl.when(kk == nk - 1)
            def _():
                finish(acc_ref[...])

    if trans_b:
        b_spec = pl.BlockSpec((tn, tk), lambda i, j, kk: (j + joff, kk))
    else:
        b_spec = pl.BlockSpec((tk, tn), lambda i, j, kk: (kk, j + joff))
    in_specs = [pl.BlockSpec((tm, tk), lambda i, j, kk: (i, kk)), b_spec]
    args = [a, b]
    if has_res:
        in_specs.append(pl.BlockSpec((tm, tn), lambda i, j, kk: (i, j)))
        args.append(residual)
    return pl.pallas_call(
        body, name=name, grid=(m // tm, n // tn, nk),
        in_specs=in_specs,
        out_specs=pl.BlockSpec((tm, tn), lambda i, j, kk: (i, j)),
        out_shape=jax.ShapeDtypeStruct((m, n), out_dtype),
        scratch_shapes=[pltpu.VMEM((tm, tn), F32)] if nk > 1 else [],
        compiler_params=_params(("parallel", "parallel", "arbitrary")),
    )(*args)


def _rms_fwd(x, w, *, ts=256):
    s, d = x.shape
    ts = _tile(s, ts)

    def body(x_ref, w_ref, h_ref):
        xv = x_ref[...]
        r = lax.rsqrt(jnp.mean(xv * xv, axis=-1, keepdims=True) + EPS)
        h_ref[...] = (xv * r * w_ref[...]).astype(h_ref.dtype)

    row = pl.BlockSpec((ts, d), lambda i: (i, 0))
    return pl.pallas_call(
        body, name="rms_fwd", grid=(s // ts,),
        in_specs=[row, pl.BlockSpec((1, d), lambda i: (0, 0))],
        out_specs=row, out_shape=jax.ShapeDtypeStruct((s, d), BF16),
        compiler_params=_params(("parallel",)),
    )(x, w.reshape(1, d))


def _rms_bwd(dh, x, g, w, *, ts=256):
    s, d = x.shape
    ts = _tile(s, ts)

    def body(dh_ref, x_ref, g_ref, w_ref, dx_ref, dxb_ref, dw_ref):
        xv = x_ref[...]
        r = lax.rsqrt(jnp.mean(xv * xv, axis=-1, keepdims=True) + EPS)
        xh = xv * r
        dhv = dh_ref[...]
        dn = dhv * w_ref[...]
        dx = g_ref[...] + r * (dn - xh * jnp.mean(dn * xh, axis=-1, keepdims=True))
        dx_ref[...] = dx
        dxb_ref[...] = dx.astype(BF16)
        part = jnp.sum(dhv * xh, axis=0, keepdims=True)

        @pl.when(pl.program_id(0) == 0)
        def _():
            dw_ref[...] = part

        @pl.when(pl.program_id(0) > 0)
        def _():
            dw_ref[...] += part

    row = pl.BlockSpec((ts, d), lambda i: (i, 0))
    vec = pl.BlockSpec((1, d), lambda i: (0, 0))
    return pl.pallas_call(
        body, name="rms_bwd", grid=(s // ts,),
        in_specs=[row, row, row, vec],
        out_specs=[row, row, vec],
        out_shape=[jax.ShapeDtypeStruct((s, d), F32), jax.ShapeDtypeStruct((s, d), BF16),
                   jax.ShapeDtypeStruct((1, d), F32)],
        compiler_params=_params(("arbitrary",)),
    )(dh, x, g, w.reshape(1, d))


def _loss_head(x, target, w, *, ts=256):
    s, d = x.shape
    ts = _tile(s, ts)

    def body(x_ref, t_ref, w_ref, loss_ref, dx_ref, dxb_ref, dw_ref):
        xv = x_ref[...]
        r = lax.rsqrt(jnp.mean(xv * xv, axis=-1, keepdims=True) + EPS)
        xh = xv * r
        wv = w_ref[...]
        diff = xh * wv - t_ref[...]
        lpart = 0.5 * jnp.sum(jnp.mean(diff * diff, axis=-1, keepdims=True), axis=0, keepdims=True)
        dout = diff * (1.0 / d)
        dn = dout * wv
        dx = r * (dn - xh * jnp.mean(dn * xh, axis=-1, keepdims=True))
        dx_ref[...] = dx
        dxb_ref[...] = dx.astype(BF16)
        part = jnp.sum(dout * xh, axis=0, keepdims=True)
        lrow = jnp.broadcast_to(lpart, loss_ref.shape)

        @pl.when(pl.program_id(0) == 0)
        def _():
            dw_ref[...] = part
            loss_ref[...] = lrow

        @pl.when(pl.program_id(0) > 0)
        def _():
            dw_ref[...] += part
            loss_ref[...] += lrow

    row = pl.BlockSpec((ts, d), lambda i: (i, 0))
    vec = pl.BlockSpec((1, d), lambda i: (0, 0))
    return pl.pallas_call(
        body, name="loss_head", grid=(s // ts,),
        in_specs=[row, row, vec],
        out_specs=[pl.BlockSpec((1, LANES), lambda i: (0, 0)), row, row, vec],
        out_shape=[jax.ShapeDtypeStruct((1, LANES), F32), jax.ShapeDtypeStruct((s, d), F32),
                   jax.ShapeDtypeStruct((s, d), BF16), jax.ShapeDtypeStruct((1, d), F32)],
        compiler_params=_params(("arbitrary",)),
    )(x, target, w.reshape(1, d))


def _prep_fwd(aq, ak, av, rq, rk, rv, cos, sin, qw, kw, *, ts=256):
    s = aq.shape[0]
    ts = _tile(s, ts)
    attn_scale = HEAD_DIM ** -0.5
    ret_scale = RET_QK_DIM ** -0.5
    nq, nk, nr = aq.shape[1] // HEAD_DIM, ak.shape[1] // HEAD_DIM, rq.shape[1] // RET_QK_DIM

    def body(aq_ref, ak_ref, av_ref, rq_ref, rk_ref, rv_ref, cos_ref, sin_ref, qw_ref, kw_ref,
             q_out, k_out, v_out, rq_out, rk_out, rv_out):
        c, sn = cos_ref[...], sin_ref[...]

        def normed(u, w):
            return u * lax.rsqrt(jnp.mean(u * u, axis=-1, keepdims=True) + EPS) * w

        for j in range(nq):
            sl = slice(j * HEAD_DIM, (j + 1) * HEAD_DIM)
            q_out[:, sl] = (_rope(normed(aq_ref[:, sl], qw_ref[...]), c, sn) * attn_scale).astype(BF16)
        for j in range(nk):
            sl = slice(j * HEAD_DIM, (j + 1) * HEAD_DIM)
            k_out[:, sl] = _rope(normed(ak_ref[:, sl], kw_ref[...]), c, sn).astype(BF16)
        for j in range(nr):
            sl = slice(j * RET_QK_DIM, (j + 1) * RET_QK_DIM)
            rq_out[:, sl] = _rope(rq_ref[:, sl], c, sn).astype(BF16)
            rk_out[:, sl] = (_rope(rk_ref[:, sl], c, sn) * ret_scale).astype(BF16)
        v_out[...] = av_ref[...].astype(BF16)
        rv_out[...] = rv_ref[...].astype(BF16)

    def row(arr):
        return pl.BlockSpec((ts, arr.shape[1]), lambda i: (i, 0))

    vec = pl.BlockSpec((1, HEAD_DIM), lambda i: (0, 0))
    ins = [aq, ak, av, rq, rk, rv]
    return pl.pallas_call(
        body, name="prep_fwd", grid=(s // ts,),
        in_specs=[row(a) for a in ins] + [row(cos), row(sin), vec, vec],
        out_specs=[row(a) for a in ins],
        out_shape=[jax.ShapeDtypeStruct(a.shape, BF16) for a in ins],
        compiler_params=_params(("parallel",)),
    )(*ins, cos, sin, qw.reshape(1, HEAD_DIM), kw.reshape(1, HEAD_DIM))


def _prep_bwd(dq, dk, drq, drk, aq, ak, cos, sin, qw, kw, *, ts=256):
    s = aq.shape[0]
    ts = _tile(s, ts)
    attn_scale = HEAD_DIM ** -0.5
    ret_scale = RET_QK_DIM ** -0.5
    nq, nk, nr = aq.shape[1] // HEAD_DIM, ak.shape[1] // HEAD_DIM, drq.shape[1] // RET_QK_DIM

    def body(dq_ref, dk_ref, drq_ref, drk_ref, aq_ref, ak_ref, cos_ref, sin_ref, qw_ref, kw_ref,
             daq_out, dak_out, drq_out, drk_out, dqw_ref, dkw_ref):
        c, sn = cos_ref[...], sin_ref[...]

        def unrope(d):
            return _rope_transposed(d, c, sn)

        def norm_bwd(dun, u, w):
            r = lax.rsqrt(jnp.mean(u * u, axis=-1, keepdims=True) + EPS)
            uh = u * r
            dn = dun * w
            du = r * (dn - uh * jnp.mean(dn * uh, axis=-1, keepdims=True))
            return du, jnp.sum(dun * uh, axis=0, keepdims=True)

        dqw = jnp.zeros((1, HEAD_DIM), F32)
        for j in range(nq):
            sl = slice(j * HEAD_DIM, (j + 1) * HEAD_DIM)
            du, dw = norm_bwd(unrope(dq_ref[:, sl] * attn_scale), aq_ref[:, sl], qw_ref[...])
            daq_out[:, sl] = du.astype(BF16)
            dqw = dqw + dw
        dkw = jnp.zeros((1, HEAD_DIM), F32)
        for j in range(nk):
            sl = slice(j * HEAD_DIM, (j + 1) * HEAD_DIM)
            du, dw = norm_bwd(unrope(dk_ref[:, sl]), ak_ref[:, sl], kw_ref[...])
            dak_out[:, sl] = du.astype(BF16)
            dkw = dkw + dw
        for j in range(nr):
            sl = slice(j * RET_QK_DIM, (j + 1) * RET_QK_DIM)
            drq_out[:, sl] = unrope(drq_ref[:, sl]).astype(BF16)
            drk_out[:, sl] = unrope(drk_ref[:, sl] * ret_scale).astype(BF16)

        @pl.when(pl.program_id(0) == 0)
        def _():
            dqw_ref[...] = dqw
            dkw_ref[...] = dkw

        @pl.when(pl.program_id(0) > 0)
        def _():
            dqw_ref[...] += dqw
            dkw_ref[...] += dkw

    def row(arr):
        return pl.BlockSpec((ts, arr.shape[1]), lambda i: (i, 0))

    vec = pl.BlockSpec((1, HEAD_DIM), lambda i: (0, 0))
    ins = [dq, dk, drq, drk, aq, ak, cos, sin]
    outs = [dq, dk, drq, drk]
    return pl.pallas_call(
        body, name="prep_bwd", grid=(s // ts,),
        in_specs=[row(a) for a in ins] + [vec, vec],
        out_specs=[row(a) for a in outs] + [vec, vec],
        out_shape=[jax.ShapeDtypeStruct(a.shape, BF16) for a in outs]
        + [jax.ShapeDtypeStruct((1, HEAD_DIM), F32)] * 2,
        compiler_params=_params(("arbitrary",)),
    )(*ins, qw.reshape(1, HEAD_DIM), kw.reshape(1, HEAD_DIM))


def _attn_fwd(q, k, v, *, tq=1024, sub=256):
    s, aw = q.shape
    tq = _tile(s, tq)
    sub = _tile(tq, sub)
    heads, kvh = aw // HEAD_DIM, k.shape[1] // HEAD_DIM
    grp = heads // kvh

    def body(q_ref, k_ref, v_ref, o_ref, lse_ref):
        kv_ = k_ref[...]
        v_ext = jnp.concatenate([v_ref[...], jnp.ones((s, HEAD_DIM), BF16)], axis=-1)
        for r in range(tq // sub):
            rows = slice(r * sub, (r + 1) * sub)
            sc = lax.dot_general(q_ref[rows, :], kv_, NT_DIMS, preferred_element_type=F32)
            m = jnp.max(sc, axis=-1, keepdims=True)
            p = jnp.exp((sc - m).astype(BF16))
            oe = jnp.dot(p, v_ext, preferred_element_type=F32)
            l = oe[:, HEAD_DIM:HEAD_DIM + 1]
            o_ref[rows, :] = (oe[:, :HEAD_DIM] / l).astype(o_ref.dtype)
            lse_ref[rows, :] = jnp.broadcast_to(m + jnp.log(l), (sub, HEAD_DIM))

    qspec = pl.BlockSpec((tq, HEAD_DIM), lambda kv, g, i: (i, kv * grp + g))
    kspec = pl.BlockSpec((s, HEAD_DIM), lambda kv, g, i: (0, kv))
    return pl.pallas_call(
        body, name="attn_fwd", grid=(kvh, grp, s // tq),
        in_specs=[qspec, kspec, kspec],
        out_specs=[qspec, qspec],
        out_shape=[jax.ShapeDtypeStruct((s, aw), BF16), jax.ShapeDtypeStruct((s, aw), F32)],
        compiler_params=_params(("parallel", "parallel", "parallel")),
    )(q, k, v)


def _attn_bwd(q, k, v, o, do, lse, *, tq=512, sub=256):
    s, aw = q.shape
    tq = _tile(s, tq)
    sub = _tile(tq, sub)
    heads, kvh = aw // HEAD_DIM, k.shape[1] // HEAD_DIM
    grp = heads // kvh
    nq = s // tq

    def body(q_ref, k_ref, v_ref, o_ref, do_ref, lse_ref, dq_ref, dk_ref, dv_ref, dk_acc, dv_acc, p_scr, ds_scr):
        g, i = pl.program_id(1), pl.program_id(2)
        kv_, vv = k_ref[...], v_ref[...]
        for r in range(tq // sub):
            rows = slice(r * sub, (r + 1) * sub)
            qv, dov = q_ref[rows, :], do_ref[rows, :]
            sc = lax.dot_general(qv, kv_, NT_DIMS, preferred_element_type=F32)
            p = jnp.exp((sc - lse_ref[rows, :1]).astype(BF16))
            dp = lax.dot_general(dov, vv, NT_DIMS, preferred_element_type=F32)
            delta = jnp.sum(dov.astype(F32) * o_ref[rows, :].astype(F32), axis=-1, keepdims=True)
            ds = p * (dp - delta).astype(BF16)
            dq_ref[rows, :] = jnp.dot(ds, kv_, preferred_element_type=F32)
            p_scr[rows, :] = p
            ds_scr[rows, :] = ds
        dvp = lax.dot_general(p_scr[...], do_ref[...], TN_DIMS, preferred_element_type=F32)
        dkp = lax.dot_general(ds_scr[...], q_ref[...], TN_DIMS, preferred_element_type=F32)
        first = jnp.logical_and(g == 0, i == 0)

        @pl.when(first)
        def _():
            dv_acc[...] = dvp
            dk_acc[...] = dkp

        @pl.when(jnp.logical_not(first))
        def _():
            dv_acc[...] += dvp
            dk_acc[...] += dkp

        @pl.when(jnp.logical_and(g == grp - 1, i == nq - 1))
        def _():
            dk_ref[...] = dk_acc[...]
            dv_ref[...] = dv_acc[...].astype(dv_ref.dtype)

    qspec = pl.BlockSpec((tq, HEAD_DIM), lambda kv, g, i: (i, kv * grp + g))
    kspec = pl.BlockSpec((s, HEAD_DIM), lambda kv, g, i: (0, kv))
    return pl.pallas_call(
        body, name="attn_bwd", grid=(kvh, grp, nq),
        in_specs=[qspec, kspec, kspec, qspec, qspec, qspec],
        out_specs=[qspec, kspec, kspec],
        out_shape=[jax.ShapeDtypeStruct((s, aw), F32), jax.ShapeDtypeStruct(k.shape, F32),
                   jax.ShapeDtypeStruct(v.shape, BF16)],
        scratch_shapes=[pltpu.VMEM((s, HEAD_DIM), F32), pltpu.VMEM((s, HEAD_DIM), F32),
                        pltpu.VMEM((tq, s), BF16), pltpu.VMEM((tq, s), BF16)],
        compiler_params=_params(("parallel", "arbitrary", "arbitrary")),
    )(q, k, v, o, do, lse)


def _sum_all(z):
    return jnp.sum(jnp.sum(z, axis=0, keepdims=True), axis=1, keepdims=True)


def _chunk_consts(df_ref, db_ref, t):
    lf = _log_sigmoid(df_ref[0][:, :1])
    lb = _log_sigmoid(db_ref[0][:, :1])
    r = lax.broadcasted_iota(jnp.int32, (t, 1), 0).astype(F32)
    c = lax.broadcasted_iota(jnp.int32, (1, t), 1).astype(F32)
    diff = r - c
    dm = jnp.exp(diff * jnp.where(diff >= 0, lf, -lb))
    return dict(diff=diff, dm=dm, r=r,
                af=jnp.exp(lf * (r + 1.0)), bf=jnp.exp(lf * (t - 1.0 - r)), gf=jnp.exp(lf * t),
                ab=jnp.exp(lb * (t - r)), bb=jnp.exp(lb * r), gb=jnp.exp(lb * t))


def _scaled(x, f):
    return (x.astype(F32) * f).astype(BF16)


def _retc_specs(s):
    qspec = pl.BlockSpec((s, RET_QK_DIM), lambda h: (0, h))
    vspec = pl.BlockSpec((s, RET_V_DIM), lambda h: (0, h))
    dspec = pl.BlockSpec((1, 1, LANES), lambda h: (h, 0, 0))
    return qspec, vspec, dspec


def _retc_fwd(q, k, v, dec_f, dec_b, *, t=256):
    s, qw = q.shape
    t = _tile(s, t)
    heads, nc = qw // RET_QK_DIM, s // t
    qspec, vspec, dspec = _retc_specs(s)

    def body(q_ref, k_ref, v_ref, df_ref, db_ref, o_ref):
        cs = _chunk_consts(df_ref, db_ref, t)

        def rows_of(i):
            return pl.ds(pl.multiple_of(i * t, t), t)

        def forward(i, sf):
            rows = rows_of(i)
            qi, ki, vi = q_ref[rows, :], k_ref[rows, :], v_ref[rows, :]
            sc = lax.dot_general(qi, ki, NT_DIMS, preferred_element_type=F32)
            intra = jnp.dot((sc * cs["dm"]).astype(BF16), vi, preferred_element_type=F32)
            cross = jnp.dot(_scaled(qi, cs["af"]), sf.astype(BF16), preferred_element_type=F32)
            o_ref[rows, :] = intra + cross
            return cs["gf"] * sf + lax.dot_general(_scaled(ki, cs["bf"]), vi, TN_DIMS, preferred_element_type=F32)

        def backward(j, sb):
            rows = rows_of(nc - 1 - j)
            qi, ki, vi = q_ref[rows, :], k_ref[rows, :], v_ref[rows, :]
            o_ref[rows, :] += jnp.dot(_scaled(qi, cs["ab"]), sb.astype(BF16), preferred_element_type=F32)
            return cs["gb"] * sb + lax.dot_general(_scaled(ki, cs["bb"]), vi, TN_DIMS, preferred_element_type=F32)

        zero = jnp.zeros((RET_QK_DIM, RET_V_DIM), F32)
        lax.fori_loop(0, nc, forward, zero)
        lax.fori_loop(0, nc, backward, zero)

    return pl.pallas_call(
        body, name="ret_fwd", grid=(heads,),
        in_specs=[qspec, qspec, vspec, dspec, dspec],
        out_specs=vspec, out_shape=jax.ShapeDtypeStruct(v.shape, F32),
        compiler_params=_params(("parallel",)),
    )(q, k, v, dec_f, dec_b)


def _retc_bwd(q, k, v, do, dec_f, dec_b, *, t=256):
    s, qw = q.shape
    t = _tile(s, t)
    heads, nc = qw // RET_QK_DIM, s // t
    qspec, vspec, dspec = _retc_specs(s)
    gspec = pl.BlockSpec((1, 8, LANES), lambda h: (h, 0, 0))

    def body(q_ref, k_ref, v_ref, do_ref, df_ref, db_ref, dq_ref, dk_ref, dv_ref, gf_ref, gb_ref,
             sf_scr, sb_scr, dv_acc):
        cs = _chunk_consts(df_ref, db_ref, t)
        r, diff, dm = cs["r"], cs["diff"], cs["dm"]

        def rows_of(i):
            return pl.ds(pl.multiple_of(i * t, t), t)

        def tn(a, b):
            return lax.dot_general(a, b, TN_DIMS, preferred_element_type=F32)

        def nt(a, b):
            return lax.dot_general(a, b, NT_DIMS, preferred_element_type=F32)

        def states_f(i, sf):
            sf_scr[i] = sf
            rows = rows_of(i)
            return cs["gf"] * sf + tn(_scaled(k_ref[rows, :], cs["bf"]), v_ref[rows, :])

        def states_b(j, sb):
            i = nc - 1 - j
            sb_scr[i] = sb
            rows = rows_of(i)
            return cs["gb"] * sb + tn(_scaled(k_ref[rows, :], cs["bb"]), v_ref[rows, :])

        zero = jnp.zeros((RET_QK_DIM, RET_V_DIM), F32)
        lax.fori_loop(0, nc, states_f, zero)
        lax.fori_loop(0, nc, states_b, zero)

        def scan_grads(i, state, u, qf, kf, vi, doi, fa, fb, step, wa, wb):
            qa, kb = qf * fa, kf * fb
            ub = u.astype(BF16)
            dqa = nt(doi, state.astype(BF16))
            dkb = nt(vi, ub)
            dv = jnp.dot(kb.astype(BF16), ub, preferred_element_type=F32)
            dlog = _sum_all(dqa * qa * wa) + _sum_all(dkb * kb * wb) + t * step * _sum_all(u * state)
            u_new = step * u + tn(qa.astype(BF16), doi)
            return dqa * fa, dkb * fb, dv, u_new, dlog

        def sweep_f(j, carry):
            u, accf, accb = carry
            i = nc - 1 - j
            rows = rows_of(i)
            qi, ki, vi, doi = q_ref[rows, :], k_ref[rows, :], v_ref[rows, :], do_ref[rows, :]
            sc = nt(qi, ki)
            p = sc * dm
            dp = nt(doi, vi)
            ds = (dp * dm).astype(BF16)
            tt = dp * p * diff
            accf = accf + _sum_all(jnp.where(diff > 0, tt, 0.0))
            accb = accb + _sum_all(jnp.where(diff < 0, -tt, 0.0))
            dq1, dk1, dv1, u, dlog = scan_grads(i, sf_scr[i], u, qi.astype(F32), ki.astype(F32), vi, doi,
                                                cs["af"], cs["bf"], cs["gf"], r + 1.0, t - 1.0 - r)
            dq_ref[rows, :] = jnp.dot(ds, ki, preferred_element_type=F32) + dq1
            dk_ref[rows, :] = tn(ds, qi) + dk1
            dv_acc[rows, :] = tn(p.astype(BF16), doi) + dv1
            return u, accf + dlog, accb

        def sweep_b(i, carry):
            w, accb = carry
            rows = rows_of(i)
            qi, ki, vi, doi = q_ref[rows, :], k_ref[rows, :], v_ref[rows, :], do_ref[rows, :]
            dq1, dk1, dv1, w, dlog = scan_grads(i, sb_scr[i], w, qi.astype(F32), ki.astype(F32), vi, doi,
                                                cs["ab"], cs["bb"], cs["gb"], t - r, r)
            dq_ref[rows, :] += dq1
            dk_ref[rows, :] += dk1
            dv_acc[rows, :] += dv1
            return w, accb + dlog

        z11 = jnp.zeros((1, 1), F32)
        _, accf, accb = lax.fori_loop(0, nc, sweep_f, (zero, z11, z11))
        _, accb = lax.fori_loop(0, nc, sweep_b, (zero, accb))
        dv_ref[...] = dv_acc[...].astype(dv_ref.dtype)
        gf_ref[...] = jnp.broadcast_to((accf / (1.0 + jnp.exp(df_ref[0][:, :1]))).reshape(1, 1, 1), gf_ref.shape)
        gb_ref[...] = jnp.broadcast_to((accb / (1.0 + jnp.exp(db_ref[0][:, :1]))).reshape(1, 1, 1), gb_ref.shape)

    return pl.pallas_call(
        body, name="ret_bwd", grid=(heads,),
        in_specs=[qspec, qspec, vspec, vspec, dspec, dspec],
        out_specs=[qspec, qspec, vspec, gspec, gspec],
        out_shape=[jax.ShapeDtypeStruct(q.shape, F32), jax.ShapeDtypeStruct(k.shape, F32),
                   jax.ShapeDtypeStruct(v.shape, BF16),
                   jax.ShapeDtypeStruct((heads, 8, LANES), F32), jax.ShapeDtypeStruct((heads, 8, LANES), F32)],
        scratch_shapes=[pltpu.VMEM((nc, RET_QK_DIM, RET_V_DIM), F32), pltpu.VMEM((nc, RET_QK_DIM, RET_V_DIM), F32),
                        pltpu.VMEM((s, RET_V_DIM), F32)],
        compiler_params=_params(("parallel",)),
    )(q, k, v, do, dec_f, dec_b)


def _gate_fwd(att, ag, ret, rg, rnw, *, ts=256):
    s, aw = att.shape
    rw = ret.shape[1]
    ts = _tile(s, ts)
    rheads = rw // RET_V_DIM

    def body(att_ref, ag_ref, ret_ref, rg_ref, w_ref, y_ref):
        sa, _ = _silu_parts(ag_ref[...])
        y_ref[:, :aw] = (sa * att_ref[...].astype(F32)).astype(BF16)
        for h in range(rheads):
            sl = slice(h * RET_V_DIM, (h + 1) * RET_V_DIM)
            rt = ret_ref[:, sl]
            rn = rt * lax.rsqrt(jnp.mean(rt * rt, axis=-1, keepdims=True) + EPS) * w_ref[:, sl]
            sr, _ = _silu_parts(rg_ref[:, sl])
            y_ref[:, aw + h * RET_V_DIM:aw + (h + 1) * RET_V_DIM] = (sr * rn).astype(BF16)

    def row(w):
        return pl.BlockSpec((ts, w), lambda i: (i, 0))

    return pl.pallas_call(
        body, name="gate_fwd", grid=(s // ts,),
        in_specs=[row(aw), row(aw), row(rw), row(rw), pl.BlockSpec((1, rw), lambda i: (0, 0))],
        out_specs=row(aw + rw), out_shape=jax.ShapeDtypeStruct((s, aw + rw), BF16),
        compiler_params=_params(("parallel",)),
    )(att, ag, ret, rg, rnw.reshape(1, rw))


def _gate_bwd(dy, att, ag, ret, rg, rnw, *, ts=256):
    s, aw = att.shape
    rw = ret.shape[1]
    ts = _tile(s, ts)
    rheads = rw // RET_V_DIM

    def body(dy_ref, att_ref, ag_ref, ret_ref, rg_ref, w_ref, datt_ref, dag_ref, dret_ref, drg_ref, dw_ref):
        sa, dsa = _silu_parts(ag_ref[...])
        dya = dy_ref[:, :aw]
        datt_ref[...] = (dya * sa).astype(BF16)
        dag_ref[...] = (dya * att_ref[...].astype(F32) * dsa).astype(BF16)
        parts = []
        for h in range(rheads):
            sl = slice(h * RET_V_DIM, (h + 1) * RET_V_DIM)
            rt = ret_ref[:, sl]
            rr = lax.rsqrt(jnp.mean(rt * rt, axis=-1, keepdims=True) + EPS)
            rh = rt * rr
            wv = w_ref[:, sl]
            sr, dsr = _silu_parts(rg_ref[:, sl])
            dyr = dy_ref[:, aw + h * RET_V_DIM:aw + (h + 1) * RET_V_DIM]
            drg_ref[:, sl] = (dyr * rh * wv * dsr).astype(BF16)
            drn = dyr * sr
            dn = drn * wv
            dret_ref[:, sl] = (rr * (dn - rh * jnp.mean(dn * rh, axis=-1, keepdims=True))).astype(BF16)
            parts.append(jnp.sum(drn * rh, axis=0, keepdims=True))
        part = jnp.concatenate(parts, axis=-1)

        @pl.when(pl.program_id(0) == 0)
        def _():
            dw_ref[...] = part

        @pl.when(pl.program_id(0) > 0)
        def _():
            dw_ref[...] += part

    def row(w):
        return pl.BlockSpec((ts, w), lambda i: (i, 0))

    vec = pl.BlockSpec((1, rw), lambda i: (0, 0))
    return pl.pallas_call(
        body, name="gate_bwd", grid=(s // ts,),
        in_specs=[row(aw + rw), row(aw), row(aw), row(rw), row(rw), vec],
        out_specs=[row(aw), row(aw), row(rw), row(rw), vec],
        out_shape=[jax.ShapeDtypeStruct((s, aw), BF16), jax.ShapeDtypeStruct((s, aw), BF16),
                   jax.ShapeDtypeStruct((s, rw), BF16), jax.ShapeDtypeStruct((s, rw), BF16),
                   jax.ShapeDtypeStruct((1, rw), F32)],
        compiler_params=_params(("arbitrary",)),
    )(dy, att, ag, ret, rg, rnw.reshape(1, rw))


def _mesh_position():
    x, y, c = lax.axis_index("x"), lax.axis_index("y"), lax.axis_index("c")
    return x, y, c, 4 * x + 2 * y + c


def _peer(x, y, c, k):
    px = 1 - x if k & 4 else x
    py = 1 - y if k & 2 else y
    pc = 1 - c if k & 1 else c
    return (px, py, pc), 4 * px + 2 * py + pc


HBM_SPEC = pl.BlockSpec(memory_space=pltpu.HBM)
SEM_SPEC = pl.BlockSpec(memory_space=pltpu.SEMAPHORE)
ANY_SPEC = pl.BlockSpec(memory_space=pl.ANY)
DATAFLOW = pltpu.SideEffectType.DATAFLOW_SIDE_EFFECTING
N_SPLIT_COPIES = N_DEV - 1


def _hbm(a):
    return pltpu.with_memory_space_constraint(a, pltpu.HBM)


def _split_start(name, copies, src, land, after):
    def body(*refs):
        (send_sems, recv_sems), token = refs[2 + len(after):4 + len(after)], refs[-1]
        sends, _ = copies(refs[0], refs[1], send_sems, recv_sems)
        for cp in sends:
            cp.start()
        token[...] = jnp.zeros_like(token)

    return pl.pallas_call(
        body, name=name,
        out_shape=(pltpu.SemaphoreType.DMA((N_SPLIT_COPIES,)), pltpu.SemaphoreType.DMA((N_SPLIT_COPIES,)),
                   pltpu.HBM(src.shape, src.dtype), pltpu.HBM(land.shape, land.dtype),
                   jax.ShapeDtypeStruct((8, LANES), F32)),
        in_specs=[HBM_SPEC] * 2 + [ANY_SPEC] * len(after),
        out_specs=(SEM_SPEC, SEM_SPEC, HBM_SPEC, HBM_SPEC, pl.BlockSpec(memory_space=pltpu.VMEM)),
        input_output_aliases={0: 2, 1: 3},
        compiler_params=pltpu.CompilerParams(has_side_effects=DATAFLOW),
    )(_hbm(src), _hbm(land), *after)


def _split_wait(name, copies, started, after):
    send_sems, recv_sems, src, land = started[:4]

    def body(*refs):
        sends, recvs = copies(refs[0], refs[1], refs[2], refs[3])
        for cp in sends:
            cp.wait_send()
        for cp in recvs:
            cp.wait_recv()

    return pl.pallas_call(
        body, name=name,
        out_shape=(pltpu.HBM(src.shape, src.dtype), pltpu.HBM(land.shape, land.dtype)),
        in_specs=[HBM_SPEC] * 2 + [SEM_SPEC, SEM_SPEC] + [ANY_SPEC] * len(after),
        out_specs=(HBM_SPEC,) * 2,
        input_output_aliases={0: 0, 1: 1},
        compiler_params=pltpu.CompilerParams(has_side_effects=DATAFLOW),
    )(src, land, send_sems, recv_sems, *after)[1]


def _slab(ref, p, size, axis):
    if axis == 1:
        return ref.at[:, pl.ds(pl.multiple_of(p * size, LANES), size)]
    return ref.at[pl.ds(pl.multiple_of(p * size, 16), size), :]


def _gather_copies(size, axis):
    def copies(shard_ref, full_ref, send_sems, recv_sems):
        x, y, c, me = _mesh_position()
        sends, recvs = [], []
        for k in range(1, N_DEV):
            peer, pid = _peer(x, y, c, k)
            sends.append(pltpu.make_async_remote_copy(
                src_ref=shard_ref, dst_ref=_slab(full_ref, me, size, axis), send_sem=send_sems.at[k - 1],
                recv_sem=recv_sems.at[k - 1], device_id=peer, device_id_type=MESH))
            recvs.append(pltpu.make_async_remote_copy(
                src_ref=shard_ref, dst_ref=_slab(full_ref, pid, size, axis), send_sem=send_sems.at[k - 1],
                recv_sem=recv_sems.at[k - 1], device_id=peer, device_id_type=MESH))
        return sends, recvs

    return copies


def _scatter_copies(size, axis):
    def copies(grad_ref, land_ref, send_sems, recv_sems):
        x, y, c, me = _mesh_position()
        sends, recvs = [], []
        for k in range(1, N_DEV):
            peer, pid = _peer(x, y, c, k)
            src = _slab(grad_ref, pid, size, axis)
            sends.append(pltpu.make_async_remote_copy(
                src_ref=src, dst_ref=land_ref.at[me], send_sem=send_sems.at[k - 1], recv_sem=recv_sems.at[k - 1],
                device_id=peer, device_id_type=MESH))
            recvs.append(pltpu.make_async_remote_copy(
                src_ref=src, dst_ref=land_ref.at[pid], send_sem=send_sems.at[k - 1], recv_sem=recv_sems.at[k - 1],
                device_id=peer, device_id_type=MESH))
        return sends, recvs

    return copies


PLACE_CHUNKS = 16


def _place_own(name, src, out_shape, pair_of, rows):
    band = rows // PLACE_CHUNKS
    assert band % 16 == 0 and band * PLACE_CHUNKS == rows, rows

    def body(src_ref, out_ref, sems):
        _, _, _, me = _mesh_position()
        cps = [pltpu.make_async_copy(*pair_of(src_ref, out_ref, me, j * band, band), sems.at[j])
               for j in range(PLACE_CHUNKS)]
        for cp in cps:
            cp.start()
        for cp in cps:
            cp.wait()

    return pl.pallas_call(
        body, name=name, in_specs=[ANY_SPEC], out_specs=ANY_SPEC, out_shape=out_shape,
        scratch_shapes=[pltpu.SemaphoreType.DMA((PLACE_CHUNKS,))],
    )(src)


def _gather_start(shard, axis, after, tag):
    size = shard.shape[axis]
    full_shape = tuple(N_DEV * n if a == axis else n for a, n in enumerate(shard.shape))

    def pair(src_ref, out_ref, me, start, n):
        if axis == 1:
            return src_ref.at[pl.ds(start, n), :], out_ref.at[pl.ds(start, n), pl.ds(pl.multiple_of(me * size, LANES), size)]
        return src_ref.at[pl.ds(start, n), :], out_ref.at[pl.ds(pl.multiple_of(me * size + start, 16), n), :]

    full = _place_own("place_shard_" + tag, shard, jax.ShapeDtypeStruct(full_shape, shard.dtype), pair, shard.shape[0])
    return _split_start("gather_start_" + tag, _gather_copies(size, axis), shard, full, after)


def _gather_wait(started, axis, after, tag):
    size = started[2].shape[axis]
    return _split_wait("gather_wait_" + tag, _gather_copies(size, axis), started, after)


def _scatter_start(grad, axis, tag):
    size = grad.shape[axis] // N_DEV
    slab_shape = tuple(size if a == axis else n for a, n in enumerate(grad.shape))

    def pair(src_ref, out_ref, me, start, n):
        if axis == 1:
            return (src_ref.at[pl.ds(start, n), pl.ds(pl.multiple_of(me * size, LANES), size)],
                    out_ref.at[me, pl.ds(start, n), :])
        return src_ref.at[pl.ds(pl.multiple_of(me * size + start, 16), n), :], out_ref.at[me, pl.ds(start, n), :]

    land = _place_own("place_slab_" + tag, grad, jax.ShapeDtypeStruct((N_DEV,) + slab_shape, grad.dtype), pair,
                      slab_shape[0])
    return _split_start("scatter_start_" + tag, _scatter_copies(size, axis), grad, land, [])


def _scatter_wait(started, axis, after, tag):
    size = started[2].shape[axis] // N_DEV
    return _split_wait("scatter_wait_" + tag, _scatter_copies(size, axis), started, after)


def _exchange_small(buf, *, name, after=()):
    r = buf.shape[0]

    def body(*refs):
        buf_ref = refs[0]
        all_ref, sum_ref, send_sems, recv_sems = refs[1 + len(after):]
        x, y, c, me = _mesh_position()
        all_ref[me] = buf_ref[...]
        sends, recvs = [], []
        for k in range(1, N_DEV):
            peer, pid = _peer(x, y, c, k)
            sends.append(pltpu.make_async_remote_copy(
                src_ref=buf_ref, dst_ref=all_ref.at[me], send_sem=send_sems.at[k - 1], recv_sem=recv_sems.at[k - 1],
                device_id=peer, device_id_type=MESH))
            recvs.append(pltpu.make_async_remote_copy(
                src_ref=buf_ref, dst_ref=all_ref.at[pid], send_sem=send_sems.at[k - 1], recv_sem=recv_sems.at[k - 1],
                device_id=peer, device_id_type=MESH))
        for cp in sends:
            cp.start()
        for cp in recvs:
            cp.wait_recv()
        for cp in sends:
            cp.wait_send()
        total = all_ref[0]
        for p in range(1, N_DEV):
            total = total + all_ref[p]
        sum_ref[...] = total

    vmem = pl.BlockSpec(memory_space=pltpu.VMEM)
    return pl.pallas_call(
        body, name=name,
        in_specs=[vmem] + [ANY_SPEC] * len(after), out_specs=[vmem, vmem],
        out_shape=[jax.ShapeDtypeStruct((N_DEV, r, LANES), F32), jax.ShapeDtypeStruct((r, LANES), F32)],
        scratch_shapes=[pltpu.SemaphoreType.DMA((N_DEV - 1,)), pltpu.SemaphoreType.DMA((N_DEV - 1,))],
        compiler_params=pltpu.CompilerParams(has_side_effects=True),
    )(buf, *after)


def _adamw_math(w, g, m, v):
    m2 = ADAM_B1 * m + (1.0 - ADAM_B1) * g
    v2 = ADAM_B2 * v + (1.0 - ADAM_B2) * (g * g)
    delta = -ADAM_LR * ((m2 / ADAM_C1) / (jnp.sqrt(v2 / ADAM_C2) + ADAM_EPS) + ADAM_WD * w)
    return delta, m2, v2


def _adamw_slabs(layer, w, m, v, land, outs, order, *, tr, name):
    depth, r, c = w.shape
    tr = _tile(r, tr)

    def body(w_ref, m_ref, v_ref, land_ref, order_ref, o0, o1, o2, o3, g_ref, d_ref, m2_ref, v2_ref):
        g = land_ref[0].astype(F32)
        for p in range(1, N_DEV):
            g = g + land_ref[p].astype(F32)
        delta, m2, v2 = _adamw_math(w_ref[...], g, m_ref[...], v_ref[...])
        g_ref[...] = g
        d_ref[...] = delta
        m2_ref[...] = m2
        v2_ref[...] = v2

    row = pl.BlockSpec((None, tr, c), lambda i: (layer, i, 0))
    return pl.pallas_call(
        body, name=name, grid=(r // tr,),
        in_specs=[row, row, row, pl.BlockSpec((N_DEV, tr, c), lambda i: (0, i, 0)),
                  pl.BlockSpec((8, LANES), lambda i: (0, 0))] + [ANY_SPEC] * 4,
        out_specs=[row] * 4, out_shape=[jax.ShapeDtypeStruct((depth, r, c), F32)] * 4,
        input_output_aliases={5: 0, 6: 1, 7: 2, 8: 3},
        compiler_params=_params(("parallel",)),
    )(w, m, v, land, order, *outs)


def _adamw_small(w, g, m, v):
    def body(w_ref, g_ref, m_ref, v_ref, d_ref, m2_ref, v2_ref):
        delta, m2, v2 = _adamw_math(w_ref[...], g_ref[...], m_ref[...], v_ref[...])
        d_ref[...] = delta
        m2_ref[...] = m2
        v2_ref[...] = v2

    vmem = pl.BlockSpec(memory_space=pltpu.VMEM)
    return pl.pallas_call(
        body, name="adamw_small", in_specs=[vmem] * 4, out_specs=[vmem] * 3,
        out_shape=[jax.ShapeDtypeStruct(w.shape, F32)] * 3,
    )(w, g, m, v)


def _pack(parts):
    flat = jnp.concatenate([p.reshape(-1).astype(F32) for p in parts])
    rows = -(-flat.shape[0] // LANES)
    rows = -(-rows // SMALL_ROWS_ALIGN) * SMALL_ROWS_ALIGN
    flat = jnp.pad(flat, (0, rows * LANES - flat.shape[0]))
    return flat.reshape(rows, LANES)


def _unpack(buf, shapes):
    flat = buf.reshape(-1)
    out, pos = [], 0
    for shp in shapes:
        size = math.prod(shp)
        out.append(flat[pos:pos + size].reshape(shp))
        pos += size
    return out


def _section_widths(d):
    aw = d // 2
    kw = aw // ATTN_GROUP
    rw = d - aw
    rqw = (rw // RET_V_DIM) * RET_QK_DIM
    return (aw, kw, kw, aw, rqw, rqw, rw, rw)


def _layer_fwd(xl, nw, win_full, wout_of, qn, kn, dec_f, dec_b, rn, cos, sin):
    widths = _section_widths(xl.shape[1])
    offs = tuple(int(o) for o in np.cumsum((0,) + widths)[:-1])
    sec_tn = _tile(widths[1], 512)
    h = _rms_fwd(xl, nw)
    secs = [_matmul(h, win_full, name="proj", b_off=offs[i], n=widths[i], tn=sec_tn) for i in range(8)]
    aq, ak, av, ag, rq, rk, rv, rg = secs
    q, k, v, rqr, rkr, rvb = _prep_fwd(aq, ak, av, rq, rk, rv, cos, sin, qn, kn)
    att, lse = _attn_fwd(q, k, v)
    ret = _retc_fwd(rqr, rkr, rvb, dec_f, dec_b)
    y = _gate_fwd(att, ag, ret, rg, rn)
    wout_full = wout_of(y)
    xn = _matmul(y, wout_full, name="out_proj", residual=xl)
    saved = dict(x=xl, h=h, aq=aq, ak=ak, ag=ag, rg=rg, q=q, k=k, v=v, rq=rqr, rk=rkr, rv=rvb,
                 att=att, lse=lse, ret=ret, y=y, win=win_full, wout=wout_full)
    return xn, saved


def _layer_bwd_weights(gb, sv, qn, kn, dec_f, dec_b, rn, cos, sin, on_dwout):
    dy = _matmul(gb, sv["wout"], name="d_y", trans_b=True)
    dwout = _matmul(sv["y"].T, gb, name="d_wout", out_dtype=BF16)
    datt, dag, dret, drg, drn = _gate_bwd(dy, sv["att"], sv["ag"], sv["ret"], sv["rg"], rn + on_dwout(dwout))
    dq, dk, dav = _attn_bwd(sv["q"], sv["k"], sv["v"], sv["att"], datt, sv["lse"])
    drq, drk, drv, gf, gbk = _retc_bwd(sv["rq"], sv["rk"], sv["rv"], dret, dec_f, dec_b)
    daq, dak, drq_p, drk_p, dqn, dkn = _prep_bwd(dq, dk, drq, drk, sv["aq"], sv["ak"], cos, sin, qn, kn)
    dproj = jnp.concatenate([daq, dak, dav, dag, drq_p, drk_p, drv, drg], axis=-1)
    dwin = _matmul(sv["h"].T, dproj, name="d_win", out_dtype=BF16)
    small = dict(qn=dqn[0], kn=dkn[0], df=gf[:, 0, 0], db=gbk[:, 0, 0], rn=drn[0])
    return dproj, dwin, small


def _layer_bwd_input(g, dproj, sv, nw):
    dh = _matmul(dproj, sv["win"], name="d_h", trans_b=True, tn=1024, tk=_tile(dproj.shape[1], 1408, LANES))
    g, gb, dnw = _rms_bwd(dh, sv["x"], g, nw)
    return g, gb, dnw[0]


def kernel(x, norm_w, w_in, q_norm, k_norm, ret_decay_fwd, ret_decay_bwd, ret_norm, w_out, final_norm, loss_target, m_norm_w, m_w_in, m_q_norm, m_k_norm, m_ret_decay_fwd, m_ret_decay_bwd, m_ret_norm, m_w_out, m_final_norm, v_norm_w, v_w_in, v_q_norm, v_k_norm, v_ret_decay_fwd, v_ret_decay_bwd, v_ret_norm, v_w_out, v_final_norm):
    depth, d, _ = w_in.shape
    seq = x.shape[1]
    rw = _section_widths(d)[6]
    rheads = rw // RET_V_DIM
    rns = ret_norm.shape[-1]
    _, _, _, me = _mesh_position()

    target = loss_target[0]
    cos, sin = _rope_tables(seq)

    rn_all, _ = _exchange_small(_pack([ret_norm]), name="gather_ret_norm")
    rn_full = rn_all.reshape(N_DEV, -1)[:, :depth * rheads * rns].reshape(N_DEV, depth, rheads, rns)
    rn_full = jnp.transpose(rn_full, (1, 2, 0, 3)).reshape(depth, rw)

    dec_f = jnp.broadcast_to(ret_decay_fwd[:, :, None, None], (depth, rheads, 1, LANES))
    dec_b = jnp.broadcast_to(ret_decay_bwd[:, :, None, None], (depth, rheads, 1, LANES))

    win_bf = w_in.astype(BF16)
    wout_bf = w_out.astype(BF16)

    saved = []
    xl = x[0]
    win_full = _gather_wait(_gather_start(win_bf[0], 1, [], "in0"), 1, [], "in0")
    for l in range(depth):
        out_sent = _gather_start(wout_bf[l], 0, [win_full], "out" + str(l))
        nw = norm_w[l] + out_sent[-1][0, 0]
        if l + 1 < depth:
            in_sent = _gather_start(win_bf[l + 1], 1, [win_full, out_sent[-1]], "in" + str(l + 1))
            nw = nw + in_sent[-1][0, 0]

        def wout_of(y, out_sent=out_sent, l=l):
            return _gather_wait(out_sent, 0, [y], "out" + str(l))

        xl, sv = _layer_fwd(xl, nw, win_full, wout_of, q_norm[l], k_norm[l], dec_f[l], dec_b[l],
                            rn_full[l], cos, sin)
        saved.append(sv)
        if l + 1 < depth:
            win_full = _gather_wait(in_sent, 1, [xl], "in" + str(l + 1))

    loss_row, g, gb, d_final = _loss_head(xl, target, final_norm)

    d_norm, d_qn, d_kn, d_df, d_db, d_rn = [], [], [], [], [], []
    lands = [None] * depth
    pending = None
    for l in reversed(range(depth)):
        sent = {}

        def on_dwout(dwout, sent=sent, l=l):
            sent["out"] = _scatter_start(dwout, 0, "out" + str(l))
            return sent["out"][-1][0, 0]

        dproj, dwin, sm = _layer_bwd_weights(gb, saved[l], q_norm[l], k_norm[l], dec_f[l], dec_b[l],
                                             rn_full[l], cos, sin, on_dwout)
        sent["in"] = _scatter_start(dwin, 1, "in" + str(l))
        g, gb, dnw = _layer_bwd_input(g, dproj, saved[l], norm_w[l] + sent["in"][-1][0, 0])
        if pending is not None:
            lands[l + 1] = (_scatter_wait(pending["in"], 1, [g], "in" + str(l + 1)),
                            _scatter_wait(pending["out"], 0, [g], "out" + str(l + 1)))
        pending = sent
        d_norm.append(dnw)
        d_qn.append(sm["qn"])
        d_kn.append(sm["kn"])
        d_df.append(sm["df"])
        d_db.append(sm["db"])
        d_rn.append(sm["rn"])
    for lst in (d_norm, d_qn, d_kn, d_df, d_db, d_rn):
        lst.reverse()
    order = pending["in"][-1]
    in_outs = [lax.empty(w_in.shape, F32) for _ in range(4)]
    out_outs = [lax.empty(w_out.shape, F32) for _ in range(4)]
    for l in reversed(range(depth)):
        if l == 0:
            lands[0] = (_scatter_wait(pending["in"], 1, [g, in_outs[0], out_outs[0]], "in0"),
                        _scatter_wait(pending["out"], 0, [g], "out0"))
        in_outs = _adamw_slabs(l, w_in, m_w_in, v_w_in, lands[l][0], in_outs, order, tr=256, name="adamw_w_in")
        out_outs = _adamw_slabs(l, w_out, m_w_out, v_w_out, lands[l][1], out_outs, order, tr=64, name="adamw_w_out")

    small_shapes = [(depth, d), (depth, HEAD_DIM), (depth, HEAD_DIM), (depth, rheads), (depth, rheads),
                    (depth, rheads, N_DEV * rns), (d,), (1,)]
    grads_local = [jnp.stack(d_norm), jnp.stack(d_qn), jnp.stack(d_kn), jnp.stack(d_df), jnp.stack(d_db),
                   jnp.stack(d_rn).reshape(depth, rheads, N_DEV * rns), d_final[0], loss_row[0, :1]]
    _, gsum = _exchange_small(_pack(grads_local), name="all_reduce_small", after=(in_outs[0], out_outs[0]))
    g_norm, g_qn, g_kn, g_df, g_db, g_rn_full, g_final, loss = _unpack(gsum, small_shapes)
    g_rn = lax.dynamic_slice_in_dim(g_rn_full, me * rns, rns, axis=2)
    small_g = [g_norm, g_qn, g_kn, g_df, g_db, g_rn, g_final]
    small_w = [norm_w, q_norm, k_norm, ret_decay_fwd, ret_decay_bwd, ret_norm, final_norm]
    small_m = [m_norm_w, m_q_norm, m_k_norm, m_ret_decay_fwd, m_ret_decay_bwd, m_ret_norm, m_final_norm]
    small_v = [v_norm_w, v_q_norm, v_k_norm, v_ret_decay_fwd, v_ret_decay_bwd, v_ret_norm, v_final_norm]
    shapes = [a.shape for a in small_w]
    sd, sm, sv2 = _adamw_small(_pack(small_w), _pack(small_g), _pack(small_m), _pack(small_v))
    small_d, small_m2, small_v2 = _unpack(sd, shapes), _unpack(sm, shapes), _unpack(sv2, shapes)

    def ordered(small, win_v, wout_v):
        return [small[0], win_v, small[1], small[2], small[3], small[4], small[5], wout_v, small[6]]

    grads = ordered(small_g, in_outs[0], out_outs[0])
    deltas = ordered(small_d, in_outs[1], out_outs[1])
    new_m = ordered(small_m2, in_outs[2], out_outs[2])
    new_v = ordered(small_v2, in_outs[3], out_outs[3])
    return (loss.reshape(()), g[None], *grads, *deltas, *new_m, *new_v)
```

```python
import functools
import math

import jax
import jax.numpy as jnp
import numpy as np
from jax import lax
from jax.experimental import pallas as pl
from jax.experimental.pallas import tpu as pltpu

F32 = jnp.float32
BF16 = jnp.bfloat16

N_DEV = 8
HEAD_DIM = 128
ATTN_GROUP = 4
RET_QK_DIM = 128
RET_V_DIM = 256
GRID_W = 64
ROPE_THETA = 10000.0
EPS = 1e-6
ADAM_LR = 0.001
ADAM_B1 = 0.9
ADAM_B2 = 0.999
ADAM_EPS = 1e-08
ADAM_WD = 0.01
ADAM_STEP = 10
ADAM_C1 = 1.0 - ADAM_B1 ** ADAM_STEP
ADAM_C2 = 1.0 - ADAM_B2 ** ADAM_STEP
LANES = 128
SMALL_ROWS_ALIGN = 8
VMEM_LIMIT = 56 * 1024 * 1024

NT_DIMS = (((1,), (1,)), ((), ()))
TN_DIMS = (((0,), (0,)), ((), ()))
MESH = pl.DeviceIdType.MESH


def _params(sem):
    return pltpu.CompilerParams(dimension_semantics=sem, vmem_limit_bytes=VMEM_LIMIT)


def _tile(dim, pref, align=16):
    if dim <= pref:
        return dim
    for t in range(pref - pref % align, 0, -align):
        if dim % t == 0:
            return t
    raise ValueError((dim, pref, align))


def _silu_parts(z):
    sg = 1.0 / (1.0 + jnp.exp(-z))
    return z * sg, sg * (1.0 + z * (1.0 - sg))


def _log_sigmoid(x):
    return jnp.minimum(x, 0.0) - jnp.log(1.0 + jnp.exp(-jnp.abs(x)))


def _swap_pairs(z):
    lane = lax.broadcasted_iota(jnp.int32, z.shape, 1)
    return jnp.where((lane % 64) < 32, pltpu.roll(z, 96, 1), pltpu.roll(z, 32, 1))


def _rope(z, cos, sin):
    return z * cos + _swap_pairs(z) * sin


def _rope_transposed(d, cos, sin):
    return d * cos + _swap_pairs(d * sin)


def _rope_tables(seq):
    rows = seq // GRID_W
    row = jnp.repeat(jnp.arange(rows), GRID_W).astype(F32)
    col = jnp.tile(jnp.arange(GRID_W), rows).astype(F32)
    axis_dim = HEAD_DIM // 2
    inv = ROPE_THETA ** (-jnp.arange(0, axis_dim, 2, dtype=F32) / axis_dim)
    ar = row[:, None] * inv[None, :]
    ac = col[:, None] * inv[None, :]
    cos = jnp.concatenate([jnp.cos(ar), jnp.cos(ar), jnp.cos(ac), jnp.cos(ac)], axis=-1)
    sin = jnp.concatenate([-jnp.sin(ar), jnp.sin(ar), -jnp.sin(ac), jnp.sin(ac)], axis=-1)
    return cos, sin


def _matmul(a, b, *, name, trans_b=False, b_off=0, n=None, out_dtype=F32, residual=None,
            tm=1024, tn=512, tk=4096):
    m, k = a.shape
    if n is None:
        n = b.shape[0] if trans_b else b.shape[1]
    tm, tn, tk = _tile(m, tm), _tile(n, tn, LANES), _tile(k, tk, LANES)
    assert b_off % tn == 0
    joff = b_off // tn
    nk = k // tk
    has_res = residual is not None

    def body(*refs):
        if has_res:
            a_ref, b_ref, r_ref, o_ref = refs[:4]
        else:
            a_ref, b_ref, o_ref = refs[:3]
        if trans_b:
            part = lax.dot_general(a_ref[...], b_ref[...], NT_DIMS, preferred_element_type=F32)
        else:
            part = jnp.dot(a_ref[...], b_ref[...], preferred_element_type=F32)

        def finish(r):
            if has_res:
                r = r + r_ref[...]
            o_ref[...] = r.astype(o_ref.dtype)

        if nk == 1:
            finish(part)
        else:
            acc_ref = refs[-1]
            kk = pl.program_id(2)

            @pl.when(kk == 0)
            def _():
                acc_ref[...] = part

            @pl.when(kk > 0)
            def _():
                acc_ref[...] += part

            @pl.when(kk == nk - 1)
            def _():
                finish(acc_ref[...])

    if trans_b:
        b_spec = pl.BlockSpec((tn, tk), lambda i, j, kk: (j + joff, kk))
    else:
        b_spec = pl.BlockSpec((tk, tn), lambda i, j, kk: (kk, j + joff))
    in_specs = [pl.BlockSpec((tm, tk), lambda i, j, kk: (i, kk)), b_spec]
    args = [a, b]
    if has_res:
        in_specs.append(pl.BlockSpec((tm, tn), lambda i, j, kk: (i, j)))
        args.append(residual)
    return pl.pallas_call(
        body, name=name, grid=(m // tm, n // tn, nk),
        in_specs=in_specs,
        out_specs=pl.BlockSpec((tm, tn), lambda i, j, kk: (i, j)),
        out_shape=jax.ShapeDtypeStruct((m, n), out_dtype),
        scratch_shapes=[pltpu.VMEM((tm, tn), F32)] if nk > 1 else [],
        compiler_params=_params(("parallel", "parallel", "arbitrary")),
    )(*args)


def _rms_fwd(x, w, *, ts=256):
    s, d = x.shape
    ts = _tile(s, ts)

    def body(x_ref, w_ref, h_ref):
        xv = x_ref[...]
        r = lax.rsqrt(jnp.mean(xv * xv, axis=-1, keepdims=True) + EPS)
        h_ref[...] = (xv * r * w_ref[...]).astype(h_ref.dtype)

    row = pl.BlockSpec((ts, d), lambda i: (i, 0))
    return pl.pallas_call(
        body, name="rms_fwd", grid=(s // ts,),
        in_specs=[row, pl.BlockSpec((1, d), lambda i: (0, 0))],
        out_specs=row, out_shape=jax.ShapeDtypeStruct((s, d), BF16),
        compiler_params=_params(("parallel",)),
    )(x, w.reshape(1, d))


def _rms_bwd(dh, x, g, w, *, ts=256):
    s, d = x.shape
    ts = _tile(s, ts)

    def body(dh_ref, x_ref, g_ref, w_ref, dx_ref, dxb_ref, dw_ref):
        xv = x_ref[...]
        r = lax.rsqrt(jnp.mean(xv * xv, axis=-1, keepdims=True) + EPS)
        xh = xv * r
        dhv = dh_ref[...]
        dn = dhv * w_ref[...]
        dx = g_ref[...] + r * (dn - xh * jnp.mean(dn * xh, axis=-1, keepdims=True))
        dx_ref[...] = dx
        dxb_ref[...] = dx.astype(BF16)
        part = jnp.sum(dhv * xh, axis=0, keepdims=True)

        @pl.when(pl.program_id(0) == 0)
        def _():
            dw_ref[...] = part

        @pl.when(pl.program_id(0) > 0)
        def _():
            dw_ref[...] += part

    row = pl.BlockSpec((ts, d), lambda i: (i, 0))
    vec = pl.BlockSpec((1, d), lambda i: (0, 0))
    return pl.pallas_call(
        body, name="rms_bwd", grid=(s // ts,),
        in_specs=[row, row, row, vec],
        out_specs=[row, row, vec],
        out_shape=[jax.ShapeDtypeStruct((s, d), F32), jax.ShapeDtypeStruct((s, d), BF16),
                   jax.ShapeDtypeStruct((1, d), F32)],
        compiler_params=_params(("arbitrary",)),
    )(dh, x, g, w.reshape(1, d))


def _loss_head(x, target, w, *, ts=256):
    s, d = x.shape
    ts = _tile(s, ts)

    def body(x_ref, t_ref, w_ref, loss_ref, dx_ref, dxb_ref, dw_ref):
        xv = x_ref[...]
        r = lax.rsqrt(jnp.mean(xv * xv, axis=-1, keepdims=True) + EPS)
        xh = xv * r
        wv = w_ref[...]
        diff = xh * wv - t_ref[...]
        lpart = 0.5 * jnp.sum(jnp.mean(diff * diff, axis=-1, keepdims=True), axis=0, keepdims=True)
        dout = diff * (1.0 / d)
        dn = dout * wv
        dx = r * (dn - xh * jnp.mean(dn * xh, axis=-1, keepdims=True))
        dx_ref[...] = dx
        dxb_ref[...] = dx.astype(BF16)
        part = jnp.sum(dout * xh, axis=0, keepdims=True)
        lrow = jnp.broadcast_to(lpart, loss_ref.shape)

        @pl.when(pl.program_id(0) == 0)
        def _():
            dw_ref[...] = part
            loss_ref[...] = lrow

        @pl.when(pl.program_id(0) > 0)
        def _():
            dw_ref[...] += part
            loss_ref[...] += lrow

    row = pl.BlockSpec((ts, d), lambda i: (i, 0))
    vec = pl.BlockSpec((1, d), lambda i: (0, 0))
    return pl.pallas_call(
        body, name="loss_head", grid=(s // ts,),
        in_specs=[row, row, vec],
        out_specs=[pl.BlockSpec((1, LANES), lambda i: (0, 0)), row, row, vec],
        out_shape=[jax.ShapeDtypeStruct((1, LANES), F32), jax.ShapeDtypeStruct((s, d), F32),
                   jax.ShapeDtypeStruct((s, d), BF16), jax.ShapeDtypeStruct((1, d), F32)],
        compiler_params=_params(("arbitrary",)),
    )(x, target, w.reshape(1, d))


def _prep_fwd(aq, ak, av, rq, rk, rv, cos, sin, qw, kw, *, ts=256):
    s = aq.shape[0]
    ts = _tile(s, ts)
    attn_scale = HEAD_DIM ** -0.5
    ret_scale = RET_QK_DIM ** -0.5
    nq, nk, nr = aq.shape[1] // HEAD_DIM, ak.shape[1] // HEAD_DIM, rq.shape[1] // RET_QK_DIM

    def body(aq_ref, ak_ref, av_ref, rq_ref, rk_ref, rv_ref, cos_ref, sin_ref, qw_ref, kw_ref,
             q_out, k_out, v_out, rq_out, rk_out, rv_out):
        c, sn = cos_ref[...], sin_ref[...]

        def normed(u, w):
            return u * lax.rsqrt(jnp.mean(u * u, axis=-1, keepdims=True) + EPS) * w

        for j in range(nq):
            sl = slice(j * HEAD_DIM, (j + 1) * HEAD_DIM)
            q_out[:, sl] = (_rope(normed(aq_ref[:, sl], qw_ref[...]), c, sn) * attn_scale).astype(BF16)
        for j in range(nk):
            sl = slice(j * HEAD_DIM, (j + 1) * HEAD_DIM)
            k_out[:, sl] = _rope(normed(ak_ref[:, sl], kw_ref[...]), c, sn).astype(BF16)
        for j in range(nr):
            sl = slice(j * RET_QK_DIM, (j + 1) * RET_QK_DIM)
            rq_out[:, sl] = _rope(rq_ref[:, sl], c, sn).astype(BF16)
            rk_out[:, sl] = (_rope(rk_ref[:, sl], c, sn) * ret_scale).astype(BF16)
        v_out[...] = av_ref[...].astype(BF16)
        rv_out[...] = rv_ref[...].astype(BF16)

    def row(arr):
        return pl.BlockSpec((ts, arr.shape[1]), lambda i: (i, 0))

    vec = pl.BlockSpec((1, HEAD_DIM), lambda i: (0, 0))
    ins = [aq, ak, av, rq, rk, rv]
    return pl.pallas_call(
        body, name="prep_fwd", grid=(s // ts,),
        in_specs=[row(a) for a in ins] + [row(cos), row(sin), vec, vec],
        out_specs=[row(a) for a in ins],
        out_shape=[jax.ShapeDtypeStruct(a.shape, BF16) for a in ins],
        compiler_params=_params(("parallel",)),
    )(*ins, cos, sin, qw.reshape(1, HEAD_DIM), kw.reshape(1, HEAD_DIM))


def _prep_bwd(dq, dk, drq, drk, aq, ak, cos, sin, qw, kw, *, ts=256):
    s = aq.shape[0]
    ts = _tile(s, ts)
    attn_scale = HEAD_DIM ** -0.5
    ret_scale = RET_QK_DIM ** -0.5
    nq, nk, nr = aq.shape[1] // HEAD_DIM, ak.shape[1] // HEAD_DIM, drq.shape[1] // RET_QK_DIM

    def body(dq_ref, dk_ref, drq_ref, drk_ref, aq_ref, ak_ref, cos_ref, sin_ref, qw_ref, kw_ref,
             daq_out, dak_out, drq_out, drk_out, dqw_ref, dkw_ref):
        c, sn = cos_ref[...], sin_ref[...]

        def unrope(d):
            return _rope_transposed(d, c, sn)

        def norm_bwd(dun, u, w):
            r = lax.rsqrt(jnp.mean(u * u, axis=-1, keepdims=True) + EPS)
            uh = u * r
            dn = dun * w
            du = r * (dn - uh * jnp.mean(dn * uh, axis=-1, keepdims=True))
            return du, jnp.sum(dun * uh, axis=0, keepdims=True)

        dqw = jnp.zeros((1, HEAD_DIM), F32)
        for j in range(nq):
            sl = slice(j * HEAD_DIM, (j + 1) * HEAD_DIM)
            du, dw = norm_bwd(unrope(dq_ref[:, sl] * attn_scale), aq_ref[:, sl], qw_ref[...])
            daq_out[:, sl] = du.astype(BF16)
            dqw = dqw + dw
        dkw = jnp.zeros((1, HEAD_DIM), F32)
        for j in range(nk):
            sl = slice(j * HEAD_DIM, (j + 1) * HEAD_DIM)
            du, dw = norm_bwd(unrope(dk_ref[:, sl]), ak_ref[:, sl], kw_ref[...])
            dak_out[:, sl] = du.astype(BF16)
            dkw = dkw + dw
        for j in range(nr):
            sl = slice(j * RET_QK_DIM, (j + 1) * RET_QK_DIM)
            drq_out[:, sl] = unrope(drq_ref[:, sl]).astype(BF16)
            drk_out[:, sl] = unrope(drk_ref[:, sl] * ret_scale).astype(BF16)

        @pl.when(pl.program_id(0) == 0)
        def _():
            dqw_ref[...] = dqw
            dkw_ref[...] = dkw

        @pl.when(pl.program_id(0) > 0)
        def _():
            dqw_ref[...] += dqw
            dkw_ref[...] += dkw

    def row(arr):
        return pl.BlockSpec((ts, arr.shape[1]), lambda i: (i, 0))

    vec = pl.BlockSpec((1, HEAD_DIM), lambda i: (0, 0))
    ins = [dq, dk, drq, drk, aq, ak, cos, sin]
    outs = [dq, dk, drq, drk]
    return pl.pallas_call(
        body, name="prep_bwd", grid=(s // ts,),
        in_specs=[row(a) for a in ins] + [vec, vec],
        out_specs=[row(a) for a in outs] + [vec, vec],
        out_shape=[jax.ShapeDtypeStruct(a.shape, BF16) for a in outs]
        + [jax.ShapeDtypeStruct((1, HEAD_DIM), F32)] * 2,
        compiler_params=_params(("arbitrary",)),
    )(*ins, qw.reshape(1, HEAD_DIM), kw.reshape(1, HEAD_DIM))


def _attn_fwd(q, k, v, *, tq=4096, sub=256):
    s, aw = q.shape
    tq = _tile(s, tq)
    sub = _tile(tq, sub)
    heads, kvh = aw // HEAD_DIM, k.shape[1] // HEAD_DIM
    grp = heads // kvh

    def body(q_ref, k_ref, v_ref, o_ref, lse_ref):
        kv_ = k_ref[...]
        v_ext = jnp.concatenate([v_ref[...], jnp.ones((s, HEAD_DIM), BF16)], axis=-1)
        for r in range(tq // sub):
            rows = slice(r * sub, (r + 1) * sub)
            sc = lax.dot_general(q_ref[rows, :], kv_, NT_DIMS, preferred_element_type=F32)
            m = jnp.max(sc, axis=-1, keepdims=True)
            p = jnp.exp((sc - m).astype(BF16))
            oe = jnp.dot(p, v_ext, preferred_element_type=F32)
            l = oe[:, HEAD_DIM:HEAD_DIM + 1]
            o_ref[rows, :] = (oe[:, :HEAD_DIM] / l).astype(o_ref.dtype)
            lse_ref[rows, :] = jnp.broadcast_to(m + jnp.log(l), (sub, HEAD_DIM))

    qspec = pl.BlockSpec((tq, HEAD_DIM), lambda kv, g, i: (i, kv * grp + g))
    kspec = pl.BlockSpec((s, HEAD_DIM), lambda kv, g, i: (0, kv))
    return pl.pallas_call(
        body, name="attn_fwd", grid=(kvh, grp, s // tq),
        in_specs=[qspec, kspec, kspec],
        out_specs=[qspec, qspec],
        out_shape=[jax.ShapeDtypeStruct((s, aw), BF16), jax.ShapeDtypeStruct((s, aw), F32)],
        compiler_params=_params(("parallel", "parallel", "parallel")),
    )(q, k, v)


def _attn_bwd(q, k, v, o, do, lse, *, tq=1024, sub=256):
    s, aw = q.shape
    tq = _tile(s, tq)
    sub = _tile(tq, sub)
    heads, kvh = aw // HEAD_DIM, k.shape[1] // HEAD_DIM
    grp = heads // kvh
    nq = s // tq

    def body(q_ref, k_ref, v_ref, o_ref, do_ref, lse_ref, dq_ref, dk_ref, dv_ref, dk_acc, dv_acc, p_scr, ds_scr):
        g, i = pl.program_id(1), pl.program_id(2)
        kv_, vv = k_ref[...], v_ref[...]
        for r in range(tq // sub):
            rows = slice(r * sub, (r + 1) * sub)
            qv, dov = q_ref[rows, :], do_ref[rows, :]
            sc = lax.dot_general(qv, kv_, NT_DIMS, preferred_element_type=F32)
            p = jnp.exp((sc - lse_ref[rows, :1]).astype(BF16))
            dp = lax.dot_general(dov, vv, NT_DIMS, preferred_element_type=F32)
            delta = jnp.sum(dov.astype(F32) * o_ref[rows, :].astype(F32), axis=-1, keepdims=True)
            ds = p * (dp - delta).astype(BF16)
            dq_ref[rows, :] = jnp.dot(ds, kv_, preferred_element_type=F32)
            p_scr[rows, :] = p
            ds_scr[rows, :] = ds
        dvp = lax.dot_general(p_scr[...], do_ref[...], TN_DIMS, preferred_element_type=F32)
        dkp = lax.dot_general(ds_scr[...], q_ref[...], TN_DIMS, preferred_element_type=F32)
        first = jnp.logical_and(g == 0, i == 0)

        @pl.when(first)
        def _():
            dv_acc[...] = dvp
            dk_acc[...] = dkp

        @pl.when(jnp.logical_not(first))
        def _():
            dv_acc[...] += dvp
            dk_acc[...] += dkp

        @pl.when(jnp.logical_and(g == grp - 1, i == nq - 1))
        def _():
            dk_ref[...] = dk_acc[...]
            dv_ref[...] = dv_acc[...].astype(dv_ref.dtype)

    qspec = pl.BlockSpec((tq, HEAD_DIM), lambda kv, g, i: (i, kv * grp + g))
    kspec = pl.BlockSpec((s, HEAD_DIM), lambda kv, g, i: (0, kv))
    return pl.pallas_call(
        body, name="attn_bwd", grid=(kvh, grp, nq),
        in_specs=[qspec, kspec, kspec, qspec, qspec, qspec],
        out_specs=[qspec, kspec, kspec],
        out_shape=[jax.ShapeDtypeStruct((s, aw), F32), jax.ShapeDtypeStruct(k.shape, F32),
                   jax.ShapeDtypeStruct(v.shape, BF16)],
        scratch_shapes=[pltpu.VMEM((s, HEAD_DIM), F32), pltpu.VMEM((s, HEAD_DIM), F32),
                        pltpu.VMEM((tq, s), BF16), pltpu.VMEM((tq, s), BF16)],
        compiler_params=_params(("parallel", "arbitrary", "arbitrary")),
    )(q, k, v, o, do, lse)


def _sum_all(z):
    return jnp.sum(jnp.sum(z, axis=0, keepdims=True), axis=1, keepdims=True)


def _chunk_consts(df_ref, db_ref, t):
    lf = _log_sigmoid(df_ref[0][:, :1])
    lb = _log_sigmoid(db_ref[0][:, :1])
    r = lax.broadcasted_iota(jnp.int32, (t, 1), 0).astype(F32)
    c = lax.broadcasted_iota(jnp.int32, (1, t), 1).astype(F32)
    diff = r - c
    dm = jnp.exp(diff * jnp.where(diff >= 0, lf, -lb))
    return dict(diff=diff, dm=dm, r=r,
                af=jnp.exp(lf * (r + 1.0)), bf=jnp.exp(lf * (t - 1.0 - r)), gf=jnp.exp(lf * t),
                ab=jnp.exp(lb * (t - r)), bb=jnp.exp(lb * r), gb=jnp.exp(lb * t))


def _scaled(x, f):
    return (x.astype(F32) * f).astype(BF16)


def _retc_specs(s):
    qspec = pl.BlockSpec((s, RET_QK_DIM), lambda h: (0, h))
    vspec = pl.BlockSpec((s, RET_V_DIM), lambda h: (0, h))
    dspec = pl.BlockSpec((1, 1, LANES), lambda h: (h, 0, 0))
    return qspec, vspec, dspec


def _retc_fwd(q, k, v, dec_f, dec_b, *, t=256):
    s, qw = q.shape
    t = _tile(s, t)
    heads, nc = qw // RET_QK_DIM, s // t
    qspec, vspec, dspec = _retc_specs(s)

    def body(q_ref, k_ref, v_ref, df_ref, db_ref, o_ref):
        cs = _chunk_consts(df_ref, db_ref, t)

        def rows_of(i):
            return pl.ds(pl.multiple_of(i * t, t), t)

        def forward(i, sf):
            rows = rows_of(i)
            qi, ki, vi = q_ref[rows, :], k_ref[rows, :], v_ref[rows, :]
            sc = lax.dot_general(qi, ki, NT_DIMS, preferred_element_type=F32)
            intra = jnp.dot((sc * cs["dm"]).astype(BF16), vi, preferred_element_type=F32)
            cross = jnp.dot(_scaled(qi, cs["af"]), sf.astype(BF16), preferred_element_type=F32)
            o_ref[rows, :] = intra + cross
            return cs["gf"] * sf + lax.dot_general(_scaled(ki, cs["bf"]), vi, TN_DIMS, preferred_element_type=F32)

        def backward(j, sb):
            rows = rows_of(nc - 1 - j)
            qi, ki, vi = q_ref[rows, :], k_ref[rows, :], v_ref[rows, :]
            o_ref[rows, :] += jnp.dot(_scaled(qi, cs["ab"]), sb.astype(BF16), preferred_element_type=F32)
            return cs["gb"] * sb + lax.dot_general(_scaled(ki, cs["bb"]), vi, TN_DIMS, preferred_element_type=F32)

        zero = jnp.zeros((RET_QK_DIM, RET_V_DIM), F32)
        lax.fori_loop(0, nc, forward, zero)
        lax.fori_loop(0, nc, backward, zero)

    return pl.pallas_call(
        body, name="ret_fwd", grid=(heads,),
        in_specs=[qspec, qspec, vspec, dspec, dspec],
        out_specs=vspec, out_shape=jax.ShapeDtypeStruct(v.shape, F32),
        compiler_params=_params(("parallel",)),
    )(q, k, v, dec_f, dec_b)


def _retc_bwd(q, k, v, do, dec_f, dec_b, *, t=256):
    s, qw = q.shape
    t = _tile(s, t)
    heads, nc = qw // RET_QK_DIM, s // t
    qspec, vspec, dspec = _retc_specs(s)
    gspec = pl.BlockSpec((1, 8, LANES), lambda h: (h, 0, 0))

    def body(q_ref, k_ref, v_ref, do_ref, df_ref, db_ref, dq_ref, dk_ref, dv_ref, gf_ref, gb_ref,
             sf_scr, sb_scr, dv_acc):
        cs = _chunk_consts(df_ref, db_ref, t)
        r, diff, dm = cs["r"], cs["diff"], cs["dm"]

        def rows_of(i):
            return pl.ds(pl.multiple_of(i * t, t), t)

        def tn(a, b):
            return lax.dot_general(a, b, TN_DIMS, preferred_element_type=F32)

        def nt(a, b):
            return lax.dot_general(a, b, NT_DIMS, preferred_element_type=F32)

        def states_f(i, sf):
            sf_scr[i] = sf
            rows = rows_of(i)
            return cs["gf"] * sf + tn(_scaled(k_ref[rows, :], cs["bf"]), v_ref[rows, :])

        def states_b(j, sb):
            i = nc - 1 - j
            sb_scr[i] = sb
            rows = rows_of(i)
            return cs["gb"] * sb + tn(_scaled(k_ref[rows, :], cs["bb"]), v_ref[rows, :])

        zero = jnp.zeros((RET_QK_DIM, RET_V_DIM), F32)
        lax.fori_loop(0, nc, states_f, zero)
        lax.fori_loop(0, nc, states_b, zero)

        def scan_grads(i, state, u, qf, kf, vi, doi, fa, fb, step, wa, wb):
            qa, kb = qf * fa, kf * fb
            ub = u.astype(BF16)
            dqa = nt(doi, state.astype(BF16))
            dkb = nt(vi, ub)
            dv = jnp.dot(kb.astype(BF16), ub, preferred_element_type=F32)
            dlog = _sum_all(dqa * qa * wa) + _sum_all(dkb * kb * wb) + t * step * _sum_all(u * state)
            u_new = step * u + tn(qa.astype(BF16), doi)
            return dqa * fa, dkb * fb, dv, u_new, dlog

        def sweep_f(j, carry):
            u, accf, accb = carry
            i = nc - 1 - j
            rows = rows_of(i)
            qi, ki, vi, doi = q_ref[rows, :], k_ref[rows, :], v_ref[rows, :], do_ref[rows, :]
            sc = nt(qi, ki)
            p = sc * dm
            dp = nt(doi, vi)
            ds = (dp * dm).astype(BF16)
            tt = dp * p * diff
            accf = accf + _sum_all(jnp.where(diff > 0, tt, 0.0))
            accb = accb + _sum_all(jnp.where(diff < 0, -tt, 0.0))
            dq1, dk1, dv1, u, dlog = scan_grads(i, sf_scr[i], u, qi.astype(F32), ki.astype(F32), vi, doi,
                                                cs["af"], cs["bf"], cs["gf"], r + 1.0, t - 1.0 - r)
            dq_ref[rows, :] = jnp.dot(ds, ki, preferred_element_type=F32) + dq1
            dk_ref[rows, :] = tn(ds, qi) + dk1
            dv_acc[rows, :] = tn(p.astype(BF16), doi) + dv1
            return u, accf + dlog, accb

        def sweep_b(i, carry):
            w, accb = carry
            rows = rows_of(i)
            qi, ki, vi, doi = q_ref[rows, :], k_ref[rows, :], v_ref[rows, :], do_ref[rows, :]
            dq1, dk1, dv1, w, dlog = scan_grads(i, sb_scr[i], w, qi.astype(F32), ki.astype(F32), vi, doi,
                                                cs["ab"], cs["bb"], cs["gb"], t - r, r)
            dq_ref[rows, :] += dq1
            dk_ref[rows, :] += dk1
            dv_acc[rows, :] += dv1
            return w, accb + dlog

        z11 = jnp.zeros((1, 1), F32)
        _, accf, accb = lax.fori_loop(0, nc, sweep_f, (zero, z11, z11))
        _, accb = lax.fori_loop(0, nc, sweep_b, (zero, accb))
        dv_ref[...] = dv_acc[...].astype(dv_ref.dtype)
        gf_ref[...] = jnp.broadcast_to((accf / (1.0 + jnp.exp(df_ref[0][:, :1]))).reshape(1, 1, 1), gf_ref.shape)
        gb_ref[...] = jnp.broadcast_to((accb / (1.0 + jnp.exp(db_ref[0][:, :1]))).reshape(1, 1, 1), gb_ref.shape)

    return pl.pallas_call(
        body, name="ret_bwd", grid=(heads,),
        in_specs=[qspec, qspec, vspec, vspec, dspec, dspec],
        out_specs=[qspec, qspec, vspec, gspec, gspec],
        out_shape=[jax.ShapeDtypeStruct(q.shape, F32), jax.ShapeDtypeStruct(k.shape, F32),
                   jax.ShapeDtypeStruct(v.shape, BF16),
                   jax.ShapeDtypeStruct((heads, 8, LANES), F32), jax.ShapeDtypeStruct((heads, 8, LANES), F32)],
        scratch_shapes=[pltpu.VMEM((nc, RET_QK_DIM, RET_V_DIM), F32), pltpu.VMEM((nc, RET_QK_DIM, RET_V_DIM), F32),
                        pltpu.VMEM((s, RET_V_DIM), F32)],
        compiler_params=_params(("parallel",)),
    )(q, k, v, do, dec_f, dec_b)


def _gate_fwd(att, ag, ret, rg, rnw, *, ts=256):
    s, aw = att.shape
    rw = ret.shape[1]
    ts = _tile(s, ts)
    rheads = rw // RET_V_DIM

    def body(att_ref, ag_ref, ret_ref, rg_ref, w_ref, y_ref):
        sa, _ = _silu_parts(ag_ref[...])
        y_ref[:, :aw] = (sa * att_ref[...].astype(F32)).astype(BF16)
        for h in range(rheads):
            sl = slice(h * RET_V_DIM, (h + 1) * RET_V_DIM)
            rt = ret_ref[:, sl]
            rn = rt * lax.rsqrt(jnp.mean(rt * rt, axis=-1, keepdims=True) + EPS) * w_ref[:, sl]
            sr, _ = _silu_parts(rg_ref[:, sl])
            y_ref[:, aw + h * RET_V_DIM:aw + (h + 1) * RET_V_DIM] = (sr * rn).astype(BF16)

    def row(w):
        return pl.BlockSpec((ts, w), lambda i: (i, 0))

    return pl.pallas_call(
        body, name="gate_fwd", grid=(s // ts,),
        in_specs=[row(aw), row(aw), row(rw), row(rw), pl.BlockSpec((1, rw), lambda i: (0, 0))],
        out_specs=row(aw + rw), out_shape=jax.ShapeDtypeStruct((s, aw + rw), BF16),
        compiler_params=_params(("parallel",)),
    )(att, ag, ret, rg, rnw.reshape(1, rw))


def _gate_bwd(dy, att, ag, ret, rg, rnw, *, ts=256):
    s, aw = att.shape
    rw = ret.shape[1]
    ts = _tile(s, ts)
    rheads = rw // RET_V_DIM

    def body(dy_ref, att_ref, ag_ref, ret_ref, rg_ref, w_ref, datt_ref, dag_ref, dret_ref, drg_ref, dw_ref):
        sa, dsa = _silu_parts(ag_ref[...])
        dya = dy_ref[:, :aw]
        datt_ref[...] = (dya * sa).astype(BF16)
        dag_ref[...] = (dya * att_ref[...].astype(F32) * dsa).astype(BF16)
        parts = []
        for h in range(rheads):
            sl = slice(h * RET_V_DIM, (h + 1) * RET_V_DIM)
            rt = ret_ref[:, sl]
            rr = lax.rsqrt(jnp.mean(rt * rt, axis=-1, keepdims=True) + EPS)
            rh = rt * rr
            wv = w_ref[:, sl]
            sr, dsr = _silu_parts(rg_ref[:, sl])
            dyr = dy_ref[:, aw + h * RET_V_DIM:aw + (h + 1) * RET_V_DIM]
            drg_ref[:, sl] = (dyr * rh * wv * dsr).astype(BF16)
            drn = dyr * sr
            dn = drn * wv
            dret_ref[:, sl] = (rr * (dn - rh * jnp.mean(dn * rh, axis=-1, keepdims=True))).astype(BF16)
            parts.append(jnp.sum(drn * rh, axis=0, keepdims=True))
        part = jnp.concatenate(parts, axis=-1)

        @pl.when(pl.program_id(0) == 0)
        def _():
            dw_ref[...] = part

        @pl.when(pl.program_id(0) > 0)
        def _():
            dw_ref[...] += part

    def row(w):
        return pl.BlockSpec((ts, w), lambda i: (i, 0))

    vec = pl.BlockSpec((1, rw), lambda i: (0, 0))
    return pl.pallas_call(
        body, name="gate_bwd", grid=(s // ts,),
        in_specs=[row(aw + rw), row(aw), row(aw), row(rw), row(rw), vec],
        out_specs=[row(aw), row(aw), row(rw), row(rw), vec],
        out_shape=[jax.ShapeDtypeStruct((s, aw), BF16), jax.ShapeDtypeStruct((s, aw), BF16),
                   jax.ShapeDtypeStruct((s, rw), BF16), jax.ShapeDtypeStruct((s, rw), BF16),
                   jax.ShapeDtypeStruct((1, rw), F32)],
        compiler_params=_params(("arbitrary",)),
    )(dy, att, ag, ret, rg, rnw.reshape(1, rw))


def _mesh_position():
    x, y, c = lax.axis_index("x"), lax.axis_index("y"), lax.axis_index("c")
    return x, y, c, 4 * x + 2 * y + c


def _peer(x, y, c, k):
    px = 1 - x if k & 4 else x
    py = 1 - y if k & 2 else y
    pc = 1 - c if k & 1 else c
    return (px, py, pc), 4 * px + 2 * py + pc


HBM_SPEC = pl.BlockSpec(memory_space=pltpu.HBM)
SEM_SPEC = pl.BlockSpec(memory_space=pltpu.SEMAPHORE)
ANY_SPEC = pl.BlockSpec(memory_space=pl.ANY)
DATAFLOW = pltpu.SideEffectType.DATAFLOW_SIDE_EFFECTING
N_SPLIT_COPIES = N_DEV - 1


def _hbm(a):
    return pltpu.with_memory_space_constraint(a, pltpu.HBM)


def _split_start(name, copies, src, land, after):
    def body(*refs):
        (send_sems, recv_sems), token = refs[2 + len(after):4 + len(after)], refs[-1]
        sends, _ = copies(refs[0], refs[1], send_sems, recv_sems)
        for cp in sends:
            cp.start()
        token[...] = jnp.zeros_like(token)

    return pl.pallas_call(
        body, name=name,
        out_shape=(pltpu.SemaphoreType.DMA((N_SPLIT_COPIES,)), pltpu.SemaphoreType.DMA((N_SPLIT_COPIES,)),
                   pltpu.HBM(src.shape, src.dtype), pltpu.HBM(land.shape, land.dtype),
                   jax.ShapeDtypeStruct((8, LANES), F32)),
        in_specs=[HBM_SPEC] * 2 + [ANY_SPEC] * len(after),
        out_specs=(SEM_SPEC, SEM_SPEC, HBM_SPEC, HBM_SPEC, pl.BlockSpec(memory_space=pltpu.VMEM)),
        input_output_aliases={0: 2, 1: 3},
        compiler_params=pltpu.CompilerParams(has_side_effects=DATAFLOW),
    )(_hbm(src), _hbm(land), *after)


def _split_wait(name, copies, started, after):
    send_sems, recv_sems, src, land = started[:4]

    def body(*refs):
        sends, recvs = copies(refs[0], refs[1], refs[2], refs[3])
        for cp in sends:
            cp.wait_send()
        for cp in recvs:
            cp.wait_recv()

    return pl.pallas_call(
        body, name=name,
        out_shape=(pltpu.HBM(src.shape, src.dtype), pltpu.HBM(land.shape, land.dtype)),
        in_specs=[HBM_SPEC] * 2 + [SEM_SPEC, SEM_SPEC] + [ANY_SPEC] * len(after),
        out_specs=(HBM_SPEC,) * 2,
        input_output_aliases={0: 0, 1: 1},
        compiler_params=pltpu.CompilerParams(has_side_effects=DATAFLOW),
    )(src, land, send_sems, recv_sems, *after)[1]


def _slab(ref, p, size, axis):
    if axis == 1:
        return ref.at[:, pl.ds(pl.multiple_of(p * size, LANES), size)]
    return ref.at[pl.ds(pl.multiple_of(p * size, 16), size), :]


def _gather_copies(size, axis):
    def copies(shard_ref, full_ref, send_sems, recv_sems):
        x, y, c, me = _mesh_position()
        sends, recvs = [], []
        for k in range(1, N_DEV):
            peer, pid = _peer(x, y, c, k)
            sends.append(pltpu.make_async_remote_copy(
                src_ref=shard_ref, dst_ref=_slab(full_ref, me, size, axis), send_sem=send_sems.at[k - 1],
                recv_sem=recv_sems.at[k - 1], device_id=peer, device_id_type=MESH))
            recvs.append(pltpu.make_async_remote_copy(
                src_ref=shard_ref, dst_ref=_slab(full_ref, pid, size, axis), send_sem=send_sems.at[k - 1],
                recv_sem=recv_sems.at[k - 1], device_id=peer, device_id_type=MESH))
        return sends, recvs

    return copies


def _scatter_copies(size, axis):
    def copies(grad_ref, land_ref, send_sems, recv_sems):
        x, y, c, me = _mesh_position()
        sends, recvs = [], []
        for k in range(1, N_DEV):
            peer, pid = _peer(x, y, c, k)
            src = _slab(grad_ref, pid, size, axis)
            sends.append(pltpu.make_async_remote_copy(
                src_ref=src, dst_ref=land_ref.at[me], send_sem=send_sems.at[k - 1], recv_sem=recv_sems.at[k - 1],
                device_id=peer, device_id_type=MESH))
            recvs.append(pltpu.make_async_remote_copy(
                src_ref=src, dst_ref=land_ref.at[pid], send_sem=send_sems.at[k - 1], recv_sem=recv_sems.at[k - 1],
                device_id=peer, device_id_type=MESH))
        return sends, recvs

    return copies


PLACE_BANDS = 8


def _place_own(name, src, out_shape, in_spec, out_spec, steps, me):
    def body(me_ref, src_ref, out_ref):
        out_ref[...] = src_ref[...]

    return pl.pallas_call(
        body, name=name, out_shape=out_shape,
        grid_spec=pltpu.PrefetchScalarGridSpec(num_scalar_prefetch=1, grid=(steps,), in_specs=[in_spec],
                                               out_specs=out_spec),
        compiler_params=_params(("parallel",)),
    )(me.reshape(1).astype(jnp.int32), src)


def _gather_start(shard, axis, me, after, tag):
    rows, cols = shard.shape
    size = shard.shape[axis]
    full_shape = tuple(N_DEV * n if a == axis else n for a, n in enumerate(shard.shape))
    band = rows // PLACE_BANDS
    in_spec = pl.BlockSpec((band, cols), lambda i, me_ref: (i, 0))
    if axis == 1:
        out_spec = pl.BlockSpec((band, cols), lambda i, me_ref: (i, me_ref[0]))
    else:
        out_spec = pl.BlockSpec((band, cols), lambda i, me_ref: (me_ref[0] * PLACE_BANDS + i, 0))
    full = _place_own("place_shard", shard, jax.ShapeDtypeStruct(full_shape, shard.dtype), in_spec, out_spec,
                      PLACE_BANDS, me)
    return _split_start("gather_start_" + tag, _gather_copies(size, axis), shard, full, after)


def _gather_wait(started, axis, after, tag):
    size = started[2].shape[axis]
    return _split_wait("gather_wait_" + tag, _gather_copies(size, axis), started, after)


def _scatter_start(grad, axis, me, tag):
    size = grad.shape[axis] // N_DEV
    rows, cols = tuple(size if a == axis else n for a, n in enumerate(grad.shape))
    band = rows // PLACE_BANDS
    if axis == 1:
        in_spec = pl.BlockSpec((band, cols), lambda i, me_ref: (i, me_ref[0]))
    else:
        in_spec = pl.BlockSpec((band, cols), lambda i, me_ref: (me_ref[0] * PLACE_BANDS + i, 0))
    out_spec = pl.BlockSpec((None, band, cols), lambda i, me_ref: (me_ref[0], i, 0))
    land = _place_own("place_slab", grad, jax.ShapeDtypeStruct((N_DEV, rows, cols), grad.dtype), in_spec, out_spec,
                      PLACE_BANDS, me)
    return _split_start("scatter_start_" + tag, _scatter_copies(size, axis), grad, land, [])


def _scatter_wait(started, axis, after, tag):
    size = started[2].shape[axis] // N_DEV
    return _split_wait("scatter_wait_" + tag, _scatter_copies(size, axis), started, after)


def _exchange_small(buf, *, name, after=()):
    r = buf.shape[0]

    def body(*refs):
        buf_ref = refs[0]
        all_ref, sum_ref, send_sems, recv_sems = refs[1 + len(after):]
        x, y, c, me = _mesh_position()
        all_ref[me] = buf_ref[...]
        sends, recvs = [], []
        for k in range(1, N_DEV):
            peer, pid = _peer(x, y, c, k)
            sends.append(pltpu.make_async_remote_copy(
                src_ref=buf_ref, dst_ref=all_ref.at[me], send_sem=send_sems.at[k - 1], recv_sem=recv_sems.at[k - 1],
                device_id=peer, device_id_type=MESH))
            recvs.append(pltpu.make_async_remote_copy(
                src_ref=buf_ref, dst_ref=all_ref.at[pid], send_sem=send_sems.at[k - 1], recv_sem=recv_sems.at[k - 1],
                device_id=peer, device_id_type=MESH))
        for cp in sends:
            cp.start()
        for cp in recvs:
            cp.wait_recv()
        for cp in sends:
            cp.wait_send()
        total = all_ref[0]
        for p in range(1, N_DEV):
            total = total + all_ref[p]
        sum_ref[...] = total

    vmem = pl.BlockSpec(memory_space=pltpu.VMEM)
    return pl.pallas_call(
        body, name=name,
        in_specs=[vmem] + [ANY_SPEC] * len(after), out_specs=[vmem, vmem],
        out_shape=[jax.ShapeDtypeStruct((N_DEV, r, LANES), F32), jax.ShapeDtypeStruct((r, LANES), F32)],
        scratch_shapes=[pltpu.SemaphoreType.DMA((N_DEV - 1,)), pltpu.SemaphoreType.DMA((N_DEV - 1,))],
        compiler_params=pltpu.CompilerParams(has_side_effects=True),
    )(buf, *after)


def _adamw_math(w, g, m, v):
    m2 = ADAM_B1 * m + (1.0 - ADAM_B1) * g
    v2 = ADAM_B2 * v + (1.0 - ADAM_B2) * (g * g)
    delta = -ADAM_LR * ((m2 / ADAM_C1) / (jnp.sqrt(v2 / ADAM_C2) + ADAM_EPS) + ADAM_WD * w)
    return delta, m2, v2


def _adamw_slabs(layer, w, m, v, land, outs, order, *, tr, name):
    depth, r, c = w.shape
    tr = _tile(r, tr)

    def body(w_ref, m_ref, v_ref, land_ref, order_ref, o0, o1, o2, o3, g_ref, d_ref, m2_ref, v2_ref):
        g = land_ref[0].astype(F32)
        for p in range(1, N_DEV):
            g = g + land_ref[p].astype(F32)
        delta, m2, v2 = _adamw_math(w_ref[...], g, m_ref[...], v_ref[...])
        g_ref[...] = g
        d_ref[...] = delta
        m2_ref[...] = m2
        v2_ref[...] = v2

    row = pl.BlockSpec((None, tr, c), lambda i: (layer, i, 0))
    return pl.pallas_call(
        body, name=name, grid=(r // tr,),
        in_specs=[row, row, row, pl.BlockSpec((N_DEV, tr, c), lambda i: (0, i, 0)),
                  pl.BlockSpec((8, LANES), lambda i: (0, 0))] + [ANY_SPEC] * 4,
        out_specs=[row] * 4, out_shape=[jax.ShapeDtypeStruct((depth, r, c), F32)] * 4,
        input_output_aliases={5: 0, 6: 1, 7: 2, 8: 3},
        compiler_params=_params(("parallel",)),
    )(w, m, v, land, order, *outs)


def _adamw_small(w, g, m, v):
    def body(w_ref, g_ref, m_ref, v_ref, d_ref, m2_ref, v2_ref):
        delta, m2, v2 = _adamw_math(w_ref[...], g_ref[...], m_ref[...], v_ref[...])
        d_ref[...] = delta
        m2_ref[...] = m2
        v2_ref[...] = v2

    vmem = pl.BlockSpec(memory_space=pltpu.VMEM)
    return pl.pallas_call(
        body, name="adamw_small", in_specs=[vmem] * 4, out_specs=[vmem] * 3,
        out_shape=[jax.ShapeDtypeStruct(w.shape, F32)] * 3,
    )(w, g, m, v)


def _pack(parts):
    flat = jnp.concatenate([p.reshape(-1).astype(F32) for p in parts])
    rows = -(-flat.shape[0] // LANES)
    rows = -(-rows // SMALL_ROWS_ALIGN) * SMALL_ROWS_ALIGN
    flat = jnp.pad(flat, (0, rows * LANES - flat.shape[0]))
    return flat.reshape(rows, LANES)


def _unpack(buf, shapes):
    flat = buf.reshape(-1)
    out, pos = [], 0
    for shp in shapes:
        size = math.prod(shp)
        out.append(flat[pos:pos + size].reshape(shp))
        pos += size
    return out


def _section_widths(d):
    aw = d // 2
    kw = aw // ATTN_GROUP
    rw = d - aw
    rqw = (rw // RET_V_DIM) * RET_QK_DIM
    return (aw, kw, kw, aw, rqw, rqw, rw, rw)


def _layer_fwd(xl, nw, win_full, wout_of, qn, kn, dec_f, dec_b, rn, cos, sin):
    widths = _section_widths(xl.shape[1])
    offs = tuple(int(o) for o in np.cumsum((0,) + widths)[:-1])
    sec_tn = _tile(widths[1], 512)
    h = _rms_fwd(xl, nw)
    secs = [_matmul(h, win_full, name="proj", b_off=offs[i], n=widths[i], tn=sec_tn) for i in range(8)]
    aq, ak, av, ag, rq, rk, rv, rg = secs
    q, k, v, rqr, rkr, rvb = _prep_fwd(aq, ak, av, rq, rk, rv, cos, sin, qn, kn)
    att, lse = _attn_fwd(q, k, v)
    ret = _retc_fwd(rqr, rkr, rvb, dec_f, dec_b)
    y = _gate_fwd(att, ag, ret, rg, rn)
    wout_full = wout_of(y)
    xn = _matmul(y, wout_full, name="out_proj", residual=xl)
    saved = dict(x=xl, h=h, aq=aq, ak=ak, ag=ag, rg=rg, q=q, k=k, v=v, rq=rqr, rk=rkr, rv=rvb,
                 att=att, lse=lse, ret=ret, y=y, win=win_full, wout=wout_full)
    return xn, saved


def _layer_bwd_weights(gb, sv, qn, kn, dec_f, dec_b, rn, cos, sin, on_dwout):
    dy = _matmul(gb, sv["wout"], name="d_y", trans_b=True)
    dwout = _matmul(sv["y"].T, gb, name="d_wout", out_dtype=BF16)
    datt, dag, dret, drg, drn = _gate_bwd(dy, sv["att"], sv["ag"], sv["ret"], sv["rg"], rn + on_dwout(dwout))
    dq, dk, dav = _attn_bwd(sv["q"], sv["k"], sv["v"], sv["att"], datt, sv["lse"])
    drq, drk, drv, gf, gbk = _retc_bwd(sv["rq"], sv["rk"], sv["rv"], dret, dec_f, dec_b)
    daq, dak, drq_p, drk_p, dqn, dkn = _prep_bwd(dq, dk, drq, drk, sv["aq"], sv["ak"], cos, sin, qn, kn)
    dproj = jnp.concatenate([daq, dak, dav, dag, drq_p, drk_p, drv, drg], axis=-1)
    dwin = _matmul(sv["h"].T, dproj, name="d_win", out_dtype=BF16)
    small = dict(qn=dqn[0], kn=dkn[0], df=gf[:, 0, 0], db=gbk[:, 0, 0], rn=drn[0])
    return dproj, dwin, small


def _layer_bwd_input(g, dproj, sv, nw):
    dh = _matmul(dproj, sv["win"], name="d_h", trans_b=True, tk=_tile(dproj.shape[1], 2816, LANES))
    g, gb, dnw = _rms_bwd(dh, sv["x"], g, nw)
    return g, gb, dnw[0]


def kernel(x, norm_w, w_in, q_norm, k_norm, ret_decay_fwd, ret_decay_bwd, ret_norm, w_out, final_norm, loss_target, m_norm_w, m_w_in, m_q_norm, m_k_norm, m_ret_decay_fwd, m_ret_decay_bwd, m_ret_norm, m_w_out, m_final_norm, v_norm_w, v_w_in, v_q_norm, v_k_norm, v_ret_decay_fwd, v_ret_decay_bwd, v_ret_norm, v_w_out, v_final_norm):
    depth, d, _ = w_in.shape
    seq = x.shape[1]
    rw = _section_widths(d)[6]
    rheads = rw // RET_V_DIM
    rns = ret_norm.shape[-1]
    _, _, _, me = _mesh_position()

    target = loss_target[0]
    cos, sin = _rope_tables(seq)

    rn_all, _ = _exchange_small(_pack([ret_norm]), name="gather_ret_norm")
    rn_full = rn_all.reshape(N_DEV, -1)[:, :depth * rheads * rns].reshape(N_DEV, depth, rheads, rns)
    rn_full = jnp.transpose(rn_full, (1, 2, 0, 3)).reshape(depth, rw)

    dec_f = jnp.broadcast_to(ret_decay_fwd[:, :, None, None], (depth, rheads, 1, LANES))
    dec_b = jnp.broadcast_to(ret_decay_bwd[:, :, None, None], (depth, rheads, 1, LANES))

    win_bf = w_in.astype(BF16)
    wout_bf = w_out.astype(BF16)

    saved = []
    xl = x[0]
    win_full = _gather_wait(_gather_start(win_bf[0], 1, me, [], "in0"), 1, [], "in0")
    for l in range(depth):
        out_sent = _gather_start(wout_bf[l], 0, me, [win_full], "out" + str(l))
        nw = norm_w[l] + out_sent[-1][0, 0]
        if l + 1 < depth:
            in_sent = _gather_start(win_bf[l + 1], 1, me, [win_full, out_sent[-1]], "in" + str(l + 1))
            nw = nw + in_sent[-1][0, 0]

        def wout_of(y, out_sent=out_sent, l=l):
            return _gather_wait(out_sent, 0, [y], "out" + str(l))

        xl, sv = _layer_fwd(xl, nw, win_full, wout_of, q_norm[l], k_norm[l], dec_f[l], dec_b[l],
                            rn_full[l], cos, sin)
        saved.append(sv)
        if l + 1 < depth:
            win_full = _gather_wait(in_sent, 1, [xl], "in" + str(l + 1))

    loss_row, g, gb, d_final = _loss_head(xl, target, final_norm)

    d_norm, d_qn, d_kn, d_df, d_db, d_rn = [], [], [], [], [], []
    lands = [None] * depth
    pending = None
    for l in reversed(range(depth)):
        sent = {}

        def on_dwout(dwout, sent=sent, l=l):
            sent["out"] = _scatter_start(dwout, 0, me, "out" + str(l))
            return sent["out"][-1][0, 0]

        dproj, dwin, sm = _layer_bwd_weights(gb, saved[l], q_norm[l], k_norm[l], dec_f[l], dec_b[l],
                                             rn_full[l], cos, sin, on_dwout)
        sent["in"] = _scatter_start(dwin, 1, me, "in" + str(l))
        g, gb, dnw = _layer_bwd_input(g, dproj, saved[l], norm_w[l] + sent["in"][-1][0, 0])
        if pending is not None:
            lands[l + 1] = (_scatter_wait(pending["in"], 1, [g], "in" + str(l + 1)),
                            _scatter_wait(pending["out"], 0, [g], "out" + str(l + 1)))
        pending = sent
        d_norm.append(dnw)
        d_qn.append(sm["qn"])
        d_kn.append(sm["kn"])
        d_df.append(sm["df"])
        d_db.append(sm["db"])
        d_rn.append(sm["rn"])
    for lst in (d_norm, d_qn, d_kn, d_df, d_db, d_rn):
        lst.reverse()
    order = pending["in"][-1]
    in_outs = [lax.empty(w_in.shape, F32) for _ in range(4)]
    out_outs = [lax.empty(w_out.shape, F32) for _ in range(4)]
    for l in reversed(range(depth)):
        if l == 0:
            lands[0] = (_scatter_wait(pending["in"], 1, [g, in_outs[0], out_outs[0]], "in0"),
                        _scatter_wait(pending["out"], 0, [g], "out0"))
        in_outs = _adamw_slabs(l, w_in, m_w_in, v_w_in, lands[l][0], in_outs, order, tr=256, name="adamw_w_in")
        out_outs = _adamw_slabs(l, w_out, m_w_out, v_w_out, lands[l][1], out_outs, order, tr=64, name="adamw_w_out")

    small_shapes = [(depth, d), (depth, HEAD_DIM), (depth, HEAD_DIM), (depth, rheads), (depth, rheads),
                    (depth, rheads, N_DEV * rns), (d,), (1,)]
    grads_local = [jnp.stack(d_norm), jnp.stack(d_qn), jnp.stack(d_kn), jnp.stack(d_df), jnp.stack(d_db),
                   jnp.stack(d_rn).reshape(depth, rheads, N_DEV * rns), d_final[0], loss_row[0, :1]]
    _, gsum = _exchange_small(_pack(grads_local), name="all_reduce_small", after=(in_outs[0], out_outs[0]))
    g_norm, g_qn, g_kn, g_df, g_db, g_rn_full, g_final, loss = _unpack(gsum, small_shapes)
    g_rn = lax.dynamic_slice_in_dim(g_rn_full, me * rns, rns, axis=2)
    small_g = [g_norm, g_qn, g_kn, g_df, g_db, g_rn, g_final]
    small_w = [norm_w, q_norm, k_norm, ret_decay_fwd, ret_decay_bwd, ret_norm, final_norm]
    small_m = [m_norm_w, m_q_norm, m_k_norm, m_ret_decay_fwd, m_ret_decay_bwd, m_ret_norm, m_final_norm]
    small_v = [v_norm_w, v_q_norm, v_k_norm, v_ret_decay_fwd, v_ret_decay_bwd, v_ret_norm, v_final_norm]
    shapes = [a.shape for a in small_w]
    sd, sm, sv2 = _adamw_small(_pack(small_w), _pack(small_g), _pack(small_m), _pack(small_v))
    small_d, small_m2, small_v2 = _unpack(sd, shapes), _unpack(sm, shapes), _unpack(sv2, shapes)

    def ordered(small, win_v, wout_v):
        return [small[0], win_v, small[1], small[2], small[3], small[4], small[5], wout_v, small[6]]

    grads = ordered(small_g, in_outs[0], out_outs[0])
    deltas = ordered(small_d, in_outs[1], out_outs[1])
    new_m = ordered(small_m2, in_outs[2], out_outs[2])
    new_v = ordered(small_v2, in_outs[3], out_outs[3])
    return (loss.reshape(()), g[None], *grads, *deltas, *new_m, *new_v)
```

```python
import functools
import math

import jax
import jax.numpy as jnp
import numpy as np
from jax import lax
from jax.experimental import pallas as pl
from jax.experimental.pallas import tpu as pltpu

F32 = jnp.float32
BF16 = jnp.bfloat16

N_DEV = 8
HEAD_DIM = 128
ATTN_GROUP = 4
RET_QK_DIM = 128
RET_V_DIM = 256
GRID_W = 64
ROPE_THETA = 10000.0
EPS = 1e-6
ADAM_LR = 0.001
ADAM_B1 = 0.9
ADAM_B2 = 0.999
ADAM_EPS = 1e-08
ADAM_WD = 0.01
ADAM_STEP = 10
ADAM_C1 = 1.0 - ADAM_B1 ** ADAM_STEP
ADAM_C2 = 1.0 - ADAM_B2 ** ADAM_STEP
LANES = 128
SMALL_ROWS_ALIGN = 8
VMEM_LIMIT = 56 * 1024 * 1024

NT_DIMS = (((1,), (1,)), ((), ()))
TN_DIMS = (((0,), (0,)), ((), ()))
MESH = pl.DeviceIdType.MESH


def _params(sem):
    return pltpu.CompilerParams(dimension_semantics=sem, vmem_limit_bytes=VMEM_LIMIT)


def _tile(dim, pref, align=16):
    if dim <= pref:
        return dim
    for t in range(pref - pref % align, 0, -align):
        if dim % t == 0:
            return t
    raise ValueError((dim, pref, align))


def _silu_parts(z):
    sg = 1.0 / (1.0 + jnp.exp(-z))
    return z * sg, sg * (1.0 + z * (1.0 - sg))


def _log_sigmoid(x):
    return jnp.minimum(x, 0.0) - jnp.log(1.0 + jnp.exp(-jnp.abs(x)))


def _swap_pairs(z):
    lane = lax.broadcasted_iota(jnp.int32, z.shape, 1)
    return jnp.where((lane % 64) < 32, pltpu.roll(z, 96, 1), pltpu.roll(z, 32, 1))


def _rope(z, cos, sin):
    return z * cos + _swap_pairs(z) * sin


def _rope_transposed(d, cos, sin):
    return d * cos + _swap_pairs(d * sin)


def _rope_tables(seq):
    rows = seq // GRID_W
    row = jnp.repeat(jnp.arange(rows), GRID_W).astype(F32)
    col = jnp.tile(jnp.arange(GRID_W), rows).astype(F32)
    axis_dim = HEAD_DIM // 2
    inv = ROPE_THETA ** (-jnp.arange(0, axis_dim, 2, dtype=F32) / axis_dim)
    ar = row[:, None] * inv[None, :]
    ac = col[:, None] * inv[None, :]
    cos = jnp.concatenate([jnp.cos(ar), jnp.cos(ar), jnp.cos(ac), jnp.cos(ac)], axis=-1)
    sin = jnp.concatenate([-jnp.sin(ar), jnp.sin(ar), -jnp.sin(ac), jnp.sin(ac)], axis=-1)
    return cos, sin


def _matmul(a, b, *, name, trans_b=False, b_off=0, n=None, out_dtype=F32, residual=None,
            tm=1024, tn=512, tk=4096):
    m, k = a.shape
    if n is None:
        n = b.shape[0] if trans_b else b.shape[1]
    tm, tn, tk = _tile(m, tm), _tile(n, tn, LANES), _tile(k, tk, LANES)
    assert b_off % tn == 0
    joff = b_off // tn
    nk = k // tk
    has_res = residual is not None

    def body(*refs):
        if has_res:
            a_ref, b_ref, r_ref, o_ref = refs[:4]
        else:
            a_ref, b_ref, o_ref = refs[:3]
        if trans_b:
            part = lax.dot_general(a_ref[...], b_ref[...], NT_DIMS, preferred_element_type=F32)
        else:
            part = jnp.dot(a_ref[...], b_ref[...], preferred_element_type=F32)

        def finish(r):
            if has_res:
                r = r + r_ref[...]
            o_ref[...] = r.astype(o_ref.dtype)

        if nk == 1:
            finish(part)
        else:
            acc_ref = refs[-1]
            kk = pl.program_id(2)

            @pl.when(kk == 0)
            def _():
                acc_ref[...] = part

            @pl.when(kk > 0)
            def _():
                acc_ref[...] += part

            @pl.when(kk == nk - 1)
            def _():
                finish(acc_ref[...])

    if trans_b:
        b_spec = pl.BlockSpec((tn, tk), lambda i, j, kk: (j + joff, kk))
    else:
        b_spec = pl.BlockSpec((tk, tn), lambda i, j, kk: (kk, j + joff))
    in_specs = [pl.BlockSpec((tm, tk), lambda i, j, kk: (i, kk)), b_spec]
    args = [a, b]
    if has_res:
        in_specs.append(pl.BlockSpec((tm, tn), lambda i, j, kk: (i, j)))
        args.append(residual)
    return pl.pallas_call(
        body, name=name, grid=(m // tm, n // tn, nk),
        in_specs=in_specs,
        out_specs=pl.BlockSpec((tm, tn), lambda i, j, kk: (i, j)),
        out_shape=jax.ShapeDtypeStruct((m, n), out_dtype),
        scratch_shapes=[pltpu.VMEM((tm, tn), F32)] if nk > 1 else [],
        compiler_params=_params(("parallel", "parallel", "arbitrary")),
    )(*args)


def _rms_fwd(x, w, *, ts=256):
    s, d = x.shape
    ts = _tile(s, ts)

    def body(x_ref, w_ref, h_ref, ht_ref):
        xv = x_ref[...]
        r = lax.rsqrt(jnp.mean(xv * xv, axis=-1, keepdims=True) + EPS)
        h = xv * r * w_ref[...]
        h_ref[...] = h.astype(BF16)
        ht_ref[...] = h.T.astype(BF16)

    row = pl.BlockSpec((ts, d), lambda i: (i, 0))
    return pl.pallas_call(
        body, name="rms_fwd", grid=(s // ts,),
        in_specs=[row, pl.BlockSpec((1, d), lambda i: (0, 0))],
        out_specs=[row, pl.BlockSpec((d, ts), lambda i: (0, i))],
        out_shape=[jax.ShapeDtypeStruct((s, d), BF16), jax.ShapeDtypeStruct((d, s), BF16)],
        compiler_params=_params(("parallel",)),
    )(x, w.reshape(1, d))


def _rms_bwd(dh, x, g, w, *, ts=256):
    s, d = x.shape
    ts = _tile(s, ts)

    def body(dh_ref, x_ref, g_ref, w_ref, dx_ref, dxb_ref, dw_ref):
        xv = x_ref[...]
        r = lax.rsqrt(jnp.mean(xv * xv, axis=-1, keepdims=True) + EPS)
        xh = xv * r
        dhv = dh_ref[...]
        dn = dhv * w_ref[...]
        dx = g_ref[...] + r * (dn - xh * jnp.mean(dn * xh, axis=-1, keepdims=True))
        dx_ref[...] = dx
        dxb_ref[...] = dx.astype(BF16)
        part = jnp.sum(dhv * xh, axis=0, keepdims=True)

        @pl.when(pl.program_id(0) == 0)
        def _():
            dw_ref[...] = part

        @pl.when(pl.program_id(0) > 0)
        def _():
            dw_ref[...] += part

    row = pl.BlockSpec((ts, d), lambda i: (i, 0))
    vec = pl.BlockSpec((1, d), lambda i: (0, 0))
    return pl.pallas_call(
        body, name="rms_bwd", grid=(s // ts,),
        in_specs=[row, row, row, vec],
        out_specs=[row, row, vec],
        out_shape=[jax.ShapeDtypeStruct((s, d), F32), jax.ShapeDtypeStruct((s, d), BF16),
                   jax.ShapeDtypeStruct((1, d), F32)],
        compiler_params=_params(("arbitrary",)),
    )(dh, x, g, w.reshape(1, d))


def _loss_head(x, target, w, *, ts=256):
    s, d = x.shape
    ts = _tile(s, ts)

    def body(x_ref, t_ref, w_ref, loss_ref, dx_ref, dxb_ref, dw_ref):
        xv = x_ref[...]
        r = lax.rsqrt(jnp.mean(xv * xv, axis=-1, keepdims=True) + EPS)
        xh = xv * r
        wv = w_ref[...]
        diff = xh * wv - t_ref[...]
        lpart = 0.5 * jnp.sum(jnp.mean(diff * diff, axis=-1, keepdims=True), axis=0, keepdims=True)
        dout = diff * (1.0 / d)
        dn = dout * wv
        dx = r * (dn - xh * jnp.mean(dn * xh, axis=-1, keepdims=True))
        dx_ref[...] = dx
        dxb_ref[...] = dx.astype(BF16)
        part = jnp.sum(dout * xh, axis=0, keepdims=True)
        lrow = jnp.broadcast_to(lpart, loss_ref.shape)

        @pl.when(pl.program_id(0) == 0)
        def _():
            dw_ref[...] = part
            loss_ref[...] = lrow

        @pl.when(pl.program_id(0) > 0)
        def _():
            dw_ref[...] += part
            loss_ref[...] += lrow

    row = pl.BlockSpec((ts, d), lambda i: (i, 0))
    vec = pl.BlockSpec((1, d), lambda i: (0, 0))
    return pl.pallas_call(
        body, name="loss_head", grid=(s // ts,),
        in_specs=[row, row, vec],
        out_specs=[pl.BlockSpec((1, LANES), lambda i: (0, 0)), row, row, vec],
        out_shape=[jax.ShapeDtypeStruct((1, LANES), F32), jax.ShapeDtypeStruct((s, d), F32),
                   jax.ShapeDtypeStruct((s, d), BF16), jax.ShapeDtypeStruct((1, d), F32)],
        compiler_params=_params(("arbitrary",)),
    )(x, target, w.reshape(1, d))


def _prep_fwd(aq, ak, av, rq, rk, rv, cos, sin, qw, kw, *, ts=256):
    s = aq.shape[0]
    ts = _tile(s, ts)
    attn_scale = HEAD_DIM ** -0.5
    ret_scale = RET_QK_DIM ** -0.5
    nq, nk, nr = aq.shape[1] // HEAD_DIM, ak.shape[1] // HEAD_DIM, rq.shape[1] // RET_QK_DIM

    def body(aq_ref, ak_ref, av_ref, rq_ref, rk_ref, rv_ref, cos_ref, sin_ref, qw_ref, kw_ref,
             q_out, k_out, v_out, rq_out, rk_out, rv_out):
        c, sn = cos_ref[...], sin_ref[...]

        def normed(u, w):
            return u * lax.rsqrt(jnp.mean(u * u, axis=-1, keepdims=True) + EPS) * w

        for j in range(nq):
            sl = slice(j * HEAD_DIM, (j + 1) * HEAD_DIM)
            q_out[:, sl] = (_rope(normed(aq_ref[:, sl], qw_ref[...]), c, sn) * attn_scale).astype(BF16)
        for j in range(nk):
            sl = slice(j * HEAD_DIM, (j + 1) * HEAD_DIM)
            k_out[:, sl] = _rope(normed(ak_ref[:, sl], kw_ref[...]), c, sn).astype(BF16)
        for j in range(nr):
            sl = slice(j * RET_QK_DIM, (j + 1) * RET_QK_DIM)
            rq_out[:, sl] = _rope(rq_ref[:, sl], c, sn).astype(BF16)
            rk_out[:, sl] = (_rope(rk_ref[:, sl], c, sn) * ret_scale).astype(BF16)
        v_out[...] = av_ref[...].astype(BF16)
        rv_out[...] = rv_ref[...].astype(BF16)

    def row(arr):
        return pl.BlockSpec((ts, arr.shape[1]), lambda i: (i, 0))

    vec = pl.BlockSpec((1, HEAD_DIM), lambda i: (0, 0))
    ins = [aq, ak, av, rq, rk, rv]
    return pl.pallas_call(
        body, name="prep_fwd", grid=(s // ts,),
        in_specs=[row(a) for a in ins] + [row(cos), row(sin), vec, vec],
        out_specs=[row(a) for a in ins],
        out_shape=[jax.ShapeDtypeStruct(a.shape, BF16) for a in ins],
        compiler_params=_params(("parallel",)),
    )(*ins, cos, sin, qw.reshape(1, HEAD_DIM), kw.reshape(1, HEAD_DIM))


def _prep_bwd(dq, dk, drq, drk, aq, ak, cos, sin, qw, kw, *, ts=256):
    s = aq.shape[0]
    ts = _tile(s, ts)
    attn_scale = HEAD_DIM ** -0.5
    ret_scale = RET_QK_DIM ** -0.5
    nq, nk, nr = aq.shape[1] // HEAD_DIM, ak.shape[1] // HEAD_DIM, drq.shape[1] // RET_QK_DIM

    def body(dq_ref, dk_ref, drq_ref, drk_ref, aq_ref, ak_ref, cos_ref, sin_ref, qw_ref, kw_ref,
             daq_out, dak_out, drq_out, drk_out, dqw_ref, dkw_ref):
        c, sn = cos_ref[...], sin_ref[...]

        def unrope(d):
            return _rope_transposed(d, c, sn)

        def norm_bwd(dun, u, w):
            r = lax.rsqrt(jnp.mean(u * u, axis=-1, keepdims=True) + EPS)
            uh = u * r
            dn = dun * w
            du = r * (dn - uh * jnp.mean(dn * uh, axis=-1, keepdims=True))
            return du, jnp.sum(dun * uh, axis=0, keepdims=True)

        dqw = jnp.zeros((1, HEAD_DIM), F32)
        for j in range(nq):
            sl = slice(j * HEAD_DIM, (j + 1) * HEAD_DIM)
            du, dw = norm_bwd(unrope(dq_ref[:, sl] * attn_scale), aq_ref[:, sl], qw_ref[...])
            daq_out[:, sl] = du.astype(BF16)
            dqw = dqw + dw
        dkw = jnp.zeros((1, HEAD_DIM), F32)
        for j in range(nk):
            sl = slice(j * HEAD_DIM, (j + 1) * HEAD_DIM)
            du, dw = norm_bwd(unrope(dk_ref[:, sl]), ak_ref[:, sl], kw_ref[...])
            dak_out[:, sl] = du.astype(BF16)
            dkw = dkw + dw
        for j in range(nr):
            sl = slice(j * RET_QK_DIM, (j + 1) * RET_QK_DIM)
            drq_out[:, sl] = unrope(drq_ref[:, sl]).astype(BF16)
            drk_out[:, sl] = unrope(drk_ref[:, sl] * ret_scale).astype(BF16)

        @pl.when(pl.program_id(0) == 0)
        def _():
            dqw_ref[...] = dqw
            dkw_ref[...] = dkw

        @pl.when(pl.program_id(0) > 0)
        def _():
            dqw_ref[...] += dqw
            dkw_ref[...] += dkw

    def row(arr):
        return pl.BlockSpec((ts, arr.shape[1]), lambda i: (i, 0))

    vec = pl.BlockSpec((1, HEAD_DIM), lambda i: (0, 0))
    ins = [dq, dk, drq, drk, aq, ak, cos, sin]
    outs = [dq, dk, drq, drk]
    return pl.pallas_call(
        body, name="prep_bwd", grid=(s // ts,),
        in_specs=[row(a) for a in ins] + [vec, vec],
        out_specs=[row(a) for a in outs] + [vec, vec],
        out_shape=[jax.ShapeDtypeStruct(a.shape, BF16) for a in outs]
        + [jax.ShapeDtypeStruct((1, HEAD_DIM), F32)] * 2,
        compiler_params=_params(("arbitrary",)),
    )(*ins, qw.reshape(1, HEAD_DIM), kw.reshape(1, HEAD_DIM))


def _attn_fwd(q, k, v, *, tq=4096, sub=256):
    s, aw = q.shape
    tq = _tile(s, tq)
    sub = _tile(tq, sub)
    heads, kvh = aw // HEAD_DIM, k.shape[1] // HEAD_DIM
    grp = heads // kvh

    def body(q_ref, k_ref, v_ref, o_ref, lse_ref):
        kv_ = k_ref[...]
        v_ext = jnp.concatenate([v_ref[...], jnp.ones((s, HEAD_DIM), BF16)], axis=-1)
        for r in range(tq // sub):
            rows = slice(r * sub, (r + 1) * sub)
            sc = lax.dot_general(q_ref[rows, :], kv_, NT_DIMS, preferred_element_type=F32)
            m = jnp.max(sc, axis=-1, keepdims=True)
            p = jnp.exp((sc - m).astype(BF16))
            oe = jnp.dot(p, v_ext, preferred_element_type=F32)
            l = oe[:, HEAD_DIM:HEAD_DIM + 1]
            o_ref[rows, :] = (oe[:, :HEAD_DIM] / l).astype(o_ref.dtype)
            lse_ref[rows, :] = jnp.broadcast_to(m + jnp.log(l), (sub, HEAD_DIM))

    qspec = pl.BlockSpec((tq, HEAD_DIM), lambda kv, g, i: (i, kv * grp + g))
    kspec = pl.BlockSpec((s, HEAD_DIM), lambda kv, g, i: (0, kv))
    return pl.pallas_call(
        body, name="attn_fwd", grid=(kvh, grp, s // tq),
        in_specs=[qspec, kspec, kspec],
        out_specs=[qspec, qspec],
        out_shape=[jax.ShapeDtypeStruct((s, aw), BF16), jax.ShapeDtypeStruct((s, aw), F32)],
        compiler_params=_params(("parallel", "parallel", "parallel")),
    )(q, k, v)


def _attn_bwd(q, k, v, o, do, lse, *, tq=1024, sub=256):
    s, aw = q.shape
    tq = _tile(s, tq)
    sub = _tile(tq, sub)
    heads, kvh = aw // HEAD_DIM, k.shape[1] // HEAD_DIM
    grp = heads // kvh
    nq = s // tq

    def body(q_ref, k_ref, v_ref, o_ref, do_ref, lse_ref, dq_ref, dk_ref, dv_ref, dk_acc, dv_acc, p_scr, ds_scr):
        g, i = pl.program_id(1), pl.program_id(2)
        kv_, vv = k_ref[...], v_ref[...]
        for r in range(tq // sub):
            rows = slice(r * sub, (r + 1) * sub)
            qv, dov = q_ref[rows, :], do_ref[rows, :]
            sc = lax.dot_general(qv, kv_, NT_DIMS, preferred_element_type=F32)
            p = jnp.exp((sc - lse_ref[rows, :1]).astype(BF16))
            dp = lax.dot_general(dov, vv, NT_DIMS, preferred_element_type=F32)
            delta = jnp.sum(dov.astype(F32) * o_ref[rows, :].astype(F32), axis=-1, keepdims=True)
            ds = p * (dp - delta).astype(BF16)
            dq_ref[rows, :] = jnp.dot(ds, kv_, preferred_element_type=F32)
            p_scr[rows, :] = p
            ds_scr[rows, :] = ds
        dvp = lax.dot_general(p_scr[...], do_ref[...], TN_DIMS, preferred_element_type=F32)
        dkp = lax.dot_general(ds_scr[...], q_ref[...], TN_DIMS, preferred_element_type=F32)
        first = jnp.logical_and(g == 0, i == 0)

        @pl.when(first)
        def _():
            dv_acc[...] = dvp
            dk_acc[...] = dkp

        @pl.when(jnp.logical_not(first))
        def _():
            dv_acc[...] += dvp
            dk_acc[...] += dkp

        @pl.when(jnp.logical_and(g == grp - 1, i == nq - 1))
        def _():
            dk_ref[...] = dk_acc[...]
            dv_ref[...] = dv_acc[...].astype(dv_ref.dtype)

    qspec = pl.BlockSpec((tq, HEAD_DIM), lambda kv, g, i: (i, kv * grp + g))
    kspec = pl.BlockSpec((s, HEAD_DIM), lambda kv, g, i: (0, kv))
    return pl.pallas_call(
        body, name="attn_bwd", grid=(kvh, grp, nq),
        in_specs=[qspec, kspec, kspec, qspec, qspec, qspec],
        out_specs=[qspec, kspec, kspec],
        out_shape=[jax.ShapeDtypeStruct((s, aw), F32), jax.ShapeDtypeStruct(k.shape, F32),
                   jax.ShapeDtypeStruct(v.shape, BF16)],
        scratch_shapes=[pltpu.VMEM((s, HEAD_DIM), F32), pltpu.VMEM((s, HEAD_DIM), F32),
                        pltpu.VMEM((tq, s), BF16), pltpu.VMEM((tq, s), BF16)],
        compiler_params=_params(("parallel", "arbitrary", "arbitrary")),
    )(q, k, v, o, do, lse)


def _sum_all(z):
    return jnp.sum(jnp.sum(z, axis=0, keepdims=True), axis=1, keepdims=True)


def _chunk_consts(df_ref, db_ref, t):
    lf = _log_sigmoid(df_ref[0][:, :1])
    lb = _log_sigmoid(db_ref[0][:, :1])
    r = lax.broadcasted_iota(jnp.int32, (t, 1), 0).astype(F32)
    c = lax.broadcasted_iota(jnp.int32, (1, t), 1).astype(F32)
    diff = r - c
    dm = jnp.exp(diff * jnp.where(diff >= 0, lf, -lb))
    return dict(diff=diff, dm=dm, r=r,
                af=jnp.exp(lf * (r + 1.0)), bf=jnp.exp(lf * (t - 1.0 - r)), gf=jnp.exp(lf * t),
                ab=jnp.exp(lb * (t - r)), bb=jnp.exp(lb * r), gb=jnp.exp(lb * t))


def _scaled(x, f):
    return (x.astype(F32) * f).astype(BF16)


def _retc_specs(s):
    qspec = pl.BlockSpec((s, RET_QK_DIM), lambda h: (0, h))
    vspec = pl.BlockSpec((s, RET_V_DIM), lambda h: (0, h))
    dspec = pl.BlockSpec((1, 1, LANES), lambda h: (h, 0, 0))
    return qspec, vspec, dspec


def _retc_fwd(q, k, v, dec_f, dec_b, *, t=256):
    s, qw = q.shape
    t = _tile(s, t)
    heads, nc = qw // RET_QK_DIM, s // t
    qspec, vspec, dspec = _retc_specs(s)

    def body(q_ref, k_ref, v_ref, df_ref, db_ref, o_ref):
        cs = _chunk_consts(df_ref, db_ref, t)

        def rows_of(i):
            return pl.ds(pl.multiple_of(i * t, t), t)

        def forward(i, sf):
            rows = rows_of(i)
            qi, ki, vi = q_ref[rows, :], k_ref[rows, :], v_ref[rows, :]
            sc = lax.dot_general(qi, ki, NT_DIMS, preferred_element_type=F32)
            intra = jnp.dot((sc * cs["dm"]).astype(BF16), vi, preferred_element_type=F32)
            cross = jnp.dot(_scaled(qi, cs["af"]), sf.astype(BF16), preferred_element_type=F32)
            o_ref[rows, :] = intra + cross
            return cs["gf"] * sf + lax.dot_general(_scaled(ki, cs["bf"]), vi, TN_DIMS, preferred_element_type=F32)

        def backward(j, sb):
            rows = rows_of(nc - 1 - j)
            qi, ki, vi = q_ref[rows, :], k_ref[rows, :], v_ref[rows, :]
            o_ref[rows, :] += jnp.dot(_scaled(qi, cs["ab"]), sb.astype(BF16), preferred_element_type=F32)
            return cs["gb"] * sb + lax.dot_general(_scaled(ki, cs["bb"]), vi, TN_DIMS, preferred_element_type=F32)

        zero = jnp.zeros((RET_QK_DIM, RET_V_DIM), F32)
        lax.fori_loop(0, nc, forward, zero, unroll=True)
        lax.fori_loop(0, nc, backward, zero, unroll=True)

    return pl.pallas_call(
        body, name="ret_fwd", grid=(heads,),
        in_specs=[qspec, qspec, vspec, dspec, dspec],
        out_specs=vspec, out_shape=jax.ShapeDtypeStruct(v.shape, F32),
        compiler_params=_params(("parallel",)),
    )(q, k, v, dec_f, dec_b)


def _retc_bwd(q, k, v, do, dec_f, dec_b, *, t=256):
    s, qw = q.shape
    t = _tile(s, t)
    heads, nc = qw // RET_QK_DIM, s // t
    qspec, vspec, dspec = _retc_specs(s)
    gspec = pl.BlockSpec((1, 8, LANES), lambda h: (h, 0, 0))

    def body(q_ref, k_ref, v_ref, do_ref, df_ref, db_ref, dq_ref, dk_ref, dv_ref, gf_ref, gb_ref,
             sf_scr, sb_scr, dv_acc):
        cs = _chunk_consts(df_ref, db_ref, t)
        r, diff, dm = cs["r"], cs["diff"], cs["dm"]

        def rows_of(i):
            return pl.ds(pl.multiple_of(i * t, t), t)

        def tn(a, b):
            return lax.dot_general(a, b, TN_DIMS, preferred_element_type=F32)

        def nt(a, b):
            return lax.dot_general(a, b, NT_DIMS, preferred_element_type=F32)

        def states_f(i, sf):
            sf_scr[i] = sf
            rows = rows_of(i)
            return cs["gf"] * sf + tn(_scaled(k_ref[rows, :], cs["bf"]), v_ref[rows, :])

        def states_b(j, sb):
            i = nc - 1 - j
            sb_scr[i] = sb
            rows = rows_of(i)
            return cs["gb"] * sb + tn(_scaled(k_ref[rows, :], cs["bb"]), v_ref[rows, :])

        zero = jnp.zeros((RET_QK_DIM, RET_V_DIM), F32)
        lax.fori_loop(0, nc, states_f, zero, unroll=True)
        lax.fori_loop(0, nc, states_b, zero, unroll=True)

        def scan_grads(i, state, u, qf, kf, vi, doi, fa, fb, step, wa, wb):
            qa, kb = qf * fa, kf * fb
            ub = u.astype(BF16)
            dqa = nt(doi, state.astype(BF16))
            dkb = nt(vi, ub)
            dv = jnp.dot(kb.astype(BF16), ub, preferred_element_type=F32)
            dlog = _sum_all(dqa * qa * wa) + _sum_all(dkb * kb * wb) + t * step * _sum_all(u * state)
            u_new = step * u + tn(qa.astype(BF16), doi)
            return dqa * fa, dkb * fb, dv, u_new, dlog

        def sweep_f(j, carry):
            u, accf, accb = carry
            i = nc - 1 - j
            rows = rows_of(i)
            qi, ki, vi, doi = q_ref[rows, :], k_ref[rows, :], v_ref[rows, :], do_ref[rows, :]
            sc = nt(qi, ki)
            p = sc * dm
            dp = nt(doi, vi)
            ds = (dp * dm).astype(BF16)
            tt = dp * p * diff
            accf = accf + _sum_all(jnp.where(diff > 0, tt, 0.0))
            accb = accb + _sum_all(jnp.where(diff < 0, -tt, 0.0))
            dq1, dk1, dv1, u, dlog = scan_grads(i, sf_scr[i], u, qi.astype(F32), ki.astype(F32), vi, doi,
                                                cs["af"], cs["bf"], cs["gf"], r + 1.0, t - 1.0 - r)
            dq_ref[rows, :] = jnp.dot(ds, ki, preferred_element_type=F32) + dq1
            dk_ref[rows, :] = tn(ds, qi) + dk1
            dv_acc[rows, :] = tn(p.astype(BF16), doi) + dv1
            return u, accf + dlog, accb

        def sweep_b(i, carry):
            w, accb = carry
            rows = rows_of(i)
            qi, ki, vi, doi = q_ref[rows, :], k_ref[rows, :], v_ref[rows, :], do_ref[rows, :]
            dq1, dk1, dv1, w, dlog = scan_grads(i, sb_scr[i], w, qi.astype(F32), ki.astype(F32), vi, doi,
                                                cs["ab"], cs["bb"], cs["gb"], t - r, r)
            dq_ref[rows, :] += dq1
            dk_ref[rows, :] += dk1
            dv_acc[rows, :] += dv1
            return w, accb + dlog

        z11 = jnp.zeros((1, 1), F32)
        _, accf, accb = lax.fori_loop(0, nc, sweep_f, (zero, z11, z11), unroll=2)
        _, accb = lax.fori_loop(0, nc, sweep_b, (zero, accb), unroll=2)
        dv_ref[...] = dv_acc[...].astype(dv_ref.dtype)
        gf_ref[...] = jnp.broadcast_to((accf / (1.0 + jnp.exp(df_ref[0][:, :1]))).reshape(1, 1, 1), gf_ref.shape)
        gb_ref[...] = jnp.broadcast_to((accb / (1.0 + jnp.exp(db_ref[0][:, :1]))).reshape(1, 1, 1), gb_ref.shape)

    return pl.pallas_call(
        body, name="ret_bwd", grid=(heads,),
        in_specs=[qspec, qspec, vspec, vspec, dspec, dspec],
        out_specs=[qspec, qspec, vspec, gspec, gspec],
        out_shape=[jax.ShapeDtypeStruct(q.shape, F32), jax.ShapeDtypeStruct(k.shape, F32),
                   jax.ShapeDtypeStruct(v.shape, BF16),
                   jax.ShapeDtypeStruct((heads, 8, LANES), F32), jax.ShapeDtypeStruct((heads, 8, LANES), F32)],
        scratch_shapes=[pltpu.VMEM((nc, RET_QK_DIM, RET_V_DIM), F32), pltpu.VMEM((nc, RET_QK_DIM, RET_V_DIM), F32),
                        pltpu.VMEM((s, RET_V_DIM), F32)],
        compiler_params=_params(("parallel",)),
    )(q, k, v, do, dec_f, dec_b)


def _gate_fwd(att, ag, ret, rg, rnw, *, ts=256):
    s, aw = att.shape
    rw = ret.shape[1]
    ts = _tile(s, ts)
    rheads = rw // RET_V_DIM

    def body(att_ref, ag_ref, ret_ref, rg_ref, w_ref, y_ref, yt_ref):
        def put(lo, hi, val):
            y_ref[:, lo:hi] = val.astype(BF16)
            yt_ref[lo:hi, :] = val.T.astype(BF16)

        sa, _ = _silu_parts(ag_ref[...])
        put(0, aw, sa * att_ref[...].astype(F32))
        for h in range(rheads):
            sl = slice(h * RET_V_DIM, (h + 1) * RET_V_DIM)
            rt = ret_ref[:, sl]
            rn = rt * lax.rsqrt(jnp.mean(rt * rt, axis=-1, keepdims=True) + EPS) * w_ref[:, sl]
            sr, _ = _silu_parts(rg_ref[:, sl])
            put(aw + h * RET_V_DIM, aw + (h + 1) * RET_V_DIM, sr * rn)

    def row(w):
        return pl.BlockSpec((ts, w), lambda i: (i, 0))

    return pl.pallas_call(
        body, name="gate_fwd", grid=(s // ts,),
        in_specs=[row(aw), row(aw), row(rw), row(rw), pl.BlockSpec((1, rw), lambda i: (0, 0))],
        out_specs=[row(aw + rw), pl.BlockSpec((aw + rw, ts), lambda i: (0, i))],
        out_shape=[jax.ShapeDtypeStruct((s, aw + rw), BF16), jax.ShapeDtypeStruct((aw + rw, s), BF16)],
        compiler_params=_params(("parallel",)),
    )(att, ag, ret, rg, rnw.reshape(1, rw))


def _gate_bwd(dy, att, ag, ret, rg, rnw, *, ts=256):
    s, aw = att.shape
    rw = ret.shape[1]
    ts = _tile(s, ts)
    rheads = rw // RET_V_DIM

    def body(dy_ref, att_ref, ag_ref, ret_ref, rg_ref, w_ref, datt_ref, dag_ref, dret_ref, drg_ref, dw_ref):
        sa, dsa = _silu_parts(ag_ref[...])
        dya = dy_ref[:, :aw]
        datt_ref[...] = (dya * sa).astype(BF16)
        dag_ref[...] = (dya * att_ref[...].astype(F32) * dsa).astype(BF16)
        parts = []
        for h in range(rheads):
            sl = slice(h * RET_V_DIM, (h + 1) * RET_V_DIM)
            rt = ret_ref[:, sl]
            rr = lax.rsqrt(jnp.mean(rt * rt, axis=-1, keepdims=True) + EPS)
            rh = rt * rr
            wv = w_ref[:, sl]
            sr, dsr = _silu_parts(rg_ref[:, sl])
            dyr = dy_ref[:, aw + h * RET_V_DIM:aw + (h + 1) * RET_V_DIM]
            drg_ref[:, sl] = (dyr * rh * wv * dsr).astype(BF16)
            drn = dyr * sr
            dn = drn * wv
            dret_ref[:, sl] = (rr * (dn - rh * jnp.mean(dn * rh, axis=-1, keepdims=True))).astype(BF16)
            parts.append(jnp.sum(drn * rh, axis=0, keepdims=True))
        part = jnp.concatenate(parts, axis=-1)

        @pl.when(pl.program_id(0) == 0)
        def _():
            dw_ref[...] = part

        @pl.when(pl.program_id(0) > 0)
        def _():
            dw_ref[...] += part

    def row(w):
        return pl.BlockSpec((ts, w), lambda i: (i, 0))

    vec = pl.BlockSpec((1, rw), lambda i: (0, 0))
    return pl.pallas_call(
        body, name="gate_bwd", grid=(s // ts,),
        in_specs=[row(aw + rw), row(aw), row(aw), row(rw), row(rw), vec],
        out_specs=[row(aw), row(aw), row(rw), row(rw), vec],
        out_shape=[jax.ShapeDtypeStruct((s, aw), BF16), jax.ShapeDtypeStruct((s, aw), BF16),
                   jax.ShapeDtypeStruct((s, rw), BF16), jax.ShapeDtypeStruct((s, rw), BF16),
                   jax.ShapeDtypeStruct((1, rw), F32)],
        compiler_params=_params(("arbitrary",)),
    )(dy, att, ag, ret, rg, rnw.reshape(1, rw))


def _mesh_position():
    x, y, c = lax.axis_index("x"), lax.axis_index("y"), lax.axis_index("c")
    return x, y, c, 4 * x + 2 * y + c


def _peer(x, y, c, k):
    px = 1 - x if k & 4 else x
    py = 1 - y if k & 2 else y
    pc = 1 - c if k & 1 else c
    return (px, py, pc), 4 * px + 2 * py + pc


HBM_SPEC = pl.BlockSpec(memory_space=pltpu.HBM)
SEM_SPEC = pl.BlockSpec(memory_space=pltpu.SEMAPHORE)
ANY_SPEC = pl.BlockSpec(memory_space=pl.ANY)
DATAFLOW = pltpu.SideEffectType.DATAFLOW_SIDE_EFFECTING


def _hbm(a):
    return pltpu.with_memory_space_constraint(a, pltpu.HBM)


def _split_start(name, copies, n, src, land, after):
    def body(*refs):
        (send_sems, recv_sems), token = refs[2 + len(after):4 + len(after)], refs[-1]
        sends, _ = copies(refs[0], refs[1], send_sems, recv_sems)
        for cp in sends:
            cp.start()
        token[...] = jnp.zeros_like(token)

    return pl.pallas_call(
        body, name=name,
        out_shape=(pltpu.SemaphoreType.DMA((n,)), pltpu.SemaphoreType.DMA((n,)),
                   pltpu.HBM(src.shape, src.dtype), pltpu.HBM(land.shape, land.dtype),
                   jax.ShapeDtypeStruct((8, LANES), F32)),
        in_specs=[HBM_SPEC] * 2 + [ANY_SPEC] * len(after),
        out_specs=(SEM_SPEC, SEM_SPEC, HBM_SPEC, HBM_SPEC, pl.BlockSpec(memory_space=pltpu.VMEM)),
        input_output_aliases={0: 2, 1: 3},
        compiler_params=pltpu.CompilerParams(has_side_effects=DATAFLOW),
    )(_hbm(src), _hbm(land), *after)


def _split_wait(name, copies, started, after):
    send_sems, recv_sems, src, land = started[:4]

    def body(*refs):
        sends, recvs = copies(refs[0], refs[1], refs[2], refs[3])
        for cp in sends:
            cp.wait_send()
        for cp in recvs:
            cp.wait_recv()

    return pl.pallas_call(
        body, name=name,
        out_shape=(pltpu.HBM(src.shape, src.dtype), pltpu.HBM(land.shape, land.dtype)),
        in_specs=[HBM_SPEC] * 2 + [SEM_SPEC, SEM_SPEC] + [ANY_SPEC] * len(after),
        out_specs=(HBM_SPEC,) * 2,
        input_output_aliases={0: 0, 1: 1},
        compiler_params=pltpu.CompilerParams(has_side_effects=DATAFLOW),
    )(src, land, send_sems, recv_sems, *after)[1]


def _slab(ref, p, size, axis):
    if axis == 1:
        return ref.at[:, pl.ds(pl.multiple_of(p * size, LANES), size)]
    return ref.at[pl.ds(pl.multiple_of(p * size, 16), size), :]


ALL_PEERS = tuple(range(1, N_DEV))
SIBLING = 1
SAME_CORE_OF_CHIPS = (2, 4, 6)


def _gather_copies(size, axis, ks):
    def copies(shard_ref, full_ref, send_sems, recv_sems):
        x, y, c, me = _mesh_position()
        sends, recvs = [], []
        for j, k in enumerate(ks):
            peer, pid = _peer(x, y, c, k)
            sends.append(pltpu.make_async_remote_copy(
                src_ref=shard_ref, dst_ref=_slab(full_ref, me, size, axis), send_sem=send_sems.at[j],
                recv_sem=recv_sems.at[j], device_id=peer, device_id_type=MESH))
            recvs.append(pltpu.make_async_remote_copy(
                src_ref=shard_ref, dst_ref=_slab(full_ref, pid, size, axis), send_sem=send_sems.at[j],
                recv_sem=recv_sems.at[j], device_id=peer, device_id_type=MESH))
        return sends, recvs

    return copies


def _pass_on_copies(size, axis):
    def copies(shard_ref, full_ref, send_sems, recv_sems):
        x, y, c, me = _mesh_position()
        sibling, _ = _peer(x, y, c, SIBLING)
        sends, recvs = [], []
        for j, k in enumerate(SAME_CORE_OF_CHIPS):
            _, landed = _peer(x, y, c, k)
            _, siblings = _peer(x, y, c, k ^ SIBLING)
            mine = _slab(full_ref, landed, size, axis)
            sends.append(pltpu.make_async_remote_copy(
                src_ref=mine, dst_ref=mine, send_sem=send_sems.at[j], recv_sem=recv_sems.at[j],
                device_id=sibling, device_id_type=MESH))
            recvs.append(pltpu.make_async_remote_copy(
                src_ref=mine, dst_ref=_slab(full_ref, siblings, size, axis), send_sem=send_sems.at[j],
                recv_sem=recv_sems.at[j], device_id=sibling, device_id_type=MESH))
        return sends, recvs

    return copies


def _scatter_copies(size, axis):
    def copies(grad_ref, land_ref, send_sems, recv_sems):
        x, y, c, me = _mesh_position()
        sends, recvs = [], []
        for k in range(1, N_DEV):
            peer, pid = _peer(x, y, c, k)
            src = _slab(grad_ref, pid, size, axis)
            sends.append(pltpu.make_async_remote_copy(
                src_ref=src, dst_ref=land_ref.at[me], send_sem=send_sems.at[k - 1], recv_sem=recv_sems.at[k - 1],
                device_id=peer, device_id_type=MESH))
            recvs.append(pltpu.make_async_remote_copy(
                src_ref=src, dst_ref=land_ref.at[pid], send_sem=send_sems.at[k - 1], recv_sem=recv_sems.at[k - 1],
                device_id=peer, device_id_type=MESH))
        return sends, recvs

    return copies


PLACE_BANDS = 8


def _place_own(name, src, out_shape, in_spec, out_spec, steps, me):
    def body(me_ref, src_ref, out_ref):
        out_ref[...] = src_ref[...]

    return pl.pallas_call(
        body, name=name, out_shape=out_shape,
        grid_spec=pltpu.PrefetchScalarGridSpec(num_scalar_prefetch=1, grid=(steps,), in_specs=[in_spec],
                                               out_specs=out_spec),
        compiler_params=_params(("parallel",)),
    )(me.reshape(1).astype(jnp.int32), src)


def _gather_start(shard, axis, ks, me, after, tag):
    rows, cols = shard.shape
    size = shard.shape[axis]
    full_shape = tuple(N_DEV * n if a == axis else n for a, n in enumerate(shard.shape))
    band = rows // PLACE_BANDS
    in_spec = pl.BlockSpec((band, cols), lambda i, me_ref: (i, 0))
    if axis == 1:
        out_spec = pl.BlockSpec((band, cols), lambda i, me_ref: (i, me_ref[0]))
    else:
        out_spec = pl.BlockSpec((band, cols), lambda i, me_ref: (me_ref[0] * PLACE_BANDS + i, 0))
    full = _place_own("place_shard", shard, jax.ShapeDtypeStruct(full_shape, shard.dtype), in_spec, out_spec,
                      PLACE_BANDS, me)
    return _split_start("gather_start_" + tag, _gather_copies(size, axis, ks), len(ks), shard, full, after)


def _gather_wait(started, axis, ks, after, tag):
    size = started[2].shape[axis]
    return _split_wait("gather_wait_" + tag, _gather_copies(size, axis, ks), started, after)


def _pass_on_start(shard, full, axis, after, tag):
    size = shard.shape[axis]
    return _split_start("pass_on_start_" + tag, _pass_on_copies(size, axis), len(SAME_CORE_OF_CHIPS), shard, full, after)


def _pass_on_wait(started, axis, after, tag):
    size = started[2].shape[axis]
    return _split_wait("pass_on_wait_" + tag, _pass_on_copies(size, axis), started, after)


def _scatter_start(grad, axis, me, tag):
    size = grad.shape[axis] // N_DEV
    rows, cols = tuple(size if a == axis else n for a, n in enumerate(grad.shape))
    band = rows // PLACE_BANDS
    if axis == 1:
        in_spec = pl.BlockSpec((band, cols), lambda i, me_ref: (i, me_ref[0]))
    else:
        in_spec = pl.BlockSpec((band, cols), lambda i, me_ref: (me_ref[0] * PLACE_BANDS + i, 0))
    out_spec = pl.BlockSpec((None, band, cols), lambda i, me_ref: (me_ref[0], i, 0))
    land = _place_own("place_slab", grad, jax.ShapeDtypeStruct((N_DEV, rows, cols), grad.dtype), in_spec, out_spec,
                      PLACE_BANDS, me)
    return _split_start("scatter_start_" + tag, _scatter_copies(size, axis), N_DEV - 1, grad, land, [])


def _scatter_wait(started, axis, after, tag):
    size = started[2].shape[axis] // N_DEV
    return _split_wait("scatter_wait_" + tag, _scatter_copies(size, axis), started, after)


def _exchange_small(buf, *, name, after=()):
    r = buf.shape[0]

    def body(*refs):
        buf_ref = refs[0]
        all_ref, sum_ref, send_sems, recv_sems = refs[1 + len(after):]
        x, y, c, me = _mesh_position()
        all_ref[me] = buf_ref[...]
        sends, recvs = [], []
        for k in range(1, N_DEV):
            peer, pid = _peer(x, y, c, k)
            sends.append(pltpu.make_async_remote_copy(
                src_ref=buf_ref, dst_ref=all_ref.at[me], send_sem=send_sems.at[k - 1], recv_sem=recv_sems.at[k - 1],
                device_id=peer, device_id_type=MESH))
            recvs.append(pltpu.make_async_remote_copy(
                src_ref=buf_ref, dst_ref=all_ref.at[pid], send_sem=send_sems.at[k - 1], recv_sem=recv_sems.at[k - 1],
                device_id=peer, device_id_type=MESH))
        for cp in sends:
            cp.start()
        for cp in recvs:
            cp.wait_recv()
        for cp in sends:
            cp.wait_send()
        total = all_ref[0]
        for p in range(1, N_DEV):
            total = total + all_ref[p]
        sum_ref[...] = total

    vmem = pl.BlockSpec(memory_space=pltpu.VMEM)
    return pl.pallas_call(
        body, name=name,
        in_specs=[vmem] + [ANY_SPEC] * len(after), out_specs=[vmem, vmem],
        out_shape=[jax.ShapeDtypeStruct((N_DEV, r, LANES), F32), jax.ShapeDtypeStruct((r, LANES), F32)],
        scratch_shapes=[pltpu.SemaphoreType.DMA((N_DEV - 1,)), pltpu.SemaphoreType.DMA((N_DEV - 1,))],
        compiler_params=pltpu.CompilerParams(has_side_effects=True),
    )(buf, *after)


def _adamw_math(w, g, m, v):
    m2 = ADAM_B1 * m + (1.0 - ADAM_B1) * g
    v2 = ADAM_B2 * v + (1.0 - ADAM_B2) * (g * g)
    delta = -ADAM_LR * ((m2 / ADAM_C1) / (jnp.sqrt(v2 / ADAM_C2) + ADAM_EPS) + ADAM_WD * w)
    return delta, m2, v2


def _adamw_slabs(layer, w, m, v, land, outs, order, *, tr, name):
    depth, r, c = w.shape
    tr = _tile(r, tr)

    def body(w_ref, m_ref, v_ref, land_ref, order_ref, o0, o1, o2, o3, g_ref, d_ref, m2_ref, v2_ref):
        g = land_ref[0].astype(F32)
        for p in range(1, N_DEV):
            g = g + land_ref[p].astype(F32)
        delta, m2, v2 = _adamw_math(w_ref[...], g, m_ref[...], v_ref[...])
        g_ref[...] = g
        d_ref[...] = delta
        m2_ref[...] = m2
        v2_ref[...] = v2

    row = pl.BlockSpec((None, tr, c), lambda i: (layer, i, 0))
    return pl.pallas_call(
        body, name=name, grid=(r // tr,),
        in_specs=[row, row, row, pl.BlockSpec((N_DEV, tr, c), lambda i: (0, i, 0)),
                  pl.BlockSpec((8, LANES), lambda i: (0, 0))] + [ANY_SPEC] * 4,
        out_specs=[row] * 4, out_shape=[jax.ShapeDtypeStruct((depth, r, c), F32)] * 4,
        input_output_aliases={5: 0, 6: 1, 7: 2, 8: 3},
        compiler_params=_params(("parallel",)),
    )(w, m, v, land, order, *outs)


def _adamw_small(w, g, m, v):
    def body(w_ref, g_ref, m_ref, v_ref, d_ref, m2_ref, v2_ref):
        delta, m2, v2 = _adamw_math(w_ref[...], g_ref[...], m_ref[...], v_ref[...])
        d_ref[...] = delta
        m2_ref[...] = m2
        v2_ref[...] = v2

    vmem = pl.BlockSpec(memory_space=pltpu.VMEM)
    return pl.pallas_call(
        body, name="adamw_small", in_specs=[vmem] * 4, out_specs=[vmem] * 3,
        out_shape=[jax.ShapeDtypeStruct(w.shape, F32)] * 3,
    )(w, g, m, v)


def _pack(parts):
    flat = jnp.concatenate([p.reshape(-1).astype(F32) for p in parts])
    rows = -(-flat.shape[0] // LANES)
    rows = -(-rows // SMALL_ROWS_ALIGN) * SMALL_ROWS_ALIGN
    flat = jnp.pad(flat, (0, rows * LANES - flat.shape[0]))
    return flat.reshape(rows, LANES)


def _unpack(buf, shapes):
    flat = buf.reshape(-1)
    out, pos = [], 0
    for shp in shapes:
        size = math.prod(shp)
        out.append(flat[pos:pos + size].reshape(shp))
        pos += size
    return out


def _section_widths(d):
    aw = d // 2
    kw = aw // ATTN_GROUP
    rw = d - aw
    rqw = (rw // RET_V_DIM) * RET_QK_DIM
    return (aw, kw, kw, aw, rqw, rqw, rw, rw)


def _layer_fwd(xl, nw, win_full, after_attn, wout_of, qn, kn, dec_f, dec_b, rn, cos, sin):
    widths = _section_widths(xl.shape[1])
    offs = tuple(int(o) for o in np.cumsum((0,) + widths)[:-1])
    sec_tn = _tile(widths[1], 512)
    h, ht = _rms_fwd(xl, nw)
    secs = [_matmul(h, win_full, name="proj", b_off=offs[i], n=widths[i], tn=sec_tn) for i in range(8)]
    aq, ak, av, ag, rq, rk, rv, rg = secs
    q, k, v, rqr, rkr, rvb = _prep_fwd(aq, ak, av, rq, rk, rv, cos, sin, qn, kn)
    att, lse = _attn_fwd(q, k, v)
    ret = _retc_fwd(rqr, rkr, rvb, dec_f + after_attn(att), dec_b)
    y, yt = _gate_fwd(att, ag, ret, rg, rn)
    wout_full = wout_of(y)
    xn = _matmul(y, wout_full, name="out_proj", residual=xl)
    saved = dict(x=xl, ht=ht, aq=aq, ak=ak, ag=ag, rg=rg, q=q, k=k, v=v, rq=rqr, rk=rkr, rv=rvb,
                 att=att, lse=lse, ret=ret, yt=yt, win=win_full, wout=wout_full)
    return xn, saved


def _layer_bwd_weights(gb, sv, qn, kn, dec_f, dec_b, rn, cos, sin, on_dwout):
    dy = _matmul(gb, sv["wout"], name="d_y", trans_b=True)
    dwout = _matmul(sv["yt"], gb, name="d_wout", out_dtype=BF16)
    datt, dag, dret, drg, drn = _gate_bwd(dy, sv["att"], sv["ag"], sv["ret"], sv["rg"], rn + on_dwout(dwout))
    dq, dk, dav = _attn_bwd(sv["q"], sv["k"], sv["v"], sv["att"], datt, sv["lse"])
    drq, drk, drv, gf, gbk = _retc_bwd(sv["rq"], sv["rk"], sv["rv"], dret, dec_f, dec_b)
    daq, dak, drq_p, drk_p, dqn, dkn = _prep_bwd(dq, dk, drq, drk, sv["aq"], sv["ak"], cos, sin, qn, kn)
    dproj = jnp.concatenate([daq, dak, dav, dag, drq_p, drk_p, drv, drg], axis=-1)
    dwin = _matmul(sv["ht"], dproj, name="d_win", out_dtype=BF16)
    small = dict(qn=dqn[0], kn=dkn[0], df=gf[:, 0, 0], db=gbk[:, 0, 0], rn=drn[0])
    return dproj, dwin, small


def _layer_bwd_input(g, dproj, sv, nw):
    dh = _matmul(dproj, sv["win"], name="d_h", trans_b=True, tk=_tile(dproj.shape[1], 5632, LANES))
    g, gb, dnw = _rms_bwd(dh, sv["x"], g, nw)
    return g, gb, dnw[0]


def kernel(x, norm_w, w_in, q_norm, k_norm, ret_decay_fwd, ret_decay_bwd, ret_norm, w_out, final_norm, loss_target, m_norm_w, m_w_in, m_q_norm, m_k_norm, m_ret_decay_fwd, m_ret_decay_bwd, m_ret_norm, m_w_out, m_final_norm, v_norm_w, v_w_in, v_q_norm, v_k_norm, v_ret_decay_fwd, v_ret_decay_bwd, v_ret_norm, v_w_out, v_final_norm):
    depth, d, _ = w_in.shape
    seq = x.shape[1]
    rw = _section_widths(d)[6]
    rheads = rw // RET_V_DIM
    rns = ret_norm.shape[-1]
    _, _, _, me = _mesh_position()

    target = loss_target[0]
    cos, sin = _rope_tables(seq)

    rn_all, _ = _exchange_small(_pack([ret_norm]), name="gather_ret_norm")
    rn_full = rn_all.reshape(N_DEV, -1)[:, :depth * rheads * rns].reshape(N_DEV, depth, rheads, rns)
    rn_full = jnp.transpose(rn_full, (1, 2, 0, 3)).reshape(depth, rw)

    dec_f = jnp.broadcast_to(ret_decay_fwd[:, :, None, None], (depth, rheads, 1, LANES))
    dec_b = jnp.broadcast_to(ret_decay_bwd[:, :, None, None], (depth, rheads, 1, LANES))

    win_bf = w_in.astype(BF16)
    wout_bf = w_out.astype(BF16)

    saved = []
    xl = x[0]
    first = (SIBLING,) + SAME_CORE_OF_CHIPS
    landed = _gather_wait(_gather_start(win_bf[0], 1, first, me, [], "in0"), 1, first, [], "in0")
    win_full = _pass_on_wait(_pass_on_start(win_bf[0], landed, 1, [], "in0"), 1, [], "in0")
    for l in range(depth):
        out_sent = _gather_start(wout_bf[l], 0, ALL_PEERS, me, [win_full], "out" + str(l))
        nw = norm_w[l] + out_sent[-1][0, 0]
        passed = {}
        if l + 1 < depth:
            in_sent = _gather_start(win_bf[l + 1], 1, first, me, [win_full, out_sent[-1]], "in" + str(l + 1))
            nw = nw + in_sent[-1][0, 0]

        def after_attn(att, passed=passed, l=l):
            if l + 1 == depth:
                return 0.0
            landed = _gather_wait(in_sent, 1, first, [att], "in" + str(l + 1))
            passed["on"] = _pass_on_start(win_bf[l + 1], landed, 1, [], "in" + str(l + 1))
            return passed["on"][-1][0, 0]

        def wout_of(y, out_sent=out_sent, l=l):
            return _gather_wait(out_sent, 0, ALL_PEERS, [y], "out" + str(l))

        xl, sv = _layer_fwd(xl, nw, win_full, after_attn, wout_of, q_norm[l], k_norm[l], dec_f[l], dec_b[l],
                            rn_full[l], cos, sin)
        saved.append(sv)
        if l + 1 < depth:
            win_full = _pass_on_wait(passed["on"], 1, [xl], "in" + str(l + 1))

    loss_row, g, gb, d_final = _loss_head(xl, target, final_norm)

    d_norm, d_qn, d_kn, d_df, d_db, d_rn = [], [], [], [], [], []
    lands = [None] * depth
    pending = None
    for l in reversed(range(depth)):
        sent = {}

        def on_dwout(dwout, sent=sent, l=l):
            sent["out"] = _scatter_start(dwout, 0, me, "out" + str(l))
            return sent["out"][-1][0, 0]

        dproj, dwin, sm = _layer_bwd_weights(gb, saved[l], q_norm[l], k_norm[l], dec_f[l], dec_b[l],
                                             rn_full[l], cos, sin, on_dwout)
        sent["in"] = _scatter_start(dwin, 1, me, "in" + str(l))
        g, gb, dnw = _layer_bwd_input(g, dproj, saved[l], norm_w[l] + sent["in"][-1][0, 0])
        if pending is not None:
            lands[l + 1] = (_scatter_wait(pending["in"], 1, [g], "in" + str(l + 1)),
                            _scatter_wait(pending["out"], 0, [g], "out" + str(l + 1)))
        pending = sent
        d_norm.append(dnw)
        d_qn.append(sm["qn"])
        d_kn.append(sm["kn"])
        d_df.append(sm["df"])
        d_db.append(sm["db"])
        d_rn.append(sm["rn"])
    for lst in (d_norm, d_qn, d_kn, d_df, d_db, d_rn):
        lst.reverse()
    order = pending["in"][-1]
    in_outs = [lax.empty(w_in.shape, F32) for _ in range(4)]
    out_outs = [lax.empty(w_out.shape, F32) for _ in range(4)]
    for l in reversed(range(depth)):
        if l == 0:
            lands[0] = (_scatter_wait(pending["in"], 1, [g, in_outs[0], out_outs[0]], "in0"),
                        _scatter_wait(pending["out"], 0, [g], "out0"))
        in_outs = _adamw_slabs(l, w_in, m_w_in, v_w_in, lands[l][0], in_outs, order, tr=256, name="adamw_w_in")
        out_outs = _adamw_slabs(l, w_out, m_w_out, v_w_out, lands[l][1], out_outs, order, tr=64, name="adamw_w_out")

    small_shapes = [(depth, d), (depth, HEAD_DIM), (depth, HEAD_DIM), (depth, rheads), (depth, rheads),
                    (depth, rheads, N_DEV * rns), (d,), (1,)]
    grads_local = [jnp.stack(d_norm), jnp.stack(d_qn), jnp.stack(d_kn), jnp.stack(d_df), jnp.stack(d_db),
                   jnp.stack(d_rn).reshape(depth, rheads, N_DEV * rns), d_final[0], loss_row[0, :1]]
    _, gsum = _exchange_small(_pack(grads_local), name="all_reduce_small", after=(in_outs[0], out_outs[0]))
    g_norm, g_qn, g_kn, g_df, g_db, g_rn_full, g_final, loss = _unpack(gsum, small_shapes)
    g_rn = lax.dynamic_slice_in_dim(g_rn_full, me * rns, rns, axis=2)
    small_g = [g_norm, g_qn, g_kn, g_df, g_db, g_rn, g_final]
    small_w = [norm_w, q_norm, k_norm, ret_decay_fwd, ret_decay_bwd, ret_norm, final_norm]
    small_m = [m_norm_w, m_q_norm, m_k_norm, m_ret_decay_fwd, m_ret_decay_bwd, m_ret_norm, m_final_norm]
    small_v = [v_norm_w, v_q_norm, v_k_norm, v_ret_decay_fwd, v_ret_decay_bwd, v_ret_norm, v_final_norm]
    shapes = [a.shape for a in small_w]
    sd, sm, sv2 = _adamw_small(_pack(small_w), _pack(small_g), _pack(small_m), _pack(small_v))
    small_d, small_m2, small_v2 = _unpack(sd, shapes), _unpack(sm, shapes), _unpack(sv2, shapes)

    def ordered(small, win_v, wout_v):
        return [small[0], win_v, small[1], small[2], small[3], small[4], small[5], wout_v, small[6]]

    grads = ordered(small_g, in_outs[0], out_outs[0])
    deltas = ordered(small_d, in_outs[1], out_outs[1])
    new_m = ordered(small_m2, in_outs[2], out_outs[2])
    new_v = ordered(small_v2, in_outs[3], out_outs[3])
    return (loss.reshape(()), g[None], *grads, *deltas, *new_m, *new_v)
```

```python
import functools
import math

import jax
import jax.numpy as jnp
import numpy as np
from jax import lax
from jax.experimental import pallas as pl
from jax.experimental.pallas import tpu as pltpu

F32 = jnp.float32
BF16 = jnp.bfloat16

N_DEV = 8
HEAD_DIM = 128
ATTN_GROUP = 4
RET_QK_DIM = 128
RET_V_DIM = 256
GRID_W = 64
ROPE_THETA = 10000.0
EPS = 1e-6
ADAM_LR = 0.001
ADAM_B1 = 0.9
ADAM_B2 = 0.999
ADAM_EPS = 1e-08
ADAM_WD = 0.01
ADAM_STEP = 10
ADAM_C1 = 1.0 - ADAM_B1 ** ADAM_STEP
ADAM_C2 = 1.0 - ADAM_B2 ** ADAM_STEP
LANES = 128
SMALL_ROWS_ALIGN = 8
VMEM_LIMIT = 56 * 1024 * 1024

NT_DIMS = (((1,), (1,)), ((), ()))
TN_DIMS = (((0,), (0,)), ((), ()))
MESH = pl.DeviceIdType.MESH


def _params(sem):
    return pltpu.CompilerParams(dimension_semantics=sem, vmem_limit_bytes=VMEM_LIMIT)


def _tile(dim, pref, align=16):
    if dim <= pref:
        return dim
    for t in range(pref - pref % align, 0, -align):
        if dim % t == 0:
            return t
    raise ValueError((dim, pref, align))


def _silu_parts(z):
    sg = 1.0 / (1.0 + jnp.exp(-z))
    return z * sg, sg * (1.0 + z * (1.0 - sg))


def _log_sigmoid(x):
    return jnp.minimum(x, 0.0) - jnp.log(1.0 + jnp.exp(-jnp.abs(x)))


def _swap_pairs(z):
    lane = lax.broadcasted_iota(jnp.int32, z.shape, 1)
    return jnp.where((lane % 64) < 32, pltpu.roll(z, 96, 1), pltpu.roll(z, 32, 1))


def _rope(z, cos, sin):
    return z * cos + _swap_pairs(z) * sin


def _rope_transposed(d, cos, sin):
    return d * cos + _swap_pairs(d * sin)


def _rope_tables(seq):
    rows = seq // GRID_W
    row = jnp.repeat(jnp.arange(rows), GRID_W).astype(F32)
    col = jnp.tile(jnp.arange(GRID_W), rows).astype(F32)
    axis_dim = HEAD_DIM // 2
    inv = ROPE_THETA ** (-jnp.arange(0, axis_dim, 2, dtype=F32) / axis_dim)
    ar = row[:, None] * inv[None, :]
    ac = col[:, None] * inv[None, :]
    cos = jnp.concatenate([jnp.cos(ar), jnp.cos(ar), jnp.cos(ac), jnp.cos(ac)], axis=-1)
    sin = jnp.concatenate([-jnp.sin(ar), jnp.sin(ar), -jnp.sin(ac), jnp.sin(ac)], axis=-1)
    return cos, sin


def _matmul(a, b, *, name, trans_b=False, out_dtype=F32, residual=None, tm=1024, tn=512, tk=4096):
    m, k = a.shape
    n = b.shape[0] if trans_b else b.shape[1]
    tm, tn, tk = _tile(m, tm), _tile(n, tn, LANES), _tile(k, tk, LANES)
    nk = k // tk
    has_res = residual is not None

    def body(*refs):
        if has_res:
            a_ref, b_ref, r_ref, o_ref = refs[:4]
        else:
            a_ref, b_ref, o_ref = refs[:3]
        if trans_b:
            part = lax.dot_general(a_ref[...], b_ref[...], NT_DIMS, preferred_element_type=F32)
        else:
            part = jnp.dot(a_ref[...], b_ref[...], preferred_element_type=F32)

        def finish(r):
            if has_res:
                r = r + r_ref[...]
            o_ref[...] = r.astype(o_ref.dtype)

        if nk == 1:
            finish(part)
        else:
            acc_ref = refs[-1]
            kk = pl.program_id(2)

            @pl.when(kk == 0)
            def _():
                acc_ref[...] = part

            @pl.when(kk > 0)
            def _():
                acc_ref[...] += part

            @pl.when(kk == nk - 1)
            def _():
                finish(acc_ref[...])

    if trans_b:
        b_spec = pl.BlockSpec((tn, tk), lambda i, j, kk: (j, kk))
    else:
        b_spec = pl.BlockSpec((tk, tn), lambda i, j, kk: (kk, j))
    in_specs = [pl.BlockSpec((tm, tk), lambda i, j, kk: (i, kk)), b_spec]
    args = [a, b]
    if has_res:
        in_specs.append(pl.BlockSpec((tm, tn), lambda i, j, kk: (i, j)))
        args.append(residual)
    return pl.pallas_call(
        body, name=name, grid=(m // tm, n // tn, nk),
        in_specs=in_specs,
        out_specs=pl.BlockSpec((tm, tn), lambda i, j, kk: (i, j)),
        out_shape=jax.ShapeDtypeStruct((m, n), out_dtype),
        scratch_shapes=[pltpu.VMEM((tm, tn), F32)] if nk > 1 else [],
        compiler_params=_params(("parallel", "parallel", "arbitrary")),
    )(*args)


def _proj_sections(a, b, widths, dtypes, *, tm=1024, tn=512):
    m, k = a.shape
    tm = _tile(m, tm)
    tn = _tile(min(widths), tn, LANES)
    assert all(w % tn == 0 for w in widths) and sum(widths) == b.shape[1]
    nblk = [w // tn for w in widths]
    first = [int(o) // tn for o in np.cumsum((0,) + tuple(widths))[:-1]]

    def body(a_ref, b_ref, *out_refs):
        j = pl.program_id(1)
        part = jnp.dot(a_ref[...], b_ref[...], preferred_element_type=F32)
        for o_ref, lo, n in zip(out_refs, first, nblk):
            @pl.when(jnp.logical_and(j >= lo, j < lo + n))
            def _(o_ref=o_ref):
                o_ref[...] = part.astype(o_ref.dtype)

    out_specs = [pl.BlockSpec((tm, tn), lambda i, j, lo=lo, n=n: (i, jnp.clip(j - lo, 0, n - 1)))
                 for lo, n in zip(first, nblk)]
    return pl.pallas_call(
        body, name="proj", grid=(m // tm, b.shape[1] // tn),
        in_specs=[pl.BlockSpec((tm, k), lambda i, j: (i, 0)), pl.BlockSpec((k, tn), lambda i, j: (0, j))],
        out_specs=out_specs,
        out_shape=[jax.ShapeDtypeStruct((m, w), dt) for w, dt in zip(widths, dtypes)],
        compiler_params=_params(("arbitrary", "arbitrary")),
    )(a, b)


def _rms_fwd(x, w, *, ts=256):
    s, d = x.shape
    ts = _tile(s, ts)

    def body(x_ref, w_ref, h_ref, ht_ref):
        xv = x_ref[...]
        r = lax.rsqrt(jnp.mean(xv * xv, axis=-1, keepdims=True) + EPS)
        h = xv * r * w_ref[...]
        h_ref[...] = h.astype(BF16)
        ht_ref[...] = h.T.astype(BF16)

    row = pl.BlockSpec((ts, d), lambda i: (i, 0))
    return pl.pallas_call(
        body, name="rms_fwd", grid=(s // ts,),
        in_specs=[row, pl.BlockSpec((1, d), lambda i: (0, 0))],
        out_specs=[row, pl.BlockSpec((d, ts), lambda i: (0, i))],
        out_shape=[jax.ShapeDtypeStruct((s, d), BF16), jax.ShapeDtypeStruct((d, s), BF16)],
        compiler_params=_params(("parallel",)),
    )(x, w.reshape(1, d))


def _rms_bwd(dh, x, g, w, *, ts=256):
    s, d = x.shape
    ts = _tile(s, ts)

    def body(dh_ref, x_ref, g_ref, w_ref, dx_ref, dxb_ref, dw_ref):
        xv = x_ref[...]
        r = lax.rsqrt(jnp.mean(xv * xv, axis=-1, keepdims=True) + EPS)
        xh = xv * r
        dhv = dh_ref[...]
        dn = dhv * w_ref[...]
        dx = g_ref[...] + r * (dn - xh * jnp.mean(dn * xh, axis=-1, keepdims=True))
        dx_ref[...] = dx
        dxb_ref[...] = dx.astype(BF16)
        part = jnp.sum(dhv * xh, axis=0, keepdims=True)

        @pl.when(pl.program_id(0) == 0)
        def _():
            dw_ref[...] = part

        @pl.when(pl.program_id(0) > 0)
        def _():
            dw_ref[...] += part

    row = pl.BlockSpec((ts, d), lambda i: (i, 0))
    vec = pl.BlockSpec((1, d), lambda i: (0, 0))
    return pl.pallas_call(
        body, name="rms_bwd", grid=(s // ts,),
        in_specs=[row, row, row, vec],
        out_specs=[row, row, vec],
        out_shape=[jax.ShapeDtypeStruct((s, d), F32), jax.ShapeDtypeStruct((s, d), BF16),
                   jax.ShapeDtypeStruct((1, d), F32)],
        compiler_params=_params(("arbitrary",)),
    )(dh, x, g, w.reshape(1, d))


def _loss_head(x, target, w, *, ts=256):
    s, d = x.shape
    ts = _tile(s, ts)

    def body(x_ref, t_ref, w_ref, loss_ref, dx_ref, dxb_ref, dw_ref):
        xv = x_ref[...]
        r = lax.rsqrt(jnp.mean(xv * xv, axis=-1, keepdims=True) + EPS)
        xh = xv * r
        wv = w_ref[...]
        diff = xh * wv - t_ref[...]
        lpart = 0.5 * jnp.sum(jnp.mean(diff * diff, axis=-1, keepdims=True), axis=0, keepdims=True)
        dout = diff * (1.0 / d)
        dn = dout * wv
        dx = r * (dn - xh * jnp.mean(dn * xh, axis=-1, keepdims=True))
        dx_ref[...] = dx
        dxb_ref[...] = dx.astype(BF16)
        part = jnp.sum(dout * xh, axis=0, keepdims=True)
        lrow = jnp.broadcast_to(lpart, loss_ref.shape)

        @pl.when(pl.program_id(0) == 0)
        def _():
            dw_ref[...] = part
            loss_ref[...] = lrow

        @pl.when(pl.program_id(0) > 0)
        def _():
            dw_ref[...] += part
            loss_ref[...] += lrow

    row = pl.BlockSpec((ts, d), lambda i: (i, 0))
    vec = pl.BlockSpec((1, d), lambda i: (0, 0))
    return pl.pallas_call(
        body, name="loss_head", grid=(s // ts,),
        in_specs=[row, row, vec],
        out_specs=[pl.BlockSpec((1, LANES), lambda i: (0, 0)), row, row, vec],
        out_shape=[jax.ShapeDtypeStruct((1, LANES), F32), jax.ShapeDtypeStruct((s, d), F32),
                   jax.ShapeDtypeStruct((s, d), BF16), jax.ShapeDtypeStruct((1, d), F32)],
        compiler_params=_params(("arbitrary",)),
    )(x, target, w.reshape(1, d))


def _prep_fwd(aq, ak, rq, rk, cos, sin, qw, kw, *, ts=256):
    s = aq.shape[0]
    ts = _tile(s, ts)
    attn_scale = HEAD_DIM ** -0.5
    ret_scale = RET_QK_DIM ** -0.5
    nq, nk, nr = aq.shape[1] // HEAD_DIM, ak.shape[1] // HEAD_DIM, rq.shape[1] // RET_QK_DIM

    def body(aq_ref, ak_ref, rq_ref, rk_ref, cos_ref, sin_ref, qw_ref, kw_ref,
             q_out, k_out, rq_out, rk_out):
        c, sn = cos_ref[...], sin_ref[...]

        def normed(u, w):
            return u * lax.rsqrt(jnp.mean(u * u, axis=-1, keepdims=True) + EPS) * w

        for j in range(nq):
            sl = slice(j * HEAD_DIM, (j + 1) * HEAD_DIM)
            q_out[:, sl] = (_rope(normed(aq_ref[:, sl], qw_ref[...]), c, sn) * attn_scale).astype(BF16)
        for j in range(nk):
            sl = slice(j * HEAD_DIM, (j + 1) * HEAD_DIM)
            k_out[:, sl] = _rope(normed(ak_ref[:, sl], kw_ref[...]), c, sn).astype(BF16)
        for j in range(nr):
            sl = slice(j * RET_QK_DIM, (j + 1) * RET_QK_DIM)
            rq_out[:, sl] = _rope(rq_ref[:, sl], c, sn).astype(BF16)
            rk_out[:, sl] = (_rope(rk_ref[:, sl], c, sn) * ret_scale).astype(BF16)

    def row(arr):
        return pl.BlockSpec((ts, arr.shape[1]), lambda i: (i, 0))

    vec = pl.BlockSpec((1, HEAD_DIM), lambda i: (0, 0))
    ins = [aq, ak, rq, rk]
    return pl.pallas_call(
        body, name="prep_fwd", grid=(s // ts,),
        in_specs=[row(a) for a in ins] + [row(cos), row(sin), vec, vec],
        out_specs=[row(a) for a in ins],
        out_shape=[jax.ShapeDtypeStruct(a.shape, BF16) for a in ins],
        compiler_params=_params(("parallel",)),
    )(*ins, cos, sin, qw.reshape(1, HEAD_DIM), kw.reshape(1, HEAD_DIM))


def _prep_bwd(dq, dk, drq, drk, aq, ak, cos, sin, qw, kw, dav, dag, drv, drg, *, ts=256):
    s = aq.shape[0]
    ts = _tile(s, ts)
    attn_scale = HEAD_DIM ** -0.5
    ret_scale = RET_QK_DIM ** -0.5
    nq, nk, nr = aq.shape[1] // HEAD_DIM, ak.shape[1] // HEAD_DIM, drq.shape[1] // RET_QK_DIM
    widths = (aq.shape[1], ak.shape[1], dav.shape[1], dag.shape[1], drq.shape[1], drk.shape[1], drv.shape[1],
              drg.shape[1])
    o_aq, o_ak, o_av, o_ag, o_rq, o_rk, o_rv, o_rg = (int(o) for o in np.cumsum((0,) + widths)[:-1])

    def body(dq_ref, dk_ref, drq_ref, drk_ref, aq_ref, ak_ref, cos_ref, sin_ref, dav_ref, dag_ref, drv_ref, drg_ref,
             qw_ref, kw_ref, dproj_ref, dqw_ref, dkw_ref):
        c, sn = cos_ref[...], sin_ref[...]
        for ref, off in ((dav_ref, o_av), (dag_ref, o_ag), (drv_ref, o_rv), (drg_ref, o_rg)):
            dproj_ref[:, off:off + ref.shape[1]] = ref[...]

        def unrope(d):
            return _rope_transposed(d, c, sn)

        def norm_bwd(dun, u, w):
            r = lax.rsqrt(jnp.mean(u * u, axis=-1, keepdims=True) + EPS)
            uh = u * r
            dn = dun * w
            du = r * (dn - uh * jnp.mean(dn * uh, axis=-1, keepdims=True))
            return du, jnp.sum(dun * uh, axis=0, keepdims=True)

        dqw = jnp.zeros((1, HEAD_DIM), F32)
        for j in range(nq):
            sl = slice(j * HEAD_DIM, (j + 1) * HEAD_DIM)
            du, dw = norm_bwd(unrope(dq_ref[:, sl] * attn_scale), aq_ref[:, sl], qw_ref[...])
            dproj_ref[:, o_aq + j * HEAD_DIM:o_aq + (j + 1) * HEAD_DIM] = du.astype(BF16)
            dqw = dqw + dw
        dkw = jnp.zeros((1, HEAD_DIM), F32)
        for j in range(nk):
            sl = slice(j * HEAD_DIM, (j + 1) * HEAD_DIM)
            du, dw = norm_bwd(unrope(dk_ref[:, sl]), ak_ref[:, sl], kw_ref[...])
            dproj_ref[:, o_ak + j * HEAD_DIM:o_ak + (j + 1) * HEAD_DIM] = du.astype(BF16)
            dkw = dkw + dw
        for j in range(nr):
            sl = slice(j * RET_QK_DIM, (j + 1) * RET_QK_DIM)
            dproj_ref[:, o_rq + j * RET_QK_DIM:o_rq + (j + 1) * RET_QK_DIM] = unrope(drq_ref[:, sl]).astype(BF16)
            dproj_ref[:, o_rk + j * RET_QK_DIM:o_rk + (j + 1) * RET_QK_DIM] = (
                unrope(drk_ref[:, sl] * ret_scale).astype(BF16))

        @pl.when(pl.program_id(0) == 0)
        def _():
            dqw_ref[...] = dqw
            dkw_ref[...] = dkw

        @pl.when(pl.program_id(0) > 0)
        def _():
            dqw_ref[...] += dqw
            dkw_ref[...] += dkw

    def row(arr):
        return pl.BlockSpec((ts, arr.shape[1]), lambda i: (i, 0))

    vec = pl.BlockSpec((1, HEAD_DIM), lambda i: (0, 0))
    ins = [dq, dk, drq, drk, aq, ak, cos, sin, dav, dag, drv, drg]
    total = sum(widths)
    return pl.pallas_call(
        body, name="prep_bwd", grid=(s // ts,),
        in_specs=[row(a) for a in ins] + [vec, vec],
        out_specs=[pl.BlockSpec((ts, total), lambda i: (i, 0)), vec, vec],
        out_shape=[jax.ShapeDtypeStruct((s, total), BF16)] + [jax.ShapeDtypeStruct((1, HEAD_DIM), F32)] * 2,
        compiler_params=_params(("arbitrary",)),
    )(*ins, qw.reshape(1, HEAD_DIM), kw.reshape(1, HEAD_DIM))


def _attn_fwd(q, k, v, *, tq=4096, sub=256):
    s, aw = q.shape
    tq = _tile(s, tq)
    sub = _tile(tq, sub)
    heads, kvh = aw // HEAD_DIM, k.shape[1] // HEAD_DIM
    grp = heads // kvh

    def body(q_ref, k_ref, v_ref, o_ref, lse_ref):
        kv_ = k_ref[...]
        v_ext = jnp.concatenate([v_ref[...], jnp.ones((s, HEAD_DIM), BF16)], axis=-1)
        for r in range(tq // sub):
            rows = slice(r * sub, (r + 1) * sub)
            sc = lax.dot_general(q_ref[rows, :], kv_, NT_DIMS, preferred_element_type=F32)
            m = jnp.max(sc, axis=-1, keepdims=True)
            p = jnp.exp((sc - m).astype(BF16))
            oe = jnp.dot(p, v_ext, preferred_element_type=F32)
            l = oe[:, HEAD_DIM:HEAD_DIM + 1]
            o_ref[rows, :] = (oe[:, :HEAD_DIM] / l).astype(o_ref.dtype)
            lse_ref[rows, :] = jnp.broadcast_to(m + jnp.log(l), (sub, HEAD_DIM))

    qspec = pl.BlockSpec((tq, HEAD_DIM), lambda kv, g, i: (i, kv * grp + g))
    kspec = pl.BlockSpec((s, HEAD_DIM), lambda kv, g, i: (0, kv))
    return pl.pallas_call(
        body, name="attn_fwd", grid=(kvh, grp, s // tq),
        in_specs=[qspec, kspec, kspec],
        out_specs=[qspec, qspec],
        out_shape=[jax.ShapeDtypeStruct((s, aw), BF16), jax.ShapeDtypeStruct((s, aw), F32)],
        compiler_params=_params(("parallel", "parallel", "parallel")),
    )(q, k, v)


def _attn_bwd(q, k, v, o, do, lse, *, tq=1024, sub=256):
    s, aw = q.shape
    tq = _tile(s, tq)
    sub = _tile(tq, sub)
    heads, kvh = aw // HEAD_DIM, k.shape[1] // HEAD_DIM
    grp = heads // kvh
    nq = s // tq

    def body(q_ref, k_ref, v_ref, o_ref, do_ref, lse_ref, dq_ref, dk_ref, dv_ref, dk_acc, dv_acc, p_scr, ds_scr):
        g, i = pl.program_id(1), pl.program_id(2)
        kv_, vv = k_ref[...], v_ref[...]
        for r in range(tq // sub):
            rows = slice(r * sub, (r + 1) * sub)
            qv, dov = q_ref[rows, :], do_ref[rows, :]
            sc = lax.dot_general(qv, kv_, NT_DIMS, preferred_element_type=F32)
            p = jnp.exp((sc - lse_ref[rows, :1]).astype(BF16))
            dp = lax.dot_general(dov, vv, NT_DIMS, preferred_element_type=F32)
            delta = jnp.sum(dov.astype(F32) * o_ref[rows, :].astype(F32), axis=-1, keepdims=True)
            ds = p * (dp - delta).astype(BF16)
            dq_ref[rows, :] = jnp.dot(ds, kv_, preferred_element_type=F32)
            p_scr[rows, :] = p
            ds_scr[rows, :] = ds
        dvp = lax.dot_general(p_scr[...], do_ref[...], TN_DIMS, preferred_element_type=F32)
        dkp = lax.dot_general(ds_scr[...], q_ref[...], TN_DIMS, preferred_element_type=F32)
        first = jnp.logical_and(g == 0, i == 0)

        @pl.when(first)
        def _():
            dv_acc[...] = dvp
            dk_acc[...] = dkp

        @pl.when(jnp.logical_not(first))
        def _():
            dv_acc[...] += dvp
            dk_acc[...] += dkp

        @pl.when(jnp.logical_and(g == grp - 1, i == nq - 1))
        def _():
            dk_ref[...] = dk_acc[...]
            dv_ref[...] = dv_acc[...].astype(dv_ref.dtype)

    qspec = pl.BlockSpec((tq, HEAD_DIM), lambda kv, g, i: (i, kv * grp + g))
    kspec = pl.BlockSpec((s, HEAD_DIM), lambda kv, g, i: (0, kv))
    return pl.pallas_call(
        body, name="attn_bwd", grid=(kvh, grp, nq),
        in_specs=[qspec, kspec, kspec, qspec, qspec, qspec],
        out_specs=[qspec, kspec, kspec],
        out_shape=[jax.ShapeDtypeStruct((s, aw), F32), jax.ShapeDtypeStruct(k.shape, F32),
                   jax.ShapeDtypeStruct(v.shape, BF16)],
        scratch_shapes=[pltpu.VMEM((s, HEAD_DIM), F32), pltpu.VMEM((s, HEAD_DIM), F32),
                        pltpu.VMEM((tq, s), BF16), pltpu.VMEM((tq, s), BF16)],
        compiler_params=_params(("parallel", "arbitrary", "arbitrary")),
    )(q, k, v, o, do, lse)


def _sum_all(z):
    return jnp.sum(jnp.sum(z, axis=0, keepdims=True), axis=1, keepdims=True)


def _chunk_consts(df_ref, db_ref, t):
    lf = _log_sigmoid(df_ref[0][:, :1])
    lb = _log_sigmoid(db_ref[0][:, :1])
    r = lax.broadcasted_iota(jnp.int32, (t, 1), 0).astype(F32)
    c = lax.broadcasted_iota(jnp.int32, (1, t), 1).astype(F32)
    diff = r - c
    dm = jnp.exp(diff * jnp.where(diff >= 0, lf, -lb))
    return dict(diff=diff, dm=dm, r=r,
                af=jnp.exp(lf * (r + 1.0)), bf=jnp.exp(lf * (t - 1.0 - r)), gf=jnp.exp(lf * t),
                ab=jnp.exp(lb * (t - r)), bb=jnp.exp(lb * r), gb=jnp.exp(lb * t))


def _scaled(x, f):
    return (x.astype(F32) * f).astype(BF16)


def _retc_specs(s):
    qspec = pl.BlockSpec((s, RET_QK_DIM), lambda h: (0, h))
    vspec = pl.BlockSpec((s, RET_V_DIM), lambda h: (0, h))
    dspec = pl.BlockSpec((1, 1, LANES), lambda h: (h, 0, 0))
    return qspec, vspec, dspec


def _retc_fwd(q, k, v, dec_f, dec_b, *, t=256):
    s, qw = q.shape
    t = _tile(s, t)
    heads, nc = qw // RET_QK_DIM, s // t
    qspec, vspec, dspec = _retc_specs(s)

    def body(q_ref, k_ref, v_ref, df_ref, db_ref, o_ref):
        cs = _chunk_consts(df_ref, db_ref, t)

        def rows_of(i):
            return pl.ds(pl.multiple_of(i * t, t), t)

        def forward(i, sf):
            rows = rows_of(i)
            qi, ki, vi = q_ref[rows, :], k_ref[rows, :], v_ref[rows, :]
            sc = lax.dot_general(qi, ki, NT_DIMS, preferred_element_type=F32)
            intra = jnp.dot((sc * cs["dm"]).astype(BF16), vi, preferred_element_type=F32)
            cross = jnp.dot(_scaled(qi, cs["af"]), sf.astype(BF16), preferred_element_type=F32)
            o_ref[rows, :] = intra + cross
            return cs["gf"] * sf + lax.dot_general(_scaled(ki, cs["bf"]), vi, TN_DIMS, preferred_element_type=F32)

        def backward(j, sb):
            rows = rows_of(nc - 1 - j)
            qi, ki, vi = q_ref[rows, :], k_ref[rows, :], v_ref[rows, :]
            o_ref[rows, :] += jnp.dot(_scaled(qi, cs["ab"]), sb.astype(BF16), preferred_element_type=F32)
            return cs["gb"] * sb + lax.dot_general(_scaled(ki, cs["bb"]), vi, TN_DIMS, preferred_element_type=F32)

        zero = jnp.zeros((RET_QK_DIM, RET_V_DIM), F32)
        lax.fori_loop(0, nc, forward, zero, unroll=True)
        lax.fori_loop(0, nc, backward, zero, unroll=True)

    return pl.pallas_call(
        body, name="ret_fwd", grid=(heads,),
        in_specs=[qspec, qspec, vspec, dspec, dspec],
        out_specs=vspec, out_shape=jax.ShapeDtypeStruct(v.shape, F32),
        compiler_params=_params(("parallel",)),
    )(q, k, v, dec_f, dec_b)


def _retc_bwd(q, k, v, do, dec_f, dec_b, *, t=256):
    s, qw = q.shape
    t = _tile(s, t)
    heads, nc = qw // RET_QK_DIM, s // t
    qspec, vspec, dspec = _retc_specs(s)
    gspec = pl.BlockSpec((1, 8, LANES), lambda h: (h, 0, 0))

    def body(q_ref, k_ref, v_ref, do_ref, df_ref, db_ref, dq_ref, dk_ref, dv_ref, gf_ref, gb_ref,
             sf_scr, sb_scr, dv_acc):
        cs = _chunk_consts(df_ref, db_ref, t)
        r, diff, dm = cs["r"], cs["diff"], cs["dm"]

        def rows_of(i):
            return pl.ds(pl.multiple_of(i * t, t), t)

        def tn(a, b):
            return lax.dot_general(a, b, TN_DIMS, preferred_element_type=F32)

        def nt(a, b):
            return lax.dot_general(a, b, NT_DIMS, preferred_element_type=F32)

        def states_f(i, sf):
            sf_scr[i] = sf
            rows = rows_of(i)
            return cs["gf"] * sf + tn(_scaled(k_ref[rows, :], cs["bf"]), v_ref[rows, :])

        def states_b(j, sb):
            i = nc - 1 - j
            sb_scr[i] = sb
            rows = rows_of(i)
            return cs["gb"] * sb + tn(_scaled(k_ref[rows, :], cs["bb"]), v_ref[rows, :])

        zero = jnp.zeros((RET_QK_DIM, RET_V_DIM), F32)
        lax.fori_loop(0, nc, states_f, zero, unroll=True)
        lax.fori_loop(0, nc, states_b, zero, unroll=True)

        def scan_grads(i, state, u, qf, kf, vi, doi, fa, fb, step, wa, wb):
            qa, kb = qf * fa, kf * fb
            ub = u.astype(BF16)
            dqa = nt(doi, state.astype(BF16))
            dkb = nt(vi, ub)
            dv = jnp.dot(kb.astype(BF16), ub, preferred_element_type=F32)
            dlog = _sum_all(dqa * qa * wa) + _sum_all(dkb * kb * wb) + t * step * _sum_all(u * state)
            u_new = step * u + tn(qa.astype(BF16), doi)
            return dqa * fa, dkb * fb, dv, u_new, dlog

        def sweep_f(j, carry):
            u, accf, accb = carry
            i = nc - 1 - j
            rows = rows_of(i)
            qi, ki, vi, doi = q_ref[rows, :], k_ref[rows, :], v_ref[rows, :], do_ref[rows, :]
            sc = nt(qi, ki)
            p = sc * dm
            dp = nt(doi, vi)
            ds = (dp * dm).astype(BF16)
            tt = dp * p * diff
            accf = accf + _sum_all(jnp.where(diff > 0, tt, 0.0))
            accb = accb + _sum_all(jnp.where(diff < 0, -tt, 0.0))
            dq1, dk1, dv1, u, dlog = scan_grads(i, sf_scr[i], u, qi.astype(F32), ki.astype(F32), vi, doi,
                                                cs["af"], cs["bf"], cs["gf"], r + 1.0, t - 1.0 - r)
            dq_ref[rows, :] = jnp.dot(ds, ki, preferred_element_type=F32) + dq1
            dk_ref[rows, :] = tn(ds, qi) + dk1
            dv_acc[rows, :] = tn(p.astype(BF16), doi) + dv1
            return u, accf + dlog, accb

        def sweep_b(i, carry):
            w, accb = carry
            rows = rows_of(i)
            qi, ki, vi, doi = q_ref[rows, :], k_ref[rows, :], v_ref[rows, :], do_ref[rows, :]
            dq1, dk1, dv1, w, dlog = scan_grads(i, sb_scr[i], w, qi.astype(F32), ki.astype(F32), vi, doi,
                                                cs["ab"], cs["bb"], cs["gb"], t - r, r)
            dq_ref[rows, :] += dq1
            dk_ref[rows, :] += dk1
            dv_acc[rows, :] += dv1
            return w, accb + dlog

        z11 = jnp.zeros((1, 1), F32)
        _, accf, accb = lax.fori_loop(0, nc, sweep_f, (zero, z11, z11), unroll=2)
        _, accb = lax.fori_loop(0, nc, sweep_b, (zero, accb), unroll=2)
        dv_ref[...] = dv_acc[...].astype(dv_ref.dtype)
        gf_ref[...] = jnp.broadcast_to((accf / (1.0 + jnp.exp(df_ref[0][:, :1]))).reshape(1, 1, 1), gf_ref.shape)
        gb_ref[...] = jnp.broadcast_to((accb / (1.0 + jnp.exp(db_ref[0][:, :1]))).reshape(1, 1, 1), gb_ref.shape)

    return pl.pallas_call(
        body, name="ret_bwd", grid=(heads,),
        in_specs=[qspec, qspec, vspec, vspec, dspec, dspec],
        out_specs=[qspec, qspec, vspec, gspec, gspec],
        out_shape=[jax.ShapeDtypeStruct(q.shape, F32), jax.ShapeDtypeStruct(k.shape, F32),
                   jax.ShapeDtypeStruct(v.shape, BF16),
                   jax.ShapeDtypeStruct((heads, 8, LANES), F32), jax.ShapeDtypeStruct((heads, 8, LANES), F32)],
        scratch_shapes=[pltpu.VMEM((nc, RET_QK_DIM, RET_V_DIM), F32), pltpu.VMEM((nc, RET_QK_DIM, RET_V_DIM), F32),
                        pltpu.VMEM((s, RET_V_DIM), F32)],
        compiler_params=_params(("parallel",)),
    )(q, k, v, do, dec_f, dec_b)


def _gate_fwd(att, ag, ret, rg, rnw, *, ts=256):
    s, aw = att.shape
    rw = ret.shape[1]
    ts = _tile(s, ts)
    rheads = rw // RET_V_DIM

    def body(att_ref, ag_ref, ret_ref, rg_ref, w_ref, y_ref, yt_ref):
        def put(lo, hi, val):
            y_ref[:, lo:hi] = val.astype(BF16)
            yt_ref[lo:hi, :] = val.T.astype(BF16)

        sa, _ = _silu_parts(ag_ref[...])
        put(0, aw, sa * att_ref[...].astype(F32))
        for h in range(rheads):
            sl = slice(h * RET_V_DIM, (h + 1) * RET_V_DIM)
            rt = ret_ref[:, sl]
            rn = rt * lax.rsqrt(jnp.mean(rt * rt, axis=-1, keepdims=True) + EPS) * w_ref[:, sl]
            sr, _ = _silu_parts(rg_ref[:, sl])
            put(aw + h * RET_V_DIM, aw + (h + 1) * RET_V_DIM, sr * rn)

    def row(w):
        return pl.BlockSpec((ts, w), lambda i: (i, 0))

    return pl.pallas_call(
        body, name="gate_fwd", grid=(s // ts,),
        in_specs=[row(aw), row(aw), row(rw), row(rw), pl.BlockSpec((1, rw), lambda i: (0, 0))],
        out_specs=[row(aw + rw), pl.BlockSpec((aw + rw, ts), lambda i: (0, i))],
        out_shape=[jax.ShapeDtypeStruct((s, aw + rw), BF16), jax.ShapeDtypeStruct((aw + rw, s), BF16)],
        compiler_params=_params(("parallel",)),
    )(att, ag, ret, rg, rnw.reshape(1, rw))


def _gate_bwd(dy, att, ag, ret, rg, rnw, *, ts=256):
    s, aw = att.shape
    rw = ret.shape[1]
    ts = _tile(s, ts)
    rheads = rw // RET_V_DIM

    def body(dy_ref, att_ref, ag_ref, ret_ref, rg_ref, w_ref, datt_ref, dag_ref, dret_ref, drg_ref, dw_ref):
        sa, dsa = _silu_parts(ag_ref[...])
        dya = dy_ref[:, :aw]
        datt_ref[...] = (dya * sa).astype(BF16)
        dag_ref[...] = (dya * att_ref[...].astype(F32) * dsa).astype(BF16)
        parts = []
        for h in range(rheads):
            sl = slice(h * RET_V_DIM, (h + 1) * RET_V_DIM)
            rt = ret_ref[:, sl]
            rr = lax.rsqrt(jnp.mean(rt * rt, axis=-1, keepdims=True) + EPS)
            rh = rt * rr
            wv = w_ref[:, sl]
            sr, dsr = _silu_parts(rg_ref[:, sl])
            dyr = dy_ref[:, aw + h * RET_V_DIM:aw + (h + 1) * RET_V_DIM]
            drg_ref[:, sl] = (dyr * rh * wv * dsr).astype(BF16)
            drn = dyr * sr
            dn = drn * wv
            dret_ref[:, sl] = (rr * (dn - rh * jnp.mean(dn * rh, axis=-1, keepdims=True))).astype(BF16)
            parts.append(jnp.sum(drn * rh, axis=0, keepdims=True))
        part = jnp.concatenate(parts, axis=-1)

        @pl.when(pl.program_id(0) == 0)
        def _():
            dw_ref[...] = part

        @pl.when(pl.program_id(0) > 0)
        def _():
            dw_ref[...] += part

    def row(w):
        return pl.BlockSpec((ts, w), lambda i: (i, 0))

    vec = pl.BlockSpec((1, rw), lambda i: (0, 0))
    return pl.pallas_call(
        body, name="gate_bwd", grid=(s // ts,),
        in_specs=[row(aw + rw), row(aw), row(aw), row(rw), row(rw), vec],
        out_specs=[row(aw), row(aw), row(rw), row(rw), vec],
        out_shape=[jax.ShapeDtypeStruct((s, aw), BF16), jax.ShapeDtypeStruct((s, aw), BF16),
                   jax.ShapeDtypeStruct((s, rw), BF16), jax.ShapeDtypeStruct((s, rw), BF16),
                   jax.ShapeDtypeStruct((1, rw), F32)],
        compiler_params=_params(("arbitrary",)),
    )(dy, att, ag, ret, rg, rnw.reshape(1, rw))


def _mesh_position():
    x, y, c = lax.axis_index("x"), lax.axis_index("y"), lax.axis_index("c")
    return x, y, c, 4 * x + 2 * y + c


def _peer(x, y, c, k):
    px = 1 - x if k & 4 else x
    py = 1 - y if k & 2 else y
    pc = 1 - c if k & 1 else c
    return (px, py, pc), 4 * px + 2 * py + pc


HBM_SPEC = pl.BlockSpec(memory_space=pltpu.HBM)
SEM_SPEC = pl.BlockSpec(memory_space=pltpu.SEMAPHORE)
ANY_SPEC = pl.BlockSpec(memory_space=pl.ANY)
DATAFLOW = pltpu.SideEffectType.DATAFLOW_SIDE_EFFECTING


def _hbm(a):
    return pltpu.with_memory_space_constraint(a, pltpu.HBM)


def _split_start(name, copies, n, src, land, after):
    def body(*refs):
        (send_sems, recv_sems), token = refs[2 + len(after):4 + len(after)], refs[-1]
        sends, _ = copies(refs[0], refs[1], send_sems, recv_sems)
        for cp in sends:
            cp.start()
        token[...] = jnp.zeros_like(token)

    return pl.pallas_call(
        body, name=name,
        out_shape=(pltpu.SemaphoreType.DMA((n,)), pltpu.SemaphoreType.DMA((n,)),
                   pltpu.HBM(src.shape, src.dtype), pltpu.HBM(land.shape, land.dtype),
                   jax.ShapeDtypeStruct((8, LANES), F32)),
        in_specs=[HBM_SPEC] * 2 + [ANY_SPEC] * len(after),
        out_specs=(SEM_SPEC, SEM_SPEC, HBM_SPEC, HBM_SPEC, pl.BlockSpec(memory_space=pltpu.VMEM)),
        input_output_aliases={0: 2, 1: 3},
        compiler_params=pltpu.CompilerParams(has_side_effects=DATAFLOW),
    )(_hbm(src), _hbm(land), *after)


def _split_wait(name, copies, started, after):
    send_sems, recv_sems, src, land = started[:4]

    def body(*refs):
        sends, recvs = copies(refs[0], refs[1], refs[2], refs[3])
        for cp in sends:
            cp.wait_send()
        for cp in recvs:
            cp.wait_recv()

    return pl.pallas_call(
        body, name=name,
        out_shape=(pltpu.HBM(src.shape, src.dtype), pltpu.HBM(land.shape, land.dtype)),
        in_specs=[HBM_SPEC] * 2 + [SEM_SPEC, SEM_SPEC] + [ANY_SPEC] * len(after),
        out_specs=(HBM_SPEC,) * 2,
        input_output_aliases={0: 0, 1: 1},
        compiler_params=pltpu.CompilerParams(has_side_effects=DATAFLOW),
    )(src, land, send_sems, recv_sems, *after)[1]


def _slab(ref, p, size, axis):
    if axis == 1:
        return ref.at[:, pl.ds(pl.multiple_of(p * size, LANES), size)]
    return ref.at[pl.ds(pl.multiple_of(p * size, 16), size), :]


ALL_PEERS = tuple(range(1, N_DEV))
SIBLING = 1
SAME_CORE_OF_CHIPS = (2, 4, 6)


def _gather_copies(size, axis, ks):
    def copies(shard_ref, full_ref, send_sems, recv_sems):
        x, y, c, me = _mesh_position()
        sends, recvs = [], []
        for j, k in enumerate(ks):
            peer, pid = _peer(x, y, c, k)
            sends.append(pltpu.make_async_remote_copy(
                src_ref=shard_ref, dst_ref=_slab(full_ref, me, size, axis), send_sem=send_sems.at[j],
                recv_sem=recv_sems.at[j], device_id=peer, device_id_type=MESH))
            recvs.append(pltpu.make_async_remote_copy(
                src_ref=shard_ref, dst_ref=_slab(full_ref, pid, size, axis), send_sem=send_sems.at[j],
                recv_sem=recv_sems.at[j], device_id=peer, device_id_type=MESH))
        return sends, recvs

    return copies


def _pass_on_copies(size, axis):
    def copies(shard_ref, full_ref, send_sems, recv_sems):
        x, y, c, me = _mesh_position()
        sibling, _ = _peer(x, y, c, SIBLING)
        sends, recvs = [], []
        for j, k in enumerate(SAME_CORE_OF_CHIPS):
            _, landed = _peer(x, y, c, k)
            _, siblings = _peer(x, y, c, k ^ SIBLING)
            mine = _slab(full_ref, landed, size, axis)
            sends.append(pltpu.make_async_remote_copy(
                src_ref=mine, dst_ref=mine, send_sem=send_sems.at[j], recv_sem=recv_sems.at[j],
                device_id=sibling, device_id_type=MESH))
            recvs.append(pltpu.make_async_remote_copy(
                src_ref=mine, dst_ref=_slab(full_ref, siblings, size, axis), send_sem=send_sems.at[j],
                recv_sem=recv_sems.at[j], device_id=sibling, device_id_type=MESH))
        return sends, recvs

    return copies


def _scatter_copies(size, axis):
    def copies(grad_ref, land_ref, send_sems, recv_sems):
        x, y, c, me = _mesh_position()
        sends, recvs = [], []
        for k in range(1, N_DEV):
            peer, pid = _peer(x, y, c, k)
            src = _slab(grad_ref, pid, size, axis)
            sends.append(pltpu.make_async_remote_copy(
                src_ref=src, dst_ref=land_ref.at[me], send_sem=send_sems.at[k - 1], recv_sem=recv_sems.at[k - 1],
                device_id=peer, device_id_type=MESH))
            recvs.append(pltpu.make_async_remote_copy(
                src_ref=src, dst_ref=land_ref.at[pid], send_sem=send_sems.at[k - 1], recv_sem=recv_sems.at[k - 1],
                device_id=peer, device_id_type=MESH))
        return sends, recvs

    return copies


PLACE_BANDS = 8


def _place_own(name, src, out_shape, in_spec, out_spec, steps, me):
    def body(me_ref, src_ref, out_ref):
        out_ref[...] = src_ref[...]

    return pl.pallas_call(
        body, name=name, out_shape=out_shape,
        grid_spec=pltpu.PrefetchScalarGridSpec(num_scalar_prefetch=1, grid=(steps,), in_specs=[in_spec],
                                               out_specs=out_spec),
        compiler_params=_params(("parallel",)),
    )(me.reshape(1).astype(jnp.int32), src)


def _gather_start(shard, axis, ks, me, after, tag):
    rows, cols = shard.shape
    size = shard.shape[axis]
    full_shape = tuple(N_DEV * n if a == axis else n for a, n in enumerate(shard.shape))
    band = rows // PLACE_BANDS
    in_spec = pl.BlockSpec((band, cols), lambda i, me_ref: (i, 0))
    if axis == 1:
        out_spec = pl.BlockSpec((band, cols), lambda i, me_ref: (i, me_ref[0]))
    else:
        out_spec = pl.BlockSpec((band, cols), lambda i, me_ref: (me_ref[0] * PLACE_BANDS + i, 0))
    full = _place_own("place_shard", shard, jax.ShapeDtypeStruct(full_shape, shard.dtype), in_spec, out_spec,
                      PLACE_BANDS, me)
    return _split_start("gather_start_" + tag, _gather_copies(size, axis, ks), len(ks), shard, full, after)


def _gather_wait(started, axis, ks, after, tag):
    size = started[2].shape[axis]
    return _split_wait("gather_wait_" + tag, _gather_copies(size, axis, ks), started, after)


def _pass_on_start(shard, full, axis, after, tag):
    size = shard.shape[axis]
    return _split_start("pass_on_start_" + tag, _pass_on_copies(size, axis), len(SAME_CORE_OF_CHIPS), shard, full, after)


def _pass_on_wait(started, axis, after, tag):
    size = started[2].shape[axis]
    return _split_wait("pass_on_wait_" + tag, _pass_on_copies(size, axis), started, after)


def _scatter_start(grad, axis, me, tag):
    size = grad.shape[axis] // N_DEV
    rows, cols = tuple(size if a == axis else n for a, n in enumerate(grad.shape))
    band = rows // PLACE_BANDS
    if axis == 1:
        in_spec = pl.BlockSpec((band, cols), lambda i, me_ref: (i, me_ref[0]))
    else:
        in_spec = pl.BlockSpec((band, cols), lambda i, me_ref: (me_ref[0] * PLACE_BANDS + i, 0))
    out_spec = pl.BlockSpec((None, band, cols), lambda i, me_ref: (me_ref[0], i, 0))
    land = _place_own("place_slab", grad, jax.ShapeDtypeStruct((N_DEV, rows, cols), grad.dtype), in_spec, out_spec,
                      PLACE_BANDS, me)
    return _split_start("scatter_start_" + tag, _scatter_copies(size, axis), N_DEV - 1, grad, land, [])


def _scatter_wait(started, axis, after, tag):
    size = started[2].shape[axis] // N_DEV
    return _split_wait("scatter_wait_" + tag, _scatter_copies(size, axis), started, after)


def _exchange_small(buf, *, name, after=()):
    r = buf.shape[0]

    def body(*refs):
        buf_ref = refs[0]
        all_ref, sum_ref, send_sems, recv_sems = refs[1 + len(after):]
        x, y, c, me = _mesh_position()
        all_ref[me] = buf_ref[...]
        sends, recvs = [], []
        for k in range(1, N_DEV):
            peer, pid = _peer(x, y, c, k)
            sends.append(pltpu.make_async_remote_copy(
                src_ref=buf_ref, dst_ref=all_ref.at[me], send_sem=send_sems.at[k - 1], recv_sem=recv_sems.at[k - 1],
                device_id=peer, device_id_type=MESH))
            recvs.append(pltpu.make_async_remote_copy(
                src_ref=buf_ref, dst_ref=all_ref.at[pid], send_sem=send_sems.at[k - 1], recv_sem=recv_sems.at[k - 1],
                device_id=peer, device_id_type=MESH))
        for cp in sends:
            cp.start()
        for cp in recvs:
            cp.wait_recv()
        for cp in sends:
            cp.wait_send()
        total = all_ref[0]
        for p in range(1, N_DEV):
            total = total + all_ref[p]
        sum_ref[...] = total

    vmem = pl.BlockSpec(memory_space=pltpu.VMEM)
    return pl.pallas_call(
        body, name=name,
        in_specs=[vmem] + [ANY_SPEC] * len(after), out_specs=[vmem, vmem],
        out_shape=[jax.ShapeDtypeStruct((N_DEV, r, LANES), F32), jax.ShapeDtypeStruct((r, LANES), F32)],
        scratch_shapes=[pltpu.SemaphoreType.DMA((N_DEV - 1,)), pltpu.SemaphoreType.DMA((N_DEV - 1,))],
        compiler_params=pltpu.CompilerParams(has_side_effects=True),
    )(buf, *after)


def _adamw_math(w, g, m, v):
    m2 = ADAM_B1 * m + (1.0 - ADAM_B1) * g
    v2 = ADAM_B2 * v + (1.0 - ADAM_B2) * (g * g)
    delta = -ADAM_LR * ((m2 / ADAM_C1) / (jnp.sqrt(v2 / ADAM_C2) + ADAM_EPS) + ADAM_WD * w)
    return delta, m2, v2


def _adamw_slabs(layer, w, m, v, land, outs, order, *, tr, name):
    depth, r, c = w.shape
    tr = _tile(r, tr)

    def body(w_ref, m_ref, v_ref, land_ref, order_ref, o0, o1, o2, o3, g_ref, d_ref, m2_ref, v2_ref):
        g = land_ref[0].astype(F32)
        for p in range(1, N_DEV):
            g = g + land_ref[p].astype(F32)
        delta, m2, v2 = _adamw_math(w_ref[...], g, m_ref[...], v_ref[...])
        g_ref[...] = g
        d_ref[...] = delta
        m2_ref[...] = m2
        v2_ref[...] = v2

    row = pl.BlockSpec((None, tr, c), lambda i: (layer, i, 0))
    return pl.pallas_call(
        body, name=name, grid=(r // tr,),
        in_specs=[row, row, row, pl.BlockSpec((N_DEV, tr, c), lambda i: (0, i, 0)),
                  pl.BlockSpec((8, LANES), lambda i: (0, 0))] + [ANY_SPEC] * 4,
        out_specs=[row] * 4, out_shape=[jax.ShapeDtypeStruct((depth, r, c), F32)] * 4,
        input_output_aliases={5: 0, 6: 1, 7: 2, 8: 3},
        compiler_params=_params(("parallel",)),
    )(w, m, v, land, order, *outs)


def _adamw_small(w, g, m, v):
    def body(w_ref, g_ref, m_ref, v_ref, d_ref, m2_ref, v2_ref):
        delta, m2, v2 = _adamw_math(w_ref[...], g_ref[...], m_ref[...], v_ref[...])
        d_ref[...] = delta
        m2_ref[...] = m2
        v2_ref[...] = v2

    vmem = pl.BlockSpec(memory_space=pltpu.VMEM)
    return pl.pallas_call(
        body, name="adamw_small", in_specs=[vmem] * 4, out_specs=[vmem] * 3,
        out_shape=[jax.ShapeDtypeStruct(w.shape, F32)] * 3,
    )(w, g, m, v)


def _pack(parts):
    flat = jnp.concatenate([p.reshape(-1).astype(F32) for p in parts])
    rows = -(-flat.shape[0] // LANES)
    rows = -(-rows // SMALL_ROWS_ALIGN) * SMALL_ROWS_ALIGN
    flat = jnp.pad(flat, (0, rows * LANES - flat.shape[0]))
    return flat.reshape(rows, LANES)


def _unpack(buf, shapes):
    flat = buf.reshape(-1)
    out, pos = [], 0
    for shp in shapes:
        size = math.prod(shp)
        out.append(flat[pos:pos + size].reshape(shp))
        pos += size
    return out


def _section_widths(d):
    aw = d // 2
    kw = aw // ATTN_GROUP
    rw = d - aw
    rqw = (rw // RET_V_DIM) * RET_QK_DIM
    return (aw, kw, kw, aw, rqw, rqw, rw, rw)


def _layer_fwd(xl, nw, win_full, after_attn, wout_of, qn, kn, dec_f, dec_b, rn, cos, sin):
    h, ht = _rms_fwd(xl, nw)
    aq, ak, v, ag, rq, rk, rvb, rg = _proj_sections(h, win_full, _section_widths(xl.shape[1]),
                                                    (F32, F32, BF16, F32, F32, F32, BF16, F32))
    q, k, rqr, rkr = _prep_fwd(aq, ak, rq, rk, cos, sin, qn, kn)
    att, lse = _attn_fwd(q, k, v)
    ret = _retc_fwd(rqr, rkr, rvb, dec_f + after_attn(att), dec_b)
    y, yt = _gate_fwd(att, ag, ret, rg, rn)
    wout_full = wout_of(y)
    xn = _matmul(y, wout_full, name="out_proj", residual=xl)
    saved = dict(x=xl, ht=ht, aq=aq, ak=ak, ag=ag, rg=rg, q=q, k=k, v=v, rq=rqr, rk=rkr, rv=rvb,
                 att=att, lse=lse, ret=ret, yt=yt, win=win_full, wout=wout_full)
    return xn, saved


def _layer_bwd_weights(gb, sv, qn, kn, dec_f, dec_b, rn, cos, sin, on_dwout):
    dy = _matmul(gb, sv["wout"], name="d_y", trans_b=True)
    dwout = _matmul(sv["yt"], gb, name="d_wout", out_dtype=BF16)
    datt, dag, dret, drg, drn = _gate_bwd(dy, sv["att"], sv["ag"], sv["ret"], sv["rg"], rn + on_dwout(dwout))
    dq, dk, dav = _attn_bwd(sv["q"], sv["k"], sv["v"], sv["att"], datt, sv["lse"])
    drq, drk, drv, gf, gbk = _retc_bwd(sv["rq"], sv["rk"], sv["rv"], dret, dec_f, dec_b)
    dproj, dqn, dkn = _prep_bwd(dq, dk, drq, drk, sv["aq"], sv["ak"], cos, sin, qn, kn, dav, dag, drv, drg)
    dwin = _matmul(sv["ht"], dproj, name="d_win", out_dtype=BF16)
    small = dict(qn=dqn[0], kn=dkn[0], df=gf[:, 0, 0], db=gbk[:, 0, 0], rn=drn[0])
    return dproj, dwin, small


def _layer_bwd_input(g, dproj, sv, nw):
    dh = _matmul(dproj, sv["win"], name="d_h", trans_b=True, tk=_tile(dproj.shape[1], 5632, LANES))
    g, gb, dnw = _rms_bwd(dh, sv["x"], g, nw)
    return g, gb, dnw[0]


def kernel(x, norm_w, w_in, q_norm, k_norm, ret_decay_fwd, ret_decay_bwd, ret_norm, w_out, final_norm, loss_target, m_norm_w, m_w_in, m_q_norm, m_k_norm, m_ret_decay_fwd, m_ret_decay_bwd, m_ret_norm, m_w_out, m_final_norm, v_norm_w, v_w_in, v_q_norm, v_k_norm, v_ret_decay_fwd, v_ret_decay_bwd, v_ret_norm, v_w_out, v_final_norm):
    depth, d, _ = w_in.shape
    seq = x.shape[1]
    rw = _section_widths(d)[6]
    rheads = rw // RET_V_DIM
    rns = ret_norm.shape[-1]
    _, _, _, me = _mesh_position()

    target = loss_target[0]
    cos, sin = _rope_tables(seq)

    rn_all, _ = _exchange_small(_pack([ret_norm]), name="gather_ret_norm")
    rn_full = rn_all.reshape(N_DEV, -1)[:, :depth * rheads * rns].reshape(N_DEV, depth, rheads, rns)
    rn_full = jnp.transpose(rn_full, (1, 2, 0, 3)).reshape(depth, rw)

    dec_f = jnp.broadcast_to(ret_decay_fwd[:, :, None, None], (depth, rheads, 1, LANES))
    dec_b = jnp.broadcast_to(ret_decay_bwd[:, :, None, None], (depth, rheads, 1, LANES))

    win_bf = w_in.astype(BF16)
    wout_bf = w_out.astype(BF16)

    saved = []
    xl = x[0]
    first = (SIBLING,) + SAME_CORE_OF_CHIPS
    landed = _gather_wait(_gather_start(win_bf[0], 1, first, me, [], "in0"), 1, first, [], "in0")
    win_full = _pass_on_wait(_pass_on_start(win_bf[0], landed, 1, [], "in0"), 1, [], "in0")
    for l in range(depth):
        out_sent = _gather_start(wout_bf[l], 0, ALL_PEERS, me, [win_full], "out" + str(l))
        nw = norm_w[l] + out_sent[-1][0, 0]
        passed = {}
        if l + 1 < depth:
            in_sent = _gather_start(win_bf[l + 1], 1, first, me, [win_full, out_sent[-1]], "in" + str(l + 1))
            nw = nw + in_sent[-1][0, 0]

        def after_attn(att, passed=passed, l=l):
            if l + 1 == depth:
                return 0.0
            landed = _gather_wait(in_sent, 1, first, [att], "in" + str(l + 1))
            passed["on"] = _pass_on_start(win_bf[l + 1], landed, 1, [], "in" + str(l + 1))
            return passed["on"][-1][0, 0]

        def wout_of(y, out_sent=out_sent, l=l):
            return _gather_wait(out_sent, 0, ALL_PEERS, [y], "out" + str(l))

        xl, sv = _layer_fwd(xl, nw, win_full, after_attn, wout_of, q_norm[l], k_norm[l], dec_f[l], dec_b[l],
                            rn_full[l], cos, sin)
        saved.append(sv)
        if l + 1 < depth:
            win_full = _pass_on_wait(passed["on"], 1, [xl], "in" + str(l + 1))

    loss_row, g, gb, d_final = _loss_head(xl, target, final_norm)

    d_norm, d_qn, d_kn, d_df, d_db, d_rn = [], [], [], [], [], []
    lands = [None] * depth
    pending = None
    for l in reversed(range(depth)):
        sent = {}

        def on_dwout(dwout, sent=sent, l=l):
            sent["out"] = _scatter_start(dwout, 0, me, "out" + str(l))
            return sent["out"][-1][0, 0]

        dproj, dwin, sm = _layer_bwd_weights(gb, saved[l], q_norm[l], k_norm[l], dec_f[l], dec_b[l],
                                             rn_full[l], cos, sin, on_dwout)
        sent["in"] = _scatter_start(dwin, 1, me, "in" + str(l))
        g, gb, dnw = _layer_bwd_input(g, dproj, saved[l], norm_w[l] + sent["in"][-1][0, 0])
        if pending is not None:
            lands[l + 1] = (_scatter_wait(pending["in"], 1, [g], "in" + str(l + 1)),
                            _scatter_wait(pending["out"], 0, [g], "out" + str(l + 1)))
        pending = sent
        d_norm.append(dnw)
        d_qn.append(sm["qn"])
        d_kn.append(sm["kn"])
        d_df.append(sm["df"])
        d_db.append(sm["db"])
        d_rn.append(sm["rn"])
    for lst in (d_norm, d_qn, d_kn, d_df, d_db, d_rn):
        lst.reverse()
    order = pending["in"][-1]
    in_outs = [lax.empty(w_in.shape, F32) for _ in range(4)]
    out_outs = [lax.empty(w_out.shape, F32) for _ in range(4)]
    for l in reversed(range(depth)):
        if l == 0:
            lands[0] = (_scatter_wait(pending["in"], 1, [g, in_outs[0], out_outs[0]], "in0"),
                        _scatter_wait(pending["out"], 0, [g], "out0"))
        in_outs = _adamw_slabs(l, w_in, m_w_in, v_w_in, lands[l][0], in_outs, order, tr=256, name="adamw_w_in")
        out_outs = _adamw_slabs(l, w_out, m_w_out, v_w_out, lands[l][1], out_outs, order, tr=64, name="adamw_w_out")

    small_shapes = [(depth, d), (depth, HEAD_DIM), (depth, HEAD_DIM), (depth, rheads), (depth, rheads),
                    (depth, rheads, N_DEV * rns), (d,), (1,)]
    grads_local = [jnp.stack(d_norm), jnp.stack(d_qn), jnp.stack(d_kn), jnp.stack(d_df), jnp.stack(d_db),
                   jnp.stack(d_rn).reshape(depth, rheads, N_DEV * rns), d_final[0], loss_row[0, :1]]
    _, gsum = _exchange_small(_pack(grads_local), name="all_reduce_small", after=(in_outs[0], out_outs[0]))
    g_norm, g_qn, g_kn, g_df, g_db, g_rn_full, g_final, loss = _unpack(gsum, small_shapes)
    g_rn = lax.dynamic_slice_in_dim(g_rn_full, me * rns, rns, axis=2)
    small_g = [g_norm, g_qn, g_kn, g_df, g_db, g_rn, g_final]
    small_w = [norm_w, q_norm, k_norm, ret_decay_fwd, ret_decay_bwd, ret_norm, final_norm]
    small_m = [m_norm_w, m_q_norm, m_k_norm, m_ret_decay_fwd, m_ret_decay_bwd, m_ret_norm, m_final_norm]
    small_v = [v_norm_w, v_q_norm, v_k_norm, v_ret_decay_fwd, v_ret_decay_bwd, v_ret_norm, v_final_norm]
    shapes = [a.shape for a in small_w]
    sd, sm, sv2 = _adamw_small(_pack(small_w), _pack(small_g), _pack(small_m), _pack(small_v))
    small_d, small_m2, small_v2 = _unpack(sd, shapes), _unpack(sm, shapes), _unpack(sv2, shapes)

    def ordered(small, win_v, wout_v):
        return [small[0], win_v, small[1], small[2], small[3], small[4], small[5], wout_v, small[6]]

    grads = ordered(small_g, in_outs[0], out_outs[0])
    deltas = ordered(small_d, in_outs[1], out_outs[1])
    new_m = ordered(small_m2, in_outs[2], out_outs[2])
    new_v = ordered(small_v2, in_outs[3], out_outs[3])
    return (loss.reshape(()), g[None], *grads, *deltas, *new_m, *new_v)
```

```python
import functools
import math

import jax
import jax.numpy as jnp
import numpy as np
from jax import lax
from jax.experimental import pallas as pl
from jax.experimental.pallas import tpu as pltpu

F32 = jnp.float32
BF16 = jnp.bfloat16

N_DEV = 8
HEAD_DIM = 128
ATTN_GROUP = 4
RET_QK_DIM = 128
RET_V_DIM = 256
GRID_W = 64
ROPE_THETA = 10000.0
EPS = 1e-6
ADAM_LR = 0.001
ADAM_B1 = 0.9
ADAM_B2 = 0.999
ADAM_EPS = 1e-08
ADAM_WD = 0.01
ADAM_STEP = 10
ADAM_C1 = 1.0 - ADAM_B1 ** ADAM_STEP
ADAM_C2 = 1.0 - ADAM_B2 ** ADAM_STEP
LANES = 128
SMALL_ROWS_ALIGN = 8
VMEM_LIMIT = 56 * 1024 * 1024

NT_DIMS = (((1,), (1,)), ((), ()))
TN_DIMS = (((0,), (0,)), ((), ()))
MESH = pl.DeviceIdType.MESH


def _params(sem):
    return pltpu.CompilerParams(dimension_semantics=sem, vmem_limit_bytes=VMEM_LIMIT)


def _tile(dim, pref, align=16):
    if dim <= pref:
        return dim
    for t in range(pref - pref % align, 0, -align):
        if dim % t == 0:
            return t
    raise ValueError((dim, pref, align))


def _silu_parts(z):
    sg = 1.0 / (1.0 + jnp.exp(-z))
    return z * sg, sg * (1.0 + z * (1.0 - sg))


def _log_sigmoid(x):
    return jnp.minimum(x, 0.0) - jnp.log(1.0 + jnp.exp(-jnp.abs(x)))


def _swap_pairs(z):
    lane = lax.broadcasted_iota(jnp.int32, z.shape, 1)
    return jnp.where((lane % 64) < 32, pltpu.roll(z, 96, 1), pltpu.roll(z, 32, 1))


def _rope(z, cos, sin):
    return z * cos + _swap_pairs(z) * sin


def _rope_transposed(d, cos, sin):
    return d * cos + _swap_pairs(d * sin)


def _rope_tables(seq):
    rows = seq // GRID_W
    row = jnp.repeat(jnp.arange(rows), GRID_W).astype(F32)
    col = jnp.tile(jnp.arange(GRID_W), rows).astype(F32)
    axis_dim = HEAD_DIM // 2
    inv = ROPE_THETA ** (-jnp.arange(0, axis_dim, 2, dtype=F32) / axis_dim)
    ar = row[:, None] * inv[None, :]
    ac = col[:, None] * inv[None, :]
    cos = jnp.concatenate([jnp.cos(ar), jnp.cos(ar), jnp.cos(ac), jnp.cos(ac)], axis=-1)
    sin = jnp.concatenate([-jnp.sin(ar), jnp.sin(ar), -jnp.sin(ac), jnp.sin(ac)], axis=-1)
    return cos, sin


def _matmul(a, b, *, name, trans_b=False, out_dtype=F32, residual=None, tm=1024, tn=512, tk=4096, after=()):
    m, k = a.shape
    n = b.shape[0] if trans_b else b.shape[1]
    tm, tn, tk = _tile(m, tm), _tile(n, tn, LANES), _tile(k, tk, LANES)
    nk = k // tk
    has_res = residual is not None

    def body(*refs):
        a_ref, b_ref = refs[:2]
        r_ref = refs[2] if has_res else None
        o_ref = refs[2 + has_res + len(after)]
        if trans_b:
            part = lax.dot_general(a_ref[...], b_ref[...], NT_DIMS, preferred_element_type=F32)
        else:
            part = jnp.dot(a_ref[...], b_ref[...], preferred_element_type=F32)

        def finish(r):
            if has_res:
                r = r + r_ref[...]
            o_ref[...] = r.astype(o_ref.dtype)

        if nk == 1:
            finish(part)
        else:
            acc_ref = refs[-1]
            kk = pl.program_id(2)

            @pl.when(kk == 0)
            def _():
                acc_ref[...] = part

            @pl.when(kk > 0)
            def _():
                acc_ref[...] += part

            @pl.when(kk == nk - 1)
            def _():
                finish(acc_ref[...])

    if trans_b:
        b_spec = pl.BlockSpec((tn, tk), lambda i, j, kk: (j, kk))
    else:
        b_spec = pl.BlockSpec((tk, tn), lambda i, j, kk: (kk, j))
    in_specs = [pl.BlockSpec((tm, tk), lambda i, j, kk: (i, kk)), b_spec]
    args = [a, b]
    if has_res:
        in_specs.append(pl.BlockSpec((tm, tn), lambda i, j, kk: (i, j)))
        args.append(residual)
    in_specs += [ANY_SPEC] * len(after)
    args += list(after)
    return pl.pallas_call(
        body, name=name, grid=(m // tm, n // tn, nk),
        in_specs=in_specs,
        out_specs=pl.BlockSpec((tm, tn), lambda i, j, kk: (i, j)),
        out_shape=jax.ShapeDtypeStruct((m, n), out_dtype),
        scratch_shapes=[pltpu.VMEM((tm, tn), F32)] if nk > 1 else [],
        compiler_params=_params(("parallel", "parallel", "arbitrary")),
    )(*args)


def _proj_sections(a, b, widths, dtypes, *, tm=1024, tn=512):
    m, k = a.shape
    tm = _tile(m, tm)
    tn = _tile(min(widths), tn, LANES)
    assert all(w % tn == 0 for w in widths) and sum(widths) == b.shape[1]
    nblk = [w // tn for w in widths]
    first = [int(o) // tn for o in np.cumsum((0,) + tuple(widths))[:-1]]

    def body(a_ref, b_ref, *out_refs):
        j = pl.program_id(1)
        part = jnp.dot(a_ref[...], b_ref[...], preferred_element_type=F32)
        for o_ref, lo, n in zip(out_refs, first, nblk):
            @pl.when(jnp.logical_and(j >= lo, j < lo + n))
            def _(o_ref=o_ref):
                o_ref[...] = part.astype(o_ref.dtype)

    out_specs = [pl.BlockSpec((tm, tn), lambda i, j, lo=lo, n=n: (i, jnp.clip(j - lo, 0, n - 1)))
                 for lo, n in zip(first, nblk)]
    return pl.pallas_call(
        body, name="proj", grid=(m // tm, b.shape[1] // tn),
        in_specs=[pl.BlockSpec((tm, k), lambda i, j: (i, 0)), pl.BlockSpec((k, tn), lambda i, j: (0, j))],
        out_specs=out_specs,
        out_shape=[jax.ShapeDtypeStruct((m, w), dt) for w, dt in zip(widths, dtypes)],
        compiler_params=_params(("arbitrary", "arbitrary")),
    )(a, b)


def _rms_fwd(x, w, *, ts=256):
    s, d = x.shape
    ts = _tile(s, ts)

    def body(x_ref, w_ref, h_ref, ht_ref):
        xv = x_ref[...]
        r = lax.rsqrt(jnp.mean(xv * xv, axis=-1, keepdims=True) + EPS)
        h = xv * r * w_ref[...]
        h_ref[...] = h.astype(BF16)
        ht_ref[...] = h.T.astype(BF16)

    row = pl.BlockSpec((ts, d), lambda i: (i, 0))
    return pl.pallas_call(
        body, name="rms_fwd", grid=(s // ts,),
        in_specs=[row, pl.BlockSpec((1, d), lambda i: (0, 0))],
        out_specs=[row, pl.BlockSpec((d, ts), lambda i: (0, i))],
        out_shape=[jax.ShapeDtypeStruct((s, d), BF16), jax.ShapeDtypeStruct((d, s), BF16)],
        compiler_params=_params(("parallel",)),
    )(x, w.reshape(1, d))


def _rms_bwd(dh, x, g, w, *, ts=256):
    s, d = x.shape
    ts = _tile(s, ts)

    def body(dh_ref, x_ref, g_ref, w_ref, dx_ref, dxb_ref, dw_ref):
        xv = x_ref[...]
        r = lax.rsqrt(jnp.mean(xv * xv, axis=-1, keepdims=True) + EPS)
        xh = xv * r
        dhv = dh_ref[...]
        dn = dhv * w_ref[...]
        dx = g_ref[...] + r * (dn - xh * jnp.mean(dn * xh, axis=-1, keepdims=True))
        dx_ref[...] = dx
        dxb_ref[...] = dx.astype(BF16)
        part = jnp.sum(dhv * xh, axis=0, keepdims=True)

        @pl.when(pl.program_id(0) == 0)
        def _():
            dw_ref[...] = part

        @pl.when(pl.program_id(0) > 0)
        def _():
            dw_ref[...] += part

    row = pl.BlockSpec((ts, d), lambda i: (i, 0))
    vec = pl.BlockSpec((1, d), lambda i: (0, 0))
    return pl.pallas_call(
        body, name="rms_bwd", grid=(s // ts,),
        in_specs=[row, row, row, vec],
        out_specs=[row, row, vec],
        out_shape=[jax.ShapeDtypeStruct((s, d), F32), jax.ShapeDtypeStruct((s, d), BF16),
                   jax.ShapeDtypeStruct((1, d), F32)],
        compiler_params=_params(("arbitrary",)),
    )(dh, x, g, w.reshape(1, d))


def _loss_head(x, target, w, *, ts=256):
    s, d = x.shape
    ts = _tile(s, ts)

    def body(x_ref, t_ref, w_ref, loss_ref, dx_ref, dxb_ref, dw_ref):
        xv = x_ref[...]
        r = lax.rsqrt(jnp.mean(xv * xv, axis=-1, keepdims=True) + EPS)
        xh = xv * r
        wv = w_ref[...]
        diff = xh * wv - t_ref[...]
        lpart = 0.5 * jnp.sum(jnp.mean(diff * diff, axis=-1, keepdims=True), axis=0, keepdims=True)
        dout = diff * (1.0 / d)
        dn = dout * wv
        dx = r * (dn - xh * jnp.mean(dn * xh, axis=-1, keepdims=True))
        dx_ref[...] = dx
        dxb_ref[...] = dx.astype(BF16)
        part = jnp.sum(dout * xh, axis=0, keepdims=True)
        lrow = jnp.broadcast_to(lpart, loss_ref.shape)

        @pl.when(pl.program_id(0) == 0)
        def _():
            dw_ref[...] = part
            loss_ref[...] = lrow

        @pl.when(pl.program_id(0) > 0)
        def _():
            dw_ref[...] += part
            loss_ref[...] += lrow

    row = pl.BlockSpec((ts, d), lambda i: (i, 0))
    vec = pl.BlockSpec((1, d), lambda i: (0, 0))
    return pl.pallas_call(
        body, name="loss_head", grid=(s // ts,),
        in_specs=[row, row, vec],
        out_specs=[pl.BlockSpec((1, LANES), lambda i: (0, 0)), row, row, vec],
        out_shape=[jax.ShapeDtypeStruct((1, LANES), F32), jax.ShapeDtypeStruct((s, d), F32),
                   jax.ShapeDtypeStruct((s, d), BF16), jax.ShapeDtypeStruct((1, d), F32)],
        compiler_params=_params(("arbitrary",)),
    )(x, target, w.reshape(1, d))


def _prep_fwd(aq, ak, rq, rk, cos, sin, qw, kw, *, ts=256):
    s = aq.shape[0]
    ts = _tile(s, ts)
    attn_scale = HEAD_DIM ** -0.5
    ret_scale = RET_QK_DIM ** -0.5
    nq, nk, nr = aq.shape[1] // HEAD_DIM, ak.shape[1] // HEAD_DIM, rq.shape[1] // RET_QK_DIM

    def body(aq_ref, ak_ref, rq_ref, rk_ref, cos_ref, sin_ref, qw_ref, kw_ref,
             q_out, k_out, rq_out, rk_out):
        c, sn = cos_ref[...], sin_ref[...]

        def normed(u, w):
            return u * lax.rsqrt(jnp.mean(u * u, axis=-1, keepdims=True) + EPS) * w

        for j in range(nq):
            sl = slice(j * HEAD_DIM, (j + 1) * HEAD_DIM)
            q_out[:, sl] = (_rope(normed(aq_ref[:, sl], qw_ref[...]), c, sn) * attn_scale).astype(BF16)
        for j in range(nk):
            sl = slice(j * HEAD_DIM, (j + 1) * HEAD_DIM)
            k_out[:, sl] = _rope(normed(ak_ref[:, sl], kw_ref[...]), c, sn).astype(BF16)
        for j in range(nr):
            sl = slice(j * RET_QK_DIM, (j + 1) * RET_QK_DIM)
            rq_out[:, sl] = _rope(rq_ref[:, sl], c, sn).astype(BF16)
            rk_out[:, sl] = (_rope(rk_ref[:, sl], c, sn) * ret_scale).astype(BF16)

    def row(arr):
        return pl.BlockSpec((ts, arr.shape[1]), lambda i: (i, 0))

    vec = pl.BlockSpec((1, HEAD_DIM), lambda i: (0, 0))
    ins = [aq, ak, rq, rk]
    return pl.pallas_call(
        body, name="prep_fwd", grid=(s // ts,),
        in_specs=[row(a) for a in ins] + [row(cos), row(sin), vec, vec],
        out_specs=[row(a) for a in ins],
        out_shape=[jax.ShapeDtypeStruct(a.shape, BF16) for a in ins],
        compiler_params=_params(("parallel",)),
    )(*ins, cos, sin, qw.reshape(1, HEAD_DIM), kw.reshape(1, HEAD_DIM))


def _prep_bwd(dq, dk, drq, drk, aq, ak, cos, sin, qw, kw, dav, dag, drv, drg, *, ts=256):
    s = aq.shape[0]
    ts = _tile(s, ts)
    attn_scale = HEAD_DIM ** -0.5
    ret_scale = RET_QK_DIM ** -0.5
    nq, nk, nr = aq.shape[1] // HEAD_DIM, ak.shape[1] // HEAD_DIM, drq.shape[1] // RET_QK_DIM
    widths = (aq.shape[1], ak.shape[1], dav.shape[1], dag.shape[1], drq.shape[1], drk.shape[1], drv.shape[1],
              drg.shape[1])
    o_aq, o_ak, o_av, o_ag, o_rq, o_rk, o_rv, o_rg = (int(o) for o in np.cumsum((0,) + widths)[:-1])

    def body(dq_ref, dk_ref, drq_ref, drk_ref, aq_ref, ak_ref, cos_ref, sin_ref, dav_ref, dag_ref, drv_ref, drg_ref,
             qw_ref, kw_ref, dproj_ref, dqw_ref, dkw_ref):
        c, sn = cos_ref[...], sin_ref[...]
        for ref, off in ((dav_ref, o_av), (dag_ref, o_ag), (drv_ref, o_rv), (drg_ref, o_rg)):
            dproj_ref[:, off:off + ref.shape[1]] = ref[...]

        def unrope(d):
            return _rope_transposed(d, c, sn)

        def norm_bwd(dun, u, w):
            r = lax.rsqrt(jnp.mean(u * u, axis=-1, keepdims=True) + EPS)
            uh = u * r
            dn = dun * w
            du = r * (dn - uh * jnp.mean(dn * uh, axis=-1, keepdims=True))
            return du, jnp.sum(dun * uh, axis=0, keepdims=True)

        dqw = jnp.zeros((1, HEAD_DIM), F32)
        for j in range(nq):
            sl = slice(j * HEAD_DIM, (j + 1) * HEAD_DIM)
            du, dw = norm_bwd(unrope(dq_ref[:, sl] * attn_scale), aq_ref[:, sl], qw_ref[...])
            dproj_ref[:, o_aq + j * HEAD_DIM:o_aq + (j + 1) * HEAD_DIM] = du.astype(BF16)
            dqw = dqw + dw
        dkw = jnp.zeros((1, HEAD_DIM), F32)
        for j in range(nk):
            sl = slice(j * HEAD_DIM, (j + 1) * HEAD_DIM)
            du, dw = norm_bwd(unrope(dk_ref[:, sl]), ak_ref[:, sl], kw_ref[...])
            dproj_ref[:, o_ak + j * HEAD_DIM:o_ak + (j + 1) * HEAD_DIM] = du.astype(BF16)
            dkw = dkw + dw
        for j in range(nr):
            sl = slice(j * RET_QK_DIM, (j + 1) * RET_QK_DIM)
            dproj_ref[:, o_rq + j * RET_QK_DIM:o_rq + (j + 1) * RET_QK_DIM] = unrope(drq_ref[:, sl]).astype(BF16)
            dproj_ref[:, o_rk + j * RET_QK_DIM:o_rk + (j + 1) * RET_QK_DIM] = (
                unrope(drk_ref[:, sl] * ret_scale).astype(BF16))

        @pl.when(pl.program_id(0) == 0)
        def _():
            dqw_ref[...] = dqw
            dkw_ref[...] = dkw

        @pl.when(pl.program_id(0) > 0)
        def _():
            dqw_ref[...] += dqw
            dkw_ref[...] += dkw

    def row(arr):
        return pl.BlockSpec((ts, arr.shape[1]), lambda i: (i, 0))

    vec = pl.BlockSpec((1, HEAD_DIM), lambda i: (0, 0))
    ins = [dq, dk, drq, drk, aq, ak, cos, sin, dav, dag, drv, drg]
    total = sum(widths)
    return pl.pallas_call(
        body, name="prep_bwd", grid=(s // ts,),
        in_specs=[row(a) for a in ins] + [vec, vec],
        out_specs=[pl.BlockSpec((ts, total), lambda i: (i, 0)), vec, vec],
        out_shape=[jax.ShapeDtypeStruct((s, total), BF16)] + [jax.ShapeDtypeStruct((1, HEAD_DIM), F32)] * 2,
        compiler_params=_params(("arbitrary",)),
    )(*ins, qw.reshape(1, HEAD_DIM), kw.reshape(1, HEAD_DIM))


def _attn_fwd(q, k, v, *, tq=4096, sub=256):
    s, aw = q.shape
    tq = _tile(s, tq)
    sub = _tile(tq, sub)
    heads, kvh = aw // HEAD_DIM, k.shape[1] // HEAD_DIM
    grp = heads // kvh

    def body(q_ref, k_ref, v_ref, o_ref, lse_ref):
        kv_ = k_ref[...]
        v_ext = jnp.concatenate([v_ref[...], jnp.ones((s, HEAD_DIM), BF16)], axis=-1)
        for r in range(tq // sub):
            rows = slice(r * sub, (r + 1) * sub)
            sc = lax.dot_general(q_ref[rows, :], kv_, NT_DIMS, preferred_element_type=F32)
            m = jnp.max(sc, axis=-1, keepdims=True)
            p = jnp.exp((sc - m).astype(BF16))
            oe = jnp.dot(p, v_ext, preferred_element_type=F32)
            l = oe[:, HEAD_DIM:HEAD_DIM + 1]
            o_ref[rows, :] = (oe[:, :HEAD_DIM] / l).astype(o_ref.dtype)
            lse_ref[rows, :] = jnp.broadcast_to(m + jnp.log(l), (sub, HEAD_DIM))

    qspec = pl.BlockSpec((tq, HEAD_DIM), lambda kv, g, i: (i, kv * grp + g))
    kspec = pl.BlockSpec((s, HEAD_DIM), lambda kv, g, i: (0, kv))
    return pl.pallas_call(
        body, name="attn_fwd", grid=(kvh, grp, s // tq),
        in_specs=[qspec, kspec, kspec],
        out_specs=[qspec, qspec],
        out_shape=[jax.ShapeDtypeStruct((s, aw), BF16), jax.ShapeDtypeStruct((s, aw), F32)],
        compiler_params=_params(("parallel", "parallel", "parallel")),
    )(q, k, v)


def _attn_bwd(q, k, v, o, do, lse, *, tq=1024, sub=256):
    s, aw = q.shape
    tq = _tile(s, tq)
    sub = _tile(tq, sub)
    heads, kvh = aw // HEAD_DIM, k.shape[1] // HEAD_DIM
    grp = heads // kvh
    nq = s // tq

    def body(q_ref, k_ref, v_ref, o_ref, do_ref, lse_ref, dq_ref, dk_ref, dv_ref, dk_acc, dv_acc, p_scr, ds_scr):
        g, i = pl.program_id(1), pl.program_id(2)
        kv_, vv = k_ref[...], v_ref[...]
        for r in range(tq // sub):
            rows = slice(r * sub, (r + 1) * sub)
            qv, dov = q_ref[rows, :], do_ref[rows, :]
            sc = lax.dot_general(qv, kv_, NT_DIMS, preferred_element_type=F32)
            p = jnp.exp((sc - lse_ref[rows, :1]).astype(BF16))
            dp = lax.dot_general(dov, vv, NT_DIMS, preferred_element_type=F32)
            delta = jnp.sum(dov.astype(F32) * o_ref[rows, :].astype(F32), axis=-1, keepdims=True)
            ds = p * (dp - delta).astype(BF16)
            dq_ref[rows, :] = jnp.dot(ds, kv_, preferred_element_type=F32)
            p_scr[rows, :] = p
            ds_scr[rows, :] = ds
        dvp = lax.dot_general(p_scr[...], do_ref[...], TN_DIMS, preferred_element_type=F32)
        dkp = lax.dot_general(ds_scr[...], q_ref[...], TN_DIMS, preferred_element_type=F32)
        first = jnp.logical_and(g == 0, i == 0)

        @pl.when(first)
        def _():
            dv_acc[...] = dvp
            dk_acc[...] = dkp

        @pl.when(jnp.logical_not(first))
        def _():
            dv_acc[...] += dvp
            dk_acc[...] += dkp

        @pl.when(jnp.logical_and(g == grp - 1, i == nq - 1))
        def _():
            dk_ref[...] = dk_acc[...]
            dv_ref[...] = dv_acc[...].astype(dv_ref.dtype)

    qspec = pl.BlockSpec((tq, HEAD_DIM), lambda kv, g, i: (i, kv * grp + g))
    kspec = pl.BlockSpec((s, HEAD_DIM), lambda kv, g, i: (0, kv))
    return pl.pallas_call(
        body, name="attn_bwd", grid=(kvh, grp, nq),
        in_specs=[qspec, kspec, kspec, qspec, qspec, qspec],
        out_specs=[qspec, kspec, kspec],
        out_shape=[jax.ShapeDtypeStruct((s, aw), F32), jax.ShapeDtypeStruct(k.shape, F32),
                   jax.ShapeDtypeStruct(v.shape, BF16)],
        scratch_shapes=[pltpu.VMEM((s, HEAD_DIM), F32), pltpu.VMEM((s, HEAD_DIM), F32),
                        pltpu.VMEM((tq, s), BF16), pltpu.VMEM((tq, s), BF16)],
        compiler_params=_params(("parallel", "arbitrary", "arbitrary")),
    )(q, k, v, o, do, lse)


def _sum_all(z):
    return jnp.sum(jnp.sum(z, axis=0, keepdims=True), axis=1, keepdims=True)


def _chunk_consts(df_ref, db_ref, t):
    lf = _log_sigmoid(df_ref[0][:, :1])
    lb = _log_sigmoid(db_ref[0][:, :1])
    r = lax.broadcasted_iota(jnp.int32, (t, 1), 0).astype(F32)
    c = lax.broadcasted_iota(jnp.int32, (1, t), 1).astype(F32)
    diff = r - c
    dm = jnp.exp(diff * jnp.where(diff >= 0, lf, -lb))
    return dict(diff=diff, dm=dm, r=r,
                af=jnp.exp(lf * (r + 1.0)), bf=jnp.exp(lf * (t - 1.0 - r)), gf=jnp.exp(lf * t),
                ab=jnp.exp(lb * (t - r)), bb=jnp.exp(lb * r), gb=jnp.exp(lb * t))


def _scaled(x, f):
    return (x.astype(F32) * f).astype(BF16)


def _retc_specs(s):
    qspec = pl.BlockSpec((s, RET_QK_DIM), lambda h: (0, h))
    vspec = pl.BlockSpec((s, RET_V_DIM), lambda h: (0, h))
    dspec = pl.BlockSpec((1, 1, LANES), lambda h: (h, 0, 0))
    return qspec, vspec, dspec


def _retc_fwd(q, k, v, dec_f, dec_b, *, t=256):
    s, qw = q.shape
    t = _tile(s, t)
    heads, nc = qw // RET_QK_DIM, s // t
    qspec, vspec, dspec = _retc_specs(s)

    def body(q_ref, k_ref, v_ref, df_ref, db_ref, o_ref):
        cs = _chunk_consts(df_ref, db_ref, t)

        def rows_of(i):
            return pl.ds(pl.multiple_of(i * t, t), t)

        def forward(i, sf):
            rows = rows_of(i)
            qi, ki, vi = q_ref[rows, :], k_ref[rows, :], v_ref[rows, :]
            sc = lax.dot_general(qi, ki, NT_DIMS, preferred_element_type=F32)
            intra = jnp.dot((sc * cs["dm"]).astype(BF16), vi, preferred_element_type=F32)
            cross = jnp.dot(_scaled(qi, cs["af"]), sf.astype(BF16), preferred_element_type=F32)
            o_ref[rows, :] = intra + cross
            return cs["gf"] * sf + lax.dot_general(_scaled(ki, cs["bf"]), vi, TN_DIMS, preferred_element_type=F32)

        def backward(j, sb):
            rows = rows_of(nc - 1 - j)
            qi, ki, vi = q_ref[rows, :], k_ref[rows, :], v_ref[rows, :]
            o_ref[rows, :] += jnp.dot(_scaled(qi, cs["ab"]), sb.astype(BF16), preferred_element_type=F32)
            return cs["gb"] * sb + lax.dot_general(_scaled(ki, cs["bb"]), vi, TN_DIMS, preferred_element_type=F32)

        zero = jnp.zeros((RET_QK_DIM, RET_V_DIM), F32)
        lax.fori_loop(0, nc, forward, zero, unroll=True)
        lax.fori_loop(0, nc, backward, zero, unroll=True)

    return pl.pallas_call(
        body, name="ret_fwd", grid=(heads,),
        in_specs=[qspec, qspec, vspec, dspec, dspec],
        out_specs=vspec, out_shape=jax.ShapeDtypeStruct(v.shape, F32),
        compiler_params=_params(("parallel",)),
    )(q, k, v, dec_f, dec_b)


def _retc_bwd(q, k, v, do, dec_f, dec_b, *, t=256):
    s, qw = q.shape
    t = _tile(s, t)
    heads, nc = qw // RET_QK_DIM, s // t
    qspec, vspec, dspec = _retc_specs(s)
    gspec = pl.BlockSpec((1, 8, LANES), lambda h: (h, 0, 0))

    def body(q_ref, k_ref, v_ref, do_ref, df_ref, db_ref, dq_ref, dk_ref, dv_ref, gf_ref, gb_ref,
             sf_scr, sb_scr, dv_acc):
        cs = _chunk_consts(df_ref, db_ref, t)
        r, diff, dm = cs["r"], cs["diff"], cs["dm"]

        def rows_of(i):
            return pl.ds(pl.multiple_of(i * t, t), t)

        def tn(a, b):
            return lax.dot_general(a, b, TN_DIMS, preferred_element_type=F32)

        def nt(a, b):
            return lax.dot_general(a, b, NT_DIMS, preferred_element_type=F32)

        def states_f(i, sf):
            sf_scr[i] = sf
            rows = rows_of(i)
            return cs["gf"] * sf + tn(_scaled(k_ref[rows, :], cs["bf"]), v_ref[rows, :])

        def states_b(j, sb):
            i = nc - 1 - j
            sb_scr[i] = sb
            rows = rows_of(i)
            return cs["gb"] * sb + tn(_scaled(k_ref[rows, :], cs["bb"]), v_ref[rows, :])

        zero = jnp.zeros((RET_QK_DIM, RET_V_DIM), F32)
        lax.fori_loop(0, nc, states_f, zero, unroll=True)
        lax.fori_loop(0, nc, states_b, zero, unroll=True)

        def scan_grads(i, state, u, qf, kf, vi, doi, fa, fb, step, wa, wb):
            qa, kb = qf * fa, kf * fb
            ub = u.astype(BF16)
            dqa = nt(doi, state.astype(BF16))
            dkb = nt(vi, ub)
            dv = jnp.dot(kb.astype(BF16), ub, preferred_element_type=F32)
            dlog = _sum_all(dqa * qa * wa) + _sum_all(dkb * kb * wb) + t * step * _sum_all(u * state)
            u_new = step * u + tn(qa.astype(BF16), doi)
            return dqa * fa, dkb * fb, dv, u_new, dlog

        def sweep_f(j, carry):
            u, accf, accb = carry
            i = nc - 1 - j
            rows = rows_of(i)
            qi, ki, vi, doi = q_ref[rows, :], k_ref[rows, :], v_ref[rows, :], do_ref[rows, :]
            sc = nt(qi, ki)
            p = sc * dm
            dp = nt(doi, vi)
            ds = (dp * dm).astype(BF16)
            tt = dp * p * diff
            accf = accf + _sum_all(jnp.where(diff > 0, tt, 0.0))
            accb = accb + _sum_all(jnp.where(diff < 0, -tt, 0.0))
            dq1, dk1, dv1, u, dlog = scan_grads(i, sf_scr[i], u, qi.astype(F32), ki.astype(F32), vi, doi,
                                                cs["af"], cs["bf"], cs["gf"], r + 1.0, t - 1.0 - r)
            dq_ref[rows, :] = jnp.dot(ds, ki, preferred_element_type=F32) + dq1
            dk_ref[rows, :] = tn(ds, qi) + dk1
            dv_acc[rows, :] = tn(p.astype(BF16), doi) + dv1
            return u, accf + dlog, accb

        def sweep_b(i, carry):
            w, accb = carry
            rows = rows_of(i)
            qi, ki, vi, doi = q_ref[rows, :], k_ref[rows, :], v_ref[rows, :], do_ref[rows, :]
            dq1, dk1, dv1, w, dlog = scan_grads(i, sb_scr[i], w, qi.astype(F32), ki.astype(F32), vi, doi,
                                                cs["ab"], cs["bb"], cs["gb"], t - r, r)
            dq_ref[rows, :] += dq1
            dk_ref[rows, :] += dk1
            dv_acc[rows, :] += dv1
            return w, accb + dlog

        z11 = jnp.zeros((1, 1), F32)
        _, accf, accb = lax.fori_loop(0, nc, sweep_f, (zero, z11, z11), unroll=2)
        _, accb = lax.fori_loop(0, nc, sweep_b, (zero, accb), unroll=2)
        dv_ref[...] = dv_acc[...].astype(dv_ref.dtype)
        gf_ref[...] = jnp.broadcast_to((accf / (1.0 + jnp.exp(df_ref[0][:, :1]))).reshape(1, 1, 1), gf_ref.shape)
        gb_ref[...] = jnp.broadcast_to((accb / (1.0 + jnp.exp(db_ref[0][:, :1]))).reshape(1, 1, 1), gb_ref.shape)

    return pl.pallas_call(
        body, name="ret_bwd", grid=(heads,),
        in_specs=[qspec, qspec, vspec, vspec, dspec, dspec],
        out_specs=[qspec, qspec, vspec, gspec, gspec],
        out_shape=[jax.ShapeDtypeStruct(q.shape, F32), jax.ShapeDtypeStruct(k.shape, F32),
                   jax.ShapeDtypeStruct(v.shape, BF16),
                   jax.ShapeDtypeStruct((heads, 8, LANES), F32), jax.ShapeDtypeStruct((heads, 8, LANES), F32)],
        scratch_shapes=[pltpu.VMEM((nc, RET_QK_DIM, RET_V_DIM), F32), pltpu.VMEM((nc, RET_QK_DIM, RET_V_DIM), F32),
                        pltpu.VMEM((s, RET_V_DIM), F32)],
        compiler_params=_params(("parallel",)),
    )(q, k, v, do, dec_f, dec_b)


def _gate_fwd(att, ag, ret, rg, rnw, *, ts=256):
    s, aw = att.shape
    rw = ret.shape[1]
    ts = _tile(s, ts)
    rheads = rw // RET_V_DIM

    def body(att_ref, ag_ref, ret_ref, rg_ref, w_ref, y_ref, yt_ref):
        def put(lo, hi, val):
            y_ref[:, lo:hi] = val.astype(BF16)
            yt_ref[lo:hi, :] = val.T.astype(BF16)

        sa, _ = _silu_parts(ag_ref[...])
        put(0, aw, sa * att_ref[...].astype(F32))
        for h in range(rheads):
            sl = slice(h * RET_V_DIM, (h + 1) * RET_V_DIM)
            rt = ret_ref[:, sl]
            rn = rt * lax.rsqrt(jnp.mean(rt * rt, axis=-1, keepdims=True) + EPS) * w_ref[:, sl]
            sr, _ = _silu_parts(rg_ref[:, sl])
            put(aw + h * RET_V_DIM, aw + (h + 1) * RET_V_DIM, sr * rn)

    def row(w):
        return pl.BlockSpec((ts, w), lambda i: (i, 0))

    return pl.pallas_call(
        body, name="gate_fwd", grid=(s // ts,),
        in_specs=[row(aw), row(aw), row(rw), row(rw), pl.BlockSpec((1, rw), lambda i: (0, 0))],
        out_specs=[row(aw + rw), pl.BlockSpec((aw + rw, ts), lambda i: (0, i))],
        out_shape=[jax.ShapeDtypeStruct((s, aw + rw), BF16), jax.ShapeDtypeStruct((aw + rw, s), BF16)],
        compiler_params=_params(("parallel",)),
    )(att, ag, ret, rg, rnw.reshape(1, rw))


def _gate_bwd(dy, att, ag, ret, rg, rnw, *, ts=256):
    s, aw = att.shape
    rw = ret.shape[1]
    ts = _tile(s, ts)
    rheads = rw // RET_V_DIM

    def body(dy_ref, att_ref, ag_ref, ret_ref, rg_ref, w_ref, datt_ref, dag_ref, dret_ref, drg_ref, dw_ref):
        sa, dsa = _silu_parts(ag_ref[...])
        dya = dy_ref[:, :aw]
        datt_ref[...] = (dya * sa).astype(BF16)
        dag_ref[...] = (dya * att_ref[...].astype(F32) * dsa).astype(BF16)
        parts = []
        for h in range(rheads):
            sl = slice(h * RET_V_DIM, (h + 1) * RET_V_DIM)
            rt = ret_ref[:, sl]
            rr = lax.rsqrt(jnp.mean(rt * rt, axis=-1, keepdims=True) + EPS)
            rh = rt * rr
            wv = w_ref[:, sl]
            sr, dsr = _silu_parts(rg_ref[:, sl])
            dyr = dy_ref[:, aw + h * RET_V_DIM:aw + (h + 1) * RET_V_DIM]
            drg_ref[:, sl] = (dyr * rh * wv * dsr).astype(BF16)
            drn = dyr * sr
            dn = drn * wv
            dret_ref[:, sl] = (rr * (dn - rh * jnp.mean(dn * rh, axis=-1, keepdims=True))).astype(BF16)
            parts.append(jnp.sum(drn * rh, axis=0, keepdims=True))
        part = jnp.concatenate(parts, axis=-1)

        @pl.when(pl.program_id(0) == 0)
        def _():
            dw_ref[...] = part

        @pl.when(pl.program_id(0) > 0)
        def _():
            dw_ref[...] += part

    def row(w):
        return pl.BlockSpec((ts, w), lambda i: (i, 0))

    vec = pl.BlockSpec((1, rw), lambda i: (0, 0))
    return pl.pallas_call(
        body, name="gate_bwd", grid=(s // ts,),
        in_specs=[row(aw + rw), row(aw), row(aw), row(rw), row(rw), vec],
        out_specs=[row(aw), row(aw), row(rw), row(rw), vec],
        out_shape=[jax.ShapeDtypeStruct((s, aw), BF16), jax.ShapeDtypeStruct((s, aw), BF16),
                   jax.ShapeDtypeStruct((s, rw), BF16), jax.ShapeDtypeStruct((s, rw), BF16),
                   jax.ShapeDtypeStruct((1, rw), F32)],
        compiler_params=_params(("arbitrary",)),
    )(dy, att, ag, ret, rg, rnw.reshape(1, rw))


def _mesh_position():
    x, y, c = lax.axis_index("x"), lax.axis_index("y"), lax.axis_index("c")
    return x, y, c, 4 * x + 2 * y + c


def _peer(x, y, c, k):
    px = 1 - x if k & 4 else x
    py = 1 - y if k & 2 else y
    pc = 1 - c if k & 1 else c
    return (px, py, pc), 4 * px + 2 * py + pc


HBM_SPEC = pl.BlockSpec(memory_space=pltpu.HBM)
SEM_SPEC = pl.BlockSpec(memory_space=pltpu.SEMAPHORE)
ANY_SPEC = pl.BlockSpec(memory_space=pl.ANY)
DATAFLOW = pltpu.SideEffectType.DATAFLOW_SIDE_EFFECTING


def _hbm(a):
    return pltpu.with_memory_space_constraint(a, pltpu.HBM)


def _split_start(name, copies, n, src, land, after):
    def body(*refs):
        (send_sems, recv_sems), token = refs[2 + len(after):4 + len(after)], refs[-1]
        sends, _ = copies(refs[0], refs[1], send_sems, recv_sems)
        for cp in sends:
            cp.start()
        token[...] = jnp.zeros_like(token)

    return pl.pallas_call(
        body, name=name,
        out_shape=(pltpu.SemaphoreType.DMA((n,)), pltpu.SemaphoreType.DMA((n,)),
                   pltpu.HBM(src.shape, src.dtype), pltpu.HBM(land.shape, land.dtype),
                   jax.ShapeDtypeStruct((8, LANES), F32)),
        in_specs=[HBM_SPEC] * 2 + [ANY_SPEC] * len(after),
        out_specs=(SEM_SPEC, SEM_SPEC, HBM_SPEC, HBM_SPEC, pl.BlockSpec(memory_space=pltpu.VMEM)),
        input_output_aliases={0: 2, 1: 3},
        compiler_params=pltpu.CompilerParams(has_side_effects=DATAFLOW),
    )(_hbm(src), _hbm(land), *after)


def _split_wait(name, copies, started, after):
    send_sems, recv_sems, src, land = started[:4]

    def body(*refs):
        sends, recvs = copies(refs[0], refs[1], refs[2], refs[3])
        for cp in sends:
            cp.wait_send()
        for cp in recvs:
            cp.wait_recv()

    return pl.pallas_call(
        body, name=name,
        out_shape=(pltpu.HBM(src.shape, src.dtype), pltpu.HBM(land.shape, land.dtype)),
        in_specs=[HBM_SPEC] * 2 + [SEM_SPEC, SEM_SPEC] + [ANY_SPEC] * len(after),
        out_specs=(HBM_SPEC,) * 2,
        input_output_aliases={0: 0, 1: 1},
        compiler_params=pltpu.CompilerParams(has_side_effects=DATAFLOW),
    )(src, land, send_sems, recv_sems, *after)


def _slab(ref, p, size, axis):
    if axis == 1:
        return ref.at[:, pl.ds(pl.multiple_of(p * size, LANES), size)]
    return ref.at[pl.ds(pl.multiple_of(p * size, 16), size), :]


ALL_PEERS = tuple(range(1, N_DEV))
SIBLING = 1
SAME_CORE_OF_CHIPS = (2, 4, 6)


def _gather_copies(size, axis, ks):
    def copies(shard_ref, full_ref, send_sems, recv_sems):
        x, y, c, me = _mesh_position()
        sends, recvs = [], []
        for j, k in enumerate(ks):
            peer, pid = _peer(x, y, c, k)
            sends.append(pltpu.make_async_remote_copy(
                src_ref=shard_ref, dst_ref=_slab(full_ref, me, size, axis), send_sem=send_sems.at[j],
                recv_sem=recv_sems.at[j], device_id=peer, device_id_type=MESH))
            recvs.append(pltpu.make_async_remote_copy(
                src_ref=shard_ref, dst_ref=_slab(full_ref, pid, size, axis), send_sem=send_sems.at[j],
                recv_sem=recv_sems.at[j], device_id=peer, device_id_type=MESH))
        return sends, recvs

    return copies


def _pass_on_copies(size, axis):
    def copies(shard_ref, full_ref, send_sems, recv_sems):
        x, y, c, me = _mesh_position()
        sibling, _ = _peer(x, y, c, SIBLING)
        sends, recvs = [], []
        for j, k in enumerate(SAME_CORE_OF_CHIPS):
            _, landed = _peer(x, y, c, k)
            _, siblings = _peer(x, y, c, k ^ SIBLING)
            mine = _slab(full_ref, landed, size, axis)
            sends.append(pltpu.make_async_remote_copy(
                src_ref=mine, dst_ref=mine, send_sem=send_sems.at[j], recv_sem=recv_sems.at[j],
                device_id=sibling, device_id_type=MESH))
            recvs.append(pltpu.make_async_remote_copy(
                src_ref=mine, dst_ref=_slab(full_ref, siblings, size, axis), send_sem=send_sems.at[j],
                recv_sem=recv_sems.at[j], device_id=sibling, device_id_type=MESH))
        return sends, recvs

    return copies


def _scatter_copies(size, axis):
    def copies(grad_ref, land_ref, send_sems, recv_sems):
        x, y, c, me = _mesh_position()
        sends, recvs = [], []
        for k in range(1, N_DEV):
            peer, pid = _peer(x, y, c, k)
            src = _slab(grad_ref, pid, size, axis)
            sends.append(pltpu.make_async_remote_copy(
                src_ref=src, dst_ref=land_ref.at[me], send_sem=send_sems.at[k - 1], recv_sem=recv_sems.at[k - 1],
                device_id=peer, device_id_type=MESH))
            recvs.append(pltpu.make_async_remote_copy(
                src_ref=src, dst_ref=land_ref.at[pid], send_sem=send_sems.at[k - 1], recv_sem=recv_sems.at[k - 1],
                device_id=peer, device_id_type=MESH))
        return sends, recvs

    return copies


PLACE_BANDS = 8


def _place_own(name, src, out_shape, in_spec, out_spec, steps, me):
    def body(me_ref, src_ref, out_ref):
        out_ref[...] = src_ref[...]

    return pl.pallas_call(
        body, name=name, out_shape=out_shape,
        grid_spec=pltpu.PrefetchScalarGridSpec(num_scalar_prefetch=1, grid=(steps,), in_specs=[in_spec],
                                               out_specs=out_spec),
        compiler_params=_params(("parallel",)),
    )(me.reshape(1).astype(jnp.int32), src)


def _gather_start(shard, axis, ks, me, after, tag):
    rows, cols = shard.shape
    size = shard.shape[axis]
    full_shape = tuple(N_DEV * n if a == axis else n for a, n in enumerate(shard.shape))
    band = rows // PLACE_BANDS
    in_spec = pl.BlockSpec((band, cols), lambda i, me_ref: (i, 0))
    if axis == 1:
        out_spec = pl.BlockSpec((band, cols), lambda i, me_ref: (i, me_ref[0]))
    else:
        out_spec = pl.BlockSpec((band, cols), lambda i, me_ref: (me_ref[0] * PLACE_BANDS + i, 0))
    full = _place_own("place_shard", shard, jax.ShapeDtypeStruct(full_shape, shard.dtype), in_spec, out_spec,
                      PLACE_BANDS, me)
    return _split_start("gather_start_" + tag, _gather_copies(size, axis, ks), len(ks), shard, full, after)


def _gather_wait(started, axis, ks, after, tag):
    size = started[2].shape[axis]
    return _split_wait("gather_wait_" + tag, _gather_copies(size, axis, ks), started, after)[1]


def _pass_on_start(shard, full, axis, after, tag):
    size = shard.shape[axis]
    return _split_start("pass_on_start_" + tag, _pass_on_copies(size, axis), len(SAME_CORE_OF_CHIPS), shard, full, after)


def _pass_on_wait(started, axis, after, tag):
    size = started[2].shape[axis]
    return _split_wait("pass_on_wait_" + tag, _pass_on_copies(size, axis), started, after)[1]


def _scatter_start(grad, axis, me, tag):
    size = grad.shape[axis] // N_DEV
    rows, cols = tuple(size if a == axis else n for a, n in enumerate(grad.shape))
    band = rows // PLACE_BANDS
    if axis == 1:
        in_spec = pl.BlockSpec((band, cols), lambda i, me_ref: (i, me_ref[0]))
    else:
        in_spec = pl.BlockSpec((band, cols), lambda i, me_ref: (me_ref[0] * PLACE_BANDS + i, 0))
    out_spec = pl.BlockSpec((None, band, cols), lambda i, me_ref: (me_ref[0], i, 0))
    land = _place_own("place_slab", grad, jax.ShapeDtypeStruct((N_DEV, rows, cols), grad.dtype), in_spec, out_spec,
                      PLACE_BANDS, me)
    return _split_start("scatter_start_" + tag, _scatter_copies(size, axis), N_DEV - 1, grad, land, [])


def _scatter_wait(started, axis, after, tag):
    size = started[2].shape[axis] // N_DEV
    land = _split_wait("scatter_wait_" + tag, _scatter_copies(size, axis), started, after)[1]
    return [(land, p) for p in range(N_DEV)]


N_CHIPS = N_DEV // 2


def _pair_copies(size, axis):
    def copies(grad_ref, land_ref, send_sems, recv_sems):
        x, y, c, me = _mesh_position()
        sibling, _ = _peer(x, y, c, SIBLING)
        sends, recvs = [], []
        for j in range(N_CHIPS):
            _, owner = _peer(x, y, c, (2 * j) ^ SIBLING)
            for lst in (sends, recvs):
                lst.append(pltpu.make_async_remote_copy(
                    src_ref=_slab(grad_ref, owner, size, axis), dst_ref=land_ref.at[j], send_sem=send_sems.at[j],
                    recv_sem=recv_sems.at[j], device_id=sibling, device_id_type=MESH))
        return sends, recvs

    return copies


def _chips_copies():
    def copies(pair_ref, land_ref, send_sems, recv_sems):
        x, y, c, me = _mesh_position()
        sends, recvs = [], []
        for j in range(1, N_CHIPS):
            owner, _ = _peer(x, y, c, 2 * j)
            for lst in (sends, recvs):
                lst.append(pltpu.make_async_remote_copy(
                    src_ref=pair_ref.at[j], dst_ref=land_ref.at[j], send_sem=send_sems.at[j - 1],
                    recv_sem=recv_sems.at[j - 1], device_id=owner, device_id_type=MESH))
        return sends, recvs

    return copies


def _pair_start(grad, axis, tag):
    size = grad.shape[axis] // N_DEV
    rows, cols = tuple(size if a == axis else n for a, n in enumerate(grad.shape))
    land = lax.empty((N_CHIPS, rows, cols), grad.dtype)
    return _split_start("pair_start_" + tag, _pair_copies(size, axis), N_CHIPS, grad, land, [])


def _pair_sums(started, axis, me, after, tag):
    size = started[2].shape[axis] // N_DEV
    grad, land = _split_wait("pair_wait_" + tag, _pair_copies(size, axis), started, after)
    _, rows, cols = land.shape
    band = rows // PLACE_BANDS
    if axis == 1:
        mine = pl.BlockSpec((band, cols), lambda j, i, me_ref: (i, me_ref[0] ^ (2 * j)))
    else:
        mine = pl.BlockSpec((band, cols), lambda j, i, me_ref: ((me_ref[0] ^ (2 * j)) * PLACE_BANDS + i, 0))
    slot = pl.BlockSpec((None, band, cols), lambda j, i, me_ref: (j, i, 0))

    def body(own_ref, mine_ref, theirs_ref, out_ref):
        out_ref[...] = (mine_ref[...].astype(F32) + theirs_ref[...].astype(F32)).astype(out_ref.dtype)

    return pl.pallas_call(
        body, name="pair_sums", out_shape=jax.ShapeDtypeStruct(land.shape, land.dtype),
        grid_spec=pltpu.PrefetchScalarGridSpec(num_scalar_prefetch=1, grid=(N_CHIPS, PLACE_BANDS),
                                               in_specs=[mine, slot], out_specs=slot),
        compiler_params=_params(("parallel", "parallel")),
    )(me.reshape(1).astype(jnp.int32), grad, land)


def _chips_start(pairs, tag):
    return _split_start("chips_start_" + tag, _chips_copies(), N_CHIPS - 1, pairs, lax.empty(pairs.shape, pairs.dtype), [])


def _chips_wait(started, after, tag):
    pairs, land = _split_wait("chips_wait_" + tag, _chips_copies(), started, after)
    return [(pairs, 0)] + [(land, j) for j in range(1, N_CHIPS)]


def _exchange_small(buf, *, name, after=()):
    r = buf.shape[0]

    def body(*refs):
        buf_ref = refs[0]
        all_ref, sum_ref, send_sems, recv_sems = refs[1 + len(after):]
        x, y, c, me = _mesh_position()
        all_ref[me] = buf_ref[...]
        sends, recvs = [], []
        for k in range(1, N_DEV):
            peer, pid = _peer(x, y, c, k)
            sends.append(pltpu.make_async_remote_copy(
                src_ref=buf_ref, dst_ref=all_ref.at[me], send_sem=send_sems.at[k - 1], recv_sem=recv_sems.at[k - 1],
                device_id=peer, device_id_type=MESH))
            recvs.append(pltpu.make_async_remote_copy(
                src_ref=buf_ref, dst_ref=all_ref.at[pid], send_sem=send_sems.at[k - 1], recv_sem=recv_sems.at[k - 1],
                device_id=peer, device_id_type=MESH))
        for cp in sends:
            cp.start()
        for cp in recvs:
            cp.wait_recv()
        for cp in sends:
            cp.wait_send()
        total = all_ref[0]
        for p in range(1, N_DEV):
            total = total + all_ref[p]
        sum_ref[...] = total

    vmem = pl.BlockSpec(memory_space=pltpu.VMEM)
    return pl.pallas_call(
        body, name=name,
        in_specs=[vmem] + [ANY_SPEC] * len(after), out_specs=[vmem, vmem],
        out_shape=[jax.ShapeDtypeStruct((N_DEV, r, LANES), F32), jax.ShapeDtypeStruct((r, LANES), F32)],
        scratch_shapes=[pltpu.SemaphoreType.DMA((N_DEV - 1,)), pltpu.SemaphoreType.DMA((N_DEV - 1,))],
        compiler_params=pltpu.CompilerParams(has_side_effects=True),
    )(buf, *after)


def _adamw_math(w, g, m, v):
    m2 = ADAM_B1 * m + (1.0 - ADAM_B1) * g
    v2 = ADAM_B2 * v + (1.0 - ADAM_B2) * (g * g)
    delta = -ADAM_LR * ((m2 / ADAM_C1) / (jnp.sqrt(v2 / ADAM_C2) + ADAM_EPS) + ADAM_WD * w)
    return delta, m2, v2


def _adamw_slabs(layer, w, m, v, addends, outs, order, *, tr, name):
    depth, r, c = w.shape
    tr = _tile(r, tr)
    n = len(addends)

    def body(*refs):
        w_ref, m_ref, v_ref = refs[:3]
        g_ref, d_ref, m2_ref, v2_ref = refs[-4:]
        g = refs[3][...].astype(F32)
        for a_ref in refs[4:3 + n]:
            g = g + a_ref[...].astype(F32)
        delta, m2, v2 = _adamw_math(w_ref[...], g, m_ref[...], v_ref[...])
        g_ref[...] = g
        d_ref[...] = delta
        m2_ref[...] = m2
        v2_ref[...] = v2

    row = pl.BlockSpec((None, tr, c), lambda i: (layer, i, 0))
    slots = [pl.BlockSpec((None, tr, c), lambda i, p=p: (p, i, 0)) for _, p in addends]
    first_out = 3 + n + 1
    return pl.pallas_call(
        body, name=name, grid=(r // tr,),
        in_specs=[row, row, row] + slots + [pl.BlockSpec((8, LANES), lambda i: (0, 0))] + [ANY_SPEC] * 4,
        out_specs=[row] * 4, out_shape=[jax.ShapeDtypeStruct((depth, r, c), F32)] * 4,
        input_output_aliases={first_out + t: t for t in range(4)},
        compiler_params=_params(("parallel",)),
    )(w, m, v, *[a for a, _ in addends], order, *outs)


def _adamw_small(w, g, m, v):
    def body(w_ref, g_ref, m_ref, v_ref, d_ref, m2_ref, v2_ref):
        delta, m2, v2 = _adamw_math(w_ref[...], g_ref[...], m_ref[...], v_ref[...])
        d_ref[...] = delta
        m2_ref[...] = m2
        v2_ref[...] = v2

    vmem = pl.BlockSpec(memory_space=pltpu.VMEM)
    return pl.pallas_call(
        body, name="adamw_small", in_specs=[vmem] * 4, out_specs=[vmem] * 3,
        out_shape=[jax.ShapeDtypeStruct(w.shape, F32)] * 3,
    )(w, g, m, v)


def _pack(parts):
    flat = jnp.concatenate([p.reshape(-1).astype(F32) for p in parts])
    rows = -(-flat.shape[0] // LANES)
    rows = -(-rows // SMALL_ROWS_ALIGN) * SMALL_ROWS_ALIGN
    flat = jnp.pad(flat, (0, rows * LANES - flat.shape[0]))
    return flat.reshape(rows, LANES)


def _unpack(buf, shapes):
    flat = buf.reshape(-1)
    out, pos = [], 0
    for shp in shapes:
        size = math.prod(shp)
        out.append(flat[pos:pos + size].reshape(shp))
        pos += size
    return out


def _section_widths(d):
    aw = d // 2
    kw = aw // ATTN_GROUP
    rw = d - aw
    rqw = (rw // RET_V_DIM) * RET_QK_DIM
    return (aw, kw, kw, aw, rqw, rqw, rw, rw)


def _layer_fwd(xl, nw, win_full, after_attn, wout_of, qn, kn, dec_f, dec_b, rn, cos, sin):
    h, ht = _rms_fwd(xl, nw)
    aq, ak, v, ag, rq, rk, rvb, rg = _proj_sections(h, win_full, _section_widths(xl.shape[1]),
                                                    (F32, F32, BF16, F32, F32, F32, BF16, F32))
    q, k, rqr, rkr = _prep_fwd(aq, ak, rq, rk, cos, sin, qn, kn)
    att, lse = _attn_fwd(q, k, v)
    ret = _retc_fwd(rqr, rkr, rvb, dec_f + after_attn(att), dec_b)
    y, yt = _gate_fwd(att, ag, ret, rg, rn)
    wout_full = wout_of(y)
    xn = _matmul(y, wout_full, name="out_proj", residual=xl)
    saved = dict(x=xl, ht=ht, aq=aq, ak=ak, ag=ag, rg=rg, q=q, k=k, v=v, rq=rqr, rk=rkr, rv=rvb,
                 att=att, lse=lse, ret=ret, yt=yt, win=win_full, wout=wout_full)
    return xn, saved


def _layer_bwd_weights(gb, sv, qn, kn, dec_f, dec_b, rn, cos, sin, on_dwout):
    dy = _matmul(gb, sv["wout"], name="d_y", trans_b=True)
    dwout = _matmul(sv["yt"], gb, name="d_wout", out_dtype=BF16)
    datt, dag, dret, drg, drn = _gate_bwd(dy, sv["att"], sv["ag"], sv["ret"], sv["rg"], rn + on_dwout(dwout))
    dq, dk, dav = _attn_bwd(sv["q"], sv["k"], sv["v"], sv["att"], datt, sv["lse"])
    drq, drk, drv, gf, gbk = _retc_bwd(sv["rq"], sv["rk"], sv["rv"], dret, dec_f, dec_b)
    dproj, dqn, dkn = _prep_bwd(dq, dk, drq, drk, sv["aq"], sv["ak"], cos, sin, qn, kn, dav, dag, drv, drg)
    dwin = _matmul(sv["ht"], dproj, name="d_win", out_dtype=BF16)
    small = dict(qn=dqn[0], kn=dkn[0], df=gf[:, 0, 0], db=gbk[:, 0, 0], rn=drn[0])
    return dproj, dwin, small


def _layer_bwd_input(g, dproj, sv, nw, behind, after_dh):
    dh = _matmul(dproj, sv["win"], name="d_h", trans_b=True, tk=_tile(dproj.shape[1], 5632, LANES), after=behind)
    g, gb, dnw = _rms_bwd(dh, sv["x"], g, nw + after_dh(dh))
    return g, gb, dnw[0]


def kernel(x, norm_w, w_in, q_norm, k_norm, ret_decay_fwd, ret_decay_bwd, ret_norm, w_out, final_norm, loss_target, m_norm_w, m_w_in, m_q_norm, m_k_norm, m_ret_decay_fwd, m_ret_decay_bwd, m_ret_norm, m_w_out, m_final_norm, v_norm_w, v_w_in, v_q_norm, v_k_norm, v_ret_decay_fwd, v_ret_decay_bwd, v_ret_norm, v_w_out, v_final_norm):
    depth, d, _ = w_in.shape
    seq = x.shape[1]
    rw = _section_widths(d)[6]
    rheads = rw // RET_V_DIM
    rns = ret_norm.shape[-1]
    _, _, _, me = _mesh_position()

    target = loss_target[0]
    cos, sin = _rope_tables(seq)

    rn_all, _ = _exchange_small(_pack([ret_norm]), name="gather_ret_norm")
    rn_full = rn_all.reshape(N_DEV, -1)[:, :depth * rheads * rns].reshape(N_DEV, depth, rheads, rns)
    rn_full = jnp.transpose(rn_full, (1, 2, 0, 3)).reshape(depth, rw)

    dec_f = jnp.broadcast_to(ret_decay_fwd[:, :, None, None], (depth, rheads, 1, LANES))
    dec_b = jnp.broadcast_to(ret_decay_bwd[:, :, None, None], (depth, rheads, 1, LANES))

    win_bf = w_in.astype(BF16)
    wout_bf = w_out.astype(BF16)

    saved = []
    xl = x[0]
    first = (SIBLING,) + SAME_CORE_OF_CHIPS
    landed = _gather_wait(_gather_start(win_bf[0], 1, first, me, [], "in0"), 1, first, [], "in0")
    win_full = _pass_on_wait(_pass_on_start(win_bf[0], landed, 1, [], "in0"), 1, [], "in0")
    for l in range(depth):
        out_sent = _gather_start(wout_bf[l], 0, ALL_PEERS, me, [win_full], "out" + str(l))
        nw = norm_w[l] + out_sent[-1][0, 0]
        passed = {}
        if l + 1 < depth:
            in_sent = _gather_start(win_bf[l + 1], 1, first, me, [win_full, out_sent[-1]], "in" + str(l + 1))
            nw = nw + in_sent[-1][0, 0]

        def after_attn(att, passed=passed, l=l):
            if l + 1 == depth:
                return 0.0
            landed = _gather_wait(in_sent, 1, first, [att], "in" + str(l + 1))
            passed["on"] = _pass_on_start(win_bf[l + 1], landed, 1, [], "in" + str(l + 1))
            return passed["on"][-1][0, 0]

        def wout_of(y, out_sent=out_sent, l=l):
            return _gather_wait(out_sent, 0, ALL_PEERS, [y], "out" + str(l))

        xl, sv = _layer_fwd(xl, nw, win_full, after_attn, wout_of, q_norm[l], k_norm[l], dec_f[l], dec_b[l],
                            rn_full[l], cos, sin)
        saved.append(sv)
        if l + 1 < depth:
            win_full = _pass_on_wait(passed["on"], 1, [xl], "in" + str(l + 1))

    loss_row, g, gb, d_final = _loss_head(xl, target, final_norm)

    d_norm, d_qn, d_kn, d_df, d_db, d_rn = [], [], [], [], [], []
    lands = [None] * depth
    pending = None
    for l in reversed(range(depth)):
        sent = {}

        def on_dwout(dwout, sent=sent, l=l):
            sent["out"] = _scatter_start(dwout, 0, me, "out" + str(l))
            return sent["out"][-1][0, 0]

        def after_dh(dh, sent=sent, l=l):
            pairs = _pair_sums(sent["pair"], 1, me, [dh], "in" + str(l))
            sent["in"] = _chips_start(pairs, "in" + str(l))
            return sent["in"][-1][0, 0]

        dproj, dwin, sm = _layer_bwd_weights(gb, saved[l], q_norm[l], k_norm[l], dec_f[l], dec_b[l],
                                             rn_full[l], cos, sin, on_dwout)
        sent["pair"] = _pair_start(dwin, 1, "in" + str(l))
        g, gb, dnw = _layer_bwd_input(g, dproj, saved[l], norm_w[l], [sent["pair"][-1]], after_dh)
        if pending is not None:
            lands[l + 1] = (_chips_wait(pending["in"], [g], "in" + str(l + 1)),
                            _scatter_wait(pending["out"], 0, [g], "out" + str(l + 1)))
        pending = sent
        d_norm.append(dnw)
        d_qn.append(sm["qn"])
        d_kn.append(sm["kn"])
        d_df.append(sm["df"])
        d_db.append(sm["db"])
        d_rn.append(sm["rn"])
    for lst in (d_norm, d_qn, d_kn, d_df, d_db, d_rn):
        lst.reverse()
    order = pending["in"][-1]
    in_outs = [lax.empty(w_in.shape, F32) for _ in range(4)]
    out_outs = [lax.empty(w_out.shape, F32) for _ in range(4)]
    for l in reversed(range(depth)):
        if l == 0:
            lands[0] = (_chips_wait(pending["in"], [g, in_outs[0], out_outs[0]], "in0"),
                        _scatter_wait(pending["out"], 0, [g], "out0"))
        in_outs = _adamw_slabs(l, w_in, m_w_in, v_w_in, lands[l][0], in_outs, order, tr=256, name="adamw_w_in")
        out_outs = _adamw_slabs(l, w_out, m_w_out, v_w_out, lands[l][1], out_outs, order, tr=64, name="adamw_w_out")

    small_shapes = [(depth, d), (depth, HEAD_DIM), (depth, HEAD_DIM), (depth, rheads), (depth, rheads),
                    (depth, rheads, N_DEV * rns), (d,), (1,)]
    grads_local = [jnp.stack(d_norm), jnp.stack(d_qn), jnp.stack(d_kn), jnp.stack(d_df), jnp.stack(d_db),
                   jnp.stack(d_rn).reshape(depth, rheads, N_DEV * rns), d_final[0], loss_row[0, :1]]
    _, gsum = _exchange_small(_pack(grads_local), name="all_reduce_small", after=(in_outs[0], out_outs[0]))
    g_norm, g_qn, g_kn, g_df, g_db, g_rn_full, g_final, loss = _unpack(gsum, small_shapes)
    g_rn = lax.dynamic_slice_in_dim(g_rn_full, me * rns, rns, axis=2)
    small_g = [g_norm, g_qn, g_kn, g_df, g_db, g_rn, g_final]
    small_w = [norm_w, q_norm, k_norm, ret_decay_fwd, ret_decay_bwd, ret_norm, final_norm]
    small_m = [m_norm_w, m_q_norm, m_k_norm, m_ret_decay_fwd, m_ret_decay_bwd, m_ret_norm, m_final_norm]
    small_v = [v_norm_w, v_q_norm, v_k_norm, v_ret_decay_fwd, v_ret_decay_bwd, v_ret_norm, v_final_norm]
    shapes = [a.shape for a in small_w]
    sd, sm, sv2 = _adamw_small(_pack(small_w), _pack(small_g), _pack(small_m), _pack(small_v))
    small_d, small_m2, small_v2 = _unpack(sd, shapes), _unpack(sm, shapes), _unpack(sv2, shapes)

    def ordered(small, win_v, wout_v):
        return [small[0], win_v, small[1], small[2], small[3], small[4], small[5], wout_v, small[6]]

    grads = ordered(small_g, in_outs[0], out_outs[0])
    deltas = ordered(small_d, in_outs[1], out_outs[1])
    new_m = ordered(small_m2, in_outs[2], out_outs[2])
    new_v = ordered(small_v2, in_outs[3], out_outs[3])
    return (loss.reshape(()), g[None], *grads, *deltas, *new_m, *new_v)
```

```python
import functools
import math

import jax
import jax.numpy as jnp
import numpy as np
from jax import lax
from jax.experimental import pallas as pl
from jax.experimental.pallas import tpu as pltpu

F32 = jnp.float32
BF16 = jnp.bfloat16

N_DEV = 8
HEAD_DIM = 128
ATTN_GROUP = 4
RET_QK_DIM = 128
RET_V_DIM = 256
GRID_W = 64
ROPE_THETA = 10000.0
EPS = 1e-6
ADAM_LR = 0.001
ADAM_B1 = 0.9
ADAM_B2 = 0.999
ADAM_EPS = 1e-08
ADAM_WD = 0.01
ADAM_STEP = 10
ADAM_C1 = 1.0 - ADAM_B1 ** ADAM_STEP
ADAM_C2 = 1.0 - ADAM_B2 ** ADAM_STEP
LANES = 128
SMALL_ROWS_ALIGN = 8
VMEM_LIMIT = 56 * 1024 * 1024

NT_DIMS = (((1,), (1,)), ((), ()))
TN_DIMS = (((0,), (0,)), ((), ()))
MESH = pl.DeviceIdType.MESH


def _params(sem):
    return pltpu.CompilerParams(dimension_semantics=sem, vmem_limit_bytes=VMEM_LIMIT)


def _tile(dim, pref, align=16):
    if dim <= pref:
        return dim
    for t in range(pref - pref % align, 0, -align):
        if dim % t == 0:
            return t
    raise ValueError((dim, pref, align))


def _silu_parts(z):
    sg = 1.0 / (1.0 + jnp.exp(-z))
    return z * sg, sg * (1.0 + z * (1.0 - sg))


def _log_sigmoid(x):
    return jnp.minimum(x, 0.0) - jnp.log(1.0 + jnp.exp(-jnp.abs(x)))


def _swap_pairs(z):
    src = lax.broadcasted_iota(jnp.int32, (HEAD_DIM, HEAD_DIM), 0)
    dst = lax.broadcasted_iota(jnp.int32, (HEAD_DIM, HEAD_DIM), 1)
    partner = jnp.where((dst % 64) < 32, dst + 32, dst - 32)
    perm = (src == partner).astype(F32)
    return jnp.dot(z, perm, precision=lax.Precision.HIGH, preferred_element_type=F32)


def _rope(z, cos, sin):
    return z * cos + _swap_pairs(z) * sin


def _rope_transposed(d, cos, sin):
    return d * cos + _swap_pairs(d * sin)


def _rope_tables(seq):
    rows = seq // GRID_W
    row = jnp.repeat(jnp.arange(rows), GRID_W).astype(F32)
    col = jnp.tile(jnp.arange(GRID_W), rows).astype(F32)
    axis_dim = HEAD_DIM // 2
    inv = ROPE_THETA ** (-jnp.arange(0, axis_dim, 2, dtype=F32) / axis_dim)
    ar = row[:, None] * inv[None, :]
    ac = col[:, None] * inv[None, :]
    cos = jnp.concatenate([jnp.cos(ar), jnp.cos(ar), jnp.cos(ac), jnp.cos(ac)], axis=-1)
    sin = jnp.concatenate([-jnp.sin(ar), jnp.sin(ar), -jnp.sin(ac), jnp.sin(ac)], axis=-1)
    return cos, sin


def _matmul(a, b, *, name, trans_b=False, out_dtype=F32, residual=None, tm=1024, tn=512, tk=4096, after=()):
    m, k = a.shape
    n = b.shape[0] if trans_b else b.shape[1]
    tm, tn, tk = _tile(m, tm), _tile(n, tn, LANES), _tile(k, tk, LANES)
    nk = k // tk
    has_res = residual is not None

    def body(*refs):
        a_ref, b_ref = refs[:2]
        r_ref = refs[2] if has_res else None
        o_ref = refs[2 + has_res + len(after)]
        if trans_b:
            part = lax.dot_general(a_ref[...], b_ref[...], NT_DIMS, preferred_element_type=F32)
        else:
            part = jnp.dot(a_ref[...], b_ref[...], preferred_element_type=F32)

        def finish(r):
            if has_res:
                r = r + r_ref[...]
            o_ref[...] = r.astype(o_ref.dtype)

        if nk == 1:
            finish(part)
        else:
            acc_ref = refs[-1]
            kk = pl.program_id(2)

            @pl.when(kk == 0)
            def _():
                acc_ref[...] = part

            @pl.when(kk > 0)
            def _():
                acc_ref[...] += part

            @pl.when(kk == nk - 1)
            def _():
                finish(acc_ref[...])

    if trans_b:
        b_spec = pl.BlockSpec((tn, tk), lambda i, j, kk: (j, kk))
    else:
        b_spec = pl.BlockSpec((tk, tn), lambda i, j, kk: (kk, j))
    in_specs = [pl.BlockSpec((tm, tk), lambda i, j, kk: (i, kk)), b_spec]
    args = [a, b]
    if has_res:
        in_specs.append(pl.BlockSpec((tm, tn), lambda i, j, kk: (i, j)))
        args.append(residual)
    in_specs += [ANY_SPEC] * len(after)
    args += list(after)
    return pl.pallas_call(
        body, name=name, grid=(m // tm, n // tn, nk),
        in_specs=in_specs,
        out_specs=pl.BlockSpec((tm, tn), lambda i, j, kk: (i, j)),
        out_shape=jax.ShapeDtypeStruct((m, n), out_dtype),
        scratch_shapes=[pltpu.VMEM((tm, tn), F32)] if nk > 1 else [],
        compiler_params=_params(("parallel", "parallel", "arbitrary")),
    )(*args)


def _proj_sections(a, b, widths, dtypes, *, tm=1024, tn=512):
    m, k = a.shape
    tm = _tile(m, tm)
    tn = _tile(min(widths), tn, LANES)
    assert all(w % tn == 0 for w in widths) and sum(widths) == b.shape[1]
    nblk = [w // tn for w in widths]
    first = [int(o) // tn for o in np.cumsum((0,) + tuple(widths))[:-1]]

    def body(a_ref, b_ref, *out_refs):
        j = pl.program_id(1)
        part = jnp.dot(a_ref[...], b_ref[...], preferred_element_type=F32)
        for o_ref, lo, n in zip(out_refs, first, nblk):
            @pl.when(jnp.logical_and(j >= lo, j < lo + n))
            def _(o_ref=o_ref):
                o_ref[...] = part.astype(o_ref.dtype)

    out_specs = [pl.BlockSpec((tm, tn), lambda i, j, lo=lo, n=n: (i, jnp.clip(j - lo, 0, n - 1)))
                 for lo, n in zip(first, nblk)]
    return pl.pallas_call(
        body, name="proj", grid=(m // tm, b.shape[1] // tn),
        in_specs=[pl.BlockSpec((tm, k), lambda i, j: (i, 0)), pl.BlockSpec((k, tn), lambda i, j: (0, j))],
        out_specs=out_specs,
        out_shape=[jax.ShapeDtypeStruct((m, w), dt) for w, dt in zip(widths, dtypes)],
        compiler_params=_params(("arbitrary", "arbitrary")),
    )(a, b)


def _rms_fwd(x, w, *, ts=256):
    s, d = x.shape
    ts = _tile(s, ts)

    def body(x_ref, w_ref, h_ref, ht_ref):
        xv = x_ref[...]
        r = lax.rsqrt(jnp.mean(xv * xv, axis=-1, keepdims=True) + EPS)
        h = xv * r * w_ref[...]
        h_ref[...] = h.astype(BF16)
        ht_ref[...] = h.T.astype(BF16)

    row = pl.BlockSpec((ts, d), lambda i: (i, 0))
    return pl.pallas_call(
        body, name="rms_fwd", grid=(s // ts,),
        in_specs=[row, pl.BlockSpec((1, d), lambda i: (0, 0))],
        out_specs=[row, pl.BlockSpec((d, ts), lambda i: (0, i))],
        out_shape=[jax.ShapeDtypeStruct((s, d), BF16), jax.ShapeDtypeStruct((d, s), BF16)],
        compiler_params=_params(("parallel",)),
    )(x, w.reshape(1, d))


def _rms_bwd(dh, x, g, w, *, ts=256):
    s, d = x.shape
    ts = _tile(s, ts)

    def body(dh_ref, x_ref, g_ref, w_ref, dx_ref, dxb_ref, dw_ref):
        xv = x_ref[...]
        r = lax.rsqrt(jnp.mean(xv * xv, axis=-1, keepdims=True) + EPS)
        xh = xv * r
        dhv = dh_ref[...]
        dn = dhv * w_ref[...]
        dx = g_ref[...] + r * (dn - xh * jnp.mean(dn * xh, axis=-1, keepdims=True))
        dx_ref[...] = dx
        dxb_ref[...] = dx.astype(BF16)
        part = jnp.sum(dhv * xh, axis=0, keepdims=True)

        @pl.when(pl.program_id(0) == 0)
        def _():
            dw_ref[...] = part

        @pl.when(pl.program_id(0) > 0)
        def _():
            dw_ref[...] += part

    row = pl.BlockSpec((ts, d), lambda i: (i, 0))
    vec = pl.BlockSpec((1, d), lambda i: (0, 0))
    return pl.pallas_call(
        body, name="rms_bwd", grid=(s // ts,),
        in_specs=[row, row, row, vec],
        out_specs=[row, row, vec],
        out_shape=[jax.ShapeDtypeStruct((s, d), F32), jax.ShapeDtypeStruct((s, d), BF16),
                   jax.ShapeDtypeStruct((1, d), F32)],
        compiler_params=_params(("arbitrary",)),
    )(dh, x, g, w.reshape(1, d))


def _loss_head(x, target, w, *, ts=256):
    s, d = x.shape
    ts = _tile(s, ts)

    def body(x_ref, t_ref, w_ref, loss_ref, dx_ref, dxb_ref, dw_ref):
        xv = x_ref[...]
        r = lax.rsqrt(jnp.mean(xv * xv, axis=-1, keepdims=True) + EPS)
        xh = xv * r
        wv = w_ref[...]
        diff = xh * wv - t_ref[...]
        lpart = 0.5 * jnp.sum(jnp.mean(diff * diff, axis=-1, keepdims=True), axis=0, keepdims=True)
        dout = diff * (1.0 / d)
        dn = dout * wv
        dx = r * (dn - xh * jnp.mean(dn * xh, axis=-1, keepdims=True))
        dx_ref[...] = dx
        dxb_ref[...] = dx.astype(BF16)
        part = jnp.sum(dout * xh, axis=0, keepdims=True)
        lrow = jnp.broadcast_to(lpart, loss_ref.shape)

        @pl.when(pl.program_id(0) == 0)
        def _():
            dw_ref[...] = part
            loss_ref[...] = lrow

        @pl.when(pl.program_id(0) > 0)
        def _():
            dw_ref[...] += part
            loss_ref[...] += lrow

    row = pl.BlockSpec((ts, d), lambda i: (i, 0))
    vec = pl.BlockSpec((1, d), lambda i: (0, 0))
    return pl.pallas_call(
        body, name="loss_head", grid=(s // ts,),
        in_specs=[row, row, vec],
        out_specs=[pl.BlockSpec((1, LANES), lambda i: (0, 0)), row, row, vec],
        out_shape=[jax.ShapeDtypeStruct((1, LANES), F32), jax.ShapeDtypeStruct((s, d), F32),
                   jax.ShapeDtypeStruct((s, d), BF16), jax.ShapeDtypeStruct((1, d), F32)],
        compiler_params=_params(("arbitrary",)),
    )(x, target, w.reshape(1, d))


def _prep_fwd(aq, ak, rq, rk, cos, sin, qw, kw, *, ts=256):
    s = aq.shape[0]
    ts = _tile(s, ts)
    attn_scale = HEAD_DIM ** -0.5
    ret_scale = RET_QK_DIM ** -0.5
    nq, nk, nr = aq.shape[1] // HEAD_DIM, ak.shape[1] // HEAD_DIM, rq.shape[1] // RET_QK_DIM

    def body(aq_ref, ak_ref, rq_ref, rk_ref, cos_ref, sin_ref, qw_ref, kw_ref,
             q_out, k_out, rq_out, rk_out):
        c, sn = cos_ref[...], sin_ref[...]

        def normed(u, w):
            return u * lax.rsqrt(jnp.mean(u * u, axis=-1, keepdims=True) + EPS) * w

        for j in range(nq):
            sl = slice(j * HEAD_DIM, (j + 1) * HEAD_DIM)
            q_out[:, sl] = (_rope(normed(aq_ref[:, sl], qw_ref[...]), c, sn) * attn_scale).astype(BF16)
        for j in range(nk):
            sl = slice(j * HEAD_DIM, (j + 1) * HEAD_DIM)
            k_out[:, sl] = _rope(normed(ak_ref[:, sl], kw_ref[...]), c, sn).astype(BF16)
        for j in range(nr):
            sl = slice(j * RET_QK_DIM, (j + 1) * RET_QK_DIM)
            rq_out[:, sl] = _rope(rq_ref[:, sl], c, sn).astype(BF16)
            rk_out[:, sl] = (_rope(rk_ref[:, sl], c, sn) * ret_scale).astype(BF16)

    def row(arr):
        return pl.BlockSpec((ts, arr.shape[1]), lambda i: (i, 0))

    vec = pl.BlockSpec((1, HEAD_DIM), lambda i: (0, 0))
    ins = [aq, ak, rq, rk]
    return pl.pallas_call(
        body, name="prep_fwd", grid=(s // ts,),
        in_specs=[row(a) for a in ins] + [row(cos), row(sin), vec, vec],
        out_specs=[row(a) for a in ins],
        out_shape=[jax.ShapeDtypeStruct(a.shape, BF16) for a in ins],
        compiler_params=_params(("parallel",)),
    )(*ins, cos, sin, qw.reshape(1, HEAD_DIM), kw.reshape(1, HEAD_DIM))


def _prep_bwd(dq, dk, drq, drk, aq, ak, cos, sin, qw, kw, dav, dag, drv, drg, *, ts=256):
    s = aq.shape[0]
    ts = _tile(s, ts)
    attn_scale = HEAD_DIM ** -0.5
    ret_scale = RET_QK_DIM ** -0.5
    nq, nk, nr = aq.shape[1] // HEAD_DIM, ak.shape[1] // HEAD_DIM, drq.shape[1] // RET_QK_DIM
    widths = (aq.shape[1], ak.shape[1], dav.shape[1], dag.shape[1], drq.shape[1], drk.shape[1], drv.shape[1],
              drg.shape[1])
    o_aq, o_ak, o_av, o_ag, o_rq, o_rk, o_rv, o_rg = (int(o) for o in np.cumsum((0,) + widths)[:-1])

    def body(dq_ref, dk_ref, drq_ref, drk_ref, aq_ref, ak_ref, cos_ref, sin_ref, dav_ref, dag_ref, drv_ref, drg_ref,
             qw_ref, kw_ref, dproj_ref, dqw_ref, dkw_ref):
        c, sn = cos_ref[...], sin_ref[...]
        for ref, off in ((dav_ref, o_av), (dag_ref, o_ag), (drv_ref, o_rv), (drg_ref, o_rg)):
            dproj_ref[:, off:off + ref.shape[1]] = ref[...]

        def unrope(d):
            return _rope_transposed(d, c, sn)

        def norm_bwd(dun, u, w):
            r = lax.rsqrt(jnp.mean(u * u, axis=-1, keepdims=True) + EPS)
            uh = u * r
            dn = dun * w
            du = r * (dn - uh * jnp.mean(dn * uh, axis=-1, keepdims=True))
            return du, jnp.sum(dun * uh, axis=0, keepdims=True)

        dqw = jnp.zeros((1, HEAD_DIM), F32)
        for j in range(nq):
            sl = slice(j * HEAD_DIM, (j + 1) * HEAD_DIM)
            du, dw = norm_bwd(unrope(dq_ref[:, sl] * attn_scale), aq_ref[:, sl], qw_ref[...])
            dproj_ref[:, o_aq + j * HEAD_DIM:o_aq + (j + 1) * HEAD_DIM] = du.astype(BF16)
            dqw = dqw + dw
        dkw = jnp.zeros((1, HEAD_DIM), F32)
        for j in range(nk):
            sl = slice(j * HEAD_DIM, (j + 1) * HEAD_DIM)
            du, dw = norm_bwd(unrope(dk_ref[:, sl]), ak_ref[:, sl], kw_ref[...])
            dproj_ref[:, o_ak + j * HEAD_DIM:o_ak + (j + 1) * HEAD_DIM] = du.astype(BF16)
            dkw = dkw + dw
        for j in range(nr):
            sl = slice(j * RET_QK_DIM, (j + 1) * RET_QK_DIM)
            dproj_ref[:, o_rq + j * RET_QK_DIM:o_rq + (j + 1) * RET_QK_DIM] = unrope(drq_ref[:, sl]).astype(BF16)
            dproj_ref[:, o_rk + j * RET_QK_DIM:o_rk + (j + 1) * RET_QK_DIM] = (
                unrope(drk_ref[:, sl] * ret_scale).astype(BF16))

        @pl.when(pl.program_id(0) == 0)
        def _():
            dqw_ref[...] = dqw
            dkw_ref[...] = dkw

        @pl.when(pl.program_id(0) > 0)
        def _():
            dqw_ref[...] += dqw
            dkw_ref[...] += dkw

    def row(arr):
        return pl.BlockSpec((ts, arr.shape[1]), lambda i: (i, 0))

    vec = pl.BlockSpec((1, HEAD_DIM), lambda i: (0, 0))
    ins = [dq, dk, drq, drk, aq, ak, cos, sin, dav, dag, drv, drg]
    total = sum(widths)
    return pl.pallas_call(
        body, name="prep_bwd", grid=(s // ts,),
        in_specs=[row(a) for a in ins] + [vec, vec],
        out_specs=[pl.BlockSpec((ts, total), lambda i: (i, 0)), vec, vec],
        out_shape=[jax.ShapeDtypeStruct((s, total), BF16)] + [jax.ShapeDtypeStruct((1, HEAD_DIM), F32)] * 2,
        compiler_params=_params(("arbitrary",)),
    )(*ins, qw.reshape(1, HEAD_DIM), kw.reshape(1, HEAD_DIM))


def _attn_fwd(q, k, v, *, tq=4096, sub=256):
    s, aw = q.shape
    tq = _tile(s, tq)
    sub = _tile(tq, sub)
    heads, kvh = aw // HEAD_DIM, k.shape[1] // HEAD_DIM
    grp = heads // kvh

    def body(q_ref, k_ref, v_ref, o_ref, lse_ref):
        kv_ = k_ref[...]
        v_ext = jnp.concatenate([v_ref[...], jnp.ones((s, HEAD_DIM), BF16)], axis=-1)
        for r in range(tq // sub):
            rows = slice(r * sub, (r + 1) * sub)
            sc = lax.dot_general(q_ref[rows, :], kv_, NT_DIMS, preferred_element_type=F32)
            m = jnp.max(sc, axis=-1, keepdims=True)
            p = jnp.exp((sc - m).astype(BF16))
            oe = jnp.dot(p, v_ext, preferred_element_type=F32)
            l = oe[:, HEAD_DIM:HEAD_DIM + 1]
            o_ref[rows, :] = (oe[:, :HEAD_DIM] / l).astype(o_ref.dtype)
            lse_ref[rows, :] = jnp.broadcast_to(m + jnp.log(l), (sub, HEAD_DIM))

    qspec = pl.BlockSpec((tq, HEAD_DIM), lambda kv, g, i: (i, kv * grp + g))
    kspec = pl.BlockSpec((s, HEAD_DIM), lambda kv, g, i: (0, kv))
    return pl.pallas_call(
        body, name="attn_fwd", grid=(kvh, grp, s // tq),
        in_specs=[qspec, kspec, kspec],
        out_specs=[qspec, qspec],
        out_shape=[jax.ShapeDtypeStruct((s, aw), BF16), jax.ShapeDtypeStruct((s, aw), F32)],
        compiler_params=_params(("parallel", "parallel", "parallel")),
    )(q, k, v)


def _attn_bwd(q, k, v, o, do, lse, *, tq=1024, sub=256):
    s, aw = q.shape
    tq = _tile(s, tq)
    sub = _tile(tq, sub)
    heads, kvh = aw // HEAD_DIM, k.shape[1] // HEAD_DIM
    grp = heads // kvh
    nq = s // tq

    def body(q_ref, k_ref, v_ref, o_ref, do_ref, lse_ref, dq_ref, dk_ref, dv_ref, dk_acc, dv_acc, p_scr, ds_scr):
        g, i = pl.program_id(1), pl.program_id(2)
        kv_, vv = k_ref[...], v_ref[...]
        for r in range(tq // sub):
            rows = slice(r * sub, (r + 1) * sub)
            qv, dov = q_ref[rows, :], do_ref[rows, :]
            sc = lax.dot_general(qv, kv_, NT_DIMS, preferred_element_type=F32)
            p = jnp.exp((sc - lse_ref[rows, :1]).astype(BF16))
            dp = lax.dot_general(dov, vv, NT_DIMS, preferred_element_type=F32)
            delta = jnp.sum(dov.astype(F32) * o_ref[rows, :].astype(F32), axis=-1, keepdims=True)
            ds = p * (dp - delta).astype(BF16)
            dq_ref[rows, :] = jnp.dot(ds, kv_, preferred_element_type=F32)
            p_scr[rows, :] = p
            ds_scr[rows, :] = ds
        dvp = lax.dot_general(p_scr[...], do_ref[...], TN_DIMS, preferred_element_type=F32)
        dkp = lax.dot_general(ds_scr[...], q_ref[...], TN_DIMS, preferred_element_type=F32)
        first = jnp.logical_and(g == 0, i == 0)

        @pl.when(first)
        def _():
            dv_acc[...] = dvp
            dk_acc[...] = dkp

        @pl.when(jnp.logical_not(first))
        def _():
            dv_acc[...] += dvp
            dk_acc[...] += dkp

        @pl.when(jnp.logical_and(g == grp - 1, i == nq - 1))
        def _():
            dk_ref[...] = dk_acc[...]
            dv_ref[...] = dv_acc[...].astype(dv_ref.dtype)

    qspec = pl.BlockSpec((tq, HEAD_DIM), lambda kv, g, i: (i, kv * grp + g))
    kspec = pl.BlockSpec((s, HEAD_DIM), lambda kv, g, i: (0, kv))
    return pl.pallas_call(
        body, name="attn_bwd", grid=(kvh, grp, nq),
        in_specs=[qspec, kspec, kspec, qspec, qspec, qspec],
        out_specs=[qspec, kspec, kspec],
        out_shape=[jax.ShapeDtypeStruct((s, aw), F32), jax.ShapeDtypeStruct(k.shape, F32),
                   jax.ShapeDtypeStruct(v.shape, BF16)],
        scratch_shapes=[pltpu.VMEM((s, HEAD_DIM), F32), pltpu.VMEM((s, HEAD_DIM), F32),
                        pltpu.VMEM((tq, s), BF16), pltpu.VMEM((tq, s), BF16)],
        compiler_params=_params(("parallel", "arbitrary", "arbitrary")),
    )(q, k, v, o, do, lse)


def _sum_all(z):
    return jnp.sum(jnp.sum(z, axis=0, keepdims=True), axis=1, keepdims=True)


def _chunk_consts(df_ref, db_ref, t):
    lf = _log_sigmoid(df_ref[0][:, :1])
    lb = _log_sigmoid(db_ref[0][:, :1])
    r = lax.broadcasted_iota(jnp.int32, (t, 1), 0).astype(F32)
    c = lax.broadcasted_iota(jnp.int32, (1, t), 1).astype(F32)
    diff = r - c
    dm = jnp.exp(diff * jnp.where(diff >= 0, lf, -lb))
    return dict(diff=diff, dm=dm, r=r,
                af=jnp.exp(lf * (r + 1.0)), bf=jnp.exp(lf * (t - 1.0 - r)), gf=jnp.exp(lf * t),
                ab=jnp.exp(lb * (t - r)), bb=jnp.exp(lb * r), gb=jnp.exp(lb * t))


def _scaled(x, f):
    return (x.astype(F32) * f).astype(BF16)


def _retc_specs(s):
    qspec = pl.BlockSpec((s, RET_QK_DIM), lambda h: (0, h))
    vspec = pl.BlockSpec((s, RET_V_DIM), lambda h: (0, h))
    dspec = pl.BlockSpec((1, 1, LANES), lambda h: (h, 0, 0))
    return qspec, vspec, dspec


def _retc_fwd(q, k, v, dec_f, dec_b, *, t=256):
    s, qw = q.shape
    t = _tile(s, t)
    heads, nc = qw // RET_QK_DIM, s // t
    qspec, vspec, dspec = _retc_specs(s)

    def body(q_ref, k_ref, v_ref, df_ref, db_ref, o_ref):
        cs = _chunk_consts(df_ref, db_ref, t)

        def rows_of(i):
            return pl.ds(pl.multiple_of(i * t, t), t)

        def forward(i, sf):
            rows = rows_of(i)
            qi, ki, vi = q_ref[rows, :], k_ref[rows, :], v_ref[rows, :]
            sc = lax.dot_general(qi, ki, NT_DIMS, preferred_element_type=F32)
            intra = jnp.dot((sc * cs["dm"]).astype(BF16), vi, preferred_element_type=F32)
            cross = jnp.dot(_scaled(qi, cs["af"]), sf.astype(BF16), preferred_element_type=F32)
            o_ref[rows, :] = intra + cross
            return cs["gf"] * sf + lax.dot_general(_scaled(ki, cs["bf"]), vi, TN_DIMS, preferred_element_type=F32)

        def backward(j, sb):
            rows = rows_of(nc - 1 - j)
            qi, ki, vi = q_ref[rows, :], k_ref[rows, :], v_ref[rows, :]
            o_ref[rows, :] += jnp.dot(_scaled(qi, cs["ab"]), sb.astype(BF16), preferred_element_type=F32)
            return cs["gb"] * sb + lax.dot_general(_scaled(ki, cs["bb"]), vi, TN_DIMS, preferred_element_type=F32)

        zero = jnp.zeros((RET_QK_DIM, RET_V_DIM), F32)
        lax.fori_loop(0, nc, forward, zero, unroll=True)
        lax.fori_loop(0, nc, backward, zero, unroll=True)

    return pl.pallas_call(
        body, name="ret_fwd", grid=(heads,),
        in_specs=[qspec, qspec, vspec, dspec, dspec],
        out_specs=vspec, out_shape=jax.ShapeDtypeStruct(v.shape, F32),
        compiler_params=_params(("parallel",)),
    )(q, k, v, dec_f, dec_b)


def _retc_bwd(q, k, v, do, dec_f, dec_b, *, t=256):
    s, qw = q.shape
    t = _tile(s, t)
    heads, nc = qw // RET_QK_DIM, s // t
    qspec, vspec, dspec = _retc_specs(s)
    gspec = pl.BlockSpec((1, 8, LANES), lambda h: (h, 0, 0))

    def body(q_ref, k_ref, v_ref, do_ref, df_ref, db_ref, dq_ref, dk_ref, dv_ref, gf_ref, gb_ref,
             sf_scr, sb_scr, dv_acc):
        cs = _chunk_consts(df_ref, db_ref, t)
        r, diff, dm = cs["r"], cs["diff"], cs["dm"]

        def rows_of(i):
            return pl.ds(pl.multiple_of(i * t, t), t)

        def tn(a, b):
            return lax.dot_general(a, b, TN_DIMS, preferred_element_type=F32)

        def nt(a, b):
            return lax.dot_general(a, b, NT_DIMS, preferred_element_type=F32)

        def states_f(i, sf):
            sf_scr[i] = sf
            rows = rows_of(i)
            return cs["gf"] * sf + tn(_scaled(k_ref[rows, :], cs["bf"]), v_ref[rows, :])

        def states_b(j, sb):
            i = nc - 1 - j
            sb_scr[i] = sb
            rows = rows_of(i)
            return cs["gb"] * sb + tn(_scaled(k_ref[rows, :], cs["bb"]), v_ref[rows, :])

        zero = jnp.zeros((RET_QK_DIM, RET_V_DIM), F32)
        lax.fori_loop(0, nc, states_f, zero, unroll=True)
        lax.fori_loop(0, nc, states_b, zero, unroll=True)

        def scan_grads(i, state, u, qf, kf, vi, doi, fa, fb, step, wa, wb):
            qa, kb = qf * fa, kf * fb
            ub = u.astype(BF16)
            dqa = nt(doi, state.astype(BF16))
            dkb = nt(vi, ub)
            dv = jnp.dot(kb.astype(BF16), ub, preferred_element_type=F32)
            dlog = _sum_all(dqa * qa * wa) + _sum_all(dkb * kb * wb) + t * step * _sum_all(u * state)
            u_new = step * u + tn(qa.astype(BF16), doi)
            return dqa * fa, dkb * fb, dv, u_new, dlog

        def sweep_f(j, carry):
            u, accf, accb = carry
            i = nc - 1 - j
            rows = rows_of(i)
            qi, ki, vi, doi = q_ref[rows, :], k_ref[rows, :], v_ref[rows, :], do_ref[rows, :]
            sc = nt(qi, ki)
            p = sc * dm
            dp = nt(doi, vi)
            ds = (dp * dm).astype(BF16)
            tt = dp * p * diff
            accf = accf + _sum_all(jnp.where(diff > 0, tt, 0.0))
            accb = accb + _sum_all(jnp.where(diff < 0, -tt, 0.0))
            dq1, dk1, dv1, u, dlog = scan_grads(i, sf_scr[i], u, qi.astype(F32), ki.astype(F32), vi, doi,
                                                cs["af"], cs["bf"], cs["gf"], r + 1.0, t - 1.0 - r)
            dq_ref[rows, :] = jnp.dot(ds, ki, preferred_element_type=F32) + dq1
            dk_ref[rows, :] = tn(ds, qi) + dk1
            dv_acc[rows, :] = tn(p.astype(BF16), doi) + dv1
            return u, accf + dlog, accb

        def sweep_b(i, carry):
            w, accb = carry
            rows = rows_of(i)
            qi, ki, vi, doi = q_ref[rows, :], k_ref[rows, :], v_ref[rows, :], do_ref[rows, :]
            dq1, dk1, dv1, w, dlog = scan_grads(i, sb_scr[i], w, qi.astype(F32), ki.astype(F32), vi, doi,
                                                cs["ab"], cs["bb"], cs["gb"], t - r, r)
            dq_ref[rows, :] += dq1
            dk_ref[rows, :] += dk1
            dv_acc[rows, :] += dv1
            return w, accb + dlog

        z11 = jnp.zeros((1, 1), F32)
        _, accf, accb = lax.fori_loop(0, nc, sweep_f, (zero, z11, z11), unroll=4)
        _, accb = lax.fori_loop(0, nc, sweep_b, (zero, accb), unroll=4)
        dv_ref[...] = dv_acc[...].astype(dv_ref.dtype)
        gf_ref[...] = jnp.broadcast_to((accf / (1.0 + jnp.exp(df_ref[0][:, :1]))).reshape(1, 1, 1), gf_ref.shape)
        gb_ref[...] = jnp.broadcast_to((accb / (1.0 + jnp.exp(db_ref[0][:, :1]))).reshape(1, 1, 1), gb_ref.shape)

    return pl.pallas_call(
        body, name="ret_bwd", grid=(heads,),
        in_specs=[qspec, qspec, vspec, vspec, dspec, dspec],
        out_specs=[qspec, qspec, vspec, gspec, gspec],
        out_shape=[jax.ShapeDtypeStruct(q.shape, F32), jax.ShapeDtypeStruct(k.shape, F32),
                   jax.ShapeDtypeStruct(v.shape, BF16),
                   jax.ShapeDtypeStruct((heads, 8, LANES), F32), jax.ShapeDtypeStruct((heads, 8, LANES), F32)],
        scratch_shapes=[pltpu.VMEM((nc, RET_QK_DIM, RET_V_DIM), F32), pltpu.VMEM((nc, RET_QK_DIM, RET_V_DIM), F32),
                        pltpu.VMEM((s, RET_V_DIM), F32)],
        compiler_params=_params(("parallel",)),
    )(q, k, v, do, dec_f, dec_b)


def _gate_fwd(att, ag, ret, rg, rnw, *, ts=256):
    s, aw = att.shape
    rw = ret.shape[1]
    ts = _tile(s, ts)
    rheads = rw // RET_V_DIM

    def body(att_ref, ag_ref, ret_ref, rg_ref, w_ref, y_ref, yt_ref):
        def put(lo, hi, val):
            y_ref[:, lo:hi] = val.astype(BF16)
            yt_ref[lo:hi, :] = val.T.astype(BF16)

        sa, _ = _silu_parts(ag_ref[...])
        put(0, aw, sa * att_ref[...].astype(F32))
        for h in range(rheads):
            sl = slice(h * RET_V_DIM, (h + 1) * RET_V_DIM)
            rt = ret_ref[:, sl]
            rn = rt * lax.rsqrt(jnp.mean(rt * rt, axis=-1, keepdims=True) + EPS) * w_ref[:, sl]
            sr, _ = _silu_parts(rg_ref[:, sl])
            put(aw + h * RET_V_DIM, aw + (h + 1) * RET_V_DIM, sr * rn)

    def row(w):
        return pl.BlockSpec((ts, w), lambda i: (i, 0))

    return pl.pallas_call(
        body, name="gate_fwd", grid=(s // ts,),
        in_specs=[row(aw), row(aw), row(rw), row(rw), pl.BlockSpec((1, rw), lambda i: (0, 0))],
        out_specs=[row(aw + rw), pl.BlockSpec((aw + rw, ts), lambda i: (0, i))],
        out_shape=[jax.ShapeDtypeStruct((s, aw + rw), BF16), jax.ShapeDtypeStruct((aw + rw, s), BF16)],
        compiler_params=_params(("parallel",)),
    )(att, ag, ret, rg, rnw.reshape(1, rw))


def _gate_bwd(dy, att, ag, ret, rg, rnw, *, ts=256):
    s, aw = att.shape
    rw = ret.shape[1]
    ts = _tile(s, ts)
    rheads = rw // RET_V_DIM

    def body(dy_ref, att_ref, ag_ref, ret_ref, rg_ref, w_ref, datt_ref, dag_ref, dret_ref, drg_ref, dw_ref):
        sa, dsa = _silu_parts(ag_ref[...])
        dya = dy_ref[:, :aw]
        datt_ref[...] = (dya * sa).astype(BF16)
        dag_ref[...] = (dya * att_ref[...].astype(F32) * dsa).astype(BF16)
        parts = []
        for h in range(rheads):
            sl = slice(h * RET_V_DIM, (h + 1) * RET_V_DIM)
            rt = ret_ref[:, sl]
            rr = lax.rsqrt(jnp.mean(rt * rt, axis=-1, keepdims=True) + EPS)
            rh = rt * rr
            wv = w_ref[:, sl]
            sr, dsr = _silu_parts(rg_ref[:, sl])
            dyr = dy_ref[:, aw + h * RET_V_DIM:aw + (h + 1) * RET_V_DIM]
            drg_ref[:, sl] = (dyr * rh * wv * dsr).astype(BF16)
            drn = dyr * sr
            dn = drn * wv
            dret_ref[:, sl] = (rr * (dn - rh * jnp.mean(dn * rh, axis=-1, keepdims=True))).astype(BF16)
            parts.append(jnp.sum(drn * rh, axis=0, keepdims=True))
        part = jnp.concatenate(parts, axis=-1)

        @pl.when(pl.program_id(0) == 0)
        def _():
            dw_ref[...] = part

        @pl.when(pl.program_id(0) > 0)
        def _():
            dw_ref[...] += part

    def row(w):
        return pl.BlockSpec((ts, w), lambda i: (i, 0))

    vec = pl.BlockSpec((1, rw), lambda i: (0, 0))
    return pl.pallas_call(
        body, name="gate_bwd", grid=(s // ts,),
        in_specs=[row(aw + rw), row(aw), row(aw), row(rw), row(rw), vec],
        out_specs=[row(aw), row(aw), row(rw), row(rw), vec],
        out_shape=[jax.ShapeDtypeStruct((s, aw), BF16), jax.ShapeDtypeStruct((s, aw), BF16),
                   jax.ShapeDtypeStruct((s, rw), BF16), jax.ShapeDtypeStruct((s, rw), BF16),
                   jax.ShapeDtypeStruct((1, rw), F32)],
        compiler_params=_params(("arbitrary",)),
    )(dy, att, ag, ret, rg, rnw.reshape(1, rw))


def _mesh_position():
    x, y, c = lax.axis_index("x"), lax.axis_index("y"), lax.axis_index("c")
    return x, y, c, 4 * x + 2 * y + c


def _peer(x, y, c, k):
    px = 1 - x if k & 4 else x
    py = 1 - y if k & 2 else y
    pc = 1 - c if k & 1 else c
    return (px, py, pc), 4 * px + 2 * py + pc


HBM_SPEC = pl.BlockSpec(memory_space=pltpu.HBM)
SEM_SPEC = pl.BlockSpec(memory_space=pltpu.SEMAPHORE)
ANY_SPEC = pl.BlockSpec(memory_space=pl.ANY)
DATAFLOW = pltpu.SideEffectType.DATAFLOW_SIDE_EFFECTING


def _hbm(a):
    return pltpu.with_memory_space_constraint(a, pltpu.HBM)


def _split_start(name, copies, n, src, land, after):
    def body(*refs):
        (send_sems, recv_sems), token = refs[2 + len(after):4 + len(after)], refs[-1]
        sends, _ = copies(refs[0], refs[1], send_sems, recv_sems)
        for cp in sends:
            cp.start()
        token[...] = jnp.zeros_like(token)

    return pl.pallas_call(
        body, name=name,
        out_shape=(pltpu.SemaphoreType.DMA((n,)), pltpu.SemaphoreType.DMA((n,)),
                   pltpu.HBM(src.shape, src.dtype), pltpu.HBM(land.shape, land.dtype),
                   jax.ShapeDtypeStruct((8, LANES), F32)),
        in_specs=[HBM_SPEC] * 2 + [ANY_SPEC] * len(after),
        out_specs=(SEM_SPEC, SEM_SPEC, HBM_SPEC, HBM_SPEC, pl.BlockSpec(memory_space=pltpu.VMEM)),
        input_output_aliases={0: 2, 1: 3},
        compiler_params=pltpu.CompilerParams(has_side_effects=DATAFLOW),
    )(_hbm(src), _hbm(land), *after)


def _split_wait(name, copies, started, after):
    send_sems, recv_sems, src, land = started[:4]

    def body(*refs):
        sends, recvs = copies(refs[0], refs[1], refs[2], refs[3])
        for cp in sends:
            cp.wait_send()
        for cp in recvs:
            cp.wait_recv()

    return pl.pallas_call(
        body, name=name,
        out_shape=(pltpu.HBM(src.shape, src.dtype), pltpu.HBM(land.shape, land.dtype)),
        in_specs=[HBM_SPEC] * 2 + [SEM_SPEC, SEM_SPEC] + [ANY_SPEC] * len(after),
        out_specs=(HBM_SPEC,) * 2,
        input_output_aliases={0: 0, 1: 1},
        compiler_params=pltpu.CompilerParams(has_side_effects=DATAFLOW),
    )(src, land, send_sems, recv_sems, *after)


def _slab(ref, p, size, axis):
    if axis == 1:
        return ref.at[:, pl.ds(pl.multiple_of(p * size, LANES), size)]
    return ref.at[pl.ds(pl.multiple_of(p * size, 16), size), :]


ALL_PEERS = tuple(range(1, N_DEV))
SIBLING = 1
SAME_CORE_OF_CHIPS = (2, 4, 6)


def _gather_copies(size, axis, ks):
    def copies(shard_ref, full_ref, send_sems, recv_sems):
        x, y, c, me = _mesh_position()
        sends, recvs = [], []
        for j, k in enumerate(ks):
            peer, pid = _peer(x, y, c, k)
            sends.append(pltpu.make_async_remote_copy(
                src_ref=shard_ref, dst_ref=_slab(full_ref, me, size, axis), send_sem=send_sems.at[j],
                recv_sem=recv_sems.at[j], device_id=peer, device_id_type=MESH))
            recvs.append(pltpu.make_async_remote_copy(
                src_ref=shard_ref, dst_ref=_slab(full_ref, pid, size, axis), send_sem=send_sems.at[j],
                recv_sem=recv_sems.at[j], device_id=peer, device_id_type=MESH))
        return sends, recvs

    return copies


def _pass_on_copies(size, axis):
    def copies(shard_ref, full_ref, send_sems, recv_sems):
        x, y, c, me = _mesh_position()
        sibling, _ = _peer(x, y, c, SIBLING)
        sends, recvs = [], []
        for j, k in enumerate(SAME_CORE_OF_CHIPS):
            _, landed = _peer(x, y, c, k)
            _, siblings = _peer(x, y, c, k ^ SIBLING)
            mine = _slab(full_ref, landed, size, axis)
            sends.append(pltpu.make_async_remote_copy(
                src_ref=mine, dst_ref=mine, send_sem=send_sems.at[j], recv_sem=recv_sems.at[j],
                device_id=sibling, device_id_type=MESH))
            recvs.append(pltpu.make_async_remote_copy(
                src_ref=mine, dst_ref=_slab(full_ref, siblings, size, axis), send_sem=send_sems.at[j],
                recv_sem=recv_sems.at[j], device_id=sibling, device_id_type=MESH))
        return sends, recvs

    return copies


def _scatter_copies(size, axis):
    def copies(grad_ref, land_ref, send_sems, recv_sems):
        x, y, c, me = _mesh_position()
        sends, recvs = [], []
        for k in range(1, N_DEV):
            peer, pid = _peer(x, y, c, k)
            src = _slab(grad_ref, pid, size, axis)
            sends.append(pltpu.make_async_remote_copy(
                src_ref=src, dst_ref=land_ref.at[me], send_sem=send_sems.at[k - 1], recv_sem=recv_sems.at[k - 1],
                device_id=peer, device_id_type=MESH))
            recvs.append(pltpu.make_async_remote_copy(
                src_ref=src, dst_ref=land_ref.at[pid], send_sem=send_sems.at[k - 1], recv_sem=recv_sems.at[k - 1],
                device_id=peer, device_id_type=MESH))
        return sends, recvs

    return copies


PLACE_BANDS = 8


def _place_own(name, src, out_shape, in_spec, out_spec, steps, me):
    def body(me_ref, src_ref, out_ref):
        out_ref[...] = src_ref[...]

    return pl.pallas_call(
        body, name=name, out_shape=out_shape,
        grid_spec=pltpu.PrefetchScalarGridSpec(num_scalar_prefetch=1, grid=(steps,), in_specs=[in_spec],
                                               out_specs=out_spec),
        compiler_params=_params(("parallel",)),
    )(me.reshape(1).astype(jnp.int32), src)


def _gather_start(shard, axis, ks, me, after, tag):
    rows, cols = shard.shape
    size = shard.shape[axis]
    full_shape = tuple(N_DEV * n if a == axis else n for a, n in enumerate(shard.shape))
    band = rows // PLACE_BANDS
    in_spec = pl.BlockSpec((band, cols), lambda i, me_ref: (i, 0))
    if axis == 1:
        out_spec = pl.BlockSpec((band, cols), lambda i, me_ref: (i, me_ref[0]))
    else:
        out_spec = pl.BlockSpec((band, cols), lambda i, me_ref: (me_ref[0] * PLACE_BANDS + i, 0))
    full = _place_own("place_shard", shard, jax.ShapeDtypeStruct(full_shape, shard.dtype), in_spec, out_spec,
                      PLACE_BANDS, me)
    return _split_start("gather_start_" + tag, _gather_copies(size, axis, ks), len(ks), shard, full, after)


def _gather_wait(started, axis, ks, after, tag):
    size = started[2].shape[axis]
    return _split_wait("gather_wait_" + tag, _gather_copies(size, axis, ks), started, after)[1]


def _pass_on_start(shard, full, axis, after, tag):
    size = shard.shape[axis]
    return _split_start("pass_on_start_" + tag, _pass_on_copies(size, axis), len(SAME_CORE_OF_CHIPS), shard, full, after)


def _pass_on_wait(started, axis, after, tag):
    size = started[2].shape[axis]
    return _split_wait("pass_on_wait_" + tag, _pass_on_copies(size, axis), started, after)[1]


def _scatter_start(grad, axis, me, tag):
    size = grad.shape[axis] // N_DEV
    rows, cols = tuple(size if a == axis else n for a, n in enumerate(grad.shape))
    band = rows // PLACE_BANDS
    if axis == 1:
        in_spec = pl.BlockSpec((band, cols), lambda i, me_ref: (i, me_ref[0]))
    else:
        in_spec = pl.BlockSpec((band, cols), lambda i, me_ref: (me_ref[0] * PLACE_BANDS + i, 0))
    out_spec = pl.BlockSpec((None, band, cols), lambda i, me_ref: (me_ref[0], i, 0))
    land = _place_own("place_slab", grad, jax.ShapeDtypeStruct((N_DEV, rows, cols), grad.dtype), in_spec, out_spec,
                      PLACE_BANDS, me)
    return _split_start("scatter_start_" + tag, _scatter_copies(size, axis), N_DEV - 1, grad, land, [])


def _scatter_wait(started, axis, after, tag):
    size = started[2].shape[axis] // N_DEV
    land = _split_wait("scatter_wait_" + tag, _scatter_copies(size, axis), started, after)[1]
    return [(land, p) for p in range(N_DEV)]


N_CHIPS = N_DEV // 2


def _pair_copies(size, axis):
    def copies(grad_ref, land_ref, send_sems, recv_sems):
        x, y, c, me = _mesh_position()
        sibling, _ = _peer(x, y, c, SIBLING)
        sends, recvs = [], []
        for j in range(N_CHIPS):
            _, owner = _peer(x, y, c, (2 * j) ^ SIBLING)
            for lst in (sends, recvs):
                lst.append(pltpu.make_async_remote_copy(
                    src_ref=_slab(grad_ref, owner, size, axis), dst_ref=land_ref.at[j], send_sem=send_sems.at[j],
                    recv_sem=recv_sems.at[j], device_id=sibling, device_id_type=MESH))
        return sends, recvs

    return copies


def _chips_copies():
    def copies(pair_ref, land_ref, send_sems, recv_sems):
        x, y, c, me = _mesh_position()
        sends, recvs = [], []
        for j in range(1, N_CHIPS):
            owner, _ = _peer(x, y, c, 2 * j)
            for lst in (sends, recvs):
                lst.append(pltpu.make_async_remote_copy(
                    src_ref=pair_ref.at[j], dst_ref=land_ref.at[j], send_sem=send_sems.at[j - 1],
                    recv_sem=recv_sems.at[j - 1], device_id=owner, device_id_type=MESH))
        return sends, recvs

    return copies


def _pair_start(grad, axis, tag):
    size = grad.shape[axis] // N_DEV
    rows, cols = tuple(size if a == axis else n for a, n in enumerate(grad.shape))
    land = lax.empty((N_CHIPS, rows, cols), grad.dtype)
    return _split_start("pair_start_" + tag, _pair_copies(size, axis), N_CHIPS, grad, land, [])


def _pair_sums(started, axis, me, after, tag):
    size = started[2].shape[axis] // N_DEV
    grad, land = _split_wait("pair_wait_" + tag, _pair_copies(size, axis), started, after)
    _, rows, cols = land.shape
    band = rows // PLACE_BANDS
    if axis == 1:
        mine = pl.BlockSpec((band, cols), lambda j, i, me_ref: (i, me_ref[0] ^ (2 * j)))
    else:
        mine = pl.BlockSpec((band, cols), lambda j, i, me_ref: ((me_ref[0] ^ (2 * j)) * PLACE_BANDS + i, 0))
    slot = pl.BlockSpec((None, band, cols), lambda j, i, me_ref: (j, i, 0))

    def body(own_ref, mine_ref, theirs_ref, out_ref):
        out_ref[...] = (mine_ref[...].astype(F32) + theirs_ref[...].astype(F32)).astype(out_ref.dtype)

    return pl.pallas_call(
        body, name="pair_sums", out_shape=jax.ShapeDtypeStruct(land.shape, land.dtype),
        grid_spec=pltpu.PrefetchScalarGridSpec(num_scalar_prefetch=1, grid=(N_CHIPS, PLACE_BANDS),
                                               in_specs=[mine, slot], out_specs=slot),
        compiler_params=_params(("parallel", "parallel")),
    )(me.reshape(1).astype(jnp.int32), grad, land)


def _chips_start(pairs, tag):
    return _split_start("chips_start_" + tag, _chips_copies(), N_CHIPS - 1, pairs, lax.empty(pairs.shape, pairs.dtype), [])


def _chips_wait(started, after, tag):
    pairs, land = _split_wait("chips_wait_" + tag, _chips_copies(), started, after)
    return [(pairs, 0)] + [(land, j) for j in range(1, N_CHIPS)]


def _exchange_small(buf, *, name, after=()):
    r = buf.shape[0]

    def body(*refs):
        buf_ref = refs[0]
        all_ref, sum_ref, send_sems, recv_sems = refs[1 + len(after):]
        x, y, c, me = _mesh_position()
        all_ref[me] = buf_ref[...]
        sends, recvs = [], []
        for k in range(1, N_DEV):
            peer, pid = _peer(x, y, c, k)
            sends.append(pltpu.make_async_remote_copy(
                src_ref=buf_ref, dst_ref=all_ref.at[me], send_sem=send_sems.at[k - 1], recv_sem=recv_sems.at[k - 1],
                device_id=peer, device_id_type=MESH))
            recvs.append(pltpu.make_async_remote_copy(
                src_ref=buf_ref, dst_ref=all_ref.at[pid], send_sem=send_sems.at[k - 1], recv_sem=recv_sems.at[k - 1],
                device_id=peer, device_id_type=MESH))
        for cp in sends:
            cp.start()
        for cp in recvs:
            cp.wait_recv()
        for cp in sends:
            cp.wait_send()
        total = all_ref[0]
        for p in range(1, N_DEV):
            total = total + all_ref[p]
        sum_ref[...] = total

    vmem = pl.BlockSpec(memory_space=pltpu.VMEM)
    return pl.pallas_call(
        body, name=name,
        in_specs=[vmem] + [ANY_SPEC] * len(after), out_specs=[vmem, vmem],
        out_shape=[jax.ShapeDtypeStruct((N_DEV, r, LANES), F32), jax.ShapeDtypeStruct((r, LANES), F32)],
        scratch_shapes=[pltpu.SemaphoreType.DMA((N_DEV - 1,)), pltpu.SemaphoreType.DMA((N_DEV - 1,))],
        compiler_params=pltpu.CompilerParams(has_side_effects=True),
    )(buf, *after)


def _adamw_math(w, g, m, v):
    m2 = ADAM_B1 * m + (1.0 - ADAM_B1) * g
    v2 = ADAM_B2 * v + (1.0 - ADAM_B2) * (g * g)
    delta = -ADAM_LR * ((m2 / ADAM_C1) / (jnp.sqrt(v2 / ADAM_C2) + ADAM_EPS) + ADAM_WD * w)
    return delta, m2, v2


def _adamw_slabs(layer, w, m, v, addends, outs, order, *, tr, name):
    depth, r, c = w.shape
    tr = _tile(r, tr)
    n = len(addends)

    def body(*refs):
        w_ref, m_ref, v_ref = refs[:3]
        g_ref, d_ref, m2_ref, v2_ref = refs[-4:]
        g = refs[3][...].astype(F32)
        for a_ref in refs[4:3 + n]:
            g = g + a_ref[...].astype(F32)
        delta, m2, v2 = _adamw_math(w_ref[...], g, m_ref[...], v_ref[...])
        g_ref[...] = g
        d_ref[...] = delta
        m2_ref[...] = m2
        v2_ref[...] = v2

    row = pl.BlockSpec((None, tr, c), lambda i: (layer, i, 0))
    slots = [pl.BlockSpec((None, tr, c), lambda i, p=p: (p, i, 0)) for _, p in addends]
    first_out = 3 + n + 1
    return pl.pallas_call(
        body, name=name, grid=(r // tr,),
        in_specs=[row, row, row] + slots + [pl.BlockSpec((8, LANES), lambda i: (0, 0))] + [ANY_SPEC] * 4,
        out_specs=[row] * 4, out_shape=[jax.ShapeDtypeStruct((depth, r, c), F32)] * 4,
        input_output_aliases={first_out + t: t for t in range(4)},
        compiler_params=_params(("parallel",)),
    )(w, m, v, *[a for a, _ in addends], order, *outs)


def _adamw_small(w, g, m, v):
    def body(w_ref, g_ref, m_ref, v_ref, d_ref, m2_ref, v2_ref):
        delta, m2, v2 = _adamw_math(w_ref[...], g_ref[...], m_ref[...], v_ref[...])
        d_ref[...] = delta
        m2_ref[...] = m2
        v2_ref[...] = v2

    vmem = pl.BlockSpec(memory_space=pltpu.VMEM)
    return pl.pallas_call(
        body, name="adamw_small", in_specs=[vmem] * 4, out_specs=[vmem] * 3,
        out_shape=[jax.ShapeDtypeStruct(w.shape, F32)] * 3,
    )(w, g, m, v)


def _pack(parts):
    flat = jnp.concatenate([p.reshape(-1).astype(F32) for p in parts])
    rows = -(-flat.shape[0] // LANES)
    rows = -(-rows // SMALL_ROWS_ALIGN) * SMALL_ROWS_ALIGN
    flat = jnp.pad(flat, (0, rows * LANES - flat.shape[0]))
    return flat.reshape(rows, LANES)


def _unpack(buf, shapes):
    flat = buf.reshape(-1)
    out, pos = [], 0
    for shp in shapes:
        size = math.prod(shp)
        out.append(flat[pos:pos + size].reshape(shp))
        pos += size
    return out


def _section_widths(d):
    aw = d // 2
    kw = aw // ATTN_GROUP
    rw = d - aw
    rqw = (rw // RET_V_DIM) * RET_QK_DIM
    return (aw, kw, kw, aw, rqw, rqw, rw, rw)


def _layer_fwd(xl, nw, win_full, after_attn, wout_of, qn, kn, dec_f, dec_b, rn, cos, sin):
    h, ht = _rms_fwd(xl, nw)
    aq, ak, v, ag, rq, rk, rvb, rg = _proj_sections(h, win_full, _section_widths(xl.shape[1]),
                                                    (F32, F32, BF16, F32, F32, F32, BF16, F32))
    q, k, rqr, rkr = _prep_fwd(aq, ak, rq, rk, cos, sin, qn, kn)
    att, lse = _attn_fwd(q, k, v)
    ret = _retc_fwd(rqr, rkr, rvb, dec_f + after_attn(att), dec_b)
    y, yt = _gate_fwd(att, ag, ret, rg, rn)
    wout_full = wout_of(y)
    xn = _matmul(y, wout_full, name="out_proj", residual=xl)
    saved = dict(x=xl, ht=ht, aq=aq, ak=ak, ag=ag, rg=rg, q=q, k=k, v=v, rq=rqr, rk=rkr, rv=rvb,
                 att=att, lse=lse, ret=ret, yt=yt, win=win_full, wout=wout_full)
    return xn, saved


def _layer_bwd_weights(gb, sv, qn, kn, dec_f, dec_b, rn, cos, sin, on_dwout):
    dy = _matmul(gb, sv["wout"], name="d_y", trans_b=True)
    dwout = _matmul(sv["yt"], gb, name="d_wout", out_dtype=BF16)
    datt, dag, dret, drg, drn = _gate_bwd(dy, sv["att"], sv["ag"], sv["ret"], sv["rg"], rn + on_dwout(dwout))
    dq, dk, dav = _attn_bwd(sv["q"], sv["k"], sv["v"], sv["att"], datt, sv["lse"])
    drq, drk, drv, gf, gbk = _retc_bwd(sv["rq"], sv["rk"], sv["rv"], dret, dec_f, dec_b)
    dproj, dqn, dkn = _prep_bwd(dq, dk, drq, drk, sv["aq"], sv["ak"], cos, sin, qn, kn, dav, dag, drv, drg)
    dwin = _matmul(sv["ht"], dproj, name="d_win", out_dtype=BF16)
    small = dict(qn=dqn[0], kn=dkn[0], df=gf[:, 0, 0], db=gbk[:, 0, 0], rn=drn[0])
    return dproj, dwin, small


def _layer_bwd_input(g, dproj, sv, nw, behind, after_dh):
    dh = _matmul(dproj, sv["win"], name="d_h", trans_b=True, tm=512, tk=dproj.shape[1], after=behind)
    g, gb, dnw = _rms_bwd(dh, sv["x"], g, nw + after_dh(dh))
    return g, gb, dnw[0]


def kernel(x, norm_w, w_in, q_norm, k_norm, ret_decay_fwd, ret_decay_bwd, ret_norm, w_out, final_norm, loss_target, m_norm_w, m_w_in, m_q_norm, m_k_norm, m_ret_decay_fwd, m_ret_decay_bwd, m_ret_norm, m_w_out, m_final_norm, v_norm_w, v_w_in, v_q_norm, v_k_norm, v_ret_decay_fwd, v_ret_decay_bwd, v_ret_norm, v_w_out, v_final_norm):
    depth, d, _ = w_in.shape
    seq = x.shape[1]
    rw = _section_widths(d)[6]
    rheads = rw // RET_V_DIM
    rns = ret_norm.shape[-1]
    _, _, _, me = _mesh_position()

    target = loss_target[0]
    cos, sin = _rope_tables(seq)

    rn_all, _ = _exchange_small(_pack([ret_norm]), name="gather_ret_norm")
    rn_full = rn_all.reshape(N_DEV, -1)[:, :depth * rheads * rns].reshape(N_DEV, depth, rheads, rns)
    rn_full = jnp.transpose(rn_full, (1, 2, 0, 3)).reshape(depth, rw)

    dec_f = jnp.broadcast_to(ret_decay_fwd[:, :, None, None], (depth, rheads, 1, LANES))
    dec_b = jnp.broadcast_to(ret_decay_bwd[:, :, None, None], (depth, rheads, 1, LANES))

    win_bf = w_in.astype(BF16)
    wout_bf = w_out.astype(BF16)

    saved = []
    xl = x[0]
    first = (SIBLING,) + SAME_CORE_OF_CHIPS
    landed = _gather_wait(_gather_start(win_bf[0], 1, first, me, [], "in0"), 1, first, [], "in0")
    win_full = _pass_on_wait(_pass_on_start(win_bf[0], landed, 1, [], "in0"), 1, [], "in0")
    for l in range(depth):
        out_sent = _gather_start(wout_bf[l], 0, ALL_PEERS, me, [win_full], "out" + str(l))
        nw = norm_w[l] + out_sent[-1][0, 0]
        passed = {}
        if l + 1 < depth:
            in_sent = _gather_start(win_bf[l + 1], 1, first, me, [win_full, out_sent[-1]], "in" + str(l + 1))
            nw = nw + in_sent[-1][0, 0]

        def after_attn(att, passed=passed, l=l):
            if l + 1 == depth:
                return 0.0
            landed = _gather_wait(in_sent, 1, first, [att], "in" + str(l + 1))
            passed["on"] = _pass_on_start(win_bf[l + 1], landed, 1, [], "in" + str(l + 1))
            return passed["on"][-1][0, 0]

        def wout_of(y, out_sent=out_sent, l=l):
            return _gather_wait(out_sent, 0, ALL_PEERS, [y], "out" + str(l))

        xl, sv = _layer_fwd(xl, nw, win_full, after_attn, wout_of, q_norm[l], k_norm[l], dec_f[l], dec_b[l],
                            rn_full[l], cos, sin)
        saved.append(sv)
        if l + 1 < depth:
            win_full = _pass_on_wait(passed["on"], 1, [xl], "in" + str(l + 1))

    loss_row, g, gb, d_final = _loss_head(xl, target, final_norm)

    d_norm, d_qn, d_kn, d_df, d_db, d_rn = [], [], [], [], [], []
    lands = [None] * depth
    pending = None
    for l in reversed(range(depth)):
        sent = {}

        def on_dwout(dwout, sent=sent, l=l):
            sent["out"] = _scatter_start(dwout, 0, me, "out" + str(l))
            return sent["out"][-1][0, 0]

        def after_dh(dh, sent=sent, l=l):
            pairs = _pair_sums(sent["pair"], 1, me, [dh], "in" + str(l))
            sent["in"] = _chips_start(pairs, "in" + str(l))
            return sent["in"][-1][0, 0]

        dproj, dwin, sm = _layer_bwd_weights(gb, saved[l], q_norm[l], k_norm[l], dec_f[l], dec_b[l],
                                             rn_full[l], cos, sin, on_dwout)
        sent["pair"] = _pair_start(dwin, 1, "in" + str(l))
        g, gb, dnw = _layer_bwd_input(g, dproj, saved[l], norm_w[l], [sent["pair"][-1]], after_dh)
        if pending is not None:
            lands[l + 1] = (_chips_wait(pending["in"], [g], "in" + str(l + 1)),
                            _scatter_wait(pending["out"], 0, [g], "out" + str(l + 1)))
        pending = sent
        d_norm.append(dnw)
        d_qn.append(sm["qn"])
        d_kn.append(sm["kn"])
        d_df.append(sm["df"])
        d_db.append(sm["db"])
        d_rn.append(sm["rn"])
    for lst in (d_norm, d_qn, d_kn, d_df, d_db, d_rn):
        lst.reverse()
    order = pending["in"][-1]
    in_outs = [lax.empty(w_in.shape, F32) for _ in range(4)]
    out_outs = [lax.empty(w_out.shape, F32) for _ in range(4)]
    for l in reversed(range(depth)):
        if l == 0:
            lands[0] = (_chips_wait(pending["in"], [g, in_outs[0], out_outs[0]], "in0"),
                        _scatter_wait(pending["out"], 0, [g], "out0"))
        in_outs = _adamw_slabs(l, w_in, m_w_in, v_w_in, lands[l][0], in_outs, order, tr=256, name="adamw_w_in")
        out_outs = _adamw_slabs(l, w_out, m_w_out, v_w_out, lands[l][1], out_outs, order, tr=64, name="adamw_w_out")

    small_shapes = [(depth, d), (depth, HEAD_DIM), (depth, HEAD_DIM), (depth, rheads), (depth, rheads),
                    (depth, rheads, N_DEV * rns), (d,), (1,)]
    grads_local = [jnp.stack(d_norm), jnp.stack(d_qn), jnp.stack(d_kn), jnp.stack(d_df), jnp.stack(d_db),
                   jnp.stack(d_rn).reshape(depth, rheads, N_DEV * rns), d_final[0], loss_row[0, :1]]
    _, gsum = _exchange_small(_pack(grads_local), name="all_reduce_small", after=(in_outs[0], out_outs[0]))
    g_norm, g_qn, g_kn, g_df, g_db, g_rn_full, g_final, loss = _unpack(gsum, small_shapes)
    g_rn = lax.dynamic_slice_in_dim(g_rn_full, me * rns, rns, axis=2)
    small_g = [g_norm, g_qn, g_kn, g_df, g_db, g_rn, g_final]
    small_w = [norm_w, q_norm, k_norm, ret_decay_fwd, ret_decay_bwd, ret_norm, final_norm]
    small_m = [m_norm_w, m_q_norm, m_k_norm, m_ret_decay_fwd, m_ret_decay_bwd, m_ret_norm, m_final_norm]
    small_v = [v_norm_w, v_q_norm, v_k_norm, v_ret_decay_fwd, v_ret_decay_bwd, v_ret_norm, v_final_norm]
    shapes = [a.shape for a in small_w]
    sd, sm, sv2 = _adamw_small(_pack(small_w), _pack(small_g), _pack(small_m), _pack(small_v))
    small_d, small_m2, small_v2 = _unpack(sd, shapes), _unpack(sm, shapes), _unpack(sv2, shapes)

    def ordered(small, win_v, wout_v):
        return [small[0], win_v, small[1], small[2], small[3], small[4], small[5], wout_v, small[6]]

    grads = ordered(small_g, in_outs[0], out_outs[0])
    deltas = ordered(small_d, in_outs[1], out_outs[1])
    new_m = ordered(small_m2, in_outs[2], out_outs[2])
    new_v = ordered(small_v2, in_outs[3], out_outs[3])
    return (loss.reshape(()), g[None], *grads, *deltas, *new_m, *new_v)
```

```python
import functools
import math

import jax
import jax.numpy as jnp
import numpy as np
from jax import lax
from jax.experimental import pallas as pl
from jax.experimental.pallas import tpu as pltpu

F32 = jnp.float32
BF16 = jnp.bfloat16

N_DEV = 8
HEAD_DIM = 128
ATTN_GROUP = 4
RET_QK_DIM = 128
RET_V_DIM = 256
GRID_W = 64
ROPE_THETA = 10000.0
EPS = 1e-6
ADAM_LR = 0.001
ADAM_B1 = 0.9
ADAM_B2 = 0.999
ADAM_EPS = 1e-08
ADAM_WD = 0.01
ADAM_STEP = 10
ADAM_C1 = 1.0 - ADAM_B1 ** ADAM_STEP
ADAM_C2 = 1.0 - ADAM_B2 ** ADAM_STEP
LANES = 128
SMALL_ROWS_ALIGN = 8
VMEM_LIMIT = 56 * 1024 * 1024

NT_DIMS = (((1,), (1,)), ((), ()))
TN_DIMS = (((0,), (0,)), ((), ()))
MESH = pl.DeviceIdType.MESH


def _params(sem):
    return pltpu.CompilerParams(dimension_semantics=sem, vmem_limit_bytes=VMEM_LIMIT)


def _tile(dim, pref, align=16):
    if dim <= pref:
        return dim
    for t in range(pref - pref % align, 0, -align):
        if dim % t == 0:
            return t
    raise ValueError((dim, pref, align))


def _silu_parts(z):
    sg = 1.0 / (1.0 + jnp.exp(-z))
    return z * sg, sg * (1.0 + z * (1.0 - sg))


def _log_sigmoid(x):
    return jnp.minimum(x, 0.0) - jnp.log(1.0 + jnp.exp(-jnp.abs(x)))


def _swap_pairs(z):
    src = lax.broadcasted_iota(jnp.int32, (HEAD_DIM, HEAD_DIM), 0)
    dst = lax.broadcasted_iota(jnp.int32, (HEAD_DIM, HEAD_DIM), 1)
    partner = jnp.where((dst % 64) < 32, dst + 32, dst - 32)
    perm = (src == partner).astype(F32)
    return jnp.dot(z, perm, precision=lax.Precision.HIGH, preferred_element_type=F32)


def _rope(z, cos, sin):
    return z * cos + _swap_pairs(z) * sin


def _rope_transposed(d, cos, sin):
    return d * cos + _swap_pairs(d * sin)


def _rope_tables(seq):
    rows = seq // GRID_W
    row = jnp.repeat(jnp.arange(rows), GRID_W).astype(F32)
    col = jnp.tile(jnp.arange(GRID_W), rows).astype(F32)
    axis_dim = HEAD_DIM // 2
    inv = ROPE_THETA ** (-jnp.arange(0, axis_dim, 2, dtype=F32) / axis_dim)
    ar = row[:, None] * inv[None, :]
    ac = col[:, None] * inv[None, :]
    cos = jnp.concatenate([jnp.cos(ar), jnp.cos(ar), jnp.cos(ac), jnp.cos(ac)], axis=-1)
    sin = jnp.concatenate([-jnp.sin(ar), jnp.sin(ar), -jnp.sin(ac), jnp.sin(ac)], axis=-1)
    return cos, sin


def _matmul(a, b, *, name, trans_b=False, out_dtype=F32, residual=None, tm=1024, tn=512, tk=4096, after=()):
    m, k = a.shape
    n = b.shape[0] if trans_b else b.shape[1]
    tm, tn, tk = _tile(m, tm), _tile(n, tn, LANES), _tile(k, tk, LANES)
    nk = k // tk
    has_res = residual is not None

    def body(*refs):
        a_ref, b_ref = refs[:2]
        r_ref = refs[2] if has_res else None
        o_ref = refs[2 + has_res + len(after)]
        if trans_b:
            part = lax.dot_general(a_ref[...], b_ref[...], NT_DIMS, preferred_element_type=F32)
        else:
            part = jnp.dot(a_ref[...], b_ref[...], preferred_element_type=F32)

        def finish(r):
            if has_res:
                r = r + r_ref[...]
            o_ref[...] = r.astype(o_ref.dtype)

        if nk == 1:
            finish(part)
        else:
            acc_ref = refs[-1]
            kk = pl.program_id(2)

            @pl.when(kk == 0)
            def _():
                acc_ref[...] = part

            @pl.when(kk > 0)
            def _():
                acc_ref[...] += part

            @pl.when(kk == nk - 1)
            def _():
                finish(acc_ref[...])

    if trans_b:
        b_spec = pl.BlockSpec((tn, tk), lambda i, j, kk: (j, kk))
    else:
        b_spec = pl.BlockSpec((tk, tn), lambda i, j, kk: (kk, j))
    in_specs = [pl.BlockSpec((tm, tk), lambda i, j, kk: (i, kk)), b_spec]
    args = [a, b]
    if has_res:
        in_specs.append(pl.BlockSpec((tm, tn), lambda i, j, kk: (i, j)))
        args.append(residual)
    in_specs += [ANY_SPEC] * len(after)
    args += list(after)
    return pl.pallas_call(
        body, name=name, grid=(m // tm, n // tn, nk),
        in_specs=in_specs,
        out_specs=pl.BlockSpec((tm, tn), lambda i, j, kk: (i, j)),
        out_shape=jax.ShapeDtypeStruct((m, n), out_dtype),
        scratch_shapes=[pltpu.VMEM((tm, tn), F32)] if nk > 1 else [],
        compiler_params=_params(("parallel", "parallel", "arbitrary")),
    )(*args)


def _proj_sections(a, b, widths, dtypes, *, tm=1024, tn=512, after=()):
    m, k = a.shape
    tm = _tile(m, tm)
    tn = _tile(min(widths), tn, LANES)
    assert all(w % tn == 0 for w in widths) and sum(widths) == b.shape[1]
    nblk = [w // tn for w in widths]
    first = [int(o) // tn for o in np.cumsum((0,) + tuple(widths))[:-1]]

    def body(a_ref, b_ref, *refs):
        j = pl.program_id(1)
        part = jnp.dot(a_ref[...], b_ref[...], preferred_element_type=F32)
        for o_ref, lo, n in zip(refs[len(after):], first, nblk):
            @pl.when(jnp.logical_and(j >= lo, j < lo + n))
            def _(o_ref=o_ref):
                o_ref[...] = part.astype(o_ref.dtype)

    out_specs = [pl.BlockSpec((tm, tn), lambda i, j, lo=lo, n=n: (i, jnp.clip(j - lo, 0, n - 1)))
                 for lo, n in zip(first, nblk)]
    return pl.pallas_call(
        body, name="proj", grid=(m // tm, b.shape[1] // tn),
        in_specs=[pl.BlockSpec((tm, k), lambda i, j: (i, 0)), pl.BlockSpec((k, tn), lambda i, j: (0, j))]
        + [ANY_SPEC] * len(after),
        out_specs=out_specs,
        out_shape=[jax.ShapeDtypeStruct((m, w), dt) for w, dt in zip(widths, dtypes)],
        compiler_params=_params(("arbitrary", "arbitrary")),
    )(a, b, *after)


def _rms_fwd(x, w, *, ts=256):
    s, d = x.shape
    ts = _tile(s, ts)

    def body(x_ref, w_ref, h_ref, ht_ref):
        xv = x_ref[...]
        r = lax.rsqrt(jnp.mean(xv * xv, axis=-1, keepdims=True) + EPS)
        h = xv * r * w_ref[...]
        h_ref[...] = h.astype(BF16)
        ht_ref[...] = h.T.astype(BF16)

    row = pl.BlockSpec((ts, d), lambda i: (i, 0))
    return pl.pallas_call(
        body, name="rms_fwd", grid=(s // ts,),
        in_specs=[row, pl.BlockSpec((1, d), lambda i: (0, 0))],
        out_specs=[row, pl.BlockSpec((d, ts), lambda i: (0, i))],
        out_shape=[jax.ShapeDtypeStruct((s, d), BF16), jax.ShapeDtypeStruct((d, s), BF16)],
        compiler_params=_params(("parallel",)),
    )(x, w.reshape(1, d))


def _rms_bwd(dh, x, g, w, *, ts=256):
    s, d = x.shape
    ts = _tile(s, ts)

    def body(dh_ref, x_ref, g_ref, w_ref, dx_ref, dxb_ref, dw_ref):
        xv = x_ref[...]
        r = lax.rsqrt(jnp.mean(xv * xv, axis=-1, keepdims=True) + EPS)
        xh = xv * r
        dhv = dh_ref[...]
        dn = dhv * w_ref[...]
        dx = g_ref[...] + r * (dn - xh * jnp.mean(dn * xh, axis=-1, keepdims=True))
        dx_ref[...] = dx
        dxb_ref[...] = dx.astype(BF16)
        part = jnp.sum(dhv * xh, axis=0, keepdims=True)

        @pl.when(pl.program_id(0) == 0)
        def _():
            dw_ref[...] = part

        @pl.when(pl.program_id(0) > 0)
        def _():
            dw_ref[...] += part

    row = pl.BlockSpec((ts, d), lambda i: (i, 0))
    vec = pl.BlockSpec((1, d), lambda i: (0, 0))
    return pl.pallas_call(
        body, name="rms_bwd", grid=(s // ts,),
        in_specs=[row, row, row, vec],
        out_specs=[row, row, vec],
        out_shape=[jax.ShapeDtypeStruct((s, d), F32), jax.ShapeDtypeStruct((s, d), BF16),
                   jax.ShapeDtypeStruct((1, d), F32)],
        compiler_params=_params(("arbitrary",)),
    )(dh, x, g, w.reshape(1, d))


def _loss_head(x, target, w, *, ts=256):
    s, d = x.shape
    ts = _tile(s, ts)

    def body(x_ref, t_ref, w_ref, loss_ref, dx_ref, dxb_ref, dw_ref):
        xv = x_ref[...]
        r = lax.rsqrt(jnp.mean(xv * xv, axis=-1, keepdims=True) + EPS)
        xh = xv * r
        wv = w_ref[...]
        diff = xh * wv - t_ref[...]
        lpart = 0.5 * jnp.sum(jnp.mean(diff * diff, axis=-1, keepdims=True), axis=0, keepdims=True)
        dout = diff * (1.0 / d)
        dn = dout * wv
        dx = r * (dn - xh * jnp.mean(dn * xh, axis=-1, keepdims=True))
        dx_ref[...] = dx
        dxb_ref[...] = dx.astype(BF16)
        part = jnp.sum(dout * xh, axis=0, keepdims=True)
        lrow = jnp.broadcast_to(lpart, loss_ref.shape)

        @pl.when(pl.program_id(0) == 0)
        def _():
            dw_ref[...] = part
            loss_ref[...] = lrow

        @pl.when(pl.program_id(0) > 0)
        def _():
            dw_ref[...] += part
            loss_ref[...] += lrow

    row = pl.BlockSpec((ts, d), lambda i: (i, 0))
    vec = pl.BlockSpec((1, d), lambda i: (0, 0))
    return pl.pallas_call(
        body, name="loss_head", grid=(s // ts,),
        in_specs=[row, row, vec],
        out_specs=[pl.BlockSpec((1, LANES), lambda i: (0, 0)), row, row, vec],
        out_shape=[jax.ShapeDtypeStruct((1, LANES), F32), jax.ShapeDtypeStruct((s, d), F32),
                   jax.ShapeDtypeStruct((s, d), BF16), jax.ShapeDtypeStruct((1, d), F32)],
        compiler_params=_params(("arbitrary",)),
    )(x, target, w.reshape(1, d))


def _prep_fwd(aq, ak, rq, rk, cos, sin, qw, kw, *, ts=256):
    s = aq.shape[0]
    ts = _tile(s, ts)
    attn_scale = HEAD_DIM ** -0.5
    ret_scale = RET_QK_DIM ** -0.5
    nq, nk, nr = aq.shape[1] // HEAD_DIM, ak.shape[1] // HEAD_DIM, rq.shape[1] // RET_QK_DIM

    def body(aq_ref, ak_ref, rq_ref, rk_ref, cos_ref, sin_ref, qw_ref, kw_ref,
             q_out, k_out, rq_out, rk_out):
        c, sn = cos_ref[...], sin_ref[...]

        def normed(u, w):
            return u * lax.rsqrt(jnp.mean(u * u, axis=-1, keepdims=True) + EPS) * w

        for j in range(nq):
            sl = slice(j * HEAD_DIM, (j + 1) * HEAD_DIM)
            q_out[:, sl] = (_rope(normed(aq_ref[:, sl], qw_ref[...]), c, sn) * attn_scale).astype(BF16)
        for j in range(nk):
            sl = slice(j * HEAD_DIM, (j + 1) * HEAD_DIM)
            k_out[:, sl] = _rope(normed(ak_ref[:, sl], kw_ref[...]), c, sn).astype(BF16)
        for j in range(nr):
            sl = slice(j * RET_QK_DIM, (j + 1) * RET_QK_DIM)
            rq_out[:, sl] = _rope(rq_ref[:, sl], c, sn).astype(BF16)
            rk_out[:, sl] = (_rope(rk_ref[:, sl], c, sn) * ret_scale).astype(BF16)

    def row(arr):
        return pl.BlockSpec((ts, arr.shape[1]), lambda i: (i, 0))

    vec = pl.BlockSpec((1, HEAD_DIM), lambda i: (0, 0))
    ins = [aq, ak, rq, rk]
    return pl.pallas_call(
        body, name="prep_fwd", grid=(s // ts,),
        in_specs=[row(a) for a in ins] + [row(cos), row(sin), vec, vec],
        out_specs=[row(a) for a in ins],
        out_shape=[jax.ShapeDtypeStruct(a.shape, BF16) for a in ins],
        compiler_params=_params(("parallel",)),
    )(*ins, cos, sin, qw.reshape(1, HEAD_DIM), kw.reshape(1, HEAD_DIM))


def _prep_bwd(dq, dk, drq, drk, aq, ak, cos, sin, qw, kw, dav, dag, drv, drg, *, ts=256):
    s = aq.shape[0]
    ts = _tile(s, ts)
    attn_scale = HEAD_DIM ** -0.5
    ret_scale = RET_QK_DIM ** -0.5
    nq, nk, nr = aq.shape[1] // HEAD_DIM, ak.shape[1] // HEAD_DIM, drq.shape[1] // RET_QK_DIM
    widths = (aq.shape[1], ak.shape[1], dav.shape[1], dag.shape[1], drq.shape[1], drk.shape[1], drv.shape[1],
              drg.shape[1])
    o_aq, o_ak, o_av, o_ag, o_rq, o_rk, o_rv, o_rg = (int(o) for o in np.cumsum((0,) + widths)[:-1])

    def body(dq_ref, dk_ref, drq_ref, drk_ref, aq_ref, ak_ref, cos_ref, sin_ref, dav_ref, dag_ref, drv_ref, drg_ref,
             qw_ref, kw_ref, dproj_ref, dqw_ref, dkw_ref):
        c, sn = cos_ref[...], sin_ref[...]
        for ref, off in ((dav_ref, o_av), (dag_ref, o_ag), (drv_ref, o_rv), (drg_ref, o_rg)):
            dproj_ref[:, off:off + ref.shape[1]] = ref[...]

        def unrope(d):
            return _rope_transposed(d, c, sn)

        def norm_bwd(dun, u, w):
            r = lax.rsqrt(jnp.mean(u * u, axis=-1, keepdims=True) + EPS)
            uh = u * r
            dn = dun * w
            du = r * (dn - uh * jnp.mean(dn * uh, axis=-1, keepdims=True))
            return du, jnp.sum(dun * uh, axis=0, keepdims=True)

        dqw = jnp.zeros((1, HEAD_DIM), F32)
        for j in range(nq):
            sl = slice(j * HEAD_DIM, (j + 1) * HEAD_DIM)
            du, dw = norm_bwd(unrope(dq_ref[:, sl] * attn_scale), aq_ref[:, sl], qw_ref[...])
            dproj_ref[:, o_aq + j * HEAD_DIM:o_aq + (j + 1) * HEAD_DIM] = du.astype(BF16)
            dqw = dqw + dw
        dkw = jnp.zeros((1, HEAD_DIM), F32)
        for j in range(nk):
            sl = slice(j * HEAD_DIM, (j + 1) * HEAD_DIM)
            du, dw = norm_bwd(unrope(dk_ref[:, sl]), ak_ref[:, sl], kw_ref[...])
            dproj_ref[:, o_ak + j * HEAD_DIM:o_ak + (j + 1) * HEAD_DIM] = du.astype(BF16)
            dkw = dkw + dw
        for j in range(nr):
            sl = slice(j * RET_QK_DIM, (j + 1) * RET_QK_DIM)
            dproj_ref[:, o_rq + j * RET_QK_DIM:o_rq + (j + 1) * RET_QK_DIM] = unrope(drq_ref[:, sl]).astype(BF16)
            dproj_ref[:, o_rk + j * RET_QK_DIM:o_rk + (j + 1) * RET_QK_DIM] = (
                unrope(drk_ref[:, sl] * ret_scale).astype(BF16))

        @pl.when(pl.program_id(0) == 0)
        def _():
            dqw_ref[...] = dqw
            dkw_ref[...] = dkw

        @pl.when(pl.program_id(0) > 0)
        def _():
            dqw_ref[...] += dqw
            dkw_ref[...] += dkw

    def row(arr):
        return pl.BlockSpec((ts, arr.shape[1]), lambda i: (i, 0))

    vec = pl.BlockSpec((1, HEAD_DIM), lambda i: (0, 0))
    ins = [dq, dk, drq, drk, aq, ak, cos, sin, dav, dag, drv, drg]
    total = sum(widths)
    return pl.pallas_call(
        body, name="prep_bwd", grid=(s // ts,),
        in_specs=[row(a) for a in ins] + [vec, vec],
        out_specs=[pl.BlockSpec((ts, total), lambda i: (i, 0)), vec, vec],
        out_shape=[jax.ShapeDtypeStruct((s, total), BF16)] + [jax.ShapeDtypeStruct((1, HEAD_DIM), F32)] * 2,
        compiler_params=_params(("arbitrary",)),
    )(*ins, qw.reshape(1, HEAD_DIM), kw.reshape(1, HEAD_DIM))


def _attn_fwd(q, k, v, *, tq=4096, sub=256):
    s, aw = q.shape
    tq = _tile(s, tq)
    sub = _tile(tq, sub)
    heads, kvh = aw // HEAD_DIM, k.shape[1] // HEAD_DIM
    grp = heads // kvh

    def body(q_ref, k_ref, v_ref, o_ref, lse_ref):
        kv_ = k_ref[...]
        v_ext = jnp.concatenate([v_ref[...], jnp.ones((s, HEAD_DIM), BF16)], axis=-1)
        for r in range(tq // sub):
            rows = slice(r * sub, (r + 1) * sub)
            sc = lax.dot_general(q_ref[rows, :], kv_, NT_DIMS, preferred_element_type=F32)
            m = jnp.max(sc, axis=-1, keepdims=True)
            p = jnp.exp((sc - m).astype(BF16))
            oe = jnp.dot(p, v_ext, preferred_element_type=F32)
            l = oe[:, HEAD_DIM:HEAD_DIM + 1]
            o_ref[rows, :] = (oe[:, :HEAD_DIM] / l).astype(o_ref.dtype)
            lse_ref[rows, :] = jnp.broadcast_to(m + jnp.log(l), (sub, HEAD_DIM))

    qspec = pl.BlockSpec((tq, HEAD_DIM), lambda kv, g, i: (i, kv * grp + g))
    kspec = pl.BlockSpec((s, HEAD_DIM), lambda kv, g, i: (0, kv))
    return pl.pallas_call(
        body, name="attn_fwd", grid=(kvh, grp, s // tq),
        in_specs=[qspec, kspec, kspec],
        out_specs=[qspec, qspec],
        out_shape=[jax.ShapeDtypeStruct((s, aw), BF16), jax.ShapeDtypeStruct((s, aw), F32)],
        compiler_params=_params(("parallel", "parallel", "parallel")),
    )(q, k, v)


def _attn_bwd(q, k, v, o, do, lse, *, tq=1024, sub=256):
    s, aw = q.shape
    tq = _tile(s, tq)
    sub = _tile(tq, sub)
    heads, kvh = aw // HEAD_DIM, k.shape[1] // HEAD_DIM
    grp = heads // kvh
    nq = s // tq

    def body(q_ref, k_ref, v_ref, o_ref, do_ref, lse_ref, dq_ref, dk_ref, dv_ref, dk_acc, dv_acc, p_scr, ds_scr):
        g, i = pl.program_id(1), pl.program_id(2)
        kv_, vv = k_ref[...], v_ref[...]
        for r in range(tq // sub):
            rows = slice(r * sub, (r + 1) * sub)
            qv, dov = q_ref[rows, :], do_ref[rows, :]
            sc = lax.dot_general(qv, kv_, NT_DIMS, preferred_element_type=F32)
            p = jnp.exp((sc - lse_ref[rows, :1]).astype(BF16))
            dp = lax.dot_general(dov, vv, NT_DIMS, preferred_element_type=F32)
            delta = jnp.sum(dov.astype(F32) * o_ref[rows, :].astype(F32), axis=-1, keepdims=True)
            ds = p * (dp - delta).astype(BF16)
            dq_ref[rows, :] = jnp.dot(ds, kv_, preferred_element_type=F32)
            p_scr[rows, :] = p
            ds_scr[rows, :] = ds
        dvp = lax.dot_general(p_scr[...], do_ref[...], TN_DIMS, preferred_element_type=F32)
        dkp = lax.dot_general(ds_scr[...], q_ref[...], TN_DIMS, preferred_element_type=F32)
        first = jnp.logical_and(g == 0, i == 0)

        @pl.when(first)
        def _():
            dv_acc[...] = dvp
            dk_acc[...] = dkp

        @pl.when(jnp.logical_not(first))
        def _():
            dv_acc[...] += dvp
            dk_acc[...] += dkp

        @pl.when(jnp.logical_and(g == grp - 1, i == nq - 1))
        def _():
            dk_ref[...] = dk_acc[...]
            dv_ref[...] = dv_acc[...].astype(dv_ref.dtype)

    qspec = pl.BlockSpec((tq, HEAD_DIM), lambda kv, g, i: (i, kv * grp + g))
    kspec = pl.BlockSpec((s, HEAD_DIM), lambda kv, g, i: (0, kv))
    return pl.pallas_call(
        body, name="attn_bwd", grid=(kvh, grp, nq),
        in_specs=[qspec, kspec, kspec, qspec, qspec, qspec],
        out_specs=[qspec, kspec, kspec],
        out_shape=[jax.ShapeDtypeStruct((s, aw), F32), jax.ShapeDtypeStruct(k.shape, F32),
                   jax.ShapeDtypeStruct(v.shape, BF16)],
        scratch_shapes=[pltpu.VMEM((s, HEAD_DIM), F32), pltpu.VMEM((s, HEAD_DIM), F32),
                        pltpu.VMEM((tq, s), BF16), pltpu.VMEM((tq, s), BF16)],
        compiler_params=_params(("parallel", "arbitrary", "arbitrary")),
    )(q, k, v, o, do, lse)


def _sum_all(z):
    return jnp.sum(jnp.sum(z, axis=0, keepdims=True), axis=1, keepdims=True)


def _chunk_consts(df_ref, db_ref, t):
    lf = _log_sigmoid(df_ref[0][:, :1])
    lb = _log_sigmoid(db_ref[0][:, :1])
    r = lax.broadcasted_iota(jnp.int32, (t, 1), 0).astype(F32)
    c = lax.broadcasted_iota(jnp.int32, (1, t), 1).astype(F32)
    diff = r - c
    dm = jnp.exp(diff * jnp.where(diff >= 0, lf, -lb))
    return dict(diff=diff, dm=dm, r=r,
                af=jnp.exp(lf * (r + 1.0)), bf=jnp.exp(lf * (t - 1.0 - r)), gf=jnp.exp(lf * t),
                ab=jnp.exp(lb * (t - r)), bb=jnp.exp(lb * r), gb=jnp.exp(lb * t))


def _scaled(x, f):
    return (x.astype(F32) * f).astype(BF16)


def _retc_specs(s):
    qspec = pl.BlockSpec((s, RET_QK_DIM), lambda h: (0, h))
    vspec = pl.BlockSpec((s, RET_V_DIM), lambda h: (0, h))
    dspec = pl.BlockSpec((1, 1, LANES), lambda h: (h, 0, 0))
    return qspec, vspec, dspec


def _retc_fwd(q, k, v, dec_f, dec_b, *, t=256):
    s, qw = q.shape
    t = _tile(s, t)
    heads, nc = qw // RET_QK_DIM, s // t
    qspec, vspec, dspec = _retc_specs(s)

    def body(q_ref, k_ref, v_ref, df_ref, db_ref, o_ref):
        cs = _chunk_consts(df_ref, db_ref, t)

        def rows_of(i):
            return pl.ds(pl.multiple_of(i * t, t), t)

        def forward(i, sf):
            rows = rows_of(i)
            qi, ki, vi = q_ref[rows, :], k_ref[rows, :], v_ref[rows, :]
            sc = lax.dot_general(qi, ki, NT_DIMS, preferred_element_type=F32)
            intra = jnp.dot((sc * cs["dm"]).astype(BF16), vi, preferred_element_type=F32)
            cross = jnp.dot(_scaled(qi, cs["af"]), sf.astype(BF16), preferred_element_type=F32)
            o_ref[rows, :] = intra + cross
            return cs["gf"] * sf + lax.dot_general(_scaled(ki, cs["bf"]), vi, TN_DIMS, preferred_element_type=F32)

        def backward(j, sb):
            rows = rows_of(nc - 1 - j)
            qi, ki, vi = q_ref[rows, :], k_ref[rows, :], v_ref[rows, :]
            o_ref[rows, :] += jnp.dot(_scaled(qi, cs["ab"]), sb.astype(BF16), preferred_element_type=F32)
            return cs["gb"] * sb + lax.dot_general(_scaled(ki, cs["bb"]), vi, TN_DIMS, preferred_element_type=F32)

        zero = jnp.zeros((RET_QK_DIM, RET_V_DIM), F32)
        lax.fori_loop(0, nc, forward, zero, unroll=True)
        lax.fori_loop(0, nc, backward, zero, unroll=True)

    return pl.pallas_call(
        body, name="ret_fwd", grid=(heads,),
        in_specs=[qspec, qspec, vspec, dspec, dspec],
        out_specs=vspec, out_shape=jax.ShapeDtypeStruct(v.shape, F32),
        compiler_params=_params(("parallel",)),
    )(q, k, v, dec_f, dec_b)


def _retc_bwd(q, k, v, do, dec_f, dec_b, *, t=512):
    s, qw = q.shape
    t = _tile(s, t)
    heads, nc = qw // RET_QK_DIM, s // t
    qspec, vspec, dspec = _retc_specs(s)
    gspec = pl.BlockSpec((1, 8, LANES), lambda h: (h, 0, 0))

    def body(q_ref, k_ref, v_ref, do_ref, df_ref, db_ref, dq_ref, dk_ref, dv_ref, gf_ref, gb_ref,
             sf_scr, sb_scr, dv_acc):
        cs = _chunk_consts(df_ref, db_ref, t)
        r, diff, dm = cs["r"], cs["diff"], cs["dm"]

        def rows_of(i):
            return pl.ds(pl.multiple_of(i * t, t), t)

        def tn(a, b):
            return lax.dot_general(a, b, TN_DIMS, preferred_element_type=F32)

        def nt(a, b):
            return lax.dot_general(a, b, NT_DIMS, preferred_element_type=F32)

        def states_f(i, sf):
            sf_scr[i] = sf
            rows = rows_of(i)
            return cs["gf"] * sf + tn(_scaled(k_ref[rows, :], cs["bf"]), v_ref[rows, :])

        def states_b(j, sb):
            i = nc - 1 - j
            sb_scr[i] = sb
            rows = rows_of(i)
            return cs["gb"] * sb + tn(_scaled(k_ref[rows, :], cs["bb"]), v_ref[rows, :])

        zero = jnp.zeros((RET_QK_DIM, RET_V_DIM), F32)
        lax.fori_loop(0, nc, states_f, zero, unroll=True)
        lax.fori_loop(0, nc, states_b, zero, unroll=True)

        def scan_grads(i, state, u, qf, kf, vi, doi, fa, fb, step, wa, wb):
            qa, kb = qf * fa, kf * fb
            ub = u.astype(BF16)
            dqa = nt(doi, state.astype(BF16))
            dkb = nt(vi, ub)
            dv = jnp.dot(kb.astype(BF16), ub, preferred_element_type=F32)
            dlog = _sum_all(dqa * qa * wa) + _sum_all(dkb * kb * wb) + t * step * _sum_all(u * state)
            u_new = step * u + tn(qa.astype(BF16), doi)
            return dqa * fa, dkb * fb, dv, u_new, dlog

        def sweep_f(j, carry):
            u, accf, accb = carry
            i = nc - 1 - j
            rows = rows_of(i)
            qi, ki, vi, doi = q_ref[rows, :], k_ref[rows, :], v_ref[rows, :], do_ref[rows, :]
            sc = nt(qi, ki)
            p = sc * dm
            dp = nt(doi, vi)
            ds = (dp * dm).astype(BF16)
            tt = dp * p * diff
            accf = accf + _sum_all(jnp.where(diff > 0, tt, 0.0))
            accb = accb + _sum_all(jnp.where(diff < 0, -tt, 0.0))
            dq1, dk1, dv1, u, dlog = scan_grads(i, sf_scr[i], u, qi.astype(F32), ki.astype(F32), vi, doi,
                                                cs["af"], cs["bf"], cs["gf"], r + 1.0, t - 1.0 - r)
            dq_ref[rows, :] = jnp.dot(ds, ki, preferred_element_type=F32) + dq1
            dk_ref[rows, :] = tn(ds, qi) + dk1
            dv_acc[rows, :] = tn(p.astype(BF16), doi) + dv1
            return u, accf + dlog, accb

        def sweep_b(i, carry):
            w, accb = carry
            rows = rows_of(i)
            qi, ki, vi, doi = q_ref[rows, :], k_ref[rows, :], v_ref[rows, :], do_ref[rows, :]
            dq1, dk1, dv1, w, dlog = scan_grads(i, sb_scr[i], w, qi.astype(F32), ki.astype(F32), vi, doi,
                                                cs["ab"], cs["bb"], cs["gb"], t - r, r)
            dq_ref[rows, :] += dq1
            dk_ref[rows, :] += dk1
            dv_acc[rows, :] += dv1
            return w, accb + dlog

        z11 = jnp.zeros((1, 1), F32)
        _, accf, accb = lax.fori_loop(0, nc, sweep_f, (zero, z11, z11), unroll=4)
        _, accb = lax.fori_loop(0, nc, sweep_b, (zero, accb), unroll=4)
        dv_ref[...] = dv_acc[...].astype(dv_ref.dtype)
        gf_ref[...] = jnp.broadcast_to((accf / (1.0 + jnp.exp(df_ref[0][:, :1]))).reshape(1, 1, 1), gf_ref.shape)
        gb_ref[...] = jnp.broadcast_to((accb / (1.0 + jnp.exp(db_ref[0][:, :1]))).reshape(1, 1, 1), gb_ref.shape)

    return pl.pallas_call(
        body, name="ret_bwd", grid=(heads,),
        in_specs=[qspec, qspec, vspec, vspec, dspec, dspec],
        out_specs=[qspec, qspec, vspec, gspec, gspec],
        out_shape=[jax.ShapeDtypeStruct(q.shape, F32), jax.ShapeDtypeStruct(k.shape, F32),
                   jax.ShapeDtypeStruct(v.shape, BF16),
                   jax.ShapeDtypeStruct((heads, 8, LANES), F32), jax.ShapeDtypeStruct((heads, 8, LANES), F32)],
        scratch_shapes=[pltpu.VMEM((nc, RET_QK_DIM, RET_V_DIM), F32), pltpu.VMEM((nc, RET_QK_DIM, RET_V_DIM), F32),
                        pltpu.VMEM((s, RET_V_DIM), F32)],
        compiler_params=_params(("parallel",)),
    )(q, k, v, do, dec_f, dec_b)


def _gate_fwd(att, ag, ret, rg, rnw, *, ts=256):
    s, aw = att.shape
    rw = ret.shape[1]
    ts = _tile(s, ts)
    rheads = rw // RET_V_DIM

    def body(att_ref, ag_ref, ret_ref, rg_ref, w_ref, y_ref, yt_ref):
        def put(lo, hi, val):
            y_ref[:, lo:hi] = val.astype(BF16)
            yt_ref[lo:hi, :] = val.T.astype(BF16)

        sa, _ = _silu_parts(ag_ref[...])
        put(0, aw, sa * att_ref[...].astype(F32))
        for h in range(rheads):
            sl = slice(h * RET_V_DIM, (h + 1) * RET_V_DIM)
            rt = ret_ref[:, sl]
            rn = rt * lax.rsqrt(jnp.mean(rt * rt, axis=-1, keepdims=True) + EPS) * w_ref[:, sl]
            sr, _ = _silu_parts(rg_ref[:, sl])
            put(aw + h * RET_V_DIM, aw + (h + 1) * RET_V_DIM, sr * rn)

    def row(w):
        return pl.BlockSpec((ts, w), lambda i: (i, 0))

    return pl.pallas_call(
        body, name="gate_fwd", grid=(s // ts,),
        in_specs=[row(aw), row(aw), row(rw), row(rw), pl.BlockSpec((1, rw), lambda i: (0, 0))],
        out_specs=[row(aw + rw), pl.BlockSpec((aw + rw, ts), lambda i: (0, i))],
        out_shape=[jax.ShapeDtypeStruct((s, aw + rw), BF16), jax.ShapeDtypeStruct((aw + rw, s), BF16)],
        compiler_params=_params(("parallel",)),
    )(att, ag, ret, rg, rnw.reshape(1, rw))


def _gate_bwd(dy, att, ag, ret, rg, rnw, *, ts=256):
    s, aw = att.shape
    rw = ret.shape[1]
    ts = _tile(s, ts)
    rheads = rw // RET_V_DIM

    def body(dy_ref, att_ref, ag_ref, ret_ref, rg_ref, w_ref, datt_ref, dag_ref, dret_ref, drg_ref, dw_ref):
        sa, dsa = _silu_parts(ag_ref[...])
        dya = dy_ref[:, :aw]
        datt_ref[...] = (dya * sa).astype(BF16)
        dag_ref[...] = (dya * att_ref[...].astype(F32) * dsa).astype(BF16)
        parts = []
        for h in range(rheads):
            sl = slice(h * RET_V_DIM, (h + 1) * RET_V_DIM)
            rt = ret_ref[:, sl]
            rr = lax.rsqrt(jnp.mean(rt * rt, axis=-1, keepdims=True) + EPS)
            rh = rt * rr
            wv = w_ref[:, sl]
            sr, dsr = _silu_parts(rg_ref[:, sl])
            dyr = dy_ref[:, aw + h * RET_V_DIM:aw + (h + 1) * RET_V_DIM]
            drg_ref[:, sl] = (dyr * rh * wv * dsr).astype(BF16)
            drn = dyr * sr
            dn = drn * wv
            dret_ref[:, sl] = (rr * (dn - rh * jnp.mean(dn * rh, axis=-1, keepdims=True))).astype(BF16)
            parts.append(jnp.sum(drn * rh, axis=0, keepdims=True))
        part = jnp.concatenate(parts, axis=-1)

        @pl.when(pl.program_id(0) == 0)
        def _():
            dw_ref[...] = part

        @pl.when(pl.program_id(0) > 0)
        def _():
            dw_ref[...] += part

    def row(w):
        return pl.BlockSpec((ts, w), lambda i: (i, 0))

    vec = pl.BlockSpec((1, rw), lambda i: (0, 0))
    return pl.pallas_call(
        body, name="gate_bwd", grid=(s // ts,),
        in_specs=[row(aw + rw), row(aw), row(aw), row(rw), row(rw), vec],
        out_specs=[row(aw), row(aw), row(rw), row(rw), vec],
        out_shape=[jax.ShapeDtypeStruct((s, aw), BF16), jax.ShapeDtypeStruct((s, aw), BF16),
                   jax.ShapeDtypeStruct((s, rw), BF16), jax.ShapeDtypeStruct((s, rw), BF16),
                   jax.ShapeDtypeStruct((1, rw), F32)],
        compiler_params=_params(("arbitrary",)),
    )(dy, att, ag, ret, rg, rnw.reshape(1, rw))


def _mesh_position():
    x, y, c = lax.axis_index("x"), lax.axis_index("y"), lax.axis_index("c")
    return x, y, c, 4 * x + 2 * y + c


def _peer(x, y, c, k):
    px = 1 - x if k & 4 else x
    py = 1 - y if k & 2 else y
    pc = 1 - c if k & 1 else c
    return (px, py, pc), 4 * px + 2 * py + pc


HBM_SPEC = pl.BlockSpec(memory_space=pltpu.HBM)
SEM_SPEC = pl.BlockSpec(memory_space=pltpu.SEMAPHORE)
ANY_SPEC = pl.BlockSpec(memory_space=pl.ANY)
DATAFLOW = pltpu.SideEffectType.DATAFLOW_SIDE_EFFECTING


def _hbm(a):
    return pltpu.with_memory_space_constraint(a, pltpu.HBM)


def _split_start(name, copies, n, src, land, after):
    def body(*refs):
        (send_sems, recv_sems), token = refs[2 + len(after):4 + len(after)], refs[-1]
        sends, _ = copies(refs[0], refs[1], send_sems, recv_sems)
        for cp in sends:
            cp.start()
        token[...] = jnp.zeros_like(token)

    return pl.pallas_call(
        body, name=name,
        out_shape=(pltpu.SemaphoreType.DMA((n,)), pltpu.SemaphoreType.DMA((n,)),
                   pltpu.HBM(src.shape, src.dtype), pltpu.HBM(land.shape, land.dtype),
                   jax.ShapeDtypeStruct((8, LANES), F32)),
        in_specs=[HBM_SPEC] * 2 + [ANY_SPEC] * len(after),
        out_specs=(SEM_SPEC, SEM_SPEC, HBM_SPEC, HBM_SPEC, pl.BlockSpec(memory_space=pltpu.VMEM)),
        input_output_aliases={0: 2, 1: 3},
        compiler_params=pltpu.CompilerParams(has_side_effects=DATAFLOW),
    )(_hbm(src), _hbm(land), *after)


def _split_wait(name, copies, started, after):
    send_sems, recv_sems, src, land = started[:4]

    def body(*refs):
        sends, recvs = copies(refs[0], refs[1], refs[2], refs[3])
        for cp in sends:
            cp.wait_send()
        for cp in recvs:
            cp.wait_recv()

    return pl.pallas_call(
        body, name=name,
        out_shape=(pltpu.HBM(src.shape, src.dtype), pltpu.HBM(land.shape, land.dtype)),
        in_specs=[HBM_SPEC] * 2 + [SEM_SPEC, SEM_SPEC] + [ANY_SPEC] * len(after),
        out_specs=(HBM_SPEC,) * 2,
        input_output_aliases={0: 0, 1: 1},
        compiler_params=pltpu.CompilerParams(has_side_effects=DATAFLOW),
    )(src, land, send_sems, recv_sems, *after)


def _slab(ref, p, size, axis):
    if axis == 1:
        return ref.at[:, pl.ds(pl.multiple_of(p * size, LANES), size)]
    return ref.at[pl.ds(pl.multiple_of(p * size, 16), size), :]


ALL_PEERS = tuple(range(1, N_DEV))
SIBLING = 1
SAME_CORE_OF_CHIPS = (2, 4, 6)


def _gather_copies(size, axis, ks):
    def copies(shard_ref, full_ref, send_sems, recv_sems):
        x, y, c, me = _mesh_position()
        sends, recvs = [], []
        for j, k in enumerate(ks):
            peer, pid = _peer(x, y, c, k)
            sends.append(pltpu.make_async_remote_copy(
                src_ref=shard_ref, dst_ref=_slab(full_ref, me, size, axis), send_sem=send_sems.at[j],
                recv_sem=recv_sems.at[j], device_id=peer, device_id_type=MESH))
            recvs.append(pltpu.make_async_remote_copy(
                src_ref=shard_ref, dst_ref=_slab(full_ref, pid, size, axis), send_sem=send_sems.at[j],
                recv_sem=recv_sems.at[j], device_id=peer, device_id_type=MESH))
        return sends, recvs

    return copies


def _pass_on_copies(size, axis):
    def copies(shard_ref, full_ref, send_sems, recv_sems):
        x, y, c, me = _mesh_position()
        sibling, _ = _peer(x, y, c, SIBLING)
        sends, recvs = [], []
        for j, k in enumerate(SAME_CORE_OF_CHIPS):
            _, landed = _peer(x, y, c, k)
            _, siblings = _peer(x, y, c, k ^ SIBLING)
            mine = _slab(full_ref, landed, size, axis)
            sends.append(pltpu.make_async_remote_copy(
                src_ref=mine, dst_ref=mine, send_sem=send_sems.at[j], recv_sem=recv_sems.at[j],
                device_id=sibling, device_id_type=MESH))
            recvs.append(pltpu.make_async_remote_copy(
                src_ref=mine, dst_ref=_slab(full_ref, siblings, size, axis), send_sem=send_sems.at[j],
                recv_sem=recv_sems.at[j], device_id=sibling, device_id_type=MESH))
        return sends, recvs

    return copies


def _scatter_copies(size, axis):
    def copies(grad_ref, land_ref, send_sems, recv_sems):
        x, y, c, me = _mesh_position()
        sends, recvs = [], []
        for k in range(1, N_DEV):
            peer, pid = _peer(x, y, c, k)
            src = _slab(grad_ref, pid, size, axis)
            sends.append(pltpu.make_async_remote_copy(
                src_ref=src, dst_ref=land_ref.at[me], send_sem=send_sems.at[k - 1], recv_sem=recv_sems.at[k - 1],
                device_id=peer, device_id_type=MESH))
            recvs.append(pltpu.make_async_remote_copy(
                src_ref=src, dst_ref=land_ref.at[pid], send_sem=send_sems.at[k - 1], recv_sem=recv_sems.at[k - 1],
                device_id=peer, device_id_type=MESH))
        return sends, recvs

    return copies


PLACE_BANDS = 8


def _place_own(name, src, out_shape, in_spec, out_spec, steps, me):
    def body(me_ref, src_ref, out_ref):
        out_ref[...] = src_ref[...]

    return pl.pallas_call(
        body, name=name, out_shape=out_shape,
        grid_spec=pltpu.PrefetchScalarGridSpec(num_scalar_prefetch=1, grid=(steps,), in_specs=[in_spec],
                                               out_specs=out_spec),
        compiler_params=_params(("parallel",)),
    )(me.reshape(1).astype(jnp.int32), src)


def _gather_start(shard, axis, ks, me, after, tag):
    rows, cols = shard.shape
    size = shard.shape[axis]
    full_shape = tuple(N_DEV * n if a == axis else n for a, n in enumerate(shard.shape))
    band = rows // PLACE_BANDS
    in_spec = pl.BlockSpec((band, cols), lambda i, me_ref: (i, 0))
    if axis == 1:
        out_spec = pl.BlockSpec((band, cols), lambda i, me_ref: (i, me_ref[0]))
    else:
        out_spec = pl.BlockSpec((band, cols), lambda i, me_ref: (me_ref[0] * PLACE_BANDS + i, 0))
    full = _place_own("place_shard", shard, jax.ShapeDtypeStruct(full_shape, shard.dtype), in_spec, out_spec,
                      PLACE_BANDS, me)
    return _split_start("gather_start_" + tag, _gather_copies(size, axis, ks), len(ks), shard, full, after)


def _gather_wait(started, axis, ks, after, tag):
    size = started[2].shape[axis]
    return _split_wait("gather_wait_" + tag, _gather_copies(size, axis, ks), started, after)[1]


def _pass_on_start(shard, full, axis, after, tag):
    size = shard.shape[axis]
    return _split_start("pass_on_start_" + tag, _pass_on_copies(size, axis), len(SAME_CORE_OF_CHIPS), shard, full, after)


def _pass_on_wait(started, axis, after, tag):
    size = started[2].shape[axis]
    return _split_wait("pass_on_wait_" + tag, _pass_on_copies(size, axis), started, after)[1]


def _scatter_start(grad, axis, me, tag):
    size = grad.shape[axis] // N_DEV
    rows, cols = tuple(size if a == axis else n for a, n in enumerate(grad.shape))
    band = rows // PLACE_BANDS
    if axis == 1:
        in_spec = pl.BlockSpec((band, cols), lambda i, me_ref: (i, me_ref[0]))
    else:
        in_spec = pl.BlockSpec((band, cols), lambda i, me_ref: (me_ref[0] * PLACE_BANDS + i, 0))
    out_spec = pl.BlockSpec((None, band, cols), lambda i, me_ref: (me_ref[0], i, 0))
    land = _place_own("place_slab", grad, jax.ShapeDtypeStruct((N_DEV, rows, cols), grad.dtype), in_spec, out_spec,
                      PLACE_BANDS, me)
    return _split_start("scatter_start_" + tag, _scatter_copies(size, axis), N_DEV - 1, grad, land, [])


def _scatter_wait(started, axis, after, tag):
    size = started[2].shape[axis] // N_DEV
    land = _split_wait("scatter_wait_" + tag, _scatter_copies(size, axis), started, after)[1]
    return [(land, p) for p in range(N_DEV)]


N_CHIPS = N_DEV // 2


def _pair_copies(size, axis):
    def copies(grad_ref, land_ref, send_sems, recv_sems):
        x, y, c, me = _mesh_position()
        sibling, _ = _peer(x, y, c, SIBLING)
        sends, recvs = [], []
        for j in range(N_CHIPS):
            _, owner = _peer(x, y, c, (2 * j) ^ SIBLING)
            for lst in (sends, recvs):
                lst.append(pltpu.make_async_remote_copy(
                    src_ref=_slab(grad_ref, owner, size, axis), dst_ref=land_ref.at[j], send_sem=send_sems.at[j],
                    recv_sem=recv_sems.at[j], device_id=sibling, device_id_type=MESH))
        return sends, recvs

    return copies


def _chips_copies():
    def copies(pair_ref, land_ref, send_sems, recv_sems):
        x, y, c, me = _mesh_position()
        sends, recvs = [], []
        for j in range(1, N_CHIPS):
            owner, _ = _peer(x, y, c, 2 * j)
            for lst in (sends, recvs):
                lst.append(pltpu.make_async_remote_copy(
                    src_ref=pair_ref.at[j], dst_ref=land_ref.at[j], send_sem=send_sems.at[j - 1],
                    recv_sem=recv_sems.at[j - 1], device_id=owner, device_id_type=MESH))
        return sends, recvs

    return copies


def _pair_start(grad, axis, tag):
    size = grad.shape[axis] // N_DEV
    rows, cols = tuple(size if a == axis else n for a, n in enumerate(grad.shape))
    land = lax.empty((N_CHIPS, rows, cols), grad.dtype)
    return _split_start("pair_start_" + tag, _pair_copies(size, axis), N_CHIPS, grad, land, [])


def _pair_sums(started, axis, me, after, tag):
    size = started[2].shape[axis] // N_DEV
    grad, land = _split_wait("pair_wait_" + tag, _pair_copies(size, axis), started, after)
    _, rows, cols = land.shape
    band = rows // PLACE_BANDS
    if axis == 1:
        mine = pl.BlockSpec((band, cols), lambda j, i, me_ref: (i, me_ref[0] ^ (2 * j)))
    else:
        mine = pl.BlockSpec((band, cols), lambda j, i, me_ref: ((me_ref[0] ^ (2 * j)) * PLACE_BANDS + i, 0))
    slot = pl.BlockSpec((None, band, cols), lambda j, i, me_ref: (j, i, 0))

    def body(own_ref, mine_ref, theirs_ref, out_ref):
        out_ref[...] = (mine_ref[...].astype(F32) + theirs_ref[...].astype(F32)).astype(out_ref.dtype)

    return pl.pallas_call(
        body, name="pair_sums", out_shape=jax.ShapeDtypeStruct(land.shape, land.dtype),
        grid_spec=pltpu.PrefetchScalarGridSpec(num_scalar_prefetch=1, grid=(N_CHIPS, PLACE_BANDS),
                                               in_specs=[mine, slot], out_specs=slot),
        compiler_params=_params(("parallel", "parallel")),
    )(me.reshape(1).astype(jnp.int32), grad, land)


def _chips_start(pairs, tag):
    return _split_start("chips_start_" + tag, _chips_copies(), N_CHIPS - 1, pairs, lax.empty(pairs.shape, pairs.dtype), [])


def _chips_wait(started, after, tag):
    pairs, land = _split_wait("chips_wait_" + tag, _chips_copies(), started, after)
    return [(pairs, 0)] + [(land, j) for j in range(1, N_CHIPS)]


def _exchange_small(buf, *, name, after=()):
    r = buf.shape[0]

    def body(*refs):
        buf_ref = refs[0]
        all_ref, sum_ref, send_sems, recv_sems = refs[1 + len(after):]
        x, y, c, me = _mesh_position()
        all_ref[me] = buf_ref[...]
        sends, recvs = [], []
        for k in range(1, N_DEV):
            peer, pid = _peer(x, y, c, k)
            sends.append(pltpu.make_async_remote_copy(
                src_ref=buf_ref, dst_ref=all_ref.at[me], send_sem=send_sems.at[k - 1], recv_sem=recv_sems.at[k - 1],
                device_id=peer, device_id_type=MESH))
            recvs.append(pltpu.make_async_remote_copy(
                src_ref=buf_ref, dst_ref=all_ref.at[pid], send_sem=send_sems.at[k - 1], recv_sem=recv_sems.at[k - 1],
                device_id=peer, device_id_type=MESH))
        for cp in sends:
            cp.start()
        for cp in recvs:
            cp.wait_recv()
        for cp in sends:
            cp.wait_send()
        total = all_ref[0]
        for p in range(1, N_DEV):
            total = total + all_ref[p]
        sum_ref[...] = total

    vmem = pl.BlockSpec(memory_space=pltpu.VMEM)
    return pl.pallas_call(
        body, name=name,
        in_specs=[vmem] + [ANY_SPEC] * len(after), out_specs=[vmem, vmem],
        out_shape=[jax.ShapeDtypeStruct((N_DEV, r, LANES), F32), jax.ShapeDtypeStruct((r, LANES), F32)],
        scratch_shapes=[pltpu.SemaphoreType.DMA((N_DEV - 1,)), pltpu.SemaphoreType.DMA((N_DEV - 1,))],
        compiler_params=pltpu.CompilerParams(has_side_effects=True),
    )(buf, *after)


def _adamw_math(w, g, m, v):
    m2 = ADAM_B1 * m + (1.0 - ADAM_B1) * g
    v2 = ADAM_B2 * v + (1.0 - ADAM_B2) * (g * g)
    delta = -ADAM_LR * ((m2 / ADAM_C1) / (jnp.sqrt(v2 / ADAM_C2) + ADAM_EPS) + ADAM_WD * w)
    return delta, m2, v2


def _adamw_slabs(layer, w, m, v, addends, outs, order, *, tr, name):
    depth, r, c = w.shape
    tr = _tile(r, tr)
    n = len(addends)

    def body(*refs):
        w_ref, m_ref, v_ref = refs[:3]
        g_ref, d_ref, m2_ref, v2_ref = refs[-4:]
        g = refs[3][...].astype(F32)
        for a_ref in refs[4:3 + n]:
            g = g + a_ref[...].astype(F32)
        delta, m2, v2 = _adamw_math(w_ref[...], g, m_ref[...], v_ref[...])
        g_ref[...] = g
        d_ref[...] = delta
        m2_ref[...] = m2
        v2_ref[...] = v2

    row = pl.BlockSpec((None, tr, c), lambda i: (layer, i, 0))
    slots = [pl.BlockSpec((None, tr, c), lambda i, p=p: (p, i, 0)) for _, p in addends]
    first_out = 3 + n + 1
    return pl.pallas_call(
        body, name=name, grid=(r // tr,),
        in_specs=[row, row, row] + slots + [pl.BlockSpec((8, LANES), lambda i: (0, 0))] + [ANY_SPEC] * 4,
        out_specs=[row] * 4, out_shape=[jax.ShapeDtypeStruct((depth, r, c), F32)] * 4,
        input_output_aliases={first_out + t: t for t in range(4)},
        compiler_params=_params(("parallel",)),
    )(w, m, v, *[a for a, _ in addends], order, *outs)


def _adamw_small(w, g, m, v):
    def body(w_ref, g_ref, m_ref, v_ref, d_ref, m2_ref, v2_ref):
        delta, m2, v2 = _adamw_math(w_ref[...], g_ref[...], m_ref[...], v_ref[...])
        d_ref[...] = delta
        m2_ref[...] = m2
        v2_ref[...] = v2

    vmem = pl.BlockSpec(memory_space=pltpu.VMEM)
    return pl.pallas_call(
        body, name="adamw_small", in_specs=[vmem] * 4, out_specs=[vmem] * 3,
        out_shape=[jax.ShapeDtypeStruct(w.shape, F32)] * 3,
    )(w, g, m, v)


def _pack(parts):
    flat = jnp.concatenate([p.reshape(-1).astype(F32) for p in parts])
    rows = -(-flat.shape[0] // LANES)
    rows = -(-rows // SMALL_ROWS_ALIGN) * SMALL_ROWS_ALIGN
    flat = jnp.pad(flat, (0, rows * LANES - flat.shape[0]))
    return flat.reshape(rows, LANES)


def _unpack(buf, shapes):
    flat = buf.reshape(-1)
    out, pos = [], 0
    for shp in shapes:
        size = math.prod(shp)
        out.append(flat[pos:pos + size].reshape(shp))
        pos += size
    return out


def _section_widths(d):
    aw = d // 2
    kw = aw // ATTN_GROUP
    rw = d - aw
    rqw = (rw // RET_V_DIM) * RET_QK_DIM
    return (aw, kw, kw, aw, rqw, rqw, rw, rw)


def _layer_fwd(xl, hh, win_full, behind, after_attn, wout_of, qn, kn, dec_f, dec_b, rn, cos, sin):
    h, ht = hh
    aq, ak, v, ag, rq, rk, rvb, rg = _proj_sections(h, win_full, _section_widths(xl.shape[1]),
                                                    (F32, F32, BF16, F32, F32, F32, BF16, F32), after=behind)
    q, k, rqr, rkr = _prep_fwd(aq, ak, rq, rk, cos, sin, qn, kn)
    att, lse = _attn_fwd(q, k, v)
    ret = _retc_fwd(rqr, rkr, rvb, dec_f + after_attn(att), dec_b)
    y, yt = _gate_fwd(att, ag, ret, rg, rn)
    wout_full = wout_of(y)
    xn = _matmul(y, wout_full, name="out_proj", residual=xl)
    saved = dict(x=xl, ht=ht, aq=aq, ak=ak, ag=ag, rg=rg, q=q, k=k, v=v, rq=rqr, rk=rkr, rv=rvb,
                 att=att, lse=lse, ret=ret, yt=yt, win=win_full, wout=wout_full)
    return xn, saved


def _layer_bwd_weights(gb, sv, qn, kn, dec_f, dec_b, rn, cos, sin, on_dwout):
    dy = _matmul(gb, sv["wout"], name="d_y", trans_b=True)
    dwout = _matmul(sv["yt"], gb, name="d_wout", out_dtype=BF16)
    datt, dag, dret, drg, drn = _gate_bwd(dy, sv["att"], sv["ag"], sv["ret"], sv["rg"], rn + on_dwout(dwout))
    dq, dk, dav = _attn_bwd(sv["q"], sv["k"], sv["v"], sv["att"], datt, sv["lse"])
    drq, drk, drv, gf, gbk = _retc_bwd(sv["rq"], sv["rk"], sv["rv"], dret, dec_f, dec_b)
    dproj, dqn, dkn = _prep_bwd(dq, dk, drq, drk, sv["aq"], sv["ak"], cos, sin, qn, kn, dav, dag, drv, drg)
    dwin = _matmul(sv["ht"], dproj, name="d_win", out_dtype=BF16)
    small = dict(qn=dqn[0], kn=dkn[0], df=gf[:, 0, 0], db=gbk[:, 0, 0], rn=drn[0])
    return dproj, dwin, small


def _layer_bwd_input(g, dproj, sv, nw, behind, after_dh):
    dh = _matmul(dproj, sv["win"], name="d_h", trans_b=True, tm=512, tk=dproj.shape[1], after=behind)
    g, gb, dnw = _rms_bwd(dh, sv["x"], g, nw + after_dh(dh))
    return g, gb, dnw[0]


def kernel(x, norm_w, w_in, q_norm, k_norm, ret_decay_fwd, ret_decay_bwd, ret_norm, w_out, final_norm, loss_target, m_norm_w, m_w_in, m_q_norm, m_k_norm, m_ret_decay_fwd, m_ret_decay_bwd, m_ret_norm, m_w_out, m_final_norm, v_norm_w, v_w_in, v_q_norm, v_k_norm, v_ret_decay_fwd, v_ret_decay_bwd, v_ret_norm, v_w_out, v_final_norm):
    depth, d, _ = w_in.shape
    seq = x.shape[1]
    rw = _section_widths(d)[6]
    rheads = rw // RET_V_DIM
    rns = ret_norm.shape[-1]
    _, _, _, me = _mesh_position()

    target = loss_target[0]
    cos, sin = _rope_tables(seq)

    rn_all, _ = _exchange_small(_pack([ret_norm]), name="gather_ret_norm")
    rn_full = rn_all.reshape(N_DEV, -1)[:, :depth * rheads * rns].reshape(N_DEV, depth, rheads, rns)
    rn_full = jnp.transpose(rn_full, (1, 2, 0, 3)).reshape(depth, rw)

    dec_f = jnp.broadcast_to(ret_decay_fwd[:, :, None, None], (depth, rheads, 1, LANES))
    dec_b = jnp.broadcast_to(ret_decay_bwd[:, :, None, None], (depth, rheads, 1, LANES))

    win_bf = [w_in[l].astype(BF16) for l in range(depth)]
    wout_bf = [w_out[l].astype(BF16) for l in range(depth)]

    saved = []
    xl = x[0]
    first = (SIBLING,) + SAME_CORE_OF_CHIPS
    in_sent = _gather_start(win_bf[0], 1, first, me, [], "in0")
    hh = _rms_fwd(xl, norm_w[0] + in_sent[-1][0, 0])
    landed = _gather_wait(in_sent, 1, first, [hh[0]], "in0")
    win_full = _pass_on_wait(_pass_on_start(win_bf[0], landed, 1, [], "in0"), 1, [], "in0")
    for l in range(depth):
        if l > 0:
            hh = _rms_fwd(xl, norm_w[l])
        out_sent = _gather_start(wout_bf[l], 0, ALL_PEERS, me, [win_full], "out" + str(l))
        behind = [out_sent[-1]]
        passed = {}
        if l + 1 < depth:
            in_sent = _gather_start(win_bf[l + 1], 1, first, me, [win_full, out_sent[-1]], "in" + str(l + 1))
            behind.append(in_sent[-1])

        def after_attn(att, passed=passed, l=l):
            if l + 1 == depth:
                return 0.0
            landed = _gather_wait(in_sent, 1, first, [att], "in" + str(l + 1))
            passed["on"] = _pass_on_start(win_bf[l + 1], landed, 1, [], "in" + str(l + 1))
            return passed["on"][-1][0, 0]

        def wout_of(y, out_sent=out_sent, l=l):
            return _gather_wait(out_sent, 0, ALL_PEERS, [y], "out" + str(l))

        xl, sv = _layer_fwd(xl, hh, win_full, behind, after_attn, wout_of, q_norm[l], k_norm[l], dec_f[l], dec_b[l],
                            rn_full[l], cos, sin)
        saved.append(sv)
        if l + 1 < depth:
            win_full = _pass_on_wait(passed["on"], 1, [xl], "in" + str(l + 1))

    loss_row, g, gb, d_final = _loss_head(xl, target, final_norm)

    d_norm, d_qn, d_kn, d_df, d_db, d_rn = [], [], [], [], [], []
    lands = [None] * depth
    pending = None
    for l in reversed(range(depth)):
        sent = {}

        def on_dwout(dwout, sent=sent, l=l):
            sent["out"] = _scatter_start(dwout, 0, me, "out" + str(l))
            return sent["out"][-1][0, 0]

        def after_dh(dh, sent=sent, l=l):
            pairs = _pair_sums(sent["pair"], 1, me, [dh], "in" + str(l))
            sent["in"] = _chips_start(pairs, "in" + str(l))
            return sent["in"][-1][0, 0]

        dproj, dwin, sm = _layer_bwd_weights(gb, saved[l], q_norm[l], k_norm[l], dec_f[l], dec_b[l],
                                             rn_full[l], cos, sin, on_dwout)
        sent["pair"] = _pair_start(dwin, 1, "in" + str(l))
        g, gb, dnw = _layer_bwd_input(g, dproj, saved[l], norm_w[l], [sent["pair"][-1]], after_dh)
        if pending is not None:
            lands[l + 1] = (_chips_wait(pending["in"], [g], "in" + str(l + 1)),
                            _scatter_wait(pending["out"], 0, [g], "out" + str(l + 1)))
        pending = sent
        d_norm.append(dnw)
        d_qn.append(sm["qn"])
        d_kn.append(sm["kn"])
        d_df.append(sm["df"])
        d_db.append(sm["db"])
        d_rn.append(sm["rn"])
    for lst in (d_norm, d_qn, d_kn, d_df, d_db, d_rn):
        lst.reverse()
    order = pending["in"][-1]
    in_outs = [lax.empty(w_in.shape, F32) for _ in range(4)]
    out_outs = [lax.empty(w_out.shape, F32) for _ in range(4)]
    for l in reversed(range(depth)):
        if l == 0:
            lands[0] = (_chips_wait(pending["in"], [g, in_outs[0], out_outs[0]], "in0"),
                        _scatter_wait(pending["out"], 0, [g], "out0"))
        in_outs = _adamw_slabs(l, w_in, m_w_in, v_w_in, lands[l][0], in_outs, order, tr=256, name="adamw_w_in")
        out_outs = _adamw_slabs(l, w_out, m_w_out, v_w_out, lands[l][1], out_outs, order, tr=64, name="adamw_w_out")

    small_shapes = [(depth, d), (depth, HEAD_DIM), (depth, HEAD_DIM), (depth, rheads), (depth, rheads),
                    (depth, rheads, N_DEV * rns), (d,), (1,)]
    grads_local = [jnp.stack(d_norm), jnp.stack(d_qn), jnp.stack(d_kn), jnp.stack(d_df), jnp.stack(d_db),
                   jnp.stack(d_rn).reshape(depth, rheads, N_DEV * rns), d_final[0], loss_row[0, :1]]
    _, gsum = _exchange_small(_pack(grads_local), name="all_reduce_small", after=(in_outs[0], out_outs[0]))
    g_norm, g_qn, g_kn, g_df, g_db, g_rn_full, g_final, loss = _unpack(gsum, small_shapes)
    g_rn = lax.dynamic_slice_in_dim(g_rn_full, me * rns, rns, axis=2)
    small_g = [g_norm, g_qn, g_kn, g_df, g_db, g_rn, g_final]
    small_w = [norm_w, q_norm, k_norm, ret_decay_fwd, ret_decay_bwd, ret_norm, final_norm]
    small_m = [m_norm_w, m_q_norm, m_k_norm, m_ret_decay_fwd, m_ret_decay_bwd, m_ret_norm, m_final_norm]
    small_v = [v_norm_w, v_q_norm, v_k_norm, v_ret_decay_fwd, v_ret_decay_bwd, v_ret_norm, v_final_norm]
    shapes = [a.shape for a in small_w]
    sd, sm, sv2 = _adamw_small(_pack(small_w), _pack(small_g), _pack(small_m), _pack(small_v))
    small_d, small_m2, small_v2 = _unpack(sd, shapes), _unpack(sm, shapes), _unpack(sv2, shapes)

    def ordered(small, win_v, wout_v):
        return [small[0], win_v, small[1], small[2], small[3], small[4], small[5], wout_v, small[6]]

    grads = ordered(small_g, in_outs[0], out_outs[0])
    deltas = ordered(small_d, in_outs[1], out_outs[1])
    new_m = ordered(small_m2, in_outs[2], out_outs[2])
    new_v = ordered(small_v2, in_outs[3], out_outs[3])
    return (loss.reshape(()), g[None], *grads, *deltas, *new_m, *new_v)
```

```python
import math

import jax
import jax.numpy as jnp
import numpy as np
from jax import lax
from jax.experimental import pallas as pl
from jax.experimental.pallas import tpu as pltpu

F32 = jnp.float32
BF16 = jnp.bfloat16

N_DEV = 8
HEAD_DIM = 128
ATTN_GROUP = 4
RET_QK_DIM = 128
RET_V_DIM = 256
GRID_W = 64
ROPE_THETA = 10000.0
EPS = 1e-6
ADAM_LR = 0.001
ADAM_B1 = 0.9
ADAM_B2 = 0.999
ADAM_EPS = 1e-08
ADAM_WD = 0.01
ADAM_STEP = 10
ADAM_C1 = 1.0 - ADAM_B1 ** ADAM_STEP
ADAM_C2 = 1.0 - ADAM_B2 ** ADAM_STEP
LANES = 128
SMALL_ROWS_ALIGN = 8
VMEM_LIMIT = 56 * 1024 * 1024

NT_DIMS = (((1,), (1,)), ((), ()))
TN_DIMS = (((0,), (0,)), ((), ()))
MESH = pl.DeviceIdType.MESH


def _params(sem):
    return pltpu.CompilerParams(dimension_semantics=sem, vmem_limit_bytes=VMEM_LIMIT)


def _tile(dim, pref, align=16):
    if dim <= pref:
        return dim
    for t in range(pref - pref % align, 0, -align):
        if dim % t == 0:
            return t
    raise ValueError((dim, pref, align))


def _silu_parts(z):
    sg = 1.0 / (1.0 + jnp.exp(-z))
    return z * sg, sg * (1.0 + z * (1.0 - sg))


def _log_sigmoid(x):
    return jnp.minimum(x, 0.0) - jnp.log(1.0 + jnp.exp(-jnp.abs(x)))


def _swap_pairs(z):
    src = lax.broadcasted_iota(jnp.int32, (HEAD_DIM, HEAD_DIM), 0)
    dst = lax.broadcasted_iota(jnp.int32, (HEAD_DIM, HEAD_DIM), 1)
    partner = jnp.where((dst % 64) < 32, dst + 32, dst - 32)
    perm = (src == partner).astype(F32)
    return jnp.dot(z, perm, precision=lax.Precision.HIGH, preferred_element_type=F32)


def _rope(z, cos, sin):
    return z * cos + _swap_pairs(z) * sin


def _rope_transposed(d, cos, sin):
    return d * cos + _swap_pairs(d * sin)


def _rope_tables(seq):
    rows = seq // GRID_W
    row = jnp.repeat(jnp.arange(rows), GRID_W).astype(F32)
    col = jnp.tile(jnp.arange(GRID_W), rows).astype(F32)
    axis_dim = HEAD_DIM // 2
    inv = ROPE_THETA ** (-jnp.arange(0, axis_dim, 2, dtype=F32) / axis_dim)
    ar = row[:, None] * inv[None, :]
    ac = col[:, None] * inv[None, :]
    cos = jnp.concatenate([jnp.cos(ar), jnp.cos(ar), jnp.cos(ac), jnp.cos(ac)], axis=-1)
    sin = jnp.concatenate([-jnp.sin(ar), jnp.sin(ar), -jnp.sin(ac), jnp.sin(ac)], axis=-1)
    return cos, sin


def _matmul(a, b, *, name, trans_b=False, out_dtype=F32, residual=None, tm=1024, tn=512, tk=4096, after=()):
    m, k = a.shape
    n = b.shape[0] if trans_b else b.shape[1]
    tm, tn, tk = _tile(m, tm), _tile(n, tn, LANES), _tile(k, tk, LANES)
    nk = k // tk
    has_res = residual is not None

    def body(*refs):
        a_ref, b_ref = refs[:2]
        r_ref = refs[2] if has_res else None
        o_ref = refs[2 + has_res + len(after)]
        if trans_b:
            part = lax.dot_general(a_ref[...], b_ref[...], NT_DIMS, preferred_element_type=F32)
        else:
            part = jnp.dot(a_ref[...], b_ref[...], preferred_element_type=F32)

        def finish(r):
            if has_res:
                r = r + r_ref[...]
            o_ref[...] = r.astype(o_ref.dtype)

        if nk == 1:
            finish(part)
        else:
            acc_ref = refs[-1]
            kk = pl.program_id(2)

            @pl.when(kk == 0)
            def _():
                acc_ref[...] = part

            @pl.when(kk > 0)
            def _():
                acc_ref[...] += part

            @pl.when(kk == nk - 1)
            def _():
                finish(acc_ref[...])

    if trans_b:
        b_spec = pl.BlockSpec((tn, tk), lambda i, j, kk: (j, kk))
    else:
        b_spec = pl.BlockSpec((tk, tn), lambda i, j, kk: (kk, j))
    in_specs = [pl.BlockSpec((tm, tk), lambda i, j, kk: (i, kk)), b_spec]
    args = [a, b]
    if has_res:
        in_specs.append(pl.BlockSpec((tm, tn), lambda i, j, kk: (i, j)))
        args.append(residual)
    in_specs += [ANY_SPEC] * len(after)
    args += list(after)
    return pl.pallas_call(
        body, name=name, grid=(m // tm, n // tn, nk),
        in_specs=in_specs,
        out_specs=pl.BlockSpec((tm, tn), lambda i, j, kk: (i, j)),
        out_shape=jax.ShapeDtypeStruct((m, n), out_dtype),
        scratch_shapes=[pltpu.VMEM((tm, tn), F32)] if nk > 1 else [],
        compiler_params=_params(("parallel", "parallel", "arbitrary")),
    )(*args)


def _proj_sections(a, b, widths, dtypes, *, tm=1024, tn=512, after=()):
    m, k = a.shape
    tm = _tile(m, tm)
    tn = _tile(min(widths), tn, LANES)
    assert all(w % tn == 0 for w in widths) and sum(widths) == b.shape[1]
    nblk = [w // tn for w in widths]
    first = [int(o) // tn for o in np.cumsum((0,) + tuple(widths))[:-1]]

    def body(a_ref, b_ref, *refs):
        j = pl.program_id(1)
        part = jnp.dot(a_ref[...], b_ref[...], preferred_element_type=F32)
        for o_ref, lo, n in zip(refs[len(after):], first, nblk):
            @pl.when(jnp.logical_and(j >= lo, j < lo + n))
            def _(o_ref=o_ref):
                o_ref[...] = part.astype(o_ref.dtype)

    out_specs = [pl.BlockSpec((tm, tn), lambda i, j, lo=lo, n=n: (i, jnp.clip(j - lo, 0, n - 1)))
                 for lo, n in zip(first, nblk)]
    return pl.pallas_call(
        body, name="proj", grid=(m // tm, b.shape[1] // tn),
        in_specs=[pl.BlockSpec((tm, k), lambda i, j: (i, 0)), pl.BlockSpec((k, tn), lambda i, j: (0, j))]
        + [ANY_SPEC] * len(after),
        out_specs=out_specs,
        out_shape=[jax.ShapeDtypeStruct((m, w), dt) for w, dt in zip(widths, dtypes)],
        compiler_params=_params(("arbitrary", "arbitrary")),
    )(a, b, *after)


def _rms_fwd(x, w, *, ts=256):
    s, d = x.shape
    ts = _tile(s, ts)

    def body(x_ref, w_ref, h_ref, ht_ref):
        xv = x_ref[...]
        r = lax.rsqrt(jnp.mean(xv * xv, axis=-1, keepdims=True) + EPS)
        h = xv * r * w_ref[...]
        h_ref[...] = h.astype(BF16)
        ht_ref[...] = h.T.astype(BF16)

    row = pl.BlockSpec((ts, d), lambda i: (i, 0))
    return pl.pallas_call(
        body, name="rms_fwd", grid=(s // ts,),
        in_specs=[row, pl.BlockSpec((1, d), lambda i: (0, 0))],
        out_specs=[row, pl.BlockSpec((d, ts), lambda i: (0, i))],
        out_shape=[jax.ShapeDtypeStruct((s, d), BF16), jax.ShapeDtypeStruct((d, s), BF16)],
        compiler_params=_params(("parallel",)),
    )(x, w.reshape(1, d))


def _rms_bwd(dh, x, g, w, *, ts=256):
    s, d = x.shape
    ts = _tile(s, ts)

    def body(dh_ref, x_ref, g_ref, w_ref, dx_ref, dxb_ref, dw_ref):
        xv = x_ref[...]
        r = lax.rsqrt(jnp.mean(xv * xv, axis=-1, keepdims=True) + EPS)
        xh = xv * r
        dhv = dh_ref[...]
        dn = dhv * w_ref[...]
        dx = g_ref[...] + r * (dn - xh * jnp.mean(dn * xh, axis=-1, keepdims=True))
        dx_ref[...] = dx
        dxb_ref[...] = dx.astype(BF16)
        part = jnp.sum(dhv * xh, axis=0, keepdims=True)

        @pl.when(pl.program_id(0) == 0)
        def _():
            dw_ref[...] = part

        @pl.when(pl.program_id(0) > 0)
        def _():
            dw_ref[...] += part

    row = pl.BlockSpec((ts, d), lambda i: (i, 0))
    vec = pl.BlockSpec((1, d), lambda i: (0, 0))
    return pl.pallas_call(
        body, name="rms_bwd", grid=(s // ts,),
        in_specs=[row, row, row, vec],
        out_specs=[row, row, vec],
        out_shape=[jax.ShapeDtypeStruct((s, d), F32), jax.ShapeDtypeStruct((s, d), BF16),
                   jax.ShapeDtypeStruct((1, d), F32)],
        compiler_params=_params(("arbitrary",)),
    )(dh, x, g, w.reshape(1, d))


def _loss_head(x, target, w, *, ts=256):
    s, d = x.shape
    ts = _tile(s, ts)

    def body(x_ref, t_ref, w_ref, loss_ref, dx_ref, dxb_ref, dw_ref):
        xv = x_ref[...]
        r = lax.rsqrt(jnp.mean(xv * xv, axis=-1, keepdims=True) + EPS)
        xh = xv * r
        wv = w_ref[...]
        diff = xh * wv - t_ref[...]
        lpart = 0.5 * jnp.sum(jnp.mean(diff * diff, axis=-1, keepdims=True), axis=0, keepdims=True)
        dout = diff * (1.0 / d)
        dn = dout * wv
        dx = r * (dn - xh * jnp.mean(dn * xh, axis=-1, keepdims=True))
        dx_ref[...] = dx
        dxb_ref[...] = dx.astype(BF16)
        part = jnp.sum(dout * xh, axis=0, keepdims=True)
        lrow = jnp.broadcast_to(lpart, loss_ref.shape)

        @pl.when(pl.program_id(0) == 0)
        def _():
            dw_ref[...] = part
            loss_ref[...] = lrow

        @pl.when(pl.program_id(0) > 0)
        def _():
            dw_ref[...] += part
            loss_ref[...] += lrow

    row = pl.BlockSpec((ts, d), lambda i: (i, 0))
    vec = pl.BlockSpec((1, d), lambda i: (0, 0))
    return pl.pallas_call(
        body, name="loss_head", grid=(s // ts,),
        in_specs=[row, row, vec],
        out_specs=[pl.BlockSpec((1, LANES), lambda i: (0, 0)), row, row, vec],
        out_shape=[jax.ShapeDtypeStruct((1, LANES), F32), jax.ShapeDtypeStruct((s, d), F32),
                   jax.ShapeDtypeStruct((s, d), BF16), jax.ShapeDtypeStruct((1, d), F32)],
        compiler_params=_params(("arbitrary",)),
    )(x, target, w.reshape(1, d))


def _prep_fwd(aq, ak, rq, rk, cos, sin, qw, kw, *, ts=256):
    s = aq.shape[0]
    ts = _tile(s, ts)
    attn_scale = HEAD_DIM ** -0.5
    ret_scale = RET_QK_DIM ** -0.5
    nq, nk, nr = aq.shape[1] // HEAD_DIM, ak.shape[1] // HEAD_DIM, rq.shape[1] // RET_QK_DIM

    def body(aq_ref, ak_ref, rq_ref, rk_ref, cos_ref, sin_ref, qw_ref, kw_ref,
             q_out, k_out, rq_out, rk_out):
        c, sn = cos_ref[...], sin_ref[...]

        def normed(u, w):
            return u * lax.rsqrt(jnp.mean(u * u, axis=-1, keepdims=True) + EPS) * w

        for j in range(nq):
            sl = slice(j * HEAD_DIM, (j + 1) * HEAD_DIM)
            q_out[:, sl] = (_rope(normed(aq_ref[:, sl], qw_ref[...]), c, sn) * attn_scale).astype(BF16)
        for j in range(nk):
            sl = slice(j * HEAD_DIM, (j + 1) * HEAD_DIM)
            k_out[:, sl] = _rope(normed(ak_ref[:, sl], kw_ref[...]), c, sn).astype(BF16)
        for j in range(nr):
            sl = slice(j * RET_QK_DIM, (j + 1) * RET_QK_DIM)
            rq_out[:, sl] = _rope(rq_ref[:, sl], c, sn).astype(BF16)
            rk_out[:, sl] = (_rope(rk_ref[:, sl], c, sn) * ret_scale).astype(BF16)

    def row(arr):
        return pl.BlockSpec((ts, arr.shape[1]), lambda i: (i, 0))

    vec = pl.BlockSpec((1, HEAD_DIM), lambda i: (0, 0))
    ins = [aq, ak, rq, rk]
    return pl.pallas_call(
        body, name="prep_fwd", grid=(s // ts,),
        in_specs=[row(a) for a in ins] + [row(cos), row(sin), vec, vec],
        out_specs=[row(a) for a in ins],
        out_shape=[jax.ShapeDtypeStruct(a.shape, BF16) for a in ins],
        compiler_params=_params(("parallel",)),
    )(*ins, cos, sin, qw.reshape(1, HEAD_DIM), kw.reshape(1, HEAD_DIM))


def _prep_bwd(dq, dk, drq, drk, aq, ak, cos, sin, qw, kw, dav, dag, drv, drg, *, ts=256):
    s = aq.shape[0]
    ts = _tile(s, ts)
    attn_scale = HEAD_DIM ** -0.5
    ret_scale = RET_QK_DIM ** -0.5
    nq, nk, nr = aq.shape[1] // HEAD_DIM, ak.shape[1] // HEAD_DIM, drq.shape[1] // RET_QK_DIM
    widths = (aq.shape[1], ak.shape[1], dav.shape[1], dag.shape[1], drq.shape[1], drk.shape[1], drv.shape[1],
              drg.shape[1])
    o_aq, o_ak, o_av, o_ag, o_rq, o_rk, o_rv, o_rg = (int(o) for o in np.cumsum((0,) + widths)[:-1])

    def body(dq_ref, dk_ref, drq_ref, drk_ref, aq_ref, ak_ref, cos_ref, sin_ref, dav_ref, dag_ref, drv_ref, drg_ref,
             qw_ref, kw_ref, dproj_ref, dqw_ref, dkw_ref):
        c, sn = cos_ref[...], sin_ref[...]
        for ref, off in ((dav_ref, o_av), (dag_ref, o_ag), (drv_ref, o_rv), (drg_ref, o_rg)):
            dproj_ref[:, off:off + ref.shape[1]] = ref[...]

        def unrope(d):
            return _rope_transposed(d, c, sn)

        def norm_bwd(dun, u, w):
            r = lax.rsqrt(jnp.mean(u * u, axis=-1, keepdims=True) + EPS)
            uh = u * r
            dn = dun * w
            du = r * (dn - uh * jnp.mean(dn * uh, axis=-1, keepdims=True))
            return du, jnp.sum(dun * uh, axis=0, keepdims=True)

        dqw = jnp.zeros((1, HEAD_DIM), F32)
        for j in range(nq):
            sl = slice(j * HEAD_DIM, (j + 1) * HEAD_DIM)
            du, dw = norm_bwd(unrope(dq_ref[:, sl] * attn_scale), aq_ref[:, sl], qw_ref[...])
            dproj_ref[:, o_aq + j * HEAD_DIM:o_aq + (j + 1) * HEAD_DIM] = du.astype(BF16)
            dqw = dqw + dw
        dkw = jnp.zeros((1, HEAD_DIM), F32)
        for j in range(nk):
            sl = slice(j * HEAD_DIM, (j + 1) * HEAD_DIM)
            du, dw = norm_bwd(unrope(dk_ref[:, sl]), ak_ref[:, sl], kw_ref[...])
            dproj_ref[:, o_ak + j * HEAD_DIM:o_ak + (j + 1) * HEAD_DIM] = du.astype(BF16)
            dkw = dkw + dw
        for j in range(nr):
            sl = slice(j * RET_QK_DIM, (j + 1) * RET_QK_DIM)
            dproj_ref[:, o_rq + j * RET_QK_DIM:o_rq + (j + 1) * RET_QK_DIM] = unrope(drq_ref[:, sl]).astype(BF16)
            dproj_ref[:, o_rk + j * RET_QK_DIM:o_rk + (j + 1) * RET_QK_DIM] = (
                unrope(drk_ref[:, sl] * ret_scale).astype(BF16))

        @pl.when(pl.program_id(0) == 0)
        def _():
            dqw_ref[...] = dqw
            dkw_ref[...] = dkw

        @pl.when(pl.program_id(0) > 0)
        def _():
            dqw_ref[...] += dqw
            dkw_ref[...] += dkw

    def row(arr):
        return pl.BlockSpec((ts, arr.shape[1]), lambda i: (i, 0))

    vec = pl.BlockSpec((1, HEAD_DIM), lambda i: (0, 0))
    ins = [dq, dk, drq, drk, aq, ak, cos, sin, dav, dag, drv, drg]
    total = sum(widths)
    return pl.pallas_call(
        body, name="prep_bwd", grid=(s // ts,),
        in_specs=[row(a) for a in ins] + [vec, vec],
        out_specs=[pl.BlockSpec((ts, total), lambda i: (i, 0)), vec, vec],
        out_shape=[jax.ShapeDtypeStruct((s, total), BF16)] + [jax.ShapeDtypeStruct((1, HEAD_DIM), F32)] * 2,
        compiler_params=_params(("arbitrary",)),
    )(*ins, qw.reshape(1, HEAD_DIM), kw.reshape(1, HEAD_DIM))


def _attn_fwd(q, k, v, *, tq=4096, sub=256):
    s, aw = q.shape
    tq = _tile(s, tq)
    sub = _tile(tq, sub)
    heads, kvh = aw // HEAD_DIM, k.shape[1] // HEAD_DIM
    grp = heads // kvh

    def body(q_ref, k_ref, v_ref, o_ref, lse_ref):
        kv_ = k_ref[...]
        v_ext = jnp.concatenate([v_ref[...], jnp.ones((s, HEAD_DIM), BF16)], axis=-1)
        for r in range(tq // sub):
            rows = slice(r * sub, (r + 1) * sub)
            sc = lax.dot_general(q_ref[rows, :], kv_, NT_DIMS, preferred_element_type=F32)
            m = jnp.max(sc, axis=-1, keepdims=True)
            p = jnp.exp((sc - m).astype(BF16))
            oe = jnp.dot(p, v_ext, preferred_element_type=F32)
            l = oe[:, HEAD_DIM:HEAD_DIM + 1]
            o_ref[rows, :] = (oe[:, :HEAD_DIM] / l).astype(o_ref.dtype)
            lse_ref[rows, :] = jnp.broadcast_to(m + jnp.log(l), (sub, HEAD_DIM))

    qspec = pl.BlockSpec((tq, HEAD_DIM), lambda kv, g, i: (i, kv * grp + g))
    kspec = pl.BlockSpec((s, HEAD_DIM), lambda kv, g, i: (0, kv))
    return pl.pallas_call(
        body, name="attn_fwd", grid=(kvh, grp, s // tq),
        in_specs=[qspec, kspec, kspec],
        out_specs=[qspec, qspec],
        out_shape=[jax.ShapeDtypeStruct((s, aw), BF16), jax.ShapeDtypeStruct((s, aw), F32)],
        compiler_params=_params(("parallel", "parallel", "parallel")),
    )(q, k, v)


def _attn_bwd(q, k, v, o, do, lse, *, tq=1024, sub=256):
    s, aw = q.shape
    tq = _tile(s, tq)
    sub = _tile(tq, sub)
    heads, kvh = aw // HEAD_DIM, k.shape[1] // HEAD_DIM
    grp = heads // kvh
    nq = s // tq

    def body(q_ref, k_ref, v_ref, o_ref, do_ref, lse_ref, dq_ref, dk_ref, dv_ref, dk_acc, dv_acc, p_scr, ds_scr):
        g, i = pl.program_id(1), pl.program_id(2)
        kv_, vv = k_ref[...], v_ref[...]
        for r in range(tq // sub):
            rows = slice(r * sub, (r + 1) * sub)
            qv, dov = q_ref[rows, :], do_ref[rows, :]
            sc = lax.dot_general(qv, kv_, NT_DIMS, preferred_element_type=F32)
            p = jnp.exp((sc - lse_ref[rows, :1]).astype(BF16))
            dp = lax.dot_general(dov, vv, NT_DIMS, preferred_element_type=F32)
            delta = jnp.sum(dov.astype(F32) * o_ref[rows, :].astype(F32), axis=-1, keepdims=True)
            ds = p * (dp - delta).astype(BF16)
            dq_ref[rows, :] = jnp.dot(ds, kv_, preferred_element_type=F32)
            p_scr[rows, :] = p
            ds_scr[rows, :] = ds
        dvp = lax.dot_general(p_scr[...], do_ref[...], TN_DIMS, preferred_element_type=F32)
        dkp = lax.dot_general(ds_scr[...], q_ref[...], TN_DIMS, preferred_element_type=F32)
        first = jnp.logical_and(g == 0, i == 0)

        @pl.when(first)
        def _():
            dv_acc[...] = dvp
            dk_acc[...] = dkp

        @pl.when(jnp.logical_not(first))
        def _():
            dv_acc[...] += dvp
            dk_acc[...] += dkp

        @pl.when(jnp.logical_and(g == grp - 1, i == nq - 1))
        def _():
            dk_ref[...] = dk_acc[...]
            dv_ref[...] = dv_acc[...].astype(dv_ref.dtype)

    qspec = pl.BlockSpec((tq, HEAD_DIM), lambda kv, g, i: (i, kv * grp + g))
    kspec = pl.BlockSpec((s, HEAD_DIM), lambda kv, g, i: (0, kv))
    return pl.pallas_call(
        body, name="attn_bwd", grid=(kvh, grp, nq),
        in_specs=[qspec, kspec, kspec, qspec, qspec, qspec],
        out_specs=[qspec, kspec, kspec],
        out_shape=[jax.ShapeDtypeStruct((s, aw), F32), jax.ShapeDtypeStruct(k.shape, F32),
                   jax.ShapeDtypeStruct(v.shape, BF16)],
        scratch_shapes=[pltpu.VMEM((s, HEAD_DIM), F32), pltpu.VMEM((s, HEAD_DIM), F32),
                        pltpu.VMEM((tq, s), BF16), pltpu.VMEM((tq, s), BF16)],
        compiler_params=_params(("parallel", "arbitrary", "arbitrary")),
    )(q, k, v, o, do, lse)


def _sum_all(z):
    return jnp.sum(jnp.sum(z, axis=0, keepdims=True), axis=1, keepdims=True)


def _chunk_consts(df_ref, db_ref, t):
    lf = _log_sigmoid(df_ref[0][:, :1])
    lb = _log_sigmoid(db_ref[0][:, :1])
    r = lax.broadcasted_iota(jnp.int32, (t, 1), 0).astype(F32)
    c = lax.broadcasted_iota(jnp.int32, (1, t), 1).astype(F32)
    diff = r - c
    dm = jnp.exp(diff * jnp.where(diff >= 0, lf, -lb))
    return dict(diff=diff, dm=dm, r=r,
                af=jnp.exp(lf * (r + 1.0)), bf=jnp.exp(lf * (t - 1.0 - r)), gf=jnp.exp(lf * t),
                ab=jnp.exp(lb * (t - r)), bb=jnp.exp(lb * r), gb=jnp.exp(lb * t))


def _scaled(x, f):
    return (x.astype(F32) * f).astype(BF16)


def _retc_specs(s):
    qspec = pl.BlockSpec((s, RET_QK_DIM), lambda h: (0, h))
    vspec = pl.BlockSpec((s, RET_V_DIM), lambda h: (0, h))
    dspec = pl.BlockSpec((1, 1, LANES), lambda h: (h, 0, 0))
    return qspec, vspec, dspec


def _retc_fwd(q, k, v, dec_f, dec_b, *, t=256):
    s, qw = q.shape
    t = _tile(s, t)
    heads, nc = qw // RET_QK_DIM, s // t
    qspec, vspec, dspec = _retc_specs(s)

    def body(q_ref, k_ref, v_ref, df_ref, db_ref, o_ref):
        cs = _chunk_consts(df_ref, db_ref, t)

        def rows_of(i):
            return pl.ds(pl.multiple_of(i * t, t), t)

        def forward(i, sf):
            rows = rows_of(i)
            qi, ki, vi = q_ref[rows, :], k_ref[rows, :], v_ref[rows, :]
            sc = lax.dot_general(qi, ki, NT_DIMS, preferred_element_type=F32)
            intra = jnp.dot((sc * cs["dm"]).astype(BF16), vi, preferred_element_type=F32)
            cross = jnp.dot(_scaled(qi, cs["af"]), sf.astype(BF16), preferred_element_type=F32)
            o_ref[rows, :] = intra + cross
            return cs["gf"] * sf + lax.dot_general(_scaled(ki, cs["bf"]), vi, TN_DIMS, preferred_element_type=F32)

        def backward(j, sb):
            rows = rows_of(nc - 1 - j)
            qi, ki, vi = q_ref[rows, :], k_ref[rows, :], v_ref[rows, :]
            o_ref[rows, :] += jnp.dot(_scaled(qi, cs["ab"]), sb.astype(BF16), preferred_element_type=F32)
            return cs["gb"] * sb + lax.dot_general(_scaled(ki, cs["bb"]), vi, TN_DIMS, preferred_element_type=F32)

        zero = jnp.zeros((RET_QK_DIM, RET_V_DIM), F32)
        lax.fori_loop(0, nc, forward, zero, unroll=True)
        lax.fori_loop(0, nc, backward, zero, unroll=True)

    return pl.pallas_call(
        body, name="ret_fwd", grid=(heads,),
        in_specs=[qspec, qspec, vspec, dspec, dspec],
        out_specs=vspec, out_shape=jax.ShapeDtypeStruct(v.shape, F32),
        compiler_params=_params(("parallel",)),
    )(q, k, v, dec_f, dec_b)


def _retc_bwd(q, k, v, do, dec_f, dec_b, *, t=256):
    s, qw = q.shape
    t = _tile(s, t)
    heads, nc = qw // RET_QK_DIM, s // t
    qspec, vspec, dspec = _retc_specs(s)
    gspec = pl.BlockSpec((1, 8, LANES), lambda h: (h, 0, 0))

    def body(q_ref, k_ref, v_ref, do_ref, df_ref, db_ref, dq_ref, dk_ref, dv_ref, gf_ref, gb_ref,
             sf_scr, sb_scr, dv_acc):
        cs = _chunk_consts(df_ref, db_ref, t)
        r, diff, dm = cs["r"], cs["diff"], cs["dm"]

        def rows_of(i):
            return pl.ds(pl.multiple_of(i * t, t), t)

        def tn(a, b):
            return lax.dot_general(a, b, TN_DIMS, preferred_element_type=F32)

        def nt(a, b):
            return lax.dot_general(a, b, NT_DIMS, preferred_element_type=F32)

        def states_f(i, sf):
            sf_scr[i] = sf
            rows = rows_of(i)
            return cs["gf"] * sf + tn(_scaled(k_ref[rows, :], cs["bf"]), v_ref[rows, :])

        def states_b(j, sb):
            i = nc - 1 - j
            sb_scr[i] = sb
            rows = rows_of(i)
            return cs["gb"] * sb + tn(_scaled(k_ref[rows, :], cs["bb"]), v_ref[rows, :])

        zero = jnp.zeros((RET_QK_DIM, RET_V_DIM), F32)
        lax.fori_loop(0, nc, states_f, zero, unroll=True)
        lax.fori_loop(0, nc, states_b, zero, unroll=True)

        def scan_grads(i, state, u, qf, kf, vi, doi, fa, fb, step, wa, wb):
            qa, kb = qf * fa, kf * fb
            ub = u.astype(BF16)
            dqa = nt(doi, state.astype(BF16))
            dkb = nt(vi, ub)
            dv = jnp.dot(kb.astype(BF16), ub, preferred_element_type=F32)
            dlog = _sum_all(dqa * qa * wa) + _sum_all(dkb * kb * wb) + t * step * _sum_all(u * state)
            u_new = step * u + tn(qa.astype(BF16), doi)
            return dqa * fa, dkb * fb, dv, u_new, dlog

        def sweep_f(j, carry):
            u, accf, accb = carry
            i = nc - 1 - j
            rows = rows_of(i)
            qi, ki, vi, doi = q_ref[rows, :], k_ref[rows, :], v_ref[rows, :], do_ref[rows, :]
            sc = nt(qi, ki)
            p = sc * dm
            dp = nt(doi, vi)
            ds = (dp * dm).astype(BF16)
            tt = dp * p * diff
            accf = accf + _sum_all(jnp.where(diff > 0, tt, 0.0))
            accb = accb + _sum_all(jnp.where(diff < 0, -tt, 0.0))
            dq1, dk1, dv1, u, dlog = scan_grads(i, sf_scr[i], u, qi.astype(F32), ki.astype(F32), vi, doi,
                                                cs["af"], cs["bf"], cs["gf"], r + 1.0, t - 1.0 - r)
            dq_ref[rows, :] = jnp.dot(ds, ki, preferred_element_type=F32) + dq1
            dk_ref[rows, :] = tn(ds, qi) + dk1
            dv_acc[rows, :] = tn(p.astype(BF16), doi) + dv1
            return u, accf + dlog, accb

        def sweep_b(i, carry):
            w, accb = carry
            rows = rows_of(i)
            qi, ki, vi, doi = q_ref[rows, :], k_ref[rows, :], v_ref[rows, :], do_ref[rows, :]
            dq1, dk1, dv1, w, dlog = scan_grads(i, sb_scr[i], w, qi.astype(F32), ki.astype(F32), vi, doi,
                                                cs["ab"], cs["bb"], cs["gb"], t - r, r)
            dq_ref[rows, :] += dq1
            dk_ref[rows, :] += dk1
            dv_acc[rows, :] += dv1
            return w, accb + dlog

        z11 = jnp.zeros((1, 1), F32)
        _, accf, accb = lax.fori_loop(0, nc, sweep_f, (zero, z11, z11), unroll=4)
        _, accb = lax.fori_loop(0, nc, sweep_b, (zero, accb), unroll=4)
        dv_ref[...] = dv_acc[...].astype(dv_ref.dtype)
        gf_ref[...] = jnp.broadcast_to((accf / (1.0 + jnp.exp(df_ref[0][:, :1]))).reshape(1, 1, 1), gf_ref.shape)
        gb_ref[...] = jnp.broadcast_to((accb / (1.0 + jnp.exp(db_ref[0][:, :1]))).reshape(1, 1, 1), gb_ref.shape)

    return pl.pallas_call(
        body, name="ret_bwd", grid=(heads,),
        in_specs=[qspec, qspec, vspec, vspec, dspec, dspec],
        out_specs=[qspec, qspec, vspec, gspec, gspec],
        out_shape=[jax.ShapeDtypeStruct(q.shape, F32), jax.ShapeDtypeStruct(k.shape, F32),
                   jax.ShapeDtypeStruct(v.shape, BF16),
                   jax.ShapeDtypeStruct((heads, 8, LANES), F32), jax.ShapeDtypeStruct((heads, 8, LANES), F32)],
        scratch_shapes=[pltpu.VMEM((nc, RET_QK_DIM, RET_V_DIM), F32), pltpu.VMEM((nc, RET_QK_DIM, RET_V_DIM), F32),
                        pltpu.VMEM((s, RET_V_DIM), F32)],
        compiler_params=_params(("parallel",)),
    )(q, k, v, do, dec_f, dec_b)


def _gate_fwd(att, ag, ret, rg, rnw, *, ts=256):
    s, aw = att.shape
    rw = ret.shape[1]
    ts = _tile(s, ts)
    rheads = rw // RET_V_DIM

    def body(att_ref, ag_ref, ret_ref, rg_ref, w_ref, y_ref, yt_ref):
        def put(lo, hi, val):
            y_ref[:, lo:hi] = val.astype(BF16)
            yt_ref[lo:hi, :] = val.T.astype(BF16)

        sa, _ = _silu_parts(ag_ref[...])
        put(0, aw, sa * att_ref[...].astype(F32))
        for h in range(rheads):
            sl = slice(h * RET_V_DIM, (h + 1) * RET_V_DIM)
            rt = ret_ref[:, sl]
            rn = rt * lax.rsqrt(jnp.mean(rt * rt, axis=-1, keepdims=True) + EPS) * w_ref[:, sl]
            sr, _ = _silu_parts(rg_ref[:, sl])
            put(aw + h * RET_V_DIM, aw + (h + 1) * RET_V_DIM, sr * rn)

    def row(w):
        return pl.BlockSpec((ts, w), lambda i: (i, 0))

    return pl.pallas_call(
        body, name="gate_fwd", grid=(s // ts,),
        in_specs=[row(aw), row(aw), row(rw), row(rw), pl.BlockSpec((1, rw), lambda i: (0, 0))],
        out_specs=[row(aw + rw), pl.BlockSpec((aw + rw, ts), lambda i: (0, i))],
        out_shape=[jax.ShapeDtypeStruct((s, aw + rw), BF16), jax.ShapeDtypeStruct((aw + rw, s), BF16)],
        compiler_params=_params(("parallel",)),
    )(att, ag, ret, rg, rnw.reshape(1, rw))


def _gate_bwd(dy, att, ag, ret, rg, rnw, *, ts=256):
    s, aw = att.shape
    rw = ret.shape[1]
    ts = _tile(s, ts)
    rheads = rw // RET_V_DIM

    def body(dy_ref, att_ref, ag_ref, ret_ref, rg_ref, w_ref, datt_ref, dag_ref, dret_ref, drg_ref, dw_ref):
        sa, dsa = _silu_parts(ag_ref[...])
        dya = dy_ref[:, :aw]
        datt_ref[...] = (dya * sa).astype(BF16)
        dag_ref[...] = (dya * att_ref[...].astype(F32) * dsa).astype(BF16)
        parts = []
        for h in range(rheads):
            sl = slice(h * RET_V_DIM, (h + 1) * RET_V_DIM)
            rt = ret_ref[:, sl]
            rr = lax.rsqrt(jnp.mean(rt * rt, axis=-1, keepdims=True) + EPS)
            rh = rt * rr
            wv = w_ref[:, sl]
            sr, dsr = _silu_parts(rg_ref[:, sl])
            dyr = dy_ref[:, aw + h * RET_V_DIM:aw + (h + 1) * RET_V_DIM]
            drg_ref[:, sl] = (dyr * rh * wv * dsr).astype(BF16)
            drn = dyr * sr
            dn = drn * wv
            dret_ref[:, sl] = (rr * (dn - rh * jnp.mean(dn * rh, axis=-1, keepdims=True))).astype(BF16)
            parts.append(jnp.sum(drn * rh, axis=0, keepdims=True))
        part = jnp.concatenate(parts, axis=-1)

        @pl.when(pl.program_id(0) == 0)
        def _():
            dw_ref[...] = part

        @pl.when(pl.program_id(0) > 0)
        def _():
            dw_ref[...] += part

    def row(w):
        return pl.BlockSpec((ts, w), lambda i: (i, 0))

    vec = pl.BlockSpec((1, rw), lambda i: (0, 0))
    return pl.pallas_call(
        body, name="gate_bwd", grid=(s // ts,),
        in_specs=[row(aw + rw), row(aw), row(aw), row(rw), row(rw), vec],
        out_specs=[row(aw), row(aw), row(rw), row(rw), vec],
        out_shape=[jax.ShapeDtypeStruct((s, aw), BF16), jax.ShapeDtypeStruct((s, aw), BF16),
                   jax.ShapeDtypeStruct((s, rw), BF16), jax.ShapeDtypeStruct((s, rw), BF16),
                   jax.ShapeDtypeStruct((1, rw), F32)],
        compiler_params=_params(("arbitrary",)),
    )(dy, att, ag, ret, rg, rnw.reshape(1, rw))


def _mesh_position():
    x, y, c = lax.axis_index("x"), lax.axis_index("y"), lax.axis_index("c")
    return x, y, c, 4 * x + 2 * y + c


def _peer(x, y, c, k):
    px = 1 - x if k & 4 else x
    py = 1 - y if k & 2 else y
    pc = 1 - c if k & 1 else c
    return (px, py, pc), 4 * px + 2 * py + pc


HBM_SPEC = pl.BlockSpec(memory_space=pltpu.HBM)
SEM_SPEC = pl.BlockSpec(memory_space=pltpu.SEMAPHORE)
ANY_SPEC = pl.BlockSpec(memory_space=pl.ANY)
DATAFLOW = pltpu.SideEffectType.DATAFLOW_SIDE_EFFECTING


def _hbm(a):
    return pltpu.with_memory_space_constraint(a, pltpu.HBM)


def _split_start(name, copies, n, src, land, after):
    def body(*refs):
        (send_sems, recv_sems), token = refs[2 + len(after):4 + len(after)], refs[-1]
        sends, _ = copies(refs[0], refs[1], send_sems, recv_sems)
        for cp in sends:
            cp.start()
        token[...] = jnp.zeros_like(token)

    return pl.pallas_call(
        body, name=name,
        out_shape=(pltpu.SemaphoreType.DMA((n,)), pltpu.SemaphoreType.DMA((n,)),
                   pltpu.HBM(src.shape, src.dtype), pltpu.HBM(land.shape, land.dtype),
                   jax.ShapeDtypeStruct((8, LANES), F32)),
        in_specs=[HBM_SPEC] * 2 + [ANY_SPEC] * len(after),
        out_specs=(SEM_SPEC, SEM_SPEC, HBM_SPEC, HBM_SPEC, pl.BlockSpec(memory_space=pltpu.VMEM)),
        input_output_aliases={0: 2, 1: 3},
        compiler_params=pltpu.CompilerParams(has_side_effects=DATAFLOW),
    )(_hbm(src), _hbm(land), *after)


def _split_wait(name, copies, started, after):
    send_sems, recv_sems, src, land = started[:4]

    def body(*refs):
        sends, recvs = copies(refs[0], refs[1], refs[2], refs[3])
        for cp in sends:
            cp.wait_send()
        for cp in recvs:
            cp.wait_recv()

    return pl.pallas_call(
        body, name=name,
        out_shape=(pltpu.HBM(src.shape, src.dtype), pltpu.HBM(land.shape, land.dtype)),
        in_specs=[HBM_SPEC] * 2 + [SEM_SPEC, SEM_SPEC] + [ANY_SPEC] * len(after),
        out_specs=(HBM_SPEC,) * 2,
        input_output_aliases={0: 0, 1: 1},
        compiler_params=pltpu.CompilerParams(has_side_effects=DATAFLOW),
    )(src, land, send_sems, recv_sems, *after)


def _slab(ref, p, size, axis):
    if axis == 1:
        return ref.at[:, pl.ds(pl.multiple_of(p * size, LANES), size)]
    return ref.at[pl.ds(pl.multiple_of(p * size, 16), size), :]


ALL_PEERS = tuple(range(1, N_DEV))
SIBLING = 1
SAME_CORE_OF_CHIPS = (2, 4, 6)


def _gather_copies(size, axis, ks):
    def copies(shard_ref, full_ref, send_sems, recv_sems):
        x, y, c, me = _mesh_position()
        sends, recvs = [], []
        for j, k in enumerate(ks):
            peer, pid = _peer(x, y, c, k)
            sends.append(pltpu.make_async_remote_copy(
                src_ref=shard_ref, dst_ref=_slab(full_ref, me, size, axis), send_sem=send_sems.at[j],
                recv_sem=recv_sems.at[j], device_id=peer, device_id_type=MESH))
            recvs.append(pltpu.make_async_remote_copy(
                src_ref=shard_ref, dst_ref=_slab(full_ref, pid, size, axis), send_sem=send_sems.at[j],
                recv_sem=recv_sems.at[j], device_id=peer, device_id_type=MESH))
        return sends, recvs

    return copies


def _pass_on_copies(size, axis):
    def copies(shard_ref, full_ref, send_sems, recv_sems):
        x, y, c, _ = _mesh_position()
        sibling, _ = _peer(x, y, c, SIBLING)
        sends, recvs = [], []
        for j, k in enumerate(SAME_CORE_OF_CHIPS):
            _, landed = _peer(x, y, c, k)
            _, siblings = _peer(x, y, c, k ^ SIBLING)
            mine = _slab(full_ref, landed, size, axis)
            sends.append(pltpu.make_async_remote_copy(
                src_ref=mine, dst_ref=mine, send_sem=send_sems.at[j], recv_sem=recv_sems.at[j],
                device_id=sibling, device_id_type=MESH))
            recvs.append(pltpu.make_async_remote_copy(
                src_ref=mine, dst_ref=_slab(full_ref, siblings, size, axis), send_sem=send_sems.at[j],
                recv_sem=recv_sems.at[j], device_id=sibling, device_id_type=MESH))
        return sends, recvs

    return copies


def _scatter_copies(size, axis):
    def copies(grad_ref, land_ref, send_sems, recv_sems):
        x, y, c, me = _mesh_position()
        sends, recvs = [], []
        for k in range(1, N_DEV):
            peer, pid = _peer(x, y, c, k)
            src = _slab(grad_ref, pid, size, axis)
            sends.append(pltpu.make_async_remote_copy(
                src_ref=src, dst_ref=land_ref.at[me], send_sem=send_sems.at[k - 1], recv_sem=recv_sems.at[k - 1],
                device_id=peer, device_id_type=MESH))
            recvs.append(pltpu.make_async_remote_copy(
                src_ref=src, dst_ref=land_ref.at[pid], send_sem=send_sems.at[k - 1], recv_sem=recv_sems.at[k - 1],
                device_id=peer, device_id_type=MESH))
        return sends, recvs

    return copies


PLACE_BANDS = 8


def _place_own(name, src, out_shape, in_spec, out_spec, steps, me):
    def body(me_ref, src_ref, out_ref):
        out_ref[...] = src_ref[...]

    return pl.pallas_call(
        body, name=name, out_shape=out_shape,
        grid_spec=pltpu.PrefetchScalarGridSpec(num_scalar_prefetch=1, grid=(steps,), in_specs=[in_spec],
                                               out_specs=out_spec),
        compiler_params=_params(("parallel",)),
    )(me.reshape(1).astype(jnp.int32), src)


def _gather_start(shard, axis, ks, me, after, tag):
    rows, cols = shard.shape
    size = shard.shape[axis]
    full_shape = tuple(N_DEV * n if a == axis else n for a, n in enumerate(shard.shape))
    band = rows // PLACE_BANDS
    in_spec = pl.BlockSpec((band, cols), lambda i, me_ref: (i, 0))
    if axis == 1:
        out_spec = pl.BlockSpec((band, cols), lambda i, me_ref: (i, me_ref[0]))
    else:
        out_spec = pl.BlockSpec((band, cols), lambda i, me_ref: (me_ref[0] * PLACE_BANDS + i, 0))
    full = _place_own("place_shard", shard, jax.ShapeDtypeStruct(full_shape, shard.dtype), in_spec, out_spec,
                      PLACE_BANDS, me)
    return _split_start("gather_start_" + tag, _gather_copies(size, axis, ks), len(ks), shard, full, after)


def _gather_wait(started, axis, ks, after, tag):
    size = started[2].shape[axis]
    return _split_wait("gather_wait_" + tag, _gather_copies(size, axis, ks), started, after)[1]


def _pass_on_start(shard, full, axis, after, tag):
    size = shard.shape[axis]
    return _split_start("pass_on_start_" + tag, _pass_on_copies(size, axis), len(SAME_CORE_OF_CHIPS), shard, full, after)


def _pass_on_wait(started, axis, after, tag):
    size = started[2].shape[axis]
    return _split_wait("pass_on_wait_" + tag, _pass_on_copies(size, axis), started, after)[1]


def _scatter_start(grad, axis, me, tag):
    size = grad.shape[axis] // N_DEV
    rows, cols = tuple(size if a == axis else n for a, n in enumerate(grad.shape))
    band = rows // PLACE_BANDS
    if axis == 1:
        in_spec = pl.BlockSpec((band, cols), lambda i, me_ref: (i, me_ref[0]))
    else:
        in_spec = pl.BlockSpec((band, cols), lambda i, me_ref: (me_ref[0] * PLACE_BANDS + i, 0))
    out_spec = pl.BlockSpec((None, band, cols), lambda i, me_ref: (me_ref[0], i, 0))
    land = _place_own("place_slab", grad, jax.ShapeDtypeStruct((N_DEV, rows, cols), grad.dtype), in_spec, out_spec,
                      PLACE_BANDS, me)
    return _split_start("scatter_start_" + tag, _scatter_copies(size, axis), N_DEV - 1, grad, land, [])


def _scatter_wait(started, axis, after, tag):
    size = started[2].shape[axis] // N_DEV
    land = _split_wait("scatter_wait_" + tag, _scatter_copies(size, axis), started, after)[1]
    return [(land, p) for p in range(N_DEV)]


N_CHIPS = N_DEV // 2


def _pair_copies(size, axis):
    def copies(grad_ref, land_ref, send_sems, recv_sems):
        x, y, c, _ = _mesh_position()
        sibling, _ = _peer(x, y, c, SIBLING)
        sends, recvs = [], []
        for j in range(N_CHIPS):
            _, owner = _peer(x, y, c, (2 * j) ^ SIBLING)
            for lst in (sends, recvs):
                lst.append(pltpu.make_async_remote_copy(
                    src_ref=_slab(grad_ref, owner, size, axis), dst_ref=land_ref.at[j], send_sem=send_sems.at[j],
                    recv_sem=recv_sems.at[j], device_id=sibling, device_id_type=MESH))
        return sends, recvs

    return copies


def _chips_copies():
    def copies(pair_ref, land_ref, send_sems, recv_sems):
        x, y, c, _ = _mesh_position()
        sends, recvs = [], []
        for j in range(1, N_CHIPS):
            owner, _ = _peer(x, y, c, 2 * j)
            for lst in (sends, recvs):
                lst.append(pltpu.make_async_remote_copy(
                    src_ref=pair_ref.at[j], dst_ref=land_ref.at[j], send_sem=send_sems.at[j - 1],
                    recv_sem=recv_sems.at[j - 1], device_id=owner, device_id_type=MESH))
        return sends, recvs

    return copies


def _pair_start(grad, axis, tag):
    size = grad.shape[axis] // N_DEV
    rows, cols = tuple(size if a == axis else n for a, n in enumerate(grad.shape))
    land = lax.empty((N_CHIPS, rows, cols), grad.dtype)
    return _split_start("pair_start_" + tag, _pair_copies(size, axis), N_CHIPS, grad, land, [])


def _pair_sums(started, axis, me, after, tag):
    size = started[2].shape[axis] // N_DEV
    grad, land = _split_wait("pair_wait_" + tag, _pair_copies(size, axis), started, after)
    _, rows, cols = land.shape
    band = rows // PLACE_BANDS
    if axis == 1:
        mine = pl.BlockSpec((band, cols), lambda j, i, me_ref: (i, me_ref[0] ^ (2 * j)))
    else:
        mine = pl.BlockSpec((band, cols), lambda j, i, me_ref: ((me_ref[0] ^ (2 * j)) * PLACE_BANDS + i, 0))
    slot = pl.BlockSpec((None, band, cols), lambda j, i, me_ref: (j, i, 0))

    def body(own_ref, mine_ref, theirs_ref, out_ref):
        out_ref[...] = (mine_ref[...].astype(F32) + theirs_ref[...].astype(F32)).astype(out_ref.dtype)

    return pl.pallas_call(
        body, name="pair_sums", out_shape=jax.ShapeDtypeStruct(land.shape, land.dtype),
        grid_spec=pltpu.PrefetchScalarGridSpec(num_scalar_prefetch=1, grid=(N_CHIPS, PLACE_BANDS),
                                               in_specs=[mine, slot], out_specs=slot),
        compiler_params=_params(("parallel", "parallel")),
    )(me.reshape(1).astype(jnp.int32), grad, land)


def _chips_start(pairs, tag):
    return _split_start("chips_start_" + tag, _chips_copies(), N_CHIPS - 1, pairs, lax.empty(pairs.shape, pairs.dtype), [])


def _chips_wait(started, after, tag):
    pairs, land = _split_wait("chips_wait_" + tag, _chips_copies(), started, after)
    return [(pairs, 0)] + [(land, j) for j in range(1, N_CHIPS)]


def _exchange_small(buf, *, name, after=()):
    r = buf.shape[0]

    def body(*refs):
        buf_ref = refs[0]
        all_ref, sum_ref, send_sems, recv_sems = refs[1 + len(after):]
        x, y, c, me = _mesh_position()
        all_ref[me] = buf_ref[...]
        sends, recvs = [], []
        for k in range(1, N_DEV):
            peer, pid = _peer(x, y, c, k)
            sends.append(pltpu.make_async_remote_copy(
                src_ref=buf_ref, dst_ref=all_ref.at[me], send_sem=send_sems.at[k - 1], recv_sem=recv_sems.at[k - 1],
                device_id=peer, device_id_type=MESH))
            recvs.append(pltpu.make_async_remote_copy(
                src_ref=buf_ref, dst_ref=all_ref.at[pid], send_sem=send_sems.at[k - 1], recv_sem=recv_sems.at[k - 1],
                device_id=peer, device_id_type=MESH))
        for cp in sends:
            cp.start()
        for cp in recvs:
            cp.wait_recv()
        for cp in sends:
            cp.wait_send()
        total = all_ref[0]
        for p in range(1, N_DEV):
            total = total + all_ref[p]
        sum_ref[...] = total

    vmem = pl.BlockSpec(memory_space=pltpu.VMEM)
    return pl.pallas_call(
        body, name=name,
        in_specs=[vmem] + [ANY_SPEC] * len(after), out_specs=[vmem, vmem],
        out_shape=[jax.ShapeDtypeStruct((N_DEV, r, LANES), F32), jax.ShapeDtypeStruct((r, LANES), F32)],
        scratch_shapes=[pltpu.SemaphoreType.DMA((N_DEV - 1,)), pltpu.SemaphoreType.DMA((N_DEV - 1,))],
        compiler_params=pltpu.CompilerParams(has_side_effects=True),
    )(buf, *after)


def _adamw_math(w, g, m, v):
    m2 = ADAM_B1 * m + (1.0 - ADAM_B1) * g
    v2 = ADAM_B2 * v + (1.0 - ADAM_B2) * (g * g)
    delta = -ADAM_LR * ((m2 / ADAM_C1) / (jnp.sqrt(v2 / ADAM_C2) + ADAM_EPS) + ADAM_WD * w)
    return delta, m2, v2


def _adamw_slabs(layer, w, m, v, addends, outs, order, *, tr, name):
    depth, r, c = w.shape
    tr = _tile(r, tr)
    n = len(addends)

    def body(*refs):
        w_ref, m_ref, v_ref = refs[:3]
        g_ref, d_ref, m2_ref, v2_ref = refs[-4:]
        g = refs[3][...].astype(F32)
        for a_ref in refs[4:3 + n]:
            g = g + a_ref[...].astype(F32)
        delta, m2, v2 = _adamw_math(w_ref[...], g, m_ref[...], v_ref[...])
        g_ref[...] = g
        d_ref[...] = delta
        m2_ref[...] = m2
        v2_ref[...] = v2

    row = pl.BlockSpec((None, tr, c), lambda i: (layer, i, 0))
    slots = [pl.BlockSpec((None, tr, c), lambda i, p=p: (p, i, 0)) for _, p in addends]
    first_out = 3 + n + 1
    return pl.pallas_call(
        body, name=name, grid=(r // tr,),
        in_specs=[row, row, row] + slots + [pl.BlockSpec((8, LANES), lambda i: (0, 0))] + [ANY_SPEC] * 4,
        out_specs=[row] * 4, out_shape=[jax.ShapeDtypeStruct((depth, r, c), F32)] * 4,
        input_output_aliases={first_out + t: t for t in range(4)},
        compiler_params=_params(("parallel",)),
    )(w, m, v, *[a for a, _ in addends], order, *outs)


def _adamw_small(w, g, m, v):
    def body(w_ref, g_ref, m_ref, v_ref, d_ref, m2_ref, v2_ref):
        delta, m2, v2 = _adamw_math(w_ref[...], g_ref[...], m_ref[...], v_ref[...])
        d_ref[...] = delta
        m2_ref[...] = m2
        v2_ref[...] = v2

    vmem = pl.BlockSpec(memory_space=pltpu.VMEM)
    return pl.pallas_call(
        body, name="adamw_small", in_specs=[vmem] * 4, out_specs=[vmem] * 3,
        out_shape=[jax.ShapeDtypeStruct(w.shape, F32)] * 3,
    )(w, g, m, v)


def _pack(parts):
    flat = jnp.concatenate([p.reshape(-1).astype(F32) for p in parts])
    rows = -(-flat.shape[0] // LANES)
    rows = -(-rows // SMALL_ROWS_ALIGN) * SMALL_ROWS_ALIGN
    flat = jnp.pad(flat, (0, rows * LANES - flat.shape[0]))
    return flat.reshape(rows, LANES)


def _unpack(buf, shapes):
    flat = buf.reshape(-1)
    out, pos = [], 0
    for shp in shapes:
        size = math.prod(shp)
        out.append(flat[pos:pos + size].reshape(shp))
        pos += size
    return out


def _section_widths(d):
    aw = d // 2
    kw = aw // ATTN_GROUP
    rw = d - aw
    rqw = (rw // RET_V_DIM) * RET_QK_DIM
    return (aw, kw, kw, aw, rqw, rqw, rw, rw)


def _layer_fwd(xl, hh, win_full, behind, after_attn, wout_of, qn, kn, dec_f, dec_b, rn, cos, sin):
    h, ht = hh
    aq, ak, v, ag, rq, rk, rvb, rg = _proj_sections(h, win_full, _section_widths(xl.shape[1]),
                                                    (F32, F32, BF16, F32, F32, F32, BF16, F32), after=behind)
    q, k, rqr, rkr = _prep_fwd(aq, ak, rq, rk, cos, sin, qn, kn)
    att, lse = _attn_fwd(q, k, v)
    ret = _retc_fwd(rqr, rkr, rvb, dec_f + after_attn(att), dec_b)
    y, yt = _gate_fwd(att, ag, ret, rg, rn)
    wout_full = wout_of(y)
    xn = _matmul(y, wout_full, name="out_proj", residual=xl)
    saved = dict(x=xl, ht=ht, aq=aq, ak=ak, ag=ag, rg=rg, q=q, k=k, v=v, rq=rqr, rk=rkr, rv=rvb,
                 att=att, lse=lse, ret=ret, yt=yt, win=win_full, wout=wout_full)
    return xn, saved


def _layer_bwd_weights(gb, sv, qn, kn, dec_f, dec_b, rn, cos, sin, on_dwout):
    dy = _matmul(gb, sv["wout"], name="d_y", trans_b=True)
    dwout = _matmul(sv["yt"], gb, name="d_wout", out_dtype=BF16)
    datt, dag, dret, drg, drn = _gate_bwd(dy, sv["att"], sv["ag"], sv["ret"], sv["rg"], rn + on_dwout(dwout))
    dq, dk, dav = _attn_bwd(sv["q"], sv["k"], sv["v"], sv["att"], datt, sv["lse"])
    drq, drk, drv, gf, gbk = _retc_bwd(sv["rq"], sv["rk"], sv["rv"], dret, dec_f, dec_b)
    dproj, dqn, dkn = _prep_bwd(dq, dk, drq, drk, sv["aq"], sv["ak"], cos, sin, qn, kn, dav, dag, drv, drg)
    dwin = _matmul(sv["ht"], dproj, name="d_win", out_dtype=BF16)
    small = dict(qn=dqn[0], kn=dkn[0], df=gf[:, 0, 0], db=gbk[:, 0, 0], rn=drn[0])
    return dproj, dwin, small


def _layer_bwd_input(g, dproj, sv, nw, behind, after_dh):
    dh = _matmul(dproj, sv["win"], name="d_h", trans_b=True, tm=512, tk=dproj.shape[1], after=behind)
    g, gb, dnw = _rms_bwd(dh, sv["x"], g, nw + after_dh(dh))
    return g, gb, dnw[0]


def kernel(x, norm_w, w_in, q_norm, k_norm, ret_decay_fwd, ret_decay_bwd, ret_norm, w_out, final_norm, loss_target, m_norm_w, m_w_in, m_q_norm, m_k_norm, m_ret_decay_fwd, m_ret_decay_bwd, m_ret_norm, m_w_out, m_final_norm, v_norm_w, v_w_in, v_q_norm, v_k_norm, v_ret_decay_fwd, v_ret_decay_bwd, v_ret_norm, v_w_out, v_final_norm):
    depth, d, _ = w_in.shape
    seq = x.shape[1]
    rw = _section_widths(d)[6]
    rheads = rw // RET_V_DIM
    rns = ret_norm.shape[-1]
    _, _, _, me = _mesh_position()

    target = loss_target[0]
    cos, sin = _rope_tables(seq)

    rn_all, _ = _exchange_small(_pack([ret_norm]), name="gather_ret_norm")
    rn_full = rn_all.reshape(N_DEV, -1)[:, :depth * rheads * rns].reshape(N_DEV, depth, rheads, rns)
    rn_full = jnp.transpose(rn_full, (1, 2, 0, 3)).reshape(depth, rw)

    dec_f = jnp.broadcast_to(ret_decay_fwd[:, :, None, None], (depth, rheads, 1, LANES))
    dec_b = jnp.broadcast_to(ret_decay_bwd[:, :, None, None], (depth, rheads, 1, LANES))

    win_bf = [w_in[l].astype(BF16) for l in range(depth)]
    wout_bf = [w_out[l].astype(BF16) for l in range(depth)]

    saved = []
    xl = x[0]
    first = (SIBLING,) + SAME_CORE_OF_CHIPS
    in_sent = _gather_start(win_bf[0], 1, first, me, [], "in0")
    hh = _rms_fwd(xl, norm_w[0] + in_sent[-1][0, 0])
    landed = _gather_wait(in_sent, 1, first, [hh[0]], "in0")
    win_full = _pass_on_wait(_pass_on_start(win_bf[0], landed, 1, [], "in0"), 1, [], "in0")
    for l in range(depth):
        if l > 0:
            hh = _rms_fwd(xl, norm_w[l])
        out_sent = _gather_start(wout_bf[l], 0, ALL_PEERS, me, [win_full], "out" + str(l))
        behind = [out_sent[-1]]
        passed = {}
        if l + 1 < depth:
            in_sent = _gather_start(win_bf[l + 1], 1, first, me, [win_full, out_sent[-1]], "in" + str(l + 1))
            behind.append(in_sent[-1])

        def after_attn(att, passed=passed, l=l):
            if l + 1 == depth:
                return 0.0
            landed = _gather_wait(in_sent, 1, first, [att], "in" + str(l + 1))
            passed["on"] = _pass_on_start(win_bf[l + 1], landed, 1, [], "in" + str(l + 1))
            return passed["on"][-1][0, 0]

        def wout_of(y, out_sent=out_sent, l=l):
            return _gather_wait(out_sent, 0, ALL_PEERS, [y], "out" + str(l))

        xl, sv = _layer_fwd(xl, hh, win_full, behind, after_attn, wout_of, q_norm[l], k_norm[l], dec_f[l], dec_b[l],
                            rn_full[l], cos, sin)
        saved.append(sv)
        if l + 1 < depth:
            win_full = _pass_on_wait(passed["on"], 1, [xl], "in" + str(l + 1))

    loss_row, g, gb, d_final = _loss_head(xl, target, final_norm)

    d_norm, d_qn, d_kn, d_df, d_db, d_rn = [], [], [], [], [], []
    lands = [None] * depth
    pending = None
    for l in reversed(range(depth)):
        sent = {}

        def on_dwout(dwout, sent=sent, l=l):
            sent["out"] = _scatter_start(dwout, 0, me, "out" + str(l))
            return sent["out"][-1][0, 0]

        def after_dh(dh, sent=sent, l=l):
            pairs = _pair_sums(sent["pair"], 1, me, [dh], "in" + str(l))
            sent["in"] = _chips_start(pairs, "in" + str(l))
            return sent["in"][-1][0, 0]

        dproj, dwin, sm = _layer_bwd_weights(gb, saved[l], q_norm[l], k_norm[l], dec_f[l], dec_b[l],
                                             rn_full[l], cos, sin, on_dwout)
        sent["pair"] = _pair_start(dwin, 1, "in" + str(l))
        g, gb, dnw = _layer_bwd_input(g, dproj, saved[l], norm_w[l], [sent["pair"][-1]], after_dh)
        if pending is not None:
            lands[l + 1] = (_chips_wait(pending["in"], [g], "in" + str(l + 1)),
                            _scatter_wait(pending["out"], 0, [g], "out" + str(l + 1)))
        pending = sent
        d_norm.append(dnw)
        d_qn.append(sm["qn"])
        d_kn.append(sm["kn"])
        d_df.append(sm["df"])
        d_db.append(sm["db"])
        d_rn.append(sm["rn"])
    for lst in (d_norm, d_qn, d_kn, d_df, d_db, d_rn):
        lst.reverse()
    order = pending["in"][-1]
    in_outs = [lax.empty(w_in.shape, F32) for _ in range(4)]
    out_outs = [lax.empty(w_out.shape, F32) for _ in range(4)]
    for l in reversed(range(1, depth)):
        in_outs = _adamw_slabs(l, w_in, m_w_in, v_w_in, lands[l][0], in_outs, order, tr=256, name="adamw_w_in")
        out_outs = _adamw_slabs(l, w_out, m_w_out, v_w_out, lands[l][1], out_outs, order, tr=64, name="adamw_w_out")
    land_out = _scatter_wait(pending["out"], 0, [g, out_outs[0]], "out0")
    out_outs = _adamw_slabs(0, w_out, m_w_out, v_w_out, land_out, out_outs, order, tr=64, name="adamw_w_out")

    small_shapes = [(depth, d), (depth, HEAD_DIM), (depth, HEAD_DIM), (depth, rheads), (depth, rheads),
                    (depth, rheads, N_DEV * rns), (d,), (1,)]
    grads_local = [jnp.stack(d_norm), jnp.stack(d_qn), jnp.stack(d_kn), jnp.stack(d_df), jnp.stack(d_db),
                   jnp.stack(d_rn).reshape(depth, rheads, N_DEV * rns), d_final[0], loss_row[0, :1]]
    _, gsum = _exchange_small(_pack(grads_local), name="all_reduce_small", after=(in_outs[0], out_outs[0]))
    land_in = _chips_wait(pending["in"], [g, gsum], "in0")
    in_outs = _adamw_slabs(0, w_in, m_w_in, v_w_in, land_in, in_outs, order, tr=256, name="adamw_w_in")
    g_norm, g_qn, g_kn, g_df, g_db, g_rn_full, g_final, loss = _unpack(gsum, small_shapes)
    g_rn = lax.dynamic_slice_in_dim(g_rn_full, me * rns, rns, axis=2)
    small_g = [g_norm, g_qn, g_kn, g_df, g_db, g_rn, g_final]
    small_w = [norm_w, q_norm, k_norm, ret_decay_fwd, ret_decay_bwd, ret_norm, final_norm]
    small_m = [m_norm_w, m_q_norm, m_k_norm, m_ret_decay_fwd, m_ret_decay_bwd, m_ret_norm, m_final_norm]
    small_v = [v_norm_w, v_q_norm, v_k_norm, v_ret_decay_fwd, v_ret_decay_bwd, v_ret_norm, v_final_norm]
    shapes = [a.shape for a in small_w]
    sd, sm, sv2 = _adamw_small(_pack(small_w), _pack(small_g), _pack(small_m), _pack(small_v))
    small_d, small_m2, small_v2 = _unpack(sd, shapes), _unpack(sm, shapes), _unpack(sv2, shapes)

    def ordered(small, win_v, wout_v):
        return [small[0], win_v, small[1], small[2], small[3], small[4], small[5], wout_v, small[6]]

    grads = ordered(small_g, in_outs[0], out_outs[0])
    deltas = ordered(small_d, in_outs[1], out_outs[1])
    new_m = ordered(small_m2, in_outs[2], out_outs[2])
    new_v = ordered(small_v2, in_outs[3], out_outs[3])
    return (loss.reshape(()), g[None], *grads, *deltas, *new_m, *new_v)
```

```python
import math

import jax
import jax.numpy as jnp
import numpy as np
from jax import lax
from jax.experimental import pallas as pl
from jax.experimental.pallas import tpu as pltpu

F32 = jnp.float32
BF16 = jnp.bfloat16

N_DEV = 8
HEAD_DIM = 128
ATTN_GROUP = 4
RET_QK_DIM = 128
RET_V_DIM = 256
GRID_W = 64
ROPE_THETA = 10000.0
EPS = 1e-6
ADAM_LR = 0.001
ADAM_B1 = 0.9
ADAM_B2 = 0.999
ADAM_EPS = 1e-08
ADAM_WD = 0.01
ADAM_STEP = 10
ADAM_C1 = 1.0 - ADAM_B1 ** ADAM_STEP
ADAM_C2 = 1.0 - ADAM_B2 ** ADAM_STEP
LANES = 128
SMALL_ROWS_ALIGN = 8
VMEM_LIMIT = 56 * 1024 * 1024

NT_DIMS = (((1,), (1,)), ((), ()))
TN_DIMS = (((0,), (0,)), ((), ()))
MESH = pl.DeviceIdType.MESH


def _params(sem):
    return pltpu.CompilerParams(dimension_semantics=sem, vmem_limit_bytes=VMEM_LIMIT)


def _tile(dim, pref, align=16):
    if dim <= pref:
        return dim
    for t in range(pref - pref % align, 0, -align):
        if dim % t == 0:
            return t
    raise ValueError((dim, pref, align))


def _silu_parts(z):
    sg = 1.0 / (1.0 + jnp.exp(-z))
    return z * sg, sg * (1.0 + z * (1.0 - sg))


def _log_sigmoid(x):
    return jnp.minimum(x, 0.0) - jnp.log(1.0 + jnp.exp(-jnp.abs(x)))


def _swap_pairs(z):
    src = lax.broadcasted_iota(jnp.int32, (HEAD_DIM, HEAD_DIM), 0)
    dst = lax.broadcasted_iota(jnp.int32, (HEAD_DIM, HEAD_DIM), 1)
    partner = jnp.where((dst % 64) < 32, dst + 32, dst - 32)
    perm = (src == partner).astype(F32)
    return jnp.dot(z, perm, precision=lax.Precision.HIGH, preferred_element_type=F32)


def _rope(z, cos, sin):
    return z * cos + _swap_pairs(z) * sin


def _rope_transposed(d, cos, sin):
    return d * cos + _swap_pairs(d * sin)


def _rope_tables(seq):
    rows = seq // GRID_W
    row = jnp.repeat(jnp.arange(rows), GRID_W).astype(F32)
    col = jnp.tile(jnp.arange(GRID_W), rows).astype(F32)
    axis_dim = HEAD_DIM // 2
    inv = ROPE_THETA ** (-jnp.arange(0, axis_dim, 2, dtype=F32) / axis_dim)
    ar = row[:, None] * inv[None, :]
    ac = col[:, None] * inv[None, :]
    cos = jnp.concatenate([jnp.cos(ar), jnp.cos(ar), jnp.cos(ac), jnp.cos(ac)], axis=-1)
    sin = jnp.concatenate([-jnp.sin(ar), jnp.sin(ar), -jnp.sin(ac), jnp.sin(ac)], axis=-1)
    return cos, sin


def _matmul(a, b, *, name, trans_b=False, out_dtype=F32, residual=None, tm=1024, tn=512, tk=4096, after=()):
    m, k = a.shape
    n = b.shape[0] if trans_b else b.shape[1]
    tm, tn, tk = _tile(m, tm), _tile(n, tn, LANES), _tile(k, tk, LANES)
    nk = k // tk
    has_res = residual is not None

    def body(*refs):
        a_ref, b_ref = refs[:2]
        r_ref = refs[2] if has_res else None
        o_ref = refs[2 + has_res + len(after)]
        if trans_b:
            part = lax.dot_general(a_ref[...], b_ref[...], NT_DIMS, preferred_element_type=F32)
        else:
            part = jnp.dot(a_ref[...], b_ref[...], preferred_element_type=F32)

        def finish(r):
            if has_res:
                r = r + r_ref[...]
            o_ref[...] = r.astype(o_ref.dtype)

        if nk == 1:
            finish(part)
        else:
            acc_ref = refs[-1]
            kk = pl.program_id(2)

            @pl.when(kk == 0)
            def _():
                acc_ref[...] = part

            @pl.when(kk > 0)
            def _():
                acc_ref[...] += part

            @pl.when(kk == nk - 1)
            def _():
                finish(acc_ref[...])

    if trans_b:
        b_spec = pl.BlockSpec((tn, tk), lambda i, j, kk: (j, kk))
    else:
        b_spec = pl.BlockSpec((tk, tn), lambda i, j, kk: (kk, j))
    in_specs = [pl.BlockSpec((tm, tk), lambda i, j, kk: (i, kk)), b_spec]
    args = [a, b]
    if has_res:
        in_specs.append(pl.BlockSpec((tm, tn), lambda i, j, kk: (i, j)))
        args.append(residual)
    in_specs += [ANY_SPEC] * len(after)
    args += list(after)
    return pl.pallas_call(
        body, name=name, grid=(m // tm, n // tn, nk),
        in_specs=in_specs,
        out_specs=pl.BlockSpec((tm, tn), lambda i, j, kk: (i, j)),
        out_shape=jax.ShapeDtypeStruct((m, n), out_dtype),
        scratch_shapes=[pltpu.VMEM((tm, tn), F32)] if nk > 1 else [],
        compiler_params=_params(("parallel", "parallel", "arbitrary")),
    )(*args)


def _proj_sections(a, b, widths, dtypes, *, tm=1024, tn=512, after=()):
    m, k = a.shape
    tm = _tile(m, tm)
    tn = _tile(min(widths), tn, LANES)
    assert all(w % tn == 0 for w in widths) and sum(widths) == b.shape[1]
    nblk = [w // tn for w in widths]
    first = [int(o) // tn for o in np.cumsum((0,) + tuple(widths))[:-1]]

    def body(a_ref, b_ref, *refs):
        j = pl.program_id(1)
        part = jnp.dot(a_ref[...], b_ref[...], preferred_element_type=F32)
        for o_ref, lo, n in zip(refs[len(after):], first, nblk):
            @pl.when(jnp.logical_and(j >= lo, j < lo + n))
            def _(o_ref=o_ref):
                o_ref[...] = part.astype(o_ref.dtype)

    out_specs = [pl.BlockSpec((tm, tn), lambda i, j, lo=lo, n=n: (i, jnp.clip(j - lo, 0, n - 1)))
                 for lo, n in zip(first, nblk)]
    return pl.pallas_call(
        body, name="proj", grid=(m // tm, b.shape[1] // tn),
        in_specs=[pl.BlockSpec((tm, k), lambda i, j: (i, 0)), pl.BlockSpec((k, tn), lambda i, j: (0, j))]
        + [ANY_SPEC] * len(after),
        out_specs=out_specs,
        out_shape=[jax.ShapeDtypeStruct((m, w), dt) for w, dt in zip(widths, dtypes)],
        compiler_params=_params(("arbitrary", "arbitrary")),
    )(a, b, *after)


def _rms_fwd(x, w, *, ts=256):
    s, d = x.shape
    ts = _tile(s, ts)

    def body(x_ref, w_ref, h_ref, ht_ref):
        xv = x_ref[...]
        r = lax.rsqrt(jnp.mean(xv * xv, axis=-1, keepdims=True) + EPS)
        h = xv * r * w_ref[...]
        h_ref[...] = h.astype(BF16)
        ht_ref[...] = h.T.astype(BF16)

    row = pl.BlockSpec((ts, d), lambda i: (i, 0))
    return pl.pallas_call(
        body, name="rms_fwd", grid=(s // ts,),
        in_specs=[row, pl.BlockSpec((1, d), lambda i: (0, 0))],
        out_specs=[row, pl.BlockSpec((d, ts), lambda i: (0, i))],
        out_shape=[jax.ShapeDtypeStruct((s, d), BF16), jax.ShapeDtypeStruct((d, s), BF16)],
        compiler_params=_params(("parallel",)),
    )(x, w.reshape(1, d))


def _rms_bwd(dh, x, g, w, *, ts=256):
    s, d = x.shape
    ts = _tile(s, ts)

    def body(dh_ref, x_ref, g_ref, w_ref, dx_ref, dxb_ref, dw_ref):
        xv = x_ref[...]
        r = lax.rsqrt(jnp.mean(xv * xv, axis=-1, keepdims=True) + EPS)
        xh = xv * r
        dhv = dh_ref[...]
        dn = dhv * w_ref[...]
        dx = g_ref[...] + r * (dn - xh * jnp.mean(dn * xh, axis=-1, keepdims=True))
        dx_ref[...] = dx
        dxb_ref[...] = dx.astype(BF16)
        part = jnp.sum(dhv * xh, axis=0, keepdims=True)

        @pl.when(pl.program_id(0) == 0)
        def _():
            dw_ref[...] = part

        @pl.when(pl.program_id(0) > 0)
        def _():
            dw_ref[...] += part

    row = pl.BlockSpec((ts, d), lambda i: (i, 0))
    vec = pl.BlockSpec((1, d), lambda i: (0, 0))
    return pl.pallas_call(
        body, name="rms_bwd", grid=(s // ts,),
        in_specs=[row, row, row, vec],
        out_specs=[row, row, vec],
        out_shape=[jax.ShapeDtypeStruct((s, d), F32), jax.ShapeDtypeStruct((s, d), BF16),
                   jax.ShapeDtypeStruct((1, d), F32)],
        compiler_params=_params(("arbitrary",)),
    )(dh, x, g, w.reshape(1, d))


def _loss_head(x, target, w, *, ts=256):
    s, d = x.shape
    ts = _tile(s, ts)

    def body(x_ref, t_ref, w_ref, loss_ref, dx_ref, dxb_ref, dw_ref):
        xv = x_ref[...]
        r = lax.rsqrt(jnp.mean(xv * xv, axis=-1, keepdims=True) + EPS)
        xh = xv * r
        wv = w_ref[...]
        diff = xh * wv - t_ref[...]
        lpart = 0.5 * jnp.sum(jnp.mean(diff * diff, axis=-1, keepdims=True), axis=0, keepdims=True)
        dout = diff * (1.0 / d)
        dn = dout * wv
        dx = r * (dn - xh * jnp.mean(dn * xh, axis=-1, keepdims=True))
        dx_ref[...] = dx
        dxb_ref[...] = dx.astype(BF16)
        part = jnp.sum(dout * xh, axis=0, keepdims=True)
        lrow = jnp.broadcast_to(lpart, loss_ref.shape)

        @pl.when(pl.program_id(0) == 0)
        def _():
            dw_ref[...] = part
            loss_ref[...] = lrow

        @pl.when(pl.program_id(0) > 0)
        def _():
            dw_ref[...] += part
            loss_ref[...] += lrow

    row = pl.BlockSpec((ts, d), lambda i: (i, 0))
    vec = pl.BlockSpec((1, d), lambda i: (0, 0))
    return pl.pallas_call(
        body, name="loss_head", grid=(s // ts,),
        in_specs=[row, row, vec],
        out_specs=[pl.BlockSpec((1, LANES), lambda i: (0, 0)), row, row, vec],
        out_shape=[jax.ShapeDtypeStruct((1, LANES), F32), jax.ShapeDtypeStruct((s, d), F32),
                   jax.ShapeDtypeStruct((s, d), BF16), jax.ShapeDtypeStruct((1, d), F32)],
        compiler_params=_params(("arbitrary",)),
    )(x, target, w.reshape(1, d))


def _prep_fwd(aq, ak, rq, rk, cos, sin, qw, kw, *, ts=256):
    s = aq.shape[0]
    ts = _tile(s, ts)
    attn_scale = HEAD_DIM ** -0.5
    ret_scale = RET_QK_DIM ** -0.5
    nq, nk, nr = aq.shape[1] // HEAD_DIM, ak.shape[1] // HEAD_DIM, rq.shape[1] // RET_QK_DIM

    def body(aq_ref, ak_ref, rq_ref, rk_ref, cos_ref, sin_ref, qw_ref, kw_ref,
             q_out, k_out, rq_out, rk_out):
        c, sn = cos_ref[...], sin_ref[...]

        def normed(u, w):
            return u * lax.rsqrt(jnp.mean(u * u, axis=-1, keepdims=True) + EPS) * w

        for j in range(nq):
            sl = slice(j * HEAD_DIM, (j + 1) * HEAD_DIM)
            q_out[:, sl] = (_rope(normed(aq_ref[:, sl], qw_ref[...]), c, sn) * attn_scale).astype(BF16)
        for j in range(nk):
            sl = slice(j * HEAD_DIM, (j + 1) * HEAD_DIM)
            k_out[:, sl] = _rope(normed(ak_ref[:, sl], kw_ref[...]), c, sn).astype(BF16)
        for j in range(nr):
            sl = slice(j * RET_QK_DIM, (j + 1) * RET_QK_DIM)
            rq_out[:, sl] = _rope(rq_ref[:, sl], c, sn).astype(BF16)
            rk_out[:, sl] = (_rope(rk_ref[:, sl], c, sn) * ret_scale).astype(BF16)

    def row(arr):
        return pl.BlockSpec((ts, arr.shape[1]), lambda i: (i, 0))

    vec = pl.BlockSpec((1, HEAD_DIM), lambda i: (0, 0))
    ins = [aq, ak, rq, rk]
    return pl.pallas_call(
        body, name="prep_fwd", grid=(s // ts,),
        in_specs=[row(a) for a in ins] + [row(cos), row(sin), vec, vec],
        out_specs=[row(a) for a in ins],
        out_shape=[jax.ShapeDtypeStruct(a.shape, BF16) for a in ins],
        compiler_params=_params(("parallel",)),
    )(*ins, cos, sin, qw.reshape(1, HEAD_DIM), kw.reshape(1, HEAD_DIM))


def _prep_bwd(dq, dk, drq, drk, aq, ak, cos, sin, qw, kw, dav, dag, drv, drg, *, ts=256):
    s = aq.shape[0]
    ts = _tile(s, ts)
    attn_scale = HEAD_DIM ** -0.5
    ret_scale = RET_QK_DIM ** -0.5
    nq, nk, nr = aq.shape[1] // HEAD_DIM, ak.shape[1] // HEAD_DIM, drq.shape[1] // RET_QK_DIM
    widths = (aq.shape[1], ak.shape[1], dav.shape[1], dag.shape[1], drq.shape[1], drk.shape[1], drv.shape[1],
              drg.shape[1])
    o_aq, o_ak, o_av, o_ag, o_rq, o_rk, o_rv, o_rg = (int(o) for o in np.cumsum((0,) + widths)[:-1])

    def body(dq_ref, dk_ref, drq_ref, drk_ref, aq_ref, ak_ref, cos_ref, sin_ref, dav_ref, dag_ref, drv_ref, drg_ref,
             qw_ref, kw_ref, dproj_ref, dqw_ref, dkw_ref):
        c, sn = cos_ref[...], sin_ref[...]
        for ref, off in ((dav_ref, o_av), (dag_ref, o_ag), (drv_ref, o_rv), (drg_ref, o_rg)):
            dproj_ref[:, off:off + ref.shape[1]] = ref[...]

        def unrope(d):
            return _rope_transposed(d, c, sn)

        def norm_bwd(dun, u, w):
            r = lax.rsqrt(jnp.mean(u * u, axis=-1, keepdims=True) + EPS)
            uh = u * r
            dn = dun * w
            du = r * (dn - uh * jnp.mean(dn * uh, axis=-1, keepdims=True))
            return du, jnp.sum(dun * uh, axis=0, keepdims=True)

        dqw = jnp.zeros((1, HEAD_DIM), F32)
        for j in range(nq):
            sl = slice(j * HEAD_DIM, (j + 1) * HEAD_DIM)
            du, dw = norm_bwd(unrope(dq_ref[:, sl] * attn_scale), aq_ref[:, sl], qw_ref[...])
            dproj_ref[:, o_aq + j * HEAD_DIM:o_aq + (j + 1) * HEAD_DIM] = du.astype(BF16)
            dqw = dqw + dw
        dkw = jnp.zeros((1, HEAD_DIM), F32)
        for j in range(nk):
            sl = slice(j * HEAD_DIM, (j + 1) * HEAD_DIM)
            du, dw = norm_bwd(unrope(dk_ref[:, sl]), ak_ref[:, sl], kw_ref[...])
            dproj_ref[:, o_ak + j * HEAD_DIM:o_ak + (j + 1) * HEAD_DIM] = du.astype(BF16)
            dkw = dkw + dw
        for j in range(nr):
            sl = slice(j * RET_QK_DIM, (j + 1) * RET_QK_DIM)
            dproj_ref[:, o_rq + j * RET_QK_DIM:o_rq + (j + 1) * RET_QK_DIM] = unrope(drq_ref[:, sl]).astype(BF16)
            dproj_ref[:, o_rk + j * RET_QK_DIM:o_rk + (j + 1) * RET_QK_DIM] = (
                unrope(drk_ref[:, sl] * ret_scale).astype(BF16))

        @pl.when(pl.program_id(0) == 0)
        def _():
            dqw_ref[...] = dqw
            dkw_ref[...] = dkw

        @pl.when(pl.program_id(0) > 0)
        def _():
            dqw_ref[...] += dqw
            dkw_ref[...] += dkw

    def row(arr):
        return pl.BlockSpec((ts, arr.shape[1]), lambda i: (i, 0))

    vec = pl.BlockSpec((1, HEAD_DIM), lambda i: (0, 0))
    ins = [dq, dk, drq, drk, aq, ak, cos, sin, dav, dag, drv, drg]
    total = sum(widths)
    return pl.pallas_call(
        body, name="prep_bwd", grid=(s // ts,),
        in_specs=[row(a) for a in ins] + [vec, vec],
        out_specs=[pl.BlockSpec((ts, total), lambda i: (i, 0)), vec, vec],
        out_shape=[jax.ShapeDtypeStruct((s, total), BF16)] + [jax.ShapeDtypeStruct((1, HEAD_DIM), F32)] * 2,
        compiler_params=_params(("arbitrary",)),
    )(*ins, qw.reshape(1, HEAD_DIM), kw.reshape(1, HEAD_DIM))


def _attn_fwd(q, k, v, *, tq=4096, sub=256):
    s, aw = q.shape
    tq = _tile(s, tq)
    sub = _tile(tq, sub)
    heads, kvh = aw // HEAD_DIM, k.shape[1] // HEAD_DIM
    grp = heads // kvh

    def body(q_ref, k_ref, v_ref, o_ref, lse_ref):
        kv_ = k_ref[...]
        v_ext = jnp.concatenate([v_ref[...], jnp.ones((s, HEAD_DIM), BF16)], axis=-1)
        for r in range(tq // sub):
            rows = slice(r * sub, (r + 1) * sub)
            sc = lax.dot_general(q_ref[rows, :], kv_, NT_DIMS, preferred_element_type=F32)
            m = jnp.max(sc, axis=-1, keepdims=True)
            p = jnp.exp((sc - m).astype(BF16))
            oe = jnp.dot(p, v_ext, preferred_element_type=F32)
            l = oe[:, HEAD_DIM:HEAD_DIM + 1]
            o_ref[rows, :] = (oe[:, :HEAD_DIM] / l).astype(o_ref.dtype)
            lse_ref[rows, :] = jnp.broadcast_to(m + jnp.log(l), (sub, HEAD_DIM))

    qspec = pl.BlockSpec((tq, HEAD_DIM), lambda kv, g, i: (i, kv * grp + g))
    kspec = pl.BlockSpec((s, HEAD_DIM), lambda kv, g, i: (0, kv))
    return pl.pallas_call(
        body, name="attn_fwd", grid=(kvh, grp, s // tq),
        in_specs=[qspec, kspec, kspec],
        out_specs=[qspec, qspec],
        out_shape=[jax.ShapeDtypeStruct((s, aw), BF16), jax.ShapeDtypeStruct((s, aw), F32)],
        compiler_params=_params(("parallel", "parallel", "parallel")),
    )(q, k, v)


def _attn_bwd(q, k, v, o, do, lse, *, tq=1024, sub=256):
    s, aw = q.shape
    tq = _tile(s, tq)
    sub = _tile(tq, sub)
    heads, kvh = aw // HEAD_DIM, k.shape[1] // HEAD_DIM
    grp = heads // kvh
    nq = s // tq

    def body(q_ref, k_ref, v_ref, o_ref, do_ref, lse_ref, dq_ref, dk_ref, dv_ref, dk_acc, dv_acc, p_scr, ds_scr):
        g, i = pl.program_id(1), pl.program_id(2)
        kv_, vv = k_ref[...], v_ref[...]
        for r in range(tq // sub):
            rows = slice(r * sub, (r + 1) * sub)
            qv, dov = q_ref[rows, :], do_ref[rows, :]
            sc = lax.dot_general(qv, kv_, NT_DIMS, preferred_element_type=F32)
            p = jnp.exp((sc - lse_ref[rows, :1]).astype(BF16))
            dp = lax.dot_general(dov, vv, NT_DIMS, preferred_element_type=F32)
            delta = jnp.sum(dov.astype(F32) * o_ref[rows, :].astype(F32), axis=-1, keepdims=True)
            ds = p * (dp - delta).astype(BF16)
            dq_ref[rows, :] = jnp.dot(ds, kv_, preferred_element_type=F32)
            p_scr[rows, :] = p
            ds_scr[rows, :] = ds
        dvp = lax.dot_general(p_scr[...], do_ref[...], TN_DIMS, preferred_element_type=F32)
        dkp = lax.dot_general(ds_scr[...], q_ref[...], TN_DIMS, preferred_element_type=F32)
        first = jnp.logical_and(g == 0, i == 0)

        @pl.when(first)
        def _():
            dv_acc[...] = dvp
            dk_acc[...] = dkp

        @pl.when(jnp.logical_not(first))
        def _():
            dv_acc[...] += dvp
            dk_acc[...] += dkp

        @pl.when(jnp.logical_and(g == grp - 1, i == nq - 1))
        def _():
            dk_ref[...] = dk_acc[...]
            dv_ref[...] = dv_acc[...].astype(dv_ref.dtype)

    qspec = pl.BlockSpec((tq, HEAD_DIM), lambda kv, g, i: (i, kv * grp + g))
    kspec = pl.BlockSpec((s, HEAD_DIM), lambda kv, g, i: (0, kv))
    return pl.pallas_call(
        body, name="attn_bwd", grid=(kvh, grp, nq),
        in_specs=[qspec, kspec, kspec, qspec, qspec, qspec],
        out_specs=[qspec, kspec, kspec],
        out_shape=[jax.ShapeDtypeStruct((s, aw), F32), jax.ShapeDtypeStruct(k.shape, F32),
                   jax.ShapeDtypeStruct(v.shape, BF16)],
        scratch_shapes=[pltpu.VMEM((s, HEAD_DIM), F32), pltpu.VMEM((s, HEAD_DIM), F32),
                        pltpu.VMEM((tq, s), BF16), pltpu.VMEM((tq, s), BF16)],
        compiler_params=_params(("parallel", "arbitrary", "arbitrary")),
    )(q, k, v, o, do, lse)


def _sum_all(z):
    return jnp.sum(jnp.sum(z, axis=0, keepdims=True), axis=1, keepdims=True)


def _chunk_consts(df_ref, db_ref, t):
    lf = _log_sigmoid(df_ref[0][:, :1])
    lb = _log_sigmoid(db_ref[0][:, :1])
    r = lax.broadcasted_iota(jnp.int32, (t, 1), 0).astype(F32)
    c = lax.broadcasted_iota(jnp.int32, (1, t), 1).astype(F32)
    diff = r - c
    dm = jnp.exp(diff * jnp.where(diff >= 0, lf, -lb))
    return dict(diff=diff, dm=dm, r=r,
                af=jnp.exp(lf * (r + 1.0)), bf=jnp.exp(lf * (t - 1.0 - r)), gf=jnp.exp(lf * t),
                ab=jnp.exp(lb * (t - r)), bb=jnp.exp(lb * r), gb=jnp.exp(lb * t))


def _scaled(x, f):
    return (x.astype(F32) * f).astype(BF16)


def _retc_specs(s):
    qspec = pl.BlockSpec((s, RET_QK_DIM), lambda h: (0, h))
    vspec = pl.BlockSpec((s, RET_V_DIM), lambda h: (0, h))
    dspec = pl.BlockSpec((1, 1, LANES), lambda h: (h, 0, 0))
    return qspec, vspec, dspec


def _retc_fwd(q, k, v, dec_f, dec_b, *, t=256):
    s, qw = q.shape
    t = _tile(s, t)
    heads, nc = qw // RET_QK_DIM, s // t
    qspec, vspec, dspec = _retc_specs(s)

    def body(q_ref, k_ref, v_ref, df_ref, db_ref, o_ref):
        cs = _chunk_consts(df_ref, db_ref, t)

        def rows_of(i):
            return pl.ds(pl.multiple_of(i * t, t), t)

        def forward(i, sf):
            rows = rows_of(i)
            qi, ki, vi = q_ref[rows, :], k_ref[rows, :], v_ref[rows, :]
            sc = lax.dot_general(qi, ki, NT_DIMS, preferred_element_type=F32)
            intra = jnp.dot((sc * cs["dm"]).astype(BF16), vi, preferred_element_type=F32)
            cross = jnp.dot(_scaled(qi, cs["af"]), sf.astype(BF16), preferred_element_type=F32)
            o_ref[rows, :] = intra + cross
            return cs["gf"] * sf + lax.dot_general(_scaled(ki, cs["bf"]), vi, TN_DIMS, preferred_element_type=F32)

        def backward(j, sb):
            rows = rows_of(nc - 1 - j)
            qi, ki, vi = q_ref[rows, :], k_ref[rows, :], v_ref[rows, :]
            o_ref[rows, :] += jnp.dot(_scaled(qi, cs["ab"]), sb.astype(BF16), preferred_element_type=F32)
            return cs["gb"] * sb + lax.dot_general(_scaled(ki, cs["bb"]), vi, TN_DIMS, preferred_element_type=F32)

        zero = jnp.zeros((RET_QK_DIM, RET_V_DIM), F32)
        lax.fori_loop(0, nc, forward, zero, unroll=True)
        lax.fori_loop(0, nc, backward, zero, unroll=True)

    return pl.pallas_call(
        body, name="ret_fwd", grid=(heads,),
        in_specs=[qspec, qspec, vspec, dspec, dspec],
        out_specs=vspec, out_shape=jax.ShapeDtypeStruct(v.shape, F32),
        compiler_params=_params(("parallel",)),
    )(q, k, v, dec_f, dec_b)


def _retc_bwd(q, k, v, do, dec_f, dec_b, *, t=256):
    s, qw = q.shape
    t = _tile(s, t)
    heads, nc = qw // RET_QK_DIM, s // t
    qspec, vspec, dspec = _retc_specs(s)
    gspec = pl.BlockSpec((1, 8, LANES), lambda h: (h, 0, 0))

    def body(q_ref, k_ref, v_ref, do_ref, df_ref, db_ref, dq_ref, dk_ref, dv_ref, gf_ref, gb_ref,
             sf_scr, sb_scr, dv_acc):
        cs = _chunk_consts(df_ref, db_ref, t)
        r, diff, dm = cs["r"], cs["diff"], cs["dm"]

        def rows_of(i):
            return pl.ds(pl.multiple_of(i * t, t), t)

        def tn(a, b):
            return lax.dot_general(a, b, TN_DIMS, preferred_element_type=F32)

        def nt(a, b):
            return lax.dot_general(a, b, NT_DIMS, preferred_element_type=F32)

        def states_f(i, sf):
            sf_scr[i] = sf
            rows = rows_of(i)
            return cs["gf"] * sf + tn(_scaled(k_ref[rows, :], cs["bf"]), v_ref[rows, :])

        def states_b(j, sb):
            i = nc - 1 - j
            sb_scr[i] = sb
            rows = rows_of(i)
            return cs["gb"] * sb + tn(_scaled(k_ref[rows, :], cs["bb"]), v_ref[rows, :])

        zero = jnp.zeros((RET_QK_DIM, RET_V_DIM), F32)
        lax.fori_loop(0, nc, states_f, zero, unroll=True)
        lax.fori_loop(0, nc, states_b, zero, unroll=True)

        def scan_grads(i, state, u, qf, kf, vi, doi, fa, fb, step, wa, wb):
            qa, kb = qf * fa, kf * fb
            ub = u.astype(BF16)
            dqa = nt(doi, state.astype(BF16))
            dkb = nt(vi, ub)
            dv = jnp.dot(kb.astype(BF16), ub, preferred_element_type=F32)
            dlog = _sum_all(dqa * qa * wa) + _sum_all(dkb * kb * wb) + t * step * _sum_all(u * state)
            u_new = step * u + tn(qa.astype(BF16), doi)
            return dqa * fa, dkb * fb, dv, u_new, dlog

        def sweep_f(j, carry):
            u, accf, accb = carry
            i = nc - 1 - j
            rows = rows_of(i)
            qi, ki, vi, doi = q_ref[rows, :], k_ref[rows, :], v_ref[rows, :], do_ref[rows, :]
            sc = nt(qi, ki)
            p = sc * dm
            dp = nt(doi, vi)
            ds = (dp * dm).astype(BF16)
            tt = dp * p * diff
            accf = accf + _sum_all(jnp.where(diff > 0, tt, 0.0))
            accb = accb + _sum_all(jnp.where(diff < 0, -tt, 0.0))
            dq1, dk1, dv1, u, dlog = scan_grads(i, sf_scr[i], u, qi.astype(F32), ki.astype(F32), vi, doi,
                                                cs["af"], cs["bf"], cs["gf"], r + 1.0, t - 1.0 - r)
            dq_ref[rows, :] = jnp.dot(ds, ki, preferred_element_type=F32) + dq1
            dk_ref[rows, :] = tn(ds, qi) + dk1
            dv_acc[rows, :] = tn(p.astype(BF16), doi) + dv1
            return u, accf + dlog, accb

        def sweep_b(i, carry):
            w, accb = carry
            rows = rows_of(i)
            qi, ki, vi, doi = q_ref[rows, :], k_ref[rows, :], v_ref[rows, :], do_ref[rows, :]
            dq1, dk1, dv1, w, dlog = scan_grads(i, sb_scr[i], w, qi.astype(F32), ki.astype(F32), vi, doi,
                                                cs["ab"], cs["bb"], cs["gb"], t - r, r)
            dq_ref[rows, :] += dq1
            dk_ref[rows, :] += dk1
            dv_acc[rows, :] += dv1
            return w, accb + dlog

        z11 = jnp.zeros((1, 1), F32)
        _, accf, accb = lax.fori_loop(0, nc, sweep_f, (zero, z11, z11), unroll=4)
        _, accb = lax.fori_loop(0, nc, sweep_b, (zero, accb), unroll=4)
        dv_ref[...] = dv_acc[...].astype(dv_ref.dtype)
        gf_ref[...] = jnp.broadcast_to((accf / (1.0 + jnp.exp(df_ref[0][:, :1]))).reshape(1, 1, 1), gf_ref.shape)
        gb_ref[...] = jnp.broadcast_to((accb / (1.0 + jnp.exp(db_ref[0][:, :1]))).reshape(1, 1, 1), gb_ref.shape)

    return pl.pallas_call(
        body, name="ret_bwd", grid=(heads,),
        in_specs=[qspec, qspec, vspec, vspec, dspec, dspec],
        out_specs=[qspec, qspec, vspec, gspec, gspec],
        out_shape=[jax.ShapeDtypeStruct(q.shape, F32), jax.ShapeDtypeStruct(k.shape, F32),
                   jax.ShapeDtypeStruct(v.shape, BF16),
                   jax.ShapeDtypeStruct((heads, 8, LANES), F32), jax.ShapeDtypeStruct((heads, 8, LANES), F32)],
        scratch_shapes=[pltpu.VMEM((nc, RET_QK_DIM, RET_V_DIM), F32), pltpu.VMEM((nc, RET_QK_DIM, RET_V_DIM), F32),
                        pltpu.VMEM((s, RET_V_DIM), F32)],
        compiler_params=_params(("parallel",)),
    )(q, k, v, do, dec_f, dec_b)


def _gate_fwd(att, ag, ret, rg, rnw, *, ts=256):
    s, aw = att.shape
    rw = ret.shape[1]
    ts = _tile(s, ts)
    rheads = rw // RET_V_DIM

    def body(att_ref, ag_ref, ret_ref, rg_ref, w_ref, y_ref, yt_ref):
        def put(lo, hi, val):
            y_ref[:, lo:hi] = val.astype(BF16)
            yt_ref[lo:hi, :] = val.T.astype(BF16)

        sa, _ = _silu_parts(ag_ref[...])
        put(0, aw, sa * att_ref[...].astype(F32))
        for h in range(rheads):
            sl = slice(h * RET_V_DIM, (h + 1) * RET_V_DIM)
            rt = ret_ref[:, sl]
            rn = rt * lax.rsqrt(jnp.mean(rt * rt, axis=-1, keepdims=True) + EPS) * w_ref[:, sl]
            sr, _ = _silu_parts(rg_ref[:, sl])
            put(aw + h * RET_V_DIM, aw + (h + 1) * RET_V_DIM, sr * rn)

    def row(w):
        return pl.BlockSpec((ts, w), lambda i: (i, 0))

    return pl.pallas_call(
        body, name="gate_fwd", grid=(s // ts,),
        in_specs=[row(aw), row(aw), row(rw), row(rw), pl.BlockSpec((1, rw), lambda i: (0, 0))],
        out_specs=[row(aw + rw), pl.BlockSpec((aw + rw, ts), lambda i: (0, i))],
        out_shape=[jax.ShapeDtypeStruct((s, aw + rw), BF16), jax.ShapeDtypeStruct((aw + rw, s), BF16)],
        compiler_params=_params(("parallel",)),
    )(att, ag, ret, rg, rnw.reshape(1, rw))


def _gate_bwd(dy, att, ag, ret, rg, rnw, *, ts=256):
    s, aw = att.shape
    rw = ret.shape[1]
    ts = _tile(s, ts)
    rheads = rw // RET_V_DIM

    def body(dy_ref, att_ref, ag_ref, ret_ref, rg_ref, w_ref, datt_ref, dag_ref, dret_ref, drg_ref, dw_ref):
        sa, dsa = _silu_parts(ag_ref[...])
        dya = dy_ref[:, :aw]
        datt_ref[...] = (dya * sa).astype(BF16)
        dag_ref[...] = (dya * att_ref[...].astype(F32) * dsa).astype(BF16)
        parts = []
        for h in range(rheads):
            sl = slice(h * RET_V_DIM, (h + 1) * RET_V_DIM)
            rt = ret_ref[:, sl]
            rr = lax.rsqrt(jnp.mean(rt * rt, axis=-1, keepdims=True) + EPS)
            rh = rt * rr
            wv = w_ref[:, sl]
            sr, dsr = _silu_parts(rg_ref[:, sl])
            dyr = dy_ref[:, aw + h * RET_V_DIM:aw + (h + 1) * RET_V_DIM]
            drg_ref[:, sl] = (dyr * rh * wv * dsr).astype(BF16)
            drn = dyr * sr
            dn = drn * wv
            dret_ref[:, sl] = (rr * (dn - rh * jnp.mean(dn * rh, axis=-1, keepdims=True))).astype(BF16)
            parts.append(jnp.sum(drn * rh, axis=0, keepdims=True))
        part = jnp.concatenate(parts, axis=-1)

        @pl.when(pl.program_id(0) == 0)
        def _():
            dw_ref[...] = part

        @pl.when(pl.program_id(0) > 0)
        def _():
            dw_ref[...] += part

    def row(w):
        return pl.BlockSpec((ts, w), lambda i: (i, 0))

    vec = pl.BlockSpec((1, rw), lambda i: (0, 0))
    return pl.pallas_call(
        body, name="gate_bwd", grid=(s // ts,),
        in_specs=[row(aw + rw), row(aw), row(aw), row(rw), row(rw), vec],
        out_specs=[row(aw), row(aw), row(rw), row(rw), vec],
        out_shape=[jax.ShapeDtypeStruct((s, aw), BF16), jax.ShapeDtypeStruct((s, aw), BF16),
                   jax.ShapeDtypeStruct((s, rw), BF16), jax.ShapeDtypeStruct((s, rw), BF16),
                   jax.ShapeDtypeStruct((1, rw), F32)],
        compiler_params=_params(("arbitrary",)),
    )(dy, att, ag, ret, rg, rnw.reshape(1, rw))


def _mesh_position():
    x, y, c = lax.axis_index("x"), lax.axis_index("y"), lax.axis_index("c")
    return x, y, c, 4 * x + 2 * y + c


def _peer(x, y, c, k):
    px = 1 - x if k & 4 else x
    py = 1 - y if k & 2 else y
    pc = 1 - c if k & 1 else c
    return (px, py, pc), 4 * px + 2 * py + pc


HBM_SPEC = pl.BlockSpec(memory_space=pltpu.HBM)
SEM_SPEC = pl.BlockSpec(memory_space=pltpu.SEMAPHORE)
ANY_SPEC = pl.BlockSpec(memory_space=pl.ANY)
DATAFLOW = pltpu.SideEffectType.DATAFLOW_SIDE_EFFECTING


def _hbm(a):
    return pltpu.with_memory_space_constraint(a, pltpu.HBM)


def _split_start(name, copies, n, src, land, after):
    def body(*refs):
        (send_sems, recv_sems), token = refs[2 + len(after):4 + len(after)], refs[-1]
        sends, _ = copies(refs[0], refs[1], send_sems, recv_sems)
        for cp in sends:
            cp.start()
        token[...] = jnp.zeros_like(token)

    return pl.pallas_call(
        body, name=name,
        out_shape=(pltpu.SemaphoreType.DMA((n,)), pltpu.SemaphoreType.DMA((n,)),
                   pltpu.HBM(src.shape, src.dtype), pltpu.HBM(land.shape, land.dtype),
                   jax.ShapeDtypeStruct((8, LANES), F32)),
        in_specs=[HBM_SPEC] * 2 + [ANY_SPEC] * len(after),
        out_specs=(SEM_SPEC, SEM_SPEC, HBM_SPEC, HBM_SPEC, pl.BlockSpec(memory_space=pltpu.VMEM)),
        input_output_aliases={0: 2, 1: 3},
        compiler_params=pltpu.CompilerParams(has_side_effects=DATAFLOW),
    )(_hbm(src), _hbm(land), *after)


def _split_wait(name, copies, started, after):
    send_sems, recv_sems, src, land = started[:4]

    def body(*refs):
        sends, recvs = copies(refs[0], refs[1], refs[2], refs[3])
        for cp in sends:
            cp.wait_send()
        for cp in recvs:
            cp.wait_recv()

    return pl.pallas_call(
        body, name=name,
        out_shape=(pltpu.HBM(src.shape, src.dtype), pltpu.HBM(land.shape, land.dtype)),
        in_specs=[HBM_SPEC] * 2 + [SEM_SPEC, SEM_SPEC] + [ANY_SPEC] * len(after),
        out_specs=(HBM_SPEC,) * 2,
        input_output_aliases={0: 0, 1: 1},
        compiler_params=pltpu.CompilerParams(has_side_effects=DATAFLOW),
    )(src, land, send_sems, recv_sems, *after)


def _slab(ref, p, size, axis):
    if axis == 1:
        return ref.at[:, pl.ds(pl.multiple_of(p * size, LANES), size)]
    return ref.at[pl.ds(pl.multiple_of(p * size, 16), size), :]


ALL_PEERS = tuple(range(1, N_DEV))
SIBLING = 1
SAME_CORE_OF_CHIPS = (2, 4, 6)


def _gather_copies(size, axis, ks):
    def copies(shard_ref, full_ref, send_sems, recv_sems):
        x, y, c, me = _mesh_position()
        sends, recvs = [], []
        for j, k in enumerate(ks):
            peer, pid = _peer(x, y, c, k)
            sends.append(pltpu.make_async_remote_copy(
                src_ref=shard_ref, dst_ref=_slab(full_ref, me, size, axis), send_sem=send_sems.at[j],
                recv_sem=recv_sems.at[j], device_id=peer, device_id_type=MESH))
            recvs.append(pltpu.make_async_remote_copy(
                src_ref=shard_ref, dst_ref=_slab(full_ref, pid, size, axis), send_sem=send_sems.at[j],
                recv_sem=recv_sems.at[j], device_id=peer, device_id_type=MESH))
        return sends, recvs

    return copies


def _pass_on_copies(size, axis):
    def copies(shard_ref, full_ref, send_sems, recv_sems):
        x, y, c, _ = _mesh_position()
        sibling, _ = _peer(x, y, c, SIBLING)
        sends, recvs = [], []
        for j, k in enumerate(SAME_CORE_OF_CHIPS):
            _, landed = _peer(x, y, c, k)
            _, siblings = _peer(x, y, c, k ^ SIBLING)
            mine = _slab(full_ref, landed, size, axis)
            sends.append(pltpu.make_async_remote_copy(
                src_ref=mine, dst_ref=mine, send_sem=send_sems.at[j], recv_sem=recv_sems.at[j],
                device_id=sibling, device_id_type=MESH))
            recvs.append(pltpu.make_async_remote_copy(
                src_ref=mine, dst_ref=_slab(full_ref, siblings, size, axis), send_sem=send_sems.at[j],
                recv_sem=recv_sems.at[j], device_id=sibling, device_id_type=MESH))
        return sends, recvs

    return copies


def _scatter_copies(size, axis):
    def copies(grad_ref, land_ref, send_sems, recv_sems):
        x, y, c, me = _mesh_position()
        sends, recvs = [], []
        for k in range(1, N_DEV):
            peer, pid = _peer(x, y, c, k)
            src = _slab(grad_ref, pid, size, axis)
            sends.append(pltpu.make_async_remote_copy(
                src_ref=src, dst_ref=land_ref.at[me], send_sem=send_sems.at[k - 1], recv_sem=recv_sems.at[k - 1],
                device_id=peer, device_id_type=MESH))
            recvs.append(pltpu.make_async_remote_copy(
                src_ref=src, dst_ref=land_ref.at[pid], send_sem=send_sems.at[k - 1], recv_sem=recv_sems.at[k - 1],
                device_id=peer, device_id_type=MESH))
        return sends, recvs

    return copies


PLACE_BANDS = 8


def _place_own(name, src, out_shape, in_spec, out_spec, steps, me):
    def body(me_ref, src_ref, out_ref):
        out_ref[...] = src_ref[...]

    return pl.pallas_call(
        body, name=name, out_shape=out_shape,
        grid_spec=pltpu.PrefetchScalarGridSpec(num_scalar_prefetch=1, grid=(steps,), in_specs=[in_spec],
                                               out_specs=out_spec),
        compiler_params=_params(("parallel",)),
    )(me.reshape(1).astype(jnp.int32), src)


def _gather_start(w_all, layer, axis, ks, me, after, tag):
    _, rows, cols = w_all.shape
    size = (rows, cols)[axis]
    full_shape = tuple(N_DEV * n if a == axis else n for a, n in enumerate((rows, cols)))
    band = rows // PLACE_BANDS
    if axis == 1:
        full_spec = pl.BlockSpec((band, cols), lambda i, me_ref: (i, me_ref[0]))
    else:
        full_spec = pl.BlockSpec((band, cols), lambda i, me_ref: (me_ref[0] * PLACE_BANDS + i, 0))

    def body(me_ref, w_ref, shard_ref, full_ref):
        shard_ref[...] = w_ref[...].astype(BF16)
        full_ref[...] = w_ref[...].astype(BF16)

    shard, full = pl.pallas_call(
        body, name="cast_place",
        out_shape=[jax.ShapeDtypeStruct((rows, cols), BF16), jax.ShapeDtypeStruct(full_shape, BF16)],
        grid_spec=pltpu.PrefetchScalarGridSpec(
            num_scalar_prefetch=1, grid=(PLACE_BANDS,),
            in_specs=[pl.BlockSpec((None, band, cols), lambda i, me_ref: (layer, i, 0))],
            out_specs=[pl.BlockSpec((band, cols), lambda i, me_ref: (i, 0)), full_spec]),
        compiler_params=_params(("parallel",)),
    )(me.reshape(1).astype(jnp.int32), w_all)
    return _split_start("gather_start_" + tag, _gather_copies(size, axis, ks), len(ks), shard, full, after)


def _gather_wait(started, axis, ks, after, tag):
    size = started[2].shape[axis]
    return _split_wait("gather_wait_" + tag, _gather_copies(size, axis, ks), started, after)


def _pass_on_start(shard, full, axis, after, tag):
    size = shard.shape[axis]
    return _split_start("pass_on_start_" + tag, _pass_on_copies(size, axis), len(SAME_CORE_OF_CHIPS), shard, full, after)


def _pass_on_wait(started, axis, after, tag):
    size = started[2].shape[axis]
    return _split_wait("pass_on_wait_" + tag, _pass_on_copies(size, axis), started, after)[1]


def _scatter_start(grad, axis, me, tag):
    size = grad.shape[axis] // N_DEV
    rows, cols = tuple(size if a == axis else n for a, n in enumerate(grad.shape))
    band = rows // PLACE_BANDS
    if axis == 1:
        in_spec = pl.BlockSpec((band, cols), lambda i, me_ref: (i, me_ref[0]))
    else:
        in_spec = pl.BlockSpec((band, cols), lambda i, me_ref: (me_ref[0] * PLACE_BANDS + i, 0))
    out_spec = pl.BlockSpec((None, band, cols), lambda i, me_ref: (me_ref[0], i, 0))
    land = _place_own("place_slab", grad, jax.ShapeDtypeStruct((N_DEV, rows, cols), grad.dtype), in_spec, out_spec,
                      PLACE_BANDS, me)
    return _split_start("scatter_start_" + tag, _scatter_copies(size, axis), N_DEV - 1, grad, land, [])


def _scatter_wait(started, axis, after, tag):
    size = started[2].shape[axis] // N_DEV
    land = _split_wait("scatter_wait_" + tag, _scatter_copies(size, axis), started, after)[1]
    return [(land, p) for p in range(N_DEV)]


N_CHIPS = N_DEV // 2


def _pair_copies(size, axis):
    def copies(grad_ref, land_ref, send_sems, recv_sems):
        x, y, c, _ = _mesh_position()
        sibling, _ = _peer(x, y, c, SIBLING)
        sends, recvs = [], []
        for j in range(N_CHIPS):
            _, owner = _peer(x, y, c, (2 * j) ^ SIBLING)
            for lst in (sends, recvs):
                lst.append(pltpu.make_async_remote_copy(
                    src_ref=_slab(grad_ref, owner, size, axis), dst_ref=land_ref.at[j], send_sem=send_sems.at[j],
                    recv_sem=recv_sems.at[j], device_id=sibling, device_id_type=MESH))
        return sends, recvs

    return copies


def _chips_copies():
    def copies(pair_ref, land_ref, send_sems, recv_sems):
        x, y, c, _ = _mesh_position()
        sends, recvs = [], []
        for j in range(1, N_CHIPS):
            owner, _ = _peer(x, y, c, 2 * j)
            for lst in (sends, recvs):
                lst.append(pltpu.make_async_remote_copy(
                    src_ref=pair_ref.at[j], dst_ref=land_ref.at[j], send_sem=send_sems.at[j - 1],
                    recv_sem=recv_sems.at[j - 1], device_id=owner, device_id_type=MESH))
        return sends, recvs

    return copies


def _pair_start(grad, axis, tag):
    size = grad.shape[axis] // N_DEV
    rows, cols = tuple(size if a == axis else n for a, n in enumerate(grad.shape))
    land = lax.empty((N_CHIPS, rows, cols), grad.dtype)
    return _split_start("pair_start_" + tag, _pair_copies(size, axis), N_CHIPS, grad, land, [])


def _pair_sums(started, axis, me, after, tag):
    size = started[2].shape[axis] // N_DEV
    grad, land = _split_wait("pair_wait_" + tag, _pair_copies(size, axis), started, after)
    _, rows, cols = land.shape
    band = rows // PLACE_BANDS
    if axis == 1:
        mine = pl.BlockSpec((band, cols), lambda j, i, me_ref: (i, me_ref[0] ^ (2 * j)))
    else:
        mine = pl.BlockSpec((band, cols), lambda j, i, me_ref: ((me_ref[0] ^ (2 * j)) * PLACE_BANDS + i, 0))
    slot = pl.BlockSpec((None, band, cols), lambda j, i, me_ref: (j, i, 0))

    def body(own_ref, mine_ref, theirs_ref, out_ref):
        out_ref[...] = (mine_ref[...].astype(F32) + theirs_ref[...].astype(F32)).astype(out_ref.dtype)

    return pl.pallas_call(
        body, name="pair_sums", out_shape=jax.ShapeDtypeStruct(land.shape, land.dtype),
        grid_spec=pltpu.PrefetchScalarGridSpec(num_scalar_prefetch=1, grid=(N_CHIPS, PLACE_BANDS),
                                               in_specs=[mine, slot], out_specs=slot),
        compiler_params=_params(("parallel", "parallel")),
    )(me.reshape(1).astype(jnp.int32), grad, land)


def _chips_start(pairs, tag):
    return _split_start("chips_start_" + tag, _chips_copies(), N_CHIPS - 1, pairs, lax.empty(pairs.shape, pairs.dtype), [])


def _chips_wait(started, after, tag):
    pairs, land = _split_wait("chips_wait_" + tag, _chips_copies(), started, after)
    return [(pairs, 0)] + [(land, j) for j in range(1, N_CHIPS)]


def _exchange_small(buf, *, name, after=()):
    r = buf.shape[0]

    def body(*refs):
        buf_ref = refs[0]
        all_ref, sum_ref, send_sems, recv_sems = refs[1 + len(after):]
        x, y, c, me = _mesh_position()
        all_ref[me] = buf_ref[...]
        sends, recvs = [], []
        for k in range(1, N_DEV):
            peer, pid = _peer(x, y, c, k)
            sends.append(pltpu.make_async_remote_copy(
                src_ref=buf_ref, dst_ref=all_ref.at[me], send_sem=send_sems.at[k - 1], recv_sem=recv_sems.at[k - 1],
                device_id=peer, device_id_type=MESH))
            recvs.append(pltpu.make_async_remote_copy(
                src_ref=buf_ref, dst_ref=all_ref.at[pid], send_sem=send_sems.at[k - 1], recv_sem=recv_sems.at[k - 1],
                device_id=peer, device_id_type=MESH))
        for cp in sends:
            cp.start()
        for cp in recvs:
            cp.wait_recv()
        for cp in sends:
            cp.wait_send()
        total = all_ref[0]
        for p in range(1, N_DEV):
            total = total + all_ref[p]
        sum_ref[...] = total

    vmem = pl.BlockSpec(memory_space=pltpu.VMEM)
    return pl.pallas_call(
        body, name=name,
        in_specs=[vmem] + [ANY_SPEC] * len(after), out_specs=[vmem, vmem],
        out_shape=[jax.ShapeDtypeStruct((N_DEV, r, LANES), F32), jax.ShapeDtypeStruct((r, LANES), F32)],
        scratch_shapes=[pltpu.SemaphoreType.DMA((N_DEV - 1,)), pltpu.SemaphoreType.DMA((N_DEV - 1,))],
        compiler_params=pltpu.CompilerParams(has_side_effects=True),
    )(buf, *after)


def _adamw_math(w, g, m, v):
    m2 = ADAM_B1 * m + (1.0 - ADAM_B1) * g
    v2 = ADAM_B2 * v + (1.0 - ADAM_B2) * (g * g)
    delta = -ADAM_LR * ((m2 / ADAM_C1) / (jnp.sqrt(v2 / ADAM_C2) + ADAM_EPS) + ADAM_WD * w)
    return delta, m2, v2


def _adamw_slabs(layer, w, m, v, addends, outs, order, *, tr, name):
    depth, r, c = w.shape
    tr = _tile(r, tr)
    n = len(addends)

    def body(*refs):
        w_ref, m_ref, v_ref = refs[:3]
        g_ref, d_ref, m2_ref, v2_ref = refs[-4:]
        g = refs[3][...].astype(F32)
        for a_ref in refs[4:3 + n]:
            g = g + a_ref[...].astype(F32)
        delta, m2, v2 = _adamw_math(w_ref[...], g, m_ref[...], v_ref[...])
        g_ref[...] = g
        d_ref[...] = delta
        m2_ref[...] = m2
        v2_ref[...] = v2

    row = pl.BlockSpec((None, tr, c), lambda i: (layer, i, 0))
    slots = [pl.BlockSpec((None, tr, c), lambda i, p=p: (p, i, 0)) for _, p in addends]
    first_out = 3 + n + 1
    return pl.pallas_call(
        body, name=name, grid=(r // tr,),
        in_specs=[row, row, row] + slots + [pl.BlockSpec((8, LANES), lambda i: (0, 0))] + [ANY_SPEC] * 4,
        out_specs=[row] * 4, out_shape=[jax.ShapeDtypeStruct((depth, r, c), F32)] * 4,
        input_output_aliases={first_out + t: t for t in range(4)},
        compiler_params=_params(("parallel",)),
    )(w, m, v, *[a for a, _ in addends], order, *outs)


def _adamw_small(w, g, m, v):
    def body(w_ref, g_ref, m_ref, v_ref, d_ref, m2_ref, v2_ref):
        delta, m2, v2 = _adamw_math(w_ref[...], g_ref[...], m_ref[...], v_ref[...])
        d_ref[...] = delta
        m2_ref[...] = m2
        v2_ref[...] = v2

    vmem = pl.BlockSpec(memory_space=pltpu.VMEM)
    return pl.pallas_call(
        body, name="adamw_small", in_specs=[vmem] * 4, out_specs=[vmem] * 3,
        out_shape=[jax.ShapeDtypeStruct(w.shape, F32)] * 3,
    )(w, g, m, v)


def _pack(parts):
    flat = jnp.concatenate([p.reshape(-1).astype(F32) for p in parts])
    rows = -(-flat.shape[0] // LANES)
    rows = -(-rows // SMALL_ROWS_ALIGN) * SMALL_ROWS_ALIGN
    flat = jnp.pad(flat, (0, rows * LANES - flat.shape[0]))
    return flat.reshape(rows, LANES)


def _unpack(buf, shapes):
    flat = buf.reshape(-1)
    out, pos = [], 0
    for shp in shapes:
        size = math.prod(shp)
        out.append(flat[pos:pos + size].reshape(shp))
        pos += size
    return out


def _section_widths(d):
    aw = d // 2
    kw = aw // ATTN_GROUP
    rw = d - aw
    rqw = (rw // RET_V_DIM) * RET_QK_DIM
    return (aw, kw, kw, aw, rqw, rqw, rw, rw)


def _layer_fwd(xl, hh, win_full, behind, after_attn, wout_of, qn, kn, dec_f, dec_b, rn, cos, sin):
    h, ht = hh
    aq, ak, v, ag, rq, rk, rvb, rg = _proj_sections(h, win_full, _section_widths(xl.shape[1]),
                                                    (F32, F32, BF16, F32, F32, F32, BF16, F32), after=behind)
    q, k, rqr, rkr = _prep_fwd(aq, ak, rq, rk, cos, sin, qn, kn)
    att, lse = _attn_fwd(q, k, v)
    ret = _retc_fwd(rqr, rkr, rvb, dec_f + after_attn(att), dec_b)
    y, yt = _gate_fwd(att, ag, ret, rg, rn)
    wout_full = wout_of(y)
    xn = _matmul(y, wout_full, name="out_proj", residual=xl)
    saved = dict(x=xl, ht=ht, aq=aq, ak=ak, ag=ag, rg=rg, q=q, k=k, v=v, rq=rqr, rk=rkr, rv=rvb,
                 att=att, lse=lse, ret=ret, yt=yt, win=win_full, wout=wout_full)
    return xn, saved


def _layer_bwd_weights(gb, sv, qn, kn, dec_f, dec_b, rn, cos, sin, on_dwout):
    dy = _matmul(gb, sv["wout"], name="d_y", trans_b=True)
    dwout = _matmul(sv["yt"], gb, name="d_wout", out_dtype=BF16)
    datt, dag, dret, drg, drn = _gate_bwd(dy, sv["att"], sv["ag"], sv["ret"], sv["rg"], rn + on_dwout(dwout))
    dq, dk, dav = _attn_bwd(sv["q"], sv["k"], sv["v"], sv["att"], datt, sv["lse"])
    drq, drk, drv, gf, gbk = _retc_bwd(sv["rq"], sv["rk"], sv["rv"], dret, dec_f, dec_b)
    dproj, dqn, dkn = _prep_bwd(dq, dk, drq, drk, sv["aq"], sv["ak"], cos, sin, qn, kn, dav, dag, drv, drg)
    dwin = _matmul(sv["ht"], dproj, name="d_win", out_dtype=BF16)
    small = dict(qn=dqn[0], kn=dkn[0], df=gf[:, 0, 0], db=gbk[:, 0, 0], rn=drn[0])
    return dproj, dwin, small


def _layer_bwd_input(g, dproj, sv, nw, behind, after_dh):
    dh = _matmul(dproj, sv["win"], name="d_h", trans_b=True, tm=512, tk=dproj.shape[1], after=behind)
    g, gb, dnw = _rms_bwd(dh, sv["x"], g, nw + after_dh(dh))
    return g, gb, dnw[0]


def kernel(x, norm_w, w_in, q_norm, k_norm, ret_decay_fwd, ret_decay_bwd, ret_norm, w_out, final_norm, loss_target, m_norm_w, m_w_in, m_q_norm, m_k_norm, m_ret_decay_fwd, m_ret_decay_bwd, m_ret_norm, m_w_out, m_final_norm, v_norm_w, v_w_in, v_q_norm, v_k_norm, v_ret_decay_fwd, v_ret_decay_bwd, v_ret_norm, v_w_out, v_final_norm):
    depth, d, _ = w_in.shape
    seq = x.shape[1]
    rw = _section_widths(d)[6]
    rheads = rw // RET_V_DIM
    rns = ret_norm.shape[-1]
    _, _, _, me = _mesh_position()

    target = loss_target[0]
    cos, sin = _rope_tables(seq)

    rn_all, _ = _exchange_small(_pack([ret_norm]), name="gather_ret_norm")
    rn_full = rn_all.reshape(N_DEV, -1)[:, :depth * rheads * rns].reshape(N_DEV, depth, rheads, rns)
    rn_full = jnp.transpose(rn_full, (1, 2, 0, 3)).reshape(depth, rw)

    dec_f = jnp.broadcast_to(ret_decay_fwd[:, :, None, None], (depth, rheads, 1, LANES))
    dec_b = jnp.broadcast_to(ret_decay_bwd[:, :, None, None], (depth, rheads, 1, LANES))


    saved = []
    xl = x[0]
    first = (SIBLING,) + SAME_CORE_OF_CHIPS
    in_sent = _gather_start(w_in, 0, 1, first, me, [], "in0")
    hh = _rms_fwd(xl, norm_w[0] + in_sent[-1][0, 0])
    shard, landed = _gather_wait(in_sent, 1, first, [hh[0]], "in0")
    win_full = _pass_on_wait(_pass_on_start(shard, landed, 1, [], "in0"), 1, [], "in0")
    for l in range(depth):
        if l > 0:
            hh = _rms_fwd(xl, norm_w[l])
        out_sent = _gather_start(w_out, l, 0, ALL_PEERS, me, [win_full], "out" + str(l))
        behind = [out_sent[-1]]
        passed = {}
        if l + 1 < depth:
            in_sent = _gather_start(w_in, l + 1, 1, first, me, [win_full, out_sent[-1]], "in" + str(l + 1))
            behind.append(in_sent[-1])

        def after_attn(att, passed=passed, l=l):
            if l + 1 == depth:
                return 0.0
            shard, landed = _gather_wait(in_sent, 1, first, [att], "in" + str(l + 1))
            passed["on"] = _pass_on_start(shard, landed, 1, [], "in" + str(l + 1))
            return passed["on"][-1][0, 0]

        def wout_of(y, out_sent=out_sent, l=l):
            return _gather_wait(out_sent, 0, ALL_PEERS, [y], "out" + str(l))[1]

        xl, sv = _layer_fwd(xl, hh, win_full, behind, after_attn, wout_of, q_norm[l], k_norm[l], dec_f[l], dec_b[l],
                            rn_full[l], cos, sin)
        saved.append(sv)
        if l + 1 < depth:
            win_full = _pass_on_wait(passed["on"], 1, [xl], "in" + str(l + 1))

    loss_row, g, gb, d_final = _loss_head(xl, target, final_norm)

    d_norm, d_qn, d_kn, d_df, d_db, d_rn = [], [], [], [], [], []
    lands = [None] * depth
    pending = None
    for l in reversed(range(depth)):
        sent = {}

        def on_dwout(dwout, sent=sent, l=l):
            sent["out"] = _scatter_start(dwout, 0, me, "out" + str(l))
            return sent["out"][-1][0, 0]

        def after_dh(dh, sent=sent, l=l):
            pairs = _pair_sums(sent["pair"], 1, me, [dh], "in" + str(l))
            sent["in"] = _chips_start(pairs, "in" + str(l))
            return sent["in"][-1][0, 0]

        dproj, dwin, sm = _layer_bwd_weights(gb, saved[l], q_norm[l], k_norm[l], dec_f[l], dec_b[l],
                                             rn_full[l], cos, sin, on_dwout)
        sent["pair"] = _pair_start(dwin, 1, "in" + str(l))
        g, gb, dnw = _layer_bwd_input(g, dproj, saved[l], norm_w[l], [sent["pair"][-1]], after_dh)
        if pending is not None:
            lands[l + 1] = (_chips_wait(pending["in"], [g], "in" + str(l + 1)),
                            _scatter_wait(pending["out"], 0, [g], "out" + str(l + 1)))
        pending = sent
        d_norm.append(dnw)
        d_qn.append(sm["qn"])
        d_kn.append(sm["kn"])
        d_df.append(sm["df"])
        d_db.append(sm["db"])
        d_rn.append(sm["rn"])
    for lst in (d_norm, d_qn, d_kn, d_df, d_db, d_rn):
        lst.reverse()
    order = pending["in"][-1]
    in_outs = [lax.empty(w_in.shape, F32) for _ in range(4)]
    out_outs = [lax.empty(w_out.shape, F32) for _ in range(4)]
    for l in reversed(range(1, depth)):
        in_outs = _adamw_slabs(l, w_in, m_w_in, v_w_in, lands[l][0], in_outs, order, tr=256, name="adamw_w_in")
        out_outs = _adamw_slabs(l, w_out, m_w_out, v_w_out, lands[l][1], out_outs, order, tr=64, name="adamw_w_out")
    land_out = _scatter_wait(pending["out"], 0, [g, out_outs[0]], "out0")
    out_outs = _adamw_slabs(0, w_out, m_w_out, v_w_out, land_out, out_outs, order, tr=64, name="adamw_w_out")

    small_shapes = [(depth, d), (depth, HEAD_DIM), (depth, HEAD_DIM), (depth, rheads), (depth, rheads),
                    (depth, rheads, N_DEV * rns), (d,), (1,)]
    grads_local = [jnp.stack(d_norm), jnp.stack(d_qn), jnp.stack(d_kn), jnp.stack(d_df), jnp.stack(d_db),
                   jnp.stack(d_rn).reshape(depth, rheads, N_DEV * rns), d_final[0], loss_row[0, :1]]
    _, gsum = _exchange_small(_pack(grads_local), name="all_reduce_small", after=(in_outs[0], out_outs[0]))
    land_in = _chips_wait(pending["in"], [g, gsum], "in0")
    in_outs = _adamw_slabs(0, w_in, m_w_in, v_w_in, land_in, in_outs, order, tr=256, name="adamw_w_in")
    g_norm, g_qn, g_kn, g_df, g_db, g_rn_full, g_final, loss = _unpack(gsum, small_shapes)
    g_rn = lax.dynamic_slice_in_dim(g_rn_full, me * rns, rns, axis=2)
    small_g = [g_norm, g_qn, g_kn, g_df, g_db, g_rn, g_final]
    small_w = [norm_w, q_norm, k_norm, ret_decay_fwd, ret_decay_bwd, ret_norm, final_norm]
    small_m = [m_norm_w, m_q_norm, m_k_norm, m_ret_decay_fwd, m_ret_decay_bwd, m_ret_norm, m_final_norm]
    small_v = [v_norm_w, v_q_norm, v_k_norm, v_ret_decay_fwd, v_ret_decay_bwd, v_ret_norm, v_final_norm]
    shapes = [a.shape for a in small_w]
    sd, sm, sv2 = _adamw_small(_pack(small_w), _pack(small_g), _pack(small_m), _pack(small_v))
    small_d, small_m2, small_v2 = _unpack(sd, shapes), _unpack(sm, shapes), _unpack(sv2, shapes)

    def ordered(small, win_v, wout_v):
        return [small[0], win_v, small[1], small[2], small[3], small[4], small[5], wout_v, small[6]]

    grads = ordered(small_g, in_outs[0], out_outs[0])
    deltas = ordered(small_d, in_outs[1], out_outs[1])
    new_m = ordered(small_m2, in_outs[2], out_outs[2])
    new_v = ordered(small_v2, in_outs[3], out_outs[3])
    return (loss.reshape(()), g[None], *grads, *deltas, *new_m, *new_v)
```

```python
import math

import jax
import jax.numpy as jnp
import numpy as np
from jax import lax
from jax.experimental import pallas as pl
from jax.experimental.pallas import tpu as pltpu

F32 = jnp.float32
BF16 = jnp.bfloat16

N_DEV = 8
HEAD_DIM = 128
ATTN_GROUP = 4
RET_QK_DIM = 128
RET_V_DIM = 256
GRID_W = 64
ROPE_THETA = 10000.0
EPS = 1e-6
ADAM_LR = 0.001
ADAM_B1 = 0.9
ADAM_B2 = 0.999
ADAM_EPS = 1e-08
ADAM_WD = 0.01
ADAM_STEP = 10
ADAM_C1 = 1.0 - ADAM_B1 ** ADAM_STEP
ADAM_C2 = 1.0 - ADAM_B2 ** ADAM_STEP
LANES = 128
SMALL_ROWS_ALIGN = 8
VMEM_LIMIT = 56 * 1024 * 1024

NT_DIMS = (((1,), (1,)), ((), ()))
TN_DIMS = (((0,), (0,)), ((), ()))
MESH = pl.DeviceIdType.MESH


def _params(sem):
    return pltpu.CompilerParams(dimension_semantics=sem, vmem_limit_bytes=VMEM_LIMIT)


def _tile(dim, pref, align=16):
    if dim <= pref:
        return dim
    for t in range(pref - pref % align, 0, -align):
        if dim % t == 0:
            return t
    raise ValueError((dim, pref, align))


def _silu_parts(z):
    sg = 1.0 / (1.0 + jnp.exp(-z))
    return z * sg, sg * (1.0 + z * (1.0 - sg))


def _log_sigmoid(x):
    return jnp.minimum(x, 0.0) - jnp.log(1.0 + jnp.exp(-jnp.abs(x)))


def _swap_pairs(z):
    src = lax.broadcasted_iota(jnp.int32, (HEAD_DIM, HEAD_DIM), 0)
    dst = lax.broadcasted_iota(jnp.int32, (HEAD_DIM, HEAD_DIM), 1)
    partner = jnp.where((dst % 64) < 32, dst + 32, dst - 32)
    perm = (src == partner).astype(F32)
    return jnp.dot(z, perm, precision=lax.Precision.HIGH, preferred_element_type=F32)


def _rope(z, cos, sin):
    return z * cos + _swap_pairs(z) * sin


def _rope_transposed(d, cos, sin):
    return d * cos + _swap_pairs(d * sin)


def _rope_tables(seq):
    rows = seq // GRID_W
    row = jnp.repeat(jnp.arange(rows), GRID_W).astype(F32)
    col = jnp.tile(jnp.arange(GRID_W), rows).astype(F32)
    axis_dim = HEAD_DIM // 2
    inv = ROPE_THETA ** (-jnp.arange(0, axis_dim, 2, dtype=F32) / axis_dim)
    ar = row[:, None] * inv[None, :]
    ac = col[:, None] * inv[None, :]
    cos = jnp.concatenate([jnp.cos(ar), jnp.cos(ar), jnp.cos(ac), jnp.cos(ac)], axis=-1)
    sin = jnp.concatenate([-jnp.sin(ar), jnp.sin(ar), -jnp.sin(ac), jnp.sin(ac)], axis=-1)
    return cos, sin


def _matmul(a, b, *, name, trans_b=False, out_dtype=F32, residual=None, tm=1024, tn=512, tk=4096, after=()):
    m, k = a.shape
    n = b.shape[0] if trans_b else b.shape[1]
    tm, tn, tk = _tile(m, tm), _tile(n, tn, LANES), _tile(k, tk, LANES)
    nk = k // tk
    has_res = residual is not None

    def body(*refs):
        a_ref, b_ref = refs[:2]
        r_ref = refs[2] if has_res else None
        o_ref = refs[2 + has_res + len(after)]
        if trans_b:
            part = lax.dot_general(a_ref[...], b_ref[...], NT_DIMS, preferred_element_type=F32)
        else:
            part = jnp.dot(a_ref[...], b_ref[...], preferred_element_type=F32)

        def finish(r):
            if has_res:
                r = r + r_ref[...]
            o_ref[...] = r.astype(o_ref.dtype)

        if nk == 1:
            finish(part)
        else:
            acc_ref = refs[-1]
            kk = pl.program_id(2)

            @pl.when(kk == 0)
            def _():
                acc_ref[...] = part

            @pl.when(kk > 0)
            def _():
                acc_ref[...] += part

            @pl.when(kk == nk - 1)
            def _():
                finish(acc_ref[...])

    if trans_b:
        b_spec = pl.BlockSpec((tn, tk), lambda i, j, kk: (j, kk))
    else:
        b_spec = pl.BlockSpec((tk, tn), lambda i, j, kk: (kk, j))
    in_specs = [pl.BlockSpec((tm, tk), lambda i, j, kk: (i, kk)), b_spec]
    args = [a, b]
    if has_res:
        in_specs.append(pl.BlockSpec((tm, tn), lambda i, j, kk: (i, j)))
        args.append(residual)
    in_specs += [ANY_SPEC] * len(after)
    args += list(after)
    return pl.pallas_call(
        body, name=name, grid=(m // tm, n // tn, nk),
        in_specs=in_specs,
        out_specs=pl.BlockSpec((tm, tn), lambda i, j, kk: (i, j)),
        out_shape=jax.ShapeDtypeStruct((m, n), out_dtype),
        scratch_shapes=[pltpu.VMEM((tm, tn), F32)] if nk > 1 else [],
        compiler_params=_params(("parallel", "parallel", "arbitrary")),
    )(*args)


def _proj_sections(a, b, widths, dtypes, *, tm=1024, tn=512, after=()):
    m, k = a.shape
    tm = _tile(m, tm)
    tn = _tile(min(widths), tn, LANES)
    assert all(w % tn == 0 for w in widths) and sum(widths) == b.shape[1]
    nblk = [w // tn for w in widths]
    first = [int(o) // tn for o in np.cumsum((0,) + tuple(widths))[:-1]]

    def body(a_ref, b_ref, *refs):
        j = pl.program_id(1)
        part = jnp.dot(a_ref[...], b_ref[...], preferred_element_type=F32)
        for o_ref, lo, n in zip(refs[len(after):], first, nblk):
            @pl.when(jnp.logical_and(j >= lo, j < lo + n))
            def _(o_ref=o_ref):
                o_ref[...] = part.astype(o_ref.dtype)

    out_specs = [pl.BlockSpec((tm, tn), lambda i, j, lo=lo, n=n: (i, jnp.clip(j - lo, 0, n - 1)))
                 for lo, n in zip(first, nblk)]
    return pl.pallas_call(
        body, name="proj", grid=(m // tm, b.shape[1] // tn),
        in_specs=[pl.BlockSpec((tm, k), lambda i, j: (i, 0)), pl.BlockSpec((k, tn), lambda i, j: (0, j))]
        + [ANY_SPEC] * len(after),
        out_specs=out_specs,
        out_shape=[jax.ShapeDtypeStruct((m, w), dt) for w, dt in zip(widths, dtypes)],
        compiler_params=_params(("arbitrary", "arbitrary")),
    )(a, b, *after)


def _rms_fwd(x, w, *, ts=256):
    s, d = x.shape
    ts = _tile(s, ts)

    def body(x_ref, w_ref, h_ref, ht_ref):
        xv = x_ref[...]
        r = lax.rsqrt(jnp.mean(xv * xv, axis=-1, keepdims=True) + EPS)
        h = xv * r * w_ref[...]
        h_ref[...] = h.astype(BF16)
        ht_ref[...] = h.T.astype(BF16)

    row = pl.BlockSpec((ts, d), lambda i: (i, 0))
    return pl.pallas_call(
        body, name="rms_fwd", grid=(s // ts,),
        in_specs=[row, pl.BlockSpec((1, d), lambda i: (0, 0))],
        out_specs=[row, pl.BlockSpec((d, ts), lambda i: (0, i))],
        out_shape=[jax.ShapeDtypeStruct((s, d), BF16), jax.ShapeDtypeStruct((d, s), BF16)],
        compiler_params=_params(("parallel",)),
    )(x, w.reshape(1, d))


def _rms_bwd(dh, x, g, w, *, ts=256):
    s, d = x.shape
    ts = _tile(s, ts)

    def body(dh_ref, x_ref, g_ref, w_ref, dx_ref, dxb_ref, dw_ref):
        xv = x_ref[...]
        r = lax.rsqrt(jnp.mean(xv * xv, axis=-1, keepdims=True) + EPS)
        xh = xv * r
        dhv = dh_ref[...]
        dn = dhv * w_ref[...]
        dx = g_ref[...] + r * (dn - xh * jnp.mean(dn * xh, axis=-1, keepdims=True))
        dx_ref[...] = dx
        dxb_ref[...] = dx.astype(BF16)
        part = jnp.sum(dhv * xh, axis=0, keepdims=True)

        @pl.when(pl.program_id(0) == 0)
        def _():
            dw_ref[...] = part

        @pl.when(pl.program_id(0) > 0)
        def _():
            dw_ref[...] += part

    row = pl.BlockSpec((ts, d), lambda i: (i, 0))
    vec = pl.BlockSpec((1, d), lambda i: (0, 0))
    return pl.pallas_call(
        body, name="rms_bwd", grid=(s // ts,),
        in_specs=[row, row, row, vec],
        out_specs=[row, row, vec],
        out_shape=[jax.ShapeDtypeStruct((s, d), F32), jax.ShapeDtypeStruct((s, d), BF16),
                   jax.ShapeDtypeStruct((1, d), F32)],
        compiler_params=_params(("arbitrary",)),
    )(dh, x, g, w.reshape(1, d))


def _loss_head(x, target, w, *, ts=256):
    s, d = x.shape
    ts = _tile(s, ts)

    def body(x_ref, t_ref, w_ref, loss_ref, dx_ref, dxb_ref, dw_ref):
        xv = x_ref[...]
        r = lax.rsqrt(jnp.mean(xv * xv, axis=-1, keepdims=True) + EPS)
        xh = xv * r
        wv = w_ref[...]
        diff = xh * wv - t_ref[...]
        lpart = 0.5 * jnp.sum(jnp.mean(diff * diff, axis=-1, keepdims=True), axis=0, keepdims=True)
        dout = diff * (1.0 / d)
        dn = dout * wv
        dx = r * (dn - xh * jnp.mean(dn * xh, axis=-1, keepdims=True))
        dx_ref[...] = dx
        dxb_ref[...] = dx.astype(BF16)
        part = jnp.sum(dout * xh, axis=0, keepdims=True)
        lrow = jnp.broadcast_to(lpart, loss_ref.shape)

        @pl.when(pl.program_id(0) == 0)
        def _():
            dw_ref[...] = part
            loss_ref[...] = lrow

        @pl.when(pl.program_id(0) > 0)
        def _():
            dw_ref[...] += part
            loss_ref[...] += lrow

    row = pl.BlockSpec((ts, d), lambda i: (i, 0))
    vec = pl.BlockSpec((1, d), lambda i: (0, 0))
    return pl.pallas_call(
        body, name="loss_head", grid=(s // ts,),
        in_specs=[row, row, vec],
        out_specs=[pl.BlockSpec((1, LANES), lambda i: (0, 0)), row, row, vec],
        out_shape=[jax.ShapeDtypeStruct((1, LANES), F32), jax.ShapeDtypeStruct((s, d), F32),
                   jax.ShapeDtypeStruct((s, d), BF16), jax.ShapeDtypeStruct((1, d), F32)],
        compiler_params=_params(("arbitrary",)),
    )(x, target, w.reshape(1, d))


def _prep_fwd(aq, ak, rq, rk, cos, sin, qw, kw, *, ts=256):
    s = aq.shape[0]
    ts = _tile(s, ts)
    attn_scale = HEAD_DIM ** -0.5
    ret_scale = RET_QK_DIM ** -0.5
    nq, nk, nr = aq.shape[1] // HEAD_DIM, ak.shape[1] // HEAD_DIM, rq.shape[1] // RET_QK_DIM

    def body(aq_ref, ak_ref, rq_ref, rk_ref, cos_ref, sin_ref, qw_ref, kw_ref,
             q_out, k_out, rq_out, rk_out):
        c, sn = cos_ref[...], sin_ref[...]

        def normed(u, w):
            return u * lax.rsqrt(jnp.mean(u * u, axis=-1, keepdims=True) + EPS) * w

        for j in range(nq):
            sl = slice(j * HEAD_DIM, (j + 1) * HEAD_DIM)
            q_out[:, sl] = (_rope(normed(aq_ref[:, sl], qw_ref[...]), c, sn) * attn_scale).astype(BF16)
        for j in range(nk):
            sl = slice(j * HEAD_DIM, (j + 1) * HEAD_DIM)
            k_out[:, sl] = _rope(normed(ak_ref[:, sl], kw_ref[...]), c, sn).astype(BF16)
        for j in range(nr):
            sl = slice(j * RET_QK_DIM, (j + 1) * RET_QK_DIM)
            rq_out[:, sl] = _rope(rq_ref[:, sl], c, sn).astype(BF16)
            rk_out[:, sl] = (_rope(rk_ref[:, sl], c, sn) * ret_scale).astype(BF16)

    def row(arr):
        return pl.BlockSpec((ts, arr.shape[1]), lambda i: (i, 0))

    vec = pl.BlockSpec((1, HEAD_DIM), lambda i: (0, 0))
    ins = [aq, ak, rq, rk]
    return pl.pallas_call(
        body, name="prep_fwd", grid=(s // ts,),
        in_specs=[row(a) for a in ins] + [row(cos), row(sin), vec, vec],
        out_specs=[row(a) for a in ins],
        out_shape=[jax.ShapeDtypeStruct(a.shape, BF16) for a in ins],
        compiler_params=_params(("parallel",)),
    )(*ins, cos, sin, qw.reshape(1, HEAD_DIM), kw.reshape(1, HEAD_DIM))


def _prep_bwd(dq, dk, drq, drk, aq, ak, cos, sin, qw, kw, dav, dag, drv, drg, *, ts=256):
    s = aq.shape[0]
    ts = _tile(s, ts)
    attn_scale = HEAD_DIM ** -0.5
    ret_scale = RET_QK_DIM ** -0.5
    nq, nk, nr = aq.shape[1] // HEAD_DIM, ak.shape[1] // HEAD_DIM, drq.shape[1] // RET_QK_DIM
    widths = (aq.shape[1], ak.shape[1], dav.shape[1], dag.shape[1], drq.shape[1], drk.shape[1], drv.shape[1],
              drg.shape[1])
    o_aq, o_ak, o_av, o_ag, o_rq, o_rk, o_rv, o_rg = (int(o) for o in np.cumsum((0,) + widths)[:-1])

    def body(dq_ref, dk_ref, drq_ref, drk_ref, aq_ref, ak_ref, cos_ref, sin_ref, dav_ref, dag_ref, drv_ref, drg_ref,
             qw_ref, kw_ref, dproj_ref, dqw_ref, dkw_ref):
        c, sn = cos_ref[...], sin_ref[...]
        for ref, off in ((dav_ref, o_av), (dag_ref, o_ag), (drv_ref, o_rv), (drg_ref, o_rg)):
            dproj_ref[:, off:off + ref.shape[1]] = ref[...]

        def unrope(d):
            return _rope_transposed(d, c, sn)

        def norm_bwd(dun, u, w):
            r = lax.rsqrt(jnp.mean(u * u, axis=-1, keepdims=True) + EPS)
            uh = u * r
            dn = dun * w
            du = r * (dn - uh * jnp.mean(dn * uh, axis=-1, keepdims=True))
            return du, jnp.sum(dun * uh, axis=0, keepdims=True)

        dqw = jnp.zeros((1, HEAD_DIM), F32)
        for j in range(nq):
            sl = slice(j * HEAD_DIM, (j + 1) * HEAD_DIM)
            du, dw = norm_bwd(unrope(dq_ref[:, sl] * attn_scale), aq_ref[:, sl], qw_ref[...])
            dproj_ref[:, o_aq + j * HEAD_DIM:o_aq + (j + 1) * HEAD_DIM] = du.astype(BF16)
            dqw = dqw + dw
        dkw = jnp.zeros((1, HEAD_DIM), F32)
        for j in range(nk):
            sl = slice(j * HEAD_DIM, (j + 1) * HEAD_DIM)
            du, dw = norm_bwd(unrope(dk_ref[:, sl]), ak_ref[:, sl], kw_ref[...])
            dproj_ref[:, o_ak + j * HEAD_DIM:o_ak + (j + 1) * HEAD_DIM] = du.astype(BF16)
            dkw = dkw + dw
        for j in range(nr):
            sl = slice(j * RET_QK_DIM, (j + 1) * RET_QK_DIM)
            dproj_ref[:, o_rq + j * RET_QK_DIM:o_rq + (j + 1) * RET_QK_DIM] = unrope(drq_ref[:, sl]).astype(BF16)
            dproj_ref[:, o_rk + j * RET_QK_DIM:o_rk + (j + 1) * RET_QK_DIM] = (
                unrope(drk_ref[:, sl] * ret_scale).astype(BF16))

        @pl.when(pl.program_id(0) == 0)
        def _():
            dqw_ref[...] = dqw
            dkw_ref[...] = dkw

        @pl.when(pl.program_id(0) > 0)
        def _():
            dqw_ref[...] += dqw
            dkw_ref[...] += dkw

    def row(arr):
        return pl.BlockSpec((ts, arr.shape[1]), lambda i: (i, 0))

    vec = pl.BlockSpec((1, HEAD_DIM), lambda i: (0, 0))
    ins = [dq, dk, drq, drk, aq, ak, cos, sin, dav, dag, drv, drg]
    total = sum(widths)
    return pl.pallas_call(
        body, name="prep_bwd", grid=(s // ts,),
        in_specs=[row(a) for a in ins] + [vec, vec],
        out_specs=[pl.BlockSpec((ts, total), lambda i: (i, 0)), vec, vec],
        out_shape=[jax.ShapeDtypeStruct((s, total), BF16)] + [jax.ShapeDtypeStruct((1, HEAD_DIM), F32)] * 2,
        compiler_params=_params(("arbitrary",)),
    )(*ins, qw.reshape(1, HEAD_DIM), kw.reshape(1, HEAD_DIM))


def _attn_fwd(q, k, v, *, tq=4096, sub=256):
    s, aw = q.shape
    tq = _tile(s, tq)
    sub = _tile(tq, sub)
    heads, kvh = aw // HEAD_DIM, k.shape[1] // HEAD_DIM
    grp = heads // kvh

    def body(q_ref, k_ref, v_ref, o_ref, lse_ref):
        kv_ = k_ref[...]
        v_ext = jnp.concatenate([v_ref[...], jnp.ones((s, HEAD_DIM), BF16)], axis=-1)
        for r in range(tq // sub):
            rows = slice(r * sub, (r + 1) * sub)
            sc = lax.dot_general(q_ref[rows, :], kv_, NT_DIMS, preferred_element_type=F32)
            m = jnp.max(sc, axis=-1, keepdims=True)
            p = jnp.exp((sc - m).astype(BF16))
            oe = jnp.dot(p, v_ext, preferred_element_type=F32)
            l = oe[:, HEAD_DIM:HEAD_DIM + 1]
            o_ref[rows, :] = (oe[:, :HEAD_DIM] / l).astype(o_ref.dtype)
            lse_ref[rows, :] = jnp.broadcast_to(m + jnp.log(l), (sub, HEAD_DIM))

    qspec = pl.BlockSpec((tq, HEAD_DIM), lambda kv, g, i: (i, kv * grp + g))
    kspec = pl.BlockSpec((s, HEAD_DIM), lambda kv, g, i: (0, kv))
    return pl.pallas_call(
        body, name="attn_fwd", grid=(kvh, grp, s // tq),
        in_specs=[qspec, kspec, kspec],
        out_specs=[qspec, qspec],
        out_shape=[jax.ShapeDtypeStruct((s, aw), BF16), jax.ShapeDtypeStruct((s, aw), F32)],
        compiler_params=_params(("parallel", "parallel", "parallel")),
    )(q, k, v)


def _attn_bwd(q, k, v, o, do, lse, *, tq=1024, sub=256):
    s, aw = q.shape
    tq = _tile(s, tq)
    sub = _tile(tq, sub)
    heads, kvh = aw // HEAD_DIM, k.shape[1] // HEAD_DIM
    grp = heads // kvh
    nq = s // tq

    def body(q_ref, k_ref, v_ref, o_ref, do_ref, lse_ref, dq_ref, dk_ref, dv_ref, dk_acc, dv_acc, p_scr, ds_scr):
        g, i = pl.program_id(1), pl.program_id(2)
        kv_, vv = k_ref[...], v_ref[...]
        for r in range(tq // sub):
            rows = slice(r * sub, (r + 1) * sub)
            qv, dov = q_ref[rows, :], do_ref[rows, :]
            sc = lax.dot_general(qv, kv_, NT_DIMS, preferred_element_type=F32)
            p = jnp.exp((sc - lse_ref[rows, :1]).astype(BF16))
            dp = lax.dot_general(dov, vv, NT_DIMS, preferred_element_type=F32)
            delta = jnp.sum(dov.astype(F32) * o_ref[rows, :].astype(F32), axis=-1, keepdims=True)
            ds = p * (dp - delta).astype(BF16)
            dq_ref[rows, :] = jnp.dot(ds, kv_, preferred_element_type=F32)
            p_scr[rows, :] = p
            ds_scr[rows, :] = ds
        dvp = lax.dot_general(p_scr[...], do_ref[...], TN_DIMS, preferred_element_type=F32)
        dkp = lax.dot_general(ds_scr[...], q_ref[...], TN_DIMS, preferred_element_type=F32)
        first = jnp.logical_and(g == 0, i == 0)

        @pl.when(first)
        def _():
            dv_acc[...] = dvp
            dk_acc[...] = dkp

        @pl.when(jnp.logical_not(first))
        def _():
            dv_acc[...] += dvp
            dk_acc[...] += dkp

        @pl.when(jnp.logical_and(g == grp - 1, i == nq - 1))
        def _():
            dk_ref[...] = dk_acc[...]
            dv_ref[...] = dv_acc[...].astype(dv_ref.dtype)

    qspec = pl.BlockSpec((tq, HEAD_DIM), lambda kv, g, i: (i, kv * grp + g))
    kspec = pl.BlockSpec((s, HEAD_DIM), lambda kv, g, i: (0, kv))
    return pl.pallas_call(
        body, name="attn_bwd", grid=(kvh, grp, nq),
        in_specs=[qspec, kspec, kspec, qspec, qspec, qspec],
        out_specs=[qspec, kspec, kspec],
        out_shape=[jax.ShapeDtypeStruct((s, aw), F32), jax.ShapeDtypeStruct(k.shape, F32),
                   jax.ShapeDtypeStruct(v.shape, BF16)],
        scratch_shapes=[pltpu.VMEM((s, HEAD_DIM), F32), pltpu.VMEM((s, HEAD_DIM), F32),
                        pltpu.VMEM((tq, s), BF16), pltpu.VMEM((tq, s), BF16)],
        compiler_params=_params(("parallel", "arbitrary", "arbitrary")),
    )(q, k, v, o, do, lse)


def _sum_all(z):
    return jnp.sum(jnp.sum(z, axis=0, keepdims=True), axis=1, keepdims=True)


def _chunk_consts(df_ref, db_ref, t):
    lf = _log_sigmoid(df_ref[0][:, :1])
    lb = _log_sigmoid(db_ref[0][:, :1])
    r = lax.broadcasted_iota(jnp.int32, (t, 1), 0).astype(F32)
    c = lax.broadcasted_iota(jnp.int32, (1, t), 1).astype(F32)
    diff = r - c
    dm = jnp.exp(diff * jnp.where(diff >= 0, lf, -lb))
    return dict(diff=diff, dm=dm, r=r,
                af=jnp.exp(lf * (r + 1.0)), bf=jnp.exp(lf * (t - 1.0 - r)), gf=jnp.exp(lf * t),
                ab=jnp.exp(lb * (t - r)), bb=jnp.exp(lb * r), gb=jnp.exp(lb * t))


def _scaled(x, f):
    return (x.astype(F32) * f).astype(BF16)


def _retc_specs(s):
    qspec = pl.BlockSpec((s, RET_QK_DIM), lambda h: (0, h))
    vspec = pl.BlockSpec((s, RET_V_DIM), lambda h: (0, h))
    dspec = pl.BlockSpec((1, 1, LANES), lambda h: (h, 0, 0))
    return qspec, vspec, dspec


def _retc_fwd(q, k, v, dec_f, dec_b, *, t=256):
    s, qw = q.shape
    t = _tile(s, t)
    heads, nc = qw // RET_QK_DIM, s // t
    qspec, vspec, dspec = _retc_specs(s)

    def body(q_ref, k_ref, v_ref, df_ref, db_ref, o_ref):
        cs = _chunk_consts(df_ref, db_ref, t)

        def rows_of(i):
            return pl.ds(pl.multiple_of(i * t, t), t)

        def forward(i, sf):
            rows = rows_of(i)
            qi, ki, vi = q_ref[rows, :], k_ref[rows, :], v_ref[rows, :]
            sc = lax.dot_general(qi, ki, NT_DIMS, preferred_element_type=F32)
            intra = jnp.dot((sc * cs["dm"]).astype(BF16), vi, preferred_element_type=F32)
            cross = jnp.dot(_scaled(qi, cs["af"]), sf.astype(BF16), preferred_element_type=F32)
            o_ref[rows, :] = intra + cross
            return cs["gf"] * sf + lax.dot_general(_scaled(ki, cs["bf"]), vi, TN_DIMS, preferred_element_type=F32)

        def backward(j, sb):
            rows = rows_of(nc - 1 - j)
            qi, ki, vi = q_ref[rows, :], k_ref[rows, :], v_ref[rows, :]
            o_ref[rows, :] += jnp.dot(_scaled(qi, cs["ab"]), sb.astype(BF16), preferred_element_type=F32)
            return cs["gb"] * sb + lax.dot_general(_scaled(ki, cs["bb"]), vi, TN_DIMS, preferred_element_type=F32)

        zero = jnp.zeros((RET_QK_DIM, RET_V_DIM), F32)
        lax.fori_loop(0, nc, forward, zero, unroll=True)
        lax.fori_loop(0, nc, backward, zero, unroll=True)

    return pl.pallas_call(
        body, name="ret_fwd", grid=(heads,),
        in_specs=[qspec, qspec, vspec, dspec, dspec],
        out_specs=vspec, out_shape=jax.ShapeDtypeStruct(v.shape, F32),
        compiler_params=_params(("parallel",)),
    )(q, k, v, dec_f, dec_b)


def _retc_bwd(q, k, v, do, dec_f, dec_b, *, t=256):
    s, qw = q.shape
    t = _tile(s, t)
    heads, nc = qw // RET_QK_DIM, s // t
    qspec, vspec, dspec = _retc_specs(s)
    gspec = pl.BlockSpec((1, 8, LANES), lambda h: (h, 0, 0))

    def body(q_ref, k_ref, v_ref, do_ref, df_ref, db_ref, dq_ref, dk_ref, dv_ref, gf_ref, gb_ref,
             sf_scr, sb_scr, dv_acc):
        cs = _chunk_consts(df_ref, db_ref, t)
        r, diff, dm = cs["r"], cs["diff"], cs["dm"]

        def rows_of(i):
            return pl.ds(pl.multiple_of(i * t, t), t)

        def tn(a, b):
            return lax.dot_general(a, b, TN_DIMS, preferred_element_type=F32)

        def nt(a, b):
            return lax.dot_general(a, b, NT_DIMS, preferred_element_type=F32)

        def states_f(i, sf):
            sf_scr[i] = sf
            rows = rows_of(i)
            return cs["gf"] * sf + tn(_scaled(k_ref[rows, :], cs["bf"]), v_ref[rows, :])

        def states_b(j, sb):
            i = nc - 1 - j
            sb_scr[i] = sb
            rows = rows_of(i)
            return cs["gb"] * sb + tn(_scaled(k_ref[rows, :], cs["bb"]), v_ref[rows, :])

        zero = jnp.zeros((RET_QK_DIM, RET_V_DIM), F32)
        lax.fori_loop(0, nc, states_f, zero, unroll=True)
        lax.fori_loop(0, nc, states_b, zero, unroll=True)

        def scan_grads(i, state, u, qf, kf, vi, doi, fa, fb, step, wa, wb):
            qa, kb = qf * fa, kf * fb
            ub = u.astype(BF16)
            dqa = nt(doi, state.astype(BF16))
            dkb = nt(vi, ub)
            dv = jnp.dot(kb.astype(BF16), ub, preferred_element_type=F32)
            dlog = _sum_all(dqa * qa * wa) + _sum_all(dkb * kb * wb) + t * step * _sum_all(u * state)
            u_new = step * u + tn(qa.astype(BF16), doi)
            return dqa * fa, dkb * fb, dv, u_new, dlog

        def sweep_f(j, carry):
            u, accf, accb = carry
            i = nc - 1 - j
            rows = rows_of(i)
            qi, ki, vi, doi = q_ref[rows, :], k_ref[rows, :], v_ref[rows, :], do_ref[rows, :]
            sc = nt(qi, ki)
            p = sc * dm
            dp = nt(doi, vi)
            ds = (dp * dm).astype(BF16)
            tt = dp * p * diff
            accf = accf + _sum_all(jnp.where(diff > 0, tt, 0.0))
            accb = accb + _sum_all(jnp.where(diff < 0, -tt, 0.0))
            dq1, dk1, dv1, u, dlog = scan_grads(i, sf_scr[i], u, qi.astype(F32), ki.astype(F32), vi, doi,
                                                cs["af"], cs["bf"], cs["gf"], r + 1.0, t - 1.0 - r)
            dq_ref[rows, :] = jnp.dot(ds, ki, preferred_element_type=F32) + dq1
            dk_ref[rows, :] = tn(ds, qi) + dk1
            dv_acc[rows, :] = tn(p.astype(BF16), doi) + dv1
            return u, accf + dlog, accb

        def sweep_b(i, carry):
            w, accb = carry
            rows = rows_of(i)
            qi, ki, vi, doi = q_ref[rows, :], k_ref[rows, :], v_ref[rows, :], do_ref[rows, :]
            dq1, dk1, dv1, w, dlog = scan_grads(i, sb_scr[i], w, qi.astype(F32), ki.astype(F32), vi, doi,
                                                cs["ab"], cs["bb"], cs["gb"], t - r, r)
            dq_ref[rows, :] += dq1
            dk_ref[rows, :] += dk1
            dv_acc[rows, :] += dv1
            return w, accb + dlog

        z11 = jnp.zeros((1, 1), F32)
        _, accf, accb = lax.fori_loop(0, nc, sweep_f, (zero, z11, z11), unroll=4)
        _, accb = lax.fori_loop(0, nc, sweep_b, (zero, accb), unroll=4)
        dv_ref[...] = dv_acc[...].astype(dv_ref.dtype)
        gf_ref[...] = jnp.broadcast_to((accf / (1.0 + jnp.exp(df_ref[0][:, :1]))).reshape(1, 1, 1), gf_ref.shape)
        gb_ref[...] = jnp.broadcast_to((accb / (1.0 + jnp.exp(db_ref[0][:, :1]))).reshape(1, 1, 1), gb_ref.shape)

    return pl.pallas_call(
        body, name="ret_bwd", grid=(heads,),
        in_specs=[qspec, qspec, vspec, vspec, dspec, dspec],
        out_specs=[qspec, qspec, vspec, gspec, gspec],
        out_shape=[jax.ShapeDtypeStruct(q.shape, F32), jax.ShapeDtypeStruct(k.shape, F32),
                   jax.ShapeDtypeStruct(v.shape, BF16),
                   jax.ShapeDtypeStruct((heads, 8, LANES), F32), jax.ShapeDtypeStruct((heads, 8, LANES), F32)],
        scratch_shapes=[pltpu.VMEM((nc, RET_QK_DIM, RET_V_DIM), F32), pltpu.VMEM((nc, RET_QK_DIM, RET_V_DIM), F32),
                        pltpu.VMEM((s, RET_V_DIM), F32)],
        compiler_params=_params(("parallel",)),
    )(q, k, v, do, dec_f, dec_b)


def _gate_fwd(att, ag, ret, rg, rnw, *, ts=256):
    s, aw = att.shape
    rw = ret.shape[1]
    ts = _tile(s, ts)
    rheads = rw // RET_V_DIM

    def body(att_ref, ag_ref, ret_ref, rg_ref, w_ref, y_ref, yt_ref):
        def put(lo, hi, val):
            y_ref[:, lo:hi] = val.astype(BF16)
            yt_ref[lo:hi, :] = val.T.astype(BF16)

        sa, _ = _silu_parts(ag_ref[...])
        put(0, aw, sa * att_ref[...].astype(F32))
        for h in range(rheads):
            sl = slice(h * RET_V_DIM, (h + 1) * RET_V_DIM)
            rt = ret_ref[:, sl]
            rn = rt * lax.rsqrt(jnp.mean(rt * rt, axis=-1, keepdims=True) + EPS) * w_ref[:, sl]
            sr, _ = _silu_parts(rg_ref[:, sl])
            put(aw + h * RET_V_DIM, aw + (h + 1) * RET_V_DIM, sr * rn)

    def row(w):
        return pl.BlockSpec((ts, w), lambda i: (i, 0))

    return pl.pallas_call(
        body, name="gate_fwd", grid=(s // ts,),
        in_specs=[row(aw), row(aw), row(rw), row(rw), pl.BlockSpec((1, rw), lambda i: (0, 0))],
        out_specs=[row(aw + rw), pl.BlockSpec((aw + rw, ts), lambda i: (0, i))],
        out_shape=[jax.ShapeDtypeStruct((s, aw + rw), BF16), jax.ShapeDtypeStruct((aw + rw, s), BF16)],
        compiler_params=_params(("parallel",)),
    )(att, ag, ret, rg, rnw.reshape(1, rw))


def _gate_bwd(dy, att, ag, ret, rg, rnw, *, ts=256):
    s, aw = att.shape
    rw = ret.shape[1]
    ts = _tile(s, ts)
    rheads = rw // RET_V_DIM

    def body(dy_ref, att_ref, ag_ref, ret_ref, rg_ref, w_ref, datt_ref, dag_ref, dret_ref, drg_ref, dw_ref):
        sa, dsa = _silu_parts(ag_ref[...])
        dya = dy_ref[:, :aw]
        datt_ref[...] = (dya * sa).astype(BF16)
        dag_ref[...] = (dya * att_ref[...].astype(F32) * dsa).astype(BF16)
        parts = []
        for h in range(rheads):
            sl = slice(h * RET_V_DIM, (h + 1) * RET_V_DIM)
            rt = ret_ref[:, sl]
            rr = lax.rsqrt(jnp.mean(rt * rt, axis=-1, keepdims=True) + EPS)
            rh = rt * rr
            wv = w_ref[:, sl]
            sr, dsr = _silu_parts(rg_ref[:, sl])
            dyr = dy_ref[:, aw + h * RET_V_DIM:aw + (h + 1) * RET_V_DIM]
            drg_ref[:, sl] = (dyr * rh * wv * dsr).astype(BF16)
            drn = dyr * sr
            dn = drn * wv
            dret_ref[:, sl] = (rr * (dn - rh * jnp.mean(dn * rh, axis=-1, keepdims=True))).astype(BF16)
            parts.append(jnp.sum(drn * rh, axis=0, keepdims=True))
        part = jnp.concatenate(parts, axis=-1)

        @pl.when(pl.program_id(0) == 0)
        def _():
            dw_ref[...] = part

        @pl.when(pl.program_id(0) > 0)
        def _():
            dw_ref[...] += part

    def row(w):
        return pl.BlockSpec((ts, w), lambda i: (i, 0))

    vec = pl.BlockSpec((1, rw), lambda i: (0, 0))
    return pl.pallas_call(
        body, name="gate_bwd", grid=(s // ts,),
        in_specs=[row(aw + rw), row(aw), row(aw), row(rw), row(rw), vec],
        out_specs=[row(aw), row(aw), row(rw), row(rw), vec],
        out_shape=[jax.ShapeDtypeStruct((s, aw), BF16), jax.ShapeDtypeStruct((s, aw), BF16),
                   jax.ShapeDtypeStruct((s, rw), BF16), jax.ShapeDtypeStruct((s, rw), BF16),
                   jax.ShapeDtypeStruct((1, rw), F32)],
        compiler_params=_params(("arbitrary",)),
    )(dy, att, ag, ret, rg, rnw.reshape(1, rw))


def _mesh_position():
    x, y, c = lax.axis_index("x"), lax.axis_index("y"), lax.axis_index("c")
    return x, y, c, 4 * x + 2 * y + c


def _peer(x, y, c, k):
    px = 1 - x if k & 4 else x
    py = 1 - y if k & 2 else y
    pc = 1 - c if k & 1 else c
    return (px, py, pc), 4 * px + 2 * py + pc


HBM_SPEC = pl.BlockSpec(memory_space=pltpu.HBM)
SEM_SPEC = pl.BlockSpec(memory_space=pltpu.SEMAPHORE)
ANY_SPEC = pl.BlockSpec(memory_space=pl.ANY)
DATAFLOW = pltpu.SideEffectType.DATAFLOW_SIDE_EFFECTING


def _hbm(a):
    return pltpu.with_memory_space_constraint(a, pltpu.HBM)


def _split_start(name, copies, n, src, land, after):
    def body(*refs):
        (send_sems, recv_sems), token = refs[2 + len(after):4 + len(after)], refs[-1]
        sends, _ = copies(refs[0], refs[1], send_sems, recv_sems)
        for cp in sends:
            cp.start()
        token[...] = jnp.zeros_like(token)

    return pl.pallas_call(
        body, name=name,
        out_shape=(pltpu.SemaphoreType.DMA((n,)), pltpu.SemaphoreType.DMA((n,)),
                   pltpu.HBM(src.shape, src.dtype), pltpu.HBM(land.shape, land.dtype),
                   jax.ShapeDtypeStruct((8, LANES), F32)),
        in_specs=[HBM_SPEC] * 2 + [ANY_SPEC] * len(after),
        out_specs=(SEM_SPEC, SEM_SPEC, HBM_SPEC, HBM_SPEC, pl.BlockSpec(memory_space=pltpu.VMEM)),
        input_output_aliases={0: 2, 1: 3},
        compiler_params=pltpu.CompilerParams(has_side_effects=DATAFLOW),
    )(_hbm(src), _hbm(land), *after)


def _split_wait(name, copies, started, after):
    send_sems, recv_sems, src, land = started[:4]

    def body(*refs):
        sends, recvs = copies(refs[0], refs[1], refs[2], refs[3])
        for cp in sends:
            cp.wait_send()
        for cp in recvs:
            cp.wait_recv()

    return pl.pallas_call(
        body, name=name,
        out_shape=(pltpu.HBM(src.shape, src.dtype), pltpu.HBM(land.shape, land.dtype)),
        in_specs=[HBM_SPEC] * 2 + [SEM_SPEC, SEM_SPEC] + [ANY_SPEC] * len(after),
        out_specs=(HBM_SPEC,) * 2,
        input_output_aliases={0: 0, 1: 1},
        compiler_params=pltpu.CompilerParams(has_side_effects=DATAFLOW),
    )(src, land, send_sems, recv_sems, *after)


def _slab(ref, p, size, axis):
    if axis == 1:
        return ref.at[:, pl.ds(pl.multiple_of(p * size, LANES), size)]
    return ref.at[pl.ds(pl.multiple_of(p * size, 16), size), :]


ALL_PEERS = tuple(range(1, N_DEV))
SIBLING = 1
SAME_CORE_OF_CHIPS = (2, 4, 6)


def _gather_copies(size, axis, ks):
    def copies(shard_ref, full_ref, send_sems, recv_sems):
        x, y, c, me = _mesh_position()
        sends, recvs = [], []
        for j, k in enumerate(ks):
            peer, pid = _peer(x, y, c, k)
            sends.append(pltpu.make_async_remote_copy(
                src_ref=shard_ref, dst_ref=_slab(full_ref, me, size, axis), send_sem=send_sems.at[j],
                recv_sem=recv_sems.at[j], device_id=peer, device_id_type=MESH))
            recvs.append(pltpu.make_async_remote_copy(
                src_ref=shard_ref, dst_ref=_slab(full_ref, pid, size, axis), send_sem=send_sems.at[j],
                recv_sem=recv_sems.at[j], device_id=peer, device_id_type=MESH))
        return sends, recvs

    return copies


def _pass_on_copies(size, axis):
    def copies(shard_ref, full_ref, send_sems, recv_sems):
        x, y, c, _ = _mesh_position()
        sibling, _ = _peer(x, y, c, SIBLING)
        sends, recvs = [], []
        for j, k in enumerate(SAME_CORE_OF_CHIPS):
            _, landed = _peer(x, y, c, k)
            _, siblings = _peer(x, y, c, k ^ SIBLING)
            mine = _slab(full_ref, landed, size, axis)
            sends.append(pltpu.make_async_remote_copy(
                src_ref=mine, dst_ref=mine, send_sem=send_sems.at[j], recv_sem=recv_sems.at[j],
                device_id=sibling, device_id_type=MESH))
            recvs.append(pltpu.make_async_remote_copy(
                src_ref=mine, dst_ref=_slab(full_ref, siblings, size, axis), send_sem=send_sems.at[j],
                recv_sem=recv_sems.at[j], device_id=sibling, device_id_type=MESH))
        return sends, recvs

    return copies


def _scatter_copies(size, axis):
    def copies(grad_ref, land_ref, send_sems, recv_sems):
        x, y, c, me = _mesh_position()
        sends, recvs = [], []
        for k in range(1, N_DEV):
            peer, pid = _peer(x, y, c, k)
            src = _slab(grad_ref, pid, size, axis)
            sends.append(pltpu.make_async_remote_copy(
                src_ref=src, dst_ref=land_ref.at[me], send_sem=send_sems.at[k - 1], recv_sem=recv_sems.at[k - 1],
                device_id=peer, device_id_type=MESH))
            recvs.append(pltpu.make_async_remote_copy(
                src_ref=src, dst_ref=land_ref.at[pid], send_sem=send_sems.at[k - 1], recv_sem=recv_sems.at[k - 1],
                device_id=peer, device_id_type=MESH))
        return sends, recvs

    return copies


PLACE_BANDS = 8


def _place_own(name, src, out_shape, in_spec, out_spec, steps, me):
    def body(me_ref, src_ref, out_ref):
        out_ref[...] = src_ref[...]

    return pl.pallas_call(
        body, name=name, out_shape=out_shape,
        grid_spec=pltpu.PrefetchScalarGridSpec(num_scalar_prefetch=1, grid=(steps,), in_specs=[in_spec],
                                               out_specs=out_spec),
        compiler_params=_params(("parallel",)),
    )(me.reshape(1).astype(jnp.int32), src)


def _cast_place(w_all, layer, axis, me):
    _, rows, cols = w_all.shape
    full_shape = tuple(N_DEV * n if a == axis else n for a, n in enumerate((rows, cols)))
    band = rows // PLACE_BANDS
    if axis == 1:
        full_spec = pl.BlockSpec((band, cols), lambda i, me_ref: (i, me_ref[0]))
    else:
        full_spec = pl.BlockSpec((band, cols), lambda i, me_ref: (me_ref[0] * PLACE_BANDS + i, 0))

    def body(me_ref, w_ref, shard_ref, full_ref):
        shard_ref[...] = w_ref[...].astype(BF16)
        full_ref[...] = w_ref[...].astype(BF16)

    return pl.pallas_call(
        body, name="cast_place",
        out_shape=[jax.ShapeDtypeStruct((rows, cols), BF16), jax.ShapeDtypeStruct(full_shape, BF16)],
        grid_spec=pltpu.PrefetchScalarGridSpec(
            num_scalar_prefetch=1, grid=(PLACE_BANDS,),
            in_specs=[pl.BlockSpec((None, band, cols), lambda i, me_ref: (layer, i, 0))],
            out_specs=[pl.BlockSpec((band, cols), lambda i, me_ref: (i, 0)), full_spec]),
        compiler_params=_params(("parallel",)),
    )(me.reshape(1).astype(jnp.int32), w_all)


def _gather_start(placed, axis, ks, after, tag):
    shard, full = placed
    return _split_start("gather_start_" + tag, _gather_copies(shard.shape[axis], axis, ks), len(ks), shard, full, after)


def _gather_wait(started, axis, ks, after, tag):
    size = started[2].shape[axis]
    return _split_wait("gather_wait_" + tag, _gather_copies(size, axis, ks), started, after)


def _pass_on_start(shard, full, axis, after, tag):
    size = shard.shape[axis]
    return _split_start("pass_on_start_" + tag, _pass_on_copies(size, axis), len(SAME_CORE_OF_CHIPS), shard, full, after)


def _pass_on_wait(started, axis, after, tag):
    size = started[2].shape[axis]
    return _split_wait("pass_on_wait_" + tag, _pass_on_copies(size, axis), started, after)[1]


def _scatter_start(grad, axis, me, tag):
    size = grad.shape[axis] // N_DEV
    rows, cols = tuple(size if a == axis else n for a, n in enumerate(grad.shape))
    band = rows // PLACE_BANDS
    if axis == 1:
        in_spec = pl.BlockSpec((band, cols), lambda i, me_ref: (i, me_ref[0]))
    else:
        in_spec = pl.BlockSpec((band, cols), lambda i, me_ref: (me_ref[0] * PLACE_BANDS + i, 0))
    out_spec = pl.BlockSpec((None, band, cols), lambda i, me_ref: (me_ref[0], i, 0))
    land = _place_own("place_slab", grad, jax.ShapeDtypeStruct((N_DEV, rows, cols), grad.dtype), in_spec, out_spec,
                      PLACE_BANDS, me)
    return _split_start("scatter_start_" + tag, _scatter_copies(size, axis), N_DEV - 1, grad, land, [])


def _scatter_wait(started, axis, after, tag):
    size = started[2].shape[axis] // N_DEV
    land = _split_wait("scatter_wait_" + tag, _scatter_copies(size, axis), started, after)[1]
    return [(land, p) for p in range(N_DEV)]


N_CHIPS = N_DEV // 2


def _pair_copies(size, axis):
    def copies(grad_ref, land_ref, send_sems, recv_sems):
        x, y, c, _ = _mesh_position()
        sibling, _ = _peer(x, y, c, SIBLING)
        sends, recvs = [], []
        for j in range(N_CHIPS):
            _, owner = _peer(x, y, c, (2 * j) ^ SIBLING)
            for lst in (sends, recvs):
                lst.append(pltpu.make_async_remote_copy(
                    src_ref=_slab(grad_ref, owner, size, axis), dst_ref=land_ref.at[j], send_sem=send_sems.at[j],
                    recv_sem=recv_sems.at[j], device_id=sibling, device_id_type=MESH))
        return sends, recvs

    return copies


def _chips_copies():
    def copies(pair_ref, land_ref, send_sems, recv_sems):
        x, y, c, _ = _mesh_position()
        sends, recvs = [], []
        for j in range(1, N_CHIPS):
            owner, _ = _peer(x, y, c, 2 * j)
            for lst in (sends, recvs):
                lst.append(pltpu.make_async_remote_copy(
                    src_ref=pair_ref.at[j], dst_ref=land_ref.at[j], send_sem=send_sems.at[j - 1],
                    recv_sem=recv_sems.at[j - 1], device_id=owner, device_id_type=MESH))
        return sends, recvs

    return copies


def _pair_start(grad, axis, tag):
    size = grad.shape[axis] // N_DEV
    rows, cols = tuple(size if a == axis else n for a, n in enumerate(grad.shape))
    land = lax.empty((N_CHIPS, rows, cols), grad.dtype)
    return _split_start("pair_start_" + tag, _pair_copies(size, axis), N_CHIPS, grad, land, [])


def _pair_sums(started, axis, me, after, tag):
    size = started[2].shape[axis] // N_DEV
    grad, land = _split_wait("pair_wait_" + tag, _pair_copies(size, axis), started, after)
    _, rows, cols = land.shape
    band = rows // PLACE_BANDS
    if axis == 1:
        mine = pl.BlockSpec((band, cols), lambda j, i, me_ref: (i, me_ref[0] ^ (2 * j)))
    else:
        mine = pl.BlockSpec((band, cols), lambda j, i, me_ref: ((me_ref[0] ^ (2 * j)) * PLACE_BANDS + i, 0))
    slot = pl.BlockSpec((None, band, cols), lambda j, i, me_ref: (j, i, 0))

    def body(own_ref, mine_ref, theirs_ref, out_ref):
        out_ref[...] = (mine_ref[...].astype(F32) + theirs_ref[...].astype(F32)).astype(out_ref.dtype)

    return pl.pallas_call(
        body, name="pair_sums", out_shape=jax.ShapeDtypeStruct(land.shape, land.dtype),
        grid_spec=pltpu.PrefetchScalarGridSpec(num_scalar_prefetch=1, grid=(N_CHIPS, PLACE_BANDS),
                                               in_specs=[mine, slot], out_specs=slot),
        compiler_params=_params(("parallel", "parallel")),
    )(me.reshape(1).astype(jnp.int32), grad, land)


def _chips_start(pairs, tag):
    return _split_start("chips_start_" + tag, _chips_copies(), N_CHIPS - 1, pairs, lax.empty(pairs.shape, pairs.dtype), [])


def _chips_wait(started, after, tag):
    pairs, land = _split_wait("chips_wait_" + tag, _chips_copies(), started, after)
    return [(pairs, 0)] + [(land, j) for j in range(1, N_CHIPS)]


def _exchange_small(buf, *, name, after=()):
    r = buf.shape[0]

    def body(*refs):
        buf_ref = refs[0]
        all_ref, sum_ref, send_sems, recv_sems = refs[1 + len(after):]
        x, y, c, me = _mesh_position()
        all_ref[me] = buf_ref[...]
        sends, recvs = [], []
        for k in range(1, N_DEV):
            peer, pid = _peer(x, y, c, k)
            sends.append(pltpu.make_async_remote_copy(
                src_ref=buf_ref, dst_ref=all_ref.at[me], send_sem=send_sems.at[k - 1], recv_sem=recv_sems.at[k - 1],
                device_id=peer, device_id_type=MESH))
            recvs.append(pltpu.make_async_remote_copy(
                src_ref=buf_ref, dst_ref=all_ref.at[pid], send_sem=send_sems.at[k - 1], recv_sem=recv_sems.at[k - 1],
                device_id=peer, device_id_type=MESH))
        for cp in sends:
            cp.start()
        for cp in recvs:
            cp.wait_recv()
        for cp in sends:
            cp.wait_send()
        total = all_ref[0]
        for p in range(1, N_DEV):
            total = total + all_ref[p]
        sum_ref[...] = total

    vmem = pl.BlockSpec(memory_space=pltpu.VMEM)
    return pl.pallas_call(
        body, name=name,
        in_specs=[vmem] + [ANY_SPEC] * len(after), out_specs=[vmem, vmem],
        out_shape=[jax.ShapeDtypeStruct((N_DEV, r, LANES), F32), jax.ShapeDtypeStruct((r, LANES), F32)],
        scratch_shapes=[pltpu.SemaphoreType.DMA((N_DEV - 1,)), pltpu.SemaphoreType.DMA((N_DEV - 1,))],
        compiler_params=pltpu.CompilerParams(has_side_effects=True),
    )(buf, *after)


def _adamw_math(w, g, m, v):
    m2 = ADAM_B1 * m + (1.0 - ADAM_B1) * g
    v2 = ADAM_B2 * v + (1.0 - ADAM_B2) * (g * g)
    delta = -ADAM_LR * ((m2 / ADAM_C1) / (jnp.sqrt(v2 / ADAM_C2) + ADAM_EPS) + ADAM_WD * w)
    return delta, m2, v2


def _adamw_slabs(layer, w, m, v, addends, outs, order, *, tr, name):
    depth, r, c = w.shape
    tr = _tile(r, tr)
    n = len(addends)

    def body(*refs):
        w_ref, m_ref, v_ref = refs[:3]
        g_ref, d_ref, m2_ref, v2_ref = refs[-4:]
        g = refs[3][...].astype(F32)
        for a_ref in refs[4:3 + n]:
            g = g + a_ref[...].astype(F32)
        delta, m2, v2 = _adamw_math(w_ref[...], g, m_ref[...], v_ref[...])
        g_ref[...] = g
        d_ref[...] = delta
        m2_ref[...] = m2
        v2_ref[...] = v2

    row = pl.BlockSpec((None, tr, c), lambda i: (layer, i, 0))
    slots = [pl.BlockSpec((None, tr, c), lambda i, p=p: (p, i, 0)) for _, p in addends]
    first_out = 3 + n + 1
    return pl.pallas_call(
        body, name=name, grid=(r // tr,),
        in_specs=[row, row, row] + slots + [pl.BlockSpec((8, LANES), lambda i: (0, 0))] + [ANY_SPEC] * 4,
        out_specs=[row] * 4, out_shape=[jax.ShapeDtypeStruct((depth, r, c), F32)] * 4,
        input_output_aliases={first_out + t: t for t in range(4)},
        compiler_params=_params(("parallel",)),
    )(w, m, v, *[a for a, _ in addends], order, *outs)


def _adamw_small(w, g, m, v):
    def body(w_ref, g_ref, m_ref, v_ref, d_ref, m2_ref, v2_ref):
        delta, m2, v2 = _adamw_math(w_ref[...], g_ref[...], m_ref[...], v_ref[...])
        d_ref[...] = delta
        m2_ref[...] = m2
        v2_ref[...] = v2

    vmem = pl.BlockSpec(memory_space=pltpu.VMEM)
    return pl.pallas_call(
        body, name="adamw_small", in_specs=[vmem] * 4, out_specs=[vmem] * 3,
        out_shape=[jax.ShapeDtypeStruct(w.shape, F32)] * 3,
    )(w, g, m, v)


def _pack(parts):
    flat = jnp.concatenate([p.reshape(-1).astype(F32) for p in parts])
    rows = -(-flat.shape[0] // LANES)
    rows = -(-rows // SMALL_ROWS_ALIGN) * SMALL_ROWS_ALIGN
    flat = jnp.pad(flat, (0, rows * LANES - flat.shape[0]))
    return flat.reshape(rows, LANES)


def _unpack(buf, shapes):
    flat = buf.reshape(-1)
    out, pos = [], 0
    for shp in shapes:
        size = math.prod(shp)
        out.append(flat[pos:pos + size].reshape(shp))
        pos += size
    return out


def _section_widths(d):
    aw = d // 2
    kw = aw // ATTN_GROUP
    rw = d - aw
    rqw = (rw // RET_V_DIM) * RET_QK_DIM
    return (aw, kw, kw, aw, rqw, rqw, rw, rw)


def _layer_fwd(xl, hh, win_full, behind, after_attn, wout_of, qn, kn, dec_f, dec_b, rn, cos, sin):
    h, ht = hh
    aq, ak, v, ag, rq, rk, rvb, rg = _proj_sections(h, win_full, _section_widths(xl.shape[1]),
                                                    (F32, F32, BF16, F32, F32, F32, BF16, F32), after=behind)
    q, k, rqr, rkr = _prep_fwd(aq, ak, rq, rk, cos, sin, qn, kn)
    att, lse = _attn_fwd(q, k, v)
    ret = _retc_fwd(rqr, rkr, rvb, dec_f + after_attn(att), dec_b)
    y, yt = _gate_fwd(att, ag, ret, rg, rn)
    wout_full = wout_of(y)
    xn = _matmul(y, wout_full, name="out_proj", residual=xl)
    saved = dict(x=xl, ht=ht, aq=aq, ak=ak, ag=ag, rg=rg, q=q, k=k, v=v, rq=rqr, rk=rkr, rv=rvb,
                 att=att, lse=lse, ret=ret, yt=yt, win=win_full, wout=wout_full)
    return xn, saved


def _layer_bwd_weights(gb, sv, qn, kn, dec_f, dec_b, rn, cos, sin, on_dwout):
    dy = _matmul(gb, sv["wout"], name="d_y", trans_b=True)
    dwout = _matmul(sv["yt"], gb, name="d_wout", out_dtype=BF16)
    datt, dag, dret, drg, drn = _gate_bwd(dy, sv["att"], sv["ag"], sv["ret"], sv["rg"], rn + on_dwout(dwout))
    dq, dk, dav = _attn_bwd(sv["q"], sv["k"], sv["v"], sv["att"], datt, sv["lse"])
    drq, drk, drv, gf, gbk = _retc_bwd(sv["rq"], sv["rk"], sv["rv"], dret, dec_f, dec_b)
    dproj, dqn, dkn = _prep_bwd(dq, dk, drq, drk, sv["aq"], sv["ak"], cos, sin, qn, kn, dav, dag, drv, drg)
    dwin = _matmul(sv["ht"], dproj, name="d_win", out_dtype=BF16)
    small = dict(qn=dqn[0], kn=dkn[0], df=gf[:, 0, 0], db=gbk[:, 0, 0], rn=drn[0])
    return dproj, dwin, small


def _layer_bwd_input(g, dproj, sv, nw, behind, after_dh):
    dh = _matmul(dproj, sv["win"], name="d_h", trans_b=True, tm=512, tk=dproj.shape[1], after=behind)
    g, gb, dnw = _rms_bwd(dh, sv["x"], g, nw + after_dh(dh))
    return g, gb, dnw[0]


def kernel(x, norm_w, w_in, q_norm, k_norm, ret_decay_fwd, ret_decay_bwd, ret_norm, w_out, final_norm, loss_target, m_norm_w, m_w_in, m_q_norm, m_k_norm, m_ret_decay_fwd, m_ret_decay_bwd, m_ret_norm, m_w_out, m_final_norm, v_norm_w, v_w_in, v_q_norm, v_k_norm, v_ret_decay_fwd, v_ret_decay_bwd, v_ret_norm, v_w_out, v_final_norm):
    depth, d, _ = w_in.shape
    seq = x.shape[1]
    rw = _section_widths(d)[6]
    rheads = rw // RET_V_DIM
    rns = ret_norm.shape[-1]
    _, _, _, me = _mesh_position()

    target = loss_target[0]
    cos, sin = _rope_tables(seq)

    rn_all, _ = _exchange_small(_pack([ret_norm]), name="gather_ret_norm")
    rn_full = rn_all.reshape(N_DEV, -1)[:, :depth * rheads * rns].reshape(N_DEV, depth, rheads, rns)
    rn_full = jnp.transpose(rn_full, (1, 2, 0, 3)).reshape(depth, rw)

    dec_f = jnp.broadcast_to(ret_decay_fwd[:, :, None, None], (depth, rheads, 1, LANES))
    dec_b = jnp.broadcast_to(ret_decay_bwd[:, :, None, None], (depth, rheads, 1, LANES))


    saved = []
    xl = x[0]
    first = (SIBLING,) + SAME_CORE_OF_CHIPS
    in_sent = _gather_start(_cast_place(w_in, 0, 1, me), 1, first, [], "in0")
    hh = _rms_fwd(xl, norm_w[0] + in_sent[-1][0, 0])
    placed_out = _cast_place(w_out, 0, 0, me)
    placed_in = _cast_place(w_in, 1, 1, me) if depth > 1 else None
    under_way = [hh[0], placed_out[0], cos, sin] + ([placed_in[0]] if depth > 1 else [])
    shard, landed = _gather_wait(in_sent, 1, first, under_way, "in0")
    win_full = _pass_on_wait(_pass_on_start(shard, landed, 1, [], "in0"), 1, [], "in0")
    for l in range(depth):
        if l > 0:
            hh = _rms_fwd(xl, norm_w[l])
            placed_out = _cast_place(w_out, l, 0, me)
            placed_in = _cast_place(w_in, l + 1, 1, me) if l + 1 < depth else None
        out_sent = _gather_start(placed_out, 0, ALL_PEERS, [win_full], "out" + str(l))
        behind = [out_sent[-1]]
        passed = {}
        if l + 1 < depth:
            in_sent = _gather_start(placed_in, 1, first, [win_full, out_sent[-1]], "in" + str(l + 1))
            behind.append(in_sent[-1])

        def after_attn(att, passed=passed, l=l):
            if l + 1 == depth:
                return 0.0
            shard, landed = _gather_wait(in_sent, 1, first, [att], "in" + str(l + 1))
            passed["on"] = _pass_on_start(shard, landed, 1, [], "in" + str(l + 1))
            return passed["on"][-1][0, 0]

        def wout_of(y, out_sent=out_sent, l=l):
            return _gather_wait(out_sent, 0, ALL_PEERS, [y], "out" + str(l))[1]

        xl, sv = _layer_fwd(xl, hh, win_full, behind, after_attn, wout_of, q_norm[l], k_norm[l], dec_f[l], dec_b[l],
                            rn_full[l], cos, sin)
        saved.append(sv)
        if l + 1 < depth:
            win_full = _pass_on_wait(passed["on"], 1, [xl], "in" + str(l + 1))

    loss_row, g, gb, d_final = _loss_head(xl, target, final_norm)

    d_norm, d_qn, d_kn, d_df, d_db, d_rn = [], [], [], [], [], []
    lands = [None] * depth
    pending = None
    for l in reversed(range(depth)):
        sent = {}

        def on_dwout(dwout, sent=sent, l=l):
            sent["out"] = _scatter_start(dwout, 0, me, "out" + str(l))
            return sent["out"][-1][0, 0]

        def after_dh(dh, sent=sent, l=l):
            pairs = _pair_sums(sent["pair"], 1, me, [dh], "in" + str(l))
            sent["in"] = _chips_start(pairs, "in" + str(l))
            return sent["in"][-1][0, 0]

        dproj, dwin, sm = _layer_bwd_weights(gb, saved[l], q_norm[l], k_norm[l], dec_f[l], dec_b[l],
                                             rn_full[l], cos, sin, on_dwout)
        sent["pair"] = _pair_start(dwin, 1, "in" + str(l))
        g, gb, dnw = _layer_bwd_input(g, dproj, saved[l], norm_w[l], [sent["pair"][-1]], after_dh)
        if pending is not None:
            lands[l + 1] = (_chips_wait(pending["in"], [g], "in" + str(l + 1)),
                            _scatter_wait(pending["out"], 0, [g], "out" + str(l + 1)))
        pending = sent
        d_norm.append(dnw)
        d_qn.append(sm["qn"])
        d_kn.append(sm["kn"])
        d_df.append(sm["df"])
        d_db.append(sm["db"])
        d_rn.append(sm["rn"])
    for lst in (d_norm, d_qn, d_kn, d_df, d_db, d_rn):
        lst.reverse()
    order = pending["in"][-1]
    in_outs = [lax.empty(w_in.shape, F32) for _ in range(4)]
    out_outs = [lax.empty(w_out.shape, F32) for _ in range(4)]
    for l in reversed(range(1, depth)):
        in_outs = _adamw_slabs(l, w_in, m_w_in, v_w_in, lands[l][0], in_outs, order, tr=256, name="adamw_w_in")
        out_outs = _adamw_slabs(l, w_out, m_w_out, v_w_out, lands[l][1], out_outs, order, tr=64, name="adamw_w_out")
    land_out = _scatter_wait(pending["out"], 0, [g, out_outs[0]], "out0")
    out_outs = _adamw_slabs(0, w_out, m_w_out, v_w_out, land_out, out_outs, order, tr=64, name="adamw_w_out")

    small_shapes = [(depth, d), (depth, HEAD_DIM), (depth, HEAD_DIM), (depth, rheads), (depth, rheads),
                    (depth, rheads, N_DEV * rns), (d,), (1,)]
    grads_local = [jnp.stack(d_norm), jnp.stack(d_qn), jnp.stack(d_kn), jnp.stack(d_df), jnp.stack(d_db),
                   jnp.stack(d_rn).reshape(depth, rheads, N_DEV * rns), d_final[0], loss_row[0, :1]]
    _, gsum = _exchange_small(_pack(grads_local), name="all_reduce_small", after=(in_outs[0], out_outs[0]))
    land_in = _chips_wait(pending["in"], [g, gsum], "in0")
    in_outs = _adamw_slabs(0, w_in, m_w_in, v_w_in, land_in, in_outs, order, tr=256, name="adamw_w_in")
    g_norm, g_qn, g_kn, g_df, g_db, g_rn_full, g_final, loss = _unpack(gsum, small_shapes)
    g_rn = lax.dynamic_slice_in_dim(g_rn_full, me * rns, rns, axis=2)
    small_g = [g_norm, g_qn, g_kn, g_df, g_db, g_rn, g_final]
    small_w = [norm_w, q_norm, k_norm, ret_decay_fwd, ret_decay_bwd, ret_norm, final_norm]
    small_m = [m_norm_w, m_q_norm, m_k_norm, m_ret_decay_fwd, m_ret_decay_bwd, m_ret_norm, m_final_norm]
    small_v = [v_norm_w, v_q_norm, v_k_norm, v_ret_decay_fwd, v_ret_decay_bwd, v_ret_norm, v_final_norm]
    shapes = [a.shape for a in small_w]
    sd, sm, sv2 = _adamw_small(_pack(small_w), _pack(small_g), _pack(small_m), _pack(small_v))
    small_d, small_m2, small_v2 = _unpack(sd, shapes), _unpack(sm, shapes), _unpack(sv2, shapes)

    def ordered(small, win_v, wout_v):
        return [small[0], win_v, small[1], small[2], small[3], small[4], small[5], wout_v, small[6]]

    grads = ordered(small_g, in_outs[0], out_outs[0])
    deltas = ordered(small_d, in_outs[1], out_outs[1])
    new_m = ordered(small_m2, in_outs[2], out_outs[2])
    new_v = ordered(small_v2, in_outs[3], out_outs[3])
    return (loss.reshape(()), g[None], *grads, *deltas, *new_m, *new_v)
```

```python
import math

import jax
import jax.numpy as jnp
import numpy as np
from jax import lax
from jax.experimental import pallas as pl
from jax.experimental.pallas import tpu as pltpu

F32 = jnp.float32
BF16 = jnp.bfloat16

N_DEV = 8
HEAD_DIM = 128
ATTN_GROUP = 4
RET_QK_DIM = 128
RET_V_DIM = 256
GRID_W = 64
ROPE_THETA = 10000.0
EPS = 1e-6
ADAM_LR = 0.001
ADAM_B1 = 0.9
ADAM_B2 = 0.999
ADAM_EPS = 1e-08
ADAM_WD = 0.01
ADAM_STEP = 10
ADAM_C1 = 1.0 - ADAM_B1 ** ADAM_STEP
ADAM_C2 = 1.0 - ADAM_B2 ** ADAM_STEP
LANES = 128
SMALL_ROWS_ALIGN = 8
VMEM_LIMIT = 56 * 1024 * 1024

NT_DIMS = (((1,), (1,)), ((), ()))
TN_DIMS = (((0,), (0,)), ((), ()))
MESH = pl.DeviceIdType.MESH


def _params(sem):
    return pltpu.CompilerParams(dimension_semantics=sem, vmem_limit_bytes=VMEM_LIMIT)


def _tile(dim, pref, align=16):
    if dim <= pref:
        return dim
    for t in range(pref - pref % align, 0, -align):
        if dim % t == 0:
            return t
    raise ValueError((dim, pref, align))


def _silu_parts(z):
    sg = 1.0 / (1.0 + jnp.exp(-z))
    return z * sg, sg * (1.0 + z * (1.0 - sg))


def _log_sigmoid(x):
    return jnp.minimum(x, 0.0) - jnp.log(1.0 + jnp.exp(-jnp.abs(x)))


def _swap_pairs(z):
    src = lax.broadcasted_iota(jnp.int32, (HEAD_DIM, HEAD_DIM), 0)
    dst = lax.broadcasted_iota(jnp.int32, (HEAD_DIM, HEAD_DIM), 1)
    partner = jnp.where((dst % 64) < 32, dst + 32, dst - 32)
    perm = (src == partner).astype(F32)
    return jnp.dot(z, perm, precision=lax.Precision.HIGH, preferred_element_type=F32)


def _rope(z, cos, sin):
    return z * cos + _swap_pairs(z) * sin


def _rope_transposed(d, cos, sin):
    return d * cos + _swap_pairs(d * sin)


def _rope_tables(seq):
    rows = seq // GRID_W
    row = jnp.repeat(jnp.arange(rows), GRID_W).astype(F32)
    col = jnp.tile(jnp.arange(GRID_W), rows).astype(F32)
    axis_dim = HEAD_DIM // 2
    inv = ROPE_THETA ** (-jnp.arange(0, axis_dim, 2, dtype=F32) / axis_dim)
    ar = row[:, None] * inv[None, :]
    ac = col[:, None] * inv[None, :]
    cos = jnp.concatenate([jnp.cos(ar), jnp.cos(ar), jnp.cos(ac), jnp.cos(ac)], axis=-1)
    sin = jnp.concatenate([-jnp.sin(ar), jnp.sin(ar), -jnp.sin(ac), jnp.sin(ac)], axis=-1)
    return cos, sin


def _matmul(a, b, *, name, trans_b=False, out_dtype=F32, residual=None, tm=1024, tn=512, tk=4096, after=()):
    m, k = a.shape
    n = b.shape[0] if trans_b else b.shape[1]
    tm, tn, tk = _tile(m, tm), _tile(n, tn, LANES), _tile(k, tk, LANES)
    nk = k // tk
    has_res = residual is not None

    def body(*refs):
        a_ref, b_ref = refs[:2]
        r_ref = refs[2] if has_res else None
        o_ref = refs[2 + has_res + len(after)]
        if trans_b:
            part = lax.dot_general(a_ref[...], b_ref[...], NT_DIMS, preferred_element_type=F32)
        else:
            part = jnp.dot(a_ref[...], b_ref[...], preferred_element_type=F32)

        def finish(r):
            if has_res:
                r = r + r_ref[...]
            o_ref[...] = r.astype(o_ref.dtype)

        if nk == 1:
            finish(part)
        else:
            acc_ref = refs[-1]
            kk = pl.program_id(2)

            @pl.when(kk == 0)
            def _():
                acc_ref[...] = part

            @pl.when(kk > 0)
            def _():
                acc_ref[...] += part

            @pl.when(kk == nk - 1)
            def _():
                finish(acc_ref[...])

    if trans_b:
        b_spec = pl.BlockSpec((tn, tk), lambda i, j, kk: (j, kk))
    else:
        b_spec = pl.BlockSpec((tk, tn), lambda i, j, kk: (kk, j))
    in_specs = [pl.BlockSpec((tm, tk), lambda i, j, kk: (i, kk)), b_spec]
    args = [a, b]
    if has_res:
        in_specs.append(pl.BlockSpec((tm, tn), lambda i, j, kk: (i, j)))
        args.append(residual)
    in_specs += [ANY_SPEC] * len(after)
    args += list(after)
    return pl.pallas_call(
        body, name=name, grid=(m // tm, n // tn, nk),
        in_specs=in_specs,
        out_specs=pl.BlockSpec((tm, tn), lambda i, j, kk: (i, j)),
        out_shape=jax.ShapeDtypeStruct((m, n), out_dtype),
        scratch_shapes=[pltpu.VMEM((tm, tn), F32)] if nk > 1 else [],
        compiler_params=_params(("parallel", "parallel", "arbitrary")),
    )(*args)


def _proj_sections(a, b, widths, dtypes, *, tm=1024, tn=512, after=()):
    m, k = a.shape
    tm = _tile(m, tm)
    tn = _tile(min(widths), tn, LANES)
    assert all(w % tn == 0 for w in widths) and sum(widths) == b.shape[1]
    nblk = [w // tn for w in widths]
    first = [int(o) // tn for o in np.cumsum((0,) + tuple(widths))[:-1]]

    def body(a_ref, b_ref, *refs):
        j = pl.program_id(1)
        part = jnp.dot(a_ref[...], b_ref[...], preferred_element_type=F32)
        for o_ref, lo, n in zip(refs[len(after):], first, nblk):
            @pl.when(jnp.logical_and(j >= lo, j < lo + n))
            def _(o_ref=o_ref):
                o_ref[...] = part.astype(o_ref.dtype)

    out_specs = [pl.BlockSpec((tm, tn), lambda i, j, lo=lo, n=n: (i, jnp.clip(j - lo, 0, n - 1)))
                 for lo, n in zip(first, nblk)]
    return pl.pallas_call(
        body, name="proj", grid=(m // tm, b.shape[1] // tn),
        in_specs=[pl.BlockSpec((tm, k), lambda i, j: (i, 0)), pl.BlockSpec((k, tn), lambda i, j: (0, j))]
        + [ANY_SPEC] * len(after),
        out_specs=out_specs,
        out_shape=[jax.ShapeDtypeStruct((m, w), dt) for w, dt in zip(widths, dtypes)],
        compiler_params=_params(("arbitrary", "arbitrary")),
    )(a, b, *after)


def _rms_fwd(x, w, *, ts=256):
    s, d = x.shape
    ts = _tile(s, ts)

    def body(x_ref, w_ref, h_ref, ht_ref):
        xv = x_ref[...]
        r = lax.rsqrt(jnp.mean(xv * xv, axis=-1, keepdims=True) + EPS)
        h = xv * r * w_ref[...]
        h_ref[...] = h.astype(BF16)
        ht_ref[...] = h.T.astype(BF16)

    row = pl.BlockSpec((ts, d), lambda i: (i, 0))
    return pl.pallas_call(
        body, name="rms_fwd", grid=(s // ts,),
        in_specs=[row, pl.BlockSpec((1, d), lambda i: (0, 0))],
        out_specs=[row, pl.BlockSpec((d, ts), lambda i: (0, i))],
        out_shape=[jax.ShapeDtypeStruct((s, d), BF16), jax.ShapeDtypeStruct((d, s), BF16)],
        compiler_params=_params(("parallel",)),
    )(x, w.reshape(1, d))


def _rms_bwd(dh, x, g, w, *, ts=256):
    s, d = x.shape
    ts = _tile(s, ts)

    def body(dh_ref, x_ref, g_ref, w_ref, dx_ref, dxb_ref, dw_ref):
        xv = x_ref[...]
        r = lax.rsqrt(jnp.mean(xv * xv, axis=-1, keepdims=True) + EPS)
        xh = xv * r
        dhv = dh_ref[...]
        dn = dhv * w_ref[...]
        dx = g_ref[...] + r * (dn - xh * jnp.mean(dn * xh, axis=-1, keepdims=True))
        dx_ref[...] = dx
        dxb_ref[...] = dx.astype(BF16)
        part = jnp.sum(dhv * xh, axis=0, keepdims=True)

        @pl.when(pl.program_id(0) == 0)
        def _():
            dw_ref[...] = part

        @pl.when(pl.program_id(0) > 0)
        def _():
            dw_ref[...] += part

    row = pl.BlockSpec((ts, d), lambda i: (i, 0))
    vec = pl.BlockSpec((1, d), lambda i: (0, 0))
    return pl.pallas_call(
        body, name="rms_bwd", grid=(s // ts,),
        in_specs=[row, row, row, vec],
        out_specs=[row, row, vec],
        out_shape=[jax.ShapeDtypeStruct((s, d), F32), jax.ShapeDtypeStruct((s, d), BF16),
                   jax.ShapeDtypeStruct((1, d), F32)],
        compiler_params=_params(("arbitrary",)),
    )(dh, x, g, w.reshape(1, d))


def _loss_head(x, target, w, *, ts=256):
    s, d = x.shape
    ts = _tile(s, ts)

    def body(x_ref, t_ref, w_ref, loss_ref, dx_ref, dxb_ref, dw_ref):
        xv = x_ref[...]
        r = lax.rsqrt(jnp.mean(xv * xv, axis=-1, keepdims=True) + EPS)
        xh = xv * r
        wv = w_ref[...]
        diff = xh * wv - t_ref[...]
        lpart = 0.5 * jnp.sum(jnp.mean(diff * diff, axis=-1, keepdims=True), axis=0, keepdims=True)
        dout = diff * (1.0 / d)
        dn = dout * wv
        dx = r * (dn - xh * jnp.mean(dn * xh, axis=-1, keepdims=True))
        dx_ref[...] = dx
        dxb_ref[...] = dx.astype(BF16)
        part = jnp.sum(dout * xh, axis=0, keepdims=True)
        lrow = jnp.broadcast_to(lpart, loss_ref.shape)

        @pl.when(pl.program_id(0) == 0)
        def _():
            dw_ref[...] = part
            loss_ref[...] = lrow

        @pl.when(pl.program_id(0) > 0)
        def _():
            dw_ref[...] += part
            loss_ref[...] += lrow

    row = pl.BlockSpec((ts, d), lambda i: (i, 0))
    vec = pl.BlockSpec((1, d), lambda i: (0, 0))
    return pl.pallas_call(
        body, name="loss_head", grid=(s // ts,),
        in_specs=[row, row, vec],
        out_specs=[pl.BlockSpec((1, LANES), lambda i: (0, 0)), row, row, vec],
        out_shape=[jax.ShapeDtypeStruct((1, LANES), F32), jax.ShapeDtypeStruct((s, d), F32),
                   jax.ShapeDtypeStruct((s, d), BF16), jax.ShapeDtypeStruct((1, d), F32)],
        compiler_params=_params(("arbitrary",)),
    )(x, target, w.reshape(1, d))


def _prep_fwd(aq, ak, rq, rk, cos, sin, qw, kw, *, ts=256):
    s = aq.shape[0]
    ts = _tile(s, ts)
    attn_scale = HEAD_DIM ** -0.5
    ret_scale = RET_QK_DIM ** -0.5
    nq, nk, nr = aq.shape[1] // HEAD_DIM, ak.shape[1] // HEAD_DIM, rq.shape[1] // RET_QK_DIM

    def body(aq_ref, ak_ref, rq_ref, rk_ref, cos_ref, sin_ref, qw_ref, kw_ref,
             q_out, k_out, rq_out, rk_out):
        c, sn = cos_ref[...], sin_ref[...]

        def normed(u, w):
            return u * lax.rsqrt(jnp.mean(u * u, axis=-1, keepdims=True) + EPS) * w

        for j in range(nq):
            sl = slice(j * HEAD_DIM, (j + 1) * HEAD_DIM)
            q_out[:, sl] = (_rope(normed(aq_ref[:, sl], qw_ref[...]), c, sn) * attn_scale).astype(BF16)
        for j in range(nk):
            sl = slice(j * HEAD_DIM, (j + 1) * HEAD_DIM)
            k_out[:, sl] = _rope(normed(ak_ref[:, sl], kw_ref[...]), c, sn).astype(BF16)
        for j in range(nr):
            sl = slice(j * RET_QK_DIM, (j + 1) * RET_QK_DIM)
            rq_out[:, sl] = _rope(rq_ref[:, sl], c, sn).astype(BF16)
            rk_out[:, sl] = (_rope(rk_ref[:, sl], c, sn) * ret_scale).astype(BF16)

    def row(arr):
        return pl.BlockSpec((ts, arr.shape[1]), lambda i: (i, 0))

    vec = pl.BlockSpec((1, HEAD_DIM), lambda i: (0, 0))
    ins = [aq, ak, rq, rk]
    return pl.pallas_call(
        body, name="prep_fwd", grid=(s // ts,),
        in_specs=[row(a) for a in ins] + [row(cos), row(sin), vec, vec],
        out_specs=[row(a) for a in ins],
        out_shape=[jax.ShapeDtypeStruct(a.shape, BF16) for a in ins],
        compiler_params=_params(("parallel",)),
    )(*ins, cos, sin, qw.reshape(1, HEAD_DIM), kw.reshape(1, HEAD_DIM))


def _prep_bwd(dq, dk, drq, drk, aq, ak, cos, sin, qw, kw, dav, dag, drv, drg, *, ts=256):
    s = aq.shape[0]
    ts = _tile(s, ts)
    attn_scale = HEAD_DIM ** -0.5
    ret_scale = RET_QK_DIM ** -0.5
    nq, nk, nr = aq.shape[1] // HEAD_DIM, ak.shape[1] // HEAD_DIM, drq.shape[1] // RET_QK_DIM
    widths = (aq.shape[1], ak.shape[1], dav.shape[1], dag.shape[1], drq.shape[1], drk.shape[1], drv.shape[1],
              drg.shape[1])
    o_aq, o_ak, o_av, o_ag, o_rq, o_rk, o_rv, o_rg = (int(o) for o in np.cumsum((0,) + widths)[:-1])

    def body(dq_ref, dk_ref, drq_ref, drk_ref, aq_ref, ak_ref, cos_ref, sin_ref, dav_ref, dag_ref, drv_ref, drg_ref,
             qw_ref, kw_ref, dproj_ref, dqw_ref, dkw_ref):
        c, sn = cos_ref[...], sin_ref[...]
        for ref, off in ((dav_ref, o_av), (dag_ref, o_ag), (drv_ref, o_rv), (drg_ref, o_rg)):
            dproj_ref[:, off:off + ref.shape[1]] = ref[...]

        def unrope(d):
            return _rope_transposed(d, c, sn)

        def norm_bwd(dun, u, w):
            r = lax.rsqrt(jnp.mean(u * u, axis=-1, keepdims=True) + EPS)
            uh = u * r
            dn = dun * w
            du = r * (dn - uh * jnp.mean(dn * uh, axis=-1, keepdims=True))
            return du, jnp.sum(dun * uh, axis=0, keepdims=True)

        dqw = jnp.zeros((1, HEAD_DIM), F32)
        for j in range(nq):
            sl = slice(j * HEAD_DIM, (j + 1) * HEAD_DIM)
            du, dw = norm_bwd(unrope(dq_ref[:, sl] * attn_scale), aq_ref[:, sl], qw_ref[...])
            dproj_ref[:, o_aq + j * HEAD_DIM:o_aq + (j + 1) * HEAD_DIM] = du.astype(BF16)
            dqw = dqw + dw
        dkw = jnp.zeros((1, HEAD_DIM), F32)
        for j in range(nk):
            sl = slice(j * HEAD_DIM, (j + 1) * HEAD_DIM)
            du, dw = norm_bwd(unrope(dk_ref[:, sl]), ak_ref[:, sl], kw_ref[...])
            dproj_ref[:, o_ak + j * HEAD_DIM:o_ak + (j + 1) * HEAD_DIM] = du.astype(BF16)
            dkw = dkw + dw
        for j in range(nr):
            sl = slice(j * RET_QK_DIM, (j + 1) * RET_QK_DIM)
            dproj_ref[:, o_rq + j * RET_QK_DIM:o_rq + (j + 1) * RET_QK_DIM] = unrope(drq_ref[:, sl]).astype(BF16)
            dproj_ref[:, o_rk + j * RET_QK_DIM:o_rk + (j + 1) * RET_QK_DIM] = (
                unrope(drk_ref[:, sl] * ret_scale).astype(BF16))

        @pl.when(pl.program_id(0) == 0)
        def _():
            dqw_ref[...] = dqw
            dkw_ref[...] = dkw

        @pl.when(pl.program_id(0) > 0)
        def _():
            dqw_ref[...] += dqw
            dkw_ref[...] += dkw

    def row(arr):
        return pl.BlockSpec((ts, arr.shape[1]), lambda i: (i, 0))

    vec = pl.BlockSpec((1, HEAD_DIM), lambda i: (0, 0))
    ins = [dq, dk, drq, drk, aq, ak, cos, sin, dav, dag, drv, drg]
    total = sum(widths)
    return pl.pallas_call(
        body, name="prep_bwd", grid=(s // ts,),
        in_specs=[row(a) for a in ins] + [vec, vec],
        out_specs=[pl.BlockSpec((ts, total), lambda i: (i, 0)), vec, vec],
        out_shape=[jax.ShapeDtypeStruct((s, total), BF16)] + [jax.ShapeDtypeStruct((1, HEAD_DIM), F32)] * 2,
        compiler_params=_params(("arbitrary",)),
    )(*ins, qw.reshape(1, HEAD_DIM), kw.reshape(1, HEAD_DIM))


def _attn_fwd(q, k, v, *, tq=4096, sub=128):
    s, aw = q.shape
    tq = _tile(s, tq)
    sub = _tile(tq, sub)
    heads, kvh = aw // HEAD_DIM, k.shape[1] // HEAD_DIM
    grp = heads // kvh

    def body(q_ref, k_ref, v_ref, o_ref, lse_ref):
        kv_ = k_ref[...]
        v_ext = jnp.concatenate([v_ref[...], jnp.ones((s, HEAD_DIM), BF16)], axis=-1)
        for r in range(tq // sub):
            rows = slice(r * sub, (r + 1) * sub)
            sc = lax.dot_general(q_ref[rows, :], kv_, NT_DIMS, preferred_element_type=F32)
            m = jnp.max(sc, axis=-1, keepdims=True)
            p = jnp.exp((sc - m).astype(BF16))
            oe = jnp.dot(p, v_ext, preferred_element_type=F32)
            l = oe[:, HEAD_DIM:HEAD_DIM + 1]
            o_ref[rows, :] = (oe[:, :HEAD_DIM] / l).astype(o_ref.dtype)
            lse_ref[rows, :] = jnp.broadcast_to(m + jnp.log(l), (sub, HEAD_DIM))

    qspec = pl.BlockSpec((tq, HEAD_DIM), lambda kv, g, i: (i, kv * grp + g))
    kspec = pl.BlockSpec((s, HEAD_DIM), lambda kv, g, i: (0, kv))
    return pl.pallas_call(
        body, name="attn_fwd", grid=(kvh, grp, s // tq),
        in_specs=[qspec, kspec, kspec],
        out_specs=[qspec, qspec],
        out_shape=[jax.ShapeDtypeStruct((s, aw), BF16), jax.ShapeDtypeStruct((s, aw), F32)],
        compiler_params=_params(("parallel", "parallel", "parallel")),
    )(q, k, v)


def _attn_bwd(q, k, v, o, do, lse, *, tq=1024, sub=256):
    s, aw = q.shape
    tq = _tile(s, tq)
    sub = _tile(tq, sub)
    heads, kvh = aw // HEAD_DIM, k.shape[1] // HEAD_DIM
    grp = heads // kvh
    nq = s // tq

    def body(q_ref, k_ref, v_ref, o_ref, do_ref, lse_ref, dq_ref, dk_ref, dv_ref, dk_acc, dv_acc, p_scr, ds_scr):
        g, i = pl.program_id(1), pl.program_id(2)
        kv_, vv = k_ref[...], v_ref[...]
        for r in range(tq // sub):
            rows = slice(r * sub, (r + 1) * sub)
            qv, dov = q_ref[rows, :], do_ref[rows, :]
            sc = lax.dot_general(qv, kv_, NT_DIMS, preferred_element_type=F32)
            p = jnp.exp((sc - lse_ref[rows, :1]).astype(BF16))
            dp = lax.dot_general(dov, vv, NT_DIMS, preferred_element_type=F32)
            delta = jnp.sum(dov.astype(F32) * o_ref[rows, :].astype(F32), axis=-1, keepdims=True)
            ds = p * (dp - delta).astype(BF16)
            dq_ref[rows, :] = jnp.dot(ds, kv_, preferred_element_type=F32)
            p_scr[rows, :] = p
            ds_scr[rows, :] = ds
        dvp = lax.dot_general(p_scr[...], do_ref[...], TN_DIMS, preferred_element_type=F32)
        dkp = lax.dot_general(ds_scr[...], q_ref[...], TN_DIMS, preferred_element_type=F32)
        first = jnp.logical_and(g == 0, i == 0)

        @pl.when(first)
        def _():
            dv_acc[...] = dvp
            dk_acc[...] = dkp

        @pl.when(jnp.logical_not(first))
        def _():
            dv_acc[...] += dvp
            dk_acc[...] += dkp

        @pl.when(jnp.logical_and(g == grp - 1, i == nq - 1))
        def _():
            dk_ref[...] = dk_acc[...]
            dv_ref[...] = dv_acc[...].astype(dv_ref.dtype)

    qspec = pl.BlockSpec((tq, HEAD_DIM), lambda kv, g, i: (i, kv * grp + g))
    kspec = pl.BlockSpec((s, HEAD_DIM), lambda kv, g, i: (0, kv))
    return pl.pallas_call(
        body, name="attn_bwd", grid=(kvh, grp, nq),
        in_specs=[qspec, kspec, kspec, qspec, qspec, qspec],
        out_specs=[qspec, kspec, kspec],
        out_shape=[jax.ShapeDtypeStruct((s, aw), F32), jax.ShapeDtypeStruct(k.shape, F32),
                   jax.ShapeDtypeStruct(v.shape, BF16)],
        scratch_shapes=[pltpu.VMEM((s, HEAD_DIM), F32), pltpu.VMEM((s, HEAD_DIM), F32),
                        pltpu.VMEM((tq, s), BF16), pltpu.VMEM((tq, s), BF16)],
        compiler_params=_params(("parallel", "arbitrary", "arbitrary")),
    )(q, k, v, o, do, lse)


def _sum_all(z):
    return jnp.sum(jnp.sum(z, axis=0, keepdims=True), axis=1, keepdims=True)


def _chunk_consts(df_ref, db_ref, t):
    lf = _log_sigmoid(df_ref[0][:, :1])
    lb = _log_sigmoid(db_ref[0][:, :1])
    r = lax.broadcasted_iota(jnp.int32, (t, 1), 0).astype(F32)
    c = lax.broadcasted_iota(jnp.int32, (1, t), 1).astype(F32)
    diff = r - c
    dm = jnp.exp(diff * jnp.where(diff >= 0, lf, -lb))
    return dict(diff=diff, dm=dm, r=r,
                af=jnp.exp(lf * (r + 1.0)), bf=jnp.exp(lf * (t - 1.0 - r)), gf=jnp.exp(lf * t),
                ab=jnp.exp(lb * (t - r)), bb=jnp.exp(lb * r), gb=jnp.exp(lb * t))


def _scaled(x, f):
    return (x.astype(F32) * f).astype(BF16)


def _retc_specs(s):
    qspec = pl.BlockSpec((s, RET_QK_DIM), lambda h: (0, h))
    vspec = pl.BlockSpec((s, RET_V_DIM), lambda h: (0, h))
    dspec = pl.BlockSpec((1, 1, LANES), lambda h: (h, 0, 0))
    return qspec, vspec, dspec


def _retc_fwd(q, k, v, dec_f, dec_b, *, t=256):
    s, qw = q.shape
    t = _tile(s, t)
    heads, nc = qw // RET_QK_DIM, s // t
    qspec, vspec, dspec = _retc_specs(s)

    def body(q_ref, k_ref, v_ref, df_ref, db_ref, o_ref):
        cs = _chunk_consts(df_ref, db_ref, t)

        def rows_of(i):
            return pl.ds(pl.multiple_of(i * t, t), t)

        def forward(i, sf):
            rows = rows_of(i)
            qi, ki, vi = q_ref[rows, :], k_ref[rows, :], v_ref[rows, :]
            sc = lax.dot_general(qi, ki, NT_DIMS, preferred_element_type=F32)
            intra = jnp.dot((sc * cs["dm"]).astype(BF16), vi, preferred_element_type=F32)
            cross = jnp.dot(_scaled(qi, cs["af"]), sf.astype(BF16), preferred_element_type=F32)
            o_ref[rows, :] = intra + cross
            return cs["gf"] * sf + lax.dot_general(_scaled(ki, cs["bf"]), vi, TN_DIMS, preferred_element_type=F32)

        def backward(j, sb):
            rows = rows_of(nc - 1 - j)
            qi, ki, vi = q_ref[rows, :], k_ref[rows, :], v_ref[rows, :]
            o_ref[rows, :] += jnp.dot(_scaled(qi, cs["ab"]), sb.astype(BF16), preferred_element_type=F32)
            return cs["gb"] * sb + lax.dot_general(_scaled(ki, cs["bb"]), vi, TN_DIMS, preferred_element_type=F32)

        zero = jnp.zeros((RET_QK_DIM, RET_V_DIM), F32)
        lax.fori_loop(0, nc, forward, zero, unroll=True)
        lax.fori_loop(0, nc, backward, zero, unroll=True)

    return pl.pallas_call(
        body, name="ret_fwd", grid=(heads,),
        in_specs=[qspec, qspec, vspec, dspec, dspec],
        out_specs=vspec, out_shape=jax.ShapeDtypeStruct(v.shape, F32),
        compiler_params=_params(("parallel",)),
    )(q, k, v, dec_f, dec_b)


def _retc_bwd(q, k, v, do, dec_f, dec_b, *, t=256):
    s, qw = q.shape
    t = _tile(s, t)
    heads, nc = qw // RET_QK_DIM, s // t
    qspec, vspec, dspec = _retc_specs(s)
    gspec = pl.BlockSpec((1, 8, LANES), lambda h: (h, 0, 0))

    def body(q_ref, k_ref, v_ref, do_ref, df_ref, db_ref, dq_ref, dk_ref, dv_ref, gf_ref, gb_ref,
             sf_scr, sb_scr, dv_acc):
        cs = _chunk_consts(df_ref, db_ref, t)
        r, diff, dm = cs["r"], cs["diff"], cs["dm"]

        def rows_of(i):
            return pl.ds(pl.multiple_of(i * t, t), t)

        def tn(a, b):
            return lax.dot_general(a, b, TN_DIMS, preferred_element_type=F32)

        def nt(a, b):
            return lax.dot_general(a, b, NT_DIMS, preferred_element_type=F32)

        def states_f(i, sf):
            sf_scr[i] = sf
            rows = rows_of(i)
            return cs["gf"] * sf + tn(_scaled(k_ref[rows, :], cs["bf"]), v_ref[rows, :])

        def states_b(j, sb):
            i = nc - 1 - j
            sb_scr[i] = sb
            rows = rows_of(i)
            return cs["gb"] * sb + tn(_scaled(k_ref[rows, :], cs["bb"]), v_ref[rows, :])

        zero = jnp.zeros((RET_QK_DIM, RET_V_DIM), F32)
        lax.fori_loop(0, nc, states_f, zero, unroll=True)
        lax.fori_loop(0, nc, states_b, zero, unroll=True)

        def scan_grads(i, state, u, qf, kf, vi, doi, fa, fb, step, wa, wb):
            qa, kb = qf * fa, kf * fb
            ub = u.astype(BF16)
            dqa = nt(doi, state.astype(BF16))
            dkb = nt(vi, ub)
            dv = jnp.dot(kb.astype(BF16), ub, preferred_element_type=F32)
            dlog = _sum_all(dqa * qa * wa) + _sum_all(dkb * kb * wb) + t * step * _sum_all(u * state)
            u_new = step * u + tn(qa.astype(BF16), doi)
            return dqa * fa, dkb * fb, dv, u_new, dlog

        def sweep_f(j, carry):
            u, accf, accb = carry
            i = nc - 1 - j
            rows = rows_of(i)
            qi, ki, vi, doi = q_ref[rows, :], k_ref[rows, :], v_ref[rows, :], do_ref[rows, :]
            sc = nt(qi, ki)
            p = sc * dm
            dp = nt(doi, vi)
            ds = (dp * dm).astype(BF16)
            tt = dp * p * diff
            accf = accf + _sum_all(jnp.where(diff > 0, tt, 0.0))
            accb = accb + _sum_all(jnp.where(diff < 0, -tt, 0.0))
            dq1, dk1, dv1, u, dlog = scan_grads(i, sf_scr[i], u, qi.astype(F32), ki.astype(F32), vi, doi,
                                                cs["af"], cs["bf"], cs["gf"], r + 1.0, t - 1.0 - r)
            dq_ref[rows, :] = jnp.dot(ds, ki, preferred_element_type=F32) + dq1
            dk_ref[rows, :] = tn(ds, qi) + dk1
            dv_acc[rows, :] = tn(p.astype(BF16), doi) + dv1
            return u, accf + dlog, accb

        def sweep_b(i, carry):
            w, accb = carry
            rows = rows_of(i)
            qi, ki, vi, doi = q_ref[rows, :], k_ref[rows, :], v_ref[rows, :], do_ref[rows, :]
            dq1, dk1, dv1, w, dlog = scan_grads(i, sb_scr[i], w, qi.astype(F32), ki.astype(F32), vi, doi,
                                                cs["ab"], cs["bb"], cs["gb"], t - r, r)
            dq_ref[rows, :] += dq1
            dk_ref[rows, :] += dk1
            dv_acc[rows, :] += dv1
            return w, accb + dlog

        z11 = jnp.zeros((1, 1), F32)
        _, accf, accb = lax.fori_loop(0, nc, sweep_f, (zero, z11, z11), unroll=4)
        _, accb = lax.fori_loop(0, nc, sweep_b, (zero, accb), unroll=4)
        dv_ref[...] = dv_acc[...].astype(dv_ref.dtype)
        gf_ref[...] = jnp.broadcast_to((accf / (1.0 + jnp.exp(df_ref[0][:, :1]))).reshape(1, 1, 1), gf_ref.shape)
        gb_ref[...] = jnp.broadcast_to((accb / (1.0 + jnp.exp(db_ref[0][:, :1]))).reshape(1, 1, 1), gb_ref.shape)

    return pl.pallas_call(
        body, name="ret_bwd", grid=(heads,),
        in_specs=[qspec, qspec, vspec, vspec, dspec, dspec],
        out_specs=[qspec, qspec, vspec, gspec, gspec],
        out_shape=[jax.ShapeDtypeStruct(q.shape, F32), jax.ShapeDtypeStruct(k.shape, F32),
                   jax.ShapeDtypeStruct(v.shape, BF16),
                   jax.ShapeDtypeStruct((heads, 8, LANES), F32), jax.ShapeDtypeStruct((heads, 8, LANES), F32)],
        scratch_shapes=[pltpu.VMEM((nc, RET_QK_DIM, RET_V_DIM), F32), pltpu.VMEM((nc, RET_QK_DIM, RET_V_DIM), F32),
                        pltpu.VMEM((s, RET_V_DIM), F32)],
        compiler_params=_params(("parallel",)),
    )(q, k, v, do, dec_f, dec_b)


def _gate_fwd(att, ag, ret, rg, rnw, *, ts=256):
    s, aw = att.shape
    rw = ret.shape[1]
    ts = _tile(s, ts)
    rheads = rw // RET_V_DIM

    def body(att_ref, ag_ref, ret_ref, rg_ref, w_ref, y_ref, yt_ref):
        def put(lo, hi, val):
            y_ref[:, lo:hi] = val.astype(BF16)
            yt_ref[lo:hi, :] = val.T.astype(BF16)

        sa, _ = _silu_parts(ag_ref[...])
        put(0, aw, sa * att_ref[...].astype(F32))
        for h in range(rheads):
            sl = slice(h * RET_V_DIM, (h + 1) * RET_V_DIM)
            rt = ret_ref[:, sl]
            rn = rt * lax.rsqrt(jnp.mean(rt * rt, axis=-1, keepdims=True) + EPS) * w_ref[:, sl]
            sr, _ = _silu_parts(rg_ref[:, sl])
            put(aw + h * RET_V_DIM, aw + (h + 1) * RET_V_DIM, sr * rn)

    def row(w):
        return pl.BlockSpec((ts, w), lambda i: (i, 0))

    return pl.pallas_call(
        body, name="gate_fwd", grid=(s // ts,),
        in_specs=[row(aw), row(aw), row(rw), row(rw), pl.BlockSpec((1, rw), lambda i: (0, 0))],
        out_specs=[row(aw + rw), pl.BlockSpec((aw + rw, ts), lambda i: (0, i))],
        out_shape=[jax.ShapeDtypeStruct((s, aw + rw), BF16), jax.ShapeDtypeStruct((aw + rw, s), BF16)],
        compiler_params=_params(("parallel",)),
    )(att, ag, ret, rg, rnw.reshape(1, rw))


def _gate_bwd(dy, att, ag, ret, rg, rnw, *, ts=256):
    s, aw = att.shape
    rw = ret.shape[1]
    ts = _tile(s, ts)
    rheads = rw // RET_V_DIM

    def body(dy_ref, att_ref, ag_ref, ret_ref, rg_ref, w_ref, datt_ref, dag_ref, dret_ref, drg_ref, dw_ref):
        sa, dsa = _silu_parts(ag_ref[...])
        dya = dy_ref[:, :aw]
        datt_ref[...] = (dya * sa).astype(BF16)
        dag_ref[...] = (dya * att_ref[...].astype(F32) * dsa).astype(BF16)
        parts = []
        for h in range(rheads):
            sl = slice(h * RET_V_DIM, (h + 1) * RET_V_DIM)
            rt = ret_ref[:, sl]
            rr = lax.rsqrt(jnp.mean(rt * rt, axis=-1, keepdims=True) + EPS)
            rh = rt * rr
            wv = w_ref[:, sl]
            sr, dsr = _silu_parts(rg_ref[:, sl])
            dyr = dy_ref[:, aw + h * RET_V_DIM:aw + (h + 1) * RET_V_DIM]
            drg_ref[:, sl] = (dyr * rh * wv * dsr).astype(BF16)
            drn = dyr * sr
            dn = drn * wv
            dret_ref[:, sl] = (rr * (dn - rh * jnp.mean(dn * rh, axis=-1, keepdims=True))).astype(BF16)
            parts.append(jnp.sum(drn * rh, axis=0, keepdims=True))
        part = jnp.concatenate(parts, axis=-1)

        @pl.when(pl.program_id(0) == 0)
        def _():
            dw_ref[...] = part

        @pl.when(pl.program_id(0) > 0)
        def _():
            dw_ref[...] += part

    def row(w):
        return pl.BlockSpec((ts, w), lambda i: (i, 0))

    vec = pl.BlockSpec((1, rw), lambda i: (0, 0))
    return pl.pallas_call(
        body, name="gate_bwd", grid=(s // ts,),
        in_specs=[row(aw + rw), row(aw), row(aw), row(rw), row(rw), vec],
        out_specs=[row(aw), row(aw), row(rw), row(rw), vec],
        out_shape=[jax.ShapeDtypeStruct((s, aw), BF16), jax.ShapeDtypeStruct((s, aw), BF16),
                   jax.ShapeDtypeStruct((s, rw), BF16), jax.ShapeDtypeStruct((s, rw), BF16),
                   jax.ShapeDtypeStruct((1, rw), F32)],
        compiler_params=_params(("arbitrary",)),
    )(dy, att, ag, ret, rg, rnw.reshape(1, rw))


def _mesh_position():
    x, y, c = lax.axis_index("x"), lax.axis_index("y"), lax.axis_index("c")
    return x, y, c, 4 * x + 2 * y + c


def _peer(x, y, c, k):
    px = 1 - x if k & 4 else x
    py = 1 - y if k & 2 else y
    pc = 1 - c if k & 1 else c
    return (px, py, pc), 4 * px + 2 * py + pc


HBM_SPEC = pl.BlockSpec(memory_space=pltpu.HBM)
SEM_SPEC = pl.BlockSpec(memory_space=pltpu.SEMAPHORE)
ANY_SPEC = pl.BlockSpec(memory_space=pl.ANY)
DATAFLOW = pltpu.SideEffectType.DATAFLOW_SIDE_EFFECTING


def _hbm(a):
    return pltpu.with_memory_space_constraint(a, pltpu.HBM)


def _split_start(name, copies, n, src, land, after):
    def body(*refs):
        (send_sems, recv_sems), token = refs[2 + len(after):4 + len(after)], refs[-1]
        sends, _ = copies(refs[0], refs[1], send_sems, recv_sems)
        for cp in sends:
            cp.start()
        token[...] = jnp.zeros_like(token)

    return pl.pallas_call(
        body, name=name,
        out_shape=(pltpu.SemaphoreType.DMA((n,)), pltpu.SemaphoreType.DMA((n,)),
                   pltpu.HBM(src.shape, src.dtype), pltpu.HBM(land.shape, land.dtype),
                   jax.ShapeDtypeStruct((8, LANES), F32)),
        in_specs=[HBM_SPEC] * 2 + [ANY_SPEC] * len(after),
        out_specs=(SEM_SPEC, SEM_SPEC, HBM_SPEC, HBM_SPEC, pl.BlockSpec(memory_space=pltpu.VMEM)),
        input_output_aliases={0: 2, 1: 3},
        compiler_params=pltpu.CompilerParams(has_side_effects=DATAFLOW),
    )(_hbm(src), _hbm(land), *after)


def _split_wait(name, copies, started, after):
    send_sems, recv_sems, src, land = started[:4]

    def body(*refs):
        sends, recvs = copies(refs[0], refs[1], refs[2], refs[3])
        for cp in sends:
            cp.wait_send()
        for cp in recvs:
            cp.wait_recv()

    return pl.pallas_call(
        body, name=name,
        out_shape=(pltpu.HBM(src.shape, src.dtype), pltpu.HBM(land.shape, land.dtype)),
        in_specs=[HBM_SPEC] * 2 + [SEM_SPEC, SEM_SPEC] + [ANY_SPEC] * len(after),
        out_specs=(HBM_SPEC,) * 2,
        input_output_aliases={0: 0, 1: 1},
        compiler_params=pltpu.CompilerParams(has_side_effects=DATAFLOW),
    )(src, land, send_sems, recv_sems, *after)


def _slab(ref, p, size, axis):
    if axis == 1:
        return ref.at[:, pl.ds(pl.multiple_of(p * size, LANES), size)]
    return ref.at[pl.ds(pl.multiple_of(p * size, 16), size), :]


ALL_PEERS = tuple(range(1, N_DEV))
SIBLING = 1
SAME_CORE_OF_CHIPS = (2, 4, 6)


def _gather_copies(size, axis, ks):
    def copies(shard_ref, full_ref, send_sems, recv_sems):
        x, y, c, me = _mesh_position()
        sends, recvs = [], []
        for j, k in enumerate(ks):
            peer, pid = _peer(x, y, c, k)
            sends.append(pltpu.make_async_remote_copy(
                src_ref=shard_ref, dst_ref=_slab(full_ref, me, size, axis), send_sem=send_sems.at[j],
                recv_sem=recv_sems.at[j], device_id=peer, device_id_type=MESH))
            recvs.append(pltpu.make_async_remote_copy(
                src_ref=shard_ref, dst_ref=_slab(full_ref, pid, size, axis), send_sem=send_sems.at[j],
                recv_sem=recv_sems.at[j], device_id=peer, device_id_type=MESH))
        return sends, recvs

    return copies


def _pass_on_copies(size, axis):
    def copies(shard_ref, full_ref, send_sems, recv_sems):
        x, y, c, _ = _mesh_position()
        sibling, _ = _peer(x, y, c, SIBLING)
        sends, recvs = [], []
        for j, k in enumerate(SAME_CORE_OF_CHIPS):
            _, landed = _peer(x, y, c, k)
            _, siblings = _peer(x, y, c, k ^ SIBLING)
            mine = _slab(full_ref, landed, size, axis)
            sends.append(pltpu.make_async_remote_copy(
                src_ref=mine, dst_ref=mine, send_sem=send_sems.at[j], recv_sem=recv_sems.at[j],
                device_id=sibling, device_id_type=MESH))
            recvs.append(pltpu.make_async_remote_copy(
                src_ref=mine, dst_ref=_slab(full_ref, siblings, size, axis), send_sem=send_sems.at[j],
                recv_sem=recv_sems.at[j], device_id=sibling, device_id_type=MESH))
        return sends, recvs

    return copies


def _scatter_copies(size, axis):
    def copies(grad_ref, land_ref, send_sems, recv_sems):
        x, y, c, me = _mesh_position()
        sends, recvs = [], []
        for k in range(1, N_DEV):
            peer, pid = _peer(x, y, c, k)
            src = _slab(grad_ref, pid, size, axis)
            sends.append(pltpu.make_async_remote_copy(
                src_ref=src, dst_ref=land_ref.at[me], send_sem=send_sems.at[k - 1], recv_sem=recv_sems.at[k - 1],
                device_id=peer, device_id_type=MESH))
            recvs.append(pltpu.make_async_remote_copy(
                src_ref=src, dst_ref=land_ref.at[pid], send_sem=send_sems.at[k - 1], recv_sem=recv_sems.at[k - 1],
                device_id=peer, device_id_type=MESH))
        return sends, recvs

    return copies


PLACE_BANDS = 8


def _place_own(name, src, out_shape, in_spec, out_spec, steps, me):
    def body(me_ref, src_ref, out_ref):
        out_ref[...] = src_ref[...]

    return pl.pallas_call(
        body, name=name, out_shape=out_shape,
        grid_spec=pltpu.PrefetchScalarGridSpec(num_scalar_prefetch=1, grid=(steps,), in_specs=[in_spec],
                                               out_specs=out_spec),
        compiler_params=_params(("parallel",)),
    )(me.reshape(1).astype(jnp.int32), src)


def _cast_place(w_all, layer, axis, me):
    _, rows, cols = w_all.shape
    full_shape = tuple(N_DEV * n if a == axis else n for a, n in enumerate((rows, cols)))
    band = rows // PLACE_BANDS
    if axis == 1:
        full_spec = pl.BlockSpec((band, cols), lambda i, me_ref: (i, me_ref[0]))
    else:
        full_spec = pl.BlockSpec((band, cols), lambda i, me_ref: (me_ref[0] * PLACE_BANDS + i, 0))

    def body(me_ref, w_ref, shard_ref, full_ref):
        shard_ref[...] = w_ref[...].astype(BF16)
        full_ref[...] = w_ref[...].astype(BF16)

    return pl.pallas_call(
        body, name="cast_place",
        out_shape=[jax.ShapeDtypeStruct((rows, cols), BF16), jax.ShapeDtypeStruct(full_shape, BF16)],
        grid_spec=pltpu.PrefetchScalarGridSpec(
            num_scalar_prefetch=1, grid=(PLACE_BANDS,),
            in_specs=[pl.BlockSpec((None, band, cols), lambda i, me_ref: (layer, i, 0))],
            out_specs=[pl.BlockSpec((band, cols), lambda i, me_ref: (i, 0)), full_spec]),
        compiler_params=_params(("parallel",)),
    )(me.reshape(1).astype(jnp.int32), w_all)


def _gather_start(placed, axis, ks, after, tag):
    shard, full = placed
    return _split_start("gather_start_" + tag, _gather_copies(shard.shape[axis], axis, ks), len(ks), shard, full, after)


def _gather_wait(started, axis, ks, after, tag):
    size = started[2].shape[axis]
    return _split_wait("gather_wait_" + tag, _gather_copies(size, axis, ks), started, after)


def _pass_on_start(shard, full, axis, after, tag):
    size = shard.shape[axis]
    return _split_start("pass_on_start_" + tag, _pass_on_copies(size, axis), len(SAME_CORE_OF_CHIPS), shard, full, after)


def _pass_on_wait(started, axis, after, tag):
    size = started[2].shape[axis]
    return _split_wait("pass_on_wait_" + tag, _pass_on_copies(size, axis), started, after)[1]


def _scatter_start(grad, axis, me, tag):
    size = grad.shape[axis] // N_DEV
    rows, cols = tuple(size if a == axis else n for a, n in enumerate(grad.shape))
    band = rows // PLACE_BANDS
    if axis == 1:
        in_spec = pl.BlockSpec((band, cols), lambda i, me_ref: (i, me_ref[0]))
    else:
        in_spec = pl.BlockSpec((band, cols), lambda i, me_ref: (me_ref[0] * PLACE_BANDS + i, 0))
    out_spec = pl.BlockSpec((None, band, cols), lambda i, me_ref: (me_ref[0], i, 0))
    land = _place_own("place_slab", grad, jax.ShapeDtypeStruct((N_DEV, rows, cols), grad.dtype), in_spec, out_spec,
                      PLACE_BANDS, me)
    return _split_start("scatter_start_" + tag, _scatter_copies(size, axis), N_DEV - 1, grad, land, [])


def _scatter_wait(started, axis, after, tag):
    size = started[2].shape[axis] // N_DEV
    land = _split_wait("scatter_wait_" + tag, _scatter_copies(size, axis), started, after)[1]
    return [(land, p) for p in range(N_DEV)]


N_CHIPS = N_DEV // 2


def _pair_copies(size, axis):
    def copies(grad_ref, land_ref, send_sems, recv_sems):
        x, y, c, _ = _mesh_position()
        sibling, _ = _peer(x, y, c, SIBLING)
        sends, recvs = [], []
        for j in range(N_CHIPS):
            _, owner = _peer(x, y, c, (2 * j) ^ SIBLING)
            for lst in (sends, recvs):
                lst.append(pltpu.make_async_remote_copy(
                    src_ref=_slab(grad_ref, owner, size, axis), dst_ref=land_ref.at[j], send_sem=send_sems.at[j],
                    recv_sem=recv_sems.at[j], device_id=sibling, device_id_type=MESH))
        return sends, recvs

    return copies


def _chips_copies():
    def copies(pair_ref, land_ref, send_sems, recv_sems):
        x, y, c, _ = _mesh_position()
        sends, recvs = [], []
        for j in range(1, N_CHIPS):
            owner, _ = _peer(x, y, c, 2 * j)
            for lst in (sends, recvs):
                lst.append(pltpu.make_async_remote_copy(
                    src_ref=pair_ref.at[j], dst_ref=land_ref.at[j], send_sem=send_sems.at[j - 1],
                    recv_sem=recv_sems.at[j - 1], device_id=owner, device_id_type=MESH))
        return sends, recvs

    return copies


def _pair_start(grad, axis, tag):
    size = grad.shape[axis] // N_DEV
    rows, cols = tuple(size if a == axis else n for a, n in enumerate(grad.shape))
    land = lax.empty((N_CHIPS, rows, cols), grad.dtype)
    return _split_start("pair_start_" + tag, _pair_copies(size, axis), N_CHIPS, grad, land, [])


def _pair_sums(started, axis, me, after, tag):
    size = started[2].shape[axis] // N_DEV
    grad, land = _split_wait("pair_wait_" + tag, _pair_copies(size, axis), started, after)
    _, rows, cols = land.shape
    band = rows // PLACE_BANDS
    if axis == 1:
        mine = pl.BlockSpec((band, cols), lambda j, i, me_ref: (i, me_ref[0] ^ (2 * j)))
    else:
        mine = pl.BlockSpec((band, cols), lambda j, i, me_ref: ((me_ref[0] ^ (2 * j)) * PLACE_BANDS + i, 0))
    slot = pl.BlockSpec((None, band, cols), lambda j, i, me_ref: (j, i, 0))

    def body(own_ref, mine_ref, theirs_ref, out_ref):
        out_ref[...] = (mine_ref[...].astype(F32) + theirs_ref[...].astype(F32)).astype(out_ref.dtype)

    return pl.pallas_call(
        body, name="pair_sums", out_shape=jax.ShapeDtypeStruct(land.shape, land.dtype),
        grid_spec=pltpu.PrefetchScalarGridSpec(num_scalar_prefetch=1, grid=(N_CHIPS, PLACE_BANDS),
                                               in_specs=[mine, slot], out_specs=slot),
        compiler_params=_params(("parallel", "parallel")),
    )(me.reshape(1).astype(jnp.int32), grad, land)


def _chips_start(pairs, tag):
    return _split_start("chips_start_" + tag, _chips_copies(), N_CHIPS - 1, pairs, lax.empty(pairs.shape, pairs.dtype), [])


def _chips_wait(started, after, tag):
    pairs, land = _split_wait("chips_wait_" + tag, _chips_copies(), started, after)
    return [(pairs, 0)] + [(land, j) for j in range(1, N_CHIPS)]


def _exchange_small(buf, *, name, after=()):
    r = buf.shape[0]

    def body(*refs):
        buf_ref = refs[0]
        all_ref, sum_ref, send_sems, recv_sems = refs[1 + len(after):]
        x, y, c, me = _mesh_position()
        all_ref[me] = buf_ref[...]
        sends, recvs = [], []
        for k in range(1, N_DEV):
            peer, pid = _peer(x, y, c, k)
            sends.append(pltpu.make_async_remote_copy(
                src_ref=buf_ref, dst_ref=all_ref.at[me], send_sem=send_sems.at[k - 1], recv_sem=recv_sems.at[k - 1],
                device_id=peer, device_id_type=MESH))
            recvs.append(pltpu.make_async_remote_copy(
                src_ref=buf_ref, dst_ref=all_ref.at[pid], send_sem=send_sems.at[k - 1], recv_sem=recv_sems.at[k - 1],
                device_id=peer, device_id_type=MESH))
        for cp in sends:
            cp.start()
        for cp in recvs:
            cp.wait_recv()
        for cp in sends:
            cp.wait_send()
        total = all_ref[0]
        for p in range(1, N_DEV):
            total = total + all_ref[p]
        sum_ref[...] = total

    vmem = pl.BlockSpec(memory_space=pltpu.VMEM)
    return pl.pallas_call(
        body, name=name,
        in_specs=[vmem] + [ANY_SPEC] * len(after), out_specs=[vmem, vmem],
        out_shape=[jax.ShapeDtypeStruct((N_DEV, r, LANES), F32), jax.ShapeDtypeStruct((r, LANES), F32)],
        scratch_shapes=[pltpu.SemaphoreType.DMA((N_DEV - 1,)), pltpu.SemaphoreType.DMA((N_DEV - 1,))],
        compiler_params=pltpu.CompilerParams(has_side_effects=True),
    )(buf, *after)


def _adamw_math(w, g, m, v):
    m2 = ADAM_B1 * m + (1.0 - ADAM_B1) * g
    v2 = ADAM_B2 * v + (1.0 - ADAM_B2) * (g * g)
    delta = -ADAM_LR * ((m2 / ADAM_C1) / (jnp.sqrt(v2 / ADAM_C2) + ADAM_EPS) + ADAM_WD * w)
    return delta, m2, v2


def _adamw_slabs(layer, w, m, v, addends, outs, order, *, tr, name):
    depth, r, c = w.shape
    tr = _tile(r, tr)
    n = len(addends)

    def body(*refs):
        w_ref, m_ref, v_ref = refs[:3]
        g_ref, d_ref, m2_ref, v2_ref = refs[-4:]
        g = refs[3][...].astype(F32)
        for a_ref in refs[4:3 + n]:
            g = g + a_ref[...].astype(F32)
        delta, m2, v2 = _adamw_math(w_ref[...], g, m_ref[...], v_ref[...])
        g_ref[...] = g
        d_ref[...] = delta
        m2_ref[...] = m2
        v2_ref[...] = v2

    row = pl.BlockSpec((None, tr, c), lambda i: (layer, i, 0))
    slots = [pl.BlockSpec((None, tr, c), lambda i, p=p: (p, i, 0)) for _, p in addends]
    first_out = 3 + n + 1
    return pl.pallas_call(
        body, name=name, grid=(r // tr,),
        in_specs=[row, row, row] + slots + [pl.BlockSpec((8, LANES), lambda i: (0, 0))] + [ANY_SPEC] * 4,
        out_specs=[row] * 4, out_shape=[jax.ShapeDtypeStruct((depth, r, c), F32)] * 4,
        input_output_aliases={first_out + t: t for t in range(4)},
        compiler_params=_params(("parallel",)),
    )(w, m, v, *[a for a, _ in addends], order, *outs)


def _adamw_small(w, g, m, v):
    def body(w_ref, g_ref, m_ref, v_ref, d_ref, m2_ref, v2_ref):
        delta, m2, v2 = _adamw_math(w_ref[...], g_ref[...], m_ref[...], v_ref[...])
        d_ref[...] = delta
        m2_ref[...] = m2
        v2_ref[...] = v2

    vmem = pl.BlockSpec(memory_space=pltpu.VMEM)
    return pl.pallas_call(
        body, name="adamw_small", in_specs=[vmem] * 4, out_specs=[vmem] * 3,
        out_shape=[jax.ShapeDtypeStruct(w.shape, F32)] * 3,
    )(w, g, m, v)


def _pack(parts):
    flat = jnp.concatenate([p.reshape(-1).astype(F32) for p in parts])
    rows = -(-flat.shape[0] // LANES)
    rows = -(-rows // SMALL_ROWS_ALIGN) * SMALL_ROWS_ALIGN
    flat = jnp.pad(flat, (0, rows * LANES - flat.shape[0]))
    return flat.reshape(rows, LANES)


def _unpack(buf, shapes):
    flat = buf.reshape(-1)
    out, pos = [], 0
    for shp in shapes:
        size = math.prod(shp)
        out.append(flat[pos:pos + size].reshape(shp))
        pos += size
    return out


def _section_widths(d):
    aw = d // 2
    kw = aw // ATTN_GROUP
    rw = d - aw
    rqw = (rw // RET_V_DIM) * RET_QK_DIM
    return (aw, kw, kw, aw, rqw, rqw, rw, rw)


def _layer_fwd(xl, hh, win_full, behind, after_attn, wout_of, qn, kn, dec_f, dec_b, rn, cos, sin):
    h, ht = hh
    aq, ak, v, ag, rq, rk, rvb, rg = _proj_sections(h, win_full, _section_widths(xl.shape[1]),
                                                    (F32, F32, BF16, F32, F32, F32, BF16, F32), after=behind)
    q, k, rqr, rkr = _prep_fwd(aq, ak, rq, rk, cos, sin, qn, kn)
    att, lse = _attn_fwd(q, k, v)
    ret = _retc_fwd(rqr, rkr, rvb, dec_f + after_attn(att), dec_b)
    y, yt = _gate_fwd(att, ag, ret, rg, rn)
    wout_full = wout_of(y)
    xn = _matmul(y, wout_full, name="out_proj", residual=xl)
    saved = dict(x=xl, ht=ht, aq=aq, ak=ak, ag=ag, rg=rg, q=q, k=k, v=v, rq=rqr, rk=rkr, rv=rvb,
                 att=att, lse=lse, ret=ret, yt=yt, win=win_full, wout=wout_full)
    return xn, saved


def _layer_bwd_weights(gb, sv, qn, kn, dec_f, dec_b, rn, cos, sin, on_dwout):
    dy = _matmul(gb, sv["wout"], name="d_y", trans_b=True, tn=1024)
    dwout = _matmul(sv["yt"], gb, name="d_wout", out_dtype=BF16, tn=1024)
    datt, dag, dret, drg, drn = _gate_bwd(dy, sv["att"], sv["ag"], sv["ret"], sv["rg"], rn + on_dwout(dwout))
    dq, dk, dav = _attn_bwd(sv["q"], sv["k"], sv["v"], sv["att"], datt, sv["lse"])
    drq, drk, drv, gf, gbk = _retc_bwd(sv["rq"], sv["rk"], sv["rv"], dret, dec_f, dec_b)
    dproj, dqn, dkn = _prep_bwd(dq, dk, drq, drk, sv["aq"], sv["ak"], cos, sin, qn, kn, dav, dag, drv, drg)
    dwin = _matmul(sv["ht"], dproj, name="d_win", out_dtype=BF16)
    small = dict(qn=dqn[0], kn=dkn[0], df=gf[:, 0, 0], db=gbk[:, 0, 0], rn=drn[0])
    return dproj, dwin, small


def _layer_bwd_input(g, dproj, sv, nw, behind, after_dh):
    dh = _matmul(dproj, sv["win"], name="d_h", trans_b=True, tm=512, tk=dproj.shape[1], after=behind)
    g, gb, dnw = _rms_bwd(dh, sv["x"], g, nw + after_dh(dh))
    return g, gb, dnw[0]


def kernel(x, norm_w, w_in, q_norm, k_norm, ret_decay_fwd, ret_decay_bwd, ret_norm, w_out, final_norm, loss_target, m_norm_w, m_w_in, m_q_norm, m_k_norm, m_ret_decay_fwd, m_ret_decay_bwd, m_ret_norm, m_w_out, m_final_norm, v_norm_w, v_w_in, v_q_norm, v_k_norm, v_ret_decay_fwd, v_ret_decay_bwd, v_ret_norm, v_w_out, v_final_norm):
    depth, d, _ = w_in.shape
    seq = x.shape[1]
    rw = _section_widths(d)[6]
    rheads = rw // RET_V_DIM
    rns = ret_norm.shape[-1]
    _, _, _, me = _mesh_position()

    target = loss_target[0]
    cos, sin = _rope_tables(seq)

    rn_all, _ = _exchange_small(_pack([ret_norm]), name="gather_ret_norm")
    rn_full = rn_all.reshape(N_DEV, -1)[:, :depth * rheads * rns].reshape(N_DEV, depth, rheads, rns)
    rn_full = jnp.transpose(rn_full, (1, 2, 0, 3)).reshape(depth, rw)

    dec_f = jnp.broadcast_to(ret_decay_fwd[:, :, None, None], (depth, rheads, 1, LANES))
    dec_b = jnp.broadcast_to(ret_decay_bwd[:, :, None, None], (depth, rheads, 1, LANES))


    saved = []
    xl = x[0]
    first = (SIBLING,) + SAME_CORE_OF_CHIPS
    in_sent = _gather_start(_cast_place(w_in, 0, 1, me), 1, first, [], "in0")
    hh = _rms_fwd(xl, norm_w[0] + in_sent[-1][0, 0])
    placed_out = _cast_place(w_out, 0, 0, me)
    placed_in = _cast_place(w_in, 1, 1, me) if depth > 1 else None
    under_way = [hh[0], placed_out[0], cos, sin] + ([placed_in[0]] if depth > 1 else [])
    shard, landed = _gather_wait(in_sent, 1, first, under_way, "in0")
    win_full = _pass_on_wait(_pass_on_start(shard, landed, 1, [], "in0"), 1, [], "in0")
    for l in range(depth):
        if l > 0:
            hh = _rms_fwd(xl, norm_w[l])
            placed_out = _cast_place(w_out, l, 0, me)
            placed_in = _cast_place(w_in, l + 1, 1, me) if l + 1 < depth else None
        out_sent = _gather_start(placed_out, 0, ALL_PEERS, [win_full], "out" + str(l))
        behind = [out_sent[-1]]
        passed = {}
        if l + 1 < depth:
            in_sent = _gather_start(placed_in, 1, first, [win_full, out_sent[-1]], "in" + str(l + 1))
            behind.append(in_sent[-1])

        def after_attn(att, passed=passed, l=l):
            if l + 1 == depth:
                return 0.0
            shard, landed = _gather_wait(in_sent, 1, first, [att], "in" + str(l + 1))
            passed["on"] = _pass_on_start(shard, landed, 1, [], "in" + str(l + 1))
            return passed["on"][-1][0, 0]

        def wout_of(y, out_sent=out_sent, l=l):
            return _gather_wait(out_sent, 0, ALL_PEERS, [y], "out" + str(l))[1]

        xl, sv = _layer_fwd(xl, hh, win_full, behind, after_attn, wout_of, q_norm[l], k_norm[l], dec_f[l], dec_b[l],
                            rn_full[l], cos, sin)
        saved.append(sv)
        if l + 1 < depth:
            win_full = _pass_on_wait(passed["on"], 1, [xl], "in" + str(l + 1))

    loss_row, g, gb, d_final = _loss_head(xl, target, final_norm)

    d_norm, d_qn, d_kn, d_df, d_db, d_rn = [], [], [], [], [], []
    lands = [None] * depth
    pending = None
    for l in reversed(range(depth)):
        sent = {}

        def on_dwout(dwout, sent=sent, l=l):
            sent["out"] = _scatter_start(dwout, 0, me, "out" + str(l))
            return sent["out"][-1][0, 0]

        def after_dh(dh, sent=sent, l=l):
            pairs = _pair_sums(sent["pair"], 1, me, [dh], "in" + str(l))
            sent["in"] = _chips_start(pairs, "in" + str(l))
            return sent["in"][-1][0, 0]

        dproj, dwin, sm = _layer_bwd_weights(gb, saved[l], q_norm[l], k_norm[l], dec_f[l], dec_b[l],
                                             rn_full[l], cos, sin, on_dwout)
        sent["pair"] = _pair_start(dwin, 1, "in" + str(l))
        g, gb, dnw = _layer_bwd_input(g, dproj, saved[l], norm_w[l], [sent["pair"][-1]], after_dh)
        if pending is not None:
            lands[l + 1] = (_chips_wait(pending["in"], [g], "in" + str(l + 1)),
                            _scatter_wait(pending["out"], 0, [g], "out" + str(l + 1)))
        pending = sent
        d_norm.append(dnw)
        d_qn.append(sm["qn"])
        d_kn.append(sm["kn"])
        d_df.append(sm["df"])
        d_db.append(sm["db"])
        d_rn.append(sm["rn"])
    for lst in (d_norm, d_qn, d_kn, d_df, d_db, d_rn):
        lst.reverse()
    order = pending["in"][-1]
    in_outs = [lax.empty(w_in.shape, F32) for _ in range(4)]
    out_outs = [lax.empty(w_out.shape, F32) for _ in range(4)]
    for l in reversed(range(1, depth)):
        in_outs = _adamw_slabs(l, w_in, m_w_in, v_w_in, lands[l][0], in_outs, order, tr=256, name="adamw_w_in")
        out_outs = _adamw_slabs(l, w_out, m_w_out, v_w_out, lands[l][1], out_outs, order, tr=64, name="adamw_w_out")
    land_out = _scatter_wait(pending["out"], 0, [g, out_outs[0]], "out0")
    out_outs = _adamw_slabs(0, w_out, m_w_out, v_w_out, land_out, out_outs, order, tr=64, name="adamw_w_out")

    small_shapes = [(depth, d), (depth, HEAD_DIM), (depth, HEAD_DIM), (depth, rheads), (depth, rheads),
                    (depth, rheads, N_DEV * rns), (d,), (1,)]
    grads_local = [jnp.stack(d_norm), jnp.stack(d_qn), jnp.stack(d_kn), jnp.stack(d_df), jnp.stack(d_db),
                   jnp.stack(d_rn).reshape(depth, rheads, N_DEV * rns), d_final[0], loss_row[0, :1]]
    _, gsum = _exchange_small(_pack(grads_local), name="all_reduce_small", after=(in_outs[0], out_outs[0]))
    land_in = _chips_wait(pending["in"], [g, gsum], "in0")
    in_outs = _adamw_slabs(0, w_in, m_w_in, v_w_in, land_in, in_outs, order, tr=256, name="adamw_w_in")
    g_norm, g_qn, g_kn, g_df, g_db, g_rn_full, g_final, loss = _unpack(gsum, small_shapes)
    g_rn = lax.dynamic_slice_in_dim(g_rn_full, me * rns, rns, axis=2)
    small_g = [g_norm, g_qn, g_kn, g_df, g_db, g_rn, g_final]
    small_w = [norm_w, q_norm, k_norm, ret_decay_fwd, ret_decay_bwd, ret_norm, final_norm]
    small_m = [m_norm_w, m_q_norm, m_k_norm, m_ret_decay_fwd, m_ret_decay_bwd, m_ret_norm, m_final_norm]
    small_v = [v_norm_w, v_q_norm, v_k_norm, v_ret_decay_fwd, v_ret_decay_bwd, v_ret_norm, v_final_norm]
    shapes = [a.shape for a in small_w]
    sd, sm, sv2 = _adamw_small(_pack(small_w), _pack(small_g), _pack(small_m), _pack(small_v))
    small_d, small_m2, small_v2 = _unpack(sd, shapes), _unpack(sm, shapes), _unpack(sv2, shapes)

    def ordered(small, win_v, wout_v):
        return [small[0], win_v, small[1], small[2], small[3], small[4], small[5], wout_v, small[6]]

    grads = ordered(small_g, in_outs[0], out_outs[0])
    deltas = ordered(small_d, in_outs[1], out_outs[1])
    new_m = ordered(small_m2, in_outs[2], out_outs[2])
    new_v = ordered(small_v2, in_outs[3], out_outs[3])
    return (loss.reshape(()), g[None], *grads, *deltas, *new_m, *new_v)
```

```python
import math

import jax
import jax.numpy as jnp
import numpy as np
from jax import lax
from jax.experimental import pallas as pl
from jax.experimental.pallas import tpu as pltpu

F32 = jnp.float32
BF16 = jnp.bfloat16

N_DEV = 8
HEAD_DIM = 128
ATTN_GROUP = 4
RET_QK_DIM = 128
RET_V_DIM = 256
GRID_W = 64
ROPE_THETA = 10000.0
EPS = 1e-6
ADAM_LR = 0.001
ADAM_B1 = 0.9
ADAM_B2 = 0.999
ADAM_EPS = 1e-08
ADAM_WD = 0.01
ADAM_STEP = 10
ADAM_C1 = 1.0 - ADAM_B1 ** ADAM_STEP
ADAM_C2 = 1.0 - ADAM_B2 ** ADAM_STEP
LANES = 128
SMALL_ROWS_ALIGN = 8
VMEM_LIMIT = 56 * 1024 * 1024

NT_DIMS = (((1,), (1,)), ((), ()))
TN_DIMS = (((0,), (0,)), ((), ()))
MESH = pl.DeviceIdType.MESH


def _params(sem):
    return pltpu.CompilerParams(dimension_semantics=sem, vmem_limit_bytes=VMEM_LIMIT)


def _tile(dim, pref, align=16):
    if dim <= pref:
        return dim
    for t in range(pref - pref % align, 0, -align):
        if dim % t == 0:
            return t
    raise ValueError((dim, pref, align))


def _silu_parts(z):
    sg = 1.0 / (1.0 + jnp.exp(-z))
    return z * sg, sg * (1.0 + z * (1.0 - sg))


def _log_sigmoid(x):
    return jnp.minimum(x, 0.0) - jnp.log(1.0 + jnp.exp(-jnp.abs(x)))


def _swap_pairs(z):
    src = lax.broadcasted_iota(jnp.int32, (HEAD_DIM, HEAD_DIM), 0)
    dst = lax.broadcasted_iota(jnp.int32, (HEAD_DIM, HEAD_DIM), 1)
    partner = jnp.where((dst % 64) < 32, dst + 32, dst - 32)
    perm = (src == partner).astype(F32)
    return jnp.dot(z, perm, precision=lax.Precision.HIGH, preferred_element_type=F32)


def _rope(z, cos, sin):
    return z * cos + _swap_pairs(z) * sin


def _rope_transposed(d, cos, sin):
    return d * cos + _swap_pairs(d * sin)


def _rope_tables(seq):
    rows = seq // GRID_W
    row = jnp.repeat(jnp.arange(rows), GRID_W).astype(F32)
    col = jnp.tile(jnp.arange(GRID_W), rows).astype(F32)
    axis_dim = HEAD_DIM // 2
    inv = ROPE_THETA ** (-jnp.arange(0, axis_dim, 2, dtype=F32) / axis_dim)
    ar = row[:, None] * inv[None, :]
    ac = col[:, None] * inv[None, :]
    cos = jnp.concatenate([jnp.cos(ar), jnp.cos(ar), jnp.cos(ac), jnp.cos(ac)], axis=-1)
    sin = jnp.concatenate([-jnp.sin(ar), jnp.sin(ar), -jnp.sin(ac), jnp.sin(ac)], axis=-1)
    return cos, sin


def _matmul(a, b, *, name, trans_b=False, out_dtype=F32, residual=None, tm=1024, tn=512, tk=4096, after=()):
    m, k = a.shape
    n = b.shape[0] if trans_b else b.shape[1]
    tm, tn, tk = _tile(m, tm), _tile(n, tn, LANES), _tile(k, tk, LANES)
    nk = k // tk
    has_res = residual is not None

    def body(*refs):
        a_ref, b_ref = refs[:2]
        r_ref = refs[2] if has_res else None
        o_ref = refs[2 + has_res + len(after)]
        if trans_b:
            part = lax.dot_general(a_ref[...], b_ref[...], NT_DIMS, preferred_element_type=F32)
        else:
            part = jnp.dot(a_ref[...], b_ref[...], preferred_element_type=F32)

        def finish(r):
            if has_res:
                r = r + r_ref[...]
            o_ref[...] = r.astype(o_ref.dtype)

        if nk == 1:
            finish(part)
        else:
            acc_ref = refs[-1]
            kk = pl.program_id(2)

            @pl.when(kk == 0)
            def _():
                acc_ref[...] = part

            @pl.when(kk > 0)
            def _():
                acc_ref[...] += part

            @pl.when(kk == nk - 1)
            def _():
                finish(acc_ref[...])

    if trans_b:
        b_spec = pl.BlockSpec((tn, tk), lambda i, j, kk: (j, kk))
    else:
        b_spec = pl.BlockSpec((tk, tn), lambda i, j, kk: (kk, j))
    in_specs = [pl.BlockSpec((tm, tk), lambda i, j, kk: (i, kk)), b_spec]
    args = [a, b]
    if has_res:
        in_specs.append(pl.BlockSpec((tm, tn), lambda i, j, kk: (i, j)))
        args.append(residual)
    in_specs += [ANY_SPEC] * len(after)
    args += list(after)
    return pl.pallas_call(
        body, name=name, grid=(m // tm, n // tn, nk),
        in_specs=in_specs,
        out_specs=pl.BlockSpec((tm, tn), lambda i, j, kk: (i, j)),
        out_shape=jax.ShapeDtypeStruct((m, n), out_dtype),
        scratch_shapes=[pltpu.VMEM((tm, tn), F32)] if nk > 1 else [],
        compiler_params=_params(("parallel", "parallel", "arbitrary")),
    )(*args)


def _proj_sections(a, b, widths, dtypes, *, tm=1024, tn=512, after=()):
    m, k = a.shape
    tm = _tile(m, tm)
    tn = _tile(min(widths), tn, LANES)
    assert all(w % tn == 0 for w in widths) and sum(widths) == b.shape[1]
    nblk = [w // tn for w in widths]
    first = [int(o) // tn for o in np.cumsum((0,) + tuple(widths))[:-1]]

    def body(a_ref, b_ref, *refs):
        j = pl.program_id(1)
        part = jnp.dot(a_ref[...], b_ref[...], preferred_element_type=F32)
        for o_ref, lo, n in zip(refs[len(after):], first, nblk):
            @pl.when(jnp.logical_and(j >= lo, j < lo + n))
            def _(o_ref=o_ref):
                o_ref[...] = part.astype(o_ref.dtype)

    out_specs = [pl.BlockSpec((tm, tn), lambda i, j, lo=lo, n=n: (i, jnp.clip(j - lo, 0, n - 1)))
                 for lo, n in zip(first, nblk)]
    return pl.pallas_call(
        body, name="proj", grid=(m // tm, b.shape[1] // tn),
        in_specs=[pl.BlockSpec((tm, k), lambda i, j: (i, 0)), pl.BlockSpec((k, tn), lambda i, j: (0, j))]
        + [ANY_SPEC] * len(after),
        out_specs=out_specs,
        out_shape=[jax.ShapeDtypeStruct((m, w), dt) for w, dt in zip(widths, dtypes)],
        compiler_params=_params(("arbitrary", "arbitrary")),
    )(a, b, *after)


def _rms_fwd(x, w, *, ts=256):
    s, d = x.shape
    ts = _tile(s, ts)

    def body(x_ref, w_ref, h_ref, ht_ref):
        xv = x_ref[...]
        r = lax.rsqrt(jnp.mean(xv * xv, axis=-1, keepdims=True) + EPS)
        h = xv * r * w_ref[...]
        h_ref[...] = h.astype(BF16)
        ht_ref[...] = h.T.astype(BF16)

    row = pl.BlockSpec((ts, d), lambda i: (i, 0))
    return pl.pallas_call(
        body, name="rms_fwd", grid=(s // ts,),
        in_specs=[row, pl.BlockSpec((1, d), lambda i: (0, 0))],
        out_specs=[row, pl.BlockSpec((d, ts), lambda i: (0, i))],
        out_shape=[jax.ShapeDtypeStruct((s, d), BF16), jax.ShapeDtypeStruct((d, s), BF16)],
        compiler_params=_params(("parallel",)),
    )(x, w.reshape(1, d))


def _rms_bwd(dh, x, g, w, *, ts=256):
    s, d = x.shape
    ts = _tile(s, ts)

    def body(dh_ref, x_ref, g_ref, w_ref, dx_ref, dxb_ref, dw_ref):
        xv = x_ref[...]
        r = lax.rsqrt(jnp.mean(xv * xv, axis=-1, keepdims=True) + EPS)
        xh = xv * r
        dhv = dh_ref[...]
        dn = dhv * w_ref[...]
        dx = g_ref[...] + r * (dn - xh * jnp.mean(dn * xh, axis=-1, keepdims=True))
        dx_ref[...] = dx
        dxb_ref[...] = dx.astype(BF16)
        part = jnp.sum(dhv * xh, axis=0, keepdims=True)

        @pl.when(pl.program_id(0) == 0)
        def _():
            dw_ref[...] = part

        @pl.when(pl.program_id(0) > 0)
        def _():
            dw_ref[...] += part

    row = pl.BlockSpec((ts, d), lambda i: (i, 0))
    vec = pl.BlockSpec((1, d), lambda i: (0, 0))
    return pl.pallas_call(
        body, name="rms_bwd", grid=(s // ts,),
        in_specs=[row, row, row, vec],
        out_specs=[row, row, vec],
        out_shape=[jax.ShapeDtypeStruct((s, d), F32), jax.ShapeDtypeStruct((s, d), BF16),
                   jax.ShapeDtypeStruct((1, d), F32)],
        compiler_params=_params(("arbitrary",)),
    )(dh, x, g, w.reshape(1, d))


def _loss_head(x, target, w, *, ts=256):
    s, d = x.shape
    ts = _tile(s, ts)

    def body(x_ref, t_ref, w_ref, loss_ref, dx_ref, dxb_ref, dw_ref):
        xv = x_ref[...]
        r = lax.rsqrt(jnp.mean(xv * xv, axis=-1, keepdims=True) + EPS)
        xh = xv * r
        wv = w_ref[...]
        diff = xh * wv - t_ref[...]
        lpart = 0.5 * jnp.sum(jnp.mean(diff * diff, axis=-1, keepdims=True), axis=0, keepdims=True)
        dout = diff * (1.0 / d)
        dn = dout * wv
        dx = r * (dn - xh * jnp.mean(dn * xh, axis=-1, keepdims=True))
        dx_ref[...] = dx
        dxb_ref[...] = dx.astype(BF16)
        part = jnp.sum(dout * xh, axis=0, keepdims=True)
        lrow = jnp.broadcast_to(lpart, loss_ref.shape)

        @pl.when(pl.program_id(0) == 0)
        def _():
            dw_ref[...] = part
            loss_ref[...] = lrow

        @pl.when(pl.program_id(0) > 0)
        def _():
            dw_ref[...] += part
            loss_ref[...] += lrow

    row = pl.BlockSpec((ts, d), lambda i: (i, 0))
    vec = pl.BlockSpec((1, d), lambda i: (0, 0))
    return pl.pallas_call(
        body, name="loss_head", grid=(s // ts,),
        in_specs=[row, row, vec],
        out_specs=[pl.BlockSpec((1, LANES), lambda i: (0, 0)), row, row, vec],
        out_shape=[jax.ShapeDtypeStruct((1, LANES), F32), jax.ShapeDtypeStruct((s, d), F32),
                   jax.ShapeDtypeStruct((s, d), BF16), jax.ShapeDtypeStruct((1, d), F32)],
        compiler_params=_params(("arbitrary",)),
    )(x, target, w.reshape(1, d))


def _prep_fwd(aq, ak, rq, rk, cos, sin, qw, kw, *, ts=256):
    s = aq.shape[0]
    ts = _tile(s, ts)
    attn_scale = HEAD_DIM ** -0.5
    ret_scale = RET_QK_DIM ** -0.5
    nq, nk, nr = aq.shape[1] // HEAD_DIM, ak.shape[1] // HEAD_DIM, rq.shape[1] // RET_QK_DIM

    def body(aq_ref, ak_ref, rq_ref, rk_ref, cos_ref, sin_ref, qw_ref, kw_ref,
             q_out, k_out, rq_out, rk_out):
        c, sn = cos_ref[...], sin_ref[...]

        def normed(u, w):
            return u * lax.rsqrt(jnp.mean(u * u, axis=-1, keepdims=True) + EPS) * w

        for j in range(nq):
            sl = slice(j * HEAD_DIM, (j + 1) * HEAD_DIM)
            q_out[:, sl] = (_rope(normed(aq_ref[:, sl], qw_ref[...]), c, sn) * attn_scale).astype(BF16)
        for j in range(nk):
            sl = slice(j * HEAD_DIM, (j + 1) * HEAD_DIM)
            k_out[:, sl] = _rope(normed(ak_ref[:, sl], kw_ref[...]), c, sn).astype(BF16)
        for j in range(nr):
            sl = slice(j * RET_QK_DIM, (j + 1) * RET_QK_DIM)
            rq_out[:, sl] = _rope(rq_ref[:, sl], c, sn).astype(BF16)
            rk_out[:, sl] = (_rope(rk_ref[:, sl], c, sn) * ret_scale).astype(BF16)

    def row(arr):
        return pl.BlockSpec((ts, arr.shape[1]), lambda i: (i, 0))

    vec = pl.BlockSpec((1, HEAD_DIM), lambda i: (0, 0))
    ins = [aq, ak, rq, rk]
    return pl.pallas_call(
        body, name="prep_fwd", grid=(s // ts,),
        in_specs=[row(a) for a in ins] + [row(cos), row(sin), vec, vec],
        out_specs=[row(a) for a in ins],
        out_shape=[jax.ShapeDtypeStruct(a.shape, BF16) for a in ins],
        compiler_params=_params(("parallel",)),
    )(*ins, cos, sin, qw.reshape(1, HEAD_DIM), kw.reshape(1, HEAD_DIM))


def _prep_bwd(dq, dk, drq, drk, aq, ak, cos, sin, qw, kw, dav, dag, drv, drg, *, ts=256):
    s = aq.shape[0]
    ts = _tile(s, ts)
    attn_scale = HEAD_DIM ** -0.5
    ret_scale = RET_QK_DIM ** -0.5
    nq, nk, nr = aq.shape[1] // HEAD_DIM, ak.shape[1] // HEAD_DIM, drq.shape[1] // RET_QK_DIM
    widths = (aq.shape[1], ak.shape[1], dav.shape[1], dag.shape[1], drq.shape[1], drk.shape[1], drv.shape[1],
              drg.shape[1])
    o_aq, o_ak, o_av, o_ag, o_rq, o_rk, o_rv, o_rg = (int(o) for o in np.cumsum((0,) + widths)[:-1])

    def body(dq_ref, dk_ref, drq_ref, drk_ref, aq_ref, ak_ref, cos_ref, sin_ref, dav_ref, dag_ref, drv_ref, drg_ref,
             qw_ref, kw_ref, dproj_ref, dqw_ref, dkw_ref):
        c, sn = cos_ref[...], sin_ref[...]
        for ref, off in ((dav_ref, o_av), (dag_ref, o_ag), (drv_ref, o_rv), (drg_ref, o_rg)):
            dproj_ref[:, off:off + ref.shape[1]] = ref[...]

        def unrope(d):
            return _rope_transposed(d, c, sn)

        def norm_bwd(dun, u, w):
            r = lax.rsqrt(jnp.mean(u * u, axis=-1, keepdims=True) + EPS)
            uh = u * r
            dn = dun * w
            du = r * (dn - uh * jnp.mean(dn * uh, axis=-1, keepdims=True))
            return du, jnp.sum(dun * uh, axis=0, keepdims=True)

        dqw = jnp.zeros((1, HEAD_DIM), F32)
        for j in range(nq):
            sl = slice(j * HEAD_DIM, (j + 1) * HEAD_DIM)
            du, dw = norm_bwd(unrope(dq_ref[:, sl] * attn_scale), aq_ref[:, sl], qw_ref[...])
            dproj_ref[:, o_aq + j * HEAD_DIM:o_aq + (j + 1) * HEAD_DIM] = du.astype(BF16)
            dqw = dqw + dw
        dkw = jnp.zeros((1, HEAD_DIM), F32)
        for j in range(nk):
            sl = slice(j * HEAD_DIM, (j + 1) * HEAD_DIM)
            du, dw = norm_bwd(unrope(dk_ref[:, sl]), ak_ref[:, sl], kw_ref[...])
            dproj_ref[:, o_ak + j * HEAD_DIM:o_ak + (j + 1) * HEAD_DIM] = du.astype(BF16)
            dkw = dkw + dw
        for j in range(nr):
            sl = slice(j * RET_QK_DIM, (j + 1) * RET_QK_DIM)
            dproj_ref[:, o_rq + j * RET_QK_DIM:o_rq + (j + 1) * RET_QK_DIM] = unrope(drq_ref[:, sl]).astype(BF16)
            dproj_ref[:, o_rk + j * RET_QK_DIM:o_rk + (j + 1) * RET_QK_DIM] = (
                unrope(drk_ref[:, sl] * ret_scale).astype(BF16))

        @pl.when(pl.program_id(0) == 0)
        def _():
            dqw_ref[...] = dqw
            dkw_ref[...] = dkw

        @pl.when(pl.program_id(0) > 0)
        def _():
            dqw_ref[...] += dqw
            dkw_ref[...] += dkw

    def row(arr):
        return pl.BlockSpec((ts, arr.shape[1]), lambda i: (i, 0))

    vec = pl.BlockSpec((1, HEAD_DIM), lambda i: (0, 0))
    ins = [dq, dk, drq, drk, aq, ak, cos, sin, dav, dag, drv, drg]
    total = sum(widths)
    return pl.pallas_call(
        body, name="prep_bwd", grid=(s // ts,),
        in_specs=[row(a) for a in ins] + [vec, vec],
        out_specs=[pl.BlockSpec((ts, total), lambda i: (i, 0)), vec, vec],
        out_shape=[jax.ShapeDtypeStruct((s, total), BF16)] + [jax.ShapeDtypeStruct((1, HEAD_DIM), F32)] * 2,
        compiler_params=_params(("arbitrary",)),
    )(*ins, qw.reshape(1, HEAD_DIM), kw.reshape(1, HEAD_DIM))


def _attn_fwd(q, k, v, *, tq=4096, sub=128):
    s, aw = q.shape
    tq = _tile(s, tq)
    sub = _tile(tq, sub)
    heads, kvh = aw // HEAD_DIM, k.shape[1] // HEAD_DIM
    grp = heads // kvh

    def body(q_ref, k_ref, v_ref, o_ref, lse_ref):
        kv_ = k_ref[...]
        v_ext = jnp.concatenate([v_ref[...], jnp.ones((s, HEAD_DIM), BF16)], axis=-1)
        for r in range(tq // sub):
            rows = slice(r * sub, (r + 1) * sub)
            sc = lax.dot_general(q_ref[rows, :], kv_, NT_DIMS, preferred_element_type=F32)
            m = jnp.max(sc, axis=-1, keepdims=True)
            p = jnp.exp((sc - m).astype(BF16))
            oe = jnp.dot(p, v_ext, preferred_element_type=F32)
            l = oe[:, HEAD_DIM:HEAD_DIM + 1]
            o_ref[rows, :] = (oe[:, :HEAD_DIM] / l).astype(o_ref.dtype)
            lse_ref[rows, :] = jnp.broadcast_to(m + jnp.log(l), (sub, HEAD_DIM))

    qspec = pl.BlockSpec((tq, HEAD_DIM), lambda kv, g, i: (i, kv * grp + g))
    kspec = pl.BlockSpec((s, HEAD_DIM), lambda kv, g, i: (0, kv))
    return pl.pallas_call(
        body, name="attn_fwd", grid=(kvh, grp, s // tq),
        in_specs=[qspec, kspec, kspec],
        out_specs=[qspec, qspec],
        out_shape=[jax.ShapeDtypeStruct((s, aw), BF16), jax.ShapeDtypeStruct((s, aw), F32)],
        compiler_params=_params(("parallel", "parallel", "parallel")),
    )(q, k, v)


def _attn_bwd(q, k, v, o, do, lse, *, tq=1024, sub=256):
    s, aw = q.shape
    tq = _tile(s, tq)
    sub = _tile(tq, sub)
    heads, kvh = aw // HEAD_DIM, k.shape[1] // HEAD_DIM
    grp = heads // kvh
    nq = s // tq

    def body(q_ref, k_ref, v_ref, o_ref, do_ref, lse_ref, dq_ref, dk_ref, dv_ref, dk_acc, dv_acc, p_scr, ds_scr):
        g, i = pl.program_id(1), pl.program_id(2)
        kv_, vv = k_ref[...], v_ref[...]
        for r in range(tq // sub):
            rows = slice(r * sub, (r + 1) * sub)
            qv, dov = q_ref[rows, :], do_ref[rows, :]
            sc = lax.dot_general(qv, kv_, NT_DIMS, preferred_element_type=F32)
            p = jnp.exp((sc - lse_ref[rows, :1]).astype(BF16))
            dp = lax.dot_general(dov, vv, NT_DIMS, preferred_element_type=F32)
            delta = jnp.sum(dov.astype(F32) * o_ref[rows, :].astype(F32), axis=-1, keepdims=True)
            ds = p * (dp - delta).astype(BF16)
            dq_ref[rows, :] = jnp.dot(ds, kv_, preferred_element_type=F32)
            p_scr[rows, :] = p
            ds_scr[rows, :] = ds
        dvp = lax.dot_general(p_scr[...], do_ref[...], TN_DIMS, preferred_element_type=F32)
        dkp = lax.dot_general(ds_scr[...], q_ref[...], TN_DIMS, preferred_element_type=F32)
        first = jnp.logical_and(g == 0, i == 0)

        @pl.when(first)
        def _():
            dv_acc[...] = dvp
            dk_acc[...] = dkp

        @pl.when(jnp.logical_not(first))
        def _():
            dv_acc[...] += dvp
            dk_acc[...] += dkp

        @pl.when(jnp.logical_and(g == grp - 1, i == nq - 1))
        def _():
            dk_ref[...] = dk_acc[...]
            dv_ref[...] = dv_acc[...].astype(dv_ref.dtype)

    qspec = pl.BlockSpec((tq, HEAD_DIM), lambda kv, g, i: (i, kv * grp + g))
    kspec = pl.BlockSpec((s, HEAD_DIM), lambda kv, g, i: (0, kv))
    return pl.pallas_call(
        body, name="attn_bwd", grid=(kvh, grp, nq),
        in_specs=[qspec, kspec, kspec, qspec, qspec, qspec],
        out_specs=[qspec, kspec, kspec],
        out_shape=[jax.ShapeDtypeStruct((s, aw), F32), jax.ShapeDtypeStruct(k.shape, F32),
                   jax.ShapeDtypeStruct(v.shape, BF16)],
        scratch_shapes=[pltpu.VMEM((s, HEAD_DIM), F32), pltpu.VMEM((s, HEAD_DIM), F32),
                        pltpu.VMEM((tq, s), BF16), pltpu.VMEM((tq, s), BF16)],
        compiler_params=_params(("parallel", "arbitrary", "arbitrary")),
    )(q, k, v, o, do, lse)


def _sum_all(z):
    return jnp.sum(jnp.sum(z, axis=0, keepdims=True), axis=1, keepdims=True)


def _chunk_consts(df_ref, db_ref, t):
    lf = _log_sigmoid(df_ref[0][:, :1])
    lb = _log_sigmoid(db_ref[0][:, :1])
    r = lax.broadcasted_iota(jnp.int32, (t, 1), 0).astype(F32)
    c = lax.broadcasted_iota(jnp.int32, (1, t), 1).astype(F32)
    diff = r - c
    dm = jnp.exp(diff * jnp.where(diff >= 0, lf, -lb))
    return dict(diff=diff, dm=dm, r=r,
                af=jnp.exp(lf * (r + 1.0)), bf=jnp.exp(lf * (t - 1.0 - r)), gf=jnp.exp(lf * t),
                ab=jnp.exp(lb * (t - r)), bb=jnp.exp(lb * r), gb=jnp.exp(lb * t))


def _scaled(x, f):
    return (x.astype(F32) * f).astype(BF16)


def _retc_specs(s):
    qspec = pl.BlockSpec((s, RET_QK_DIM), lambda h: (0, h))
    vspec = pl.BlockSpec((s, RET_V_DIM), lambda h: (0, h))
    dspec = pl.BlockSpec((1, 1, LANES), lambda h: (h, 0, 0))
    return qspec, vspec, dspec


def _retc_fwd(q, k, v, dec_f, dec_b, *, t=256):
    s, qw = q.shape
    t = _tile(s, t)
    heads, nc = qw // RET_QK_DIM, s // t
    qspec, vspec, dspec = _retc_specs(s)

    def body(q_ref, k_ref, v_ref, df_ref, db_ref, o_ref):
        cs = _chunk_consts(df_ref, db_ref, t)

        def rows_of(i):
            return pl.ds(pl.multiple_of(i * t, t), t)

        def forward(i, sf):
            rows = rows_of(i)
            qi, ki, vi = q_ref[rows, :], k_ref[rows, :], v_ref[rows, :]
            sc = lax.dot_general(qi, ki, NT_DIMS, preferred_element_type=F32)
            intra = jnp.dot((sc * cs["dm"]).astype(BF16), vi, preferred_element_type=F32)
            cross = jnp.dot(_scaled(qi, cs["af"]), sf.astype(BF16), preferred_element_type=F32)
            o_ref[rows, :] = intra + cross
            return cs["gf"] * sf + lax.dot_general(_scaled(ki, cs["bf"]), vi, TN_DIMS, preferred_element_type=F32)

        def backward(j, sb):
            rows = rows_of(nc - 1 - j)
            qi, ki, vi = q_ref[rows, :], k_ref[rows, :], v_ref[rows, :]
            o_ref[rows, :] += jnp.dot(_scaled(qi, cs["ab"]), sb.astype(BF16), preferred_element_type=F32)
            return cs["gb"] * sb + lax.dot_general(_scaled(ki, cs["bb"]), vi, TN_DIMS, preferred_element_type=F32)

        zero = jnp.zeros((RET_QK_DIM, RET_V_DIM), F32)
        lax.fori_loop(0, nc, forward, zero, unroll=True)
        lax.fori_loop(0, nc, backward, zero, unroll=True)

    return pl.pallas_call(
        body, name="ret_fwd", grid=(heads,),
        in_specs=[qspec, qspec, vspec, dspec, dspec],
        out_specs=vspec, out_shape=jax.ShapeDtypeStruct(v.shape, F32),
        compiler_params=_params(("parallel",)),
    )(q, k, v, dec_f, dec_b)


def _retc_bwd(q, k, v, do, dec_f, dec_b, *, t=256):
    s, qw = q.shape
    t = _tile(s, t)
    heads, nc = qw // RET_QK_DIM, s // t
    qspec, vspec, dspec = _retc_specs(s)
    gspec = pl.BlockSpec((1, 8, LANES), lambda h: (h, 0, 0))

    def body(q_ref, k_ref, v_ref, do_ref, df_ref, db_ref, dq_ref, dk_ref, dv_ref, gf_ref, gb_ref,
             sf_scr, sb_scr, dv_acc):
        cs = _chunk_consts(df_ref, db_ref, t)
        r, diff, dm = cs["r"], cs["diff"], cs["dm"]

        def rows_of(i):
            return pl.ds(pl.multiple_of(i * t, t), t)

        def tn(a, b):
            return lax.dot_general(a, b, TN_DIMS, preferred_element_type=F32)

        def nt(a, b):
            return lax.dot_general(a, b, NT_DIMS, preferred_element_type=F32)

        def states_f(i, sf):
            sf_scr[i] = sf
            rows = rows_of(i)
            return cs["gf"] * sf + tn(_scaled(k_ref[rows, :], cs["bf"]), v_ref[rows, :])

        def states_b(j, sb):
            i = nc - 1 - j
            sb_scr[i] = sb
            rows = rows_of(i)
            return cs["gb"] * sb + tn(_scaled(k_ref[rows, :], cs["bb"]), v_ref[rows, :])

        zero = jnp.zeros((RET_QK_DIM, RET_V_DIM), F32)
        lax.fori_loop(0, nc, states_f, zero, unroll=True)
        lax.fori_loop(0, nc, states_b, zero, unroll=True)

        def scan_grads(i, state, u, qf, kf, vi, doi, fa, fb, step, wa, wb):
            qa, kb = qf * fa, kf * fb
            ub = u.astype(BF16)
            dqa = nt(doi, state.astype(BF16))
            dkb = nt(vi, ub)
            dv = jnp.dot(kb.astype(BF16), ub, preferred_element_type=F32)
            dlog = _sum_all(dqa * qa * wa) + _sum_all(dkb * kb * wb) + t * step * _sum_all(u * state)
            u_new = step * u + tn(qa.astype(BF16), doi)
            return dqa * fa, dkb * fb, dv, u_new, dlog

        def sweep_f(j, carry):
            u, accf, accb = carry
            i = nc - 1 - j
            rows = rows_of(i)
            qi, ki, vi, doi = q_ref[rows, :], k_ref[rows, :], v_ref[rows, :], do_ref[rows, :]
            sc = nt(qi, ki)
            p = sc * dm
            dp = nt(doi, vi)
            ds = (dp * dm).astype(BF16)
            tt = dp * p * diff
            accf = accf + _sum_all(jnp.where(diff > 0, tt, 0.0))
            accb = accb + _sum_all(jnp.where(diff < 0, -tt, 0.0))
            dq1, dk1, dv1, u, dlog = scan_grads(i, sf_scr[i], u, qi.astype(F32), ki.astype(F32), vi, doi,
                                                cs["af"], cs["bf"], cs["gf"], r + 1.0, t - 1.0 - r)
            dq_ref[rows, :] = jnp.dot(ds, ki, preferred_element_type=F32) + dq1
            dk_ref[rows, :] = tn(ds, qi) + dk1
            dv_acc[rows, :] = tn(p.astype(BF16), doi) + dv1
            return u, accf + dlog, accb

        def sweep_b(i, carry):
            w, accb = carry
            rows = rows_of(i)
            qi, ki, vi, doi = q_ref[rows, :], k_ref[rows, :], v_ref[rows, :], do_ref[rows, :]
            dq1, dk1, dv1, w, dlog = scan_grads(i, sb_scr[i], w, qi.astype(F32), ki.astype(F32), vi, doi,
                                                cs["ab"], cs["bb"], cs["gb"], t - r, r)
            dq_ref[rows, :] += dq1
            dk_ref[rows, :] += dk1
            dv_acc[rows, :] += dv1
            return w, accb + dlog

        z11 = jnp.zeros((1, 1), F32)
        _, accf, accb = lax.fori_loop(0, nc, sweep_f, (zero, z11, z11), unroll=4)
        _, accb = lax.fori_loop(0, nc, sweep_b, (zero, accb), unroll=4)
        dv_ref[...] = dv_acc[...].astype(dv_ref.dtype)
        gf_ref[...] = jnp.broadcast_to((accf / (1.0 + jnp.exp(df_ref[0][:, :1]))).reshape(1, 1, 1), gf_ref.shape)
        gb_ref[...] = jnp.broadcast_to((accb / (1.0 + jnp.exp(db_ref[0][:, :1]))).reshape(1, 1, 1), gb_ref.shape)

    return pl.pallas_call(
        body, name="ret_bwd", grid=(heads,),
        in_specs=[qspec, qspec, vspec, vspec, dspec, dspec],
        out_specs=[qspec, qspec, vspec, gspec, gspec],
        out_shape=[jax.ShapeDtypeStruct(q.shape, F32), jax.ShapeDtypeStruct(k.shape, F32),
                   jax.ShapeDtypeStruct(v.shape, BF16),
                   jax.ShapeDtypeStruct((heads, 8, LANES), F32), jax.ShapeDtypeStruct((heads, 8, LANES), F32)],
        scratch_shapes=[pltpu.VMEM((nc, RET_QK_DIM, RET_V_DIM), F32), pltpu.VMEM((nc, RET_QK_DIM, RET_V_DIM), F32),
                        pltpu.VMEM((s, RET_V_DIM), F32)],
        compiler_params=_params(("parallel",)),
    )(q, k, v, do, dec_f, dec_b)


def _gate_fwd(att, ag, ret, rg, rnw, *, ts=256):
    s, aw = att.shape
    rw = ret.shape[1]
    ts = _tile(s, ts)
    rheads = rw // RET_V_DIM

    def body(att_ref, ag_ref, ret_ref, rg_ref, w_ref, y_ref, yt_ref):
        def put(lo, hi, val):
            y_ref[:, lo:hi] = val.astype(BF16)
            yt_ref[lo:hi, :] = val.T.astype(BF16)

        sa, _ = _silu_parts(ag_ref[...])
        put(0, aw, sa * att_ref[...].astype(F32))
        for h in range(rheads):
            sl = slice(h * RET_V_DIM, (h + 1) * RET_V_DIM)
            rt = ret_ref[:, sl]
            rn = rt * lax.rsqrt(jnp.mean(rt * rt, axis=-1, keepdims=True) + EPS) * w_ref[:, sl]
            sr, _ = _silu_parts(rg_ref[:, sl])
            put(aw + h * RET_V_DIM, aw + (h + 1) * RET_V_DIM, sr * rn)

    def row(w):
        return pl.BlockSpec((ts, w), lambda i: (i, 0))

    return pl.pallas_call(
        body, name="gate_fwd", grid=(s // ts,),
        in_specs=[row(aw), row(aw), row(rw), row(rw), pl.BlockSpec((1, rw), lambda i: (0, 0))],
        out_specs=[row(aw + rw), pl.BlockSpec((aw + rw, ts), lambda i: (0, i))],
        out_shape=[jax.ShapeDtypeStruct((s, aw + rw), BF16), jax.ShapeDtypeStruct((aw + rw, s), BF16)],
        compiler_params=_params(("parallel",)),
    )(att, ag, ret, rg, rnw.reshape(1, rw))


def _gate_bwd(dy, att, ag, ret, rg, rnw, *, ts=256):
    s, aw = att.shape
    rw = ret.shape[1]
    ts = _tile(s, ts)
    rheads = rw // RET_V_DIM

    def body(dy_ref, att_ref, ag_ref, ret_ref, rg_ref, w_ref, datt_ref, dag_ref, dret_ref, drg_ref, dw_ref):
        sa, dsa = _silu_parts(ag_ref[...])
        dya = dy_ref[:, :aw]
        datt_ref[...] = (dya * sa).astype(BF16)
        dag_ref[...] = (dya * att_ref[...].astype(F32) * dsa).astype(BF16)
        parts = []
        for h in range(rheads):
            sl = slice(h * RET_V_DIM, (h + 1) * RET_V_DIM)
            rt = ret_ref[:, sl]
            rr = lax.rsqrt(jnp.mean(rt * rt, axis=-1, keepdims=True) + EPS)
            rh = rt * rr
            wv = w_ref[:, sl]
            sr, dsr = _silu_parts(rg_ref[:, sl])
            dyr = dy_ref[:, aw + h * RET_V_DIM:aw + (h + 1) * RET_V_DIM]
            drg_ref[:, sl] = (dyr * rh * wv * dsr).astype(BF16)
            drn = dyr * sr
            dn = drn * wv
            dret_ref[:, sl] = (rr * (dn - rh * jnp.mean(dn * rh, axis=-1, keepdims=True))).astype(BF16)
            parts.append(jnp.sum(drn * rh, axis=0, keepdims=True))
        part = jnp.concatenate(parts, axis=-1)

        @pl.when(pl.program_id(0) == 0)
        def _():
            dw_ref[...] = part

        @pl.when(pl.program_id(0) > 0)
        def _():
            dw_ref[...] += part

    def row(w):
        return pl.BlockSpec((ts, w), lambda i: (i, 0))

    vec = pl.BlockSpec((1, rw), lambda i: (0, 0))
    return pl.pallas_call(
        body, name="gate_bwd", grid=(s // ts,),
        in_specs=[row(aw + rw), row(aw), row(aw), row(rw), row(rw), vec],
        out_specs=[row(aw), row(aw), row(rw), row(rw), vec],
        out_shape=[jax.ShapeDtypeStruct((s, aw), BF16), jax.ShapeDtypeStruct((s, aw), BF16),
                   jax.ShapeDtypeStruct((s, rw), BF16), jax.ShapeDtypeStruct((s, rw), BF16),
                   jax.ShapeDtypeStruct((1, rw), F32)],
        compiler_params=_params(("arbitrary",)),
    )(dy, att, ag, ret, rg, rnw.reshape(1, rw))


def _mesh_position():
    x, y, c = lax.axis_index("x"), lax.axis_index("y"), lax.axis_index("c")
    return x, y, c, 4 * x + 2 * y + c


def _peer(x, y, c, k):
    px = 1 - x if k & 4 else x
    py = 1 - y if k & 2 else y
    pc = 1 - c if k & 1 else c
    return (px, py, pc), 4 * px + 2 * py + pc


HBM_SPEC = pl.BlockSpec(memory_space=pltpu.HBM)
SEM_SPEC = pl.BlockSpec(memory_space=pltpu.SEMAPHORE)
ANY_SPEC = pl.BlockSpec(memory_space=pl.ANY)
DATAFLOW = pltpu.SideEffectType.DATAFLOW_SIDE_EFFECTING


def _hbm(a):
    return pltpu.with_memory_space_constraint(a, pltpu.HBM)


def _split_start(name, copies, n, src, land, after):
    def body(*refs):
        (send_sems, recv_sems), token = refs[2 + len(after):4 + len(after)], refs[-1]
        sends, _ = copies(refs[0], refs[1], send_sems, recv_sems)
        for cp in sends:
            cp.start()
        token[...] = jnp.zeros_like(token)

    return pl.pallas_call(
        body, name=name,
        out_shape=(pltpu.SemaphoreType.DMA((n,)), pltpu.SemaphoreType.DMA((n,)),
                   pltpu.HBM(src.shape, src.dtype), pltpu.HBM(land.shape, land.dtype),
                   jax.ShapeDtypeStruct((8, LANES), F32)),
        in_specs=[HBM_SPEC] * 2 + [ANY_SPEC] * len(after),
        out_specs=(SEM_SPEC, SEM_SPEC, HBM_SPEC, HBM_SPEC, pl.BlockSpec(memory_space=pltpu.VMEM)),
        input_output_aliases={0: 2, 1: 3},
        compiler_params=pltpu.CompilerParams(has_side_effects=DATAFLOW),
    )(_hbm(src), _hbm(land), *after)


def _split_wait(name, copies, started, after):
    send_sems, recv_sems, src, land = started[:4]

    def body(*refs):
        sends, recvs = copies(refs[0], refs[1], refs[2], refs[3])
        for cp in sends:
            cp.wait_send()
        for cp in recvs:
            cp.wait_recv()

    return pl.pallas_call(
        body, name=name,
        out_shape=(pltpu.HBM(src.shape, src.dtype), pltpu.HBM(land.shape, land.dtype)),
        in_specs=[HBM_SPEC] * 2 + [SEM_SPEC, SEM_SPEC] + [ANY_SPEC] * len(after),
        out_specs=(HBM_SPEC,) * 2,
        input_output_aliases={0: 0, 1: 1},
        compiler_params=pltpu.CompilerParams(has_side_effects=DATAFLOW),
    )(src, land, send_sems, recv_sems, *after)


def _slab(ref, p, size, axis):
    if axis == 1:
        return ref.at[:, pl.ds(pl.multiple_of(p * size, LANES), size)]
    return ref.at[pl.ds(pl.multiple_of(p * size, 16), size), :]


ALL_PEERS = tuple(range(1, N_DEV))
SIBLING = 1
SAME_CORE_OF_CHIPS = (2, 4, 6)


def _gather_copies(size, axis, ks):
    def copies(shard_ref, full_ref, send_sems, recv_sems):
        x, y, c, me = _mesh_position()
        sends, recvs = [], []
        for j, k in enumerate(ks):
            peer, pid = _peer(x, y, c, k)
            sends.append(pltpu.make_async_remote_copy(
                src_ref=shard_ref, dst_ref=_slab(full_ref, me, size, axis), send_sem=send_sems.at[j],
                recv_sem=recv_sems.at[j], device_id=peer, device_id_type=MESH))
            recvs.append(pltpu.make_async_remote_copy(
                src_ref=shard_ref, dst_ref=_slab(full_ref, pid, size, axis), send_sem=send_sems.at[j],
                recv_sem=recv_sems.at[j], device_id=peer, device_id_type=MESH))
        return sends, recvs

    return copies


def _pass_on_copies(size, axis):
    def copies(shard_ref, full_ref, send_sems, recv_sems):
        x, y, c, _ = _mesh_position()
        sibling, _ = _peer(x, y, c, SIBLING)
        sends, recvs = [], []
        for j, k in enumerate(SAME_CORE_OF_CHIPS):
            _, landed = _peer(x, y, c, k)
            _, siblings = _peer(x, y, c, k ^ SIBLING)
            mine = _slab(full_ref, landed, size, axis)
            sends.append(pltpu.make_async_remote_copy(
                src_ref=mine, dst_ref=mine, send_sem=send_sems.at[j], recv_sem=recv_sems.at[j],
                device_id=sibling, device_id_type=MESH))
            recvs.append(pltpu.make_async_remote_copy(
                src_ref=mine, dst_ref=_slab(full_ref, siblings, size, axis), send_sem=send_sems.at[j],
                recv_sem=recv_sems.at[j], device_id=sibling, device_id_type=MESH))
        return sends, recvs

    return copies


def _scatter_copies(size, axis):
    def copies(grad_ref, land_ref, send_sems, recv_sems):
        x, y, c, me = _mesh_position()
        sends, recvs = [], []
        for k in range(1, N_DEV):
            peer, pid = _peer(x, y, c, k)
            src = _slab(grad_ref, pid, size, axis)
            sends.append(pltpu.make_async_remote_copy(
                src_ref=src, dst_ref=land_ref.at[me], send_sem=send_sems.at[k - 1], recv_sem=recv_sems.at[k - 1],
                device_id=peer, device_id_type=MESH))
            recvs.append(pltpu.make_async_remote_copy(
                src_ref=src, dst_ref=land_ref.at[pid], send_sem=send_sems.at[k - 1], recv_sem=recv_sems.at[k - 1],
                device_id=peer, device_id_type=MESH))
        return sends, recvs

    return copies


PLACE_BANDS = 8


def _place_own(name, src, out_shape, in_spec, out_spec, steps, me):
    def body(me_ref, src_ref, out_ref):
        out_ref[...] = src_ref[...]

    return pl.pallas_call(
        body, name=name, out_shape=out_shape,
        grid_spec=pltpu.PrefetchScalarGridSpec(num_scalar_prefetch=1, grid=(steps,), in_specs=[in_spec],
                                               out_specs=out_spec),
        compiler_params=_params(("parallel",)),
    )(me.reshape(1).astype(jnp.int32), src)


def _cast_place(w_all, layer, axis, me):
    _, rows, cols = w_all.shape
    full_shape = tuple(N_DEV * n if a == axis else n for a, n in enumerate((rows, cols)))
    band = rows // PLACE_BANDS
    if axis == 1:
        full_spec = pl.BlockSpec((band, cols), lambda i, me_ref: (i, me_ref[0]))
    else:
        full_spec = pl.BlockSpec((band, cols), lambda i, me_ref: (me_ref[0] * PLACE_BANDS + i, 0))

    def body(me_ref, w_ref, shard_ref, full_ref):
        shard_ref[...] = w_ref[...].astype(BF16)
        full_ref[...] = w_ref[...].astype(BF16)

    return pl.pallas_call(
        body, name="cast_place",
        out_shape=[jax.ShapeDtypeStruct((rows, cols), BF16), jax.ShapeDtypeStruct(full_shape, BF16)],
        grid_spec=pltpu.PrefetchScalarGridSpec(
            num_scalar_prefetch=1, grid=(PLACE_BANDS,),
            in_specs=[pl.BlockSpec((None, band, cols), lambda i, me_ref: (layer, i, 0))],
            out_specs=[pl.BlockSpec((band, cols), lambda i, me_ref: (i, 0)), full_spec]),
        compiler_params=_params(("parallel",)),
    )(me.reshape(1).astype(jnp.int32), w_all)


def _gather_start(placed, axis, ks, after, tag):
    shard, full = placed
    return _split_start("gather_start_" + tag, _gather_copies(shard.shape[axis], axis, ks), len(ks), shard, full, after)


def _gather_wait(started, axis, ks, after, tag):
    size = started[2].shape[axis]
    return _split_wait("gather_wait_" + tag, _gather_copies(size, axis, ks), started, after)


def _pass_on_start(shard, full, axis, after, tag):
    size = shard.shape[axis]
    return _split_start("pass_on_start_" + tag, _pass_on_copies(size, axis), len(SAME_CORE_OF_CHIPS), shard, full, after)


def _pass_on_wait(started, axis, after, tag):
    size = started[2].shape[axis]
    return _split_wait("pass_on_wait_" + tag, _pass_on_copies(size, axis), started, after)[1]


def _scatter_start(grad, axis, me, tag):
    size = grad.shape[axis] // N_DEV
    rows, cols = tuple(size if a == axis else n for a, n in enumerate(grad.shape))
    band = rows // PLACE_BANDS
    if axis == 1:
        in_spec = pl.BlockSpec((band, cols), lambda i, me_ref: (i, me_ref[0]))
    else:
        in_spec = pl.BlockSpec((band, cols), lambda i, me_ref: (me_ref[0] * PLACE_BANDS + i, 0))
    out_spec = pl.BlockSpec((None, band, cols), lambda i, me_ref: (me_ref[0], i, 0))
    land = _place_own("place_slab", grad, jax.ShapeDtypeStruct((N_DEV, rows, cols), grad.dtype), in_spec, out_spec,
                      PLACE_BANDS, me)
    return _split_start("scatter_start_" + tag, _scatter_copies(size, axis), N_DEV - 1, grad, land, [])


def _scatter_wait(started, axis, after, tag):
    size = started[2].shape[axis] // N_DEV
    land = _split_wait("scatter_wait_" + tag, _scatter_copies(size, axis), started, after)[1]
    return [(land, p) for p in range(N_DEV)]


N_CHIPS = N_DEV // 2


def _pair_copies(size, axis):
    def copies(grad_ref, land_ref, send_sems, recv_sems):
        x, y, c, _ = _mesh_position()
        sibling, _ = _peer(x, y, c, SIBLING)
        sends, recvs = [], []
        for j in range(N_CHIPS):
            _, owner = _peer(x, y, c, (2 * j) ^ SIBLING)
            for lst in (sends, recvs):
                lst.append(pltpu.make_async_remote_copy(
                    src_ref=_slab(grad_ref, owner, size, axis), dst_ref=land_ref.at[j], send_sem=send_sems.at[j],
                    recv_sem=recv_sems.at[j], device_id=sibling, device_id_type=MESH))
        return sends, recvs

    return copies


def _chips_copies():
    def copies(pair_ref, land_ref, send_sems, recv_sems):
        x, y, c, _ = _mesh_position()
        sends, recvs = [], []
        for j in range(1, N_CHIPS):
            owner, _ = _peer(x, y, c, 2 * j)
            for lst in (sends, recvs):
                lst.append(pltpu.make_async_remote_copy(
                    src_ref=pair_ref.at[j], dst_ref=land_ref.at[j], send_sem=send_sems.at[j - 1],
                    recv_sem=recv_sems.at[j - 1], device_id=owner, device_id_type=MESH))
        return sends, recvs

    return copies


def _pair_start(grad, axis, tag):
    size = grad.shape[axis] // N_DEV
    rows, cols = tuple(size if a == axis else n for a, n in enumerate(grad.shape))
    land = lax.empty((N_CHIPS, rows, cols), grad.dtype)
    return _split_start("pair_start_" + tag, _pair_copies(size, axis), N_CHIPS, grad, land, [])


def _pair_sums(started, axis, me, after, tag):
    size = started[2].shape[axis] // N_DEV
    grad, land = _split_wait("pair_wait_" + tag, _pair_copies(size, axis), started, after)
    _, rows, cols = land.shape
    band = rows // PLACE_BANDS
    if axis == 1:
        mine = pl.BlockSpec((band, cols), lambda j, i, me_ref: (i, me_ref[0] ^ (2 * j)))
    else:
        mine = pl.BlockSpec((band, cols), lambda j, i, me_ref: ((me_ref[0] ^ (2 * j)) * PLACE_BANDS + i, 0))
    slot = pl.BlockSpec((None, band, cols), lambda j, i, me_ref: (j, i, 0))

    def body(own_ref, mine_ref, theirs_ref, out_ref):
        out_ref[...] = (mine_ref[...].astype(F32) + theirs_ref[...].astype(F32)).astype(out_ref.dtype)

    return pl.pallas_call(
        body, name="pair_sums", out_shape=jax.ShapeDtypeStruct(land.shape, land.dtype),
        grid_spec=pltpu.PrefetchScalarGridSpec(num_scalar_prefetch=1, grid=(N_CHIPS, PLACE_BANDS),
                                               in_specs=[mine, slot], out_specs=slot),
        compiler_params=_params(("parallel", "parallel")),
    )(me.reshape(1).astype(jnp.int32), grad, land)


def _chips_start(pairs, tag):
    return _split_start("chips_start_" + tag, _chips_copies(), N_CHIPS - 1, pairs, lax.empty(pairs.shape, pairs.dtype), [])


def _chips_wait(started, after, tag):
    pairs, land = _split_wait("chips_wait_" + tag, _chips_copies(), started, after)
    return [(pairs, 0)] + [(land, j) for j in range(1, N_CHIPS)]


def _exchange_small(buf, *, name, after=()):
    r = buf.shape[0]

    def body(*refs):
        buf_ref = refs[0]
        all_ref, sum_ref, send_sems, recv_sems = refs[1 + len(after):]
        x, y, c, me = _mesh_position()
        all_ref[me] = buf_ref[...]
        sends, recvs = [], []
        for k in range(1, N_DEV):
            peer, pid = _peer(x, y, c, k)
            sends.append(pltpu.make_async_remote_copy(
                src_ref=buf_ref, dst_ref=all_ref.at[me], send_sem=send_sems.at[k - 1], recv_sem=recv_sems.at[k - 1],
                device_id=peer, device_id_type=MESH))
            recvs.append(pltpu.make_async_remote_copy(
                src_ref=buf_ref, dst_ref=all_ref.at[pid], send_sem=send_sems.at[k - 1], recv_sem=recv_sems.at[k - 1],
                device_id=peer, device_id_type=MESH))
        for cp in sends:
            cp.start()
        for cp in recvs:
            cp.wait_recv()
        for cp in sends:
            cp.wait_send()
        total = all_ref[0]
        for p in range(1, N_DEV):
            total = total + all_ref[p]
        sum_ref[...] = total

    vmem = pl.BlockSpec(memory_space=pltpu.VMEM)
    return pl.pallas_call(
        body, name=name,
        in_specs=[vmem] + [ANY_SPEC] * len(after), out_specs=[vmem, vmem],
        out_shape=[jax.ShapeDtypeStruct((N_DEV, r, LANES), F32), jax.ShapeDtypeStruct((r, LANES), F32)],
        scratch_shapes=[pltpu.SemaphoreType.DMA((N_DEV - 1,)), pltpu.SemaphoreType.DMA((N_DEV - 1,))],
        compiler_params=pltpu.CompilerParams(has_side_effects=True),
    )(buf, *after)


def _adamw_math(w, g, m, v):
    m2 = ADAM_B1 * m + (1.0 - ADAM_B1) * g
    v2 = ADAM_B2 * v + (1.0 - ADAM_B2) * (g * g)
    delta = -ADAM_LR * ((m2 / ADAM_C1) / (jnp.sqrt(v2 / ADAM_C2) + ADAM_EPS) + ADAM_WD * w)
    return delta, m2, v2


def _adamw_slabs(layer, w, m, v, addends, outs, order, *, tr, name):
    depth, r, c = w.shape
    tr = _tile(r, tr)
    n = len(addends)

    def body(*refs):
        w_ref, m_ref, v_ref = refs[:3]
        g_ref, d_ref, m2_ref, v2_ref = refs[-4:]
        g = refs[3][...].astype(F32)
        for a_ref in refs[4:3 + n]:
            g = g + a_ref[...].astype(F32)
        delta, m2, v2 = _adamw_math(w_ref[...], g, m_ref[...], v_ref[...])
        g_ref[...] = g
        d_ref[...] = delta
        m2_ref[...] = m2
        v2_ref[...] = v2

    row = pl.BlockSpec((None, tr, c), lambda i: (layer, i, 0))
    slots = [pl.BlockSpec((None, tr, c), lambda i, p=p: (p, i, 0)) for _, p in addends]
    first_out = 3 + n + 1
    return pl.pallas_call(
        body, name=name, grid=(r // tr,),
        in_specs=[row, row, row] + slots + [pl.BlockSpec((8, LANES), lambda i: (0, 0))] + [ANY_SPEC] * 4,
        out_specs=[row] * 4, out_shape=[jax.ShapeDtypeStruct((depth, r, c), F32)] * 4,
        input_output_aliases={first_out + t: t for t in range(4)},
        compiler_params=_params(("parallel",)),
    )(w, m, v, *[a for a, _ in addends], order, *outs)


def _adamw_small(w, g, m, v):
    def body(w_ref, g_ref, m_ref, v_ref, d_ref, m2_ref, v2_ref):
        delta, m2, v2 = _adamw_math(w_ref[...], g_ref[...], m_ref[...], v_ref[...])
        d_ref[...] = delta
        m2_ref[...] = m2
        v2_ref[...] = v2

    vmem = pl.BlockSpec(memory_space=pltpu.VMEM)
    return pl.pallas_call(
        body, name="adamw_small", in_specs=[vmem] * 4, out_specs=[vmem] * 3,
        out_shape=[jax.ShapeDtypeStruct(w.shape, F32)] * 3,
    )(w, g, m, v)


def _pack(parts):
    flat = jnp.concatenate([p.reshape(-1).astype(F32) for p in parts])
    rows = -(-flat.shape[0] // LANES)
    rows = -(-rows // SMALL_ROWS_ALIGN) * SMALL_ROWS_ALIGN
    flat = jnp.pad(flat, (0, rows * LANES - flat.shape[0]))
    return flat.reshape(rows, LANES)


def _unpack(buf, shapes):
    flat = buf.reshape(-1)
    out, pos = [], 0
    for shp in shapes:
        size = math.prod(shp)
        out.append(flat[pos:pos + size].reshape(shp))
        pos += size
    return out


def _section_widths(d):
    aw = d // 2
    kw = aw // ATTN_GROUP
    rw = d - aw
    rqw = (rw // RET_V_DIM) * RET_QK_DIM
    return (aw, kw, kw, aw, rqw, rqw, rw, rw)


def _layer_fwd(xl, hh, win_full, behind, after_attn, wout_of, qn, kn, dec_f, dec_b, rn, cos, sin):
    h, ht = hh
    aq, ak, v, ag, rq, rk, rvb, rg = _proj_sections(h, win_full, _section_widths(xl.shape[1]),
                                                    (F32, F32, BF16, F32, F32, F32, BF16, F32), after=behind)
    q, k, rqr, rkr = _prep_fwd(aq, ak, rq, rk, cos, sin, qn, kn)
    att, lse = _attn_fwd(q, k, v)
    ret = _retc_fwd(rqr, rkr, rvb, dec_f + after_attn(att), dec_b)
    y, yt = _gate_fwd(att, ag, ret, rg, rn)
    wout_full = wout_of(y)
    xn = _matmul(y, wout_full, name="out_proj", residual=xl, tn=1024)
    saved = dict(x=xl, ht=ht, aq=aq, ak=ak, ag=ag, rg=rg, q=q, k=k, v=v, rq=rqr, rk=rkr, rv=rvb,
                 att=att, lse=lse, ret=ret, yt=yt, win=win_full, wout=wout_full)
    return xn, saved


def _layer_bwd_weights(gb, sv, qn, kn, dec_f, dec_b, rn, cos, sin, on_dwout):
    dy = _matmul(gb, sv["wout"], name="d_y", trans_b=True, tn=1024)
    dwout = _matmul(sv["yt"], gb, name="d_wout", out_dtype=BF16, tn=1024)
    datt, dag, dret, drg, drn = _gate_bwd(dy, sv["att"], sv["ag"], sv["ret"], sv["rg"], rn + on_dwout(dwout))
    dq, dk, dav = _attn_bwd(sv["q"], sv["k"], sv["v"], sv["att"], datt, sv["lse"])
    drq, drk, drv, gf, gbk = _retc_bwd(sv["rq"], sv["rk"], sv["rv"], dret, dec_f, dec_b)
    dproj, dqn, dkn = _prep_bwd(dq, dk, drq, drk, sv["aq"], sv["ak"], cos, sin, qn, kn, dav, dag, drv, drg)
    dwin = _matmul(sv["ht"], dproj, name="d_win", out_dtype=BF16, tn=1024)
    small = dict(qn=dqn[0], kn=dkn[0], df=gf[:, 0, 0], db=gbk[:, 0, 0], rn=drn[0])
    return dproj, dwin, small


def _layer_bwd_input(g, dproj, sv, nw, behind, after_dh):
    dh = _matmul(dproj, sv["win"], name="d_h", trans_b=True, tm=512, tk=dproj.shape[1], after=behind)
    g, gb, dnw = _rms_bwd(dh, sv["x"], g, nw + after_dh(dh))
    return g, gb, dnw[0]


def kernel(x, norm_w, w_in, q_norm, k_norm, ret_decay_fwd, ret_decay_bwd, ret_norm, w_out, final_norm, loss_target, m_norm_w, m_w_in, m_q_norm, m_k_norm, m_ret_decay_fwd, m_ret_decay_bwd, m_ret_norm, m_w_out, m_final_norm, v_norm_w, v_w_in, v_q_norm, v_k_norm, v_ret_decay_fwd, v_ret_decay_bwd, v_ret_norm, v_w_out, v_final_norm):
    depth, d, _ = w_in.shape
    seq = x.shape[1]
    rw = _section_widths(d)[6]
    rheads = rw // RET_V_DIM
    rns = ret_norm.shape[-1]
    _, _, _, me = _mesh_position()

    target = loss_target[0]
    cos, sin = _rope_tables(seq)

    rn_all, _ = _exchange_small(_pack([ret_norm]), name="gather_ret_norm")
    rn_full = rn_all.reshape(N_DEV, -1)[:, :depth * rheads * rns].reshape(N_DEV, depth, rheads, rns)
    rn_full = jnp.transpose(rn_full, (1, 2, 0, 3)).reshape(depth, rw)

    dec_f = jnp.broadcast_to(ret_decay_fwd[:, :, None, None], (depth, rheads, 1, LANES))
    dec_b = jnp.broadcast_to(ret_decay_bwd[:, :, None, None], (depth, rheads, 1, LANES))


    saved = []
    xl = x[0]
    first = (SIBLING,) + SAME_CORE_OF_CHIPS
    in_sent = _gather_start(_cast_place(w_in, 0, 1, me), 1, first, [], "in0")
    hh = _rms_fwd(xl, norm_w[0] + in_sent[-1][0, 0])
    placed_out = _cast_place(w_out, 0, 0, me)
    placed_in = _cast_place(w_in, 1, 1, me) if depth > 1 else None
    under_way = [hh[0], placed_out[0], cos, sin] + ([placed_in[0]] if depth > 1 else [])
    shard, landed = _gather_wait(in_sent, 1, first, under_way, "in0")
    win_full = _pass_on_wait(_pass_on_start(shard, landed, 1, [], "in0"), 1, [], "in0")
    for l in range(depth):
        if l > 0:
            hh = _rms_fwd(xl, norm_w[l])
            placed_out = _cast_place(w_out, l, 0, me)
            placed_in = _cast_place(w_in, l + 1, 1, me) if l + 1 < depth else None
        out_sent = _gather_start(placed_out, 0, ALL_PEERS, [win_full], "out" + str(l))
        behind = [out_sent[-1]]
        passed = {}
        if l + 1 < depth:
            in_sent = _gather_start(placed_in, 1, first, [win_full, out_sent[-1]], "in" + str(l + 1))
            behind.append(in_sent[-1])

        def after_attn(att, passed=passed, l=l):
            if l + 1 == depth:
                return 0.0
            shard, landed = _gather_wait(in_sent, 1, first, [att], "in" + str(l + 1))
            passed["on"] = _pass_on_start(shard, landed, 1, [], "in" + str(l + 1))
            return passed["on"][-1][0, 0]

        def wout_of(y, out_sent=out_sent, l=l):
            return _gather_wait(out_sent, 0, ALL_PEERS, [y], "out" + str(l))[1]

        xl, sv = _layer_fwd(xl, hh, win_full, behind, after_attn, wout_of, q_norm[l], k_norm[l], dec_f[l], dec_b[l],
                            rn_full[l], cos, sin)
        saved.append(sv)
        if l + 1 < depth:
            win_full = _pass_on_wait(passed["on"], 1, [xl], "in" + str(l + 1))

    loss_row, g, gb, d_final = _loss_head(xl, target, final_norm)

    d_norm, d_qn, d_kn, d_df, d_db, d_rn = [], [], [], [], [], []
    lands = [None] * depth
    pending = None
    for l in reversed(range(depth)):
        sent = {}

        def on_dwout(dwout, sent=sent, l=l):
            sent["out"] = _scatter_start(dwout, 0, me, "out" + str(l))
            return sent["out"][-1][0, 0]

        def after_dh(dh, sent=sent, l=l):
            pairs = _pair_sums(sent["pair"], 1, me, [dh], "in" + str(l))
            sent["in"] = _chips_start(pairs, "in" + str(l))
            return sent["in"][-1][0, 0]

        dproj, dwin, sm = _layer_bwd_weights(gb, saved[l], q_norm[l], k_norm[l], dec_f[l], dec_b[l],
                                             rn_full[l], cos, sin, on_dwout)
        sent["pair"] = _pair_start(dwin, 1, "in" + str(l))
        g, gb, dnw = _layer_bwd_input(g, dproj, saved[l], norm_w[l], [sent["pair"][-1]], after_dh)
        if pending is not None:
            lands[l + 1] = (_chips_wait(pending["in"], [g], "in" + str(l + 1)),
                            _scatter_wait(pending["out"], 0, [g], "out" + str(l + 1)))
        pending = sent
        d_norm.append(dnw)
        d_qn.append(sm["qn"])
        d_kn.append(sm["kn"])
        d_df.append(sm["df"])
        d_db.append(sm["db"])
        d_rn.append(sm["rn"])
    for lst in (d_norm, d_qn, d_kn, d_df, d_db, d_rn):
        lst.reverse()
    order = pending["in"][-1]
    in_outs = [lax.empty(w_in.shape, F32) for _ in range(4)]
    out_outs = [lax.empty(w_out.shape, F32) for _ in range(4)]
    for l in reversed(range(1, depth)):
        in_outs = _adamw_slabs(l, w_in, m_w_in, v_w_in, lands[l][0], in_outs, order, tr=256, name="adamw_w_in")
        out_outs = _adamw_slabs(l, w_out, m_w_out, v_w_out, lands[l][1], out_outs, order, tr=64, name="adamw_w_out")
    land_out = _scatter_wait(pending["out"], 0, [g, out_outs[0]], "out0")
    out_outs = _adamw_slabs(0, w_out, m_w_out, v_w_out, land_out, out_outs, order, tr=64, name="adamw_w_out")

    small_shapes = [(depth, d), (depth, HEAD_DIM), (depth, HEAD_DIM), (depth, rheads), (depth, rheads),
                    (depth, rheads, N_DEV * rns), (d,), (1,)]
    grads_local = [jnp.stack(d_norm), jnp.stack(d_qn), jnp.stack(d_kn), jnp.stack(d_df), jnp.stack(d_db),
                   jnp.stack(d_rn).reshape(depth, rheads, N_DEV * rns), d_final[0], loss_row[0, :1]]
    _, gsum = _exchange_small(_pack(grads_local), name="all_reduce_small", after=(in_outs[0], out_outs[0]))
    land_in = _chips_wait(pending["in"], [g, gsum], "in0")
    in_outs = _adamw_slabs(0, w_in, m_w_in, v_w_in, land_in, in_outs, order, tr=256, name="adamw_w_in")
    g_norm, g_qn, g_kn, g_df, g_db, g_rn_full, g_final, loss = _unpack(gsum, small_shapes)
    g_rn = lax.dynamic_slice_in_dim(g_rn_full, me * rns, rns, axis=2)
    small_g = [g_norm, g_qn, g_kn, g_df, g_db, g_rn, g_final]
    small_w = [norm_w, q_norm, k_norm, ret_decay_fwd, ret_decay_bwd, ret_norm, final_norm]
    small_m = [m_norm_w, m_q_norm, m_k_norm, m_ret_decay_fwd, m_ret_decay_bwd, m_ret_norm, m_final_norm]
    small_v = [v_norm_w, v_q_norm, v_k_norm, v_ret_decay_fwd, v_ret_decay_bwd, v_ret_norm, v_final_norm]
    shapes = [a.shape for a in small_w]
    sd, sm, sv2 = _adamw_small(_pack(small_w), _pack(small_g), _pack(small_m), _pack(small_v))
    small_d, small_m2, small_v2 = _unpack(sd, shapes), _unpack(sm, shapes), _unpack(sv2, shapes)

    def ordered(small, win_v, wout_v):
        return [small[0], win_v, small[1], small[2], small[3], small[4], small[5], wout_v, small[6]]

    grads = ordered(small_g, in_outs[0], out_outs[0])
    deltas = ordered(small_d, in_outs[1], out_outs[1])
    new_m = ordered(small_m2, in_outs[2], out_outs[2])
    new_v = ordered(small_v2, in_outs[3], out_outs[3])
    return (loss.reshape(()), g[None], *grads, *deltas, *new_m, *new_v)
```
